```python
import jax, jax.numpy as jnp
from jax import lax
import numpy as np

D_MODEL = 1024
BATCH = 4
SEQ = 4096
DEPTH = 1

ATTN_HEAD_DIM = 64
ATTN_WIDTH = D_MODEL // 2
ATTN_HEADS = ATTN_WIDTH // ATTN_HEAD_DIM
GLA_HEADS = 4
GLA_VAL_WIDTH = D_MODEL // 2
GLA_KEY_WIDTH = GLA_VAL_WIDTH // 2
GLA_DK = GLA_KEY_WIDTH // GLA_HEADS
GLA_DV = GLA_VAL_WIDTH // GLA_HEADS
GLA_GATE_RANK = 16
GLA_TAU = 16.0
GLA_CHUNK = 64
MIX_WIDTH = ATTN_WIDTH + GLA_VAL_WIDTH
IN_SPLITS = (ATTN_WIDTH, ATTN_WIDTH, ATTN_WIDTH,
             GLA_KEY_WIDTH, GLA_KEY_WIDTH, GLA_VAL_WIDTH, GLA_VAL_WIDTH, 2 * GLA_GATE_RANK)
IN_WIDTH = sum(IN_SPLITS)

DILATED_BRANCHES = ((128, 1), (512, 4), (2048, 16))
HALF_SPAN = 64
T5_BUCKETS = 32
T5_MAX_DISTANCE = 1024
NEG_INF = -1e30

N_EXPERTS = 32
TOP_K = 4
D_FF = D_MODEL
SWIGLU_LIMIT = 7.0
SWIGLU_ALPHA = 1.702
MOE_BLOCK = 128

NORM_EPS = 1e-5

kernel_name = "hybrid_dilated_attn_gla_moe_encoder"


def _rmsnorm(x, g):
    xf = x.astype(jnp.float32)
    y = xf * lax.rsqrt(jnp.mean(xf * xf, axis=-1, keepdims=True) + NORM_EPS)
    return (y * g.astype(jnp.float32)).astype(x.dtype)


def _t5_bucket(rel):
    nb = T5_BUCKETS // 2
    max_exact = nb // 2
    n = np.abs(rel)
    large = max_exact + (np.log(np.maximum(n, 1).astype(np.float32) / max_exact)
                         / np.log(T5_MAX_DISTANCE / max_exact) * (nb - max_exact)).astype(np.int32)
    large = np.minimum(large, nb - 1)
    return (np.where(rel > 0, nb, 0) + np.where(n < max_exact, n, large)).astype(np.int32)


def _dilated_branch(q, k, v, rel_bias, dil):
    B, S, H, Dh = q.shape
    L = S // dil
    blk = HALF_SPAN
    nb = -(-L // blk)
    Lp = nb * blk
    N = B * dil

    def to_res(t):
        return t.reshape(B, L, dil, H, Dh).transpose(0, 2, 3, 1, 4).reshape(N, H, L, Dh)

    def windows(t):
        tp = jnp.pad(to_res(t), ((0, 0), (0, 0), (blk, Lp - L + blk), (0, 0))).reshape(N, H, nb + 2, blk, Dh)
        return jnp.concatenate([tp[:, :, :-2], tp[:, :, 1:-1], tp[:, :, 2:]], axis=3)

    qb = jnp.pad(to_res(q), ((0, 0), (0, 0), (0, Lp - L), (0, 0))).reshape(N, H, nb, blk, Dh)
    kw, vw = windows(k), windows(v)

    s_idx = np.arange(blk)[:, None]
    t_idx = np.arange(3 * blk)[None, :]
    off = t_idx - blk - s_idx
    bias = jnp.transpose(rel_bias[_t5_bucket(off * dil)], (2, 0, 1)).astype(jnp.float32)
    ki = np.arange(nb)[:, None, None] * blk + t_idx - blk
    valid = (ki >= 0) & (ki < L) & (np.abs(off) <= HALF_SPAN)

    logits = jnp.einsum('nhbqd,nhbkd->nhbqk', qb, kw).astype(jnp.float32) * (Dh ** -0.5)
    logits = jnp.where(valid[None, None], logits + bias[None, :, None], NEG_INF)
    m = jnp.max(logits, axis=-1, keepdims=True)
    p = jnp.exp(logits - m)
    den = jnp.sum(p, axis=-1)
    out = jnp.einsum('nhbqk,nhbkd->nhbqd', p, vw.astype(jnp.float32)) / den[..., None]
    lse = m[..., 0] + jnp.log(den)
    out = out.reshape(N, H, Lp, Dh)[:, :, :L]
    lse = lse.reshape(N, H, Lp)[:, :, :L]
    out = out.reshape(B, dil, H, L, Dh).transpose(0, 3, 1, 2, 4).reshape(B, S, H, Dh)
    lse = lse.reshape(B, dil, H, L).transpose(0, 3, 1, 2).reshape(B, S, H)
    return out, lse


def _dilated_attention(q, k, v, rel_bias):
    B, S = q.shape[:2]
    res = [_dilated_branch(q, k, v, rel_bias, d) for (_, d) in DILATED_BRANCHES]
    outs = jnp.stack([r[0] for r in res])
    lses = jnp.stack([r[1] for r in res])
    w = jax.nn.softmax(lses, axis=0)
    out = jnp.sum(w[..., None] * outs, axis=0)
    return out.reshape(B, S, ATTN_WIDTH).astype(q.dtype)


def _gla_chunked(q, k, v, log_a):
    B, H, S, dk = q.shape
    dv = v.shape[-1]
    C = GLA_CHUNK
    nc = S // C
    q, k, log_a = (t.reshape(B, H, nc, C, dk) for t in (q, k, log_a))
    v = v.reshape(B, H, nc, C, dv)
    b = jnp.cumsum(log_a, axis=3)
    G = b[:, :, :, -1]
    qd = q * jnp.exp(b)
    kd = k * jnp.exp(-b)
    tri = np.tril(np.ones((C, C), dtype=bool))
    A = jnp.where(tri, jnp.einsum('bhncd,bhnsd->bhncs', qd, kd), 0.0)
    o_intra = jnp.einsum('bhncs,bhnse->bhnce', A, v)
    kst = k * jnp.exp(G[:, :, :, None, :] - b)
    dS = jnp.einsum('bhncd,bhnce->bhnde', kst, v)

    def step(state, inp):
        dS_n, decay_n = inp
        return decay_n[..., None] * state + dS_n, state

    _, S_prev = lax.scan(step, jnp.zeros((B, H, dk, dv), jnp.float32),
                         (dS.transpose(2, 0, 1, 3, 4), jnp.exp(G).transpose(2, 0, 1, 3)))
    S_prev = S_prev.transpose(1, 2, 0, 3, 4)
    o_inter = jnp.einsum('bhncd,bhnde->bhnce', qd, S_prev)
    return (o_intra + o_inter).reshape(B, H, S, dv)


def _bi_gla(q, k, v, out_gate, z_low, wg_f, bg_f, wg_b, bg_b, norm_g):
    B, S = q.shape[:2]
    f32 = jnp.float32

    def heads(t, d):
        return t.astype(f32).reshape(B, S, GLA_HEADS, d).transpose(0, 2, 1, 3)

    qh = heads(q, GLA_DK) * (GLA_DK ** -0.5)
    kh = heads(k, GLA_DK)
    vh = heads(v, GLA_DV)
    z = z_low.astype(f32)
    la_f = heads(jax.nn.log_sigmoid(z[..., :GLA_GATE_RANK] @ wg_f.astype(f32) + bg_f.astype(f32)) / GLA_TAU, GLA_DK)
    la_b = heads(jax.nn.log_sigmoid(z[..., GLA_GATE_RANK:] @ wg_b.astype(f32) + bg_b.astype(f32)) / GLA_TAU, GLA_DK)
    fwd = _gla_chunked(qh, kh, vh, la_f)
    flip = lambda t: jnp.flip(t, axis=2)
    bwd = flip(_gla_chunked(flip(qh), flip(kh), flip(vh), flip(la_b)))
    o = fwd + bwd
    o = o * lax.rsqrt(jnp.mean(o * o, axis=-1, keepdims=True) + NORM_EPS)
    o = o.transpose(0, 2, 1, 3).reshape(B, S, GLA_VAL_WIDTH) * norm_g.astype(f32)
    return (o * jax.nn.silu(out_gate.astype(f32))).astype(q.dtype)


def _moe(h, router_w, router_b, w1, b1, w2, b2):
    B, S, D = h.shape
    T = B * S
    xf = h.reshape(T, D)
    logits = (xf @ router_w + router_b).astype(jnp.float32)
    topv, topi = lax.top_k(logits, TOP_K)
    gates = jax.nn.softmax(topv, axis=-1)
    flat_e = topi.reshape(-1).astype(jnp.int32)
    flat_tok = jnp.broadcast_to(jnp.arange(T, dtype=jnp.int32)[:, None], (T, TOP_K)).reshape(-1)
    flat_g = gates.reshape(-1)
    order = jnp.argsort(flat_e)
    se = flat_e[order]
    counts = jnp.bincount(flat_e, length=N_EXPERTS)
    starts = jnp.cumsum(counts) - counts
    pcounts = (counts + MOE_BLOCK - 1) // MOE_BLOCK * MOE_BLOCK
    pends = jnp.cumsum(pcounts)
    pstarts = pends - pcounts
    dest = (pstarts[se] + jnp.arange(T * TOP_K, dtype=jnp.int32) - starts[se]).astype(jnp.int32)
    P = T * TOP_K + N_EXPERTS * MOE_BLOCK
    nblk = P // MOE_BLOCK
    buf_tok = jnp.zeros((P,), jnp.int32).at[dest].set(flat_tok[order])
    buf_gate = jnp.zeros((P,), jnp.float32).at[dest].set(flat_g[order])
    block_e = jnp.clip(jnp.searchsorted(pends, jnp.arange(nblk) * MOE_BLOCK, side='right'),
                       0, N_EXPERTS - 1).astype(jnp.int32)
    xin = xf[buf_tok].reshape(nblk, MOE_BLOCK, D)

    def expert_block(args):
        xb, e = args
        hh = xb @ w1[e] + b1[e]
        x_glu = jnp.minimum(hh[..., ::2], SWIGLU_LIMIT)
        x_lin = jnp.clip(hh[..., 1::2], -SWIGLU_LIMIT, SWIGLU_LIMIT)
        act = x_glu * jax.nn.sigmoid(SWIGLU_ALPHA * x_glu) * (x_lin + 1.0)
        return act @ w2[e] + b2[e]

    y = lax.map(expert_block, (xin, block_e)).reshape(P, D)
    out = jnp.zeros((T, D), jnp.float32).at[buf_tok].add(y.astype(jnp.float32) * buf_gate[:, None])
    return out.reshape(B, S, D).astype(h.dtype)


def setup_inputs(seed: int = 0) -> dict:
    key = jax.random.key(seed)
    ks = jax.random.split(key, 20)
    f32 = jnp.float32
    nrm = lambda k, shape, s: jax.random.normal(k, shape, f32) * s
    return {
        "x": nrm(ks[0], (BATCH, SEQ, D_MODEL), 1.0),
        "norm1_g": 1.0 + nrm(ks[1], (DEPTH, D_MODEL), 0.02),
        "w_in": nrm(ks[2], (DEPTH, D_MODEL, IN_WIDTH), D_MODEL ** -0.5),
        "rel_bias": nrm(ks[3], (T5_BUCKETS, ATTN_HEADS), 0.5),
        "gla_wg_fwd": nrm(ks[4], (DEPTH, GLA_GATE_RANK, GLA_KEY_WIDTH), GLA_GATE_RANK ** -0.5),
        "gla_bg_fwd": nrm(ks[5], (DEPTH, GLA_KEY_WIDTH), 0.1),
        "gla_wg_bwd": nrm(ks[6], (DEPTH, GLA_GATE_RANK, GLA_KEY_WIDTH), GLA_GATE_RANK ** -0.5),
        "gla_bg_bwd": nrm(ks[7], (DEPTH, GLA_KEY_WIDTH), 0.1),
        "gla_norm_g": 1.0 + nrm(ks[8], (DEPTH, GLA_VAL_WIDTH), 0.02),
        "w_out": nrm(ks[9], (DEPTH, MIX_WIDTH, D_MODEL), MIX_WIDTH ** -0.5),
        "norm2_g": 1.0 + nrm(ks[10], (DEPTH, D_MODEL), 0.02),
        "router_w": nrm(ks[11], (DEPTH, D_MODEL, N_EXPERTS), D_MODEL ** -0.5),
        "router_b": nrm(ks[12], (DEPTH, N_EXPERTS), 0.01),
        "moe_w1": nrm(ks[13], (DEPTH, N_EXPERTS, D_MODEL, 2 * D_FF), D_MODEL ** -0.5),
        "moe_b1": nrm(ks[14], (DEPTH, N_EXPERTS, 2 * D_FF), 0.01),
        "moe_w2": nrm(ks[15], (DEPTH, N_EXPERTS, D_FF, D_MODEL), D_FF ** -0.5),
        "moe_b2": nrm(ks[16], (DEPTH, N_EXPERTS, D_MODEL), 0.01),
        "final_g": 1.0 + nrm(ks[17], (D_MODEL,), 0.02),
    }


def reference(x, norm1_g, w_in, rel_bias, gla_wg_fwd, gla_bg_fwd, gla_wg_bwd, gla_bg_bwd,
              gla_norm_g, w_out, norm2_g, router_w, router_b, moe_w1, moe_b1, moe_w2, moe_b2,
              final_g):
    B, S, _ = x.shape
    split_pts = np.cumsum(IN_SPLITS)[:-1].tolist()
    for l in range(DEPTH):
        n = _rmsnorm(x, norm1_g[l])
        proj = n @ w_in[l]
        aq, ak, av, gq, gk, gv, gout, gz = jnp.split(proj, split_pts, axis=-1)
        hsplit = lambda t: t.reshape(B, S, ATTN_HEADS, ATTN_HEAD_DIM)
        a = _dilated_attention(hsplit(aq), hsplit(ak), hsplit(av), rel_bias)
        g = _bi_gla(gq, gk, gv, gout, gz, gla_wg_fwd[l], gla_bg_fwd[l],
                    gla_wg_bwd[l], gla_bg_bwd[l], gla_norm_g[l])
        x = x + jnp.concatenate([a, g], axis=-1) @ w_out[l]
        x = x + _moe(_rmsnorm(x, norm2_g[l]), router_w[l], router_b[l],
                     moe_w1[l], moe_b1[l], moe_w2[l], moe_b2[l])
    return _rmsnorm(x, final_g)
```

```python
import functools

import jax
import jax.numpy as jnp
import numpy as np
from jax import lax
from jax.experimental import pallas as pl
from jax.experimental.pallas import tpu as pltpu

F32, BF16, I32 = jnp.float32, jnp.bfloat16, jnp.int32

LANES = 128
HEAD_DIM = 64
ATTN_HEADS = 8
ATTN_W = ATTN_HEADS * HEAD_DIM
HALF_SPAN = 64
DILATIONS = (1, 4, 16)
T5_BUCKETS = 32
T5_MAX_DISTANCE = 1024
NEG_INF = -1e30
GLA_KEY_W = 256
GLA_VAL_W = 512
GLA_DV = 128
GLA_RANK = 16
GLA_TAU = 16.0
GLA_CHUNK = 64
N_EXPERTS = 32
TOP_K = 4
SWIGLU_LIMIT = 7.0
SWIGLU_ALPHA = 1.702
NORM_EPS = 1e-5
MOE_BLOCK = 256
ROW_TILES = 8

ATTN_COLS = 3 * ATTN_W
GLA_COLS = 2 * GLA_KEY_W + 2 * GLA_VAL_W
Z_COLS = LANES
PROJ_COLS = ATTN_COLS + GLA_COLS + Z_COLS

NT_DIMS = (((1,), (1,)), ((), ()))


def _sigmoid(x):
    return 1.0 / (1.0 + jnp.exp(-x))


def _split3(a):
    h1 = a.astype(BF16)
    r1 = a - h1.astype(F32)
    h2 = r1.astype(BF16)
    h3 = (r1 - h2.astype(F32)).astype(BF16)
    return h1, h2, h3


def _dot_f32(a, b):
    ah, al, _ = _split3(a)
    bh, bl, _ = _split3(b)
    d = lambda u, v: jnp.dot(u, v, preferred_element_type=F32)
    return d(ah, bh) + d(al, bh) + d(ah, bl)


def _in_proj_body(x_ref, g_ref, w_ref, wg_ref, bg_ref, attn_ref, gla_ref, la_ref):
    x = x_ref[...]
    n = x * lax.rsqrt(jnp.mean(x * x, axis=-1, keepdims=True) + NORM_EPS) * g_ref[...]
    p = jnp.dot(n.astype(BF16), w_ref[...], preferred_element_type=F32)
    attn_ref[...] = p[:, :ATTN_COLS].astype(BF16)
    gla_ref[...] = p[:, ATTN_COLS:ATTN_COLS + GLA_COLS].astype(BF16)
    z = p[:, ATTN_COLS + GLA_COLS:]
    zz = _dot_f32(z, wg_ref[...]) + bg_ref[...]
    log_sig = jnp.minimum(zz, 0.0) - jnp.log1p(jnp.exp(-jnp.abs(zz)))
    la_ref[...] = log_sig * (1.0 / GLA_TAU)


def _in_proj(x2, norm_g, w_in, wg_f, bg_f, wg_b, bg_b):
    T, D = x2.shape
    tm = 512
    w = jnp.pad(w_in.astype(BF16), ((0, 0), (0, PROJ_COLS - w_in.shape[1])))
    wg = jnp.zeros((Z_COLS, 2 * GLA_KEY_W), F32)
    wg = wg.at[:GLA_RANK, :GLA_KEY_W].set(wg_f).at[GLA_RANK:2 * GLA_RANK, GLA_KEY_W:].set(wg_b)
    bg = jnp.concatenate([bg_f, bg_b]).reshape(1, 2 * GLA_KEY_W)
    row = lambda c: pl.BlockSpec((tm, c), lambda i: (i, 0))
    full = lambda a: pl.BlockSpec(a.shape, lambda i: (0,) * a.ndim)
    g2 = norm_g.reshape(1, D)
    return pl.pallas_call(
        _in_proj_body,
        grid=(T // tm,),
        in_specs=[row(D), full(g2), full(w), full(wg), full(bg)],
        out_specs=[row(ATTN_COLS), row(GLA_COLS), row(2 * GLA_KEY_W)],
        out_shape=[jax.ShapeDtypeStruct((T, ATTN_COLS), BF16),
                   jax.ShapeDtypeStruct((T, GLA_COLS), BF16),
                   jax.ShapeDtypeStruct((T, 2 * GLA_KEY_W), F32)],
        compiler_params=pltpu.CompilerParams(dimension_semantics=("arbitrary",),
                                             vmem_limit_bytes=48 * 2**20),
        name="in_proj",
    )(x2, g2, w, wg, bg)


def _t5_bucket(rel):
    nb = T5_BUCKETS // 2
    max_exact = nb // 2
    n = np.abs(rel)
    large = max_exact + (np.log(np.maximum(n, 1).astype(np.float32) / max_exact)
                         / np.log(T5_MAX_DISTANCE / max_exact) * (nb - max_exact)).astype(np.int32)
    large = np.minimum(large, nb - 1)
    return (np.where(rel > 0, nb, 0) + np.where(n < max_exact, n, large)).astype(np.int32)


def _bucket_tables(dil):
    s = np.arange(HALF_SPAN)[:, None]
    t = np.arange(3 * HALF_SPAN)[None, :]
    tabs = []
    for c in (0, HALF_SPAN, 2 * HALF_SPAN):
        off = t - c - s
        tabs.append(np.where(np.abs(off) <= HALF_SPAN, _t5_bucket(off * dil), -1))
    return np.stack(tabs).astype(np.int32)


def _attn_body(rb_ref, idx_ref, q_ref, k_ref, v_ref, o_ref, lse_ref, tab_ref, *, L, TQ):
    blk = HALF_SPAN
    first = (pl.program_id(0) == 0) & (pl.program_id(1) == 0) & (pl.program_id(2) == 0)

    @pl.when(first)
    def _():
        for v in range(3):
            idx = idx_ref[v]
            for h in range(ATTN_HEADS):
                t = jnp.where(idx < 0, NEG_INF, 0.0).astype(F32)
                for j in range(T5_BUCKETS):
                    t = jnp.where(idx == j, rb_ref[j, h], t)
                tab_ref[v, h] = t

    i = pl.program_id(2)
    lo = lax.broadcasted_iota(I32, (blk, LANES), 1) < HEAD_DIM
    scale = HEAD_DIM ** -0.5
    for sb in range(TQ // blk):
        l0 = i * TQ + sb * blk
        ws = pl.multiple_of(jnp.clip(l0 - blk, 0, L - 3 * blk), blk)
        var = jnp.where(l0 == 0, 0, jnp.where(l0 == L - blk, 2, 1))
        rows = slice(sb * blk, (sb + 1) * blk)
        for hp in range(ATTN_HEADS // 2):
            cols = slice(hp * LANES, (hp + 1) * LANES)
            q2 = q_ref[0, rows, cols]
            k2 = k_ref[0, pl.ds(ws, 3 * blk), cols]
            v2 = v_ref[0, pl.ds(ws, 3 * blk), cols]
            outs, lses = [], []
            for hh in range(2):
                qm = jnp.where(lo if hh == 0 else ~lo, q2, jnp.zeros_like(q2))
                s = lax.dot_general(qm, k2, NT_DIMS, preferred_element_type=F32)
                logits = s * scale + tab_ref[var, 2 * hp + hh]
                m = jnp.max(logits, axis=-1, keepdims=True)
                p = jnp.exp(logits - m)
                den = jnp.sum(p, axis=-1, keepdims=True)
                pv = jnp.dot(p.astype(BF16), v2, preferred_element_type=F32)
                outs.append(pv / den)
                lses.append(jnp.broadcast_to(m + jnp.log(den), (blk, LANES)))
            o_ref[0, rows, cols] = jnp.where(lo, outs[0], outs[1]).astype(BF16)
            lse_ref[0, rows, cols] = jnp.where(lo, lses[0], lses[1])


def _attn_branch(attn, rel_bias, B, S, dil):
    L = S // dil
    TQ = 256
    a3 = attn.reshape(B, L, dil * ATTN_COLS)
    idx = jnp.asarray(_bucket_tables(dil))
    body = functools.partial(_attn_body, L=L, TQ=TQ)
    kv = lambda j: pl.BlockSpec((1, L, ATTN_W), lambda b, r, i: (b, 0, 3 * r + j))
    out_spec = pl.BlockSpec((1, TQ, ATTN_W), lambda b, r, i: (b, i, r))
    o, lse = pl.pallas_call(
        body,
        grid=(B, dil, L // TQ),
        in_specs=[pl.BlockSpec(memory_space=pltpu.SMEM),
                  pl.BlockSpec(idx.shape, lambda b, r, i: (0, 0, 0)),
                  pl.BlockSpec((1, TQ, ATTN_W), lambda b, r, i: (b, i, 3 * r)),
                  kv(1), kv(2)],
        out_specs=[out_spec, out_spec],
        out_shape=[jax.ShapeDtypeStruct((B, L, dil * ATTN_W), BF16),
                   jax.ShapeDtypeStruct((B, L, dil * ATTN_W), F32)],
        scratch_shapes=[pltpu.VMEM((3, ATTN_HEADS, HALF_SPAN, 3 * HALF_SPAN), F32)],
        compiler_params=pltpu.CompilerParams(dimension_semantics=("arbitrary",) * 3,
                                             vmem_limit_bytes=48 * 2**20),
        name=f"attn_d{dil}",
    )(rel_bias, idx, a3, a3, a3)
    return o.reshape(B * S, ATTN_W), lse.reshape(B * S, ATTN_W)


def _gla_body(q_ref, k_ref, v_ref, gate_ref, laf_ref, lab_ref, ng_ref, out_ref, of_ref, ob_ref, st_ref, *, S):
    C = GLA_CHUNK
    nc = S // C
    r = lax.broadcasted_iota(I32, (C, C), 0)
    c = lax.broadcasted_iota(I32, (C, C), 1)
    lo = lax.broadcasted_iota(I32, (C, LANES), 1) < HEAD_DIM
    st_ref[...] = jnp.zeros_like(st_ref)

    def one_chunk(ci, direction):
        mask = (r >= c) if direction == 0 else (r <= c)
        la_ref, o_ref = (laf_ref, of_ref) if direction == 0 else (lab_ref, ob_ref)
        r0 = pl.multiple_of(ci * C, C)
        rows = pl.ds(r0, C)
        q = q_ref[0, rows, :].astype(F32) * (HEAD_DIM ** -0.5)
        k = k_ref[0, rows, :].astype(F32)
        tri = mask.astype(BF16)
        l1, l2, l3 = _split3(la_ref[0, rows, :])
        cs = lambda h: jnp.dot(tri, h, preferred_element_type=F32)
        b = cs(l1) + cs(l2) + cs(l3)
        g_tot = b[C - 1:C, :] if direction == 0 else b[0:1, :]
        qd = q * jnp.exp(b)
        kd = (k * jnp.exp(-b)).astype(BF16)
        kst = (k * jnp.exp(g_tot - b)).astype(BF16)
        decay = jnp.exp(g_tot)
        for hh in range(2):
            qm = jnp.where(lo if hh == 0 else ~lo, qd, 0.0).astype(BF16)
            a = lax.dot_general(qm, kd, NT_DIMS, preferred_element_type=F32)
            a = jnp.where(mask, a, 0.0).astype(BF16)
            vh = v_ref[0, rows, hh * GLA_DV:(hh + 1) * GLA_DV]
            st = st_ref[direction, hh]
            o = (jnp.dot(a, vh, preferred_element_type=F32)
                 + lax.dot_general(qm, st.astype(BF16), NT_DIMS, preferred_element_type=F32))
            o_ref[rows, hh * GLA_DV:(hh + 1) * GLA_DV] = o
            d_st = jnp.dot(vh.astype(F32).T.astype(BF16), kst, preferred_element_type=F32)
            st_ref[direction, hh] = st * decay + d_st

    def step(i, carry):
        one_chunk(i, 0)
        one_chunk(nc - 1 - i, 1)
        return carry

    lax.fori_loop(0, nc, step, 0)

    tr = 256
    def epilogue(i, carry):
        rows = pl.ds(pl.multiple_of(i * tr, tr), tr)
        for hh in range(2):
            cols = slice(hh * GLA_DV, (hh + 1) * GLA_DV)
            o = of_ref[rows, cols] + ob_ref[rows, cols]
            o = o * lax.rsqrt(jnp.mean(o * o, axis=-1, keepdims=True) + NORM_EPS) * ng_ref[0, :, cols]
            gate = gate_ref[0, rows, cols].astype(F32)
            out_ref[0, rows, cols] = (o * (gate * _sigmoid(gate))).astype(BF16)
        return carry

    lax.fori_loop(0, S // tr, epilogue, 0)


def _gla(gla, la, norm_g, B, S):
    g3 = gla.reshape(B, S, GLA_COLS)
    la3 = la.reshape(B, S, 2 * GLA_KEY_W)
    ng = norm_g.reshape(2, 1, 2 * GLA_DV)
    body = functools.partial(_gla_body, S=S)
    spec = lambda w, off: pl.BlockSpec((1, S, w), lambda b, p: (b, 0, off + p))
    out = pl.pallas_call(
        body,
        grid=(B, 2),
        in_specs=[spec(LANES, 0),
                  spec(LANES, GLA_KEY_W // LANES),
                  spec(2 * GLA_DV, (2 * GLA_KEY_W) // (2 * GLA_DV)),
                  spec(2 * GLA_DV, (2 * GLA_KEY_W + GLA_VAL_W) // (2 * GLA_DV)),
                  spec(LANES, 0),
                  spec(LANES, GLA_KEY_W // LANES),
                  pl.BlockSpec((1, 1, 2 * GLA_DV), lambda b, p: (p, 0, 0))],
        out_specs=pl.BlockSpec((1, S, 2 * GLA_DV), lambda b, p: (b, 0, p)),
        out_shape=jax.ShapeDtypeStruct((B, S, GLA_VAL_W), BF16),
        scratch_shapes=[pltpu.VMEM((S, 2 * GLA_DV), F32), pltpu.VMEM((S, 2 * GLA_DV), F32),
                        pltpu.VMEM((2, 2, GLA_DV, LANES), F32)],
        compiler_params=pltpu.CompilerParams(dimension_semantics=("arbitrary", "arbitrary"),
                                             vmem_limit_bytes=56 * 2**20),
        name="gla",
    )(g3, g3, g3, g3, la3, la3, ng)
    return out.reshape(B * S, GLA_VAL_W)


def _mix_body(o1, o2, o3, l1, l2, l3, g_ref, x_ref, wo_ref, n2_ref, rwh_ref, rwl_ref, rb_ref,
              x1_ref, h_ref, e_ref, rk_ref, gt_ref, cnt_ref, base_ref, *, tm):
    @pl.when(pl.program_id(0) == 0)
    def _():
        base_ref[...] = jnp.zeros_like(base_ref)

    ls = [l1[...], l2[...], l3[...]]
    m = jnp.maximum(jnp.maximum(ls[0], ls[1]), ls[2])
    es = [jnp.exp(l - m) for l in ls]
    a = (es[0] * o1[...].astype(F32) + es[1] * o2[...].astype(F32) + es[2] * o3[...].astype(F32)) / (es[0] + es[1] + es[2])
    mix = (jnp.dot(a.astype(BF16), wo_ref[:ATTN_W, :], preferred_element_type=F32)
           + jnp.dot(g_ref[...], wo_ref[ATTN_W:, :], preferred_element_type=F32))
    x1 = x_ref[...] + mix
    x1_ref[...] = x1
    h = x1 * lax.rsqrt(jnp.mean(x1 * x1, axis=-1, keepdims=True) + NORM_EPS) * n2_ref[...]
    for j in range(ROW_TILES):
        h_ref[:, j, :] = h[:, j * LANES:(j + 1) * LANES]

    hh, hl, _ = _split3(h)
    dg = lambda u, v: lax.dot_general(u, v, NT_DIMS, preferred_element_type=F32)
    logits = dg(rwh_ref[...], hh) + dg(rwl_ref[...], hh) + dg(rwh_ref[...], hl) + rb_ref[...]
    ie = lax.broadcasted_iota(I32, (N_EXPERTS, tm), 0)
    cur = logits
    vals, idxs = [], []
    for _ in range(TOP_K):
        mx = jnp.max(cur, axis=0, keepdims=True)
        ix = jnp.min(jnp.where(cur == mx, ie, N_EXPERTS), axis=0, keepdims=True)
        vals.append(mx)
        idxs.append(ix)
        cur = jnp.where(ie == ix, -jnp.inf, cur)
    ex = [jnp.exp(v - vals[0]) for v in vals]
    den = ex[0] + ex[1] + ex[2] + ex[3]
    gates = [e / den for e in ex]

    onehots = [ie == ix for ix in idxs]
    chosen = (onehots[0] | onehots[1] | onehots[2] | onehots[3]).astype(F32)
    tr = lax.broadcasted_iota(I32, (tm, tm), 0)
    tc = lax.broadcasted_iota(I32, (tm, tm), 1)
    before = jnp.dot(chosen.astype(BF16), (tr < tc).astype(BF16), preferred_element_type=F32)
    tot = before + base_ref[:, 0:1]
    ranks = [jnp.sum(jnp.where(oh, tot, 0.0), axis=0, keepdims=True) for oh in onehots]
    new_base = base_ref[...] + jnp.sum(chosen, axis=1, keepdims=True)
    base_ref[...] = new_base
    cnt_ref[...] = new_base

    sub = lax.broadcasted_iota(I32, (8, tm), 0)
    pack = lambda rows, zero: functools.reduce(
        lambda acc, kr: jnp.where(sub == kr[0], kr[1], acc), enumerate(rows), jnp.full((8, tm), zero, rows[0].dtype))
    e_ref[...] = pack(idxs, 0)
    rk_ref[...] = pack([rk.astype(I32) for rk in ranks], 0)
    g8 = pack(gates, 0.0)
    gt_ref[...] = jnp.concatenate([g8, jnp.zeros((LANES - 8, tm), F32)], axis=0).T


def _mix(outs, lses, g, x2, w_out, norm2_g, router_w, router_b):
    T, D = x2.shape
    tm = 512
    wo = w_out.astype(BF16)
    rwt = router_w.T
    rwh = rwt.astype(BF16)
    rwl = (rwt - rwh.astype(F32)).astype(BF16)
    rb = router_b.reshape(N_EXPERTS, 1)
    n2 = norm2_g.reshape(1, D)
    row = lambda c: pl.BlockSpec((tm, c), lambda i: (i, 0))
    full = lambda a: pl.BlockSpec(a.shape, lambda i: (0,) * a.ndim)
    col = pl.BlockSpec((8, tm), lambda i: (0, i))
    return pl.pallas_call(
        functools.partial(_mix_body, tm=tm),
        grid=(T // tm,),
        in_specs=[row(ATTN_W)] * 6 + [row(GLA_VAL_W), row(D), full(wo), full(n2), full(rwh), full(rwl), full(rb)],
        out_specs=[row(D), pl.BlockSpec((tm, ROW_TILES, LANES), lambda i: (i, 0, 0)), col, col, row(LANES),
                   pl.BlockSpec((N_EXPERTS, LANES), lambda i: (0, 0))],
        out_shape=[jax.ShapeDtypeStruct((T, D), F32),
                   jax.ShapeDtypeStruct((T, ROW_TILES, LANES), F32),
                   jax.ShapeDtypeStruct((8, T), I32),
                   jax.ShapeDtypeStruct((8, T), I32),
                   jax.ShapeDtypeStruct((T, LANES), F32),
                   jax.ShapeDtypeStruct((N_EXPERTS, LANES), F32)],
        scratch_shapes=[pltpu.VMEM((N_EXPERTS, LANES), F32)],
        compiler_params=pltpu.CompilerParams(dimension_semantics=("arbitrary",),
                                             vmem_limit_bytes=48 * 2**20),
        name="mix_router",
    )(*outs, *lses, g, x2, wo, n2, rwh, rwl, rb)


def _plan_body(cnt_ref, e_ref, rk_ref, dest_ref, be_ref, nu_ref, *, nblk):
    shift = MOE_BLOCK.bit_length() - 1
    e = e_ref[...]
    dest = rk_ref[...]
    start = jnp.int32(0)
    for ex in range(N_EXPERTS):
        dest = dest + jnp.where(e == ex, start, 0)
        nb = (cnt_ref[ex] + (MOE_BLOCK - 1)) >> shift
        b0 = start >> shift

        def fill(j, carry, b0=b0, ex=ex):
            be_ref[b0 + j] = jnp.int32(ex)
            return carry

        lax.fori_loop(0, nb, fill, 0)
        start = start + (nb << shift)
    dest_ref[...] = dest
    used = start >> shift

    def tail(j, carry):
        be_ref[j] = jnp.int32(N_EXPERTS - 1)
        return carry

    lax.fori_loop(used, nblk, tail, 0)
    nu_ref[0] = used


def _plan(counts, e_t, rk_t, nblk):
    smem = pl.BlockSpec(memory_space=pltpu.SMEM)
    vmem = pl.BlockSpec(memory_space=pltpu.VMEM)
    return pl.pallas_call(
        functools.partial(_plan_body, nblk=nblk),
        in_specs=[smem, vmem, vmem],
        out_specs=[vmem, smem, smem],
        out_shape=[jax.ShapeDtypeStruct(e_t.shape, I32),
                   jax.ShapeDtypeStruct((nblk,), I32),
                   jax.ShapeDtypeStruct((1,), I32)],
        name="plan",
    )(counts, e_t, rk_t)


def _dispatch_body(dest_ref, h_ref, xin_in_ref, xin_ref, sem, *, tmd):
    del xin_in_ref
    base = pl.program_id(0) * tmd

    def issue(t, carry):
        for k in range(TOP_K):
            pltpu.make_async_copy(h_ref.at[pl.ds(base + t, 1)], xin_ref.at[pl.ds(dest_ref[k, t], 1)], sem).start()
        return carry

    lax.fori_loop(0, tmd, issue, 0, unroll=8)
    pltpu.make_async_copy(h_ref.at[pl.ds(0, TOP_K * tmd)], xin_ref.at[pl.ds(0, TOP_K * tmd)], sem).wait()


def _dispatch(dest, h, n_rows):
    T = h.shape[0]
    tmd = 1024
    xin0 = jnp.zeros((n_rows, ROW_TILES, LANES), F32)
    return pl.pallas_call(
        functools.partial(_dispatch_body, tmd=tmd),
        grid=(T // tmd,),
        in_specs=[pl.BlockSpec((8, tmd), lambda i: (0, i), memory_space=pltpu.SMEM),
                  pl.BlockSpec(memory_space=pl.ANY),
                  pl.BlockSpec(memory_space=pl.ANY)],
        out_specs=pl.BlockSpec(memory_space=pl.ANY),
        out_shape=jax.ShapeDtypeStruct(xin0.shape, F32),
        scratch_shapes=[pltpu.SemaphoreType.DMA(())],
        input_output_aliases={2: 0},
        compiler_params=pltpu.CompilerParams(dimension_semantics=("arbitrary",)),
        name="dispatch",
    )(dest, h, xin0)


def _moe_body(be_ref, nu_ref, x_ref, w1g_ref, w1l_ref, b1g_ref, b1l_ref, w2_ref, b2_ref, y_ref):
    del be_ref
    live = pl.program_id(0) < nu_ref[0]

    @pl.when(live)
    def _():
        x = jnp.concatenate([x_ref[:, c, :] for c in range(ROW_TILES)], axis=1).astype(BF16)
        hg = jnp.dot(x, w1g_ref[0], preferred_element_type=F32) + b1g_ref[0]
        hl = jnp.dot(x, w1l_ref[0], preferred_element_type=F32) + b1l_ref[0]
        xg = jnp.minimum(hg, SWIGLU_LIMIT)
        xl = jnp.clip(hl, -SWIGLU_LIMIT, SWIGLU_LIMIT)
        act = xg * _sigmoid(SWIGLU_ALPHA * xg) * (xl + 1.0)
        y = jnp.dot(act.astype(BF16), w2_ref[0], preferred_element_type=F32) + b2_ref[0]
        for c in range(ROW_TILES):
            y_ref[:, c, :] = y[:, c * LANES:(c + 1) * LANES]

    @pl.when(jnp.logical_not(live))
    def _():
        y_ref[...] = jnp.zeros_like(y_ref)


def _moe(block_e, n_used, xin, w1, b1, w2, b2):
    n_rows = xin.shape[0]
    D = w1.shape[1]
    F = w2.shape[1]
    nblk = n_rows // MOE_BLOCK
    w1g = w1[:, :, 0::2].astype(BF16)
    w1l = w1[:, :, 1::2].astype(BF16)
    b1g = b1[:, 0::2].reshape(N_EXPERTS, 1, F)
    b1l = b1[:, 1::2].reshape(N_EXPERTS, 1, F)
    w2b = w2.astype(BF16)
    b2r = b2.reshape(N_EXPERTS, 1, D)
    xspec = pl.BlockSpec((MOE_BLOCK, ROW_TILES, LANES), lambda j, be, nu: (j, 0, 0))
    wspec = lambda a: pl.BlockSpec((1,) + a.shape[1:], lambda j, be, nu: (be[j], 0, 0))
    return pl.pallas_call(
        _moe_body,
        grid_spec=pltpu.PrefetchScalarGridSpec(
            num_scalar_prefetch=2,
            grid=(nblk,),
            in_specs=[xspec, wspec(w1g), wspec(w1l), wspec(b1g), wspec(b1l), wspec(w2b), wspec(b2r)],
            out_specs=xspec),
        out_shape=jax.ShapeDtypeStruct(xin.shape, F32),
        compiler_params=pltpu.CompilerParams(dimension_semantics=("arbitrary",),
                                             vmem_limit_bytes=48 * 2**20),
        name="moe",
    )(block_e, n_used, xin, w1g, w1l, b1g, b1l, w2b, b2r)


def _combine_body(dest_ref, y_ref, x1_ref, gt_ref, fg_ref, out_ref, buf_ref, sem, *, tmc):
    def issue(t, carry):
        for k in range(TOP_K):
            pltpu.make_async_copy(y_ref.at[pl.ds(dest_ref[k, t], 1)], buf_ref.at[k, pl.ds(t, 1)], sem).start()
        return carry

    lax.fori_loop(0, tmc, issue, 0, unroll=8)
    for k in range(TOP_K):
        pltpu.make_async_copy(y_ref.at[pl.ds(0, tmc)], buf_ref.at[k], sem).wait()

    acc = x1_ref[...]
    g = gt_ref[...]
    for k in range(TOP_K):
        yk = jnp.concatenate([buf_ref[k, :, c, :] for c in range(ROW_TILES)], axis=1)
        acc = acc + yk * g[:, k:k + 1]
    out_ref[...] = acc * lax.rsqrt(jnp.mean(acc * acc, axis=-1, keepdims=True) + NORM_EPS) * fg_ref[...]


def _combine(dest, y, x1, gates, final_g):
    T, D = x1.shape
    tmc = 256
    fg = final_g.reshape(1, D)
    return pl.pallas_call(
        functools.partial(_combine_body, tmc=tmc),
        grid=(T // tmc,),
        in_specs=[pl.BlockSpec((8, tmc), lambda i: (0, i), memory_space=pltpu.SMEM),
                  pl.BlockSpec(memory_space=pl.ANY),
                  pl.BlockSpec((tmc, D), lambda i: (i, 0)),
                  pl.BlockSpec((tmc, LANES), lambda i: (i, 0)),
                  pl.BlockSpec((1, D), lambda i: (0, 0))],
        out_specs=pl.BlockSpec((tmc, D), lambda i: (i, 0)),
        out_shape=jax.ShapeDtypeStruct((T, D), F32),
        scratch_shapes=[pltpu.VMEM((TOP_K, tmc, ROW_TILES, LANES), F32), pltpu.SemaphoreType.DMA(())],
        compiler_params=pltpu.CompilerParams(dimension_semantics=("arbitrary",),
                                             vmem_limit_bytes=48 * 2**20),
        name="combine",
    )(dest, y, x1, gates, fg)


def _mixers(x2, B, S, norm1_g, w_in, rel_bias, wg_f, bg_f, wg_b, bg_b, gla_norm_g):
    attn, gla, la = _in_proj(x2, norm1_g, w_in, wg_f, bg_f, wg_b, bg_b)
    branches = [_attn_branch(attn, rel_bias, B, S, d) for d in DILATIONS]
    g = _gla(gla, la, gla_norm_g, B, S)
    return [o for o, _ in branches], [l for _, l in branches], g


def _moe_layer(h, e_t, rk_t, gates, cnt, x1, w1, b1, w2, b2, final_g):
    T = x1.shape[0]
    n_rows = T * TOP_K + N_EXPERTS * MOE_BLOCK
    counts = cnt[:, 0].astype(I32)
    dest, block_e, n_used = _plan(counts, e_t, rk_t, n_rows // MOE_BLOCK)
    xin = _dispatch(dest, h, n_rows)
    y = _moe(block_e, n_used, xin, w1, b1, w2, b2)
    return _combine(dest, y, x1, gates, final_g)


def kernel(x, norm1_g, w_in, rel_bias, gla_wg_fwd, gla_bg_fwd, gla_wg_bwd, gla_bg_bwd, gla_norm_g, w_out, norm2_g, router_w, router_b, moe_w1, moe_b1, moe_w2, moe_b2, final_g):
    B, S, D = x.shape
    assert w_in.shape[0] == 1, "one layer"
    x2 = x.reshape(B * S, D)
    outs, lses, g = _mixers(x2, B, S, norm1_g[0], w_in[0], rel_bias, gla_wg_fwd[0], gla_bg_fwd[0],
                            gla_wg_bwd[0], gla_bg_bwd[0], gla_norm_g[0])
    x1, h, e_t, rk_t, gates, cnt = _mix(outs, lses, g, x2, w_out[0], norm2_g[0], router_w[0], router_b[0])
    out = _moe_layer(h, e_t, rk_t, gates, cnt, x1, moe_w1[0], moe_b1[0], moe_w2[0], moe_b2[0], final_g)
    return out.reshape(B, S, D)
```

```python
import functools

import jax
import jax.numpy as jnp
import numpy as np
from jax import lax
from jax.experimental import pallas as pl
from jax.experimental.pallas import tpu as pltpu

F32, BF16, I32 = jnp.float32, jnp.bfloat16, jnp.int32

LANES = 128
HEAD_DIM = 64
ATTN_HEADS = 8
ATTN_W = ATTN_HEADS * HEAD_DIM
HALF_SPAN = 64
DILATIONS = (1, 4, 16)
T5_BUCKETS = 32
T5_MAX_DISTANCE = 1024
NEG_INF = -1e30
GLA_KEY_W = 256
GLA_VAL_W = 512
GLA_DV = 128
GLA_RANK = 16
GLA_TAU = 16.0
GLA_CHUNK = 64
N_EXPERTS = 32
TOP_K = 4
SWIGLU_LIMIT = 7.0
SWIGLU_ALPHA = 1.702
NORM_EPS = 1e-5
MOE_BLOCK = 256
ROW_TILES = 8

ATTN_COLS = 3 * ATTN_W
GLA_COLS = 2 * GLA_KEY_W + 2 * GLA_VAL_W
Z_COLS = LANES
PROJ_COLS = ATTN_COLS + GLA_COLS + Z_COLS

NT_DIMS = (((1,), (1,)), ((), ()))


def _sigmoid(x):
    return 1.0 / (1.0 + jnp.exp(-x))


def _split3(a):
    h1 = a.astype(BF16)
    r1 = a - h1.astype(F32)
    h2 = r1.astype(BF16)
    h3 = (r1 - h2.astype(F32)).astype(BF16)
    return h1, h2, h3


def _dot_f32(a, b):
    ah, al, _ = _split3(a)
    bh, bl, _ = _split3(b)
    d = lambda u, v: jnp.dot(u, v, preferred_element_type=F32)
    return d(ah, bh) + d(al, bh) + d(ah, bl)


def _in_proj_body(x_ref, g_ref, w_ref, wg_ref, bg_ref, attn_ref, gla_ref, la_ref):
    x = x_ref[...]
    n = x * lax.rsqrt(jnp.mean(x * x, axis=-1, keepdims=True) + NORM_EPS) * g_ref[...]
    p = jnp.dot(n.astype(BF16), w_ref[...], preferred_element_type=F32)
    attn_ref[...] = p[:, :ATTN_COLS].astype(BF16)
    gla_ref[...] = p[:, ATTN_COLS:ATTN_COLS + GLA_COLS].astype(BF16)
    z = p[:, ATTN_COLS + GLA_COLS:]
    zz = _dot_f32(z, wg_ref[...]) + bg_ref[...]
    log_sig = jnp.minimum(zz, 0.0) - jnp.log1p(jnp.exp(-jnp.abs(zz)))
    la_ref[...] = log_sig * (1.0 / GLA_TAU)


def _in_proj(x2, norm_g, w_in, wg_f, bg_f, wg_b, bg_b):
    T, D = x2.shape
    tm = 512
    w = jnp.pad(w_in.astype(BF16), ((0, 0), (0, PROJ_COLS - w_in.shape[1])))
    wg = jnp.zeros((Z_COLS, 2 * GLA_KEY_W), F32)
    wg = wg.at[:GLA_RANK, :GLA_KEY_W].set(wg_f).at[GLA_RANK:2 * GLA_RANK, GLA_KEY_W:].set(wg_b)
    bg = jnp.concatenate([bg_f, bg_b]).reshape(1, 2 * GLA_KEY_W)
    row = lambda c: pl.BlockSpec((tm, c), lambda i: (i, 0))
    full = lambda a: pl.BlockSpec(a.shape, lambda i: (0,) * a.ndim)
    g2 = norm_g.reshape(1, D)
    return pl.pallas_call(
        _in_proj_body,
        grid=(T // tm,),
        in_specs=[row(D), full(g2), full(w), full(wg), full(bg)],
        out_specs=[row(ATTN_COLS), row(GLA_COLS), row(2 * GLA_KEY_W)],
        out_shape=[jax.ShapeDtypeStruct((T, ATTN_COLS), BF16),
                   jax.ShapeDtypeStruct((T, GLA_COLS), BF16),
                   jax.ShapeDtypeStruct((T, 2 * GLA_KEY_W), F32)],
        compiler_params=pltpu.CompilerParams(dimension_semantics=("arbitrary",),
                                             vmem_limit_bytes=48 * 2**20),
        name="in_proj",
    )(x2, g2, w, wg, bg)


def _t5_bucket(rel):
    nb = T5_BUCKETS // 2
    max_exact = nb // 2
    n = np.abs(rel)
    large = max_exact + (np.log(np.maximum(n, 1).astype(np.float32) / max_exact)
                         / np.log(T5_MAX_DISTANCE / max_exact) * (nb - max_exact)).astype(np.int32)
    large = np.minimum(large, nb - 1)
    return (np.where(rel > 0, nb, 0) + np.where(n < max_exact, n, large)).astype(np.int32)


def _bucket_tables(dil):
    s = np.arange(HALF_SPAN)[:, None]
    t = np.arange(3 * HALF_SPAN)[None, :]
    tabs = []
    for c in (0, HALF_SPAN, 2 * HALF_SPAN):
        off = t - c - s
        tabs.append(np.where(np.abs(off) <= HALF_SPAN, _t5_bucket(off * dil), -1))
    return np.stack(tabs).astype(np.int32)


def _attn_body(rb_ref, idx_ref, q_ref, k_ref, v_ref, o_ref, lse_ref, tab_ref, *, L, TQ):
    blk = HALF_SPAN
    first = (pl.program_id(0) == 0) & (pl.program_id(1) == 0) & (pl.program_id(2) == 0)

    @pl.when(first)
    def _():
        for v in range(3):
            idx = idx_ref[v]
            for h in range(ATTN_HEADS):
                t = jnp.where(idx < 0, NEG_INF, 0.0).astype(F32)
                for j in range(T5_BUCKETS):
                    t = jnp.where(idx == j, rb_ref[j, h], t)
                tab_ref[v, h] = t

    i = pl.program_id(2)
    lo = lax.broadcasted_iota(I32, (blk, LANES), 1) < HEAD_DIM
    scale = HEAD_DIM ** -0.5
    for sb in range(TQ // blk):
        l0 = i * TQ + sb * blk
        ws = pl.multiple_of(jnp.clip(l0 - blk, 0, L - 3 * blk), blk)
        var = jnp.where(l0 == 0, 0, jnp.where(l0 == L - blk, 2, 1))
        rows = slice(sb * blk, (sb + 1) * blk)
        for hp in range(ATTN_HEADS // 2):
            cols = slice(hp * LANES, (hp + 1) * LANES)
            q2 = q_ref[0, rows, cols]
            k2 = k_ref[0, pl.ds(ws, 3 * blk), cols]
            v2 = v_ref[0, pl.ds(ws, 3 * blk), cols]
            outs, lses = [], []
            for hh in range(2):
                qm = jnp.where(lo if hh == 0 else ~lo, q2, jnp.zeros_like(q2))
                s = lax.dot_general(qm, k2, NT_DIMS, preferred_element_type=F32)
                logits = s * scale + tab_ref[var, 2 * hp + hh]
                m = jnp.max(logits, axis=-1, keepdims=True)
                p = jnp.exp(logits - m)
                den = jnp.sum(p, axis=-1, keepdims=True)
                pv = jnp.dot(p.astype(BF16), v2, preferred_element_type=F32)
                outs.append(pv / den)
                lses.append(jnp.broadcast_to(m + jnp.log(den), (blk, LANES)))
            o_ref[0, rows, cols] = jnp.where(lo, outs[0], outs[1]).astype(BF16)
            lse_ref[0, rows, cols] = jnp.where(lo, lses[0], lses[1])


def _attn_branch(attn, rel_bias, B, S, dil):
    L = S // dil
    TQ = 256
    a3 = attn.reshape(B, L, dil * ATTN_COLS)
    idx = jnp.asarray(_bucket_tables(dil))
    body = functools.partial(_attn_body, L=L, TQ=TQ)
    kv = lambda j: pl.BlockSpec((1, L, ATTN_W), lambda b, r, i: (b, 0, 3 * r + j))
    out_spec = pl.BlockSpec((1, TQ, ATTN_W), lambda b, r, i: (b, i, r))
    o, lse = pl.pallas_call(
        body,
        grid=(B, dil, L // TQ),
        in_specs=[pl.BlockSpec(memory_space=pltpu.SMEM),
                  pl.BlockSpec(idx.shape, lambda b, r, i: (0, 0, 0)),
                  pl.BlockSpec((1, TQ, ATTN_W), lambda b, r, i: (b, i, 3 * r)),
                  kv(1), kv(2)],
        out_specs=[out_spec, out_spec],
        out_shape=[jax.ShapeDtypeStruct((B, L, dil * ATTN_W), BF16),
                   jax.ShapeDtypeStruct((B, L, dil * ATTN_W), F32)],
        scratch_shapes=[pltpu.VMEM((3, ATTN_HEADS, HALF_SPAN, 3 * HALF_SPAN), F32)],
        compiler_params=pltpu.CompilerParams(dimension_semantics=("arbitrary",) * 3,
                                             vmem_limit_bytes=48 * 2**20),
        name=f"attn_d{dil}",
    )(rel_bias, idx, a3, a3, a3)
    return o.reshape(B * S, ATTN_W), lse.reshape(B * S, ATTN_W)


def _gla_body(q_ref, k_ref, v_ref, gate_ref, laf_ref, lab_ref, ng_ref, out_ref, of_ref, ob_ref, st_ref, *, S):
    C = GLA_CHUNK
    nc = S // C
    r = lax.broadcasted_iota(I32, (C, C), 0)
    c = lax.broadcasted_iota(I32, (C, C), 1)
    lo = lax.broadcasted_iota(I32, (C, LANES), 1) < HEAD_DIM
    st_ref[...] = jnp.zeros_like(st_ref)

    def one_chunk(ci, direction):
        mask = (r >= c) if direction == 0 else (r <= c)
        la_ref, o_ref = (laf_ref, of_ref) if direction == 0 else (lab_ref, ob_ref)
        r0 = pl.multiple_of(ci * C, C)
        rows = pl.ds(r0, C)
        q = q_ref[0, rows, :].astype(F32) * (HEAD_DIM ** -0.5)
        k = k_ref[0, rows, :].astype(F32)
        tri = mask.astype(BF16)
        l1, l2, l3 = _split3(la_ref[0, rows, :])
        cs = lambda h: jnp.dot(tri, h, preferred_element_type=F32)
        b = cs(l1) + cs(l2) + cs(l3)
        g_tot = b[C - 1:C, :] if direction == 0 else b[0:1, :]
        qd = q * jnp.exp(b)
        kd = (k * jnp.exp(-b)).astype(BF16)
        kst = (k * jnp.exp(g_tot - b)).astype(BF16)
        decay = jnp.exp(g_tot)
        for hh in range(2):
            qm = jnp.where(lo if hh == 0 else ~lo, qd, 0.0).astype(BF16)
            a = lax.dot_general(qm, kd, NT_DIMS, preferred_element_type=F32)
            a = jnp.where(mask, a, 0.0).astype(BF16)
            vh = v_ref[0, rows, hh * GLA_DV:(hh + 1) * GLA_DV]
            st = st_ref[direction, hh]
            o = (jnp.dot(a, vh, preferred_element_type=F32)
                 + lax.dot_general(qm, st.astype(BF16), NT_DIMS, preferred_element_type=F32))
            o_ref[rows, hh * GLA_DV:(hh + 1) * GLA_DV] = o
            d_st = jnp.dot(vh.astype(F32).T.astype(BF16), kst, preferred_element_type=F32)
            st_ref[direction, hh] = st * decay + d_st

    def step(i, carry):
        one_chunk(i, 0)
        one_chunk(nc - 1 - i, 1)
        return carry

    lax.fori_loop(0, nc, step, 0)

    tr = 256
    def epilogue(i, carry):
        rows = pl.ds(pl.multiple_of(i * tr, tr), tr)
        for hh in range(2):
            cols = slice(hh * GLA_DV, (hh + 1) * GLA_DV)
            o = of_ref[rows, cols] + ob_ref[rows, cols]
            o = o * lax.rsqrt(jnp.mean(o * o, axis=-1, keepdims=True) + NORM_EPS) * ng_ref[0, :, cols]
            gate = gate_ref[0, rows, cols].astype(F32)
            out_ref[0, rows, cols] = (o * (gate * _sigmoid(gate))).astype(BF16)
        return carry

    lax.fori_loop(0, S // tr, epilogue, 0)


def _gla(gla, la, norm_g, B, S):
    g3 = gla.reshape(B, S, GLA_COLS)
    la3 = la.reshape(B, S, 2 * GLA_KEY_W)
    ng = norm_g.reshape(2, 1, 2 * GLA_DV)
    body = functools.partial(_gla_body, S=S)
    spec = lambda w, off: pl.BlockSpec((1, S, w), lambda b, p: (b, 0, off + p))
    out = pl.pallas_call(
        body,
        grid=(B, 2),
        in_specs=[spec(LANES, 0),
                  spec(LANES, GLA_KEY_W // LANES),
                  spec(2 * GLA_DV, (2 * GLA_KEY_W) // (2 * GLA_DV)),
                  spec(2 * GLA_DV, (2 * GLA_KEY_W + GLA_VAL_W) // (2 * GLA_DV)),
                  spec(LANES, 0),
                  spec(LANES, GLA_KEY_W // LANES),
                  pl.BlockSpec((1, 1, 2 * GLA_DV), lambda b, p: (p, 0, 0))],
        out_specs=pl.BlockSpec((1, S, 2 * GLA_DV), lambda b, p: (b, 0, p)),
        out_shape=jax.ShapeDtypeStruct((B, S, GLA_VAL_W), BF16),
        scratch_shapes=[pltpu.VMEM((S, 2 * GLA_DV), F32), pltpu.VMEM((S, 2 * GLA_DV), F32),
                        pltpu.VMEM((2, 2, GLA_DV, LANES), F32)],
        compiler_params=pltpu.CompilerParams(dimension_semantics=("arbitrary", "arbitrary"),
                                             vmem_limit_bytes=56 * 2**20),
        name="gla",
    )(g3, g3, g3, g3, la3, la3, ng)
    return out.reshape(B * S, GLA_VAL_W)


def _mix_body(o1, o2, o3, l1, l2, l3, g_ref, x_ref, wo_ref, n2_ref, rwh_ref, rwl_ref, rb_ref,
              x1_ref, h_ref, e_ref, rk_ref, gt_ref, cnt_ref, base_ref, *, tm):
    @pl.when(pl.program_id(0) == 0)
    def _():
        base_ref[...] = jnp.zeros_like(base_ref)

    ls = [l1[...], l2[...], l3[...]]
    m = jnp.maximum(jnp.maximum(ls[0], ls[1]), ls[2])
    es = [jnp.exp(l - m) for l in ls]
    a = (es[0] * o1[...].astype(F32) + es[1] * o2[...].astype(F32) + es[2] * o3[...].astype(F32)) / (es[0] + es[1] + es[2])
    mix = (jnp.dot(a.astype(BF16), wo_ref[:ATTN_W, :], preferred_element_type=F32)
           + jnp.dot(g_ref[...], wo_ref[ATTN_W:, :], preferred_element_type=F32))
    x1 = x_ref[...] + mix
    x1_ref[...] = x1
    h = x1 * lax.rsqrt(jnp.mean(x1 * x1, axis=-1, keepdims=True) + NORM_EPS) * n2_ref[...]
    for j in range(ROW_TILES):
        h_ref[:, j, :] = h[:, j * LANES:(j + 1) * LANES]

    hh, hl, _ = _split3(h)
    dg = lambda u, v: lax.dot_general(u, v, NT_DIMS, preferred_element_type=F32)
    logits = dg(rwh_ref[...], hh) + dg(rwl_ref[...], hh) + dg(rwh_ref[...], hl) + rb_ref[...]
    ie = lax.broadcasted_iota(I32, (N_EXPERTS, tm), 0)
    cur = logits
    vals, idxs = [], []
    for _ in range(TOP_K):
        mx = jnp.max(cur, axis=0, keepdims=True)
        ix = jnp.min(jnp.where(cur == mx, ie, N_EXPERTS), axis=0, keepdims=True)
        vals.append(mx)
        idxs.append(ix)
        cur = jnp.where(ie == ix, -jnp.inf, cur)
    ex = [jnp.exp(v - vals[0]) for v in vals]
    den = ex[0] + ex[1] + ex[2] + ex[3]
    gates = [e / den for e in ex]

    onehots = [ie == ix for ix in idxs]
    chosen = (onehots[0] | onehots[1] | onehots[2] | onehots[3]).astype(F32)
    tr = lax.broadcasted_iota(I32, (tm, tm), 0)
    tc = lax.broadcasted_iota(I32, (tm, tm), 1)
    before = jnp.dot(chosen.astype(BF16), (tr < tc).astype(BF16), preferred_element_type=F32)
    tot = before + base_ref[:, 0:1]
    ranks = [jnp.sum(jnp.where(oh, tot, 0.0), axis=0, keepdims=True) for oh in onehots]
    new_base = base_ref[...] + jnp.sum(chosen, axis=1, keepdims=True)
    base_ref[...] = new_base
    cnt_ref[...] = new_base

    sub = lax.broadcasted_iota(I32, (8, tm), 0)
    pack = lambda rows, zero: functools.reduce(
        lambda acc, kr: jnp.where(sub == kr[0], kr[1], acc), enumerate(rows), jnp.full((8, tm), zero, rows[0].dtype))
    e_ref[...] = pack(idxs, 0)
    rk_ref[...] = pack([rk.astype(I32) for rk in ranks], 0)
    g8 = pack(gates, 0.0)
    gt_ref[...] = jnp.concatenate([g8, jnp.zeros((LANES - 8, tm), F32)], axis=0).T


def _mix(outs, lses, g, x2, w_out, norm2_g, router_w, router_b):
    T, D = x2.shape
    tm = 512
    wo = w_out.astype(BF16)
    rwt = router_w.T
    rwh = rwt.astype(BF16)
    rwl = (rwt - rwh.astype(F32)).astype(BF16)
    rb = router_b.reshape(N_EXPERTS, 1)
    n2 = norm2_g.reshape(1, D)
    row = lambda c: pl.BlockSpec((tm, c), lambda i: (i, 0))
    full = lambda a: pl.BlockSpec(a.shape, lambda i: (0,) * a.ndim)
    col = pl.BlockSpec((8, tm), lambda i: (0, i))
    return pl.pallas_call(
        functools.partial(_mix_body, tm=tm),
        grid=(T // tm,),
        in_specs=[row(ATTN_W)] * 6 + [row(GLA_VAL_W), row(D), full(wo), full(n2), full(rwh), full(rwl), full(rb)],
        out_specs=[row(D), pl.BlockSpec((tm, ROW_TILES, LANES), lambda i: (i, 0, 0)), col, col, row(LANES),
                   pl.BlockSpec((N_EXPERTS, LANES), lambda i: (0, 0))],
        out_shape=[jax.ShapeDtypeStruct((T, D), F32),
                   jax.ShapeDtypeStruct((T, ROW_TILES, LANES), F32),
                   jax.ShapeDtypeStruct((8, T), I32),
                   jax.ShapeDtypeStruct((8, T), I32),
                   jax.ShapeDtypeStruct((T, LANES), F32),
                   jax.ShapeDtypeStruct((N_EXPERTS, LANES), F32)],
        scratch_shapes=[pltpu.VMEM((N_EXPERTS, LANES), F32)],
        compiler_params=pltpu.CompilerParams(dimension_semantics=("arbitrary",),
                                             vmem_limit_bytes=48 * 2**20),
        name="mix_router",
    )(*outs, *lses, g, x2, wo, n2, rwh, rwl, rb)


def _plan_body(cnt_ref, e_ref, rk_ref, dest_ref, be_ref, nu_ref, *, nblk):
    shift = MOE_BLOCK.bit_length() - 1
    e = e_ref[...]
    dest = rk_ref[...]
    start = jnp.int32(0)
    for ex in range(N_EXPERTS):
        dest = dest + jnp.where(e == ex, start, 0)
        nb = (cnt_ref[ex] + (MOE_BLOCK - 1)) >> shift
        b0 = start >> shift

        def fill(j, carry, b0=b0, ex=ex):
            be_ref[b0 + j] = jnp.int32(ex)
            return carry

        lax.fori_loop(0, nb, fill, 0)
        start = start + (nb << shift)
    dest_ref[...] = dest
    used = start >> shift

    def tail(j, carry):
        be_ref[j] = jnp.int32(N_EXPERTS - 1)
        return carry

    lax.fori_loop(used, nblk, tail, 0)
    nu_ref[0] = used


def _plan(counts, e_t, rk_t, nblk):
    smem = pl.BlockSpec(memory_space=pltpu.SMEM)
    vmem = pl.BlockSpec(memory_space=pltpu.VMEM)
    return pl.pallas_call(
        functools.partial(_plan_body, nblk=nblk),
        in_specs=[smem, vmem, vmem],
        out_specs=[vmem, smem, smem],
        out_shape=[jax.ShapeDtypeStruct(e_t.shape, I32),
                   jax.ShapeDtypeStruct((nblk,), I32),
                   jax.ShapeDtypeStruct((1,), I32)],
        name="plan",
    )(counts, e_t, rk_t)


def _dispatch_body(dest_ref, h_ref, xin_in_ref, xin_ref, sem, *, tmd):
    del xin_in_ref

    def issue(t, carry):
        for k in range(TOP_K):
            pltpu.make_async_copy(h_ref.at[pl.ds(t, 1)], xin_ref.at[pl.ds(dest_ref[k, t], 1)], sem).start()
        return carry

    lax.fori_loop(0, tmd, issue, 0, unroll=8)
    for k in range(TOP_K):
        pltpu.make_async_copy(h_ref, xin_ref.at[pl.ds(0, tmd)], sem).wait()


def _dispatch(dest, h, n_rows):
    T = h.shape[0]
    tmd = 512
    xin0 = jnp.zeros((n_rows, ROW_TILES, LANES), F32)
    return pl.pallas_call(
        functools.partial(_dispatch_body, tmd=tmd),
        grid=(T // tmd,),
        in_specs=[pl.BlockSpec((8, tmd), lambda i: (0, i), memory_space=pltpu.SMEM),
                  pl.BlockSpec((tmd, ROW_TILES, LANES), lambda i: (i, 0, 0)),
                  pl.BlockSpec(memory_space=pl.ANY)],
        out_specs=pl.BlockSpec(memory_space=pl.ANY),
        out_shape=jax.ShapeDtypeStruct(xin0.shape, F32),
        scratch_shapes=[pltpu.SemaphoreType.DMA(())],
        input_output_aliases={2: 0},
        compiler_params=pltpu.CompilerParams(dimension_semantics=("arbitrary",)),
        name="dispatch",
    )(dest, h, xin0)


def _w1_split_body(w_ref, g_ref, l_ref, t_ref, *, tf):
    wt = w_ref[0].T
    for c in range(wt.shape[1] // LANES):
        cols = slice(c * LANES, (c + 1) * LANES)
        t_ref[c] = wt[:, cols]
        g_ref[0, :, cols] = t_ref[c, pl.ds(0, tf // 2, stride=2), :].astype(BF16)
        l_ref[0, :, cols] = t_ref[c, pl.ds(1, tf // 2, stride=2), :].astype(BF16)


def _w1_split(w1):
    E, D, F2 = w1.shape
    tf = 512
    out_spec = pl.BlockSpec((1, tf // 2, D), lambda e, j: (e, j, 0))
    return pl.pallas_call(
        functools.partial(_w1_split_body, tf=tf),
        grid=(E, F2 // tf),
        in_specs=[pl.BlockSpec((1, D, tf), lambda e, j: (e, 0, j))],
        out_specs=[out_spec, out_spec],
        out_shape=[jax.ShapeDtypeStruct((E, F2 // 2, D), BF16)] * 2,
        scratch_shapes=[pltpu.VMEM((D // LANES, tf, LANES), F32)],
        compiler_params=pltpu.CompilerParams(dimension_semantics=("arbitrary", "arbitrary")),
        name="w1_split",
    )(w1)


def _moe_body(be_ref, nu_ref, x_ref, w1g_ref, w1l_ref, b1g_ref, b1l_ref, w2_ref, b2_ref, y_ref):
    del be_ref
    live = pl.program_id(0) < nu_ref[0]

    @pl.when(live)
    def _():
        x = jnp.concatenate([x_ref[:, c, :] for c in range(ROW_TILES)], axis=1).astype(BF16)
        hg = lax.dot_general(x, w1g_ref[0], NT_DIMS, preferred_element_type=F32) + b1g_ref[0]
        hl = lax.dot_general(x, w1l_ref[0], NT_DIMS, preferred_element_type=F32) + b1l_ref[0]
        xg = jnp.minimum(hg, SWIGLU_LIMIT)
        xl = jnp.clip(hl, -SWIGLU_LIMIT, SWIGLU_LIMIT)
        act = xg * _sigmoid(SWIGLU_ALPHA * xg) * (xl + 1.0)
        y = jnp.dot(act.astype(BF16), w2_ref[0], preferred_element_type=F32) + b2_ref[0]
        for c in range(ROW_TILES):
            y_ref[:, c, :] = y[:, c * LANES:(c + 1) * LANES]

    @pl.when(jnp.logical_not(live))
    def _():
        y_ref[...] = jnp.zeros_like(y_ref)


def _moe(block_e, n_used, xin, w1, b1, w2, b2):
    n_rows = xin.shape[0]
    D = w1.shape[1]
    F = w2.shape[1]
    nblk = n_rows // MOE_BLOCK
    w1g, w1l = _w1_split(w1)
    b1g = b1[:, 0::2].reshape(N_EXPERTS, 1, F)
    b1l = b1[:, 1::2].reshape(N_EXPERTS, 1, F)
    w2b = w2.astype(BF16)
    b2r = b2.reshape(N_EXPERTS, 1, D)
    xspec = pl.BlockSpec((MOE_BLOCK, ROW_TILES, LANES), lambda j, be, nu: (j, 0, 0))
    wspec = lambda a: pl.BlockSpec((1,) + a.shape[1:], lambda j, be, nu: (be[j], 0, 0))
    return pl.pallas_call(
        _moe_body,
        grid_spec=pltpu.PrefetchScalarGridSpec(
            num_scalar_prefetch=2,
            grid=(nblk,),
            in_specs=[xspec, wspec(w1g), wspec(w1l), wspec(b1g), wspec(b1l), wspec(w2b), wspec(b2r)],
            out_specs=xspec),
        out_shape=jax.ShapeDtypeStruct(xin.shape, F32),
        compiler_params=pltpu.CompilerParams(dimension_semantics=("arbitrary",),
                                             vmem_limit_bytes=48 * 2**20),
        name="moe",
    )(block_e, n_used, xin, w1g, w1l, b1g, b1l, w2b, b2r)


def _combine_body(dest_ref, y_ref, x1_ref, gt_ref, fg_ref, out_ref, buf_ref, sem, *, tmc):
    def issue(t, carry):
        for k in range(TOP_K):
            pltpu.make_async_copy(y_ref.at[pl.ds(dest_ref[k, t], 1)], buf_ref.at[k, pl.ds(t, 1)], sem).start()
        return carry

    lax.fori_loop(0, tmc, issue, 0, unroll=8)
    for k in range(TOP_K):
        pltpu.make_async_copy(y_ref.at[pl.ds(0, tmc)], buf_ref.at[k], sem).wait()

    acc = x1_ref[...]
    g = gt_ref[...]
    for k in range(TOP_K):
        yk = jnp.concatenate([buf_ref[k, :, c, :] for c in range(ROW_TILES)], axis=1)
        acc = acc + yk * g[:, k:k + 1]
    out_ref[...] = acc * lax.rsqrt(jnp.mean(acc * acc, axis=-1, keepdims=True) + NORM_EPS) * fg_ref[...]


def _combine(dest, y, x1, gates, final_g):
    T, D = x1.shape
    tmc = 256
    fg = final_g.reshape(1, D)
    return pl.pallas_call(
        functools.partial(_combine_body, tmc=tmc),
        grid=(T // tmc,),
        in_specs=[pl.BlockSpec((8, tmc), lambda i: (0, i), memory_space=pltpu.SMEM),
                  pl.BlockSpec(memory_space=pl.ANY),
                  pl.BlockSpec((tmc, D), lambda i: (i, 0)),
                  pl.BlockSpec((tmc, LANES), lambda i: (i, 0)),
                  pl.BlockSpec((1, D), lambda i: (0, 0))],
        out_specs=pl.BlockSpec((tmc, D), lambda i: (i, 0)),
        out_shape=jax.ShapeDtypeStruct((T, D), F32),
        scratch_shapes=[pltpu.VMEM((TOP_K, tmc, ROW_TILES, LANES), F32), pltpu.SemaphoreType.DMA(())],
        compiler_params=pltpu.CompilerParams(dimension_semantics=("arbitrary",),
                                             vmem_limit_bytes=48 * 2**20),
        name="combine",
    )(dest, y, x1, gates, fg)


def _mixers(x2, B, S, norm1_g, w_in, rel_bias, wg_f, bg_f, wg_b, bg_b, gla_norm_g):
    attn, gla, la = _in_proj(x2, norm1_g, w_in, wg_f, bg_f, wg_b, bg_b)
    branches = [_attn_branch(attn, rel_bias, B, S, d) for d in DILATIONS]
    g = _gla(gla, la, gla_norm_g, B, S)
    return [o for o, _ in branches], [l for _, l in branches], g


def _moe_layer(h, e_t, rk_t, gates, cnt, x1, w1, b1, w2, b2, final_g):
    T = x1.shape[0]
    n_rows = T * TOP_K + N_EXPERTS * MOE_BLOCK
    counts = cnt[:, 0].astype(I32)
    dest, block_e, n_used = _plan(counts, e_t, rk_t, n_rows // MOE_BLOCK)
    xin = _dispatch(dest, h, n_rows)
    y = _moe(block_e, n_used, xin, w1, b1, w2, b2)
    return _combine(dest, y, x1, gates, final_g)


def kernel(x, norm1_g, w_in, rel_bias, gla_wg_fwd, gla_bg_fwd, gla_wg_bwd, gla_bg_bwd, gla_norm_g, w_out, norm2_g, router_w, router_b, moe_w1, moe_b1, moe_w2, moe_b2, final_g):
    B, S, D = x.shape
    assert w_in.shape[0] == 1, "one layer"
    x2 = x.reshape(B * S, D)
    outs, lses, g = _mixers(x2, B, S, norm1_g[0], w_in[0], rel_bias, gla_wg_fwd[0], gla_bg_fwd[0],
                            gla_wg_bwd[0], gla_bg_bwd[0], gla_norm_g[0])
    x1, h, e_t, rk_t, gates, cnt = _mix(outs, lses, g, x2, w_out[0], norm2_g[0], router_w[0], router_b[0])
    out = _moe_layer(h, e_t, rk_t, gates, cnt, x1, moe_w1[0], moe_b1[0], moe_w2[0], moe_b2[0], final_g)
    return out.reshape(B, S, D)
```

```python
import functools

import jax
import jax.numpy as jnp
import numpy as np
from jax import lax
from jax.experimental import pallas as pl
from jax.experimental.pallas import tpu as pltpu

F32, BF16, I32 = jnp.float32, jnp.bfloat16, jnp.int32

LANES = 128
HEAD_DIM = 64
ATTN_HEADS = 8
ATTN_W = ATTN_HEADS * HEAD_DIM
HALF_SPAN = 64
ATTN_QBLK = 128
ATTN_KWIN = 256
DILATIONS = (1, 4, 16)
T5_BUCKETS = 32
T5_MAX_DISTANCE = 1024
NEG_INF = -1e30
GLA_KEY_W = 256
GLA_VAL_W = 512
GLA_DV = 128
GLA_RANK = 16
GLA_TAU = 16.0
GLA_CHUNK = 64
N_EXPERTS = 32
TOP_K = 4
SWIGLU_LIMIT = 7.0
SWIGLU_ALPHA = 1.702
NORM_EPS = 1e-5
MOE_BLOCK = 256

ATTN_COLS = 3 * ATTN_W
GLA_COLS = 2 * GLA_KEY_W + 2 * GLA_VAL_W
Z_COLS = LANES
PROJ_COLS = ATTN_COLS + GLA_COLS + Z_COLS

NT_DIMS = (((1,), (1,)), ((), ()))


def _sigmoid(x):
    return 1.0 / (1.0 + jnp.exp(-x))


def _split3(a):
    h1 = a.astype(BF16)
    r1 = a - h1.astype(F32)
    h2 = r1.astype(BF16)
    h3 = (r1 - h2.astype(F32)).astype(BF16)
    return h1, h2, h3


def _dot_f32(a, b):
    ah, al, _ = _split3(a)
    bh, bl, _ = _split3(b)
    d = lambda u, v: jnp.dot(u, v, preferred_element_type=F32)
    return d(ah, bh) + d(al, bh) + d(ah, bl)


def _in_proj_body(x_ref, g_ref, w_ref, wg_ref, bg_ref, attn_ref, gla_ref, la_ref):
    x = x_ref[...]
    n = x * lax.rsqrt(jnp.mean(x * x, axis=-1, keepdims=True) + NORM_EPS) * g_ref[...]
    p = jnp.dot(n.astype(BF16), w_ref[...], preferred_element_type=F32)
    attn_ref[...] = p[:, :ATTN_COLS].astype(BF16)
    gla_ref[...] = p[:, ATTN_COLS:ATTN_COLS + GLA_COLS].astype(BF16)
    z = p[:, ATTN_COLS + GLA_COLS:]
    zz = _dot_f32(z, wg_ref[...]) + bg_ref[...]
    log_sig = jnp.minimum(zz, 0.0) - jnp.log1p(jnp.exp(-jnp.abs(zz)))
    la_ref[...] = log_sig * (1.0 / GLA_TAU)


def _in_proj(x2, norm_g, w_in, wg_f, bg_f, wg_b, bg_b):
    T, D = x2.shape
    tm = 512
    w = jnp.pad(w_in.astype(BF16), ((0, 0), (0, PROJ_COLS - w_in.shape[1])))
    wg = jnp.zeros((Z_COLS, 2 * GLA_KEY_W), F32)
    wg = wg.at[:GLA_RANK, :GLA_KEY_W].set(wg_f).at[GLA_RANK:2 * GLA_RANK, GLA_KEY_W:].set(wg_b)
    bg = jnp.concatenate([bg_f, bg_b]).reshape(1, 2 * GLA_KEY_W)
    row = lambda c: pl.BlockSpec((tm, c), lambda i: (i, 0))
    full = lambda a: pl.BlockSpec(a.shape, lambda i: (0,) * a.ndim)
    g2 = norm_g.reshape(1, D)
    return pl.pallas_call(
        _in_proj_body,
        grid=(T // tm,),
        in_specs=[row(D), full(g2), full(w), full(wg), full(bg)],
        out_specs=[row(ATTN_COLS), row(GLA_COLS), row(2 * GLA_KEY_W)],
        out_shape=[jax.ShapeDtypeStruct((T, ATTN_COLS), BF16),
                   jax.ShapeDtypeStruct((T, GLA_COLS), BF16),
                   jax.ShapeDtypeStruct((T, 2 * GLA_KEY_W), F32)],
        compiler_params=pltpu.CompilerParams(dimension_semantics=("arbitrary",),
                                             vmem_limit_bytes=48 * 2**20),
        name="in_proj",
    )(x2, g2, w, wg, bg)


def _t5_bucket(rel):
    nb = T5_BUCKETS // 2
    max_exact = nb // 2
    n = np.abs(rel)
    large = max_exact + (np.log(np.maximum(n, 1).astype(np.float32) / max_exact)
                         / np.log(T5_MAX_DISTANCE / max_exact) * (nb - max_exact)).astype(np.int32)
    large = np.minimum(large, nb - 1)
    return (np.where(rel > 0, nb, 0) + np.where(n < max_exact, n, large)).astype(np.int32)


def _bucket_tables(dil):
    s = np.arange(ATTN_QBLK)[:, None]
    t = np.arange(ATTN_KWIN)[None, :]
    tabs = []
    for c in (0, HALF_SPAN, 2 * HALF_SPAN):
        off = t - c - s
        tabs.append(np.where(np.abs(off) <= HALF_SPAN, _t5_bucket(off * dil), -1))
    return np.stack(tabs).astype(np.int32)


def _attn_body(rb_ref, idx_ref, q_ref, k_ref, v_ref, o_ref, lse_ref, tab_ref, *, L, TQ):
    first = (pl.program_id(0) == 0) & (pl.program_id(1) == 0) & (pl.program_id(2) == 0)

    @pl.when(first)
    def _():
        for v in range(3):
            idx = idx_ref[v]
            for h in range(ATTN_HEADS):
                fill = lambda j, t, h=h, idx=idx: jnp.where(idx == j, rb_ref[j, h], t)
                tab_ref[v, h] = lax.fori_loop(0, T5_BUCKETS, fill, jnp.where(idx < 0, NEG_INF, 0.0).astype(F32))

    i = pl.program_id(2)
    lo = lax.broadcasted_iota(I32, (ATTN_QBLK, LANES), 1) < HEAD_DIM
    scale = HEAD_DIM ** -0.5
    for sb in range(TQ // ATTN_QBLK):
        l0 = i * TQ + sb * ATTN_QBLK
        ws = pl.multiple_of(jnp.clip(l0 - HALF_SPAN, 0, L - ATTN_KWIN), HALF_SPAN)
        var = jnp.where(l0 == 0, 0, jnp.where(l0 == L - ATTN_QBLK, 2, 1))
        rows = slice(sb * ATTN_QBLK, (sb + 1) * ATTN_QBLK)
        for hp in range(ATTN_HEADS // 2):
            cols = slice(hp * LANES, (hp + 1) * LANES)
            q2 = q_ref[0, rows, cols]
            k2 = k_ref[0, pl.ds(ws, ATTN_KWIN), cols]
            v2 = v_ref[0, pl.ds(ws, ATTN_KWIN), cols]
            zero = jnp.zeros_like(q2)
            qs = jnp.concatenate([jnp.where(lo, q2, zero), jnp.where(lo, zero, q2)], axis=0)
            s = lax.dot_general(qs, k2, NT_DIMS, preferred_element_type=F32)
            bias = jnp.concatenate([tab_ref[var, 2 * hp], tab_ref[var, 2 * hp + 1]], axis=0)
            logits = s * scale + bias
            m = jnp.max(logits, axis=-1, keepdims=True)
            p = jnp.exp(logits - m)
            den = jnp.sum(p, axis=-1, keepdims=True)
            pv = jnp.dot(p.astype(BF16), v2, preferred_element_type=F32) / den
            lse = jnp.broadcast_to(m + jnp.log(den), (2 * ATTN_QBLK, LANES))
            o_ref[0, rows, cols] = jnp.where(lo, pv[:ATTN_QBLK], pv[ATTN_QBLK:]).astype(BF16)
            lse_ref[0, rows, cols] = jnp.where(lo, lse[:ATTN_QBLK], lse[ATTN_QBLK:])


def _attn_branch(attn, rel_bias, B, S, dil):
    L = S // dil
    TQ = 256
    a3 = attn.reshape(B, L, dil * ATTN_COLS)
    idx = jnp.asarray(_bucket_tables(dil))
    body = functools.partial(_attn_body, L=L, TQ=TQ)
    kv = lambda j: pl.BlockSpec((1, L, ATTN_W), lambda b, r, i: (b, 0, 3 * r + j))
    out_spec = pl.BlockSpec((1, TQ, ATTN_W), lambda b, r, i: (b, i, r))
    o, lse = pl.pallas_call(
        body,
        grid=(B, dil, L // TQ),
        in_specs=[pl.BlockSpec(memory_space=pltpu.SMEM),
                  pl.BlockSpec(idx.shape, lambda b, r, i: (0, 0, 0)),
                  pl.BlockSpec((1, TQ, ATTN_W), lambda b, r, i: (b, i, 3 * r)),
                  kv(1), kv(2)],
        out_specs=[out_spec, out_spec],
        out_shape=[jax.ShapeDtypeStruct((B, L, dil * ATTN_W), BF16),
                   jax.ShapeDtypeStruct((B, L, dil * ATTN_W), F32)],
        scratch_shapes=[pltpu.VMEM((3, ATTN_HEADS, ATTN_QBLK, ATTN_KWIN), F32)],
        compiler_params=pltpu.CompilerParams(dimension_semantics=("arbitrary",) * 3,
                                             vmem_limit_bytes=48 * 2**20),
        name=f"attn_d{dil}",
    )(rel_bias, idx, a3, a3, a3)
    return o.reshape(B * S, ATTN_W), lse.reshape(B * S, ATTN_W)


def _gla_body(q_ref, k_ref, v_ref, gate_ref, laf_ref, lab_ref, ng_ref, out_ref, of_ref, ob_ref, st_ref, *, S):
    C = GLA_CHUNK
    nc = S // C
    r = lax.broadcasted_iota(I32, (C, C), 0)
    c = lax.broadcasted_iota(I32, (C, C), 1)
    lo = lax.broadcasted_iota(I32, (C, LANES), 1) < HEAD_DIM
    st_ref[...] = jnp.zeros_like(st_ref)

    def one_chunk(ci, direction):
        mask = (r >= c) if direction == 0 else (r <= c)
        la_ref, o_ref = (laf_ref, of_ref) if direction == 0 else (lab_ref, ob_ref)
        r0 = pl.multiple_of(ci * C, C)
        rows = pl.ds(r0, C)
        q = q_ref[0, rows, :].astype(F32) * (HEAD_DIM ** -0.5)
        k = k_ref[0, rows, :].astype(F32)
        tri = mask.astype(BF16)
        l1, l2, l3 = _split3(la_ref[0, rows, :])
        cs = lambda h: jnp.dot(tri, h, preferred_element_type=F32)
        b = cs(l1) + cs(l2) + cs(l3)
        g_tot = b[C - 1:C, :] if direction == 0 else b[0:1, :]
        qd = q * jnp.exp(b)
        kd = (k * jnp.exp(-b)).astype(BF16)
        kst = (k * jnp.exp(g_tot - b)).astype(BF16)
        decay = jnp.exp(g_tot)
        for hh in range(2):
            qm = jnp.where(lo if hh == 0 else ~lo, qd, 0.0).astype(BF16)
            a = lax.dot_general(qm, kd, NT_DIMS, preferred_element_type=F32)
            a = jnp.where(mask, a, 0.0).astype(BF16)
            vh = v_ref[0, rows, hh * GLA_DV:(hh + 1) * GLA_DV]
            st = st_ref[direction, hh]
            o = (jnp.dot(a, vh, preferred_element_type=F32)
                 + lax.dot_general(qm, st.astype(BF16), NT_DIMS, preferred_element_type=F32))
            o_ref[rows, hh * GLA_DV:(hh + 1) * GLA_DV] = o
            d_st = jnp.dot(vh.astype(F32).T.astype(BF16), kst, preferred_element_type=F32)
            st_ref[direction, hh] = st * decay + d_st

    def step(i, carry):
        one_chunk(i, 0)
        one_chunk(nc - 1 - i, 1)
        return carry

    lax.fori_loop(0, nc, step, 0)

    tr = 256
    def epilogue(i, carry):
        rows = pl.ds(pl.multiple_of(i * tr, tr), tr)
        for hh in range(2):
            cols = slice(hh * GLA_DV, (hh + 1) * GLA_DV)
            o = of_ref[rows, cols] + ob_ref[rows, cols]
            o = o * lax.rsqrt(jnp.mean(o * o, axis=-1, keepdims=True) + NORM_EPS) * ng_ref[0, :, cols]
            gate = gate_ref[0, rows, cols].astype(F32)
            out_ref[0, rows, cols] = (o * (gate * _sigmoid(gate))).astype(BF16)
        return carry

    lax.fori_loop(0, S // tr, epilogue, 0)


def _gla(gla, la, norm_g, B, S):
    g3 = gla.reshape(B, S, GLA_COLS)
    la3 = la.reshape(B, S, 2 * GLA_KEY_W)
    ng = norm_g.reshape(2, 1, 2 * GLA_DV)
    body = functools.partial(_gla_body, S=S)
    spec = lambda w, off: pl.BlockSpec((1, S, w), lambda b, p: (b, 0, off + p))
    out = pl.pallas_call(
        body,
        grid=(B, 2),
        in_specs=[spec(LANES, 0),
                  spec(LANES, GLA_KEY_W // LANES),
                  spec(2 * GLA_DV, (2 * GLA_KEY_W) // (2 * GLA_DV)),
                  spec(2 * GLA_DV, (2 * GLA_KEY_W + GLA_VAL_W) // (2 * GLA_DV)),
                  spec(LANES, 0),
                  spec(LANES, GLA_KEY_W // LANES),
                  pl.BlockSpec((1, 1, 2 * GLA_DV), lambda b, p: (p, 0, 0))],
        out_specs=pl.BlockSpec((1, S, 2 * GLA_DV), lambda b, p: (b, 0, p)),
        out_shape=jax.ShapeDtypeStruct((B, S, GLA_VAL_W), BF16),
        scratch_shapes=[pltpu.VMEM((S, 2 * GLA_DV), F32), pltpu.VMEM((S, 2 * GLA_DV), F32),
                        pltpu.VMEM((2, 2, GLA_DV, LANES), F32)],
        compiler_params=pltpu.CompilerParams(dimension_semantics=("arbitrary", "arbitrary"),
                                             vmem_limit_bytes=56 * 2**20),
        name="gla",
    )(g3, g3, g3, g3, la3, la3, ng)
    return out.reshape(B * S, GLA_VAL_W)


def _mix_body(o1, o2, o3, l1, l2, l3, g_ref, x_ref, wo_ref, n2_ref, rwh_ref, rwl_ref, rb_ref,
              x1_ref, h_ref, e_ref, rk_ref, gt_ref, cnt_ref, base_ref, *, tm):
    @pl.when(pl.program_id(0) == 0)
    def _():
        base_ref[...] = jnp.zeros_like(base_ref)

    ls = [l1[...], l2[...], l3[...]]
    m = jnp.maximum(jnp.maximum(ls[0], ls[1]), ls[2])
    es = [jnp.exp(l - m) for l in ls]
    a = (es[0] * o1[...].astype(F32) + es[1] * o2[...].astype(F32) + es[2] * o3[...].astype(F32)) / (es[0] + es[1] + es[2])
    mix = (jnp.dot(a.astype(BF16), wo_ref[:ATTN_W, :], preferred_element_type=F32)
           + jnp.dot(g_ref[...], wo_ref[ATTN_W:, :], preferred_element_type=F32))
    x1 = x_ref[...] + mix
    x1_ref[...] = x1
    h = x1 * lax.rsqrt(jnp.mean(x1 * x1, axis=-1, keepdims=True) + NORM_EPS) * n2_ref[...]
    h_ref[...] = h

    hh, hl, _ = _split3(h)
    dg = lambda u, v: lax.dot_general(u, v, NT_DIMS, preferred_element_type=F32)
    logits = dg(rwh_ref[...], hh) + dg(rwl_ref[...], hh) + dg(rwh_ref[...], hl) + rb_ref[...]
    ie = lax.broadcasted_iota(I32, (N_EXPERTS, tm), 0)
    cur = logits
    vals, idxs = [], []
    for _ in range(TOP_K):
        mx = jnp.max(cur, axis=0, keepdims=True)
        ix = jnp.min(jnp.where(cur == mx, ie, N_EXPERTS), axis=0, keepdims=True)
        vals.append(mx)
        idxs.append(ix)
        cur = jnp.where(ie == ix, -jnp.inf, cur)
    ex = [jnp.exp(v - vals[0]) for v in vals]
    den = ex[0] + ex[1] + ex[2] + ex[3]
    gates = [e / den for e in ex]

    onehots = [ie == ix for ix in idxs]
    chosen = (onehots[0] | onehots[1] | onehots[2] | onehots[3]).astype(F32)
    tr = lax.broadcasted_iota(I32, (tm, tm), 0)
    tc = lax.broadcasted_iota(I32, (tm, tm), 1)
    before = jnp.dot(chosen.astype(BF16), (tr < tc).astype(BF16), preferred_element_type=F32)
    tot = before + base_ref[:, 0:1]
    ranks = [jnp.sum(jnp.where(oh, tot, 0.0), axis=0, keepdims=True) for oh in onehots]
    new_base = base_ref[...] + jnp.sum(chosen, axis=1, keepdims=True)
    base_ref[...] = new_base
    cnt_ref[...] = new_base

    sub = lax.broadcasted_iota(I32, (8, tm), 0)
    pack = lambda rows, zero: functools.reduce(
        lambda acc, kr: jnp.where(sub == kr[0], kr[1], acc), enumerate(rows), jnp.full((8, tm), zero, rows[0].dtype))
    e_ref[...] = pack(idxs, 0)
    rk_ref[...] = pack([rk.astype(I32) for rk in ranks], 0)
    g8 = pack(gates, 0.0)
    gt_ref[...] = jnp.concatenate([g8, jnp.zeros((LANES - 8, tm), F32)], axis=0).T


def _mix(outs, lses, g, x2, w_out, norm2_g, router_w, router_b):
    T, D = x2.shape
    tm = 512
    wo = w_out.astype(BF16)
    rwt = router_w.T
    rwh = rwt.astype(BF16)
    rwl = (rwt - rwh.astype(F32)).astype(BF16)
    rb = router_b.reshape(N_EXPERTS, 1)
    n2 = norm2_g.reshape(1, D)
    row = lambda c: pl.BlockSpec((tm, c), lambda i: (i, 0))
    full = lambda a: pl.BlockSpec(a.shape, lambda i: (0,) * a.ndim)
    col = pl.BlockSpec((8, tm), lambda i: (0, i))
    return pl.pallas_call(
        functools.partial(_mix_body, tm=tm),
        grid=(T // tm,),
        in_specs=[row(ATTN_W)] * 6 + [row(GLA_VAL_W), row(D), full(wo), full(n2), full(rwh), full(rwl), full(rb)],
        out_specs=[row(D), row(D), col, col, row(LANES),
                   pl.BlockSpec((N_EXPERTS, LANES), lambda i: (0, 0))],
        out_shape=[jax.ShapeDtypeStruct((T, D), F32),
                   jax.ShapeDtypeStruct((T, D), F32),
                   jax.ShapeDtypeStruct((8, T), I32),
                   jax.ShapeDtypeStruct((8, T), I32),
                   jax.ShapeDtypeStruct((T, LANES), F32),
                   jax.ShapeDtypeStruct((N_EXPERTS, LANES), F32)],
        scratch_shapes=[pltpu.VMEM((N_EXPERTS, LANES), F32)],
        compiler_params=pltpu.CompilerParams(dimension_semantics=("arbitrary",),
                                             vmem_limit_bytes=48 * 2**20),
        name="mix_router",
    )(*outs, *lses, g, x2, wo, n2, rwh, rwl, rb)


def _plan_body(cnt_ref, e_ref, rk_ref, dest_ref, be_ref, nu_ref, *, nblk):
    shift = MOE_BLOCK.bit_length() - 1
    e = e_ref[...]
    dest = rk_ref[...]
    start = jnp.int32(0)
    for ex in range(N_EXPERTS):
        dest = dest + jnp.where(e == ex, start, 0)
        nb = (cnt_ref[ex] + (MOE_BLOCK - 1)) >> shift
        b0 = start >> shift

        def fill(j, carry, b0=b0, ex=ex):
            be_ref[b0 + j] = jnp.int32(ex)
            return carry

        lax.fori_loop(0, nb, fill, 0)
        start = start + (nb << shift)
    dest_ref[...] = dest
    used = start >> shift

    def tail(j, carry):
        be_ref[j] = jnp.int32(N_EXPERTS - 1)
        return carry

    lax.fori_loop(used, nblk, tail, 0)
    nu_ref[0] = used


def _plan(counts, e_t, rk_t, nblk):
    smem = pl.BlockSpec(memory_space=pltpu.SMEM)
    vmem = pl.BlockSpec(memory_space=pltpu.VMEM)
    return pl.pallas_call(
        functools.partial(_plan_body, nblk=nblk),
        in_specs=[smem, vmem, vmem],
        out_specs=[vmem, smem, smem],
        out_shape=[jax.ShapeDtypeStruct(e_t.shape, I32),
                   jax.ShapeDtypeStruct((nblk,), I32),
                   jax.ShapeDtypeStruct((1,), I32)],
        name="plan",
    )(counts, e_t, rk_t)


def _dispatch_body(dest_ref, h_ref, xin_in_ref, xin_ref, sem, *, tmd):
    del xin_in_ref

    def issue(t, carry):
        for k in range(TOP_K):
            pltpu.make_async_copy(h_ref.at[pl.ds(t, 1)], xin_ref.at[pl.ds(dest_ref[k, t], 1)], sem).start()
        return carry

    lax.fori_loop(0, tmd, issue, 0, unroll=8)
    for k in range(TOP_K):
        pltpu.make_async_copy(h_ref, xin_ref.at[pl.ds(0, tmd)], sem).wait()


def _dispatch(dest, h, n_rows):
    T = h.shape[0]
    tmd = 512
    xin0 = jnp.zeros((n_rows, h.shape[1]), F32)
    return pl.pallas_call(
        functools.partial(_dispatch_body, tmd=tmd),
        grid=(T // tmd,),
        in_specs=[pl.BlockSpec((8, tmd), lambda i: (0, i), memory_space=pltpu.SMEM),
                  pl.BlockSpec((tmd, h.shape[1]), lambda i: (i, 0)),
                  pl.BlockSpec(memory_space=pl.ANY)],
        out_specs=pl.BlockSpec(memory_space=pl.ANY),
        out_shape=jax.ShapeDtypeStruct(xin0.shape, F32),
        scratch_shapes=[pltpu.SemaphoreType.DMA(())],
        input_output_aliases={2: 0},
        compiler_params=pltpu.CompilerParams(dimension_semantics=("arbitrary",)),
        name="dispatch",
    )(dest, h, xin0)


def _w1_split_body(w_ref, g_ref, l_ref, t_ref, *, tf):
    wt = w_ref[0].T
    for c in range(wt.shape[1] // LANES):
        cols = slice(c * LANES, (c + 1) * LANES)
        t_ref[c] = wt[:, cols]
        g_ref[0, :, cols] = t_ref[c, pl.ds(0, tf // 2, stride=2), :].astype(BF16)
        l_ref[0, :, cols] = t_ref[c, pl.ds(1, tf // 2, stride=2), :].astype(BF16)


def _w1_split(w1):
    E, D, F2 = w1.shape
    tf = 512
    out_spec = pl.BlockSpec((1, tf // 2, D), lambda e, j: (e, j, 0))
    return pl.pallas_call(
        functools.partial(_w1_split_body, tf=tf),
        grid=(E, F2 // tf),
        in_specs=[pl.BlockSpec((1, D, tf), lambda e, j: (e, 0, j))],
        out_specs=[out_spec, out_spec],
        out_shape=[jax.ShapeDtypeStruct((E, F2 // 2, D), BF16)] * 2,
        scratch_shapes=[pltpu.VMEM((D // LANES, tf, LANES), F32)],
        compiler_params=pltpu.CompilerParams(dimension_semantics=("arbitrary", "arbitrary")),
        name="w1_split",
    )(w1)


def _moe_body(be_ref, nu_ref, x_ref, w1g_ref, w1l_ref, b1g_ref, b1l_ref, w2_ref, b2_ref, y_ref):
    del be_ref
    live = pl.program_id(0) < nu_ref[0]

    @pl.when(live)
    def _():
        x = x_ref[...].astype(BF16)
        hg = lax.dot_general(x, w1g_ref[0], NT_DIMS, preferred_element_type=F32) + b1g_ref[0]
        hl = lax.dot_general(x, w1l_ref[0], NT_DIMS, preferred_element_type=F32) + b1l_ref[0]
        xg = jnp.minimum(hg, SWIGLU_LIMIT)
        xl = jnp.clip(hl, -SWIGLU_LIMIT, SWIGLU_LIMIT)
        act = xg * _sigmoid(SWIGLU_ALPHA * xg) * (xl + 1.0)
        y_ref[...] = jnp.dot(act.astype(BF16), w2_ref[0], preferred_element_type=F32) + b2_ref[0]

    @pl.when(jnp.logical_not(live))
    def _():
        y_ref[...] = jnp.zeros_like(y_ref)


def _moe(block_e, n_used, xin, w1, b1, w2, b2):
    n_rows = xin.shape[0]
    D = w1.shape[1]
    F = w2.shape[1]
    nblk = n_rows // MOE_BLOCK
    w1g, w1l = _w1_split(w1)
    b1g = b1[:, 0::2].reshape(N_EXPERTS, 1, F)
    b1l = b1[:, 1::2].reshape(N_EXPERTS, 1, F)
    w2b = w2.astype(BF16)
    b2r = b2.reshape(N_EXPERTS, 1, D)
    xspec = pl.BlockSpec((MOE_BLOCK, D), lambda j, be, nu: (j, 0))
    wspec = lambda a: pl.BlockSpec((1,) + a.shape[1:], lambda j, be, nu: (be[j], 0, 0))
    return pl.pallas_call(
        _moe_body,
        grid_spec=pltpu.PrefetchScalarGridSpec(
            num_scalar_prefetch=2,
            grid=(nblk,),
            in_specs=[xspec, wspec(w1g), wspec(w1l), wspec(b1g), wspec(b1l), wspec(w2b), wspec(b2r)],
            out_specs=xspec),
        out_shape=jax.ShapeDtypeStruct(xin.shape, F32),
        compiler_params=pltpu.CompilerParams(dimension_semantics=("arbitrary",),
                                             vmem_limit_bytes=48 * 2**20),
        name="moe",
    )(block_e, n_used, xin, w1g, w1l, b1g, b1l, w2b, b2r)


def _combine_body(dest_ref, y_ref, x1_ref, gt_ref, fg_ref, out_ref, buf_ref, sem, *, tmc):
    def issue(t, carry):
        for k in range(TOP_K):
            pltpu.make_async_copy(y_ref.at[pl.ds(dest_ref[k, t], 1)], buf_ref.at[k, pl.ds(t, 1)], sem).start()
        return carry

    lax.fori_loop(0, tmc, issue, 0, unroll=8)
    for k in range(TOP_K):
        pltpu.make_async_copy(y_ref.at[pl.ds(0, tmc)], buf_ref.at[k], sem).wait()

    acc = x1_ref[...]
    g = gt_ref[...]
    for k in range(TOP_K):
        acc = acc + buf_ref[k] * g[:, k:k + 1]
    out_ref[...] = acc * lax.rsqrt(jnp.mean(acc * acc, axis=-1, keepdims=True) + NORM_EPS) * fg_ref[...]


def _combine(dest, y, x1, gates, final_g):
    T, D = x1.shape
    tmc = 256
    fg = final_g.reshape(1, D)
    return pl.pallas_call(
        functools.partial(_combine_body, tmc=tmc),
        grid=(T // tmc,),
        in_specs=[pl.BlockSpec((8, tmc), lambda i: (0, i), memory_space=pltpu.SMEM),
                  pl.BlockSpec(memory_space=pl.ANY),
                  pl.BlockSpec((tmc, D), lambda i: (i, 0)),
                  pl.BlockSpec((tmc, LANES), lambda i: (i, 0)),
                  pl.BlockSpec((1, D), lambda i: (0, 0))],
        out_specs=pl.BlockSpec((tmc, D), lambda i: (i, 0)),
        out_shape=jax.ShapeDtypeStruct((T, D), F32),
        scratch_shapes=[pltpu.VMEM((TOP_K, tmc, D), F32), pltpu.SemaphoreType.DMA(())],
        compiler_params=pltpu.CompilerParams(dimension_semantics=("arbitrary",),
                                             vmem_limit_bytes=48 * 2**20),
        name="combine",
    )(dest, y, x1, gates, fg)


def _mixers(x2, B, S, norm1_g, w_in, rel_bias, wg_f, bg_f, wg_b, bg_b, gla_norm_g):
    attn, gla, la = _in_proj(x2, norm1_g, w_in, wg_f, bg_f, wg_b, bg_b)
    branches = [_attn_branch(attn, rel_bias, B, S, d) for d in DILATIONS]
    g = _gla(gla, la, gla_norm_g, B, S)
    return [o for o, _ in branches], [l for _, l in branches], g


def _moe_layer(h, e_t, rk_t, gates, cnt, x1, w1, b1, w2, b2, final_g):
    T = x1.shape[0]
    n_rows = T * TOP_K + N_EXPERTS * MOE_BLOCK
    counts = cnt[:, 0].astype(I32)
    dest, block_e, n_used = _plan(counts, e_t, rk_t, n_rows // MOE_BLOCK)
    xin = _dispatch(dest, h, n_rows)
    y = _moe(block_e, n_used, xin, w1, b1, w2, b2)
    return _combine(dest, y, x1, gates, final_g)


def kernel(x, norm1_g, w_in, rel_bias, gla_wg_fwd, gla_bg_fwd, gla_wg_bwd, gla_bg_bwd, gla_norm_g, w_out, norm2_g, router_w, router_b, moe_w1, moe_b1, moe_w2, moe_b2, final_g):
    B, S, D = x.shape
    assert w_in.shape[0] == 1, "one layer"
    x2 = x.reshape(B * S, D)
    outs, lses, g = _mixers(x2, B, S, norm1_g[0], w_in[0], rel_bias, gla_wg_fwd[0], gla_bg_fwd[0],
                            gla_wg_bwd[0], gla_bg_bwd[0], gla_norm_g[0])
    x1, h, e_t, rk_t, gates, cnt = _mix(outs, lses, g, x2, w_out[0], norm2_g[0], router_w[0], router_b[0])
    out = _moe_layer(h, e_t, rk_t, gates, cnt, x1, moe_w1[0], moe_b1[0], moe_w2[0], moe_b2[0], final_g)
    return out.reshape(B, S, D)
```

```python
import functools

import jax
import jax.numpy as jnp
import numpy as np
from jax import lax
from jax.experimental import pallas as pl
from jax.experimental.pallas import tpu as pltpu

F32, BF16, I32 = jnp.float32, jnp.bfloat16, jnp.int32

LANES = 128
HEAD_DIM = 64
ATTN_HEADS = 8
ATTN_W = ATTN_HEADS * HEAD_DIM
HALF_SPAN = 64
ATTN_QBLK = 128
ATTN_KWIN = 256
DILATIONS = (1, 4, 16)
T5_BUCKETS = 32
T5_MAX_DISTANCE = 1024
NEG_INF = -1e30
GLA_KEY_W = 256
GLA_VAL_W = 512
GLA_DV = 128
GLA_RANK = 16
GLA_TAU = 16.0
GLA_CHUNK = 64
N_EXPERTS = 32
TOP_K = 4
SWIGLU_LIMIT = 7.0
SWIGLU_ALPHA = 1.702
NORM_EPS = 1e-5
MOE_BLOCK = 256

ATTN_COLS = 3 * ATTN_W
GLA_COLS = 2 * GLA_KEY_W + 2 * GLA_VAL_W
Z_COLS = LANES
PROJ_COLS = ATTN_COLS + GLA_COLS + Z_COLS

NT_DIMS = (((1,), (1,)), ((), ()))


def _sigmoid(x):
    return 1.0 / (1.0 + jnp.exp(-x))


def _split3(a):
    h1 = a.astype(BF16)
    r1 = a - h1.astype(F32)
    h2 = r1.astype(BF16)
    h3 = (r1 - h2.astype(F32)).astype(BF16)
    return h1, h2, h3


def _dot_f32(a, b):
    ah, al, _ = _split3(a)
    bh, bl, _ = _split3(b)
    d = lambda u, v: jnp.dot(u, v, preferred_element_type=F32)
    return d(ah, bh) + d(al, bh) + d(ah, bl)


def _in_proj_body(x_ref, g_ref, w_ref, wg_ref, bg_ref, a1_ref, a4_ref, a16_ref, gla_ref, b_ref, p_ref, *, tm):
    x = x_ref[...]
    n = x * lax.rsqrt(jnp.mean(x * x, axis=-1, keepdims=True) + NORM_EPS) * g_ref[...]
    p = jnp.dot(n.astype(BF16), w_ref[...], preferred_element_type=F32)
    gla_ref[...] = p[:, ATTN_COLS:ATTN_COLS + GLA_COLS].astype(BF16)

    a1_ref[0, 0] = p[:, :ATTN_COLS].astype(BF16)
    for c in range(ATTN_COLS // LANES):
        p_ref[c] = p[:, c * LANES:(c + 1) * LANES]
    for dil, a_ref in ((DILATIONS[1], a4_ref), (DILATIONS[2], a16_ref)):
        for r in range(dil):
            for c in range(ATTN_COLS // LANES):
                a_ref[0, r, :, c * LANES:(c + 1) * LANES] = p_ref[c, pl.ds(r, tm // dil, stride=dil), :].astype(BF16)

    z = p[:, ATTN_COLS + GLA_COLS:]
    zz = _dot_f32(z, wg_ref[...]) + bg_ref[...]
    la = (jnp.minimum(zz, 0.0) - jnp.log1p(jnp.exp(-jnp.abs(zz)))) * (1.0 / GLA_TAU)
    C = GLA_CHUNK
    r_i = lax.broadcasted_iota(I32, (C, C), 0)
    c_i = lax.broadcasted_iota(I32, (C, C), 1)
    lower = (r_i >= c_i).astype(BF16)
    upper = (r_i <= c_i).astype(BF16)
    W = GLA_KEY_W
    for j in range(tm // C):
        rows = slice(j * C, (j + 1) * C)
        for tri, cols in ((lower, slice(0, W)), (upper, slice(W, 2 * W))):
            terms = jnp.concatenate(_split3(la[rows, cols]), axis=1)
            cs = jnp.dot(tri, terms, preferred_element_type=F32)
            b_ref[rows, cols] = cs[:, :W] + cs[:, W:2 * W] + cs[:, 2 * W:]


def _in_proj(x2, B, S, norm_g, w_in, wg_f, bg_f, wg_b, bg_b):
    T, D = x2.shape
    tm = 512
    nt = S // tm
    w = jnp.pad(w_in.astype(BF16), ((0, 0), (0, PROJ_COLS - w_in.shape[1])))
    wg = jnp.zeros((Z_COLS, 2 * GLA_KEY_W), F32)
    wg = wg.at[:GLA_RANK, :GLA_KEY_W].set(wg_f).at[GLA_RANK:2 * GLA_RANK, GLA_KEY_W:].set(wg_b)
    bg = jnp.concatenate([bg_f, bg_b]).reshape(1, 2 * GLA_KEY_W)
    row = lambda c: pl.BlockSpec((tm, c), lambda b, i: (b * nt + i, 0))
    full = lambda a: pl.BlockSpec(a.shape, lambda b, i: (0,) * a.ndim)
    res = lambda d: pl.BlockSpec((1, d, tm // d, ATTN_COLS), lambda b, i: (b, 0, i, 0))
    res_shape = lambda d: jax.ShapeDtypeStruct((B, d, S // d, ATTN_COLS), BF16)
    g2 = norm_g.reshape(1, D)
    return pl.pallas_call(
        functools.partial(_in_proj_body, tm=tm),
        grid=(B, nt),
        in_specs=[row(D), full(g2), full(w), full(wg), full(bg)],
        out_specs=[res(d) for d in DILATIONS] + [row(GLA_COLS), row(2 * GLA_KEY_W)],
        out_shape=[res_shape(d) for d in DILATIONS]
                  + [jax.ShapeDtypeStruct((T, GLA_COLS), BF16), jax.ShapeDtypeStruct((T, 2 * GLA_KEY_W), F32)],
        scratch_shapes=[pltpu.VMEM((ATTN_COLS // LANES, tm, LANES), F32)],
        compiler_params=pltpu.CompilerParams(dimension_semantics=("arbitrary", "arbitrary"),
                                             vmem_limit_bytes=56 * 2**20),
        name="in_proj",
    )(x2, g2, w, wg, bg)


def _t5_bucket(rel):
    nb = T5_BUCKETS // 2
    max_exact = nb // 2
    n = np.abs(rel)
    large = max_exact + (np.log(np.maximum(n, 1).astype(np.float32) / max_exact)
                         / np.log(T5_MAX_DISTANCE / max_exact) * (nb - max_exact)).astype(np.int32)
    large = np.minimum(large, nb - 1)
    return (np.where(rel > 0, nb, 0) + np.where(n < max_exact, n, large)).astype(np.int32)


def _bucket_tables(dil):
    s = np.arange(ATTN_QBLK)[:, None]
    t = np.arange(ATTN_KWIN)[None, :]
    tabs = []
    for c in (0, HALF_SPAN, 2 * HALF_SPAN):
        off = t - c - s
        tabs.append(np.where(np.abs(off) <= HALF_SPAN, _t5_bucket(off * dil), -1))
    return np.stack(tabs).astype(np.int32)


def _attn_body(rb_ref, idx_ref, q_ref, k_ref, v_ref, o_ref, lse_ref, tab_ref, *, L, TQ):
    first = (pl.program_id(0) == 0) & (pl.program_id(1) == 0) & (pl.program_id(2) == 0)

    @pl.when(first)
    def _():
        for v in range(3):
            idx = idx_ref[v]
            for h in range(ATTN_HEADS):
                fill = lambda j, t, h=h, idx=idx: jnp.where(idx == j, rb_ref[j, h], t)
                tab_ref[v, h] = lax.fori_loop(0, T5_BUCKETS, fill, jnp.where(idx < 0, NEG_INF, 0.0).astype(F32))

    i = pl.program_id(2)
    lo = lax.broadcasted_iota(I32, (ATTN_QBLK, LANES), 1) < HEAD_DIM
    scale = HEAD_DIM ** -0.5
    for sb in range(TQ // ATTN_QBLK):
        l0 = i * TQ + sb * ATTN_QBLK
        ws = pl.multiple_of(jnp.clip(l0 - HALF_SPAN, 0, L - ATTN_KWIN), HALF_SPAN)
        var = jnp.where(l0 == 0, 0, jnp.where(l0 == L - ATTN_QBLK, 2, 1))
        rows = slice(sb * ATTN_QBLK, (sb + 1) * ATTN_QBLK)
        for hp in range(ATTN_HEADS // 2):
            cols = slice(hp * LANES, (hp + 1) * LANES)
            q2 = q_ref[rows, cols]
            k2 = k_ref[pl.ds(ws, ATTN_KWIN), cols]
            v2 = v_ref[pl.ds(ws, ATTN_KWIN), cols]
            zero = jnp.zeros_like(q2)
            qs = jnp.concatenate([jnp.where(lo, q2, zero), jnp.where(lo, zero, q2)], axis=0)
            s = lax.dot_general(qs, k2, NT_DIMS, preferred_element_type=F32)
            bias = jnp.concatenate([tab_ref[var, 2 * hp], tab_ref[var, 2 * hp + 1]], axis=0)
            logits = s * scale + bias
            m = jnp.max(logits, axis=-1, keepdims=True)
            p = jnp.exp(logits - m)
            den = jnp.sum(p, axis=-1, keepdims=True)
            pv = jnp.dot(p.astype(BF16), v2, preferred_element_type=F32) / den
            lse = jnp.broadcast_to(m + jnp.log(den), (2 * ATTN_QBLK, LANES))
            o_ref[rows, cols] = jnp.where(lo, pv[:ATTN_QBLK], pv[ATTN_QBLK:]).astype(BF16)
            lse_ref[rows, cols] = jnp.where(lo, lse[:ATTN_QBLK], lse[ATTN_QBLK:])


def _attn_branch(attn, rel_bias, dil):
    B, _, L, _ = attn.shape
    TQ = 256
    idx = jnp.asarray(_bucket_tables(dil))
    body = functools.partial(_attn_body, L=L, TQ=TQ)
    kv = lambda j: pl.BlockSpec((None, None, L, ATTN_W), lambda b, r, i: (b, r, 0, j))
    q_spec = pl.BlockSpec((None, None, TQ, ATTN_W), lambda b, r, i: (b, r, i, 0))
    return pl.pallas_call(
        body,
        grid=(B, dil, L // TQ),
        in_specs=[pl.BlockSpec(memory_space=pltpu.SMEM),
                  pl.BlockSpec(idx.shape, lambda b, r, i: (0, 0, 0)),
                  q_spec, kv(1), kv(2)],
        out_specs=[q_spec, q_spec],
        out_shape=[jax.ShapeDtypeStruct((B, dil, L, ATTN_W), BF16),
                   jax.ShapeDtypeStruct((B, dil, L, ATTN_W), F32)],
        scratch_shapes=[pltpu.VMEM((3, ATTN_HEADS, ATTN_QBLK, ATTN_KWIN), F32)],
        compiler_params=pltpu.CompilerParams(dimension_semantics=("arbitrary",) * 3,
                                             vmem_limit_bytes=48 * 2**20),
        name=f"attn_d{dil}",
    )(rel_bias, idx, attn, attn, attn)


def _gla_body(q_ref, k_ref, v_ref, gate_ref, bf_ref, bb_ref, ng_ref, out_ref, of_ref, ob_ref, st_ref, *, S):
    C = GLA_CHUNK
    nc = S // C
    r = lax.broadcasted_iota(I32, (C, LANES), 0)
    s2 = lax.broadcasted_iota(I32, (C, LANES), 1) % HEAD_DIM
    lo = lax.broadcasted_iota(I32, (C, LANES), 1) < HEAD_DIM
    blockdiag = ((lax.broadcasted_iota(I32, (2 * GLA_DV, LANES), 0) < GLA_DV)
                 == (lax.broadcasted_iota(I32, (2 * GLA_DV, LANES), 1) < HEAD_DIM))
    st_ref[...] = jnp.zeros_like(st_ref)

    def one_chunk(ci, direction):
        mask = (r >= s2) if direction == 0 else (r <= s2)
        b_ref, o_ref = (bf_ref, of_ref) if direction == 0 else (bb_ref, ob_ref)
        rows = pl.ds(pl.multiple_of(ci * C, C), C)
        q = q_ref[0, rows, :].astype(F32) * (HEAD_DIM ** -0.5)
        k = k_ref[0, rows, :].astype(F32)
        v = v_ref[0, rows, :]
        b = b_ref[0, rows, :]
        g_tot = b[C - 1:C, :] if direction == 0 else b[0:1, :]
        qd = (q * jnp.exp(b)).astype(BF16)
        kd = (k * jnp.exp(-b)).astype(BF16)
        kst = (k * jnp.exp(g_tot - b)).astype(BF16)
        zk = jnp.zeros_like(kd)
        ks = jnp.concatenate([jnp.where(lo, kd, zk), jnp.where(lo, zk, kd)], axis=0)
        a = lax.dot_general(qd, ks, NT_DIMS, preferred_element_type=F32)
        a = jnp.where(mask, a, 0.0).astype(BF16)
        zv = jnp.zeros((C, GLA_DV), BF16)
        vblk = jnp.concatenate([jnp.concatenate([v[:, :GLA_DV], zv], axis=1),
                                jnp.concatenate([zv, v[:, GLA_DV:]], axis=1)], axis=0)
        st = st_ref[direction]
        o_ref[rows, :] = (jnp.dot(a, vblk, preferred_element_type=F32)
                          + lax.dot_general(qd, st.astype(BF16), NT_DIMS, preferred_element_type=F32))
        d_st = jnp.dot(v.astype(F32).T.astype(BF16), kst, preferred_element_type=F32)
        st_ref[direction] = st * jnp.exp(g_tot) + jnp.where(blockdiag, d_st, 0.0)

    def step(i, carry):
        one_chunk(i, 0)
        one_chunk(nc - 1 - i, 1)
        return carry

    lax.fori_loop(0, nc, step, 0, unroll=2)

    tr = 256
    def epilogue(i, carry):
        rows = pl.ds(pl.multiple_of(i * tr, tr), tr)
        for hh in range(2):
            cols = slice(hh * GLA_DV, (hh + 1) * GLA_DV)
            o = of_ref[rows, cols] + ob_ref[rows, cols]
            o = o * lax.rsqrt(jnp.mean(o * o, axis=-1, keepdims=True) + NORM_EPS) * ng_ref[0, :, cols]
            gate = gate_ref[0, rows, cols].astype(F32)
            out_ref[0, rows, cols] = (o * (gate * _sigmoid(gate))).astype(BF16)
        return carry

    lax.fori_loop(0, S // tr, epilogue, 0)


def _gla(gla, bcum, norm_g, B, S):
    g3 = gla.reshape(B, S, GLA_COLS)
    b3 = bcum.reshape(B, S, 2 * GLA_KEY_W)
    ng = norm_g.reshape(2, 1, 2 * GLA_DV)
    body = functools.partial(_gla_body, S=S)
    spec = lambda w, off: pl.BlockSpec((1, S, w), lambda b, p: (b, 0, off + p))
    out = pl.pallas_call(
        body,
        grid=(B, 2),
        in_specs=[spec(LANES, 0),
                  spec(LANES, GLA_KEY_W // LANES),
                  spec(2 * GLA_DV, (2 * GLA_KEY_W) // (2 * GLA_DV)),
                  spec(2 * GLA_DV, (2 * GLA_KEY_W + GLA_VAL_W) // (2 * GLA_DV)),
                  spec(LANES, 0),
                  spec(LANES, GLA_KEY_W // LANES),
                  pl.BlockSpec((1, 1, 2 * GLA_DV), lambda b, p: (p, 0, 0))],
        out_specs=pl.BlockSpec((1, S, 2 * GLA_DV), lambda b, p: (b, 0, p)),
        out_shape=jax.ShapeDtypeStruct((B, S, GLA_VAL_W), BF16),
        scratch_shapes=[pltpu.VMEM((S, 2 * GLA_DV), F32), pltpu.VMEM((S, 2 * GLA_DV), F32),
                        pltpu.VMEM((2, 2 * GLA_DV, LANES), F32)],
        compiler_params=pltpu.CompilerParams(dimension_semantics=("arbitrary", "arbitrary"),
                                             vmem_limit_bytes=56 * 2**20),
        name="gla",
    )(g3, g3, g3, g3, b3, b3, ng)
    return out.reshape(B * S, GLA_VAL_W)


def _natural_rows(blk_ref, u_ref, dil, tm):
    if dil == 1:
        return blk_ref[0, 0].astype(F32)
    for r in range(dil):
        for c in range(ATTN_W // LANES):
            u_ref[c, pl.ds(r, tm // dil, stride=dil), :] = blk_ref[0, r, :, c * LANES:(c + 1) * LANES].astype(F32)
    return jnp.concatenate([u_ref[c] for c in range(ATTN_W // LANES)], axis=1)


def _mix_body(o1, o2, o3, l1, l2, l3, g_ref, x_ref, wo_ref, n2_ref, rwh_ref, rwl_ref, rb_ref,
              x1_ref, h_ref, e_ref, rk_ref, gt_ref, cnt_ref, base_ref, u_ref, *, tm):
    @pl.when((pl.program_id(0) == 0) & (pl.program_id(1) == 0))
    def _():
        base_ref[...] = jnp.zeros_like(base_ref)

    ls = [_natural_rows(l, u_ref, d, tm) for l, d in zip((l1, l2, l3), DILATIONS)]
    m = jnp.maximum(jnp.maximum(ls[0], ls[1]), ls[2])
    es = [jnp.exp(l - m) for l in ls]
    a = es[0] * _natural_rows(o1, u_ref, DILATIONS[0], tm)
    a = a + es[1] * _natural_rows(o2, u_ref, DILATIONS[1], tm)
    a = a + es[2] * _natural_rows(o3, u_ref, DILATIONS[2], tm)
    a = a / (es[0] + es[1] + es[2])
    mix = (jnp.dot(a.astype(BF16), wo_ref[:ATTN_W, :], preferred_element_type=F32)
           + jnp.dot(g_ref[...], wo_ref[ATTN_W:, :], preferred_element_type=F32))
    x1 = x_ref[...] + mix
    x1_ref[...] = x1
    h = x1 * lax.rsqrt(jnp.mean(x1 * x1, axis=-1, keepdims=True) + NORM_EPS) * n2_ref[...]
    h_ref[...] = h

    hh, hl, _ = _split3(h)
    dg = lambda u, v: lax.dot_general(u, v, NT_DIMS, preferred_element_type=F32)
    logits = dg(rwh_ref[...], hh) + dg(rwl_ref[...], hh) + dg(rwh_ref[...], hl) + rb_ref[...]
    ie = lax.broadcasted_iota(I32, (N_EXPERTS, tm), 0)
    cur = logits
    vals, idxs = [], []
    for _ in range(TOP_K):
        mx = jnp.max(cur, axis=0, keepdims=True)
        ix = jnp.min(jnp.where(cur == mx, ie, N_EXPERTS), axis=0, keepdims=True)
        vals.append(mx)
        idxs.append(ix)
        cur = jnp.where(ie == ix, -jnp.inf, cur)
    ex = [jnp.exp(v - vals[0]) for v in vals]
    den = ex[0] + ex[1] + ex[2] + ex[3]
    gates = [e / den for e in ex]

    onehots = [ie == ix for ix in idxs]
    chosen = (onehots[0] | onehots[1] | onehots[2] | onehots[3]).astype(F32)
    tr = lax.broadcasted_iota(I32, (tm, tm), 0)
    tc = lax.broadcasted_iota(I32, (tm, tm), 1)
    before = jnp.dot(chosen.astype(BF16), (tr < tc).astype(BF16), preferred_element_type=F32)
    tot = before + base_ref[:, 0:1]
    ranks = [jnp.sum(jnp.where(oh, tot, 0.0), axis=0, keepdims=True) for oh in onehots]
    new_base = base_ref[...] + jnp.sum(chosen, axis=1, keepdims=True)
    base_ref[...] = new_base
    cnt_ref[...] = new_base

    sub = lax.broadcasted_iota(I32, (8, tm), 0)
    pack = lambda rows, zero: functools.reduce(
        lambda acc, kr: jnp.where(sub == kr[0], kr[1], acc), enumerate(rows), jnp.full((8, tm), zero, rows[0].dtype))
    e_ref[...] = pack(idxs, 0)
    rk_ref[...] = pack([rk.astype(I32) for rk in ranks], 0)
    g8 = pack(gates, 0.0)
    gt_ref[...] = jnp.concatenate([g8, jnp.zeros((LANES - 8, tm), F32)], axis=0).T


def _mix(outs, lses, g, x2, B, S, w_out, norm2_g, router_w, router_b):
    T, D = x2.shape
    tm = 512
    nt = S // tm
    wo = w_out.astype(BF16)
    rwt = router_w.T
    rwh = rwt.astype(BF16)
    rwl = (rwt - rwh.astype(F32)).astype(BF16)
    rb = router_b.reshape(N_EXPERTS, 1)
    n2 = norm2_g.reshape(1, D)
    row = lambda c: pl.BlockSpec((tm, c), lambda b, i: (b * nt + i, 0))
    full = lambda a: pl.BlockSpec(a.shape, lambda b, i: (0,) * a.ndim)
    col = pl.BlockSpec((8, tm), lambda b, i: (0, b * nt + i))
    res = [pl.BlockSpec((1, d, tm // d, ATTN_W), lambda b, i: (b, 0, i, 0)) for d in DILATIONS]
    return pl.pallas_call(
        functools.partial(_mix_body, tm=tm),
        grid=(B, nt),
        in_specs=res + res + [row(GLA_VAL_W), row(D), full(wo), full(n2), full(rwh), full(rwl), full(rb)],
        out_specs=[row(D), row(D), col, col, row(LANES),
                   pl.BlockSpec((N_EXPERTS, LANES), lambda b, i: (0, 0))],
        out_shape=[jax.ShapeDtypeStruct((T, D), F32),
                   jax.ShapeDtypeStruct((T, D), F32),
                   jax.ShapeDtypeStruct((8, T), I32),
                   jax.ShapeDtypeStruct((8, T), I32),
                   jax.ShapeDtypeStruct((T, LANES), F32),
                   jax.ShapeDtypeStruct((N_EXPERTS, LANES), F32)],
        scratch_shapes=[pltpu.VMEM((N_EXPERTS, LANES), F32), pltpu.VMEM((ATTN_W // LANES, tm, LANES), F32)],
        compiler_params=pltpu.CompilerParams(dimension_semantics=("arbitrary", "arbitrary"),
                                             vmem_limit_bytes=48 * 2**20),
        name="mix_router",
    )(*outs, *lses, g, x2, wo, n2, rwh, rwl, rb)


def _plan_body(cnt_ref, e_ref, rk_ref, dest_ref, be_ref, nu_ref, *, nblk):
    shift = MOE_BLOCK.bit_length() - 1
    e = e_ref[...]
    dest = rk_ref[...]
    start = jnp.int32(0)
    for ex in range(N_EXPERTS):
        dest = dest + jnp.where(e == ex, start, 0)
        nb = (cnt_ref[ex] + (MOE_BLOCK - 1)) >> shift
        b0 = start >> shift

        def fill(j, carry, b0=b0, ex=ex):
            be_ref[b0 + j] = jnp.int32(ex)
            return carry

        lax.fori_loop(0, nb, fill, 0)
        start = start + (nb << shift)
    dest_ref[...] = dest
    used = start >> shift

    def tail(j, carry):
        be_ref[j] = jnp.int32(N_EXPERTS - 1)
        return carry

    lax.fori_loop(used, nblk, tail, 0)
    nu_ref[0] = used


def _plan(counts, e_t, rk_t, nblk):
    smem = pl.BlockSpec(memory_space=pltpu.SMEM)
    vmem = pl.BlockSpec(memory_space=pltpu.VMEM)
    return pl.pallas_call(
        functools.partial(_plan_body, nblk=nblk),
        in_specs=[smem, vmem, vmem],
        out_specs=[vmem, smem, smem],
        out_shape=[jax.ShapeDtypeStruct(e_t.shape, I32),
                   jax.ShapeDtypeStruct((nblk,), I32),
                   jax.ShapeDtypeStruct((1,), I32)],
        name="plan",
    )(counts, e_t, rk_t)


def _dispatch_body(dest_ref, h_ref, xin_in_ref, xin_ref, sem, *, tmd):
    del xin_in_ref

    def issue(t, carry):
        for k in range(TOP_K):
            pltpu.make_async_copy(h_ref.at[pl.ds(t, 1)], xin_ref.at[pl.ds(dest_ref[k, t], 1)], sem).start(priority=k % 2)
        return carry

    lax.fori_loop(0, tmd, issue, 0, unroll=8)
    for k in range(TOP_K):
        pltpu.make_async_copy(h_ref, xin_ref.at[pl.ds(0, tmd)], sem).wait()


def _dispatch(dest, h, n_rows):
    T = h.shape[0]
    tmd = 512
    xin0 = jnp.zeros((n_rows, h.shape[1]), F32)
    return pl.pallas_call(
        functools.partial(_dispatch_body, tmd=tmd),
        grid=(T // tmd,),
        in_specs=[pl.BlockSpec((8, tmd), lambda i: (0, i), memory_space=pltpu.SMEM),
                  pl.BlockSpec((tmd, h.shape[1]), lambda i: (i, 0)),
                  pl.BlockSpec(memory_space=pl.ANY)],
        out_specs=pl.BlockSpec(memory_space=pl.ANY),
        out_shape=jax.ShapeDtypeStruct(xin0.shape, F32),
        scratch_shapes=[pltpu.SemaphoreType.DMA(())],
        input_output_aliases={2: 0},
        compiler_params=pltpu.CompilerParams(dimension_semantics=("arbitrary",)),
        name="dispatch",
    )(dest, h, xin0)


def _w1_split_body(w_ref, g_ref, l_ref, t_ref, *, tf):
    wt = w_ref[0].T
    for c in range(wt.shape[1] // LANES):
        cols = slice(c * LANES, (c + 1) * LANES)
        t_ref[c] = wt[:, cols]
        g_ref[0, :, cols] = t_ref[c, pl.ds(0, tf // 2, stride=2), :].astype(BF16)
        l_ref[0, :, cols] = t_ref[c, pl.ds(1, tf // 2, stride=2), :].astype(BF16)


def _w1_split(w1):
    E, D, F2 = w1.shape
    tf = 512
    out_spec = pl.BlockSpec((1, tf // 2, D), lambda e, j: (e, j, 0))
    return pl.pallas_call(
        functools.partial(_w1_split_body, tf=tf),
        grid=(E, F2 // tf),
        in_specs=[pl.BlockSpec((1, D, tf), lambda e, j: (e, 0, j))],
        out_specs=[out_spec, out_spec],
        out_shape=[jax.ShapeDtypeStruct((E, F2 // 2, D), BF16)] * 2,
        scratch_shapes=[pltpu.VMEM((D // LANES, tf, LANES), F32)],
        compiler_params=pltpu.CompilerParams(dimension_semantics=("arbitrary", "arbitrary")),
        name="w1_split",
    )(w1)


def _moe_body(be_ref, nu_ref, x_ref, w1g_ref, w1l_ref, b1g_ref, b1l_ref, w2_ref, b2_ref, y_ref):
    del be_ref
    live = pl.program_id(0) < nu_ref[0]

    @pl.when(live)
    def _():
        x = x_ref[...].astype(BF16)
        hg = lax.dot_general(x, w1g_ref[0], NT_DIMS, preferred_element_type=F32) + b1g_ref[0]
        hl = lax.dot_general(x, w1l_ref[0], NT_DIMS, preferred_element_type=F32) + b1l_ref[0]
        xg = jnp.minimum(hg, SWIGLU_LIMIT)
        xl = jnp.clip(hl, -SWIGLU_LIMIT, SWIGLU_LIMIT)
        act = xg * _sigmoid(SWIGLU_ALPHA * xg) * (xl + 1.0)
        y_ref[...] = jnp.dot(act.astype(BF16), w2_ref[0], preferred_element_type=F32) + b2_ref[0]

    @pl.when(jnp.logical_not(live))
    def _():
        y_ref[...] = jnp.zeros_like(y_ref)


def _moe(block_e, n_used, xin, w1, b1, w2, b2):
    n_rows = xin.shape[0]
    D = w1.shape[1]
    F = w2.shape[1]
    nblk = n_rows // MOE_BLOCK
    w1g, w1l = _w1_split(w1)
    b1g = b1[:, 0::2].reshape(N_EXPERTS, 1, F)
    b1l = b1[:, 1::2].reshape(N_EXPERTS, 1, F)
    w2b = w2.astype(BF16)
    b2r = b2.reshape(N_EXPERTS, 1, D)
    xspec = pl.BlockSpec((MOE_BLOCK, D), lambda j, be, nu: (j, 0))
    wspec = lambda a: pl.BlockSpec((1,) + a.shape[1:], lambda j, be, nu: (be[j], 0, 0))
    return pl.pallas_call(
        _moe_body,
        grid_spec=pltpu.PrefetchScalarGridSpec(
            num_scalar_prefetch=2,
            grid=(nblk,),
            in_specs=[xspec, wspec(w1g), wspec(w1l), wspec(b1g), wspec(b1l), wspec(w2b), wspec(b2r)],
            out_specs=xspec),
        out_shape=jax.ShapeDtypeStruct(xin.shape, F32),
        compiler_params=pltpu.CompilerParams(dimension_semantics=("arbitrary",),
                                             vmem_limit_bytes=48 * 2**20),
        name="moe",
    )(block_e, n_used, xin, w1g, w1l, b1g, b1l, w2b, b2r)


def _combine_body(dest_ref, y_ref, x1_ref, gt_ref, fg_ref, out_ref, buf_ref, sem, *, tmc):
    def issue(t, carry):
        for k in range(TOP_K):
            pltpu.make_async_copy(y_ref.at[pl.ds(dest_ref[k, t], 1)], buf_ref.at[k, pl.ds(t, 1)], sem).start(priority=k % 2)
        return carry

    lax.fori_loop(0, tmc, issue, 0, unroll=8)
    for k in range(TOP_K):
        pltpu.make_async_copy(y_ref.at[pl.ds(0, tmc)], buf_ref.at[k], sem).wait()

    acc = x1_ref[...]
    g = gt_ref[...]
    for k in range(TOP_K):
        acc = acc + buf_ref[k] * g[:, k:k + 1]
    out_ref[...] = acc * lax.rsqrt(jnp.mean(acc * acc, axis=-1, keepdims=True) + NORM_EPS) * fg_ref[...]


def _combine(dest, y, x1, gates, final_g):
    T, D = x1.shape
    tmc = 256
    fg = final_g.reshape(1, D)
    return pl.pallas_call(
        functools.partial(_combine_body, tmc=tmc),
        grid=(T // tmc,),
        in_specs=[pl.BlockSpec((8, tmc), lambda i: (0, i), memory_space=pltpu.SMEM),
                  pl.BlockSpec(memory_space=pl.ANY),
                  pl.BlockSpec((tmc, D), lambda i: (i, 0)),
                  pl.BlockSpec((tmc, LANES), lambda i: (i, 0)),
                  pl.BlockSpec((1, D), lambda i: (0, 0))],
        out_specs=pl.BlockSpec((tmc, D), lambda i: (i, 0)),
        out_shape=jax.ShapeDtypeStruct((T, D), F32),
        scratch_shapes=[pltpu.VMEM((TOP_K, tmc, D), F32), pltpu.SemaphoreType.DMA(())],
        compiler_params=pltpu.CompilerParams(dimension_semantics=("arbitrary",),
                                             vmem_limit_bytes=48 * 2**20),
        name="combine",
    )(dest, y, x1, gates, fg)


def _mixers(x2, B, S, norm1_g, w_in, rel_bias, wg_f, bg_f, wg_b, bg_b, gla_norm_g):
    a1, a4, a16, gla, bcum = _in_proj(x2, B, S, norm1_g, w_in, wg_f, bg_f, wg_b, bg_b)
    branches = [_attn_branch(a, rel_bias, d) for a, d in zip((a1, a4, a16), DILATIONS)]
    g = _gla(gla, bcum, gla_norm_g, B, S)
    return [o for o, _ in branches], [l for _, l in branches], g


def _moe_layer(h, e_t, rk_t, gates, cnt, x1, w1, b1, w2, b2, final_g):
    T = x1.shape[0]
    n_rows = T * TOP_K + N_EXPERTS * MOE_BLOCK
    counts = cnt[:, 0].astype(I32)
    dest, block_e, n_used = _plan(counts, e_t, rk_t, n_rows // MOE_BLOCK)
    xin = _dispatch(dest, h, n_rows)
    y = _moe(block_e, n_used, xin, w1, b1, w2, b2)
    return _combine(dest, y, x1, gates, final_g)


def kernel(x, norm1_g, w_in, rel_bias, gla_wg_fwd, gla_bg_fwd, gla_wg_bwd, gla_bg_bwd, gla_norm_g, w_out, norm2_g, router_w, router_b, moe_w1, moe_b1, moe_w2, moe_b2, final_g):
    B, S, D = x.shape
    assert w_in.shape[0] == 1, "one layer"
    x2 = x.reshape(B * S, D)
    outs, lses, g = _mixers(x2, B, S, norm1_g[0], w_in[0], rel_bias, gla_wg_fwd[0], gla_bg_fwd[0],
                            gla_wg_bwd[0], gla_bg_bwd[0], gla_norm_g[0])
    x1, h, e_t, rk_t, gates, cnt = _mix(outs, lses, g, x2, B, S, w_out[0], norm2_g[0], router_w[0], router_b[0])
    out = _moe_layer(h, e_t, rk_t, gates, cnt, x1, moe_w1[0], moe_b1[0], moe_w2[0], moe_b2[0], final_g)
    return out.reshape(B, S, D)
```

```python
import functools

import jax
import jax.numpy as jnp
import numpy as np
from jax import lax
from jax.experimental import pallas as pl
from jax.experimental.pallas import tpu as pltpu

F32, BF16, I32 = jnp.float32, jnp.bfloat16, jnp.int32

LANES = 128
HEAD_DIM = 64
ATTN_HEADS = 8
ATTN_W = ATTN_HEADS * HEAD_DIM
HALF_SPAN = 64
ATTN_QBLK = 128
ATTN_KWIN = 256
DILATIONS = (1, 4, 16)
T5_BUCKETS = 32
T5_MAX_DISTANCE = 1024
NEG_INF = -1e30
GLA_KEY_W = 256
GLA_VAL_W = 512
GLA_DV = 128
GLA_RANK = 16
GLA_TAU = 16.0
GLA_CHUNK = 64
N_EXPERTS = 32
TOP_K = 4
SWIGLU_LIMIT = 7.0
SWIGLU_ALPHA = 1.702
NORM_EPS = 1e-5
MOE_BLOCK = 256
ROUTE_TILE = 512
GROUP_ROWS = ROUTE_TILE * TOP_K + 256

ATTN_COLS = 3 * ATTN_W
GLA_COLS = 2 * GLA_KEY_W + 2 * GLA_VAL_W
Z_COLS = LANES
PROJ_COLS = ATTN_COLS + GLA_COLS + Z_COLS

NT_DIMS = (((1,), (1,)), ((), ()))


def _sigmoid(x):
    return 1.0 / (1.0 + jnp.exp(-x))


def _split3(a):
    h1 = a.astype(BF16)
    r1 = a - h1.astype(F32)
    h2 = r1.astype(BF16)
    h3 = (r1 - h2.astype(F32)).astype(BF16)
    return h1, h2, h3


def _dot_f32(a, b):
    ah, al, _ = _split3(a)
    bh, bl, _ = _split3(b)
    d = lambda u, v: jnp.dot(u, v, preferred_element_type=F32)
    return d(ah, bh) + d(al, bh) + d(ah, bl)


def _in_proj_body(x_ref, g_ref, w_ref, wg_ref, bg_ref, a1_ref, a4_ref, a16_ref, gla_ref, b_ref, p_ref, *, tm):
    x = x_ref[...]
    n = x * lax.rsqrt(jnp.mean(x * x, axis=-1, keepdims=True) + NORM_EPS) * g_ref[...]
    p = jnp.dot(n.astype(BF16), w_ref[...], preferred_element_type=F32)
    gla_ref[...] = p[:, ATTN_COLS:ATTN_COLS + GLA_COLS].astype(BF16)

    a1_ref[0, 0] = p[:, :ATTN_COLS].astype(BF16)
    for c in range(ATTN_COLS // LANES):
        p_ref[c] = p[:, c * LANES:(c + 1) * LANES]
    for dil, a_ref in ((DILATIONS[1], a4_ref), (DILATIONS[2], a16_ref)):
        for r in range(dil):
            for c in range(ATTN_COLS // LANES):
                a_ref[0, r, :, c * LANES:(c + 1) * LANES] = p_ref[c, pl.ds(r, tm // dil, stride=dil), :].astype(BF16)

    z = p[:, ATTN_COLS + GLA_COLS:]
    zz = _dot_f32(z, wg_ref[...]) + bg_ref[...]
    la = (jnp.minimum(zz, 0.0) - jnp.log1p(jnp.exp(-jnp.abs(zz)))) * (1.0 / GLA_TAU)
    C = GLA_CHUNK
    r_i = lax.broadcasted_iota(I32, (C, C), 0)
    c_i = lax.broadcasted_iota(I32, (C, C), 1)
    lower = (r_i >= c_i).astype(BF16)
    upper = (r_i <= c_i).astype(BF16)
    W = GLA_KEY_W
    for j in range(tm // C):
        rows = slice(j * C, (j + 1) * C)
        for tri, cols in ((lower, slice(0, W)), (upper, slice(W, 2 * W))):
            terms = jnp.concatenate(_split3(la[rows, cols]), axis=1)
            cs = jnp.dot(tri, terms, preferred_element_type=F32)
            b_ref[rows, cols] = cs[:, :W] + cs[:, W:2 * W] + cs[:, 2 * W:]


def _in_proj(x2, B, S, norm_g, w_in, wg_f, bg_f, wg_b, bg_b):
    T, D = x2.shape
    tm = 512
    nt = S // tm
    w = jnp.pad(w_in.astype(BF16), ((0, 0), (0, PROJ_COLS - w_in.shape[1])))
    wg = jnp.zeros((Z_COLS, 2 * GLA_KEY_W), F32)
    wg = wg.at[:GLA_RANK, :GLA_KEY_W].set(wg_f).at[GLA_RANK:2 * GLA_RANK, GLA_KEY_W:].set(wg_b)
    bg = jnp.concatenate([bg_f, bg_b]).reshape(1, 2 * GLA_KEY_W)
    row = lambda c: pl.BlockSpec((tm, c), lambda b, i: (b * nt + i, 0))
    full = lambda a: pl.BlockSpec(a.shape, lambda b, i: (0,) * a.ndim)
    res = lambda d: pl.BlockSpec((1, d, tm // d, ATTN_COLS), lambda b, i: (b, 0, i, 0))
    res_shape = lambda d: jax.ShapeDtypeStruct((B, d, S // d, ATTN_COLS), BF16)
    g2 = norm_g.reshape(1, D)
    return pl.pallas_call(
        functools.partial(_in_proj_body, tm=tm),
        grid=(B, nt),
        in_specs=[row(D), full(g2), full(w), full(wg), full(bg)],
        out_specs=[res(d) for d in DILATIONS] + [row(GLA_COLS), row(2 * GLA_KEY_W)],
        out_shape=[res_shape(d) for d in DILATIONS]
                  + [jax.ShapeDtypeStruct((T, GLA_COLS), BF16), jax.ShapeDtypeStruct((T, 2 * GLA_KEY_W), F32)],
        scratch_shapes=[pltpu.VMEM((ATTN_COLS // LANES, tm, LANES), F32)],
        compiler_params=pltpu.CompilerParams(dimension_semantics=("arbitrary", "arbitrary"),
                                             vmem_limit_bytes=56 * 2**20),
        name="in_proj",
    )(x2, g2, w, wg, bg)


def _t5_bucket(rel):
    nb = T5_BUCKETS // 2
    max_exact = nb // 2
    n = np.abs(rel)
    large = max_exact + (np.log(np.maximum(n, 1).astype(np.float32) / max_exact)
                         / np.log(T5_MAX_DISTANCE / max_exact) * (nb - max_exact)).astype(np.int32)
    large = np.minimum(large, nb - 1)
    return (np.where(rel > 0, nb, 0) + np.where(n < max_exact, n, large)).astype(np.int32)


def _bucket_tables(dil):
    s = np.arange(ATTN_QBLK)[:, None]
    t = np.arange(ATTN_KWIN)[None, :]
    tabs = []
    for c in (0, HALF_SPAN, 2 * HALF_SPAN):
        off = t - c - s
        tabs.append(np.where(np.abs(off) <= HALF_SPAN, _t5_bucket(off * dil), -1))
    return np.stack(tabs).astype(np.int32)


def _attn_body(rb_ref, idx_ref, q_ref, k_ref, v_ref, o_ref, lse_ref, tab_ref, *, L, TQ):
    first = (pl.program_id(0) == 0) & (pl.program_id(1) == 0) & (pl.program_id(2) == 0)

    @pl.when(first)
    def _():
        for v in range(3):
            idx = idx_ref[v]
            for h in range(ATTN_HEADS):
                fill = lambda j, t, h=h, idx=idx: jnp.where(idx == j, rb_ref[j, h], t)
                tab_ref[v, h] = lax.fori_loop(0, T5_BUCKETS, fill, jnp.where(idx < 0, NEG_INF, 0.0).astype(F32))

    i = pl.program_id(2)
    lo = lax.broadcasted_iota(I32, (ATTN_QBLK, LANES), 1) < HEAD_DIM
    scale = HEAD_DIM ** -0.5
    for sb in range(TQ // ATTN_QBLK):
        l0 = i * TQ + sb * ATTN_QBLK
        ws = pl.multiple_of(jnp.clip(l0 - HALF_SPAN, 0, L - ATTN_KWIN), HALF_SPAN)
        var = jnp.where(l0 == 0, 0, jnp.where(l0 == L - ATTN_QBLK, 2, 1))
        rows = slice(sb * ATTN_QBLK, (sb + 1) * ATTN_QBLK)
        for hp in range(ATTN_HEADS // 2):
            cols = slice(hp * LANES, (hp + 1) * LANES)
            q2 = q_ref[rows, cols]
            k2 = k_ref[pl.ds(ws, ATTN_KWIN), cols]
            v2 = v_ref[pl.ds(ws, ATTN_KWIN), cols]
            zero = jnp.zeros_like(q2)
            qs = jnp.concatenate([jnp.where(lo, q2, zero), jnp.where(lo, zero, q2)], axis=0)
            s = lax.dot_general(qs, k2, NT_DIMS, preferred_element_type=F32)
            bias = jnp.concatenate([tab_ref[var, 2 * hp], tab_ref[var, 2 * hp + 1]], axis=0)
            logits = s * scale + bias
            m = jnp.max(logits, axis=-1, keepdims=True)
            p = jnp.exp(logits - m)
            den = jnp.sum(p, axis=-1, keepdims=True)
            pv = jnp.dot(p.astype(BF16), v2, preferred_element_type=F32) / den
            lse = jnp.broadcast_to(m + jnp.log(den), (2 * ATTN_QBLK, LANES))
            o_ref[rows, cols] = jnp.where(lo, pv[:ATTN_QBLK], pv[ATTN_QBLK:]).astype(BF16)
            lse_ref[rows, cols] = jnp.where(lo, lse[:ATTN_QBLK], lse[ATTN_QBLK:])


def _attn_branch(attn, rel_bias, dil):
    B, _, L, _ = attn.shape
    TQ = 256
    idx = jnp.asarray(_bucket_tables(dil))
    body = functools.partial(_attn_body, L=L, TQ=TQ)
    kv = lambda j: pl.BlockSpec((None, None, L, ATTN_W), lambda b, r, i: (b, r, 0, j))
    q_spec = pl.BlockSpec((None, None, TQ, ATTN_W), lambda b, r, i: (b, r, i, 0))
    return pl.pallas_call(
        body,
        grid=(B, dil, L // TQ),
        in_specs=[pl.BlockSpec(memory_space=pltpu.SMEM),
                  pl.BlockSpec(idx.shape, lambda b, r, i: (0, 0, 0)),
                  q_spec, kv(1), kv(2)],
        out_specs=[q_spec, q_spec],
        out_shape=[jax.ShapeDtypeStruct((B, dil, L, ATTN_W), BF16),
                   jax.ShapeDtypeStruct((B, dil, L, ATTN_W), F32)],
        scratch_shapes=[pltpu.VMEM((3, ATTN_HEADS, ATTN_QBLK, ATTN_KWIN), F32)],
        compiler_params=pltpu.CompilerParams(dimension_semantics=("arbitrary",) * 3,
                                             vmem_limit_bytes=48 * 2**20),
        name=f"attn_d{dil}",
    )(rel_bias, idx, attn, attn, attn)


def _gla_body(q_ref, k_ref, v_ref, gate_ref, bf_ref, bb_ref, ng_ref, out_ref, of_ref, ob_ref, st_ref, *, S):
    C = GLA_CHUNK
    nc = S // C
    r = lax.broadcasted_iota(I32, (C, LANES), 0)
    s2 = lax.broadcasted_iota(I32, (C, LANES), 1) % HEAD_DIM
    lo = lax.broadcasted_iota(I32, (C, LANES), 1) < HEAD_DIM
    blockdiag = ((lax.broadcasted_iota(I32, (2 * GLA_DV, LANES), 0) < GLA_DV)
                 == (lax.broadcasted_iota(I32, (2 * GLA_DV, LANES), 1) < HEAD_DIM))
    st_ref[...] = jnp.zeros_like(st_ref)

    def one_chunk(ci, direction):
        mask = (r >= s2) if direction == 0 else (r <= s2)
        b_ref, o_ref = (bf_ref, of_ref) if direction == 0 else (bb_ref, ob_ref)
        rows = pl.ds(pl.multiple_of(ci * C, C), C)
        q = q_ref[0, rows, :].astype(F32) * (HEAD_DIM ** -0.5)
        k = k_ref[0, rows, :].astype(F32)
        v = v_ref[0, rows, :]
        b = b_ref[0, rows, :]
        g_tot = b[C - 1:C, :] if direction == 0 else b[0:1, :]
        qd = (q * jnp.exp(b)).astype(BF16)
        kd = (k * jnp.exp(-b)).astype(BF16)
        kst = (k * jnp.exp(g_tot - b)).astype(BF16)
        zk = jnp.zeros_like(kd)
        ks = jnp.concatenate([jnp.where(lo, kd, zk), jnp.where(lo, zk, kd)], axis=0)
        a = lax.dot_general(qd, ks, NT_DIMS, preferred_element_type=F32)
        a = jnp.where(mask, a, 0.0).astype(BF16)
        zv = jnp.zeros((C, GLA_DV), BF16)
        vblk = jnp.concatenate([jnp.concatenate([v[:, :GLA_DV], zv], axis=1),
                                jnp.concatenate([zv, v[:, GLA_DV:]], axis=1)], axis=0)
        st = st_ref[direction]
        o_ref[rows, :] = (jnp.dot(a, vblk, preferred_element_type=F32)
                          + lax.dot_general(qd, st.astype(BF16), NT_DIMS, preferred_element_type=F32))
        d_st = jnp.dot(v.astype(F32).T.astype(BF16), kst, preferred_element_type=F32)
        st_ref[direction] = st * jnp.exp(g_tot) + jnp.where(blockdiag, d_st, 0.0)

    def step(i, carry):
        one_chunk(i, 0)
        one_chunk(nc - 1 - i, 1)
        return carry

    lax.fori_loop(0, nc, step, 0, unroll=2)

    tr = 256
    def epilogue(i, carry):
        rows = pl.ds(pl.multiple_of(i * tr, tr), tr)
        for hh in range(2):
            cols = slice(hh * GLA_DV, (hh + 1) * GLA_DV)
            o = of_ref[rows, cols] + ob_ref[rows, cols]
            o = o * lax.rsqrt(jnp.mean(o * o, axis=-1, keepdims=True) + NORM_EPS) * ng_ref[0, :, cols]
            gate = gate_ref[0, rows, cols].astype(F32)
            out_ref[0, rows, cols] = (o * (gate * _sigmoid(gate))).astype(BF16)
        return carry

    lax.fori_loop(0, S // tr, epilogue, 0)


def _gla(gla, bcum, norm_g, B, S):
    g3 = gla.reshape(B, S, GLA_COLS)
    b3 = bcum.reshape(B, S, 2 * GLA_KEY_W)
    ng = norm_g.reshape(2, 1, 2 * GLA_DV)
    body = functools.partial(_gla_body, S=S)
    spec = lambda w, off: pl.BlockSpec((1, S, w), lambda b, p: (b, 0, off + p))
    out = pl.pallas_call(
        body,
        grid=(B, 2),
        in_specs=[spec(LANES, 0),
                  spec(LANES, GLA_KEY_W // LANES),
                  spec(2 * GLA_DV, (2 * GLA_KEY_W) // (2 * GLA_DV)),
                  spec(2 * GLA_DV, (2 * GLA_KEY_W + GLA_VAL_W) // (2 * GLA_DV)),
                  spec(LANES, 0),
                  spec(LANES, GLA_KEY_W // LANES),
                  pl.BlockSpec((1, 1, 2 * GLA_DV), lambda b, p: (p, 0, 0))],
        out_specs=pl.BlockSpec((1, S, 2 * GLA_DV), lambda b, p: (b, 0, p)),
        out_shape=jax.ShapeDtypeStruct((B, S, GLA_VAL_W), BF16),
        scratch_shapes=[pltpu.VMEM((S, 2 * GLA_DV), F32), pltpu.VMEM((S, 2 * GLA_DV), F32),
                        pltpu.VMEM((2, 2 * GLA_DV, LANES), F32)],
        compiler_params=pltpu.CompilerParams(dimension_semantics=("arbitrary", "arbitrary"),
                                             vmem_limit_bytes=56 * 2**20),
        name="gla",
    )(g3, g3, g3, g3, b3, b3, ng)
    return out.reshape(B * S, GLA_VAL_W)


def _natural_rows(blk_ref, u_ref, dil, tm):
    if dil == 1:
        return blk_ref[0, 0].astype(F32)
    for r in range(dil):
        for c in range(ATTN_W // LANES):
            u_ref[c, pl.ds(r, tm // dil, stride=dil), :] = blk_ref[0, r, :, c * LANES:(c + 1) * LANES].astype(F32)
    return jnp.concatenate([u_ref[c] for c in range(ATTN_W // LANES)], axis=1)


def _mix_body(o1, o2, o3, l1, l2, l3, g_ref, x_ref, wo_ref, n2_ref, rwh_ref, rwl_ref, rb_ref,
              x1_ref, h_ref, m_ref, slab_ref, tcnt_ref, u_ref, *, tm):
    ls = [_natural_rows(l, u_ref, d, tm) for l, d in zip((l1, l2, l3), DILATIONS)]
    m = jnp.maximum(jnp.maximum(ls[0], ls[1]), ls[2])
    es = [jnp.exp(l - m) for l in ls]
    a = es[0] * _natural_rows(o1, u_ref, DILATIONS[0], tm)
    a = a + es[1] * _natural_rows(o2, u_ref, DILATIONS[1], tm)
    a = a + es[2] * _natural_rows(o3, u_ref, DILATIONS[2], tm)
    a = a / (es[0] + es[1] + es[2])
    mix = (jnp.dot(a.astype(BF16), wo_ref[:ATTN_W, :], preferred_element_type=F32)
           + jnp.dot(g_ref[...], wo_ref[ATTN_W:, :], preferred_element_type=F32))
    x1 = x_ref[...] + mix
    x1_ref[...] = x1
    h = x1 * lax.rsqrt(jnp.mean(x1 * x1, axis=-1, keepdims=True) + NORM_EPS) * n2_ref[...]
    h_ref[...] = h

    hh, hl, _ = _split3(h)
    dg = lambda u, v: lax.dot_general(u, v, NT_DIMS, preferred_element_type=F32)
    logits = dg(rwh_ref[...], hh) + dg(rwl_ref[...], hh) + dg(rwh_ref[...], hl) + rb_ref[...]
    ie = lax.broadcasted_iota(I32, (N_EXPERTS, tm), 0)
    cur = logits
    vals, idxs = [], []
    for _ in range(TOP_K):
        mx = jnp.max(cur, axis=0, keepdims=True)
        ix = jnp.min(jnp.where(cur == mx, ie, N_EXPERTS), axis=0, keepdims=True)
        vals.append(mx)
        idxs.append(ix)
        cur = jnp.where(ie == ix, -jnp.inf, cur)
    ex = [jnp.exp(v - vals[0]) for v in vals]
    den = ex[0] + ex[1] + ex[2] + ex[3]
    gates = [e / den for e in ex]

    onehots = [ie == ix for ix in idxs]
    chosen = (onehots[0] | onehots[1] | onehots[2] | onehots[3]).astype(F32)
    tr = lax.broadcasted_iota(I32, (tm, tm), 0)
    tc = lax.broadcasted_iota(I32, (tm, tm), 1)
    before = jnp.dot(chosen.astype(BF16), (tr < tc).astype(BF16), preferred_element_type=F32)
    count = jnp.broadcast_to(jnp.sum(chosen, axis=1, keepdims=True), (N_EXPERTS, LANES))
    tcnt_ref[0] = count
    run = jnp.floor((count + 7.0) * 0.125) * 8.0
    er = lax.broadcasted_iota(I32, (N_EXPERTS, N_EXPERTS), 0)
    ec = lax.broadcasted_iota(I32, (N_EXPERTS, N_EXPERTS), 1)
    run_start = jnp.dot((ec < er).astype(BF16), run.astype(BF16), preferred_element_type=F32)
    where_ = before + run_start[:, 0:1]
    locs = [jnp.sum(jnp.where(oh, where_, 0.0), axis=0, keepdims=True) for oh in onehots]

    def pack(rows, n):
        sub = lax.broadcasted_iota(I32, (n, tm), 0)
        out = jnp.zeros((n, tm), rows[0].dtype)
        for k, row in enumerate(rows):
            out = jnp.where(sub == k, row, out)
        return out

    m_ref[...] = pack([l.astype(I32) for l in locs], 8)
    slab = pack(gates + locs, 8)
    slab_ref[...] = jnp.concatenate([slab, jnp.zeros((LANES - 8, tm), F32)], axis=0).T


def _mix(outs, lses, g, x2, B, S, w_out, norm2_g, router_w, router_b):
    T, D = x2.shape
    tm = ROUTE_TILE
    nt = S // tm
    wo = w_out.astype(BF16)
    rwt = router_w.T
    rwh = rwt.astype(BF16)
    rwl = (rwt - rwh.astype(F32)).astype(BF16)
    rb = router_b.reshape(N_EXPERTS, 1)
    n2 = norm2_g.reshape(1, D)
    row = lambda c: pl.BlockSpec((tm, c), lambda b, i: (b * nt + i, 0))
    full = lambda a: pl.BlockSpec(a.shape, lambda b, i: (0,) * a.ndim)
    col = pl.BlockSpec((8, tm), lambda b, i: (0, b * nt + i))
    res = [pl.BlockSpec((1, d, tm // d, ATTN_W), lambda b, i: (b, 0, i, 0)) for d in DILATIONS]
    return pl.pallas_call(
        functools.partial(_mix_body, tm=tm),
        grid=(B, nt),
        in_specs=res + res + [row(GLA_VAL_W), row(D), full(wo), full(n2), full(rwh), full(rwl), full(rb)],
        out_specs=[row(D), row(D), col, row(LANES),
                   pl.BlockSpec((1, N_EXPERTS, LANES), lambda b, i: (b * nt + i, 0, 0))],
        out_shape=[jax.ShapeDtypeStruct((T, D), F32),
                   jax.ShapeDtypeStruct((T, D), F32),
                   jax.ShapeDtypeStruct((8, T), I32),
                   jax.ShapeDtypeStruct((T, LANES), F32),
                   jax.ShapeDtypeStruct((T // tm, N_EXPERTS, LANES), F32)],
        scratch_shapes=[pltpu.VMEM((ATTN_W // LANES, tm, LANES), F32)],
        compiler_params=pltpu.CompilerParams(dimension_semantics=("arbitrary", "arbitrary"),
                                             vmem_limit_bytes=48 * 2**20),
        name="mix_router",
    )(*outs, *lses, g, x2, wo, n2, rwh, rwl, rb)


def _round8(c):
    return ((c + 7) >> 3) << 3


def _plan_body(tc_ref, rs_ref, p0_ref, pn_ref, be_ref, nu_ref, *, nt, nblk):
    shift = MOE_BLOCK.bit_length() - 1
    start = jnp.int32(0)
    for ex in range(N_EXPERTS):
        def run(i, tot, start=start, ex=ex):
            rs_ref[i, ex] = start + tot
            return tot + _round8(tc_ref[i, ex])

        tot = lax.fori_loop(0, nt, run, jnp.int32(0))
        nb = (tot + (MOE_BLOCK - 1)) >> shift
        p0_ref[ex] = start + tot
        pn_ref[ex] = ((nb << shift) - tot) >> 3
        b0 = start >> shift

        def fill(j, carry, b0=b0, ex=ex):
            be_ref[b0 + j] = jnp.int32(ex)
            return carry

        lax.fori_loop(0, nb, fill, 0)
        start = start + (nb << shift)
    used = start >> shift

    def tail(j, carry):
        be_ref[j] = jnp.int32(N_EXPERTS - 1)
        return carry

    lax.fori_loop(used, nblk, tail, 0)
    nu_ref[0] = used


def _plan(tcnt, nblk):
    nt = tcnt.shape[0]
    smem = pl.BlockSpec(memory_space=pltpu.SMEM)
    return pl.pallas_call(
        functools.partial(_plan_body, nt=nt, nblk=nblk),
        in_specs=[smem],
        out_specs=[smem] * 5,
        out_shape=[jax.ShapeDtypeStruct(tcnt.shape, I32),
                   jax.ShapeDtypeStruct((N_EXPERTS,), I32),
                   jax.ShapeDtypeStruct((N_EXPERTS,), I32),
                   jax.ShapeDtypeStruct((nblk,), I32),
                   jax.ShapeDtypeStruct((1,), I32)],
        name="plan",
    )(tcnt)


def _for_each_run(tc_ref, rs_ref, i, fn):
    start = jnp.int32(0)
    for ex in range(N_EXPERTS):
        n8 = (tc_ref[i, ex] + 7) >> 3
        g0 = rs_ref[i, ex]

        def body(q, carry, start=start, g0=g0):
            fn(pl.multiple_of(start + q * 8, 8), pl.multiple_of(g0 + q * 8, 8))
            return carry

        lax.fori_loop(0, n8, body, 0)
        start = start + n8 * 8
    return start >> 3


def _dispatch_body(tc_ref, rs_ref, p0_ref, pn_ref, nu_ref, m_ref, h_ref, xin_ref, g_ref, z_ref, sem, *, tm, R):
    i = pl.program_id(0)
    rows = m_ref[...]
    jj = lax.broadcasted_iota(I32, (R, tm), 0)
    hit = (jj == rows[0:1]) | (jj == rows[1:2]) | (jj == rows[2:3]) | (jj == rows[3:4])
    g_ref[...] = jnp.dot(jnp.where(hit, 1.0, 0.0).astype(BF16), h_ref[...].astype(BF16), preferred_element_type=F32)

    group = lambda ref, r: ref.at[pl.ds(r, 8)]
    n_groups = _for_each_run(tc_ref, rs_ref, i,
                             lambda lr, gr: pltpu.make_async_copy(group(g_ref, lr), group(xin_ref, gr), sem).start())

    def wait_groups(n):
        def body(q, carry):
            pltpu.make_async_copy(group(g_ref, 0), group(xin_ref, 0), sem).wait()
            return carry
        lax.fori_loop(0, n, body, 0)

    wait_groups(n_groups)

    @pl.when(i == 0)
    def _():
        z_ref[...] = jnp.zeros_like(z_ref)
        total = jnp.int32(0)
        for ex in range(N_EXPERTS):
            def body(q, carry, ex=ex):
                pltpu.make_async_copy(group(z_ref, 0), group(xin_ref, pl.multiple_of(p0_ref[ex] + q * 8, 8)), sem).start()
                return carry
            lax.fori_loop(0, pn_ref[ex], body, 0)
            total = total + pn_ref[ex]
        wait_groups(total)

        block = lambda j: xin_ref.at[pl.ds(pl.multiple_of(j * MOE_BLOCK, MOE_BLOCK), MOE_BLOCK)]
        n_blocks = xin_ref.shape[0] // MOE_BLOCK

        def fill(j, carry):
            pltpu.make_async_copy(z_ref, block(j), sem).start()
            return carry

        def fill_wait(j, carry):
            pltpu.make_async_copy(z_ref, block(j), sem).wait()
            return carry

        lax.fori_loop(nu_ref[0], n_blocks, fill, 0)
        lax.fori_loop(nu_ref[0], n_blocks, fill_wait, 0)


def _dispatch(tcnt, rowstart, pad0, padn, n_used, m_t, h, n_rows):
    T, D = h.shape
    tm = ROUTE_TILE
    col = pl.BlockSpec((8, tm), lambda i, *_: (0, i))
    return pl.pallas_call(
        functools.partial(_dispatch_body, tm=tm, R=GROUP_ROWS),
        grid_spec=pltpu.PrefetchScalarGridSpec(
            num_scalar_prefetch=5,
            grid=(T // tm,),
            in_specs=[col, pl.BlockSpec((tm, D), lambda i, *_: (i, 0))],
            out_specs=pl.BlockSpec(memory_space=pl.ANY),
            scratch_shapes=[pltpu.VMEM((GROUP_ROWS, D), F32), pltpu.VMEM((MOE_BLOCK, D), F32),
                            pltpu.SemaphoreType.DMA(())]),
        out_shape=jax.ShapeDtypeStruct((n_rows, D), F32),
        compiler_params=pltpu.CompilerParams(dimension_semantics=("arbitrary",), vmem_limit_bytes=56 * 2**20),
        name="dispatch",
    )(tcnt, rowstart, pad0, padn, n_used, m_t, h)


def _w1_split_body(w_ref, g_ref, l_ref, t_ref, *, tf):
    wt = w_ref[0].T
    for c in range(wt.shape[1] // LANES):
        cols = slice(c * LANES, (c + 1) * LANES)
        t_ref[c] = wt[:, cols]
        g_ref[0, :, cols] = t_ref[c, pl.ds(0, tf // 2, stride=2), :].astype(BF16)
        l_ref[0, :, cols] = t_ref[c, pl.ds(1, tf // 2, stride=2), :].astype(BF16)


def _w1_split(w1):
    E, D, F2 = w1.shape
    tf = 512
    out_spec = pl.BlockSpec((1, tf // 2, D), lambda e, j: (e, j, 0))
    return pl.pallas_call(
        functools.partial(_w1_split_body, tf=tf),
        grid=(E, F2 // tf),
        in_specs=[pl.BlockSpec((1, D, tf), lambda e, j: (e, 0, j))],
        out_specs=[out_spec, out_spec],
        out_shape=[jax.ShapeDtypeStruct((E, F2 // 2, D), BF16)] * 2,
        scratch_shapes=[pltpu.VMEM((D // LANES, tf, LANES), F32)],
        compiler_params=pltpu.CompilerParams(dimension_semantics=("arbitrary", "arbitrary")),
        name="w1_split",
    )(w1)


def _moe_body(be_ref, nu_ref, x_ref, w1g_ref, w1l_ref, b1g_ref, b1l_ref, w2_ref, b2_ref, y_ref):
    del be_ref
    live = pl.program_id(0) < nu_ref[0]

    @pl.when(live)
    def _():
        x = x_ref[...].astype(BF16)
        hg = lax.dot_general(x, w1g_ref[0], NT_DIMS, preferred_element_type=F32) + b1g_ref[0]
        hl = lax.dot_general(x, w1l_ref[0], NT_DIMS, preferred_element_type=F32) + b1l_ref[0]
        xg = jnp.minimum(hg, SWIGLU_LIMIT)
        xl = jnp.clip(hl, -SWIGLU_LIMIT, SWIGLU_LIMIT)
        act = xg * _sigmoid(SWIGLU_ALPHA * xg) * (xl + 1.0)
        y_ref[...] = jnp.dot(act.astype(BF16), w2_ref[0], preferred_element_type=F32) + b2_ref[0]

    @pl.when(jnp.logical_not(live))
    def _():
        y_ref[...] = jnp.zeros_like(y_ref)


def _moe(block_e, n_used, xin, w1, b1, w2, b2):
    n_rows = xin.shape[0]
    D = w1.shape[1]
    F = w2.shape[1]
    nblk = n_rows // MOE_BLOCK
    w1g, w1l = _w1_split(w1)
    b1g = b1[:, 0::2].reshape(N_EXPERTS, 1, F)
    b1l = b1[:, 1::2].reshape(N_EXPERTS, 1, F)
    w2b = w2.astype(BF16)
    b2r = b2.reshape(N_EXPERTS, 1, D)
    xspec = pl.BlockSpec((MOE_BLOCK, D), lambda j, be, nu: (jnp.minimum(j, nu[0] - 1), 0))
    yspec = pl.BlockSpec((MOE_BLOCK, D), lambda j, be, nu: (j, 0))
    wspec = lambda a: pl.BlockSpec((1,) + a.shape[1:], lambda j, be, nu: (be[j], 0, 0))
    return pl.pallas_call(
        _moe_body,
        grid_spec=pltpu.PrefetchScalarGridSpec(
            num_scalar_prefetch=2,
            grid=(nblk,),
            in_specs=[xspec, wspec(w1g), wspec(w1l), wspec(b1g), wspec(b1l), wspec(w2b), wspec(b2r)],
            out_specs=yspec),
        out_shape=jax.ShapeDtypeStruct(xin.shape, F32),
        compiler_params=pltpu.CompilerParams(dimension_semantics=("arbitrary",),
                                             vmem_limit_bytes=48 * 2**20),
        name="moe",
    )(block_e, n_used, xin, w1g, w1l, b1g, b1l, w2b, b2r)


def _combine_body(tc_ref, rs_ref, slab_ref, y_ref, x1_ref, fg_ref, out_ref, g_ref, sem, *, tm, R):
    i = pl.program_id(0)

    @pl.when(i == 0)
    def _():
        g_ref[...] = jnp.zeros_like(g_ref)

    group = lambda ref, r: ref.at[pl.ds(r, 8)]
    n_groups = _for_each_run(tc_ref, rs_ref, i,
                             lambda lr, gr: pltpu.make_async_copy(group(y_ref, gr), group(g_ref, lr), sem).start())

    slab = slab_ref[...]
    gates = slab[:, 0:TOP_K]
    rows = slab[:, TOP_K:2 * TOP_K].astype(I32)
    jj = lax.broadcasted_iota(I32, (tm, R), 1)
    sel = jnp.zeros((tm, R), F32)
    for k in range(TOP_K):
        sel = jnp.where(jj == rows[:, k:k + 1], gates[:, k:k + 1], sel)

    def wait(q, carry):
        pltpu.make_async_copy(group(y_ref, 0), group(g_ref, 0), sem).wait()
        return carry

    lax.fori_loop(0, n_groups, wait, 0)
    acc = x1_ref[...] + jnp.dot(sel.astype(BF16), g_ref[...].astype(BF16), preferred_element_type=F32)
    out_ref[...] = acc * lax.rsqrt(jnp.mean(acc * acc, axis=-1, keepdims=True) + NORM_EPS) * fg_ref[...]


def _combine(tcnt, rowstart, slab, y, x1, final_g):
    T, D = x1.shape
    tm = ROUTE_TILE
    fg = final_g.reshape(1, D)
    return pl.pallas_call(
        functools.partial(_combine_body, tm=tm, R=GROUP_ROWS),
        grid_spec=pltpu.PrefetchScalarGridSpec(
            num_scalar_prefetch=2,
            grid=(T // tm,),
            in_specs=[pl.BlockSpec((tm, LANES), lambda i, *_: (i, 0)),
                      pl.BlockSpec(memory_space=pl.ANY),
                      pl.BlockSpec((tm, D), lambda i, *_: (i, 0)),
                      pl.BlockSpec((1, D), lambda i, *_: (0, 0))],
            out_specs=pl.BlockSpec((tm, D), lambda i, *_: (i, 0)),
            scratch_shapes=[pltpu.VMEM((GROUP_ROWS, D), F32), pltpu.SemaphoreType.DMA(())]),
        out_shape=jax.ShapeDtypeStruct((T, D), F32),
        compiler_params=pltpu.CompilerParams(dimension_semantics=("arbitrary",), vmem_limit_bytes=56 * 2**20),
        name="combine",
    )(tcnt, rowstart, slab, y, x1, fg)


def _mixers(x2, B, S, norm1_g, w_in, rel_bias, wg_f, bg_f, wg_b, bg_b, gla_norm_g):
    a1, a4, a16, gla, bcum = _in_proj(x2, B, S, norm1_g, w_in, wg_f, bg_f, wg_b, bg_b)
    branches = [_attn_branch(a, rel_bias, d) for a, d in zip((a1, a4, a16), DILATIONS)]
    g = _gla(gla, bcum, gla_norm_g, B, S)
    return [o for o, _ in branches], [l for _, l in branches], g


def _moe_layer(h, m_t, slab, tcnt_f, x1, w1, b1, w2, b2, final_g):
    T = x1.shape[0]
    nt = T // ROUTE_TILE
    nblk = (T * TOP_K + nt * N_EXPERTS * 7) // MOE_BLOCK + N_EXPERTS
    tcnt = tcnt_f[:, :, 0].astype(I32)
    rowstart, pad0, padn, block_e, n_used = _plan(tcnt, nblk)
    xin = _dispatch(tcnt, rowstart, pad0, padn, n_used, m_t, h, nblk * MOE_BLOCK)
    y = _moe(block_e, n_used, xin, w1, b1, w2, b2)
    return _combine(tcnt, rowstart, slab, y, x1, final_g)


def kernel(x, norm1_g, w_in, rel_bias, gla_wg_fwd, gla_bg_fwd, gla_wg_bwd, gla_bg_bwd, gla_norm_g, w_out, norm2_g, router_w, router_b, moe_w1, moe_b1, moe_w2, moe_b2, final_g):
    B, S, D = x.shape
    assert w_in.shape[0] == 1, "one layer"
    x2 = x.reshape(B * S, D)
    outs, lses, g = _mixers(x2, B, S, norm1_g[0], w_in[0], rel_bias, gla_wg_fwd[0], gla_bg_fwd[0],
                            gla_wg_bwd[0], gla_bg_bwd[0], gla_norm_g[0])
    x1, h, m_t, slab, tcnt = _mix(outs, lses, g, x2, B, S, w_out[0], norm2_g[0], router_w[0], router_b[0])
    out = _moe_layer(h, m_t, slab, tcnt, x1, moe_w1[0], moe_b1[0], moe_w2[0], moe_b2[0], final_g)
    return out.reshape(B, S, D)
```

```python
import functools

import jax
import jax.numpy as jnp
import numpy as np
from jax import lax
from jax.experimental import pallas as pl
from jax.experimental.pallas import tpu as pltpu

F32, BF16, I32 = jnp.float32, jnp.bfloat16, jnp.int32

LANES = 128
HEAD_DIM = 64
ATTN_HEADS = 8
ATTN_W = ATTN_HEADS * HEAD_DIM
HALF_SPAN = 64
ATTN_QBLK = 128
ATTN_KWIN = 256
DILATIONS = (1, 4, 16)
T5_BUCKETS = 32
T5_MAX_DISTANCE = 1024
NEG_INF = -1e30
GLA_KEY_W = 256
GLA_VAL_W = 512
GLA_DV = 128
GLA_RANK = 16
GLA_TAU = 16.0
GLA_CHUNK = 64
N_EXPERTS = 32
TOP_K = 4
SWIGLU_LIMIT = 7.0
SWIGLU_ALPHA = 1.702
NORM_EPS = 1e-5
MOE_BLOCK = 512
ROUTE_TILE = 512
GROUP_ROWS = ROUTE_TILE * TOP_K + 256

ATTN_COLS = 3 * ATTN_W
GLA_COLS = 2 * GLA_KEY_W + 2 * GLA_VAL_W
Z_COLS = LANES
PROJ_COLS = ATTN_COLS + GLA_COLS + Z_COLS

NT_DIMS = (((1,), (1,)), ((), ()))


def _sigmoid(x):
    return 1.0 / (1.0 + jnp.exp(-x))


def _split3(a):
    h1 = a.astype(BF16)
    r1 = a - h1.astype(F32)
    h2 = r1.astype(BF16)
    h3 = (r1 - h2.astype(F32)).astype(BF16)
    return h1, h2, h3


def _dot_f32(a, b):
    ah, al, _ = _split3(a)
    bh, bl, _ = _split3(b)
    d = lambda u, v: jnp.dot(u, v, preferred_element_type=F32)
    return d(ah, bh) + d(al, bh) + d(ah, bl)


def _in_proj_body(x_ref, g_ref, w_ref, wg_ref, bg_ref, a1_ref, a4_ref, a16_ref, gla_ref, b_ref, p_ref, *, tm):
    x = x_ref[...]
    n = x * lax.rsqrt(jnp.mean(x * x, axis=-1, keepdims=True) + NORM_EPS) * g_ref[...]
    p = jnp.dot(n.astype(BF16), w_ref[...], preferred_element_type=F32)
    gla_ref[...] = p[:, ATTN_COLS:ATTN_COLS + GLA_COLS].astype(BF16)

    a1_ref[0, 0] = p[:, :ATTN_COLS].astype(BF16)
    for c in range(ATTN_COLS // LANES):
        p_ref[c] = p[:, c * LANES:(c + 1) * LANES]
    for dil, a_ref in ((DILATIONS[1], a4_ref), (DILATIONS[2], a16_ref)):
        for r in range(dil):
            for c in range(ATTN_COLS // LANES):
                a_ref[0, r, :, c * LANES:(c + 1) * LANES] = p_ref[c, pl.ds(r, tm // dil, stride=dil), :].astype(BF16)

    z = p[:, ATTN_COLS + GLA_COLS:]
    zz = _dot_f32(z, wg_ref[...]) + bg_ref[...]
    la = (jnp.minimum(zz, 0.0) - jnp.log1p(jnp.exp(-jnp.abs(zz)))) * (1.0 / GLA_TAU)
    C = GLA_CHUNK
    r_i = lax.broadcasted_iota(I32, (C, C), 0)
    c_i = lax.broadcasted_iota(I32, (C, C), 1)
    lower = (r_i >= c_i).astype(BF16)
    upper = (r_i <= c_i).astype(BF16)
    W = GLA_KEY_W
    for j in range(tm // C):
        rows = slice(j * C, (j + 1) * C)
        for tri, cols in ((lower, slice(0, W)), (upper, slice(W, 2 * W))):
            terms = jnp.concatenate(_split3(la[rows, cols]), axis=1)
            cs = jnp.dot(tri, terms, preferred_element_type=F32)
            b_ref[rows, cols] = cs[:, :W] + cs[:, W:2 * W] + cs[:, 2 * W:]


def _in_proj(x2, B, S, norm_g, w_in, wg_f, bg_f, wg_b, bg_b):
    T, D = x2.shape
    tm = 512
    nt = S // tm
    w = jnp.pad(w_in.astype(BF16), ((0, 0), (0, PROJ_COLS - w_in.shape[1])))
    wg = jnp.zeros((Z_COLS, 2 * GLA_KEY_W), F32)
    wg = wg.at[:GLA_RANK, :GLA_KEY_W].set(wg_f).at[GLA_RANK:2 * GLA_RANK, GLA_KEY_W:].set(wg_b)
    bg = jnp.concatenate([bg_f, bg_b]).reshape(1, 2 * GLA_KEY_W)
    row = lambda c: pl.BlockSpec((tm, c), lambda b, i: (b * nt + i, 0))
    full = lambda a: pl.BlockSpec(a.shape, lambda b, i: (0,) * a.ndim)
    res = lambda d: pl.BlockSpec((1, d, tm // d, ATTN_COLS), lambda b, i: (b, 0, i, 0))
    res_shape = lambda d: jax.ShapeDtypeStruct((B, d, S // d, ATTN_COLS), BF16)
    g2 = norm_g.reshape(1, D)
    return pl.pallas_call(
        functools.partial(_in_proj_body, tm=tm),
        grid=(B, nt),
        in_specs=[row(D), full(g2), full(w), full(wg), full(bg)],
        out_specs=[res(d) for d in DILATIONS] + [row(GLA_COLS), row(2 * GLA_KEY_W)],
        out_shape=[res_shape(d) for d in DILATIONS]
                  + [jax.ShapeDtypeStruct((T, GLA_COLS), BF16), jax.ShapeDtypeStruct((T, 2 * GLA_KEY_W), F32)],
        scratch_shapes=[pltpu.VMEM((ATTN_COLS // LANES, tm, LANES), F32)],
        compiler_params=pltpu.CompilerParams(dimension_semantics=("arbitrary", "arbitrary"),
                                             vmem_limit_bytes=56 * 2**20),
        name="in_proj",
    )(x2, g2, w, wg, bg)


def _t5_bucket(rel):
    nb = T5_BUCKETS // 2
    max_exact = nb // 2
    n = np.abs(rel)
    large = max_exact + (np.log(np.maximum(n, 1).astype(np.float32) / max_exact)
                         / np.log(T5_MAX_DISTANCE / max_exact) * (nb - max_exact)).astype(np.int32)
    large = np.minimum(large, nb - 1)
    return (np.where(rel > 0, nb, 0) + np.where(n < max_exact, n, large)).astype(np.int32)


def _bucket_tables(dil):
    s = np.arange(ATTN_QBLK)[:, None]
    t = np.arange(ATTN_KWIN)[None, :]
    tabs = []
    for c in (0, HALF_SPAN, 2 * HALF_SPAN):
        off = t - c - s
        tabs.append(np.where(np.abs(off) <= HALF_SPAN, _t5_bucket(off * dil), -1))
    return np.stack(tabs).astype(np.int32)


def _attn_body(rb_ref, idx_ref, q_ref, k_ref, v_ref, o_ref, lse_ref, tab_ref, *, L, TQ):
    first = (pl.program_id(0) == 0) & (pl.program_id(1) == 0) & (pl.program_id(2) == 0)

    @pl.when(first)
    def _():
        for v in range(3):
            idx = idx_ref[v]
            for h in range(ATTN_HEADS):
                fill = lambda j, t, h=h, idx=idx: jnp.where(idx == j, rb_ref[j, h], t)
                tab_ref[v, h] = lax.fori_loop(0, T5_BUCKETS, fill, jnp.where(idx < 0, NEG_INF, 0.0).astype(F32))

    i = pl.program_id(2)
    lo = lax.broadcasted_iota(I32, (ATTN_QBLK, LANES), 1) < HEAD_DIM
    scale = HEAD_DIM ** -0.5
    for sb in range(TQ // ATTN_QBLK):
        l0 = i * TQ + sb * ATTN_QBLK
        ws = pl.multiple_of(jnp.clip(l0 - HALF_SPAN, 0, L - ATTN_KWIN), HALF_SPAN)
        var = jnp.where(l0 == 0, 0, jnp.where(l0 == L - ATTN_QBLK, 2, 1))
        rows = slice(sb * ATTN_QBLK, (sb + 1) * ATTN_QBLK)
        for hp in range(ATTN_HEADS // 2):
            cols = slice(hp * LANES, (hp + 1) * LANES)
            q2 = q_ref[rows, cols]
            k2 = k_ref[pl.ds(ws, ATTN_KWIN), cols]
            v2 = v_ref[pl.ds(ws, ATTN_KWIN), cols]
            zero = jnp.zeros_like(q2)
            qs = jnp.concatenate([jnp.where(lo, q2, zero), jnp.where(lo, zero, q2)], axis=0)
            s = lax.dot_general(qs, k2, NT_DIMS, preferred_element_type=F32)
            bias = jnp.concatenate([tab_ref[var, 2 * hp], tab_ref[var, 2 * hp + 1]], axis=0)
            logits = s * scale + bias
            m = jnp.max(logits, axis=-1, keepdims=True)
            p = jnp.exp(logits - m)
            den = jnp.sum(p, axis=-1, keepdims=True)
            pv = jnp.dot(p.astype(BF16), v2, preferred_element_type=F32) / den
            lse = jnp.broadcast_to(m + jnp.log(den), (2 * ATTN_QBLK, LANES))
            o_ref[rows, cols] = jnp.where(lo, pv[:ATTN_QBLK], pv[ATTN_QBLK:]).astype(BF16)
            lse_ref[rows, cols] = jnp.where(lo, lse[:ATTN_QBLK], lse[ATTN_QBLK:])


def _attn_branch(attn, rel_bias, dil):
    B, _, L, _ = attn.shape
    TQ = 256
    idx = jnp.asarray(_bucket_tables(dil))
    body = functools.partial(_attn_body, L=L, TQ=TQ)
    kv = lambda j: pl.BlockSpec((None, None, L, ATTN_W), lambda b, r, i: (b, r, 0, j))
    q_spec = pl.BlockSpec((None, None, TQ, ATTN_W), lambda b, r, i: (b, r, i, 0))
    return pl.pallas_call(
        body,
        grid=(B, dil, L // TQ),
        in_specs=[pl.BlockSpec(memory_space=pltpu.SMEM),
                  pl.BlockSpec(idx.shape, lambda b, r, i: (0, 0, 0)),
                  q_spec, kv(1), kv(2)],
        out_specs=[q_spec, q_spec],
        out_shape=[jax.ShapeDtypeStruct((B, dil, L, ATTN_W), BF16),
                   jax.ShapeDtypeStruct((B, dil, L, ATTN_W), F32)],
        scratch_shapes=[pltpu.VMEM((3, ATTN_HEADS, ATTN_QBLK, ATTN_KWIN), F32)],
        compiler_params=pltpu.CompilerParams(dimension_semantics=("arbitrary",) * 3,
                                             vmem_limit_bytes=48 * 2**20),
        name=f"attn_d{dil}",
    )(rel_bias, idx, attn, attn, attn)


def _gla_body(q_ref, k_ref, v_ref, gate_ref, bf_ref, bb_ref, ng_ref, out_ref, of_ref, ob_ref, st_ref, *, S):
    C = GLA_CHUNK
    nc = S // C
    r = lax.broadcasted_iota(I32, (C, LANES), 0)
    s2 = lax.broadcasted_iota(I32, (C, LANES), 1) % HEAD_DIM
    lo = lax.broadcasted_iota(I32, (C, LANES), 1) < HEAD_DIM
    blockdiag = ((lax.broadcasted_iota(I32, (2 * GLA_DV, LANES), 0) < GLA_DV)
                 == (lax.broadcasted_iota(I32, (2 * GLA_DV, LANES), 1) < HEAD_DIM))
    st_ref[...] = jnp.zeros_like(st_ref)

    def one_chunk(ci, direction):
        mask = (r >= s2) if direction == 0 else (r <= s2)
        b_ref, o_ref = (bf_ref, of_ref) if direction == 0 else (bb_ref, ob_ref)
        rows = pl.ds(pl.multiple_of(ci * C, C), C)
        q = q_ref[0, rows, :].astype(F32) * (HEAD_DIM ** -0.5)
        k = k_ref[0, rows, :].astype(F32)
        v = v_ref[0, rows, :]
        b = b_ref[0, rows, :]
        g_tot = b[C - 1:C, :] if direction == 0 else b[0:1, :]
        qd = (q * jnp.exp(b)).astype(BF16)
        kd = (k * jnp.exp(-b)).astype(BF16)
        kst = (k * jnp.exp(g_tot - b)).astype(BF16)
        zk = jnp.zeros_like(kd)
        ks = jnp.concatenate([jnp.where(lo, kd, zk), jnp.where(lo, zk, kd)], axis=0)
        a = lax.dot_general(qd, ks, NT_DIMS, preferred_element_type=F32)
        a = jnp.where(mask, a, 0.0).astype(BF16)
        zv = jnp.zeros((C, GLA_DV), BF16)
        vblk = jnp.concatenate([jnp.concatenate([v[:, :GLA_DV], zv], axis=1),
                                jnp.concatenate([zv, v[:, GLA_DV:]], axis=1)], axis=0)
        st = st_ref[direction]
        o_ref[rows, :] = (jnp.dot(a, vblk, preferred_element_type=F32)
                          + lax.dot_general(qd, st.astype(BF16), NT_DIMS, preferred_element_type=F32))
        d_st = jnp.dot(v.astype(F32).T.astype(BF16), kst, preferred_element_type=F32)
        st_ref[direction] = st * jnp.exp(g_tot) + jnp.where(blockdiag, d_st, 0.0)

    def step(i, carry):
        one_chunk(i, 0)
        one_chunk(nc - 1 - i, 1)
        return carry

    lax.fori_loop(0, nc, step, 0, unroll=2)

    tr = 256
    def epilogue(i, carry):
        rows = pl.ds(pl.multiple_of(i * tr, tr), tr)
        for hh in range(2):
            cols = slice(hh * GLA_DV, (hh + 1) * GLA_DV)
            o = of_ref[rows, cols] + ob_ref[rows, cols]
            o = o * lax.rsqrt(jnp.mean(o * o, axis=-1, keepdims=True) + NORM_EPS) * ng_ref[0, :, cols]
            gate = gate_ref[0, rows, cols].astype(F32)
            out_ref[0, rows, cols] = (o * (gate * _sigmoid(gate))).astype(BF16)
        return carry

    lax.fori_loop(0, S // tr, epilogue, 0)


def _gla(gla, bcum, norm_g, B, S):
    g3 = gla.reshape(B, S, GLA_COLS)
    b3 = bcum.reshape(B, S, 2 * GLA_KEY_W)
    ng = norm_g.reshape(2, 1, 2 * GLA_DV)
    body = functools.partial(_gla_body, S=S)
    spec = lambda w, off: pl.BlockSpec((1, S, w), lambda b, p: (b, 0, off + p))
    out = pl.pallas_call(
        body,
        grid=(B, 2),
        in_specs=[spec(LANES, 0),
                  spec(LANES, GLA_KEY_W // LANES),
                  spec(2 * GLA_DV, (2 * GLA_KEY_W) // (2 * GLA_DV)),
                  spec(2 * GLA_DV, (2 * GLA_KEY_W + GLA_VAL_W) // (2 * GLA_DV)),
                  spec(LANES, 0),
                  spec(LANES, GLA_KEY_W // LANES),
                  pl.BlockSpec((1, 1, 2 * GLA_DV), lambda b, p: (p, 0, 0))],
        out_specs=pl.BlockSpec((1, S, 2 * GLA_DV), lambda b, p: (b, 0, p)),
        out_shape=jax.ShapeDtypeStruct((B, S, GLA_VAL_W), BF16),
        scratch_shapes=[pltpu.VMEM((S, 2 * GLA_DV), F32), pltpu.VMEM((S, 2 * GLA_DV), F32),
                        pltpu.VMEM((2, 2 * GLA_DV, LANES), F32)],
        compiler_params=pltpu.CompilerParams(dimension_semantics=("arbitrary", "arbitrary"),
                                             vmem_limit_bytes=56 * 2**20),
        name="gla",
    )(g3, g3, g3, g3, b3, b3, ng)
    return out.reshape(B * S, GLA_VAL_W)


def _natural_rows(blk_ref, u_ref, dil, tm):
    if dil == 1:
        return blk_ref[0, 0].astype(F32)
    for r in range(dil):
        for c in range(ATTN_W // LANES):
            u_ref[c, pl.ds(r, tm // dil, stride=dil), :] = blk_ref[0, r, :, c * LANES:(c + 1) * LANES].astype(F32)
    return jnp.concatenate([u_ref[c] for c in range(ATTN_W // LANES)], axis=1)


def _mix_body(o1, o2, o3, l1, l2, l3, g_ref, x_ref, wo_ref, n2_ref, rwh_ref, rwl_ref, rb_ref,
              x1_ref, h_ref, m_ref, slab_ref, tcnt_ref, u_ref, *, tm):
    ls = [_natural_rows(l, u_ref, d, tm) for l, d in zip((l1, l2, l3), DILATIONS)]
    m = jnp.maximum(jnp.maximum(ls[0], ls[1]), ls[2])
    es = [jnp.exp(l - m) for l in ls]
    a = es[0] * _natural_rows(o1, u_ref, DILATIONS[0], tm)
    a = a + es[1] * _natural_rows(o2, u_ref, DILATIONS[1], tm)
    a = a + es[2] * _natural_rows(o3, u_ref, DILATIONS[2], tm)
    a = a / (es[0] + es[1] + es[2])
    mix = (jnp.dot(a.astype(BF16), wo_ref[:ATTN_W, :], preferred_element_type=F32)
           + jnp.dot(g_ref[...], wo_ref[ATTN_W:, :], preferred_element_type=F32))
    x1 = x_ref[...] + mix
    x1_ref[...] = x1
    h = x1 * lax.rsqrt(jnp.mean(x1 * x1, axis=-1, keepdims=True) + NORM_EPS) * n2_ref[...]
    h_ref[...] = h

    hh, hl, _ = _split3(h)
    dg = lambda u, v: lax.dot_general(u, v, NT_DIMS, preferred_element_type=F32)
    logits = dg(rwh_ref[...], hh) + dg(rwl_ref[...], hh) + dg(rwh_ref[...], hl) + rb_ref[...]
    ie = lax.broadcasted_iota(I32, (N_EXPERTS, tm), 0)
    cur = logits
    vals, idxs = [], []
    for _ in range(TOP_K):
        mx = jnp.max(cur, axis=0, keepdims=True)
        ix = jnp.min(jnp.where(cur == mx, ie, N_EXPERTS), axis=0, keepdims=True)
        vals.append(mx)
        idxs.append(ix)
        cur = jnp.where(ie == ix, -jnp.inf, cur)
    ex = [jnp.exp(v - vals[0]) for v in vals]
    den = ex[0] + ex[1] + ex[2] + ex[3]
    gates = [e / den for e in ex]

    onehots = [ie == ix for ix in idxs]
    chosen = (onehots[0] | onehots[1] | onehots[2] | onehots[3]).astype(F32)
    tr = lax.broadcasted_iota(I32, (tm, tm), 0)
    tc = lax.broadcasted_iota(I32, (tm, tm), 1)
    before = jnp.dot(chosen.astype(BF16), (tr < tc).astype(BF16), preferred_element_type=F32)
    count = jnp.broadcast_to(jnp.sum(chosen, axis=1, keepdims=True), (N_EXPERTS, LANES))
    tcnt_ref[0] = count
    run = jnp.floor((count + 7.0) * 0.125) * 8.0
    er = lax.broadcasted_iota(I32, (N_EXPERTS, N_EXPERTS), 0)
    ec = lax.broadcasted_iota(I32, (N_EXPERTS, N_EXPERTS), 1)
    run_start = jnp.dot((ec < er).astype(BF16), run.astype(BF16), preferred_element_type=F32)
    where_ = before + run_start[:, 0:1]
    locs = [jnp.sum(jnp.where(oh, where_, 0.0), axis=0, keepdims=True) for oh in onehots]

    def pack(rows, n):
        sub = lax.broadcasted_iota(I32, (n, tm), 0)
        out = jnp.zeros((n, tm), rows[0].dtype)
        for k, row in enumerate(rows):
            out = jnp.where(sub == k, row, out)
        return out

    m_ref[...] = pack([l.astype(I32) for l in locs], 8)
    slab = pack(gates + locs, 8)
    slab_ref[...] = jnp.concatenate([slab, jnp.zeros((LANES - 8, tm), F32)], axis=0).T


def _mix(outs, lses, g, x2, B, S, w_out, norm2_g, router_w, router_b):
    T, D = x2.shape
    tm = ROUTE_TILE
    nt = S // tm
    wo = w_out.astype(BF16)
    rwt = router_w.T
    rwh = rwt.astype(BF16)
    rwl = (rwt - rwh.astype(F32)).astype(BF16)
    rb = router_b.reshape(N_EXPERTS, 1)
    n2 = norm2_g.reshape(1, D)
    row = lambda c: pl.BlockSpec((tm, c), lambda b, i: (b * nt + i, 0))
    full = lambda a: pl.BlockSpec(a.shape, lambda b, i: (0,) * a.ndim)
    col = pl.BlockSpec((8, tm), lambda b, i: (0, b * nt + i))
    res = [pl.BlockSpec((1, d, tm // d, ATTN_W), lambda b, i: (b, 0, i, 0)) for d in DILATIONS]
    return pl.pallas_call(
        functools.partial(_mix_body, tm=tm),
        grid=(B, nt),
        in_specs=res + res + [row(GLA_VAL_W), row(D), full(wo), full(n2), full(rwh), full(rwl), full(rb)],
        out_specs=[row(D), row(D), col, row(LANES),
                   pl.BlockSpec((1, N_EXPERTS, LANES), lambda b, i: (b * nt + i, 0, 0))],
        out_shape=[jax.ShapeDtypeStruct((T, D), F32),
                   jax.ShapeDtypeStruct((T, D), F32),
                   jax.ShapeDtypeStruct((8, T), I32),
                   jax.ShapeDtypeStruct((T, LANES), F32),
                   jax.ShapeDtypeStruct((T // tm, N_EXPERTS, LANES), F32)],
        scratch_shapes=[pltpu.VMEM((ATTN_W // LANES, tm, LANES), F32)],
        compiler_params=pltpu.CompilerParams(dimension_semantics=("arbitrary", "arbitrary"),
                                             vmem_limit_bytes=48 * 2**20),
        name="mix_router",
    )(*outs, *lses, g, x2, wo, n2, rwh, rwl, rb)


def _round8(c):
    return ((c + 7) >> 3) << 3


def _plan_body(tc_ref, rs_ref, p0_ref, pn_ref, be_ref, nu_ref, *, nt, nblk):
    shift = MOE_BLOCK.bit_length() - 1
    start = jnp.int32(0)
    for ex in range(N_EXPERTS):
        def run(i, tot, start=start, ex=ex):
            rs_ref[i, ex] = start + tot
            return tot + _round8(tc_ref[i, ex])

        tot = lax.fori_loop(0, nt, run, jnp.int32(0))
        nb = (tot + (MOE_BLOCK - 1)) >> shift
        p0_ref[ex] = start + tot
        pn_ref[ex] = ((nb << shift) - tot) >> 3
        b0 = start >> shift

        def fill(j, carry, b0=b0, ex=ex):
            be_ref[b0 + j] = jnp.int32(ex)
            return carry

        lax.fori_loop(0, nb, fill, 0)
        start = start + (nb << shift)
    used = start >> shift

    def tail(j, carry):
        be_ref[j] = jnp.int32(N_EXPERTS - 1)
        return carry

    lax.fori_loop(used, nblk, tail, 0)
    nu_ref[0] = used


def _plan(tcnt, nblk):
    nt = tcnt.shape[0]
    smem = pl.BlockSpec(memory_space=pltpu.SMEM)
    return pl.pallas_call(
        functools.partial(_plan_body, nt=nt, nblk=nblk),
        in_specs=[smem],
        out_specs=[smem] * 5,
        out_shape=[jax.ShapeDtypeStruct(tcnt.shape, I32),
                   jax.ShapeDtypeStruct((N_EXPERTS,), I32),
                   jax.ShapeDtypeStruct((N_EXPERTS,), I32),
                   jax.ShapeDtypeStruct((nblk,), I32),
                   jax.ShapeDtypeStruct((1,), I32)],
        name="plan",
    )(tcnt)


def _for_each_run(tc_ref, rs_ref, i, fn):
    start = jnp.int32(0)
    for ex in range(N_EXPERTS):
        n8 = (tc_ref[i, ex] + 7) >> 3
        g0 = rs_ref[i, ex]

        def body(q, carry, start=start, g0=g0):
            fn(pl.multiple_of(start + q * 8, 8), pl.multiple_of(g0 + q * 8, 8))
            return carry

        lax.fori_loop(0, n8, body, 0)
        start = start + n8 * 8


def _group_count(tc_ref, i):
    n = jnp.int32(0)
    for ex in range(N_EXPERTS):
        n = n + ((tc_ref[i, ex] + 7) >> 3)
    return n


def _wait_groups(n, src_ref, dst_ref, sem):
    def waits(count, rows):
        def body(q, carry):
            pltpu.make_async_copy(src_ref.at[pl.ds(0, rows)], dst_ref.at[pl.ds(0, rows)], sem).wait()
            return carry
        lax.fori_loop(0, count, body, 0)

    waits(n >> 6, 512)
    waits((n >> 3) & 7, 64)
    waits(n & 7, 8)


def _dispatch_body(tc_ref, rs_ref, p0_ref, pn_ref, nu_ref, m_ref, h_ref, xin_ref, g_ref, z_ref, sem, zsem, *, tm, R):
    i = pl.program_id(0)
    slot = i % 2
    rows = m_ref[...]
    jj = lax.broadcasted_iota(I32, (R, tm), 0)
    hit = (jj == rows[0:1]) | (jj == rows[1:2]) | (jj == rows[2:3]) | (jj == rows[3:4])
    g_ref[slot] = jnp.dot(jnp.where(hit, 1.0, 0.0).astype(BF16), h_ref[...].astype(BF16), preferred_element_type=F32)

    group = lambda ref, r: ref.at[pl.ds(r, 8)]

    @pl.when(i > 0)
    def _():
        _wait_groups(_group_count(tc_ref, i - 1), g_ref.at[1 - slot], xin_ref, sem.at[1 - slot])

    _for_each_run(tc_ref, rs_ref, i,
                  lambda lr, gr: pltpu.make_async_copy(group(g_ref.at[slot], lr), group(xin_ref, gr), sem.at[slot]).start())

    @pl.when(i == pl.num_programs(0) - 1)
    def _():
        _wait_groups(_group_count(tc_ref, i), g_ref.at[slot], xin_ref, sem.at[slot])

    @pl.when(i == 0)
    def _():
        z_ref[...] = jnp.zeros_like(z_ref)
        total = jnp.int32(0)
        for ex in range(N_EXPERTS):
            def body(q, carry, ex=ex):
                pltpu.make_async_copy(group(z_ref, 0), group(xin_ref, pl.multiple_of(p0_ref[ex] + q * 8, 8)), zsem).start()
                return carry
            lax.fori_loop(0, pn_ref[ex], body, 0)
            total = total + pn_ref[ex]
        _wait_groups(total, z_ref, xin_ref, zsem)

        block = lambda j: xin_ref.at[pl.ds(pl.multiple_of(j * MOE_BLOCK, MOE_BLOCK), MOE_BLOCK)]
        n_blocks = xin_ref.shape[0] // MOE_BLOCK

        def fill(j, carry):
            pltpu.make_async_copy(z_ref, block(j), zsem).start()
            return carry

        def fill_wait(j, carry):
            pltpu.make_async_copy(z_ref, block(j), zsem).wait()
            return carry

        lax.fori_loop(nu_ref[0], n_blocks, fill, 0)
        lax.fori_loop(nu_ref[0], n_blocks, fill_wait, 0)


def _dispatch(tcnt, rowstart, pad0, padn, n_used, m_t, h, n_rows):
    T, D = h.shape
    tm = ROUTE_TILE
    col = pl.BlockSpec((8, tm), lambda i, *_: (0, i))
    return pl.pallas_call(
        functools.partial(_dispatch_body, tm=tm, R=GROUP_ROWS),
        grid_spec=pltpu.PrefetchScalarGridSpec(
            num_scalar_prefetch=5,
            grid=(T // tm,),
            in_specs=[col, pl.BlockSpec((tm, D), lambda i, *_: (i, 0))],
            out_specs=pl.BlockSpec(memory_space=pl.ANY),
            scratch_shapes=[pltpu.VMEM((2, GROUP_ROWS, D), F32), pltpu.VMEM((MOE_BLOCK, D), F32),
                            pltpu.SemaphoreType.DMA((2,)), pltpu.SemaphoreType.DMA(())]),
        out_shape=jax.ShapeDtypeStruct((n_rows, D), F32),
        compiler_params=pltpu.CompilerParams(dimension_semantics=("arbitrary",), vmem_limit_bytes=56 * 2**20),
        name="dispatch",
    )(tcnt, rowstart, pad0, padn, n_used, m_t, h)


def _w1_split_body(w_ref, g_ref, l_ref, t_ref, *, tf):
    wt = w_ref[0].T
    for c in range(wt.shape[1] // LANES):
        cols = slice(c * LANES, (c + 1) * LANES)
        t_ref[c] = wt[:, cols]
        g_ref[0, :, cols] = t_ref[c, pl.ds(0, tf // 2, stride=2), :].astype(BF16)
        l_ref[0, :, cols] = t_ref[c, pl.ds(1, tf // 2, stride=2), :].astype(BF16)


def _w1_split(w1):
    E, D, F2 = w1.shape
    tf = 512
    out_spec = pl.BlockSpec((1, tf // 2, D), lambda e, j: (e, j, 0))
    return pl.pallas_call(
        functools.partial(_w1_split_body, tf=tf),
        grid=(E, F2 // tf),
        in_specs=[pl.BlockSpec((1, D, tf), lambda e, j: (e, 0, j))],
        out_specs=[out_spec, out_spec],
        out_shape=[jax.ShapeDtypeStruct((E, F2 // 2, D), BF16)] * 2,
        scratch_shapes=[pltpu.VMEM((D // LANES, tf, LANES), F32)],
        compiler_params=pltpu.CompilerParams(dimension_semantics=("arbitrary", "arbitrary")),
        name="w1_split",
    )(w1)


def _moe_body(be_ref, nu_ref, x_ref, w1g_ref, w1l_ref, b1g_ref, b1l_ref, w2_ref, b2_ref, y_ref):
    del be_ref
    live = pl.program_id(0) < nu_ref[0]

    @pl.when(live)
    def _():
        x = x_ref[...].astype(BF16)
        hg = lax.dot_general(x, w1g_ref[0], NT_DIMS, preferred_element_type=F32) + b1g_ref[0]
        hl = lax.dot_general(x, w1l_ref[0], NT_DIMS, preferred_element_type=F32) + b1l_ref[0]
        xg = jnp.minimum(hg, SWIGLU_LIMIT)
        xl = jnp.clip(hl, -SWIGLU_LIMIT, SWIGLU_LIMIT)
        act = xg * _sigmoid(SWIGLU_ALPHA * xg) * (xl + 1.0)
        y_ref[...] = jnp.dot(act.astype(BF16), w2_ref[0], preferred_element_type=F32) + b2_ref[0]

    @pl.when(jnp.logical_not(live))
    def _():
        y_ref[...] = jnp.zeros_like(y_ref)


def _moe(block_e, n_used, xin, w1, b1, w2, b2):
    n_rows = xin.shape[0]
    D = w1.shape[1]
    F = w2.shape[1]
    nblk = n_rows // MOE_BLOCK
    w1g, w1l = _w1_split(w1)
    b1g = b1[:, 0::2].reshape(N_EXPERTS, 1, F)
    b1l = b1[:, 1::2].reshape(N_EXPERTS, 1, F)
    w2b = w2.astype(BF16)
    b2r = b2.reshape(N_EXPERTS, 1, D)
    xspec = pl.BlockSpec((MOE_BLOCK, D), lambda j, be, nu: (jnp.minimum(j, nu[0] - 1), 0))
    yspec = pl.BlockSpec((MOE_BLOCK, D), lambda j, be, nu: (j, 0))
    wspec = lambda a: pl.BlockSpec((1,) + a.shape[1:], lambda j, be, nu: (be[j], 0, 0))
    return pl.pallas_call(
        _moe_body,
        grid_spec=pltpu.PrefetchScalarGridSpec(
            num_scalar_prefetch=2,
            grid=(nblk,),
            in_specs=[xspec, wspec(w1g), wspec(w1l), wspec(b1g), wspec(b1l), wspec(w2b), wspec(b2r)],
            out_specs=yspec),
        out_shape=jax.ShapeDtypeStruct(xin.shape, F32),
        compiler_params=pltpu.CompilerParams(dimension_semantics=("arbitrary",),
                                             vmem_limit_bytes=48 * 2**20),
        name="moe",
    )(block_e, n_used, xin, w1g, w1l, b1g, b1l, w2b, b2r)


def _combine_body(tc_ref, rs_ref, slab_ref, y_ref, x1_ref, fg_ref, out_ref, g_ref, sem, *, tm, R):
    i = pl.program_id(0)
    slot = i % 2
    group = lambda ref, r: ref.at[pl.ds(r, 8)]

    def fetch(tile, slot):
        _for_each_run(tc_ref, rs_ref, tile,
                      lambda lr, gr: pltpu.make_async_copy(group(y_ref, gr), group(g_ref.at[slot], lr), sem.at[slot]).start())

    @pl.when(i == 0)
    def _():
        g_ref[...] = jnp.zeros_like(g_ref)
        fetch(i, slot)

    @pl.when(i + 1 < pl.num_programs(0))
    def _():
        fetch(i + 1, 1 - slot)

    slab = slab_ref[...]
    gates = slab[:, 0:TOP_K]
    rows = slab[:, TOP_K:2 * TOP_K].astype(I32)
    jj = lax.broadcasted_iota(I32, (tm, R), 1)
    sel = jnp.zeros((tm, R), F32)
    for k in range(TOP_K):
        sel = jnp.where(jj == rows[:, k:k + 1], gates[:, k:k + 1], sel)

    _wait_groups(_group_count(tc_ref, i), y_ref, g_ref.at[slot], sem.at[slot])
    acc = x1_ref[...] + jnp.dot(sel.astype(BF16), g_ref[slot].astype(BF16), preferred_element_type=F32)
    out_ref[...] = acc * lax.rsqrt(jnp.mean(acc * acc, axis=-1, keepdims=True) + NORM_EPS) * fg_ref[...]


def _combine(tcnt, rowstart, slab, y, x1, final_g):
    T, D = x1.shape
    tm = ROUTE_TILE
    fg = final_g.reshape(1, D)
    return pl.pallas_call(
        functools.partial(_combine_body, tm=tm, R=GROUP_ROWS),
        grid_spec=pltpu.PrefetchScalarGridSpec(
            num_scalar_prefetch=2,
            grid=(T // tm,),
            in_specs=[pl.BlockSpec((tm, LANES), lambda i, *_: (i, 0)),
                      pl.BlockSpec(memory_space=pl.ANY),
                      pl.BlockSpec((tm, D), lambda i, *_: (i, 0)),
                      pl.BlockSpec((1, D), lambda i, *_: (0, 0))],
            out_specs=pl.BlockSpec((tm, D), lambda i, *_: (i, 0)),
            scratch_shapes=[pltpu.VMEM((2, GROUP_ROWS, D), F32), pltpu.SemaphoreType.DMA((2,))]),
        out_shape=jax.ShapeDtypeStruct((T, D), F32),
        compiler_params=pltpu.CompilerParams(dimension_semantics=("arbitrary",), vmem_limit_bytes=56 * 2**20),
        name="combine",
    )(tcnt, rowstart, slab, y, x1, fg)


def _mixers(x2, B, S, norm1_g, w_in, rel_bias, wg_f, bg_f, wg_b, bg_b, gla_norm_g):
    a1, a4, a16, gla, bcum = _in_proj(x2, B, S, norm1_g, w_in, wg_f, bg_f, wg_b, bg_b)
    branches = [_attn_branch(a, rel_bias, d) for a, d in zip((a1, a4, a16), DILATIONS)]
    g = _gla(gla, bcum, gla_norm_g, B, S)
    return [o for o, _ in branches], [l for _, l in branches], g


def _moe_layer(h, m_t, slab, tcnt_f, x1, w1, b1, w2, b2, final_g):
    T = x1.shape[0]
    nt = T // ROUTE_TILE
    nblk = (T * TOP_K + nt * N_EXPERTS * 7) // MOE_BLOCK + N_EXPERTS
    tcnt = tcnt_f[:, :, 0].astype(I32)
    rowstart, pad0, padn, block_e, n_used = _plan(tcnt, nblk)
    xin = _dispatch(tcnt, rowstart, pad0, padn, n_used, m_t, h, nblk * MOE_BLOCK)
    y = _moe(block_e, n_used, xin, w1, b1, w2, b2)
    return _combine(tcnt, rowstart, slab, y, x1, final_g)


def kernel(x, norm1_g, w_in, rel_bias, gla_wg_fwd, gla_bg_fwd, gla_wg_bwd, gla_bg_bwd, gla_norm_g, w_out, norm2_g, router_w, router_b, moe_w1, moe_b1, moe_w2, moe_b2, final_g):
    B, S, D = x.shape
    assert w_in.shape[0] == 1, "one layer"
    x2 = x.reshape(B * S, D)
    outs, lses, g = _mixers(x2, B, S, norm1_g[0], w_in[0], rel_bias, gla_wg_fwd[0], gla_bg_fwd[0],
                            gla_wg_bwd[0], gla_bg_bwd[0], gla_norm_g[0])
    x1, h, m_t, slab, tcnt = _mix(outs, lses, g, x2, B, S, w_out[0], norm2_g[0], router_w[0], router_b[0])
    out = _moe_layer(h, m_t, slab, tcnt, x1, moe_w1[0], moe_b1[0], moe_w2[0], moe_b2[0], final_g)
    return out.reshape(B, S, D)
```

```python
import functools

import jax
import jax.numpy as jnp
import numpy as np
from jax import lax
from jax.experimental import pallas as pl
from jax.experimental.pallas import tpu as pltpu

F32, BF16, I32 = jnp.float32, jnp.bfloat16, jnp.int32

LANES = 128
HEAD_DIM = 64
ATTN_HEADS = 8
ATTN_W = ATTN_HEADS * HEAD_DIM
HALF_SPAN = 64
ATTN_QBLK = 128
ATTN_KWIN = 256
DILATIONS = (1, 4, 16)
T5_BUCKETS = 32
T5_MAX_DISTANCE = 1024
NEG_INF = -1e30
GLA_KEY_W = 256
GLA_VAL_W = 512
GLA_DV = 128
GLA_RANK = 16
GLA_TAU = 16.0
GLA_CHUNK = 64
N_EXPERTS = 32
TOP_K = 4
SWIGLU_LIMIT = 7.0
SWIGLU_ALPHA = 1.702
NORM_EPS = 1e-5
MOE_BLOCK = 512
ROUTE_TILE = 512
RUN_ALIGN = 16
RUN_SHIFT = 4
GROUP_ROWS = ROUTE_TILE * TOP_K + N_EXPERTS * RUN_ALIGN

ATTN_COLS = 3 * ATTN_W
GLA_COLS = 2 * GLA_KEY_W + 2 * GLA_VAL_W
Z_COLS = LANES
PROJ_COLS = ATTN_COLS + GLA_COLS + Z_COLS

NT_DIMS = (((1,), (1,)), ((), ()))


def _sigmoid(x):
    return 1.0 / (1.0 + jnp.exp(-x))


def _split3(a):
    h1 = a.astype(BF16)
    r1 = a - h1.astype(F32)
    h2 = r1.astype(BF16)
    h3 = (r1 - h2.astype(F32)).astype(BF16)
    return h1, h2, h3


def _dot_f32(a, b):
    ah, al, _ = _split3(a)
    bh, bl, _ = _split3(b)
    d = lambda u, v: jnp.dot(u, v, preferred_element_type=F32)
    return d(ah, bh) + d(al, bh) + d(ah, bl)


def _in_proj_body(x_ref, g_ref, w_ref, wg_ref, bg_ref, a1_ref, a4_ref, a16_ref, gla_ref, b_ref, p_ref, *, tm):
    x = x_ref[...]
    n = x * lax.rsqrt(jnp.mean(x * x, axis=-1, keepdims=True) + NORM_EPS) * g_ref[...]
    p = jnp.dot(n.astype(BF16), w_ref[...], preferred_element_type=F32)
    gla_ref[...] = p[:, ATTN_COLS:ATTN_COLS + GLA_COLS].astype(BF16)

    a1_ref[0, 0] = p[:, :ATTN_COLS].astype(BF16)
    for c in range(ATTN_COLS // LANES):
        p_ref[c] = p[:, c * LANES:(c + 1) * LANES]
    for dil, a_ref in ((DILATIONS[1], a4_ref), (DILATIONS[2], a16_ref)):
        for r in range(dil):
            for c in range(ATTN_COLS // LANES):
                a_ref[0, r, :, c * LANES:(c + 1) * LANES] = p_ref[c, pl.ds(r, tm // dil, stride=dil), :].astype(BF16)

    z = p[:, ATTN_COLS + GLA_COLS:]
    zz = _dot_f32(z, wg_ref[...]) + bg_ref[...]
    la = (jnp.minimum(zz, 0.0) - jnp.log1p(jnp.exp(-jnp.abs(zz)))) * (1.0 / GLA_TAU)
    C = GLA_CHUNK
    r_i = lax.broadcasted_iota(I32, (C, C), 0)
    c_i = lax.broadcasted_iota(I32, (C, C), 1)
    lower = (r_i >= c_i).astype(BF16)
    upper = (r_i <= c_i).astype(BF16)
    W = GLA_KEY_W
    for j in range(tm // C):
        rows = slice(j * C, (j + 1) * C)
        for tri, cols in ((lower, slice(0, W)), (upper, slice(W, 2 * W))):
            terms = jnp.concatenate(_split3(la[rows, cols]), axis=1)
            cs = jnp.dot(tri, terms, preferred_element_type=F32)
            b_ref[rows, cols] = cs[:, :W] + cs[:, W:2 * W] + cs[:, 2 * W:]


def _in_proj(x2, B, S, norm_g, w_in, wg_f, bg_f, wg_b, bg_b):
    T, D = x2.shape
    tm = 512
    nt = S // tm
    w = jnp.pad(w_in.astype(BF16), ((0, 0), (0, PROJ_COLS - w_in.shape[1])))
    wg = jnp.zeros((Z_COLS, 2 * GLA_KEY_W), F32)
    wg = wg.at[:GLA_RANK, :GLA_KEY_W].set(wg_f).at[GLA_RANK:2 * GLA_RANK, GLA_KEY_W:].set(wg_b)
    bg = jnp.concatenate([bg_f, bg_b]).reshape(1, 2 * GLA_KEY_W)
    row = lambda c: pl.BlockSpec((tm, c), lambda b, i: (b * nt + i, 0))
    full = lambda a: pl.BlockSpec(a.shape, lambda b, i: (0,) * a.ndim)
    res = lambda d: pl.BlockSpec((1, d, tm // d, ATTN_COLS), lambda b, i: (b, 0, i, 0))
    res_shape = lambda d: jax.ShapeDtypeStruct((B, d, S // d, ATTN_COLS), BF16)
    g2 = norm_g.reshape(1, D)
    return pl.pallas_call(
        functools.partial(_in_proj_body, tm=tm),
        grid=(B, nt),
        in_specs=[row(D), full(g2), full(w), full(wg), full(bg)],
        out_specs=[res(d) for d in DILATIONS] + [row(GLA_COLS), row(2 * GLA_KEY_W)],
        out_shape=[res_shape(d) for d in DILATIONS]
                  + [jax.ShapeDtypeStruct((T, GLA_COLS), BF16), jax.ShapeDtypeStruct((T, 2 * GLA_KEY_W), F32)],
        scratch_shapes=[pltpu.VMEM((ATTN_COLS // LANES, tm, LANES), F32)],
        compiler_params=pltpu.CompilerParams(dimension_semantics=("arbitrary", "arbitrary"),
                                             vmem_limit_bytes=56 * 2**20),
        name="in_proj",
    )(x2, g2, w, wg, bg)


def _t5_bucket(rel):
    nb = T5_BUCKETS // 2
    max_exact = nb // 2
    n = np.abs(rel)
    large = max_exact + (np.log(np.maximum(n, 1).astype(np.float32) / max_exact)
                         / np.log(T5_MAX_DISTANCE / max_exact) * (nb - max_exact)).astype(np.int32)
    large = np.minimum(large, nb - 1)
    return (np.where(rel > 0, nb, 0) + np.where(n < max_exact, n, large)).astype(np.int32)


def _bucket_tables(dil):
    s = np.arange(ATTN_QBLK)[:, None]
    t = np.arange(ATTN_KWIN)[None, :]
    tabs = []
    for c in (0, HALF_SPAN, 2 * HALF_SPAN):
        off = t - c - s
        tabs.append(np.where(np.abs(off) <= HALF_SPAN, _t5_bucket(off * dil), -1))
    return np.stack(tabs).astype(np.int32)


def _attn_body(rb_ref, idx_ref, q_ref, k_ref, v_ref, o_ref, lse_ref, tab_ref, *, L, TQ):
    first = (pl.program_id(0) == 0) & (pl.program_id(1) == 0) & (pl.program_id(2) == 0)

    @pl.when(first)
    def _():
        for v in range(3):
            idx = idx_ref[v]
            for h in range(ATTN_HEADS):
                fill = lambda j, t, h=h, idx=idx: jnp.where(idx == j, rb_ref[j, h], t)
                tab_ref[v, h] = lax.fori_loop(0, T5_BUCKETS, fill, jnp.where(idx < 0, NEG_INF, 0.0).astype(F32))

    i = pl.program_id(2)
    lo = lax.broadcasted_iota(I32, (ATTN_QBLK, LANES), 1) < HEAD_DIM
    scale = HEAD_DIM ** -0.5
    for sb in range(TQ // ATTN_QBLK):
        l0 = i * TQ + sb * ATTN_QBLK
        ws = pl.multiple_of(jnp.clip(l0 - HALF_SPAN, 0, L - ATTN_KWIN), HALF_SPAN)
        var = jnp.where(l0 == 0, 0, jnp.where(l0 == L - ATTN_QBLK, 2, 1))
        rows = slice(sb * ATTN_QBLK, (sb + 1) * ATTN_QBLK)
        for hp in range(ATTN_HEADS // 2):
            cols = slice(hp * LANES, (hp + 1) * LANES)
            q2 = q_ref[rows, cols]
            k2 = k_ref[pl.ds(ws, ATTN_KWIN), cols]
            v2 = v_ref[pl.ds(ws, ATTN_KWIN), cols]
            zero = jnp.zeros_like(q2)
            qs = jnp.concatenate([jnp.where(lo, q2, zero), jnp.where(lo, zero, q2)], axis=0)
            s = lax.dot_general(qs, k2, NT_DIMS, preferred_element_type=F32)
            bias = jnp.concatenate([tab_ref[var, 2 * hp], tab_ref[var, 2 * hp + 1]], axis=0)
            logits = s * scale + bias
            m = jnp.max(logits, axis=-1, keepdims=True)
            p = jnp.exp(logits - m)
            den = jnp.sum(p, axis=-1, keepdims=True)
            pv = jnp.dot(p.astype(BF16), v2, preferred_element_type=F32) / den
            lse = jnp.broadcast_to(m + jnp.log(den), (2 * ATTN_QBLK, LANES))
            o_ref[rows, cols] = jnp.where(lo, pv[:ATTN_QBLK], pv[ATTN_QBLK:]).astype(BF16)
            lse_ref[rows, cols] = jnp.where(lo, lse[:ATTN_QBLK], lse[ATTN_QBLK:])


def _attn_branch(attn, rel_bias, dil):
    B, _, L, _ = attn.shape
    TQ = 256
    idx = jnp.asarray(_bucket_tables(dil))
    body = functools.partial(_attn_body, L=L, TQ=TQ)
    kv = lambda j: pl.BlockSpec((None, None, L, ATTN_W), lambda b, r, i: (b, r, 0, j))
    q_spec = pl.BlockSpec((None, None, TQ, ATTN_W), lambda b, r, i: (b, r, i, 0))
    return pl.pallas_call(
        body,
        grid=(B, dil, L // TQ),
        in_specs=[pl.BlockSpec(memory_space=pltpu.SMEM),
                  pl.BlockSpec(idx.shape, lambda b, r, i: (0, 0, 0)),
                  q_spec, kv(1), kv(2)],
        out_specs=[q_spec, q_spec],
        out_shape=[jax.ShapeDtypeStruct((B, dil, L, ATTN_W), BF16),
                   jax.ShapeDtypeStruct((B, dil, L, ATTN_W), F32)],
        scratch_shapes=[pltpu.VMEM((3, ATTN_HEADS, ATTN_QBLK, ATTN_KWIN), F32)],
        compiler_params=pltpu.CompilerParams(dimension_semantics=("arbitrary",) * 3,
                                             vmem_limit_bytes=48 * 2**20),
        name=f"attn_d{dil}",
    )(rel_bias, idx, attn, attn, attn)


def _gla_body(q_ref, k_ref, v_ref, gate_ref, bf_ref, bb_ref, ng_ref, out_ref, of_ref, ob_ref, st_ref, *, S):
    C = GLA_CHUNK
    nc = S // C
    r = lax.broadcasted_iota(I32, (C, LANES), 0)
    s2 = lax.broadcasted_iota(I32, (C, LANES), 1) % HEAD_DIM
    lo = lax.broadcasted_iota(I32, (C, LANES), 1) < HEAD_DIM
    blockdiag = ((lax.broadcasted_iota(I32, (2 * GLA_DV, LANES), 0) < GLA_DV)
                 == (lax.broadcasted_iota(I32, (2 * GLA_DV, LANES), 1) < HEAD_DIM))
    st_ref[...] = jnp.zeros_like(st_ref)

    def one_chunk(ci, direction):
        mask = (r >= s2) if direction == 0 else (r <= s2)
        b_ref, o_ref = (bf_ref, of_ref) if direction == 0 else (bb_ref, ob_ref)
        rows = pl.ds(pl.multiple_of(ci * C, C), C)
        q = q_ref[0, rows, :].astype(F32) * (HEAD_DIM ** -0.5)
        k = k_ref[0, rows, :].astype(F32)
        v = v_ref[0, rows, :]
        b = b_ref[0, rows, :]
        g_tot = b[C - 1:C, :] if direction == 0 else b[0:1, :]
        qd = (q * jnp.exp(b)).astype(BF16)
        kd = (k * jnp.exp(-b)).astype(BF16)
        kst = (k * jnp.exp(g_tot - b)).astype(BF16)
        zk = jnp.zeros_like(kd)
        ks = jnp.concatenate([jnp.where(lo, kd, zk), jnp.where(lo, zk, kd)], axis=0)
        a = lax.dot_general(qd, ks, NT_DIMS, preferred_element_type=F32)
        a = jnp.where(mask, a, 0.0).astype(BF16)
        zv = jnp.zeros((C, GLA_DV), BF16)
        vblk = jnp.concatenate([jnp.concatenate([v[:, :GLA_DV], zv], axis=1),
                                jnp.concatenate([zv, v[:, GLA_DV:]], axis=1)], axis=0)
        st = st_ref[direction]
        o_ref[rows, :] = (jnp.dot(a, vblk, preferred_element_type=F32)
                          + lax.dot_general(qd, st.astype(BF16), NT_DIMS, preferred_element_type=F32))
        d_st = jnp.dot(v.astype(F32).T.astype(BF16), kst, preferred_element_type=F32)
        st_ref[direction] = st * jnp.exp(g_tot) + jnp.where(blockdiag, d_st, 0.0)

    def step(i, carry):
        one_chunk(i, 0)
        one_chunk(nc - 1 - i, 1)
        return carry

    lax.fori_loop(0, nc, step, 0, unroll=2)

    tr = 256
    def epilogue(i, carry):
        rows = pl.ds(pl.multiple_of(i * tr, tr), tr)
        for hh in range(2):
            cols = slice(hh * GLA_DV, (hh + 1) * GLA_DV)
            o = of_ref[rows, cols] + ob_ref[rows, cols]
            o = o * lax.rsqrt(jnp.mean(o * o, axis=-1, keepdims=True) + NORM_EPS) * ng_ref[0, :, cols]
            gate = gate_ref[0, rows, cols].astype(F32)
            out_ref[0, rows, cols] = (o * (gate * _sigmoid(gate))).astype(BF16)
        return carry

    lax.fori_loop(0, S // tr, epilogue, 0)


def _gla(gla, bcum, norm_g, B, S):
    g3 = gla.reshape(B, S, GLA_COLS)
    b3 = bcum.reshape(B, S, 2 * GLA_KEY_W)
    ng = norm_g.reshape(2, 1, 2 * GLA_DV)
    body = functools.partial(_gla_body, S=S)
    spec = lambda w, off: pl.BlockSpec((1, S, w), lambda b, p: (b, 0, off + p))
    out = pl.pallas_call(
        body,
        grid=(B, 2),
        in_specs=[spec(LANES, 0),
                  spec(LANES, GLA_KEY_W // LANES),
                  spec(2 * GLA_DV, (2 * GLA_KEY_W) // (2 * GLA_DV)),
                  spec(2 * GLA_DV, (2 * GLA_KEY_W + GLA_VAL_W) // (2 * GLA_DV)),
                  spec(LANES, 0),
                  spec(LANES, GLA_KEY_W // LANES),
                  pl.BlockSpec((1, 1, 2 * GLA_DV), lambda b, p: (p, 0, 0))],
        out_specs=pl.BlockSpec((1, S, 2 * GLA_DV), lambda b, p: (b, 0, p)),
        out_shape=jax.ShapeDtypeStruct((B, S, GLA_VAL_W), BF16),
        scratch_shapes=[pltpu.VMEM((S, 2 * GLA_DV), F32), pltpu.VMEM((S, 2 * GLA_DV), F32),
                        pltpu.VMEM((2, 2 * GLA_DV, LANES), F32)],
        compiler_params=pltpu.CompilerParams(dimension_semantics=("arbitrary", "arbitrary"),
                                             vmem_limit_bytes=56 * 2**20),
        name="gla",
    )(g3, g3, g3, g3, b3, b3, ng)
    return out.reshape(B * S, GLA_VAL_W)


def _natural_rows(blk_ref, u_ref, dil, tm):
    if dil == 1:
        return blk_ref[0, 0].astype(F32)
    for r in range(dil):
        for c in range(ATTN_W // LANES):
            u_ref[c, pl.ds(r, tm // dil, stride=dil), :] = blk_ref[0, r, :, c * LANES:(c + 1) * LANES].astype(F32)
    return jnp.concatenate([u_ref[c] for c in range(ATTN_W // LANES)], axis=1)


def _mix_body(o1, o2, o3, l1, l2, l3, g_ref, x_ref, wo_ref, n2_ref, rwh_ref, rwl_ref, rb_ref,
              x1_ref, h_ref, m_ref, slab_ref, tcnt_ref, u_ref, *, tm):
    ls = [_natural_rows(l, u_ref, d, tm) for l, d in zip((l1, l2, l3), DILATIONS)]
    m = jnp.maximum(jnp.maximum(ls[0], ls[1]), ls[2])
    es = [jnp.exp(l - m) for l in ls]
    a = es[0] * _natural_rows(o1, u_ref, DILATIONS[0], tm)
    a = a + es[1] * _natural_rows(o2, u_ref, DILATIONS[1], tm)
    a = a + es[2] * _natural_rows(o3, u_ref, DILATIONS[2], tm)
    a = a / (es[0] + es[1] + es[2])
    mix = (jnp.dot(a.astype(BF16), wo_ref[:ATTN_W, :], preferred_element_type=F32)
           + jnp.dot(g_ref[...], wo_ref[ATTN_W:, :], preferred_element_type=F32))
    x1 = x_ref[...] + mix
    x1_ref[...] = x1
    h = x1 * lax.rsqrt(jnp.mean(x1 * x1, axis=-1, keepdims=True) + NORM_EPS) * n2_ref[...]
    h_ref[...] = h

    hh, hl, _ = _split3(h)
    dg = lambda u, v: lax.dot_general(u, v, NT_DIMS, preferred_element_type=F32)
    logits = dg(rwh_ref[...], hh) + dg(rwl_ref[...], hh) + dg(rwh_ref[...], hl) + rb_ref[...]
    ie = lax.broadcasted_iota(I32, (N_EXPERTS, tm), 0)
    cur = logits
    vals, idxs = [], []
    for _ in range(TOP_K):
        mx = jnp.max(cur, axis=0, keepdims=True)
        ix = jnp.min(jnp.where(cur == mx, ie, N_EXPERTS), axis=0, keepdims=True)
        vals.append(mx)
        idxs.append(ix)
        cur = jnp.where(ie == ix, -jnp.inf, cur)
    ex = [jnp.exp(v - vals[0]) for v in vals]
    den = ex[0] + ex[1] + ex[2] + ex[3]
    gates = [e / den for e in ex]

    onehots = [ie == ix for ix in idxs]
    chosen = (onehots[0] | onehots[1] | onehots[2] | onehots[3]).astype(F32)
    tr = lax.broadcasted_iota(I32, (tm, tm), 0)
    tc = lax.broadcasted_iota(I32, (tm, tm), 1)
    before = jnp.dot(chosen.astype(BF16), (tr < tc).astype(BF16), preferred_element_type=F32)
    count = jnp.broadcast_to(jnp.sum(chosen, axis=1, keepdims=True), (N_EXPERTS, LANES))
    tcnt_ref[0] = count
    run = jnp.floor((count + (RUN_ALIGN - 1.0)) * (1.0 / RUN_ALIGN)) * RUN_ALIGN
    er = lax.broadcasted_iota(I32, (N_EXPERTS, N_EXPERTS), 0)
    ec = lax.broadcasted_iota(I32, (N_EXPERTS, N_EXPERTS), 1)
    run_start = jnp.dot((ec < er).astype(BF16), run.astype(BF16), preferred_element_type=F32)
    where_ = before + run_start[:, 0:1]
    locs = [jnp.sum(jnp.where(oh, where_, 0.0), axis=0, keepdims=True) for oh in onehots]

    def pack(rows, n):
        sub = lax.broadcasted_iota(I32, (n, tm), 0)
        out = jnp.zeros((n, tm), rows[0].dtype)
        for k, row in enumerate(rows):
            out = jnp.where(sub == k, row, out)
        return out

    m_ref[...] = pack([l.astype(I32) for l in locs], 8)
    slab = pack(gates + locs, 8)
    slab_ref[...] = jnp.concatenate([slab, jnp.zeros((LANES - 8, tm), F32)], axis=0).T


def _mix(outs, lses, g, x2, B, S, w_out, norm2_g, router_w, router_b):
    T, D = x2.shape
    tm = ROUTE_TILE
    nt = S // tm
    wo = w_out.astype(BF16)
    rwt = router_w.T
    rwh = rwt.astype(BF16)
    rwl = (rwt - rwh.astype(F32)).astype(BF16)
    rb = router_b.reshape(N_EXPERTS, 1)
    n2 = norm2_g.reshape(1, D)
    row = lambda c: pl.BlockSpec((tm, c), lambda b, i: (b * nt + i, 0))
    full = lambda a: pl.BlockSpec(a.shape, lambda b, i: (0,) * a.ndim)
    col = pl.BlockSpec((8, tm), lambda b, i: (0, b * nt + i))
    res = [pl.BlockSpec((1, d, tm // d, ATTN_W), lambda b, i: (b, 0, i, 0)) for d in DILATIONS]
    return pl.pallas_call(
        functools.partial(_mix_body, tm=tm),
        grid=(B, nt),
        in_specs=res + res + [row(GLA_VAL_W), row(D), full(wo), full(n2), full(rwh), full(rwl), full(rb)],
        out_specs=[row(D), row(D), col, row(LANES),
                   pl.BlockSpec((1, N_EXPERTS, LANES), lambda b, i: (b * nt + i, 0, 0))],
        out_shape=[jax.ShapeDtypeStruct((T, D), F32),
                   jax.ShapeDtypeStruct((T, D), F32),
                   jax.ShapeDtypeStruct((8, T), I32),
                   jax.ShapeDtypeStruct((T, LANES), F32),
                   jax.ShapeDtypeStruct((T // tm, N_EXPERTS, LANES), F32)],
        scratch_shapes=[pltpu.VMEM((ATTN_W // LANES, tm, LANES), F32)],
        compiler_params=pltpu.CompilerParams(dimension_semantics=("arbitrary", "arbitrary"),
                                             vmem_limit_bytes=48 * 2**20),
        name="mix_router",
    )(*outs, *lses, g, x2, wo, n2, rwh, rwl, rb)


def _run_groups(c):
    return (c + (RUN_ALIGN - 1)) >> RUN_SHIFT


def _plan_body(tc_ref, rs_ref, p0_ref, pn_ref, be_ref, nu_ref, *, nt, nblk):
    shift = MOE_BLOCK.bit_length() - 1
    start = jnp.int32(0)
    for ex in range(N_EXPERTS):
        def run(i, tot, start=start, ex=ex):
            rs_ref[i, ex] = start + tot
            return tot + (_run_groups(tc_ref[i, ex]) << RUN_SHIFT)

        tot = lax.fori_loop(0, nt, run, jnp.int32(0))
        nb = (tot + (MOE_BLOCK - 1)) >> shift
        p0_ref[ex] = start + tot
        pn_ref[ex] = ((nb << shift) - tot) >> RUN_SHIFT
        b0 = start >> shift

        def fill(j, carry, b0=b0, ex=ex):
            be_ref[b0 + j] = jnp.int32(ex)
            return carry

        lax.fori_loop(0, nb, fill, 0)
        start = start + (nb << shift)
    used = start >> shift

    def tail(j, carry):
        be_ref[j] = jnp.int32(N_EXPERTS - 1)
        return carry

    lax.fori_loop(used, nblk, tail, 0)
    nu_ref[0] = used


def _plan(tcnt, nblk):
    nt = tcnt.shape[0]
    smem = pl.BlockSpec(memory_space=pltpu.SMEM)
    return pl.pallas_call(
        functools.partial(_plan_body, nt=nt, nblk=nblk),
        in_specs=[smem],
        out_specs=[smem] * 5,
        out_shape=[jax.ShapeDtypeStruct(tcnt.shape, I32),
                   jax.ShapeDtypeStruct((N_EXPERTS,), I32),
                   jax.ShapeDtypeStruct((N_EXPERTS,), I32),
                   jax.ShapeDtypeStruct((nblk,), I32),
                   jax.ShapeDtypeStruct((1,), I32)],
        name="plan",
    )(tcnt)


def _for_each_run(tc_ref, rs_ref, i, fn):
    start = jnp.int32(0)
    for ex in range(N_EXPERTS):
        ng = _run_groups(tc_ref[i, ex])
        g0 = rs_ref[i, ex]

        def body(q, carry, start=start, g0=g0):
            fn(pl.multiple_of(start + q * RUN_ALIGN, RUN_ALIGN), pl.multiple_of(g0 + q * RUN_ALIGN, RUN_ALIGN))
            return carry

        lax.fori_loop(0, ng, body, 0)
        start = start + (ng << RUN_SHIFT)


def _group_count(tc_ref, i):
    n = jnp.int32(0)
    for ex in range(N_EXPERTS):
        n = n + _run_groups(tc_ref[i, ex])
    return n


def _wait_groups(n, src_ref, dst_ref, sem):
    def waits(count, rows):
        def body(q, carry):
            pltpu.make_async_copy(src_ref.at[pl.ds(0, rows)], dst_ref.at[pl.ds(0, rows)], sem).wait()
            return carry
        lax.fori_loop(0, count, body, 0)

    waits(n >> 6, 64 * RUN_ALIGN)
    waits((n >> 3) & 7, 8 * RUN_ALIGN)
    waits(n & 7, RUN_ALIGN)


def _dispatch_body(tc_ref, rs_ref, p0_ref, pn_ref, nu_ref, m_ref, h_ref, xin_ref, g_ref, z_ref, sem, zsem, *, tm, R):
    i = pl.program_id(0)
    slot = i % 2
    rows = m_ref[...]
    jj = lax.broadcasted_iota(I32, (R, tm), 0)
    hit = (jj == rows[0:1]) | (jj == rows[1:2]) | (jj == rows[2:3]) | (jj == rows[3:4])
    g_ref[slot] = jnp.dot(jnp.where(hit, 1.0, 0.0).astype(BF16), h_ref[...].astype(BF16),
                          preferred_element_type=F32).astype(BF16)

    group = lambda ref, r: ref.at[pl.ds(r, RUN_ALIGN)]

    @pl.when(i > 0)
    def _():
        _wait_groups(_group_count(tc_ref, i - 1), g_ref.at[1 - slot], xin_ref, sem.at[1 - slot])

    _for_each_run(tc_ref, rs_ref, i,
                  lambda lr, gr: pltpu.make_async_copy(group(g_ref.at[slot], lr), group(xin_ref, gr), sem.at[slot]).start())

    @pl.when(i == pl.num_programs(0) - 1)
    def _():
        _wait_groups(_group_count(tc_ref, i), g_ref.at[slot], xin_ref, sem.at[slot])

    @pl.when(i == 0)
    def _():
        z_ref[...] = jnp.zeros_like(z_ref)
        total = jnp.int32(0)
        for ex in range(N_EXPERTS):
            def body(q, carry, ex=ex):
                pltpu.make_async_copy(group(z_ref, 0), group(xin_ref, pl.multiple_of(p0_ref[ex] + q * RUN_ALIGN, RUN_ALIGN)), zsem).start()
                return carry
            lax.fori_loop(0, pn_ref[ex], body, 0)
            total = total + pn_ref[ex]
        _wait_groups(total, z_ref, xin_ref, zsem)

        block = lambda j: xin_ref.at[pl.ds(pl.multiple_of(j * MOE_BLOCK, MOE_BLOCK), MOE_BLOCK)]
        n_blocks = xin_ref.shape[0] // MOE_BLOCK

        def fill(j, carry):
            pltpu.make_async_copy(z_ref, block(j), zsem).start()
            return carry

        def fill_wait(j, carry):
            pltpu.make_async_copy(z_ref, block(j), zsem).wait()
            return carry

        lax.fori_loop(nu_ref[0], n_blocks, fill, 0)
        lax.fori_loop(nu_ref[0], n_blocks, fill_wait, 0)


def _dispatch(tcnt, rowstart, pad0, padn, n_used, m_t, h, n_rows):
    T, D = h.shape
    tm = ROUTE_TILE
    col = pl.BlockSpec((8, tm), lambda i, *_: (0, i))
    return pl.pallas_call(
        functools.partial(_dispatch_body, tm=tm, R=GROUP_ROWS),
        grid_spec=pltpu.PrefetchScalarGridSpec(
            num_scalar_prefetch=5,
            grid=(T // tm,),
            in_specs=[col, pl.BlockSpec((tm, D), lambda i, *_: (i, 0))],
            out_specs=pl.BlockSpec(memory_space=pl.ANY),
            scratch_shapes=[pltpu.VMEM((2, GROUP_ROWS, D), BF16), pltpu.VMEM((MOE_BLOCK, D), BF16),
                            pltpu.SemaphoreType.DMA((2,)), pltpu.SemaphoreType.DMA(())]),
        out_shape=jax.ShapeDtypeStruct((n_rows, D), BF16),
        compiler_params=pltpu.CompilerParams(dimension_semantics=("arbitrary",), vmem_limit_bytes=56 * 2**20),
        name="dispatch",
    )(tcnt, rowstart, pad0, padn, n_used, m_t, h)


def _w1_split_body(w_ref, g_ref, l_ref, t_ref, *, tf):
    wt = w_ref[0].T
    for c in range(wt.shape[1] // LANES):
        cols = slice(c * LANES, (c + 1) * LANES)
        t_ref[c] = wt[:, cols]
        g_ref[0, :, cols] = t_ref[c, pl.ds(0, tf // 2, stride=2), :].astype(BF16)
        l_ref[0, :, cols] = t_ref[c, pl.ds(1, tf // 2, stride=2), :].astype(BF16)


def _w1_split(w1):
    E, D, F2 = w1.shape
    tf = 512
    out_spec = pl.BlockSpec((1, tf // 2, D), lambda e, j: (e, j, 0))
    return pl.pallas_call(
        functools.partial(_w1_split_body, tf=tf),
        grid=(E, F2 // tf),
        in_specs=[pl.BlockSpec((1, D, tf), lambda e, j: (e, 0, j))],
        out_specs=[out_spec, out_spec],
        out_shape=[jax.ShapeDtypeStruct((E, F2 // 2, D), BF16)] * 2,
        scratch_shapes=[pltpu.VMEM((D // LANES, tf, LANES), F32)],
        compiler_params=pltpu.CompilerParams(dimension_semantics=("arbitrary", "arbitrary")),
        name="w1_split",
    )(w1)


def _moe_body(be_ref, nu_ref, x_ref, w1g_ref, w1l_ref, b1g_ref, b1l_ref, w2_ref, b2_ref, y_ref):
    del be_ref
    live = pl.program_id(0) < nu_ref[0]

    @pl.when(live)
    def _():
        x = x_ref[...]
        hg = lax.dot_general(x, w1g_ref[0], NT_DIMS, preferred_element_type=F32) + b1g_ref[0]
        hl = lax.dot_general(x, w1l_ref[0], NT_DIMS, preferred_element_type=F32) + b1l_ref[0]
        xg = jnp.minimum(hg, SWIGLU_LIMIT)
        xl = jnp.clip(hl, -SWIGLU_LIMIT, SWIGLU_LIMIT)
        act = xg * _sigmoid(SWIGLU_ALPHA * xg) * (xl + 1.0)
        y = jnp.dot(act.astype(BF16), w2_ref[0].astype(BF16), preferred_element_type=F32) + b2_ref[0]
        y_ref[...] = y.astype(BF16)

    @pl.when(jnp.logical_not(live))
    def _():
        y_ref[...] = jnp.zeros_like(y_ref)


def _moe(block_e, n_used, xin, w1, b1, w2, b2):
    n_rows = xin.shape[0]
    D = w1.shape[1]
    F = w2.shape[1]
    nblk = n_rows // MOE_BLOCK
    w1g, w1l = _w1_split(w1)
    b1g = b1[:, 0::2].reshape(N_EXPERTS, 1, F)
    b1l = b1[:, 1::2].reshape(N_EXPERTS, 1, F)
    b2r = b2.reshape(N_EXPERTS, 1, D)
    xspec = pl.BlockSpec((MOE_BLOCK, D), lambda j, be, nu: (jnp.minimum(j, nu[0] - 1), 0))
    yspec = pl.BlockSpec((MOE_BLOCK, D), lambda j, be, nu: (j, 0))
    wspec = lambda a: pl.BlockSpec((1,) + a.shape[1:], lambda j, be, nu: (be[j], 0, 0))
    return pl.pallas_call(
        _moe_body,
        grid_spec=pltpu.PrefetchScalarGridSpec(
            num_scalar_prefetch=2,
            grid=(nblk,),
            in_specs=[xspec, wspec(w1g), wspec(w1l), wspec(b1g), wspec(b1l), wspec(w2), wspec(b2r)],
            out_specs=yspec),
        out_shape=jax.ShapeDtypeStruct(xin.shape, BF16),
        compiler_params=pltpu.CompilerParams(dimension_semantics=("arbitrary",),
                                             vmem_limit_bytes=48 * 2**20),
        name="moe",
    )(block_e, n_used, xin, w1g, w1l, b1g, b1l, w2, b2r)


def _combine_body(tc_ref, rs_ref, slab_ref, y_ref, x1_ref, fg_ref, out_ref, g_ref, sem, *, tm, R):
    i = pl.program_id(0)
    slot = i % 2
    group = lambda ref, r: ref.at[pl.ds(r, RUN_ALIGN)]

    def fetch(tile, slot):
        _for_each_run(tc_ref, rs_ref, tile,
                      lambda lr, gr: pltpu.make_async_copy(group(y_ref, gr), group(g_ref.at[slot], lr), sem.at[slot]).start())

    @pl.when(i == 0)
    def _():
        g_ref[...] = jnp.zeros_like(g_ref)
        fetch(i, slot)

    @pl.when(i + 1 < pl.num_programs(0))
    def _():
        fetch(i + 1, 1 - slot)

    slab = slab_ref[...]
    gates = slab[:, 0:TOP_K]
    rows = slab[:, TOP_K:2 * TOP_K].astype(I32)
    jj = lax.broadcasted_iota(I32, (tm, R), 1)
    sel = jnp.zeros((tm, R), F32)
    for k in range(TOP_K):
        sel = jnp.where(jj == rows[:, k:k + 1], gates[:, k:k + 1], sel)

    _wait_groups(_group_count(tc_ref, i), y_ref, g_ref.at[slot], sem.at[slot])
    acc = x1_ref[...] + jnp.dot(sel.astype(BF16), g_ref[slot], preferred_element_type=F32)
    out_ref[...] = acc * lax.rsqrt(jnp.mean(acc * acc, axis=-1, keepdims=True) + NORM_EPS) * fg_ref[...]


def _combine(tcnt, rowstart, slab, y, x1, final_g):
    T, D = x1.shape
    tm = ROUTE_TILE
    fg = final_g.reshape(1, D)
    return pl.pallas_call(
        functools.partial(_combine_body, tm=tm, R=GROUP_ROWS),
        grid_spec=pltpu.PrefetchScalarGridSpec(
            num_scalar_prefetch=2,
            grid=(T // tm,),
            in_specs=[pl.BlockSpec((tm, LANES), lambda i, *_: (i, 0)),
                      pl.BlockSpec(memory_space=pl.ANY),
                      pl.BlockSpec((tm, D), lambda i, *_: (i, 0)),
                      pl.BlockSpec((1, D), lambda i, *_: (0, 0))],
            out_specs=pl.BlockSpec((tm, D), lambda i, *_: (i, 0)),
            scratch_shapes=[pltpu.VMEM((2, GROUP_ROWS, D), BF16), pltpu.SemaphoreType.DMA((2,))]),
        out_shape=jax.ShapeDtypeStruct((T, D), F32),
        compiler_params=pltpu.CompilerParams(dimension_semantics=("arbitrary",), vmem_limit_bytes=56 * 2**20),
        name="combine",
    )(tcnt, rowstart, slab, y, x1, fg)


def _mixers(x2, B, S, norm1_g, w_in, rel_bias, wg_f, bg_f, wg_b, bg_b, gla_norm_g):
    a1, a4, a16, gla, bcum = _in_proj(x2, B, S, norm1_g, w_in, wg_f, bg_f, wg_b, bg_b)
    branches = [_attn_branch(a, rel_bias, d) for a, d in zip((a1, a4, a16), DILATIONS)]
    g = _gla(gla, bcum, gla_norm_g, B, S)
    return [o for o, _ in branches], [l for _, l in branches], g


def _moe_layer(h, m_t, slab, tcnt_f, x1, w1, b1, w2, b2, final_g):
    T = x1.shape[0]
    nt = T // ROUTE_TILE
    nblk = (T * TOP_K + nt * N_EXPERTS * (RUN_ALIGN - 1)) // MOE_BLOCK + N_EXPERTS
    tcnt = tcnt_f[:, :, 0].astype(I32)
    rowstart, pad0, padn, block_e, n_used = _plan(tcnt, nblk)
    xin = _dispatch(tcnt, rowstart, pad0, padn, n_used, m_t, h, nblk * MOE_BLOCK)
    y = _moe(block_e, n_used, xin, w1, b1, w2, b2)
    return _combine(tcnt, rowstart, slab, y, x1, final_g)


def kernel(x, norm1_g, w_in, rel_bias, gla_wg_fwd, gla_bg_fwd, gla_wg_bwd, gla_bg_bwd, gla_norm_g, w_out, norm2_g, router_w, router_b, moe_w1, moe_b1, moe_w2, moe_b2, final_g):
    B, S, D = x.shape
    assert w_in.shape[0] == 1, "one layer"
    x2 = x.reshape(B * S, D)
    outs, lses, g = _mixers(x2, B, S, norm1_g[0], w_in[0], rel_bias, gla_wg_fwd[0], gla_bg_fwd[0],
                            gla_wg_bwd[0], gla_bg_bwd[0], gla_norm_g[0])
    x1, h, m_t, slab, tcnt = _mix(outs, lses, g, x2, B, S, w_out[0], norm2_g[0], router_w[0], router_b[0])
    out = _moe_layer(h, m_t, slab, tcnt, x1, moe_w1[0], moe_b1[0], moe_w2[0], moe_b2[0], final_g)
    return out.reshape(B, S, D)
```

```python
import functools

import jax
import jax.numpy as jnp
import numpy as np
from jax import lax
from jax.experimental import pallas as pl
from jax.experimental.pallas import tpu as pltpu

F32, BF16, I32 = jnp.float32, jnp.bfloat16, jnp.int32

LANES = 128
HEAD_DIM = 64
ATTN_HEADS = 8
ATTN_W = ATTN_HEADS * HEAD_DIM
HALF_SPAN = 64
ATTN_QBLK = 128
ATTN_KWIN = 256
DILATIONS = (1, 4, 16)
T5_BUCKETS = 32
T5_MAX_DISTANCE = 1024
NEG_INF = -1e30
GLA_KEY_W = 256
GLA_VAL_W = 512
GLA_DV = 128
GLA_RANK = 16
GLA_TAU = 16.0
GLA_CHUNK = 64
N_EXPERTS = 32
TOP_K = 4
SWIGLU_LIMIT = 7.0
SWIGLU_ALPHA = 1.702
NORM_EPS = 1e-5
MOE_BLOCK = 512
W1_SUBTILE = 512
ROUTE_TILE = 512
RUN_ALIGN = 16
RUN_SHIFT = 4
GROUP_ROWS = ROUTE_TILE * TOP_K + N_EXPERTS * RUN_ALIGN

ATTN_COLS = 3 * ATTN_W
GLA_COLS = 2 * GLA_KEY_W + 2 * GLA_VAL_W
Z_COLS = LANES
PROJ_COLS = ATTN_COLS + GLA_COLS + Z_COLS

NT_DIMS = (((1,), (1,)), ((), ()))


def _sigmoid(x):
    return 1.0 / (1.0 + jnp.exp(-x))


def _split3(a):
    h1 = a.astype(BF16)
    r1 = a - h1.astype(F32)
    h2 = r1.astype(BF16)
    h3 = (r1 - h2.astype(F32)).astype(BF16)
    return h1, h2, h3


def _dot_f32(a, b):
    ah, al, _ = _split3(a)
    bh, bl, _ = _split3(b)
    d = lambda u, v: jnp.dot(u, v, preferred_element_type=F32)
    return d(ah, bh) + d(al, bh) + d(ah, bl)


def _split_w1_tile(w_ref, g_ref, l_ref, t_ref):
    sub = t_ref.shape[1]
    for s in range(w_ref.shape[2] // sub):
        wt = w_ref[0, :, s * sub:(s + 1) * sub].T
        out_rows = slice(s * sub // 2, (s + 1) * sub // 2)
        for c in range(wt.shape[1] // LANES):
            cols = slice(c * LANES, (c + 1) * LANES)
            t_ref[c] = wt[:, cols]
            g_ref[0, out_rows, cols] = t_ref[c, pl.ds(0, sub // 2, stride=2), :].astype(BF16)
            l_ref[0, out_rows, cols] = t_ref[c, pl.ds(1, sub // 2, stride=2), :].astype(BF16)


def _in_proj_body(x_ref, g_ref, w_ref, wg_ref, bg_ref, w1_ref, a1_ref, a4_ref, a16_ref, gla_ref, b_ref,
                  w1g_ref, w1l_ref, p_ref, t_ref, *, tm):
    _split_w1_tile(w1_ref, w1g_ref, w1l_ref, t_ref)

    x = x_ref[...]
    n = x * lax.rsqrt(jnp.mean(x * x, axis=-1, keepdims=True) + NORM_EPS) * g_ref[...]
    p = jnp.dot(n.astype(BF16), w_ref[...], preferred_element_type=F32)
    gla_ref[...] = p[:, ATTN_COLS:ATTN_COLS + GLA_COLS].astype(BF16)

    a1_ref[0, 0] = p[:, :ATTN_COLS].astype(BF16)
    for c in range(ATTN_COLS // LANES):
        p_ref[c] = p[:, c * LANES:(c + 1) * LANES]
    for dil, a_ref in ((DILATIONS[1], a4_ref), (DILATIONS[2], a16_ref)):
        for r in range(dil):
            for c in range(ATTN_COLS // LANES):
                a_ref[0, r, :, c * LANES:(c + 1) * LANES] = p_ref[c, pl.ds(r, tm // dil, stride=dil), :].astype(BF16)

    z = p[:, ATTN_COLS + GLA_COLS:]
    zz = _dot_f32(z, wg_ref[...]) + bg_ref[...]
    la = (jnp.minimum(zz, 0.0) - jnp.log1p(jnp.exp(-jnp.abs(zz)))) * (1.0 / GLA_TAU)
    C = GLA_CHUNK
    r_i = lax.broadcasted_iota(I32, (C, C), 0)
    c_i = lax.broadcasted_iota(I32, (C, C), 1)
    lower = (r_i >= c_i).astype(BF16)
    upper = (r_i <= c_i).astype(BF16)
    W = GLA_KEY_W
    for j in range(tm // C):
        rows = slice(j * C, (j + 1) * C)
        for tri, cols in ((lower, slice(0, W)), (upper, slice(W, 2 * W))):
            terms = jnp.concatenate(_split3(la[rows, cols]), axis=1)
            cs = jnp.dot(tri, terms, preferred_element_type=F32)
            b_ref[rows, cols] = cs[:, :W] + cs[:, W:2 * W] + cs[:, 2 * W:]


def _in_proj(x2, B, S, norm_g, w_in, wg_f, bg_f, wg_b, bg_b, w1):
    T, D = x2.shape
    tm = 256
    nt = S // tm
    E, _, F2 = w1.shape
    w1_cols = E * F2 // (B * nt)
    per_e = F2 // w1_cols
    assert w1_cols * B * nt == E * F2 and per_e * w1_cols == F2 and w1_cols % W1_SUBTILE == 0
    w1_in = pl.BlockSpec((1, D, w1_cols), lambda b, i: ((b * nt + i) // per_e, 0, (b * nt + i) % per_e))
    w1_out = pl.BlockSpec((1, w1_cols // 2, D), lambda b, i: ((b * nt + i) // per_e, (b * nt + i) % per_e, 0))
    w = jnp.pad(w_in.astype(BF16), ((0, 0), (0, PROJ_COLS - w_in.shape[1])))
    wg = jnp.zeros((Z_COLS, 2 * GLA_KEY_W), F32)
    wg = wg.at[:GLA_RANK, :GLA_KEY_W].set(wg_f).at[GLA_RANK:2 * GLA_RANK, GLA_KEY_W:].set(wg_b)
    bg = jnp.concatenate([bg_f, bg_b]).reshape(1, 2 * GLA_KEY_W)
    row = lambda c: pl.BlockSpec((tm, c), lambda b, i: (b * nt + i, 0))
    full = lambda a: pl.BlockSpec(a.shape, lambda b, i: (0,) * a.ndim)
    res = lambda d: pl.BlockSpec((1, d, tm // d, ATTN_COLS), lambda b, i: (b, 0, i, 0))
    res_shape = lambda d: jax.ShapeDtypeStruct((B, d, S // d, ATTN_COLS), BF16)
    g2 = norm_g.reshape(1, D)
    return pl.pallas_call(
        functools.partial(_in_proj_body, tm=tm),
        grid=(B, nt),
        in_specs=[row(D), full(g2), full(w), full(wg), full(bg), w1_in],
        out_specs=[res(d) for d in DILATIONS] + [row(GLA_COLS), row(2 * GLA_KEY_W), w1_out, w1_out],
        out_shape=[res_shape(d) for d in DILATIONS]
                  + [jax.ShapeDtypeStruct((T, GLA_COLS), BF16), jax.ShapeDtypeStruct((T, 2 * GLA_KEY_W), F32)]
                  + [jax.ShapeDtypeStruct((E, F2 // 2, D), BF16)] * 2,
        scratch_shapes=[pltpu.VMEM((ATTN_COLS // LANES, tm, LANES), F32),
                        pltpu.VMEM((D // LANES, W1_SUBTILE, LANES), F32)],
        compiler_params=pltpu.CompilerParams(dimension_semantics=("arbitrary", "arbitrary"),
                                             vmem_limit_bytes=56 * 2**20),
        name="in_proj",
    )(x2, g2, w, wg, bg, w1)


def _t5_bucket(rel):
    nb = T5_BUCKETS // 2
    max_exact = nb // 2
    n = np.abs(rel)
    large = max_exact + (np.log(np.maximum(n, 1).astype(np.float32) / max_exact)
                         / np.log(T5_MAX_DISTANCE / max_exact) * (nb - max_exact)).astype(np.int32)
    large = np.minimum(large, nb - 1)
    return (np.where(rel > 0, nb, 0) + np.where(n < max_exact, n, large)).astype(np.int32)


def _bucket_tables(dil):
    s = np.arange(ATTN_QBLK)[:, None]
    t = np.arange(ATTN_KWIN)[None, :]
    tabs = []
    for c in (0, HALF_SPAN, 2 * HALF_SPAN):
        off = t - c - s
        tabs.append(np.where(np.abs(off) <= HALF_SPAN, _t5_bucket(off * dil), -1))
    return np.stack(tabs).astype(np.int32)


def _attn_body(rb_ref, idx_ref, q_ref, k_ref, v_ref, o_ref, lse_ref, tab_ref, *, L, TQ):
    first = (pl.program_id(0) == 0) & (pl.program_id(1) == 0) & (pl.program_id(2) == 0)

    @pl.when(first)
    def _():
        for v in range(3):
            idx = idx_ref[v]
            for h in range(ATTN_HEADS):
                fill = lambda j, t, h=h, idx=idx: jnp.where(idx == j, rb_ref[j, h], t)
                tab_ref[v, h] = lax.fori_loop(0, T5_BUCKETS, fill, jnp.where(idx < 0, NEG_INF, 0.0).astype(F32))

    i = pl.program_id(2)
    lo = lax.broadcasted_iota(I32, (ATTN_QBLK, LANES), 1) < HEAD_DIM
    scale = HEAD_DIM ** -0.5
    for sb in range(TQ // ATTN_QBLK):
        l0 = i * TQ + sb * ATTN_QBLK
        ws = pl.multiple_of(jnp.clip(l0 - HALF_SPAN, 0, L - ATTN_KWIN), HALF_SPAN)
        var = jnp.where(l0 == 0, 0, jnp.where(l0 == L - ATTN_QBLK, 2, 1))
        rows = slice(sb * ATTN_QBLK, (sb + 1) * ATTN_QBLK)
        for hp in range(ATTN_HEADS // 2):
            cols = slice(hp * LANES, (hp + 1) * LANES)
            q2 = q_ref[rows, cols]
            k2 = k_ref[pl.ds(ws, ATTN_KWIN), cols]
            v2 = v_ref[pl.ds(ws, ATTN_KWIN), cols]
            zero = jnp.zeros_like(q2)
            qs = jnp.concatenate([jnp.where(lo, q2, zero), jnp.where(lo, zero, q2)], axis=0)
            s = lax.dot_general(qs, k2, NT_DIMS, preferred_element_type=F32)
            bias = jnp.concatenate([tab_ref[var, 2 * hp], tab_ref[var, 2 * hp + 1]], axis=0)
            logits = s * scale + bias
            m = jnp.max(logits, axis=-1, keepdims=True)
            p = jnp.exp(logits - m)
            den = jnp.sum(p, axis=-1, keepdims=True)
            pv = jnp.dot(p.astype(BF16), v2, preferred_element_type=F32) / den
            lse = jnp.broadcast_to(m + jnp.log(den), (2 * ATTN_QBLK, LANES))
            o_ref[rows, cols] = jnp.where(lo, pv[:ATTN_QBLK], pv[ATTN_QBLK:]).astype(BF16)
            lse_ref[rows, cols] = jnp.where(lo, lse[:ATTN_QBLK], lse[ATTN_QBLK:])


def _attn_branch(attn, rel_bias, dil):
    B, _, L, _ = attn.shape
    TQ = 256
    idx = jnp.asarray(_bucket_tables(dil))
    body = functools.partial(_attn_body, L=L, TQ=TQ)
    kv = lambda j: pl.BlockSpec((None, None, L, ATTN_W), lambda b, r, i: (b, r, 0, j))
    q_spec = pl.BlockSpec((None, None, TQ, ATTN_W), lambda b, r, i: (b, r, i, 0))
    return pl.pallas_call(
        body,
        grid=(B, dil, L // TQ),
        in_specs=[pl.BlockSpec(memory_space=pltpu.SMEM),
                  pl.BlockSpec(idx.shape, lambda b, r, i: (0, 0, 0)),
                  q_spec, kv(1), kv(2)],
        out_specs=[q_spec, q_spec],
        out_shape=[jax.ShapeDtypeStruct((B, dil, L, ATTN_W), BF16),
                   jax.ShapeDtypeStruct((B, dil, L, ATTN_W), F32)],
        scratch_shapes=[pltpu.VMEM((3, ATTN_HEADS, ATTN_QBLK, ATTN_KWIN), F32)],
        compiler_params=pltpu.CompilerParams(dimension_semantics=("arbitrary",) * 3,
                                             vmem_limit_bytes=48 * 2**20),
        name=f"attn_d{dil}",
    )(rel_bias, idx, attn, attn, attn)


def _gla_body(q_ref, k_ref, v_ref, gate_ref, bf_ref, bb_ref, ng_ref, out_ref, of_ref, ob_ref, st_ref, *, S):
    C = GLA_CHUNK
    nc = S // C
    r = lax.broadcasted_iota(I32, (C, LANES), 0)
    s2 = lax.broadcasted_iota(I32, (C, LANES), 1) % HEAD_DIM
    lo = lax.broadcasted_iota(I32, (C, LANES), 1) < HEAD_DIM
    blockdiag = ((lax.broadcasted_iota(I32, (2 * GLA_DV, LANES), 0) < GLA_DV)
                 == (lax.broadcasted_iota(I32, (2 * GLA_DV, LANES), 1) < HEAD_DIM))
    st_ref[...] = jnp.zeros_like(st_ref)

    def one_chunk(ci, direction):
        mask = (r >= s2) if direction == 0 else (r <= s2)
        b_ref, o_ref = (bf_ref, of_ref) if direction == 0 else (bb_ref, ob_ref)
        rows = pl.ds(pl.multiple_of(ci * C, C), C)
        q = q_ref[0, rows, :].astype(F32) * (HEAD_DIM ** -0.5)
        k = k_ref[0, rows, :].astype(F32)
        v = v_ref[0, rows, :]
        b = b_ref[0, rows, :]
        g_tot = b[C - 1:C, :] if direction == 0 else b[0:1, :]
        qd = (q * jnp.exp(b)).astype(BF16)
        kd = (k * jnp.exp(-b)).astype(BF16)
        kst = (k * jnp.exp(g_tot - b)).astype(BF16)
        zk = jnp.zeros_like(kd)
        ks = jnp.concatenate([jnp.where(lo, kd, zk), jnp.where(lo, zk, kd)], axis=0)
        a = lax.dot_general(qd, ks, NT_DIMS, preferred_element_type=F32)
        a = jnp.where(mask, a, 0.0).astype(BF16)
        zv = jnp.zeros((C, GLA_DV), BF16)
        vblk = jnp.concatenate([jnp.concatenate([v[:, :GLA_DV], zv], axis=1),
                                jnp.concatenate([zv, v[:, GLA_DV:]], axis=1)], axis=0)
        st = st_ref[direction]
        o_ref[rows, :] = (jnp.dot(a, vblk, preferred_element_type=F32)
                          + lax.dot_general(qd, st.astype(BF16), NT_DIMS, preferred_element_type=F32))
        d_st = jnp.dot(v.astype(F32).T.astype(BF16), kst, preferred_element_type=F32)
        st_ref[direction] = st * jnp.exp(g_tot) + jnp.where(blockdiag, d_st, 0.0)

    def step(i, carry):
        one_chunk(i, 0)
        one_chunk(nc - 1 - i, 1)
        return carry

    lax.fori_loop(0, nc, step, 0, unroll=2)

    tr = 256
    def epilogue(i, carry):
        rows = pl.ds(pl.multiple_of(i * tr, tr), tr)
        for hh in range(2):
            cols = slice(hh * GLA_DV, (hh + 1) * GLA_DV)
            o = of_ref[rows, cols] + ob_ref[rows, cols]
            o = o * lax.rsqrt(jnp.mean(o * o, axis=-1, keepdims=True) + NORM_EPS) * ng_ref[0, :, cols]
            gate = gate_ref[0, rows, cols].astype(F32)
            out_ref[0, rows, cols] = (o * (gate * _sigmoid(gate))).astype(BF16)
        return carry

    lax.fori_loop(0, S // tr, epilogue, 0)


def _gla(gla, bcum, norm_g, B, S):
    g3 = gla.reshape(B, S, GLA_COLS)
    b3 = bcum.reshape(B, S, 2 * GLA_KEY_W)
    ng = norm_g.reshape(2, 1, 2 * GLA_DV)
    body = functools.partial(_gla_body, S=S)
    spec = lambda w, off: pl.BlockSpec((1, S, w), lambda b, p: (b, 0, off + p))
    out = pl.pallas_call(
        body,
        grid=(B, 2),
        in_specs=[spec(LANES, 0),
                  spec(LANES, GLA_KEY_W // LANES),
                  spec(2 * GLA_DV, (2 * GLA_KEY_W) // (2 * GLA_DV)),
                  spec(2 * GLA_DV, (2 * GLA_KEY_W + GLA_VAL_W) // (2 * GLA_DV)),
                  spec(LANES, 0),
                  spec(LANES, GLA_KEY_W // LANES),
                  pl.BlockSpec((1, 1, 2 * GLA_DV), lambda b, p: (p, 0, 0))],
        out_specs=pl.BlockSpec((1, S, 2 * GLA_DV), lambda b, p: (b, 0, p)),
        out_shape=jax.ShapeDtypeStruct((B, S, GLA_VAL_W), BF16),
        scratch_shapes=[pltpu.VMEM((S, 2 * GLA_DV), F32), pltpu.VMEM((S, 2 * GLA_DV), F32),
                        pltpu.VMEM((2, 2 * GLA_DV, LANES), F32)],
        compiler_params=pltpu.CompilerParams(dimension_semantics=("arbitrary", "arbitrary"),
                                             vmem_limit_bytes=56 * 2**20),
        name="gla",
    )(g3, g3, g3, g3, b3, b3, ng)
    return out.reshape(B * S, GLA_VAL_W)


def _natural_rows(blk_ref, u_ref, dil, tm):
    if dil == 1:
        return blk_ref[0, 0].astype(F32)
    for r in range(dil):
        for c in range(ATTN_W // LANES):
            u_ref[c, pl.ds(r, tm // dil, stride=dil), :] = blk_ref[0, r, :, c * LANES:(c + 1) * LANES].astype(F32)
    return jnp.concatenate([u_ref[c] for c in range(ATTN_W // LANES)], axis=1)


def _mix_body(o1, o2, o3, l1, l2, l3, g_ref, x_ref, wo_ref, n2_ref, rwh_ref, rwl_ref, rb_ref,
              x1_ref, h_ref, m_ref, slab_ref, tcnt_ref, u_ref, *, tm):
    ls = [_natural_rows(l, u_ref, d, tm) for l, d in zip((l1, l2, l3), DILATIONS)]
    m = jnp.maximum(jnp.maximum(ls[0], ls[1]), ls[2])
    es = [jnp.exp(l - m) for l in ls]
    a = es[0] * _natural_rows(o1, u_ref, DILATIONS[0], tm)
    a = a + es[1] * _natural_rows(o2, u_ref, DILATIONS[1], tm)
    a = a + es[2] * _natural_rows(o3, u_ref, DILATIONS[2], tm)
    a = a / (es[0] + es[1] + es[2])
    mix = (jnp.dot(a.astype(BF16), wo_ref[:ATTN_W, :], preferred_element_type=F32)
           + jnp.dot(g_ref[...], wo_ref[ATTN_W:, :], preferred_element_type=F32))
    x1 = x_ref[...] + mix
    x1_ref[...] = x1
    h = x1 * lax.rsqrt(jnp.mean(x1 * x1, axis=-1, keepdims=True) + NORM_EPS) * n2_ref[...]
    h_ref[...] = h

    hh, hl, _ = _split3(h)
    dg = lambda u, v: lax.dot_general(u, v, NT_DIMS, preferred_element_type=F32)
    logits = dg(rwh_ref[...], hh) + dg(rwl_ref[...], hh) + dg(rwh_ref[...], hl) + rb_ref[...]
    ie = lax.broadcasted_iota(I32, (N_EXPERTS, tm), 0)
    cur = logits
    vals, idxs = [], []
    for _ in range(TOP_K):
        mx = jnp.max(cur, axis=0, keepdims=True)
        ix = jnp.min(jnp.where(cur == mx, ie, N_EXPERTS), axis=0, keepdims=True)
        vals.append(mx)
        idxs.append(ix)
        cur = jnp.where(ie == ix, -jnp.inf, cur)
    ex = [jnp.exp(v - vals[0]) for v in vals]
    den = ex[0] + ex[1] + ex[2] + ex[3]
    gates = [e / den for e in ex]

    onehots = [ie == ix for ix in idxs]
    chosen = (onehots[0] | onehots[1] | onehots[2] | onehots[3]).astype(F32)
    tr = lax.broadcasted_iota(I32, (tm, tm), 0)
    tc = lax.broadcasted_iota(I32, (tm, tm), 1)
    before = jnp.dot(chosen.astype(BF16), (tr < tc).astype(BF16), preferred_element_type=F32)
    count = jnp.broadcast_to(jnp.sum(chosen, axis=1, keepdims=True), (N_EXPERTS, LANES))
    tcnt_ref[0] = count
    run = jnp.floor((count + (RUN_ALIGN - 1.0)) * (1.0 / RUN_ALIGN)) * RUN_ALIGN
    er = lax.broadcasted_iota(I32, (N_EXPERTS, N_EXPERTS), 0)
    ec = lax.broadcasted_iota(I32, (N_EXPERTS, N_EXPERTS), 1)
    run_start = jnp.dot((ec < er).astype(BF16), run.astype(BF16), preferred_element_type=F32)
    where_ = before + run_start[:, 0:1]
    locs = [jnp.sum(jnp.where(oh, where_, 0.0), axis=0, keepdims=True) for oh in onehots]

    def pack(rows, n):
        sub = lax.broadcasted_iota(I32, (n, tm), 0)
        out = jnp.zeros((n, tm), rows[0].dtype)
        for k, row in enumerate(rows):
            out = jnp.where(sub == k, row, out)
        return out

    m_ref[...] = pack([l.astype(I32) for l in locs], 8)
    slab = pack(gates + locs, 8)
    slab_ref[...] = jnp.concatenate([slab, jnp.zeros((LANES - 8, tm), F32)], axis=0).T


def _mix(outs, lses, g, x2, B, S, w_out, norm2_g, router_w, router_b):
    T, D = x2.shape
    tm = ROUTE_TILE
    nt = S // tm
    wo = w_out.astype(BF16)
    rwt = router_w.T
    rwh = rwt.astype(BF16)
    rwl = (rwt - rwh.astype(F32)).astype(BF16)
    rb = router_b.reshape(N_EXPERTS, 1)
    n2 = norm2_g.reshape(1, D)
    row = lambda c: pl.BlockSpec((tm, c), lambda b, i: (b * nt + i, 0))
    full = lambda a: pl.BlockSpec(a.shape, lambda b, i: (0,) * a.ndim)
    col = pl.BlockSpec((8, tm), lambda b, i: (0, b * nt + i))
    res = [pl.BlockSpec((1, d, tm // d, ATTN_W), lambda b, i: (b, 0, i, 0)) for d in DILATIONS]
    return pl.pallas_call(
        functools.partial(_mix_body, tm=tm),
        grid=(B, nt),
        in_specs=res + res + [row(GLA_VAL_W), row(D), full(wo), full(n2), full(rwh), full(rwl), full(rb)],
        out_specs=[row(D), row(D), col, row(LANES),
                   pl.BlockSpec((1, N_EXPERTS, LANES), lambda b, i: (b * nt + i, 0, 0))],
        out_shape=[jax.ShapeDtypeStruct((T, D), F32),
                   jax.ShapeDtypeStruct((T, D), F32),
                   jax.ShapeDtypeStruct((8, T), I32),
                   jax.ShapeDtypeStruct((T, LANES), F32),
                   jax.ShapeDtypeStruct((T // tm, N_EXPERTS, LANES), F32)],
        scratch_shapes=[pltpu.VMEM((ATTN_W // LANES, tm, LANES), F32)],
        compiler_params=pltpu.CompilerParams(dimension_semantics=("arbitrary", "arbitrary"),
                                             vmem_limit_bytes=48 * 2**20),
        name="mix_router",
    )(*outs, *lses, g, x2, wo, n2, rwh, rwl, rb)


def _run_groups(c):
    return (c + (RUN_ALIGN - 1)) >> RUN_SHIFT


def _plan_body(tc_ref, rs_ref, p0_ref, pn_ref, be_ref, nu_ref, *, nt, nblk):
    shift = MOE_BLOCK.bit_length() - 1
    start = jnp.int32(0)
    for ex in range(N_EXPERTS):
        def run(i, tot, start=start, ex=ex):
            rs_ref[i, ex] = start + tot
            return tot + (_run_groups(tc_ref[i, ex]) << RUN_SHIFT)

        tot = lax.fori_loop(0, nt, run, jnp.int32(0))
        nb = (tot + (MOE_BLOCK - 1)) >> shift
        p0_ref[ex] = start + tot
        pn_ref[ex] = ((nb << shift) - tot) >> RUN_SHIFT
        b0 = start >> shift

        def fill(j, carry, b0=b0, ex=ex):
            be_ref[b0 + j] = jnp.int32(ex)
            return carry

        lax.fori_loop(0, nb, fill, 0)
        start = start + (nb << shift)
    used = start >> shift

    def tail(j, carry):
        be_ref[j] = jnp.int32(N_EXPERTS - 1)
        return carry

    lax.fori_loop(used, nblk, tail, 0)
    nu_ref[0] = used


def _plan(tcnt, nblk):
    nt = tcnt.shape[0]
    smem = pl.BlockSpec(memory_space=pltpu.SMEM)
    return pl.pallas_call(
        functools.partial(_plan_body, nt=nt, nblk=nblk),
        in_specs=[smem],
        out_specs=[smem] * 5,
        out_shape=[jax.ShapeDtypeStruct(tcnt.shape, I32),
                   jax.ShapeDtypeStruct((N_EXPERTS,), I32),
                   jax.ShapeDtypeStruct((N_EXPERTS,), I32),
                   jax.ShapeDtypeStruct((nblk,), I32),
                   jax.ShapeDtypeStruct((1,), I32)],
        name="plan",
    )(tcnt)


def _for_each_run(tc_ref, rs_ref, i, fn):
    start = jnp.int32(0)
    for ex in range(N_EXPERTS):
        ng = _run_groups(tc_ref[i, ex])
        g0 = rs_ref[i, ex]

        def body(q, carry, start=start, g0=g0):
            fn(pl.multiple_of(start + q * RUN_ALIGN, RUN_ALIGN), pl.multiple_of(g0 + q * RUN_ALIGN, RUN_ALIGN))
            return carry

        lax.fori_loop(0, ng, body, 0)
        start = start + (ng << RUN_SHIFT)


def _group_count(tc_ref, i):
    n = jnp.int32(0)
    for ex in range(N_EXPERTS):
        n = n + _run_groups(tc_ref[i, ex])
    return n


def _wait_groups(n, src_ref, dst_ref, sem):
    def waits(count, rows):
        def body(q, carry):
            pltpu.make_async_copy(src_ref.at[pl.ds(0, rows)], dst_ref.at[pl.ds(0, rows)], sem).wait()
            return carry
        lax.fori_loop(0, count, body, 0)

    waits(n >> 6, 64 * RUN_ALIGN)
    waits((n >> 3) & 7, 8 * RUN_ALIGN)
    waits(n & 7, RUN_ALIGN)


def _dispatch_body(tc_ref, rs_ref, p0_ref, pn_ref, nu_ref, m_ref, h_ref, xin_ref, g_ref, z_ref, sem, zsem, *, tm, R):
    i = pl.program_id(0)
    slot = i % 2
    rows = m_ref[...]
    jj = lax.broadcasted_iota(I32, (R, tm), 0)
    hit = (jj == rows[0:1]) | (jj == rows[1:2]) | (jj == rows[2:3]) | (jj == rows[3:4])
    g_ref[slot] = jnp.dot(jnp.where(hit, 1.0, 0.0).astype(BF16), h_ref[...].astype(BF16),
                          preferred_element_type=F32).astype(BF16)

    group = lambda ref, r: ref.at[pl.ds(r, RUN_ALIGN)]

    @pl.when(i > 0)
    def _():
        _wait_groups(_group_count(tc_ref, i - 1), g_ref.at[1 - slot], xin_ref, sem.at[1 - slot])

    _for_each_run(tc_ref, rs_ref, i,
                  lambda lr, gr: pltpu.make_async_copy(group(g_ref.at[slot], lr), group(xin_ref, gr), sem.at[slot]).start())

    @pl.when(i == pl.num_programs(0) - 1)
    def _():
        _wait_groups(_group_count(tc_ref, i), g_ref.at[slot], xin_ref, sem.at[slot])

    @pl.when(i == 0)
    def _():
        z_ref[...] = jnp.zeros_like(z_ref)
        total = jnp.int32(0)
        for ex in range(N_EXPERTS):
            def body(q, carry, ex=ex):
                pltpu.make_async_copy(group(z_ref, 0), group(xin_ref, pl.multiple_of(p0_ref[ex] + q * RUN_ALIGN, RUN_ALIGN)), zsem).start()
                return carry
            lax.fori_loop(0, pn_ref[ex], body, 0)
            total = total + pn_ref[ex]
        _wait_groups(total, z_ref, xin_ref, zsem)

        block = lambda j: xin_ref.at[pl.ds(pl.multiple_of(j * MOE_BLOCK, MOE_BLOCK), MOE_BLOCK)]
        n_blocks = xin_ref.shape[0] // MOE_BLOCK

        def fill(j, carry):
            pltpu.make_async_copy(z_ref, block(j), zsem).start()
            return carry

        def fill_wait(j, carry):
            pltpu.make_async_copy(z_ref, block(j), zsem).wait()
            return carry

        lax.fori_loop(nu_ref[0], n_blocks, fill, 0)
        lax.fori_loop(nu_ref[0], n_blocks, fill_wait, 0)


def _dispatch(tcnt, rowstart, pad0, padn, n_used, m_t, h, n_rows):
    T, D = h.shape
    tm = ROUTE_TILE
    col = pl.BlockSpec((8, tm), lambda i, *_: (0, i))
    return pl.pallas_call(
        functools.partial(_dispatch_body, tm=tm, R=GROUP_ROWS),
        grid_spec=pltpu.PrefetchScalarGridSpec(
            num_scalar_prefetch=5,
            grid=(T // tm,),
            in_specs=[col, pl.BlockSpec((tm, D), lambda i, *_: (i, 0))],
            out_specs=pl.BlockSpec(memory_space=pl.ANY),
            scratch_shapes=[pltpu.VMEM((2, GROUP_ROWS, D), BF16), pltpu.VMEM((MOE_BLOCK, D), BF16),
                            pltpu.SemaphoreType.DMA((2,)), pltpu.SemaphoreType.DMA(())]),
        out_shape=jax.ShapeDtypeStruct((n_rows, D), BF16),
        compiler_params=pltpu.CompilerParams(dimension_semantics=("arbitrary",), vmem_limit_bytes=56 * 2**20),
        name="dispatch",
    )(tcnt, rowstart, pad0, padn, n_used, m_t, h)


def _moe_body(be_ref, nu_ref, x_ref, w1g_ref, w1l_ref, b1g_ref, b1l_ref, w2_ref, b2_ref, y_ref):
    del be_ref
    live = pl.program_id(0) < nu_ref[0]

    @pl.when(live)
    def _():
        x = x_ref[...]
        hg = lax.dot_general(x, w1g_ref[0], NT_DIMS, preferred_element_type=F32) + b1g_ref[0]
        hl = lax.dot_general(x, w1l_ref[0], NT_DIMS, preferred_element_type=F32) + b1l_ref[0]
        xg = jnp.minimum(hg, SWIGLU_LIMIT)
        xl = jnp.clip(hl, -SWIGLU_LIMIT, SWIGLU_LIMIT)
        act = xg * _sigmoid(SWIGLU_ALPHA * xg) * (xl + 1.0)
        y = jnp.dot(act.astype(BF16), w2_ref[0].astype(BF16), preferred_element_type=F32) + b2_ref[0]
        y_ref[...] = y.astype(BF16)

    @pl.when(jnp.logical_not(live))
    def _():
        y_ref[...] = jnp.zeros_like(y_ref)


def _moe(block_e, n_used, xin, w1g, w1l, b1, w2, b2):
    n_rows = xin.shape[0]
    D = w2.shape[2]
    F = w2.shape[1]
    nblk = n_rows // MOE_BLOCK
    b1g = b1[:, 0::2].reshape(N_EXPERTS, 1, F)
    b1l = b1[:, 1::2].reshape(N_EXPERTS, 1, F)
    b2r = b2.reshape(N_EXPERTS, 1, D)
    xspec = pl.BlockSpec((MOE_BLOCK, D), lambda j, be, nu: (jnp.minimum(j, nu[0] - 1), 0))
    yspec = pl.BlockSpec((MOE_BLOCK, D), lambda j, be, nu: (j, 0))
    wspec = lambda a: pl.BlockSpec((1,) + a.shape[1:], lambda j, be, nu: (be[j], 0, 0))
    return pl.pallas_call(
        _moe_body,
        grid_spec=pltpu.PrefetchScalarGridSpec(
            num_scalar_prefetch=2,
            grid=(nblk,),
            in_specs=[xspec, wspec(w1g), wspec(w1l), wspec(b1g), wspec(b1l), wspec(w2), wspec(b2r)],
            out_specs=yspec),
        out_shape=jax.ShapeDtypeStruct(xin.shape, BF16),
        compiler_params=pltpu.CompilerParams(dimension_semantics=("arbitrary",),
                                             vmem_limit_bytes=48 * 2**20),
        name="moe",
    )(block_e, n_used, xin, w1g, w1l, b1g, b1l, w2, b2r)


def _combine_body(tc_ref, rs_ref, slab_ref, y_ref, x1_ref, fg_ref, out_ref, g_ref, sem, *, tm, R):
    i = pl.program_id(0)
    slot = i % 2
    group = lambda ref, r: ref.at[pl.ds(r, RUN_ALIGN)]

    def fetch(tile, slot):
        _for_each_run(tc_ref, rs_ref, tile,
                      lambda lr, gr: pltpu.make_async_copy(group(y_ref, gr), group(g_ref.at[slot], lr), sem.at[slot]).start())

    @pl.when(i == 0)
    def _():
        g_ref[...] = jnp.zeros_like(g_ref)
        fetch(i, slot)

    @pl.when(i + 1 < pl.num_programs(0))
    def _():
        fetch(i + 1, 1 - slot)

    slab = slab_ref[...]
    gates = slab[:, 0:TOP_K]
    rows = slab[:, TOP_K:2 * TOP_K].astype(I32)
    jj = lax.broadcasted_iota(I32, (tm, R), 1)
    sel = jnp.zeros((tm, R), F32)
    for k in range(TOP_K):
        sel = jnp.where(jj == rows[:, k:k + 1], gates[:, k:k + 1], sel)

    _wait_groups(_group_count(tc_ref, i), y_ref, g_ref.at[slot], sem.at[slot])
    acc = x1_ref[...] + jnp.dot(sel.astype(BF16), g_ref[slot], preferred_element_type=F32)
    out_ref[...] = acc * lax.rsqrt(jnp.mean(acc * acc, axis=-1, keepdims=True) + NORM_EPS) * fg_ref[...]


def _combine(tcnt, rowstart, slab, y, x1, final_g):
    T, D = x1.shape
    tm = ROUTE_TILE
    fg = final_g.reshape(1, D)
    return pl.pallas_call(
        functools.partial(_combine_body, tm=tm, R=GROUP_ROWS),
        grid_spec=pltpu.PrefetchScalarGridSpec(
            num_scalar_prefetch=2,
            grid=(T // tm,),
            in_specs=[pl.BlockSpec((tm, LANES), lambda i, *_: (i, 0)),
                      pl.BlockSpec(memory_space=pl.ANY),
                      pl.BlockSpec((tm, D), lambda i, *_: (i, 0)),
                      pl.BlockSpec((1, D), lambda i, *_: (0, 0))],
            out_specs=pl.BlockSpec((tm, D), lambda i, *_: (i, 0)),
            scratch_shapes=[pltpu.VMEM((2, GROUP_ROWS, D), BF16), pltpu.SemaphoreType.DMA((2,))]),
        out_shape=jax.ShapeDtypeStruct((T, D), F32),
        compiler_params=pltpu.CompilerParams(dimension_semantics=("arbitrary",), vmem_limit_bytes=56 * 2**20),
        name="combine",
    )(tcnt, rowstart, slab, y, x1, fg)


def _mixers(x2, B, S, norm1_g, w_in, rel_bias, wg_f, bg_f, wg_b, bg_b, gla_norm_g, w1):
    a1, a4, a16, gla, bcum, w1g, w1l = _in_proj(x2, B, S, norm1_g, w_in, wg_f, bg_f, wg_b, bg_b, w1)
    branches = [_attn_branch(a, rel_bias, d) for a, d in zip((a1, a4, a16), DILATIONS)]
    g = _gla(gla, bcum, gla_norm_g, B, S)
    return [o for o, _ in branches], [l for _, l in branches], g, w1g, w1l


def _moe_layer(h, m_t, slab, tcnt_f, x1, w1g, w1l, b1, w2, b2, final_g):
    T = x1.shape[0]
    nt = T // ROUTE_TILE
    nblk = (T * TOP_K + nt * N_EXPERTS * (RUN_ALIGN - 1)) // MOE_BLOCK + N_EXPERTS
    tcnt = tcnt_f[:, :, 0].astype(I32)
    rowstart, pad0, padn, block_e, n_used = _plan(tcnt, nblk)
    xin = _dispatch(tcnt, rowstart, pad0, padn, n_used, m_t, h, nblk * MOE_BLOCK)
    y = _moe(block_e, n_used, xin, w1g, w1l, b1, w2, b2)
    return _combine(tcnt, rowstart, slab, y, x1, final_g)


def kernel(x, norm1_g, w_in, rel_bias, gla_wg_fwd, gla_bg_fwd, gla_wg_bwd, gla_bg_bwd, gla_norm_g, w_out, norm2_g, router_w, router_b, moe_w1, moe_b1, moe_w2, moe_b2, final_g):
    B, S, D = x.shape
    assert w_in.shape[0] == 1, "one layer"
    x2 = x.reshape(B * S, D)
    outs, lses, g, w1g, w1l = _mixers(x2, B, S, norm1_g[0], w_in[0], rel_bias, gla_wg_fwd[0], gla_bg_fwd[0],
                                      gla_wg_bwd[0], gla_bg_bwd[0], gla_norm_g[0], moe_w1[0])
    x1, h, m_t, slab, tcnt = _mix(outs, lses, g, x2, B, S, w_out[0], norm2_g[0], router_w[0], router_b[0])
    out = _moe_layer(h, m_t, slab, tcnt, x1, w1g, w1l, moe_b1[0], moe_w2[0], moe_b2[0], final_g)
    return out.reshape(B, S, D)
```

```python
import functools

import jax
import jax.numpy as jnp
import numpy as np
from jax import lax
from jax.experimental import pallas as pl
from jax.experimental.pallas import tpu as pltpu

F32, BF16, I32 = jnp.float32, jnp.bfloat16, jnp.int32

LANES = 128
HEAD_DIM = 64
ATTN_HEADS = 8
ATTN_W = ATTN_HEADS * HEAD_DIM
HALF_SPAN = 64
ATTN_QBLK = 128
ATTN_KWIN = 256
DILATIONS = (1, 4, 16)
T5_BUCKETS = 32
T5_MAX_DISTANCE = 1024
NEG_INF = -1e30
LOG2_E = 1.4426950408889634
LN_2 = 0.6931471805599453
GLA_KEY_W = 256
GLA_VAL_W = 512
GLA_DV = 128
GLA_RANK = 16
GLA_TAU = 16.0
GLA_CHUNK = 64
N_EXPERTS = 32
TOP_K = 4
SWIGLU_LIMIT = 7.0
SWIGLU_ALPHA = 1.702
NORM_EPS = 1e-5
MOE_BLOCK = 512
W1_SUBTILE = 512
ROUTE_TILE = 512
RUN_ALIGN = 16
RUN_SHIFT = 4
GROUP_ROWS = ROUTE_TILE * TOP_K + N_EXPERTS * RUN_ALIGN

ATTN_COLS = 3 * ATTN_W
GLA_COLS = 2 * GLA_KEY_W + 2 * GLA_VAL_W
Z_COLS = LANES
PROJ_COLS = ATTN_COLS + GLA_COLS + Z_COLS

NT_DIMS = (((1,), (1,)), ((), ()))


def _sigmoid(x):
    return 1.0 / (1.0 + jnp.exp(-x))


def _split3(a):
    h1 = a.astype(BF16)
    r1 = a - h1.astype(F32)
    h2 = r1.astype(BF16)
    h3 = (r1 - h2.astype(F32)).astype(BF16)
    return h1, h2, h3


def _dot_f32(a, b):
    ah, al, _ = _split3(a)
    bh, bl, _ = _split3(b)
    d = lambda u, v: jnp.dot(u, v, preferred_element_type=F32)
    return d(ah, bh) + d(al, bh) + d(ah, bl)


def _split_w1_tile(w_ref, g_ref, l_ref, t_ref):
    sub = t_ref.shape[1]
    for s in range(w_ref.shape[2] // sub):
        wt = w_ref[0, :, s * sub:(s + 1) * sub].T
        out_rows = slice(s * sub // 2, (s + 1) * sub // 2)
        for c in range(wt.shape[1] // LANES):
            cols = slice(c * LANES, (c + 1) * LANES)
            t_ref[c] = wt[:, cols]
            g_ref[0, out_rows, cols] = t_ref[c, pl.ds(0, sub // 2, stride=2), :].astype(BF16)
            l_ref[0, out_rows, cols] = t_ref[c, pl.ds(1, sub // 2, stride=2), :].astype(BF16)


def _in_proj_body(x_ref, g_ref, w_ref, wg_ref, bg_ref, w1_ref, a1_ref, a4_ref, a16_ref, gla_ref, b_ref,
                  w1g_ref, w1l_ref, p_ref, t_ref, *, tm):
    _split_w1_tile(w1_ref, w1g_ref, w1l_ref, t_ref)

    x = x_ref[...]
    n = x * lax.rsqrt(jnp.mean(x * x, axis=-1, keepdims=True) + NORM_EPS) * g_ref[...]
    p = jnp.dot(n.astype(BF16), w_ref[...], preferred_element_type=F32)
    gla_ref[...] = p[:, ATTN_COLS:ATTN_COLS + GLA_COLS].astype(BF16)

    a1_ref[0, 0] = p[:, :ATTN_COLS].astype(BF16)
    for c in range(ATTN_COLS // LANES):
        p_ref[c] = p[:, c * LANES:(c + 1) * LANES]
    for dil, a_ref in ((DILATIONS[1], a4_ref), (DILATIONS[2], a16_ref)):
        for r in range(dil):
            for c in range(ATTN_COLS // LANES):
                a_ref[0, r, :, c * LANES:(c + 1) * LANES] = p_ref[c, pl.ds(r, tm // dil, stride=dil), :].astype(BF16)

    z = p[:, ATTN_COLS + GLA_COLS:]
    zz = _dot_f32(z, wg_ref[...]) + bg_ref[...]
    la = (jnp.minimum(zz, 0.0) - jnp.log1p(jnp.exp(-jnp.abs(zz)))) * (1.0 / GLA_TAU)
    C = GLA_CHUNK
    r_i = lax.broadcasted_iota(I32, (C, C), 0)
    c_i = lax.broadcasted_iota(I32, (C, C), 1)
    lower = (r_i >= c_i).astype(BF16)
    upper = (r_i <= c_i).astype(BF16)
    W = GLA_KEY_W
    for j in range(tm // C):
        rows = slice(j * C, (j + 1) * C)
        for tri, cols in ((lower, slice(0, W)), (upper, slice(W, 2 * W))):
            terms = jnp.concatenate(_split3(la[rows, cols]), axis=1)
            cs = jnp.dot(tri, terms, preferred_element_type=F32)
            b_ref[rows, cols] = cs[:, :W] + cs[:, W:2 * W] + cs[:, 2 * W:]


def _in_proj(x2, B, S, norm_g, w_in, wg_f, bg_f, wg_b, bg_b, w1):
    T, D = x2.shape
    tm = 256
    nt = S // tm
    E, _, F2 = w1.shape
    w1_cols = E * F2 // (B * nt)
    per_e = F2 // w1_cols
    assert w1_cols * B * nt == E * F2 and per_e * w1_cols == F2 and w1_cols % W1_SUBTILE == 0
    w1_in = pl.BlockSpec((1, D, w1_cols), lambda b, i: ((b * nt + i) // per_e, 0, (b * nt + i) % per_e))
    w1_out = pl.BlockSpec((1, w1_cols // 2, D), lambda b, i: ((b * nt + i) // per_e, (b * nt + i) % per_e, 0))
    w = jnp.pad(w_in.astype(BF16), ((0, 0), (0, PROJ_COLS - w_in.shape[1])))
    wg = jnp.zeros((Z_COLS, 2 * GLA_KEY_W), F32)
    wg = wg.at[:GLA_RANK, :GLA_KEY_W].set(wg_f).at[GLA_RANK:2 * GLA_RANK, GLA_KEY_W:].set(wg_b)
    bg = jnp.concatenate([bg_f, bg_b]).reshape(1, 2 * GLA_KEY_W)
    row = lambda c: pl.BlockSpec((tm, c), lambda b, i: (b * nt + i, 0))
    full = lambda a: pl.BlockSpec(a.shape, lambda b, i: (0,) * a.ndim)
    res = lambda d: pl.BlockSpec((1, d, tm // d, ATTN_COLS), lambda b, i: (b, 0, i, 0))
    res_shape = lambda d: jax.ShapeDtypeStruct((B, d, S // d, ATTN_COLS), BF16)
    g2 = norm_g.reshape(1, D)
    return pl.pallas_call(
        functools.partial(_in_proj_body, tm=tm),
        grid=(B, nt),
        in_specs=[row(D), full(g2), full(w), full(wg), full(bg), w1_in],
        out_specs=[res(d) for d in DILATIONS] + [row(GLA_COLS), row(2 * GLA_KEY_W), w1_out, w1_out],
        out_shape=[res_shape(d) for d in DILATIONS]
                  + [jax.ShapeDtypeStruct((T, GLA_COLS), BF16), jax.ShapeDtypeStruct((T, 2 * GLA_KEY_W), F32)]
                  + [jax.ShapeDtypeStruct((E, F2 // 2, D), BF16)] * 2,
        scratch_shapes=[pltpu.VMEM((ATTN_COLS // LANES, tm, LANES), F32),
                        pltpu.VMEM((D // LANES, W1_SUBTILE, LANES), F32)],
        compiler_params=pltpu.CompilerParams(dimension_semantics=("arbitrary", "arbitrary"),
                                             vmem_limit_bytes=56 * 2**20),
        name="in_proj",
    )(x2, g2, w, wg, bg, w1)


def _t5_bucket(rel):
    nb = T5_BUCKETS // 2
    max_exact = nb // 2
    n = np.abs(rel)
    large = max_exact + (np.log(np.maximum(n, 1).astype(np.float32) / max_exact)
                         / np.log(T5_MAX_DISTANCE / max_exact) * (nb - max_exact)).astype(np.int32)
    large = np.minimum(large, nb - 1)
    return (np.where(rel > 0, nb, 0) + np.where(n < max_exact, n, large)).astype(np.int32)


def _bucket_tables(dil):
    s = np.arange(ATTN_QBLK)[:, None]
    t = np.arange(ATTN_KWIN)[None, :]
    tabs = []
    for c in (0, HALF_SPAN, 2 * HALF_SPAN):
        off = t - c - s
        tabs.append(np.where(np.abs(off) <= HALF_SPAN, _t5_bucket(off * dil), -1))
    return np.stack(tabs).astype(np.int32)


def _attn_body(rb_ref, idx_ref, q_ref, k_ref, v_ref, o_ref, lse_ref, tab_ref, *, L, TQ):
    first = (pl.program_id(0) == 0) & (pl.program_id(1) == 0) & (pl.program_id(2) == 0)

    @pl.when(first)
    def _():
        for v in range(3):
            idx = idx_ref[v]
            for h in range(ATTN_HEADS):
                fill = lambda j, t, h=h, idx=idx: jnp.where(idx == j, rb_ref[j, h] * LOG2_E, t)
                tab_ref[v, h] = lax.fori_loop(0, T5_BUCKETS, fill, jnp.where(idx < 0, NEG_INF, 0.0).astype(F32))

    i = pl.program_id(2)
    lo = lax.broadcasted_iota(I32, (ATTN_QBLK, LANES), 1) < HEAD_DIM
    scale = HEAD_DIM ** -0.5 * LOG2_E
    for sb in range(TQ // ATTN_QBLK):
        l0 = i * TQ + sb * ATTN_QBLK
        ws = pl.multiple_of(jnp.clip(l0 - HALF_SPAN, 0, L - ATTN_KWIN), HALF_SPAN)
        var = jnp.where(l0 == 0, 0, jnp.where(l0 == L - ATTN_QBLK, 2, 1))
        rows = slice(sb * ATTN_QBLK, (sb + 1) * ATTN_QBLK)
        for hp in range(ATTN_HEADS // 2):
            cols = slice(hp * LANES, (hp + 1) * LANES)
            q2 = q_ref[rows, cols]
            k2 = k_ref[pl.ds(ws, ATTN_KWIN), cols]
            v2 = v_ref[pl.ds(ws, ATTN_KWIN), cols]
            zero = jnp.zeros_like(q2)
            qs = jnp.concatenate([jnp.where(lo, q2, zero), jnp.where(lo, zero, q2)], axis=0)
            s = lax.dot_general(qs, k2, NT_DIMS, preferred_element_type=F32)
            bias = jnp.concatenate([tab_ref[var, 2 * hp], tab_ref[var, 2 * hp + 1]], axis=0)
            logits = s * scale + bias
            m = jnp.max(logits, axis=-1, keepdims=True)
            p = jnp.exp2(logits - m)
            den = jnp.sum(p, axis=-1, keepdims=True)
            pv = jnp.dot(p.astype(BF16), v2, preferred_element_type=F32) / den
            lse = jnp.broadcast_to(m * LN_2 + jnp.log(den), (2 * ATTN_QBLK, LANES))
            o_ref[rows, cols] = jnp.where(lo, pv[:ATTN_QBLK], pv[ATTN_QBLK:]).astype(BF16)
            lse_ref[rows, cols] = jnp.where(lo, lse[:ATTN_QBLK], lse[ATTN_QBLK:])


def _attn_branch(attn, rel_bias, dil):
    B, _, L, _ = attn.shape
    TQ = 256
    idx = jnp.asarray(_bucket_tables(dil))
    body = functools.partial(_attn_body, L=L, TQ=TQ)
    kv = lambda j: pl.BlockSpec((None, None, L, ATTN_W), lambda b, r, i: (b, r, 0, j))
    q_spec = pl.BlockSpec((None, None, TQ, ATTN_W), lambda b, r, i: (b, r, i, 0))
    return pl.pallas_call(
        body,
        grid=(B, dil, L // TQ),
        in_specs=[pl.BlockSpec(memory_space=pltpu.SMEM),
                  pl.BlockSpec(idx.shape, lambda b, r, i: (0, 0, 0)),
                  q_spec, kv(1), kv(2)],
        out_specs=[q_spec, q_spec],
        out_shape=[jax.ShapeDtypeStruct((B, dil, L, ATTN_W), BF16),
                   jax.ShapeDtypeStruct((B, dil, L, ATTN_W), F32)],
        scratch_shapes=[pltpu.VMEM((3, ATTN_HEADS, ATTN_QBLK, ATTN_KWIN), F32)],
        compiler_params=pltpu.CompilerParams(dimension_semantics=("arbitrary",) * 3,
                                             vmem_limit_bytes=48 * 2**20),
        name=f"attn_d{dil}",
    )(rel_bias, idx, attn, attn, attn)


def _gla_body(q_ref, k_ref, v_ref, gate_ref, bf_ref, bb_ref, ng_ref, out_ref, of_ref, ob_ref, st_ref, *, S):
    C = GLA_CHUNK
    nc = S // C
    r = lax.broadcasted_iota(I32, (C, LANES), 0)
    s2 = lax.broadcasted_iota(I32, (C, LANES), 1) % HEAD_DIM
    lo = lax.broadcasted_iota(I32, (C, LANES), 1) < HEAD_DIM
    blockdiag = ((lax.broadcasted_iota(I32, (2 * GLA_DV, LANES), 0) < GLA_DV)
                 == (lax.broadcasted_iota(I32, (2 * GLA_DV, LANES), 1) < HEAD_DIM))
    st_ref[...] = jnp.zeros_like(st_ref)

    def one_chunk(ci, direction):
        mask = (r >= s2) if direction == 0 else (r <= s2)
        b_ref, o_ref = (bf_ref, of_ref) if direction == 0 else (bb_ref, ob_ref)
        rows = pl.ds(pl.multiple_of(ci * C, C), C)
        q = q_ref[0, rows, :].astype(F32) * (HEAD_DIM ** -0.5)
        k = k_ref[0, rows, :].astype(F32)
        v = v_ref[0, rows, :]
        b = b_ref[0, rows, :]
        g_tot = b[C - 1:C, :] if direction == 0 else b[0:1, :]
        qd = (q * jnp.exp(b)).astype(BF16)
        kd = (k * jnp.exp(-b)).astype(BF16)
        kst = (k * jnp.exp(g_tot - b)).astype(BF16)
        zk = jnp.zeros_like(kd)
        ks = jnp.concatenate([jnp.where(lo, kd, zk), jnp.where(lo, zk, kd)], axis=0)
        a = lax.dot_general(qd, ks, NT_DIMS, preferred_element_type=F32)
        a = jnp.where(mask, a, 0.0).astype(BF16)
        zv = jnp.zeros((C, GLA_DV), BF16)
        vblk = jnp.concatenate([jnp.concatenate([v[:, :GLA_DV], zv], axis=1),
                                jnp.concatenate([zv, v[:, GLA_DV:]], axis=1)], axis=0)
        st = st_ref[direction]
        o_ref[rows, :] = (jnp.dot(a, vblk, preferred_element_type=F32)
                          + lax.dot_general(qd, st.astype(BF16), NT_DIMS, preferred_element_type=F32))
        d_st = jnp.dot(v.astype(F32).T.astype(BF16), kst, preferred_element_type=F32)
        st_ref[direction] = st * jnp.exp(g_tot) + jnp.where(blockdiag, d_st, 0.0)

    def step(i, carry):
        one_chunk(i, 0)
        one_chunk(nc - 1 - i, 1)
        return carry

    lax.fori_loop(0, nc, step, 0, unroll=2)

    tr = 256
    def epilogue(i, carry):
        rows = pl.ds(pl.multiple_of(i * tr, tr), tr)
        for hh in range(2):
            cols = slice(hh * GLA_DV, (hh + 1) * GLA_DV)
            o = of_ref[rows, cols] + ob_ref[rows, cols]
            o = o * lax.rsqrt(jnp.mean(o * o, axis=-1, keepdims=True) + NORM_EPS) * ng_ref[0, :, cols]
            gate = gate_ref[0, rows, cols].astype(F32)
            out_ref[0, rows, cols] = (o * (gate * _sigmoid(gate))).astype(BF16)
        return carry

    lax.fori_loop(0, S // tr, epilogue, 0)


def _gla(gla, bcum, norm_g, B, S):
    g3 = gla.reshape(B, S, GLA_COLS)
    b3 = bcum.reshape(B, S, 2 * GLA_KEY_W)
    ng = norm_g.reshape(2, 1, 2 * GLA_DV)
    body = functools.partial(_gla_body, S=S)
    spec = lambda w, off: pl.BlockSpec((1, S, w), lambda b, p: (b, 0, off + p))
    out = pl.pallas_call(
        body,
        grid=(B, 2),
        in_specs=[spec(LANES, 0),
                  spec(LANES, GLA_KEY_W // LANES),
                  spec(2 * GLA_DV, (2 * GLA_KEY_W) // (2 * GLA_DV)),
                  spec(2 * GLA_DV, (2 * GLA_KEY_W + GLA_VAL_W) // (2 * GLA_DV)),
                  spec(LANES, 0),
                  spec(LANES, GLA_KEY_W // LANES),
                  pl.BlockSpec((1, 1, 2 * GLA_DV), lambda b, p: (p, 0, 0))],
        out_specs=pl.BlockSpec((1, S, 2 * GLA_DV), lambda b, p: (b, 0, p)),
        out_shape=jax.ShapeDtypeStruct((B, S, GLA_VAL_W), BF16),
        scratch_shapes=[pltpu.VMEM((S, 2 * GLA_DV), F32), pltpu.VMEM((S, 2 * GLA_DV), F32),
                        pltpu.VMEM((2, 2 * GLA_DV, LANES), F32)],
        compiler_params=pltpu.CompilerParams(dimension_semantics=("arbitrary", "arbitrary"),
                                             vmem_limit_bytes=56 * 2**20),
        name="gla",
    )(g3, g3, g3, g3, b3, b3, ng)
    return out.reshape(B * S, GLA_VAL_W)


def _natural_rows(blk_ref, u_ref, dil, tm):
    if dil == 1:
        return blk_ref[0, 0].astype(F32)
    for r in range(dil):
        for c in range(ATTN_W // LANES):
            u_ref[c, pl.ds(r, tm // dil, stride=dil), :] = blk_ref[0, r, :, c * LANES:(c + 1) * LANES].astype(F32)
    return jnp.concatenate([u_ref[c] for c in range(ATTN_W // LANES)], axis=1)


def _mix_body(o1, o2, o3, l1, l2, l3, g_ref, x_ref, wo_ref, n2_ref, rwh_ref, rwl_ref, rb_ref,
              x1_ref, h_ref, m_ref, slab_ref, tcnt_ref, u_ref, *, tm):
    ls = [_natural_rows(l, u_ref, d, tm) for l, d in zip((l1, l2, l3), DILATIONS)]
    m = jnp.maximum(jnp.maximum(ls[0], ls[1]), ls[2])
    es = [jnp.exp(l - m) for l in ls]
    a = es[0] * _natural_rows(o1, u_ref, DILATIONS[0], tm)
    a = a + es[1] * _natural_rows(o2, u_ref, DILATIONS[1], tm)
    a = a + es[2] * _natural_rows(o3, u_ref, DILATIONS[2], tm)
    a = a / (es[0] + es[1] + es[2])
    mix = (jnp.dot(a.astype(BF16), wo_ref[:ATTN_W, :], preferred_element_type=F32)
           + jnp.dot(g_ref[...], wo_ref[ATTN_W:, :], preferred_element_type=F32))
    x1 = x_ref[...] + mix
    x1_ref[...] = x1
    h = x1 * lax.rsqrt(jnp.mean(x1 * x1, axis=-1, keepdims=True) + NORM_EPS) * n2_ref[...]
    h_ref[...] = h

    hh, hl, _ = _split3(h)
    dg = lambda u, v: lax.dot_general(u, v, NT_DIMS, preferred_element_type=F32)
    logits = dg(rwh_ref[...], hh) + dg(rwl_ref[...], hh) + dg(rwh_ref[...], hl) + rb_ref[...]
    ie = lax.broadcasted_iota(I32, (N_EXPERTS, tm), 0)
    cur = logits
    vals, idxs = [], []
    for _ in range(TOP_K):
        mx = jnp.max(cur, axis=0, keepdims=True)
        ix = jnp.min(jnp.where(cur == mx, ie, N_EXPERTS), axis=0, keepdims=True)
        vals.append(mx)
        idxs.append(ix)
        cur = jnp.where(ie == ix, -jnp.inf, cur)
    ex = [jnp.exp(v - vals[0]) for v in vals]
    den = ex[0] + ex[1] + ex[2] + ex[3]
    gates = [e / den for e in ex]

    onehots = [ie == ix for ix in idxs]
    chosen = (onehots[0] | onehots[1] | onehots[2] | onehots[3]).astype(F32)
    tr = lax.broadcasted_iota(I32, (tm, tm), 0)
    tc = lax.broadcasted_iota(I32, (tm, tm), 1)
    before = jnp.dot(chosen.astype(BF16), (tr < tc).astype(BF16), preferred_element_type=F32)
    count = jnp.broadcast_to(jnp.sum(chosen, axis=1, keepdims=True), (N_EXPERTS, LANES))
    tcnt_ref[0] = count
    run = jnp.floor((count + (RUN_ALIGN - 1.0)) * (1.0 / RUN_ALIGN)) * RUN_ALIGN
    er = lax.broadcasted_iota(I32, (N_EXPERTS, N_EXPERTS), 0)
    ec = lax.broadcasted_iota(I32, (N_EXPERTS, N_EXPERTS), 1)
    run_start = jnp.dot((ec < er).astype(BF16), run.astype(BF16), preferred_element_type=F32)
    where_ = before + run_start[:, 0:1]
    locs = [jnp.sum(jnp.where(oh, where_, 0.0), axis=0, keepdims=True) for oh in onehots]

    def pack(rows, n):
        sub = lax.broadcasted_iota(I32, (n, tm), 0)
        out = jnp.zeros((n, tm), rows[0].dtype)
        for k, row in enumerate(rows):
            out = jnp.where(sub == k, row, out)
        return out

    m_ref[...] = pack([l.astype(I32) for l in locs], 8)
    slab = pack(gates + locs, 8)
    slab_ref[...] = jnp.concatenate([slab, jnp.zeros((LANES - 8, tm), F32)], axis=0).T


def _mix(outs, lses, g, x2, B, S, w_out, norm2_g, router_w, router_b):
    T, D = x2.shape
    tm = ROUTE_TILE
    nt = S // tm
    wo = w_out.astype(BF16)
    rwt = router_w.T
    rwh = rwt.astype(BF16)
    rwl = (rwt - rwh.astype(F32)).astype(BF16)
    rb = router_b.reshape(N_EXPERTS, 1)
    n2 = norm2_g.reshape(1, D)
    row = lambda c: pl.BlockSpec((tm, c), lambda b, i: (b * nt + i, 0))
    full = lambda a: pl.BlockSpec(a.shape, lambda b, i: (0,) * a.ndim)
    col = pl.BlockSpec((8, tm), lambda b, i: (0, b * nt + i))
    res = [pl.BlockSpec((1, d, tm // d, ATTN_W), lambda b, i: (b, 0, i, 0)) for d in DILATIONS]
    return pl.pallas_call(
        functools.partial(_mix_body, tm=tm),
        grid=(B, nt),
        in_specs=res + res + [row(GLA_VAL_W), row(D), full(wo), full(n2), full(rwh), full(rwl), full(rb)],
        out_specs=[row(D), row(D), col, row(LANES),
                   pl.BlockSpec((1, N_EXPERTS, LANES), lambda b, i: (b * nt + i, 0, 0))],
        out_shape=[jax.ShapeDtypeStruct((T, D), F32),
                   jax.ShapeDtypeStruct((T, D), F32),
                   jax.ShapeDtypeStruct((8, T), I32),
                   jax.ShapeDtypeStruct((T, LANES), F32),
                   jax.ShapeDtypeStruct((T // tm, N_EXPERTS, LANES), F32)],
        scratch_shapes=[pltpu.VMEM((ATTN_W // LANES, tm, LANES), F32)],
        compiler_params=pltpu.CompilerParams(dimension_semantics=("arbitrary", "arbitrary"),
                                             vmem_limit_bytes=48 * 2**20),
        name="mix_router",
    )(*outs, *lses, g, x2, wo, n2, rwh, rwl, rb)


def _run_groups(c):
    return (c + (RUN_ALIGN - 1)) >> RUN_SHIFT


def _plan_body(tc_ref, rs_ref, p0_ref, pn_ref, be_ref, hv_ref, nu_ref, *, nt, nblk):
    shift = MOE_BLOCK.bit_length() - 1
    start = jnp.int32(0)
    for ex in range(N_EXPERTS):
        def run(i, tot, start=start, ex=ex):
            rs_ref[i, ex] = start + tot
            return tot + (_run_groups(tc_ref[i, ex]) << RUN_SHIFT)

        tot = lax.fori_loop(0, nt, run, jnp.int32(0))
        nb = (tot + (MOE_BLOCK - 1)) >> shift
        p0_ref[ex] = start + tot
        pn_ref[ex] = ((nb << shift) - tot) >> RUN_SHIFT
        b0 = start >> shift

        def fill(j, carry, b0=b0, ex=ex):
            be_ref[b0 + j] = jnp.int32(ex)
            hv_ref[b0 + j] = jnp.int32(0)
            return carry

        lax.fori_loop(0, nb, fill, 0)
        last_rows = tot - ((nb - 1) << shift)

        @pl.when((nb > 0) & (last_rows <= MOE_BLOCK // 2))
        def _(b0=b0, nb=nb):
            hv_ref[b0 + nb - 1] = jnp.int32(1)

        start = start + (nb << shift)
    used = start >> shift

    def tail(j, carry):
        be_ref[j] = jnp.int32(N_EXPERTS - 1)
        hv_ref[j] = jnp.int32(0)
        return carry

    lax.fori_loop(used, nblk, tail, 0)
    nu_ref[0] = used


def _plan(tcnt, nblk):
    nt = tcnt.shape[0]
    smem = pl.BlockSpec(memory_space=pltpu.SMEM)
    return pl.pallas_call(
        functools.partial(_plan_body, nt=nt, nblk=nblk),
        in_specs=[smem],
        out_specs=[smem] * 6,
        out_shape=[jax.ShapeDtypeStruct(tcnt.shape, I32),
                   jax.ShapeDtypeStruct((N_EXPERTS,), I32),
                   jax.ShapeDtypeStruct((N_EXPERTS,), I32),
                   jax.ShapeDtypeStruct((nblk,), I32),
                   jax.ShapeDtypeStruct((nblk,), I32),
                   jax.ShapeDtypeStruct((1,), I32)],
        name="plan",
    )(tcnt)


def _for_each_run(tc_ref, rs_ref, i, fn):
    start = jnp.int32(0)
    for ex in range(N_EXPERTS):
        ng = _run_groups(tc_ref[i, ex])
        g0 = rs_ref[i, ex]

        def body(q, carry, start=start, g0=g0):
            fn(pl.multiple_of(start + q * RUN_ALIGN, RUN_ALIGN), pl.multiple_of(g0 + q * RUN_ALIGN, RUN_ALIGN))
            return carry

        lax.fori_loop(0, ng, body, 0)
        start = start + (ng << RUN_SHIFT)


def _group_count(tc_ref, i):
    n = jnp.int32(0)
    for ex in range(N_EXPERTS):
        n = n + _run_groups(tc_ref[i, ex])
    return n


def _wait_groups(n, src_ref, dst_ref, sem):
    def waits(count, rows):
        def body(q, carry):
            pltpu.make_async_copy(src_ref.at[pl.ds(0, rows)], dst_ref.at[pl.ds(0, rows)], sem).wait()
            return carry
        lax.fori_loop(0, count, body, 0)

    waits(n >> 6, 64 * RUN_ALIGN)
    waits((n >> 3) & 7, 8 * RUN_ALIGN)
    waits(n & 7, RUN_ALIGN)


def _dispatch_body(tc_ref, rs_ref, p0_ref, pn_ref, nu_ref, m_ref, h_ref, xin_ref, g_ref, z_ref, sem, zsem, *, tm, R):
    i = pl.program_id(0)
    slot = i % 2
    rows = m_ref[...].astype(jnp.int16)
    jj = lax.broadcasted_iota(I32, (R, tm), 0).astype(jnp.int16)
    hit = (jj == rows[0:1]) | (jj == rows[1:2]) | (jj == rows[2:3]) | (jj == rows[3:4])
    onehot = jnp.where(hit, jnp.ones((), BF16), jnp.zeros((), BF16))
    g_ref[slot] = jnp.dot(onehot, h_ref[...].astype(BF16), preferred_element_type=F32).astype(BF16)

    group = lambda ref, r: ref.at[pl.ds(r, RUN_ALIGN)]

    @pl.when(i > 0)
    def _():
        _wait_groups(_group_count(tc_ref, i - 1), g_ref.at[1 - slot], xin_ref, sem.at[1 - slot])

    _for_each_run(tc_ref, rs_ref, i,
                  lambda lr, gr: pltpu.make_async_copy(group(g_ref.at[slot], lr), group(xin_ref, gr), sem.at[slot]).start())

    @pl.when(i == pl.num_programs(0) - 1)
    def _():
        _wait_groups(_group_count(tc_ref, i), g_ref.at[slot], xin_ref, sem.at[slot])

    @pl.when(i == 0)
    def _():
        z_ref[...] = jnp.zeros_like(z_ref)
        total = jnp.int32(0)
        for ex in range(N_EXPERTS):
            def body(q, carry, ex=ex):
                pltpu.make_async_copy(group(z_ref, 0), group(xin_ref, pl.multiple_of(p0_ref[ex] + q * RUN_ALIGN, RUN_ALIGN)), zsem).start()
                return carry
            lax.fori_loop(0, pn_ref[ex], body, 0)
            total = total + pn_ref[ex]
        _wait_groups(total, z_ref, xin_ref, zsem)

        block = lambda j: xin_ref.at[pl.ds(pl.multiple_of(j * MOE_BLOCK, MOE_BLOCK), MOE_BLOCK)]
        n_blocks = xin_ref.shape[0] // MOE_BLOCK

        def fill(j, carry):
            pltpu.make_async_copy(z_ref, block(j), zsem).start()
            return carry

        def fill_wait(j, carry):
            pltpu.make_async_copy(z_ref, block(j), zsem).wait()
            return carry

        lax.fori_loop(nu_ref[0], n_blocks, fill, 0)
        lax.fori_loop(nu_ref[0], n_blocks, fill_wait, 0)


def _dispatch(tcnt, rowstart, pad0, padn, n_used, m_t, h, n_rows):
    T, D = h.shape
    tm = ROUTE_TILE
    col = pl.BlockSpec((8, tm), lambda i, *_: (0, i))
    return pl.pallas_call(
        functools.partial(_dispatch_body, tm=tm, R=GROUP_ROWS),
        grid_spec=pltpu.PrefetchScalarGridSpec(
            num_scalar_prefetch=5,
            grid=(T // tm,),
            in_specs=[col, pl.BlockSpec((tm, D), lambda i, *_: (i, 0))],
            out_specs=pl.BlockSpec(memory_space=pl.ANY),
            scratch_shapes=[pltpu.VMEM((2, GROUP_ROWS, D), BF16), pltpu.VMEM((MOE_BLOCK, D), BF16),
                            pltpu.SemaphoreType.DMA((2,)), pltpu.SemaphoreType.DMA(())]),
        out_shape=jax.ShapeDtypeStruct((n_rows, D), BF16),
        compiler_params=pltpu.CompilerParams(dimension_semantics=("arbitrary",), vmem_limit_bytes=56 * 2**20),
        name="dispatch",
    )(tcnt, rowstart, pad0, padn, n_used, m_t, h)


def _moe_body(be_ref, nu_ref, hv_ref, x_ref, w1g_ref, w1l_ref, b1g_ref, b1l_ref, w2_ref, b2_ref, y_ref):
    del be_ref
    j = pl.program_id(0)
    live = j < nu_ref[0]
    half = hv_ref[j] == 1
    top = slice(0, MOE_BLOCK // 2)
    rest = slice(MOE_BLOCK // 2, MOE_BLOCK)

    def expert_mlp(rows):
        x = x_ref[rows, :]
        hg = lax.dot_general(x, w1g_ref[0], NT_DIMS, preferred_element_type=F32) + b1g_ref[0]
        hl = lax.dot_general(x, w1l_ref[0], NT_DIMS, preferred_element_type=F32) + b1l_ref[0]
        xg = jnp.minimum(hg, SWIGLU_LIMIT)
        xl = jnp.clip(hl, -SWIGLU_LIMIT, SWIGLU_LIMIT)
        act = xg * _sigmoid(SWIGLU_ALPHA * xg) * (xl + 1.0)
        y = jnp.dot(act.astype(BF16), w2_ref[0].astype(BF16), preferred_element_type=F32) + b2_ref[0]
        y_ref[rows, :] = y.astype(BF16)

    @pl.when(live & jnp.logical_not(half))
    def _():
        expert_mlp(slice(0, MOE_BLOCK))

    @pl.when(live & half)
    def _():
        expert_mlp(top)
        y_ref[rest, :] = jnp.zeros((MOE_BLOCK // 2, y_ref.shape[1]), y_ref.dtype)

    @pl.when(jnp.logical_not(live))
    def _():
        y_ref[...] = jnp.zeros_like(y_ref)


def _moe(block_e, n_used, half_block, xin, w1g, w1l, b1, w2, b2):
    n_rows = xin.shape[0]
    D = w2.shape[2]
    F = w2.shape[1]
    nblk = n_rows // MOE_BLOCK
    b1g = b1[:, 0::2].reshape(N_EXPERTS, 1, F)
    b1l = b1[:, 1::2].reshape(N_EXPERTS, 1, F)
    b2r = b2.reshape(N_EXPERTS, 1, D)
    xspec = pl.BlockSpec((MOE_BLOCK, D), lambda j, be, nu, hv: (jnp.minimum(j, nu[0] - 1), 0))
    yspec = pl.BlockSpec((MOE_BLOCK, D), lambda j, be, nu, hv: (j, 0))
    wspec = lambda a: pl.BlockSpec((1,) + a.shape[1:], lambda j, be, nu, hv: (be[j], 0, 0))
    return pl.pallas_call(
        _moe_body,
        grid_spec=pltpu.PrefetchScalarGridSpec(
            num_scalar_prefetch=3,
            grid=(nblk,),
            in_specs=[xspec, wspec(w1g), wspec(w1l), wspec(b1g), wspec(b1l), wspec(w2), wspec(b2r)],
            out_specs=yspec),
        out_shape=jax.ShapeDtypeStruct(xin.shape, BF16),
        compiler_params=pltpu.CompilerParams(dimension_semantics=("arbitrary",),
                                             vmem_limit_bytes=48 * 2**20),
        name="moe",
    )(block_e, n_used, half_block, xin, w1g, w1l, b1g, b1l, w2, b2r)


def _combine_body(tc_ref, rs_ref, slab_ref, y_ref, x1_ref, fg_ref, out_ref, g_ref, sem, *, tm, R):
    i = pl.program_id(0)
    slot = i % 2
    group = lambda ref, r: ref.at[pl.ds(r, RUN_ALIGN)]

    def fetch(tile, slot):
        _for_each_run(tc_ref, rs_ref, tile,
                      lambda lr, gr: pltpu.make_async_copy(group(y_ref, gr), group(g_ref.at[slot], lr), sem.at[slot]).start())

    @pl.when(i == 0)
    def _():
        g_ref[...] = jnp.zeros_like(g_ref)
        fetch(i, slot)

    @pl.when(i + 1 < pl.num_programs(0))
    def _():
        fetch(i + 1, 1 - slot)

    slab = slab_ref[...]
    gates = slab[:, 0:TOP_K].astype(BF16)
    rows = slab[:, TOP_K:2 * TOP_K].astype(I32).astype(jnp.int16)
    jj = lax.broadcasted_iota(I32, (tm, R), 1).astype(jnp.int16)
    sel = jnp.zeros((tm, R), BF16)
    for k in range(TOP_K):
        sel = jnp.where(jj == rows[:, k:k + 1], gates[:, k:k + 1], sel)

    _wait_groups(_group_count(tc_ref, i), y_ref, g_ref.at[slot], sem.at[slot])
    acc = x1_ref[...] + jnp.dot(sel, g_ref[slot], preferred_element_type=F32)
    out_ref[...] = acc * lax.rsqrt(jnp.mean(acc * acc, axis=-1, keepdims=True) + NORM_EPS) * fg_ref[...]


def _combine(tcnt, rowstart, slab, y, x1, final_g):
    T, D = x1.shape
    tm = ROUTE_TILE
    fg = final_g.reshape(1, D)
    return pl.pallas_call(
        functools.partial(_combine_body, tm=tm, R=GROUP_ROWS),
        grid_spec=pltpu.PrefetchScalarGridSpec(
            num_scalar_prefetch=2,
            grid=(T // tm,),
            in_specs=[pl.BlockSpec((tm, LANES), lambda i, *_: (i, 0)),
                      pl.BlockSpec(memory_space=pl.ANY),
                      pl.BlockSpec((tm, D), lambda i, *_: (i, 0)),
                      pl.BlockSpec((1, D), lambda i, *_: (0, 0))],
            out_specs=pl.BlockSpec((tm, D), lambda i, *_: (i, 0)),
            scratch_shapes=[pltpu.VMEM((2, GROUP_ROWS, D), BF16), pltpu.SemaphoreType.DMA((2,))]),
        out_shape=jax.ShapeDtypeStruct((T, D), F32),
        compiler_params=pltpu.CompilerParams(dimension_semantics=("arbitrary",), vmem_limit_bytes=56 * 2**20),
        name="combine",
    )(tcnt, rowstart, slab, y, x1, fg)


def _mixers(x2, B, S, norm1_g, w_in, rel_bias, wg_f, bg_f, wg_b, bg_b, gla_norm_g, w1):
    a1, a4, a16, gla, bcum, w1g, w1l = _in_proj(x2, B, S, norm1_g, w_in, wg_f, bg_f, wg_b, bg_b, w1)
    branches = [_attn_branch(a, rel_bias, d) for a, d in zip((a1, a4, a16), DILATIONS)]
    g = _gla(gla, bcum, gla_norm_g, B, S)
    return [o for o, _ in branches], [l for _, l in branches], g, w1g, w1l


def _moe_layer(h, m_t, slab, tcnt_f, x1, w1g, w1l, b1, w2, b2, final_g):
    T = x1.shape[0]
    nt = T // ROUTE_TILE
    nblk = (T * TOP_K + nt * N_EXPERTS * (RUN_ALIGN - 1)) // MOE_BLOCK + N_EXPERTS
    tcnt = tcnt_f[:, :, 0].astype(I32)
    rowstart, pad0, padn, block_e, half_block, n_used = _plan(tcnt, nblk)
    xin = _dispatch(tcnt, rowstart, pad0, padn, n_used, m_t, h, nblk * MOE_BLOCK)
    y = _moe(block_e, n_used, half_block, xin, w1g, w1l, b1, w2, b2)
    return _combine(tcnt, rowstart, slab, y, x1, final_g)


def kernel(x, norm1_g, w_in, rel_bias, gla_wg_fwd, gla_bg_fwd, gla_wg_bwd, gla_bg_bwd, gla_norm_g, w_out, norm2_g, router_w, router_b, moe_w1, moe_b1, moe_w2, moe_b2, final_g):
    B, S, D = x.shape
    assert w_in.shape[0] == 1, "one layer"
    x2 = x.reshape(B * S, D)
    outs, lses, g, w1g, w1l = _mixers(x2, B, S, norm1_g[0], w_in[0], rel_bias, gla_wg_fwd[0], gla_bg_fwd[0],
                                      gla_wg_bwd[0], gla_bg_bwd[0], gla_norm_g[0], moe_w1[0])
    x1, h, m_t, slab, tcnt = _mix(outs, lses, g, x2, B, S, w_out[0], norm2_g[0], router_w[0], router_b[0])
    out = _moe_layer(h, m_t, slab, tcnt, x1, w1g, w1l, moe_b1[0], moe_w2[0], moe_b2[0], final_g)
    return out.reshape(B, S, D)
```

```python
import functools

import jax
import jax.numpy as jnp
import numpy as np
from jax import lax
from jax.experimental import pallas as pl
from jax.experimental.pallas import tpu as pltpu

F32, BF16, I32 = jnp.float32, jnp.bfloat16, jnp.int32

LANES = 128
HEAD_DIM = 64
ATTN_HEADS = 8
ATTN_W = ATTN_HEADS * HEAD_DIM
HALF_SPAN = 64
ATTN_QBLK = 128
ATTN_KWIN = 256
DILATIONS = (1, 4, 16)
T5_BUCKETS = 32
T5_MAX_DISTANCE = 1024
NEG_INF = -1e30
LOG2_E = 1.4426950408889634
LN_2 = 0.6931471805599453
GLA_KEY_W = 256
GLA_VAL_W = 512
GLA_DV = 128
GLA_RANK = 16
GLA_TAU = 16.0
GLA_CHUNK = 64
N_EXPERTS = 32
TOP_K = 4
SWIGLU_LIMIT = 7.0
SWIGLU_ALPHA = 1.702
NORM_EPS = 1e-5
MOE_BLOCK = 512
W1_SUBTILE = 512
ROUTE_TILE = 512
RUN_ALIGN = 16
RUN_SHIFT = 4
GROUP_ROWS = ROUTE_TILE * TOP_K + N_EXPERTS * RUN_ALIGN

ATTN_COLS = 3 * ATTN_W
GLA_COLS = 2 * GLA_KEY_W + 2 * GLA_VAL_W
Z_COLS = LANES
PROJ_COLS = ATTN_COLS + GLA_COLS + Z_COLS

NT_DIMS = (((1,), (1,)), ((), ()))


def _sigmoid(x):
    return 1.0 / (1.0 + jnp.exp(-x))


def _split3(a):
    h1 = a.astype(BF16)
    r1 = a - h1.astype(F32)
    h2 = r1.astype(BF16)
    h3 = (r1 - h2.astype(F32)).astype(BF16)
    return h1, h2, h3


def _dot_f32(a, b):
    ah, al, _ = _split3(a)
    bh, bl, _ = _split3(b)
    d = lambda u, v: jnp.dot(u, v, preferred_element_type=F32)
    return d(ah, bh) + d(al, bh) + d(ah, bl)


def _split_w1_tile(w_ref, g_ref, l_ref, t_ref):
    sub = t_ref.shape[1]
    for s in range(w_ref.shape[2] // sub):
        wt = w_ref[0, :, s * sub:(s + 1) * sub].T
        out_rows = slice(s * sub // 2, (s + 1) * sub // 2)
        for c in range(wt.shape[1] // LANES):
            cols = slice(c * LANES, (c + 1) * LANES)
            t_ref[c] = wt[:, cols]
            g_ref[0, out_rows, cols] = t_ref[c, pl.ds(0, sub // 2, stride=2), :].astype(BF16)
            l_ref[0, out_rows, cols] = t_ref[c, pl.ds(1, sub // 2, stride=2), :].astype(BF16)


def _in_proj_body(x_ref, g_ref, w_ref, wg_ref, bg_ref, w1_ref, a1_ref, a4_ref, a16_ref, gla_ref, b_ref,
                  w1g_ref, w1l_ref, t_ref, *, tm):
    _split_w1_tile(w1_ref, w1g_ref, w1l_ref, t_ref)

    x = x_ref[...]
    n = x * lax.rsqrt(jnp.mean(x * x, axis=-1, keepdims=True) + NORM_EPS) * g_ref[...]
    p = jnp.dot(n.astype(BF16), w_ref[...], preferred_element_type=F32)
    gla_ref[...] = p[:, ATTN_COLS:ATTN_COLS + GLA_COLS].astype(BF16)

    qkv = p[:, :ATTN_COLS].astype(BF16)
    a1_ref[0, 0] = qkv
    dst = lax.broadcasted_iota(I32, (tm, tm), 0)
    src = lax.broadcasted_iota(I32, (tm, tm), 1)
    for dil, a_ref in ((DILATIONS[1], a4_ref), (DILATIONS[2], a16_ref)):
        per = tm // dil
        perm = (src == (dst % per) * dil + dst // per).astype(F32).astype(BF16)
        moved = jnp.dot(perm, qkv, preferred_element_type=F32).astype(BF16)
        for r in range(dil):
            a_ref[0, r] = moved[r * per:(r + 1) * per]

    z = p[:, ATTN_COLS + GLA_COLS:]
    zz = _dot_f32(z, wg_ref[...]) + bg_ref[...]
    la = (jnp.minimum(zz, 0.0) - jnp.log1p(jnp.exp(-jnp.abs(zz)))) * (1.0 / GLA_TAU)
    C = GLA_CHUNK
    r_i = lax.broadcasted_iota(I32, (C, C), 0)
    c_i = lax.broadcasted_iota(I32, (C, C), 1)
    lower = (r_i >= c_i).astype(BF16)
    upper = (r_i <= c_i).astype(BF16)
    W = GLA_KEY_W
    for j in range(tm // C):
        rows = slice(j * C, (j + 1) * C)
        for tri, cols in ((lower, slice(0, W)), (upper, slice(W, 2 * W))):
            terms = jnp.concatenate(_split3(la[rows, cols]), axis=1)
            cs = jnp.dot(tri, terms, preferred_element_type=F32)
            b_ref[rows, cols] = cs[:, :W] + cs[:, W:2 * W] + cs[:, 2 * W:]


def _in_proj(x2, B, S, norm_g, w_in, wg_f, bg_f, wg_b, bg_b, w1):
    T, D = x2.shape
    tm = 256
    nt = S // tm
    E, _, F2 = w1.shape
    w1_cols = E * F2 // (B * nt)
    per_e = F2 // w1_cols
    assert w1_cols * B * nt == E * F2 and per_e * w1_cols == F2 and w1_cols % W1_SUBTILE == 0
    w1_in = pl.BlockSpec((1, D, w1_cols), lambda b, i: ((b * nt + i) // per_e, 0, (b * nt + i) % per_e))
    w1_out = pl.BlockSpec((1, w1_cols // 2, D), lambda b, i: ((b * nt + i) // per_e, (b * nt + i) % per_e, 0))
    w = jnp.pad(w_in.astype(BF16), ((0, 0), (0, PROJ_COLS - w_in.shape[1])))
    wg = jnp.zeros((Z_COLS, 2 * GLA_KEY_W), F32)
    wg = wg.at[:GLA_RANK, :GLA_KEY_W].set(wg_f).at[GLA_RANK:2 * GLA_RANK, GLA_KEY_W:].set(wg_b)
    bg = jnp.concatenate([bg_f, bg_b]).reshape(1, 2 * GLA_KEY_W)
    row = lambda c: pl.BlockSpec((tm, c), lambda b, i: (b * nt + i, 0))
    full = lambda a: pl.BlockSpec(a.shape, lambda b, i: (0,) * a.ndim)
    res = lambda d: pl.BlockSpec((1, d, tm // d, ATTN_COLS), lambda b, i: (b, 0, i, 0))
    res_shape = lambda d: jax.ShapeDtypeStruct((B, d, S // d, ATTN_COLS), BF16)
    g2 = norm_g.reshape(1, D)
    return pl.pallas_call(
        functools.partial(_in_proj_body, tm=tm),
        grid=(B, nt),
        in_specs=[row(D), full(g2), full(w), full(wg), full(bg), w1_in],
        out_specs=[res(d) for d in DILATIONS] + [row(GLA_COLS), row(2 * GLA_KEY_W), w1_out, w1_out],
        out_shape=[res_shape(d) for d in DILATIONS]
                  + [jax.ShapeDtypeStruct((T, GLA_COLS), BF16), jax.ShapeDtypeStruct((T, 2 * GLA_KEY_W), F32)]
                  + [jax.ShapeDtypeStruct((E, F2 // 2, D), BF16)] * 2,
        scratch_shapes=[pltpu.VMEM((D // LANES, W1_SUBTILE, LANES), F32)],
        compiler_params=pltpu.CompilerParams(dimension_semantics=("arbitrary", "arbitrary"),
                                             vmem_limit_bytes=56 * 2**20),
        name="in_proj",
    )(x2, g2, w, wg, bg, w1)


def _t5_bucket(rel):
    nb = T5_BUCKETS // 2
    max_exact = nb // 2
    n = np.abs(rel)
    large = max_exact + (np.log(np.maximum(n, 1).astype(np.float32) / max_exact)
                         / np.log(T5_MAX_DISTANCE / max_exact) * (nb - max_exact)).astype(np.int32)
    large = np.minimum(large, nb - 1)
    return (np.where(rel > 0, nb, 0) + np.where(n < max_exact, n, large)).astype(np.int32)


def _bucket_tables(dil):
    s = np.arange(ATTN_QBLK)[:, None]
    t = np.arange(ATTN_KWIN)[None, :]
    tabs = []
    for c in (0, HALF_SPAN, 2 * HALF_SPAN):
        off = t - c - s
        tabs.append(np.where(np.abs(off) <= HALF_SPAN, _t5_bucket(off * dil), -1))
    return np.stack(tabs).astype(np.int32)


def _attn_body(rb_ref, idx_ref, q_ref, k_ref, v_ref, o_ref, lse_ref, tab_ref, *, L, TQ):
    first = (pl.program_id(0) == 0) & (pl.program_id(1) == 0) & (pl.program_id(2) == 0)

    @pl.when(first)
    def _():
        for v in range(3):
            idx = idx_ref[v]
            for h in range(ATTN_HEADS):
                fill = lambda j, t, h=h, idx=idx: jnp.where(idx == j, rb_ref[j, h] * LOG2_E, t)
                tab_ref[v, h] = lax.fori_loop(0, T5_BUCKETS, fill, jnp.where(idx < 0, NEG_INF, 0.0).astype(F32))

    i = pl.program_id(2)
    lo = lax.broadcasted_iota(I32, (ATTN_QBLK, LANES), 1) < HEAD_DIM
    scale = HEAD_DIM ** -0.5 * LOG2_E
    for sb in range(TQ // ATTN_QBLK):
        l0 = i * TQ + sb * ATTN_QBLK
        ws = pl.multiple_of(jnp.clip(l0 - HALF_SPAN, 0, L - ATTN_KWIN), HALF_SPAN)
        var = jnp.where(l0 == 0, 0, jnp.where(l0 == L - ATTN_QBLK, 2, 1))
        rows = slice(sb * ATTN_QBLK, (sb + 1) * ATTN_QBLK)
        for hp in range(ATTN_HEADS // 2):
            cols = slice(hp * LANES, (hp + 1) * LANES)
            q2 = q_ref[rows, cols]
            k2 = k_ref[pl.ds(ws, ATTN_KWIN), cols]
            v2 = v_ref[pl.ds(ws, ATTN_KWIN), cols]
            zero = jnp.zeros_like(q2)
            qs = jnp.concatenate([jnp.where(lo, q2, zero), jnp.where(lo, zero, q2)], axis=0)
            s = lax.dot_general(qs, k2, NT_DIMS, preferred_element_type=F32)
            bias = jnp.concatenate([tab_ref[var, 2 * hp], tab_ref[var, 2 * hp + 1]], axis=0)
            logits = s * scale + bias
            m = jnp.max(logits, axis=-1, keepdims=True)
            p = jnp.exp2(logits - m)
            den = jnp.sum(p, axis=-1, keepdims=True)
            pv = jnp.dot(p.astype(BF16), v2, preferred_element_type=F32) / den
            lse = jnp.broadcast_to(m * LN_2 + jnp.log(den), (2 * ATTN_QBLK, LANES))
            o_ref[rows, cols] = jnp.where(lo, pv[:ATTN_QBLK], pv[ATTN_QBLK:]).astype(BF16)
            lse_ref[rows, cols] = jnp.where(lo, lse[:ATTN_QBLK], lse[ATTN_QBLK:])


def _attn_branch(attn, rel_bias, dil):
    B, _, L, _ = attn.shape
    TQ = 256
    idx = jnp.asarray(_bucket_tables(dil))
    body = functools.partial(_attn_body, L=L, TQ=TQ)
    kv = lambda j: pl.BlockSpec((None, None, L, ATTN_W), lambda b, r, i: (b, r, 0, j))
    q_spec = pl.BlockSpec((None, None, TQ, ATTN_W), lambda b, r, i: (b, r, i, 0))
    return pl.pallas_call(
        body,
        grid=(B, dil, L // TQ),
        in_specs=[pl.BlockSpec(memory_space=pltpu.SMEM),
                  pl.BlockSpec(idx.shape, lambda b, r, i: (0, 0, 0)),
                  q_spec, kv(1), kv(2)],
        out_specs=[q_spec, q_spec],
        out_shape=[jax.ShapeDtypeStruct((B, dil, L, ATTN_W), BF16),
                   jax.ShapeDtypeStruct((B, dil, L, ATTN_W), F32)],
        scratch_shapes=[pltpu.VMEM((3, ATTN_HEADS, ATTN_QBLK, ATTN_KWIN), F32)],
        compiler_params=pltpu.CompilerParams(dimension_semantics=("arbitrary",) * 3,
                                             vmem_limit_bytes=48 * 2**20),
        name=f"attn_d{dil}",
    )(rel_bias, idx, attn, attn, attn)


def _gla_body(q_ref, k_ref, v_ref, gate_ref, bf_ref, bb_ref, ng_ref, out_ref, of_ref, ob_ref, st_ref, *, S):
    C = GLA_CHUNK
    nc = S // C
    r = lax.broadcasted_iota(I32, (C, LANES), 0)
    s2 = lax.broadcasted_iota(I32, (C, LANES), 1) % HEAD_DIM
    lo = lax.broadcasted_iota(I32, (C, LANES), 1) < HEAD_DIM
    blockdiag = ((lax.broadcasted_iota(I32, (2 * GLA_DV, LANES), 0) < GLA_DV)
                 == (lax.broadcasted_iota(I32, (2 * GLA_DV, LANES), 1) < HEAD_DIM))
    st_ref[...] = jnp.zeros_like(st_ref)

    def one_chunk(ci, direction):
        mask = (r >= s2) if direction == 0 else (r <= s2)
        b_ref, o_ref = (bf_ref, of_ref) if direction == 0 else (bb_ref, ob_ref)
        rows = pl.ds(pl.multiple_of(ci * C, C), C)
        q = q_ref[0, rows, :].astype(F32) * (HEAD_DIM ** -0.5)
        k = k_ref[0, rows, :].astype(F32)
        v = v_ref[0, rows, :]
        b = b_ref[0, rows, :]
        g_tot = b[C - 1:C, :] if direction == 0 else b[0:1, :]
        qd = (q * jnp.exp(b)).astype(BF16)
        kd = (k * jnp.exp(-b)).astype(BF16)
        kst = (k * jnp.exp(g_tot - b)).astype(BF16)
        zk = jnp.zeros_like(kd)
        ks = jnp.concatenate([jnp.where(lo, kd, zk), jnp.where(lo, zk, kd)], axis=0)
        a = lax.dot_general(qd, ks, NT_DIMS, preferred_element_type=F32)
        a = jnp.where(mask, a, 0.0).astype(BF16)
        zv = jnp.zeros((C, GLA_DV), BF16)
        vblk = jnp.concatenate([jnp.concatenate([v[:, :GLA_DV], zv], axis=1),
                                jnp.concatenate([zv, v[:, GLA_DV:]], axis=1)], axis=0)
        st = st_ref[direction]
        o_ref[rows, :] = (jnp.dot(a, vblk, preferred_element_type=F32)
                          + lax.dot_general(qd, st.astype(BF16), NT_DIMS, preferred_element_type=F32))
        d_st = jnp.dot(v.astype(F32).T.astype(BF16), kst, preferred_element_type=F32)
        st_ref[direction] = st * jnp.exp(g_tot) + jnp.where(blockdiag, d_st, 0.0)

    def step(i, carry):
        one_chunk(i, 0)
        one_chunk(nc - 1 - i, 1)
        return carry

    lax.fori_loop(0, nc, step, 0, unroll=8)

    tr = 256
    def epilogue(i, carry):
        rows = pl.ds(pl.multiple_of(i * tr, tr), tr)
        for hh in range(2):
            cols = slice(hh * GLA_DV, (hh + 1) * GLA_DV)
            o = of_ref[rows, cols] + ob_ref[rows, cols]
            o = o * lax.rsqrt(jnp.mean(o * o, axis=-1, keepdims=True) + NORM_EPS) * ng_ref[0, :, cols]
            gate = gate_ref[0, rows, cols].astype(F32)
            out_ref[0, rows, cols] = (o * (gate * _sigmoid(gate))).astype(BF16)
        return carry

    lax.fori_loop(0, S // tr, epilogue, 0)


def _gla(gla, bcum, norm_g, B, S):
    g3 = gla.reshape(B, S, GLA_COLS)
    b3 = bcum.reshape(B, S, 2 * GLA_KEY_W)
    ng = norm_g.reshape(2, 1, 2 * GLA_DV)
    body = functools.partial(_gla_body, S=S)
    spec = lambda w, off: pl.BlockSpec((1, S, w), lambda b, p: (b, 0, off + p))
    out = pl.pallas_call(
        body,
        grid=(B, 2),
        in_specs=[spec(LANES, 0),
                  spec(LANES, GLA_KEY_W // LANES),
                  spec(2 * GLA_DV, (2 * GLA_KEY_W) // (2 * GLA_DV)),
                  spec(2 * GLA_DV, (2 * GLA_KEY_W + GLA_VAL_W) // (2 * GLA_DV)),
                  spec(LANES, 0),
                  spec(LANES, GLA_KEY_W // LANES),
                  pl.BlockSpec((1, 1, 2 * GLA_DV), lambda b, p: (p, 0, 0))],
        out_specs=pl.BlockSpec((1, S, 2 * GLA_DV), lambda b, p: (b, 0, p)),
        out_shape=jax.ShapeDtypeStruct((B, S, GLA_VAL_W), BF16),
        scratch_shapes=[pltpu.VMEM((S, 2 * GLA_DV), F32), pltpu.VMEM((S, 2 * GLA_DV), F32),
                        pltpu.VMEM((2, 2 * GLA_DV, LANES), F32)],
        compiler_params=pltpu.CompilerParams(dimension_semantics=("arbitrary", "arbitrary"),
                                             vmem_limit_bytes=56 * 2**20),
        name="gla",
    )(g3, g3, g3, g3, b3, b3, ng)
    return out.reshape(B * S, GLA_VAL_W)


def _natural_rows(blk_ref, u_ref, dil, tm):
    if dil == 1:
        return blk_ref[0, 0].astype(F32)
    for r in range(dil):
        for c in range(ATTN_W // LANES):
            u_ref[c, pl.ds(r, tm // dil, stride=dil), :] = blk_ref[0, r, :, c * LANES:(c + 1) * LANES].astype(F32)
    return jnp.concatenate([u_ref[c] for c in range(ATTN_W // LANES)], axis=1)


def _mix_body(o1, o2, o3, l1, l2, l3, g_ref, x_ref, wo_ref, n2_ref, rwh_ref, rwl_ref, rb_ref,
              x1_ref, h_ref, m_ref, slab_ref, tcnt_ref, u_ref, *, tm):
    ls = [_natural_rows(l, u_ref, d, tm) for l, d in zip((l1, l2, l3), DILATIONS)]
    m = jnp.maximum(jnp.maximum(ls[0], ls[1]), ls[2])
    es = [jnp.exp(l - m) for l in ls]
    a = es[0] * _natural_rows(o1, u_ref, DILATIONS[0], tm)
    a = a + es[1] * _natural_rows(o2, u_ref, DILATIONS[1], tm)
    a = a + es[2] * _natural_rows(o3, u_ref, DILATIONS[2], tm)
    a = a / (es[0] + es[1] + es[2])
    mix = (jnp.dot(a.astype(BF16), wo_ref[:ATTN_W, :], preferred_element_type=F32)
           + jnp.dot(g_ref[...], wo_ref[ATTN_W:, :], preferred_element_type=F32))
    x1 = x_ref[...] + mix
    x1_ref[...] = x1
    h = x1 * lax.rsqrt(jnp.mean(x1 * x1, axis=-1, keepdims=True) + NORM_EPS) * n2_ref[...]
    h_ref[...] = h

    hh, hl, _ = _split3(h)
    dg = lambda u, v: lax.dot_general(u, v, NT_DIMS, preferred_element_type=F32)
    logits = dg(rwh_ref[...], hh) + dg(rwl_ref[...], hh) + dg(rwh_ref[...], hl) + rb_ref[...]
    ie = lax.broadcasted_iota(I32, (N_EXPERTS, tm), 0)
    cur = logits
    vals, idxs = [], []
    for _ in range(TOP_K):
        mx = jnp.max(cur, axis=0, keepdims=True)
        ix = jnp.min(jnp.where(cur == mx, ie, N_EXPERTS), axis=0, keepdims=True)
        vals.append(mx)
        idxs.append(ix)
        cur = jnp.where(ie == ix, -jnp.inf, cur)
    ex = [jnp.exp(v - vals[0]) for v in vals]
    den = ex[0] + ex[1] + ex[2] + ex[3]
    gates = [e / den for e in ex]

    onehots = [ie == ix for ix in idxs]
    chosen = (onehots[0] | onehots[1] | onehots[2] | onehots[3]).astype(F32)
    tr = lax.broadcasted_iota(I32, (tm, tm), 0)
    tc = lax.broadcasted_iota(I32, (tm, tm), 1)
    before = jnp.dot(chosen.astype(BF16), (tr < tc).astype(BF16), preferred_element_type=F32)
    count = jnp.broadcast_to(jnp.sum(chosen, axis=1, keepdims=True), (N_EXPERTS, LANES))
    tcnt_ref[0] = count
    run = jnp.floor((count + (RUN_ALIGN - 1.0)) * (1.0 / RUN_ALIGN)) * RUN_ALIGN
    er = lax.broadcasted_iota(I32, (N_EXPERTS, N_EXPERTS), 0)
    ec = lax.broadcasted_iota(I32, (N_EXPERTS, N_EXPERTS), 1)
    run_start = jnp.dot((ec < er).astype(BF16), run.astype(BF16), preferred_element_type=F32)
    where_ = before + run_start[:, 0:1]
    locs = [jnp.sum(jnp.where(oh, where_, 0.0), axis=0, keepdims=True) for oh in onehots]

    def pack(rows, n):
        sub = lax.broadcasted_iota(I32, (n, tm), 0)
        out = jnp.zeros((n, tm), rows[0].dtype)
        for k, row in enumerate(rows):
            out = jnp.where(sub == k, row, out)
        return out

    m_ref[...] = pack([l.astype(I32) for l in locs], 8)
    slab = pack(gates + locs, 8)
    slab_ref[...] = jnp.concatenate([slab, jnp.zeros((LANES - 8, tm), F32)], axis=0).T


def _mix(outs, lses, g, x2, B, S, w_out, norm2_g, router_w, router_b):
    T, D = x2.shape
    tm = ROUTE_TILE
    nt = S // tm
    wo = w_out.astype(BF16)
    rwt = router_w.T
    rwh = rwt.astype(BF16)
    rwl = (rwt - rwh.astype(F32)).astype(BF16)
    rb = router_b.reshape(N_EXPERTS, 1)
    n2 = norm2_g.reshape(1, D)
    row = lambda c: pl.BlockSpec((tm, c), lambda b, i: (b * nt + i, 0))
    full = lambda a: pl.BlockSpec(a.shape, lambda b, i: (0,) * a.ndim)
    col = pl.BlockSpec((8, tm), lambda b, i: (0, b * nt + i))
    res = [pl.BlockSpec((1, d, tm // d, ATTN_W), lambda b, i: (b, 0, i, 0)) for d in DILATIONS]
    return pl.pallas_call(
        functools.partial(_mix_body, tm=tm),
        grid=(B, nt),
        in_specs=res + res + [row(GLA_VAL_W), row(D), full(wo), full(n2), full(rwh), full(rwl), full(rb)],
        out_specs=[row(D), row(D), col, row(LANES),
                   pl.BlockSpec((1, N_EXPERTS, LANES), lambda b, i: (b * nt + i, 0, 0))],
        out_shape=[jax.ShapeDtypeStruct((T, D), F32),
                   jax.ShapeDtypeStruct((T, D), F32),
                   jax.ShapeDtypeStruct((8, T), I32),
                   jax.ShapeDtypeStruct((T, LANES), F32),
                   jax.ShapeDtypeStruct((T // tm, N_EXPERTS, LANES), F32)],
        scratch_shapes=[pltpu.VMEM((ATTN_W // LANES, tm, LANES), F32)],
        compiler_params=pltpu.CompilerParams(dimension_semantics=("arbitrary", "arbitrary"),
                                             vmem_limit_bytes=48 * 2**20),
        name="mix_router",
    )(*outs, *lses, g, x2, wo, n2, rwh, rwl, rb)


def _run_groups(c):
    return (c + (RUN_ALIGN - 1)) >> RUN_SHIFT


def _plan_body(tc_ref, rs_ref, p0_ref, pn_ref, be_ref, hv_ref, nu_ref, *, nt, nblk):
    shift = MOE_BLOCK.bit_length() - 1
    start = jnp.int32(0)
    for ex in range(N_EXPERTS):
        def run(i, tot, start=start, ex=ex):
            rs_ref[i, ex] = start + tot
            return tot + (_run_groups(tc_ref[i, ex]) << RUN_SHIFT)

        tot = lax.fori_loop(0, nt, run, jnp.int32(0))
        nb = (tot + (MOE_BLOCK - 1)) >> shift
        p0_ref[ex] = start + tot
        pn_ref[ex] = ((nb << shift) - tot) >> RUN_SHIFT
        b0 = start >> shift

        def fill(j, carry, b0=b0, ex=ex):
            be_ref[b0 + j] = jnp.int32(ex)
            hv_ref[b0 + j] = jnp.int32(0)
            return carry

        lax.fori_loop(0, nb, fill, 0)
        last_rows = tot - ((nb - 1) << shift)

        @pl.when((nb > 0) & (last_rows <= MOE_BLOCK // 2))
        def _(b0=b0, nb=nb):
            hv_ref[b0 + nb - 1] = jnp.int32(1)

        start = start + (nb << shift)
    used = start >> shift

    def tail(j, carry):
        be_ref[j] = jnp.int32(N_EXPERTS - 1)
        hv_ref[j] = jnp.int32(0)
        return carry

    lax.fori_loop(used, nblk, tail, 0)
    nu_ref[0] = used


def _plan(tcnt, nblk):
    nt = tcnt.shape[0]
    smem = pl.BlockSpec(memory_space=pltpu.SMEM)
    return pl.pallas_call(
        functools.partial(_plan_body, nt=nt, nblk=nblk),
        in_specs=[smem],
        out_specs=[smem] * 6,
        out_shape=[jax.ShapeDtypeStruct(tcnt.shape, I32),
                   jax.ShapeDtypeStruct((N_EXPERTS,), I32),
                   jax.ShapeDtypeStruct((N_EXPERTS,), I32),
                   jax.ShapeDtypeStruct((nblk,), I32),
                   jax.ShapeDtypeStruct((nblk,), I32),
                   jax.ShapeDtypeStruct((1,), I32)],
        name="plan",
    )(tcnt)


def _for_each_run(tc_ref, rs_ref, i, fn):
    start = jnp.int32(0)
    for ex in range(N_EXPERTS):
        ng = _run_groups(tc_ref[i, ex])
        g0 = rs_ref[i, ex]

        def body(q, carry, start=start, g0=g0):
            fn(pl.multiple_of(start + q * RUN_ALIGN, RUN_ALIGN), pl.multiple_of(g0 + q * RUN_ALIGN, RUN_ALIGN))
            return carry

        lax.fori_loop(0, ng, body, 0)
        start = start + (ng << RUN_SHIFT)


def _group_count(tc_ref, i):
    n = jnp.int32(0)
    for ex in range(N_EXPERTS):
        n = n + _run_groups(tc_ref[i, ex])
    return n


def _wait_groups(n, src_ref, dst_ref, sem):
    def waits(count, rows):
        def body(q, carry):
            pltpu.make_async_copy(src_ref.at[pl.ds(0, rows)], dst_ref.at[pl.ds(0, rows)], sem).wait()
            return carry
        lax.fori_loop(0, count, body, 0)

    waits(n >> 6, 64 * RUN_ALIGN)
    waits((n >> 3) & 7, 8 * RUN_ALIGN)
    waits(n & 7, RUN_ALIGN)


def _dispatch_body(tc_ref, rs_ref, p0_ref, pn_ref, nu_ref, m_ref, h_ref, xin_ref, g_ref, z_ref, sem, zsem, *, tm, R):
    i = pl.program_id(0)
    slot = i % 2
    rows = m_ref[...].astype(jnp.int16)
    jj = lax.broadcasted_iota(I32, (R, tm), 0).astype(jnp.int16)
    hit = (jj == rows[0:1]) | (jj == rows[1:2]) | (jj == rows[2:3]) | (jj == rows[3:4])
    onehot = jnp.where(hit, jnp.ones((), BF16), jnp.zeros((), BF16))
    g_ref[slot] = jnp.dot(onehot, h_ref[...].astype(BF16), preferred_element_type=F32).astype(BF16)

    group = lambda ref, r: ref.at[pl.ds(r, RUN_ALIGN)]

    @pl.when(i > 0)
    def _():
        _wait_groups(_group_count(tc_ref, i - 1), g_ref.at[1 - slot], xin_ref, sem.at[1 - slot])

    _for_each_run(tc_ref, rs_ref, i,
                  lambda lr, gr: pltpu.make_async_copy(group(g_ref.at[slot], lr), group(xin_ref, gr), sem.at[slot]).start())

    @pl.when(i == pl.num_programs(0) - 1)
    def _():
        _wait_groups(_group_count(tc_ref, i), g_ref.at[slot], xin_ref, sem.at[slot])

    @pl.when(i == 0)
    def _():
        z_ref[...] = jnp.zeros_like(z_ref)
        total = jnp.int32(0)
        for ex in range(N_EXPERTS):
            def body(q, carry, ex=ex):
                pltpu.make_async_copy(group(z_ref, 0), group(xin_ref, pl.multiple_of(p0_ref[ex] + q * RUN_ALIGN, RUN_ALIGN)), zsem).start()
                return carry
            lax.fori_loop(0, pn_ref[ex], body, 0)
            total = total + pn_ref[ex]
        _wait_groups(total, z_ref, xin_ref, zsem)

        block = lambda j: xin_ref.at[pl.ds(pl.multiple_of(j * MOE_BLOCK, MOE_BLOCK), MOE_BLOCK)]
        n_blocks = xin_ref.shape[0] // MOE_BLOCK

        def fill(j, carry):
            pltpu.make_async_copy(z_ref, block(j), zsem).start()
            return carry

        def fill_wait(j, carry):
            pltpu.make_async_copy(z_ref, block(j), zsem).wait()
            return carry

        lax.fori_loop(nu_ref[0], n_blocks, fill, 0)
        lax.fori_loop(nu_ref[0], n_blocks, fill_wait, 0)


def _dispatch(tcnt, rowstart, pad0, padn, n_used, m_t, h, n_rows):
    T, D = h.shape
    tm = ROUTE_TILE
    col = pl.BlockSpec((8, tm), lambda i, *_: (0, i))
    return pl.pallas_call(
        functools.partial(_dispatch_body, tm=tm, R=GROUP_ROWS),
        grid_spec=pltpu.PrefetchScalarGridSpec(
            num_scalar_prefetch=5,
            grid=(T // tm,),
            in_specs=[col, pl.BlockSpec((tm, D), lambda i, *_: (i, 0))],
            out_specs=pl.BlockSpec(memory_space=pl.ANY),
            scratch_shapes=[pltpu.VMEM((2, GROUP_ROWS, D), BF16), pltpu.VMEM((MOE_BLOCK, D), BF16),
                            pltpu.SemaphoreType.DMA((2,)), pltpu.SemaphoreType.DMA(())]),
        out_shape=jax.ShapeDtypeStruct((n_rows, D), BF16),
        compiler_params=pltpu.CompilerParams(dimension_semantics=("arbitrary",), vmem_limit_bytes=56 * 2**20),
        name="dispatch",
    )(tcnt, rowstart, pad0, padn, n_used, m_t, h)


def _moe_body(be_ref, nu_ref, hv_ref, x_ref, w1g_ref, w1l_ref, b1g_ref, b1l_ref, w2_ref, b2_ref, y_ref):
    del be_ref
    j = pl.program_id(0)
    live = j < nu_ref[0]
    half = hv_ref[j] == 1
    top = slice(0, MOE_BLOCK // 2)
    rest = slice(MOE_BLOCK // 2, MOE_BLOCK)

    def expert_mlp(rows):
        x = x_ref[rows, :]
        hg = lax.dot_general(x, w1g_ref[0], NT_DIMS, preferred_element_type=F32) + b1g_ref[0]
        hl = lax.dot_general(x, w1l_ref[0], NT_DIMS, preferred_element_type=F32) + b1l_ref[0]
        xg = jnp.minimum(hg, SWIGLU_LIMIT)
        xl = jnp.clip(hl, -SWIGLU_LIMIT, SWIGLU_LIMIT)
        act = xg * _sigmoid(SWIGLU_ALPHA * xg) * (xl + 1.0)
        y = jnp.dot(act.astype(BF16), w2_ref[0].astype(BF16), preferred_element_type=F32) + b2_ref[0]
        y_ref[rows, :] = y.astype(BF16)

    @pl.when(live & jnp.logical_not(half))
    def _():
        expert_mlp(slice(0, MOE_BLOCK))

    @pl.when(live & half)
    def _():
        expert_mlp(top)
        y_ref[rest, :] = jnp.zeros((MOE_BLOCK // 2, y_ref.shape[1]), y_ref.dtype)

    @pl.when(jnp.logical_not(live))
    def _():
        y_ref[...] = jnp.zeros_like(y_ref)


def _moe(block_e, n_used, half_block, xin, w1g, w1l, b1, w2, b2):
    n_rows = xin.shape[0]
    D = w2.shape[2]
    F = w2.shape[1]
    nblk = n_rows // MOE_BLOCK
    b1g = b1[:, 0::2].reshape(N_EXPERTS, 1, F)
    b1l = b1[:, 1::2].reshape(N_EXPERTS, 1, F)
    b2r = b2.reshape(N_EXPERTS, 1, D)
    xspec = pl.BlockSpec((MOE_BLOCK, D), lambda j, be, nu, hv: (jnp.minimum(j, nu[0] - 1), 0))
    yspec = pl.BlockSpec((MOE_BLOCK, D), lambda j, be, nu, hv: (j, 0))
    wspec = lambda a: pl.BlockSpec((1,) + a.shape[1:], lambda j, be, nu, hv: (be[j], 0, 0))
    return pl.pallas_call(
        _moe_body,
        grid_spec=pltpu.PrefetchScalarGridSpec(
            num_scalar_prefetch=3,
            grid=(nblk,),
            in_specs=[xspec, wspec(w1g), wspec(w1l), wspec(b1g), wspec(b1l), wspec(w2), wspec(b2r)],
            out_specs=yspec),
        out_shape=jax.ShapeDtypeStruct(xin.shape, BF16),
        compiler_params=pltpu.CompilerParams(dimension_semantics=("arbitrary",),
                                             vmem_limit_bytes=48 * 2**20),
        name="moe",
    )(block_e, n_used, half_block, xin, w1g, w1l, b1g, b1l, w2, b2r)


def _combine_body(tc_ref, rs_ref, slab_ref, y_ref, x1_ref, fg_ref, out_ref, g_ref, sem, *, tm, R):
    i = pl.program_id(0)
    slot = i % 2
    group = lambda ref, r: ref.at[pl.ds(r, RUN_ALIGN)]

    def fetch(tile, slot):
        _for_each_run(tc_ref, rs_ref, tile,
                      lambda lr, gr: pltpu.make_async_copy(group(y_ref, gr), group(g_ref.at[slot], lr), sem.at[slot]).start())

    @pl.when(i == 0)
    def _():
        g_ref[...] = jnp.zeros_like(g_ref)
        fetch(i, slot)

    @pl.when(i + 1 < pl.num_programs(0))
    def _():
        fetch(i + 1, 1 - slot)

    slab = slab_ref[...]
    gates = slab[:, 0:TOP_K].astype(BF16)
    rows = slab[:, TOP_K:2 * TOP_K].astype(I32).astype(jnp.int16)
    jj = lax.broadcasted_iota(I32, (tm, R), 1).astype(jnp.int16)
    sel = jnp.zeros((tm, R), BF16)
    for k in range(TOP_K):
        sel = jnp.where(jj == rows[:, k:k + 1], gates[:, k:k + 1], sel)

    _wait_groups(_group_count(tc_ref, i), y_ref, g_ref.at[slot], sem.at[slot])
    acc = x1_ref[...] + jnp.dot(sel, g_ref[slot], preferred_element_type=F32)
    out_ref[...] = acc * lax.rsqrt(jnp.mean(acc * acc, axis=-1, keepdims=True) + NORM_EPS) * fg_ref[...]


def _combine(tcnt, rowstart, slab, y, x1, final_g):
    T, D = x1.shape
    tm = ROUTE_TILE
    fg = final_g.reshape(1, D)
    return pl.pallas_call(
        functools.partial(_combine_body, tm=tm, R=GROUP_ROWS),
        grid_spec=pltpu.PrefetchScalarGridSpec(
            num_scalar_prefetch=2,
            grid=(T // tm,),
            in_specs=[pl.BlockSpec((tm, LANES), lambda i, *_: (i, 0)),
                      pl.BlockSpec(memory_space=pl.ANY),
                      pl.BlockSpec((tm, D), lambda i, *_: (i, 0)),
                      pl.BlockSpec((1, D), lambda i, *_: (0, 0))],
            out_specs=pl.BlockSpec((tm, D), lambda i, *_: (i, 0)),
            scratch_shapes=[pltpu.VMEM((2, GROUP_ROWS, D), BF16), pltpu.SemaphoreType.DMA((2,))]),
        out_shape=jax.ShapeDtypeStruct((T, D), F32),
        compiler_params=pltpu.CompilerParams(dimension_semantics=("arbitrary",), vmem_limit_bytes=56 * 2**20),
        name="combine",
    )(tcnt, rowstart, slab, y, x1, fg)


def _mixers(x2, B, S, norm1_g, w_in, rel_bias, wg_f, bg_f, wg_b, bg_b, gla_norm_g, w1):
    a1, a4, a16, gla, bcum, w1g, w1l = _in_proj(x2, B, S, norm1_g, w_in, wg_f, bg_f, wg_b, bg_b, w1)
    branches = [_attn_branch(a, rel_bias, d) for a, d in zip((a1, a4, a16), DILATIONS)]
    g = _gla(gla, bcum, gla_norm_g, B, S)
    return [o for o, _ in branches], [l for _, l in branches], g, w1g, w1l


def _moe_layer(h, m_t, slab, tcnt_f, x1, w1g, w1l, b1, w2, b2, final_g):
    T = x1.shape[0]
    nt = T // ROUTE_TILE
    nblk = (T * TOP_K + nt * N_EXPERTS * (RUN_ALIGN - 1)) // MOE_BLOCK + N_EXPERTS
    tcnt = tcnt_f[:, :, 0].astype(I32)
    rowstart, pad0, padn, block_e, half_block, n_used = _plan(tcnt, nblk)
    xin = _dispatch(tcnt, rowstart, pad0, padn, n_used, m_t, h, nblk * MOE_BLOCK)
    y = _moe(block_e, n_used, half_block, xin, w1g, w1l, b1, w2, b2)
    return _combine(tcnt, rowstart, slab, y, x1, final_g)


def kernel(x, norm1_g, w_in, rel_bias, gla_wg_fwd, gla_bg_fwd, gla_wg_bwd, gla_bg_bwd, gla_norm_g, w_out, norm2_g, router_w, router_b, moe_w1, moe_b1, moe_w2, moe_b2, final_g):
    B, S, D = x.shape
    assert w_in.shape[0] == 1, "one layer"
    x2 = x.reshape(B * S, D)
    outs, lses, g, w1g, w1l = _mixers(x2, B, S, norm1_g[0], w_in[0], rel_bias, gla_wg_fwd[0], gla_bg_fwd[0],
                                      gla_wg_bwd[0], gla_bg_bwd[0], gla_norm_g[0], moe_w1[0])
    x1, h, m_t, slab, tcnt = _mix(outs, lses, g, x2, B, S, w_out[0], norm2_g[0], router_w[0], router_b[0])
    out = _moe_layer(h, m_t, slab, tcnt, x1, w1g, w1l, moe_b1[0], moe_w2[0], moe_b2[0], final_g)
    return out.reshape(B, S, D)
```

```python
import functools

import jax
import jax.numpy as jnp
import numpy as np
from jax import lax
from jax.experimental import pallas as pl
from jax.experimental.pallas import tpu as pltpu

F32, BF16, I32 = jnp.float32, jnp.bfloat16, jnp.int32

LANES = 128
HEAD_DIM = 64
ATTN_HEADS = 8
ATTN_W = ATTN_HEADS * HEAD_DIM
HALF_SPAN = 64
ATTN_QBLK = 128
ATTN_KWIN = 256
ATTN_STEP_ROWS = 512
DILATIONS = (1, 4, 16)
T5_BUCKETS = 32
T5_MAX_DISTANCE = 1024
NEG_INF = -1e30
LOG2_E = 1.4426950408889634
LN_2 = 0.6931471805599453
GLA_KEY_W = 256
GLA_VAL_W = 512
GLA_DV = 128
GLA_RANK = 16
GLA_TAU = 16.0
GLA_CHUNK = 64
N_EXPERTS = 32
TOP_K = 4
SWIGLU_LIMIT = 7.0
SWIGLU_ALPHA = 1.702
NORM_EPS = 1e-5
MOE_BLOCK = 1024
MOE_PART = 256
W1_SUBTILE = 512
ROUTE_TILE = 512
RUN_ALIGN = 16
RUN_SHIFT = 4
GROUP_ROWS = ROUTE_TILE * TOP_K + N_EXPERTS * RUN_ALIGN

ATTN_COLS = 3 * ATTN_W
GLA_COLS = 2 * GLA_KEY_W + 2 * GLA_VAL_W
Z_COLS = LANES
PROJ_COLS = ATTN_COLS + GLA_COLS + Z_COLS

NT_DIMS = (((1,), (1,)), ((), ()))


def _sigmoid(x):
    return 1.0 / (1.0 + jnp.exp(-x))


def _split3(a):
    h1 = a.astype(BF16)
    r1 = a - h1.astype(F32)
    h2 = r1.astype(BF16)
    h3 = (r1 - h2.astype(F32)).astype(BF16)
    return h1, h2, h3


def _dot_f32(a, b):
    ah, al, _ = _split3(a)
    bh, bl, _ = _split3(b)
    d = lambda u, v: jnp.dot(u, v, preferred_element_type=F32)
    return d(ah, bh) + d(al, bh) + d(ah, bl)


def _split_w1_tile(w_ref, g_ref, l_ref, t_ref):
    sub = t_ref.shape[1]
    for s in range(w_ref.shape[2] // sub):
        wt = w_ref[0, :, s * sub:(s + 1) * sub].T
        out_rows = slice(s * sub // 2, (s + 1) * sub // 2)
        for c in range(wt.shape[1] // LANES):
            cols = slice(c * LANES, (c + 1) * LANES)
            t_ref[c] = wt[:, cols]
            g_ref[0, out_rows, cols] = t_ref[c, pl.ds(0, sub // 2, stride=2), :].astype(BF16)
            l_ref[0, out_rows, cols] = t_ref[c, pl.ds(1, sub // 2, stride=2), :].astype(BF16)


def _in_proj_body(x_ref, g_ref, w_ref, wg_ref, bg_ref, w1_ref, a1_ref, a4_ref, a16_ref, gla_ref, b_ref,
                  w1g_ref, w1l_ref, t_ref, *, tm):
    _split_w1_tile(w1_ref, w1g_ref, w1l_ref, t_ref)

    x = x_ref[...]
    n = x * lax.rsqrt(jnp.mean(x * x, axis=-1, keepdims=True) + NORM_EPS) * g_ref[...]
    p = jnp.dot(n.astype(BF16), w_ref[...], preferred_element_type=F32)
    gla_ref[...] = p[:, ATTN_COLS:ATTN_COLS + GLA_COLS].astype(BF16)

    qkv = p[:, :ATTN_COLS].astype(BF16)
    a1_ref[0, 0] = qkv
    dst = lax.broadcasted_iota(I32, (tm, tm), 0)
    src = lax.broadcasted_iota(I32, (tm, tm), 1)
    for dil, a_ref in ((DILATIONS[1], a4_ref), (DILATIONS[2], a16_ref)):
        per = tm // dil
        perm = (src == (dst % per) * dil + dst // per).astype(F32).astype(BF16)
        moved = jnp.dot(perm, qkv, preferred_element_type=F32).astype(BF16)
        for r in range(dil):
            a_ref[0, r] = moved[r * per:(r + 1) * per]

    z = p[:, ATTN_COLS + GLA_COLS:]
    zz = _dot_f32(z, wg_ref[...]) + bg_ref[...]
    la = (jnp.minimum(zz, 0.0) - jnp.log1p(jnp.exp(-jnp.abs(zz)))) * (1.0 / GLA_TAU)
    C = GLA_CHUNK
    r_i = lax.broadcasted_iota(I32, (C, C), 0)
    c_i = lax.broadcasted_iota(I32, (C, C), 1)
    lower = (r_i >= c_i).astype(BF16)
    upper = (r_i <= c_i).astype(BF16)
    W = GLA_KEY_W
    for j in range(tm // C):
        rows = slice(j * C, (j + 1) * C)
        for tri, cols in ((lower, slice(0, W)), (upper, slice(W, 2 * W))):
            terms = jnp.concatenate(_split3(la[rows, cols]), axis=1)
            cs = jnp.dot(tri, terms, preferred_element_type=F32)
            b_ref[rows, cols] = cs[:, :W] + cs[:, W:2 * W] + cs[:, 2 * W:]


def _in_proj(x2, B, S, norm_g, w_in, wg_f, bg_f, wg_b, bg_b, w1):
    T, D = x2.shape
    tm = 256
    nt = S // tm
    E, _, F2 = w1.shape
    w1_cols = E * F2 // (B * nt)
    per_e = F2 // w1_cols
    assert w1_cols * B * nt == E * F2 and per_e * w1_cols == F2 and w1_cols % W1_SUBTILE == 0
    w1_in = pl.BlockSpec((1, D, w1_cols), lambda b, i: ((b * nt + i) // per_e, 0, (b * nt + i) % per_e))
    w1_out = pl.BlockSpec((1, w1_cols // 2, D), lambda b, i: ((b * nt + i) // per_e, (b * nt + i) % per_e, 0))
    w = jnp.pad(w_in.astype(BF16), ((0, 0), (0, PROJ_COLS - w_in.shape[1])))
    wg = jnp.zeros((Z_COLS, 2 * GLA_KEY_W), F32)
    wg = wg.at[:GLA_RANK, :GLA_KEY_W].set(wg_f).at[GLA_RANK:2 * GLA_RANK, GLA_KEY_W:].set(wg_b)
    bg = jnp.concatenate([bg_f, bg_b]).reshape(1, 2 * GLA_KEY_W)
    row = lambda c: pl.BlockSpec((tm, c), lambda b, i: (b * nt + i, 0))
    full = lambda a: pl.BlockSpec(a.shape, lambda b, i: (0,) * a.ndim)
    res = lambda d: pl.BlockSpec((1, d, tm // d, ATTN_COLS), lambda b, i: (b, 0, i, 0))
    res_shape = lambda d: jax.ShapeDtypeStruct((B, d, S // d, ATTN_COLS), BF16)
    g2 = norm_g.reshape(1, D)
    return pl.pallas_call(
        functools.partial(_in_proj_body, tm=tm),
        grid=(B, nt),
        in_specs=[row(D), full(g2), full(w), full(wg), full(bg), w1_in],
        out_specs=[res(d) for d in DILATIONS] + [row(GLA_COLS), row(2 * GLA_KEY_W), w1_out, w1_out],
        out_shape=[res_shape(d) for d in DILATIONS]
                  + [jax.ShapeDtypeStruct((T, GLA_COLS), BF16), jax.ShapeDtypeStruct((T, 2 * GLA_KEY_W), F32)]
                  + [jax.ShapeDtypeStruct((E, F2 // 2, D), BF16)] * 2,
        scratch_shapes=[pltpu.VMEM((D // LANES, W1_SUBTILE, LANES), F32)],
        compiler_params=pltpu.CompilerParams(dimension_semantics=("arbitrary", "arbitrary"),
                                             vmem_limit_bytes=56 * 2**20),
        name="in_proj",
    )(x2, g2, w, wg, bg, w1)


def _t5_bucket(rel):
    nb = T5_BUCKETS // 2
    max_exact = nb // 2
    n = np.abs(rel)
    large = max_exact + (np.log(np.maximum(n, 1).astype(np.float32) / max_exact)
                         / np.log(T5_MAX_DISTANCE / max_exact) * (nb - max_exact)).astype(np.int32)
    large = np.minimum(large, nb - 1)
    return (np.where(rel > 0, nb, 0) + np.where(n < max_exact, n, large)).astype(np.int32)


def _bucket_tables(dil):
    s = np.arange(ATTN_QBLK)[:, None]
    t = np.arange(ATTN_KWIN)[None, :]
    tabs = []
    for c in (0, HALF_SPAN, 2 * HALF_SPAN):
        off = t - c - s
        tabs.append(np.where(np.abs(off) <= HALF_SPAN, _t5_bucket(off * dil), -1))
    return np.stack(tabs).astype(np.int32)


def _attn_body(rb_ref, idx_ref, q_ref, k_ref, v_ref, o_ref, lse_ref, tab_ref, *, L, TQ):
    first = (pl.program_id(0) == 0) & (pl.program_id(1) == 0) & (pl.program_id(2) == 0)

    @pl.when(first)
    def _():
        for v in range(3):
            idx = idx_ref[v]
            for h in range(ATTN_HEADS):
                fill = lambda j, t, h=h, idx=idx: jnp.where(idx == j, rb_ref[j, h] * LOG2_E, t)
                tab_ref[v, h] = lax.fori_loop(0, T5_BUCKETS, fill, jnp.where(idx < 0, NEG_INF, 0.0).astype(F32))

    i = pl.program_id(2)
    lo = lax.broadcasted_iota(I32, (ATTN_QBLK, LANES), 1) < HEAD_DIM
    scale = HEAD_DIM ** -0.5 * LOG2_E
    for rb, sb in [(rb, sb) for rb in range(q_ref.shape[0]) for sb in range(TQ // ATTN_QBLK)]:
        l0 = i * TQ + sb * ATTN_QBLK
        ws = pl.multiple_of(jnp.clip(l0 - HALF_SPAN, 0, L - ATTN_KWIN), HALF_SPAN)
        var = jnp.where(l0 == 0, 0, jnp.where(l0 == L - ATTN_QBLK, 2, 1))
        rows = slice(sb * ATTN_QBLK, (sb + 1) * ATTN_QBLK)
        for hp in range(ATTN_HEADS // 2):
            cols = slice(hp * LANES, (hp + 1) * LANES)
            q2 = q_ref[rb, rows, cols]
            k2 = k_ref[rb, pl.ds(ws, ATTN_KWIN), cols]
            v2 = v_ref[rb, pl.ds(ws, ATTN_KWIN), cols]
            zero = jnp.zeros_like(q2)
            qs = jnp.concatenate([jnp.where(lo, q2, zero), jnp.where(lo, zero, q2)], axis=0)
            s = lax.dot_general(qs, k2, NT_DIMS, preferred_element_type=F32)
            bias = jnp.concatenate([tab_ref[var, 2 * hp], tab_ref[var, 2 * hp + 1]], axis=0)
            logits = s * scale + bias
            m = jnp.max(logits, axis=-1, keepdims=True)
            p = jnp.exp2(logits - m)
            den = jnp.sum(p, axis=-1, keepdims=True)
            pv = jnp.dot(p.astype(BF16), v2, preferred_element_type=F32) / den
            lse = jnp.broadcast_to(m * LN_2 + jnp.log(den), (2 * ATTN_QBLK, LANES))
            o_ref[rb, rows, cols] = jnp.where(lo, pv[:ATTN_QBLK], pv[ATTN_QBLK:]).astype(BF16)
            lse_ref[rb, rows, cols] = jnp.where(lo, lse[:ATTN_QBLK], lse[ATTN_QBLK:])


def _attn_branch(attn, rel_bias, dil):
    B, _, L, _ = attn.shape
    TQ = min(ATTN_STEP_ROWS, L)
    RB = ATTN_STEP_ROWS // TQ
    idx = jnp.asarray(_bucket_tables(dil))
    body = functools.partial(_attn_body, L=L, TQ=TQ)
    kv = lambda j: pl.BlockSpec((None, RB, L, ATTN_W), lambda b, r, i: (b, r, 0, j))
    q_spec = pl.BlockSpec((None, RB, TQ, ATTN_W), lambda b, r, i: (b, r, i, 0))
    return pl.pallas_call(
        body,
        grid=(B, dil // RB, L // TQ),
        in_specs=[pl.BlockSpec(memory_space=pltpu.SMEM),
                  pl.BlockSpec(idx.shape, lambda b, r, i: (0, 0, 0)),
                  q_spec, kv(1), kv(2)],
        out_specs=[q_spec, q_spec],
        out_shape=[jax.ShapeDtypeStruct((B, dil, L, ATTN_W), BF16),
                   jax.ShapeDtypeStruct((B, dil, L, ATTN_W), F32)],
        scratch_shapes=[pltpu.VMEM((3, ATTN_HEADS, ATTN_QBLK, ATTN_KWIN), F32)],
        compiler_params=pltpu.CompilerParams(dimension_semantics=("arbitrary",) * 3,
                                             vmem_limit_bytes=48 * 2**20),
        name=f"attn_d{dil}",
    )(rel_bias, idx, attn, attn, attn)


def _gla_body(q_ref, k_ref, v_ref, gate_ref, bf_ref, bb_ref, ng_ref, out_ref, of_ref, ob_ref, st_ref, *, S):
    C = GLA_CHUNK
    nc = S // C
    r = lax.broadcasted_iota(I32, (C, LANES), 0)
    s2 = lax.broadcasted_iota(I32, (C, LANES), 1) % HEAD_DIM
    lo = lax.broadcasted_iota(I32, (C, LANES), 1) < HEAD_DIM
    blockdiag = ((lax.broadcasted_iota(I32, (2 * GLA_DV, LANES), 0) < GLA_DV)
                 == (lax.broadcasted_iota(I32, (2 * GLA_DV, LANES), 1) < HEAD_DIM))
    st_ref[...] = jnp.zeros_like(st_ref)

    def one_chunk(ci, direction):
        mask = (r >= s2) if direction == 0 else (r <= s2)
        b_ref, o_ref = (bf_ref, of_ref) if direction == 0 else (bb_ref, ob_ref)
        rows = pl.ds(pl.multiple_of(ci * C, C), C)
        q = q_ref[0, rows, :].astype(F32) * (HEAD_DIM ** -0.5)
        k = k_ref[0, rows, :].astype(F32)
        v = v_ref[0, rows, :]
        b = b_ref[0, rows, :]
        g_tot = b[C - 1:C, :] if direction == 0 else b[0:1, :]
        qd = (q * jnp.exp(b)).astype(BF16)
        kd = (k * jnp.exp(-b)).astype(BF16)
        kst = (k * jnp.exp(g_tot - b)).astype(BF16)
        zk = jnp.zeros_like(kd)
        ks = jnp.concatenate([jnp.where(lo, kd, zk), jnp.where(lo, zk, kd)], axis=0)
        a = lax.dot_general(qd, ks, NT_DIMS, preferred_element_type=F32)
        a = jnp.where(mask, a, 0.0).astype(BF16)
        zv = jnp.zeros((C, GLA_DV), BF16)
        vblk = jnp.concatenate([jnp.concatenate([v[:, :GLA_DV], zv], axis=1),
                                jnp.concatenate([zv, v[:, GLA_DV:]], axis=1)], axis=0)
        st = st_ref[direction]
        o_ref[rows, :] = (jnp.dot(a, vblk, preferred_element_type=F32)
                          + lax.dot_general(qd, st.astype(BF16), NT_DIMS, preferred_element_type=F32))
        d_st = jnp.dot(v.astype(F32).T.astype(BF16), kst, preferred_element_type=F32)
        st_ref[direction] = st * jnp.exp(g_tot) + jnp.where(blockdiag, d_st, 0.0)

    def step(i, carry):
        one_chunk(i, 0)
        one_chunk(nc - 1 - i, 1)
        return carry

    lax.fori_loop(0, nc, step, 0, unroll=8)

    tr = 256
    def epilogue(i, carry):
        rows = pl.ds(pl.multiple_of(i * tr, tr), tr)
        for hh in range(2):
            cols = slice(hh * GLA_DV, (hh + 1) * GLA_DV)
            o = of_ref[rows, cols] + ob_ref[rows, cols]
            o = o * lax.rsqrt(jnp.mean(o * o, axis=-1, keepdims=True) + NORM_EPS) * ng_ref[0, :, cols]
            gate = gate_ref[0, rows, cols].astype(F32)
            out_ref[0, rows, cols] = (o * (gate * _sigmoid(gate))).astype(BF16)
        return carry

    lax.fori_loop(0, S // tr, epilogue, 0)


def _gla(gla, bcum, norm_g, B, S):
    g3 = gla.reshape(B, S, GLA_COLS)
    b3 = bcum.reshape(B, S, 2 * GLA_KEY_W)
    ng = norm_g.reshape(2, 1, 2 * GLA_DV)
    body = functools.partial(_gla_body, S=S)
    spec = lambda w, off: pl.BlockSpec((1, S, w), lambda b, p: (b, 0, off + p))
    out = pl.pallas_call(
        body,
        grid=(B, 2),
        in_specs=[spec(LANES, 0),
                  spec(LANES, GLA_KEY_W // LANES),
                  spec(2 * GLA_DV, (2 * GLA_KEY_W) // (2 * GLA_DV)),
                  spec(2 * GLA_DV, (2 * GLA_KEY_W + GLA_VAL_W) // (2 * GLA_DV)),
                  spec(LANES, 0),
                  spec(LANES, GLA_KEY_W // LANES),
                  pl.BlockSpec((1, 1, 2 * GLA_DV), lambda b, p: (p, 0, 0))],
        out_specs=pl.BlockSpec((1, S, 2 * GLA_DV), lambda b, p: (b, 0, p)),
        out_shape=jax.ShapeDtypeStruct((B, S, GLA_VAL_W), BF16),
        scratch_shapes=[pltpu.VMEM((S, 2 * GLA_DV), F32), pltpu.VMEM((S, 2 * GLA_DV), F32),
                        pltpu.VMEM((2, 2 * GLA_DV, LANES), F32)],
        compiler_params=pltpu.CompilerParams(dimension_semantics=("arbitrary", "arbitrary"),
                                             vmem_limit_bytes=56 * 2**20),
        name="gla",
    )(g3, g3, g3, g3, b3, b3, ng)
    return out.reshape(B * S, GLA_VAL_W)


def _natural_rows(blk_ref, u_ref, dil, tm):
    if dil == 1:
        return blk_ref[0, 0].astype(F32)
    for r in range(dil):
        for c in range(ATTN_W // LANES):
            u_ref[c, pl.ds(r, tm // dil, stride=dil), :] = blk_ref[0, r, :, c * LANES:(c + 1) * LANES].astype(F32)
    return jnp.concatenate([u_ref[c] for c in range(ATTN_W // LANES)], axis=1)


def _mix_body(o1, o2, o3, l1, l2, l3, g_ref, x_ref, wo_ref, n2_ref, rwh_ref, rwl_ref, rb_ref,
              x1_ref, h_ref, m_ref, slab_ref, tcnt_ref, u_ref, *, tm):
    ls = [_natural_rows(l, u_ref, d, tm) for l, d in zip((l1, l2, l3), DILATIONS)]
    m = jnp.maximum(jnp.maximum(ls[0], ls[1]), ls[2])
    es = [jnp.exp(l - m) for l in ls]
    a = es[0] * _natural_rows(o1, u_ref, DILATIONS[0], tm)
    a = a + es[1] * _natural_rows(o2, u_ref, DILATIONS[1], tm)
    a = a + es[2] * _natural_rows(o3, u_ref, DILATIONS[2], tm)
    a = a / (es[0] + es[1] + es[2])
    mix = (jnp.dot(a.astype(BF16), wo_ref[:ATTN_W, :], preferred_element_type=F32)
           + jnp.dot(g_ref[...], wo_ref[ATTN_W:, :], preferred_element_type=F32))
    x1 = x_ref[...] + mix
    x1_ref[...] = x1
    h = x1 * lax.rsqrt(jnp.mean(x1 * x1, axis=-1, keepdims=True) + NORM_EPS) * n2_ref[...]
    h_ref[...] = h

    hh, hl, _ = _split3(h)
    dg = lambda u, v: lax.dot_general(u, v, NT_DIMS, preferred_element_type=F32)
    logits = dg(rwh_ref[...], hh) + dg(rwl_ref[...], hh) + dg(rwh_ref[...], hl) + rb_ref[...]
    ie = lax.broadcasted_iota(I32, (N_EXPERTS, tm), 0)
    cur = logits
    vals, idxs = [], []
    for _ in range(TOP_K):
        mx = jnp.max(cur, axis=0, keepdims=True)
        ix = jnp.min(jnp.where(cur == mx, ie, N_EXPERTS), axis=0, keepdims=True)
        vals.append(mx)
        idxs.append(ix)
        cur = jnp.where(ie == ix, -jnp.inf, cur)
    ex = [jnp.exp(v - vals[0]) for v in vals]
    den = ex[0] + ex[1] + ex[2] + ex[3]
    gates = [e / den for e in ex]

    onehots = [ie == ix for ix in idxs]
    chosen = (onehots[0] | onehots[1] | onehots[2] | onehots[3]).astype(F32)
    tr = lax.broadcasted_iota(I32, (tm, tm), 0)
    tc = lax.broadcasted_iota(I32, (tm, tm), 1)
    before = jnp.dot(chosen.astype(BF16), (tr < tc).astype(BF16), preferred_element_type=F32)
    count = jnp.broadcast_to(jnp.sum(chosen, axis=1, keepdims=True), (N_EXPERTS, LANES))
    tcnt_ref[0] = count
    run = jnp.floor((count + (RUN_ALIGN - 1.0)) * (1.0 / RUN_ALIGN)) * RUN_ALIGN
    er = lax.broadcasted_iota(I32, (N_EXPERTS, N_EXPERTS), 0)
    ec = lax.broadcasted_iota(I32, (N_EXPERTS, N_EXPERTS), 1)
    run_start = jnp.dot((ec < er).astype(BF16), run.astype(BF16), preferred_element_type=F32)
    where_ = before + run_start[:, 0:1]
    locs = [jnp.sum(jnp.where(oh, where_, 0.0), axis=0, keepdims=True) for oh in onehots]

    def pack(rows, n):
        sub = lax.broadcasted_iota(I32, (n, tm), 0)
        out = jnp.zeros((n, tm), rows[0].dtype)
        for k, row in enumerate(rows):
            out = jnp.where(sub == k, row, out)
        return out

    m_ref[...] = pack([l.astype(I32) for l in locs], 8)
    slab = pack(gates + locs, 8)
    slab_ref[...] = jnp.concatenate([slab, jnp.zeros((LANES - 8, tm), F32)], axis=0).T


def _mix(outs, lses, g, x2, B, S, w_out, norm2_g, router_w, router_b):
    T, D = x2.shape
    tm = ROUTE_TILE
    nt = S // tm
    wo = w_out.astype(BF16)
    rwt = router_w.T
    rwh = rwt.astype(BF16)
    rwl = (rwt - rwh.astype(F32)).astype(BF16)
    rb = router_b.reshape(N_EXPERTS, 1)
    n2 = norm2_g.reshape(1, D)
    row = lambda c: pl.BlockSpec((tm, c), lambda b, i: (b * nt + i, 0))
    full = lambda a: pl.BlockSpec(a.shape, lambda b, i: (0,) * a.ndim)
    col = pl.BlockSpec((8, tm), lambda b, i: (0, b * nt + i))
    res = [pl.BlockSpec((1, d, tm // d, ATTN_W), lambda b, i: (b, 0, i, 0)) for d in DILATIONS]
    return pl.pallas_call(
        functools.partial(_mix_body, tm=tm),
        grid=(B, nt),
        in_specs=res + res + [row(GLA_VAL_W), row(D), full(wo), full(n2), full(rwh), full(rwl), full(rb)],
        out_specs=[row(D), row(D), col, row(LANES),
                   pl.BlockSpec((1, N_EXPERTS, LANES), lambda b, i: (b * nt + i, 0, 0))],
        out_shape=[jax.ShapeDtypeStruct((T, D), F32),
                   jax.ShapeDtypeStruct((T, D), F32),
                   jax.ShapeDtypeStruct((8, T), I32),
                   jax.ShapeDtypeStruct((T, LANES), F32),
                   jax.ShapeDtypeStruct((T // tm, N_EXPERTS, LANES), F32)],
        scratch_shapes=[pltpu.VMEM((ATTN_W // LANES, tm, LANES), F32)],
        compiler_params=pltpu.CompilerParams(dimension_semantics=("arbitrary", "arbitrary"),
                                             vmem_limit_bytes=48 * 2**20),
        name="mix_router",
    )(*outs, *lses, g, x2, wo, n2, rwh, rwl, rb)


def _run_groups(c):
    return (c + (RUN_ALIGN - 1)) >> RUN_SHIFT


def _plan_body(tc_ref, rs_ref, p0_ref, pn_ref, be_ref, hv_ref, nu_ref, *, nt, nblk):
    shift = MOE_BLOCK.bit_length() - 1
    start = jnp.int32(0)
    for ex in range(N_EXPERTS):
        def run(i, tot, start=start, ex=ex):
            rs_ref[i, ex] = start + tot
            return tot + (_run_groups(tc_ref[i, ex]) << RUN_SHIFT)

        tot = lax.fori_loop(0, nt, run, jnp.int32(0))
        nb = (tot + (MOE_BLOCK - 1)) >> shift
        p0_ref[ex] = start + tot
        pn_ref[ex] = ((nb << shift) - tot) >> RUN_SHIFT
        b0 = start >> shift

        def fill(j, carry, b0=b0, ex=ex):
            be_ref[b0 + j] = jnp.int32(ex)
            hv_ref[b0 + j] = jnp.int32(MOE_BLOCK // MOE_PART)
            return carry

        lax.fori_loop(0, nb, fill, 0)
        last_rows = tot - ((nb - 1) << shift)

        @pl.when(nb > 0)
        def _(b0=b0, nb=nb, last_rows=last_rows):
            hv_ref[b0 + nb - 1] = (last_rows + (MOE_PART - 1)) >> (MOE_PART.bit_length() - 1)

        start = start + (nb << shift)
    used = start >> shift

    def tail(j, carry):
        be_ref[j] = jnp.int32(N_EXPERTS - 1)
        hv_ref[j] = jnp.int32(0)
        return carry

    lax.fori_loop(used, nblk, tail, 0)
    nu_ref[0] = used


def _plan(tcnt, nblk):
    nt = tcnt.shape[0]
    smem = pl.BlockSpec(memory_space=pltpu.SMEM)
    return pl.pallas_call(
        functools.partial(_plan_body, nt=nt, nblk=nblk),
        in_specs=[smem],
        out_specs=[smem] * 6,
        out_shape=[jax.ShapeDtypeStruct(tcnt.shape, I32),
                   jax.ShapeDtypeStruct((N_EXPERTS,), I32),
                   jax.ShapeDtypeStruct((N_EXPERTS,), I32),
                   jax.ShapeDtypeStruct((nblk,), I32),
                   jax.ShapeDtypeStruct((nblk,), I32),
                   jax.ShapeDtypeStruct((1,), I32)],
        name="plan",
    )(tcnt)


def _for_each_run(tc_ref, rs_ref, i, fn):
    start = jnp.int32(0)
    for ex in range(N_EXPERTS):
        ng = _run_groups(tc_ref[i, ex])
        g0 = rs_ref[i, ex]

        def body(q, carry, start=start, g0=g0):
            fn(pl.multiple_of(start + q * RUN_ALIGN, RUN_ALIGN), pl.multiple_of(g0 + q * RUN_ALIGN, RUN_ALIGN))
            return carry

        lax.fori_loop(0, ng, body, 0)
        start = start + (ng << RUN_SHIFT)


def _group_count(tc_ref, i):
    n = jnp.int32(0)
    for ex in range(N_EXPERTS):
        n = n + _run_groups(tc_ref[i, ex])
    return n


def _wait_groups(n, src_ref, dst_ref, sem):
    def waits(count, rows):
        def body(q, carry):
            pltpu.make_async_copy(src_ref.at[pl.ds(0, rows)], dst_ref.at[pl.ds(0, rows)], sem).wait()
            return carry
        lax.fori_loop(0, count, body, 0)

    waits(n >> 6, 64 * RUN_ALIGN)
    waits((n >> 3) & 7, 8 * RUN_ALIGN)
    waits(n & 7, RUN_ALIGN)


def _dispatch_body(tc_ref, rs_ref, p0_ref, pn_ref, nu_ref, m_ref, h_ref, xin_ref, g_ref, z_ref, sem, zsem, *, tm, R):
    i = pl.program_id(0)
    slot = i % 2
    rows = m_ref[...].astype(jnp.int16)
    jj = lax.broadcasted_iota(I32, (R, tm), 0).astype(jnp.int16)
    hit = (jj == rows[0:1]) | (jj == rows[1:2]) | (jj == rows[2:3]) | (jj == rows[3:4])
    onehot = jnp.where(hit, jnp.ones((), BF16), jnp.zeros((), BF16))
    g_ref[slot] = jnp.dot(onehot, h_ref[...].astype(BF16), preferred_element_type=F32).astype(BF16)

    group = lambda ref, r: ref.at[pl.ds(r, RUN_ALIGN)]

    @pl.when(i > 0)
    def _():
        _wait_groups(_group_count(tc_ref, i - 1), g_ref.at[1 - slot], xin_ref, sem.at[1 - slot])

    _for_each_run(tc_ref, rs_ref, i,
                  lambda lr, gr: pltpu.make_async_copy(group(g_ref.at[slot], lr), group(xin_ref, gr), sem.at[slot]).start())

    @pl.when(i == pl.num_programs(0) - 1)
    def _():
        _wait_groups(_group_count(tc_ref, i), g_ref.at[slot], xin_ref, sem.at[slot])

    @pl.when(i == 0)
    def _():
        z_ref[...] = jnp.zeros_like(z_ref)
        total = jnp.int32(0)
        for ex in range(N_EXPERTS):
            def body(q, carry, ex=ex):
                pltpu.make_async_copy(group(z_ref, 0), group(xin_ref, pl.multiple_of(p0_ref[ex] + q * RUN_ALIGN, RUN_ALIGN)), zsem).start()
                return carry
            lax.fori_loop(0, pn_ref[ex], body, 0)
            total = total + pn_ref[ex]
        _wait_groups(total, z_ref, xin_ref, zsem)

        block = lambda j: xin_ref.at[pl.ds(pl.multiple_of(j * MOE_BLOCK, MOE_BLOCK), MOE_BLOCK)]
        n_blocks = xin_ref.shape[0] // MOE_BLOCK

        def fill(j, carry):
            pltpu.make_async_copy(z_ref, block(j), zsem).start()
            return carry

        def fill_wait(j, carry):
            pltpu.make_async_copy(z_ref, block(j), zsem).wait()
            return carry

        lax.fori_loop(nu_ref[0], n_blocks, fill, 0)
        lax.fori_loop(nu_ref[0], n_blocks, fill_wait, 0)


def _dispatch(tcnt, rowstart, pad0, padn, n_used, m_t, h, n_rows):
    T, D = h.shape
    tm = ROUTE_TILE
    col = pl.BlockSpec((8, tm), lambda i, *_: (0, i))
    return pl.pallas_call(
        functools.partial(_dispatch_body, tm=tm, R=GROUP_ROWS),
        grid_spec=pltpu.PrefetchScalarGridSpec(
            num_scalar_prefetch=5,
            grid=(T // tm,),
            in_specs=[col, pl.BlockSpec((tm, D), lambda i, *_: (i, 0))],
            out_specs=pl.BlockSpec(memory_space=pl.ANY),
            scratch_shapes=[pltpu.VMEM((2, GROUP_ROWS, D), BF16), pltpu.VMEM((MOE_BLOCK, D), BF16),
                            pltpu.SemaphoreType.DMA((2,)), pltpu.SemaphoreType.DMA(())]),
        out_shape=jax.ShapeDtypeStruct((n_rows, D), BF16),
        compiler_params=pltpu.CompilerParams(dimension_semantics=("arbitrary",), vmem_limit_bytes=56 * 2**20),
        name="dispatch",
    )(tcnt, rowstart, pad0, padn, n_used, m_t, h)


def _moe_body(be_ref, nu_ref, hv_ref, x_ref, w1g_ref, w1l_ref, b1g_ref, b1l_ref, w2_ref, b2_ref, y_ref):
    del be_ref
    j = pl.program_id(0)
    live = j < nu_ref[0]

    def expert_mlp(rows):
        x = x_ref[rows, :]
        hg = lax.dot_general(x, w1g_ref[0], NT_DIMS, preferred_element_type=F32) + b1g_ref[0]
        hl = lax.dot_general(x, w1l_ref[0], NT_DIMS, preferred_element_type=F32) + b1l_ref[0]
        xg = jnp.minimum(hg, SWIGLU_LIMIT)
        xl = jnp.clip(hl, -SWIGLU_LIMIT, SWIGLU_LIMIT)
        act = xg * _sigmoid(SWIGLU_ALPHA * xg) * (xl + 1.0)
        y = jnp.dot(act.astype(BF16), w2_ref[0].astype(BF16), preferred_element_type=F32) + b2_ref[0]
        y_ref[rows, :] = y.astype(BF16)

    for parts in range(1, MOE_BLOCK // MOE_PART + 1):
        @pl.when(live & (hv_ref[j] == parts))
        def _(parts=parts):
            used = parts * MOE_PART
            expert_mlp(slice(0, used))
            if used < MOE_BLOCK:
                y_ref[used:, :] = jnp.zeros((MOE_BLOCK - used, y_ref.shape[1]), y_ref.dtype)

    @pl.when(jnp.logical_not(live))
    def _():
        y_ref[...] = jnp.zeros_like(y_ref)


def _moe(block_e, n_used, half_block, xin, w1g, w1l, b1, w2, b2):
    n_rows = xin.shape[0]
    D = w2.shape[2]
    F = w2.shape[1]
    nblk = n_rows // MOE_BLOCK
    b1g = b1[:, 0::2].reshape(N_EXPERTS, 1, F)
    b1l = b1[:, 1::2].reshape(N_EXPERTS, 1, F)
    b2r = b2.reshape(N_EXPERTS, 1, D)
    xspec = pl.BlockSpec((MOE_BLOCK, D), lambda j, be, nu, hv: (jnp.minimum(j, nu[0] - 1), 0))
    yspec = pl.BlockSpec((MOE_BLOCK, D), lambda j, be, nu, hv: (j, 0))
    wspec = lambda a: pl.BlockSpec((1,) + a.shape[1:], lambda j, be, nu, hv: (be[j], 0, 0))
    return pl.pallas_call(
        _moe_body,
        grid_spec=pltpu.PrefetchScalarGridSpec(
            num_scalar_prefetch=3,
            grid=(nblk,),
            in_specs=[xspec, wspec(w1g), wspec(w1l), wspec(b1g), wspec(b1l), wspec(w2), wspec(b2r)],
            out_specs=yspec),
        out_shape=jax.ShapeDtypeStruct(xin.shape, BF16),
        compiler_params=pltpu.CompilerParams(dimension_semantics=("arbitrary",),
                                             vmem_limit_bytes=48 * 2**20),
        name="moe",
    )(block_e, n_used, half_block, xin, w1g, w1l, b1g, b1l, w2, b2r)


def _combine_body(tc_ref, rs_ref, slab_ref, y_ref, x1_ref, fg_ref, out_ref, g_ref, sem, *, tm, R):
    i = pl.program_id(0)
    slot = i % 2
    group = lambda ref, r: ref.at[pl.ds(r, RUN_ALIGN)]

    def fetch(tile, slot):
        _for_each_run(tc_ref, rs_ref, tile,
                      lambda lr, gr: pltpu.make_async_copy(group(y_ref, gr), group(g_ref.at[slot], lr), sem.at[slot]).start())

    @pl.when(i == 0)
    def _():
        g_ref[...] = jnp.zeros_like(g_ref)
        fetch(i, slot)

    @pl.when(i + 1 < pl.num_programs(0))
    def _():
        fetch(i + 1, 1 - slot)

    slab = slab_ref[...]
    gates = slab[:, 0:TOP_K].astype(BF16)
    rows = slab[:, TOP_K:2 * TOP_K].astype(I32).astype(jnp.int16)
    jj = lax.broadcasted_iota(I32, (tm, R), 1).astype(jnp.int16)
    sel = jnp.zeros((tm, R), BF16)
    for k in range(TOP_K):
        sel = jnp.where(jj == rows[:, k:k + 1], gates[:, k:k + 1], sel)

    _wait_groups(_group_count(tc_ref, i), y_ref, g_ref.at[slot], sem.at[slot])
    acc = x1_ref[...] + jnp.dot(sel, g_ref[slot], preferred_element_type=F32)
    out_ref[...] = acc * lax.rsqrt(jnp.mean(acc * acc, axis=-1, keepdims=True) + NORM_EPS) * fg_ref[...]


def _combine(tcnt, rowstart, slab, y, x1, final_g):
    T, D = x1.shape
    tm = ROUTE_TILE
    fg = final_g.reshape(1, D)
    return pl.pallas_call(
        functools.partial(_combine_body, tm=tm, R=GROUP_ROWS),
        grid_spec=pltpu.PrefetchScalarGridSpec(
            num_scalar_prefetch=2,
            grid=(T // tm,),
            in_specs=[pl.BlockSpec((tm, LANES), lambda i, *_: (i, 0)),
                      pl.BlockSpec(memory_space=pl.ANY),
                      pl.BlockSpec((tm, D), lambda i, *_: (i, 0)),
                      pl.BlockSpec((1, D), lambda i, *_: (0, 0))],
            out_specs=pl.BlockSpec((tm, D), lambda i, *_: (i, 0)),
            scratch_shapes=[pltpu.VMEM((2, GROUP_ROWS, D), BF16), pltpu.SemaphoreType.DMA((2,))]),
        out_shape=jax.ShapeDtypeStruct((T, D), F32),
        compiler_params=pltpu.CompilerParams(dimension_semantics=("arbitrary",), vmem_limit_bytes=56 * 2**20),
        name="combine",
    )(tcnt, rowstart, slab, y, x1, fg)


def _mixers(x2, B, S, norm1_g, w_in, rel_bias, wg_f, bg_f, wg_b, bg_b, gla_norm_g, w1):
    a1, a4, a16, gla, bcum, w1g, w1l = _in_proj(x2, B, S, norm1_g, w_in, wg_f, bg_f, wg_b, bg_b, w1)
    branches = [_attn_branch(a, rel_bias, d) for a, d in zip((a1, a4, a16), DILATIONS)]
    g = _gla(gla, bcum, gla_norm_g, B, S)
    return [o for o, _ in branches], [l for _, l in branches], g, w1g, w1l


def _moe_layer(h, m_t, slab, tcnt_f, x1, w1g, w1l, b1, w2, b2, final_g):
    T = x1.shape[0]
    nt = T // ROUTE_TILE
    nblk = (T * TOP_K + nt * N_EXPERTS * (RUN_ALIGN - 1)) // MOE_BLOCK + N_EXPERTS
    tcnt = tcnt_f[:, :, 0].astype(I32)
    rowstart, pad0, padn, block_e, half_block, n_used = _plan(tcnt, nblk)
    xin = _dispatch(tcnt, rowstart, pad0, padn, n_used, m_t, h, nblk * MOE_BLOCK)
    y = _moe(block_e, n_used, half_block, xin, w1g, w1l, b1, w2, b2)
    return _combine(tcnt, rowstart, slab, y, x1, final_g)


def kernel(x, norm1_g, w_in, rel_bias, gla_wg_fwd, gla_bg_fwd, gla_wg_bwd, gla_bg_bwd, gla_norm_g, w_out, norm2_g, router_w, router_b, moe_w1, moe_b1, moe_w2, moe_b2, final_g):
    B, S, D = x.shape
    assert w_in.shape[0] == 1, "one layer"
    x2 = x.reshape(B * S, D)
    outs, lses, g, w1g, w1l = _mixers(x2, B, S, norm1_g[0], w_in[0], rel_bias, gla_wg_fwd[0], gla_bg_fwd[0],
                                      gla_wg_bwd[0], gla_bg_bwd[0], gla_norm_g[0], moe_w1[0])
    x1, h, m_t, slab, tcnt = _mix(outs, lses, g, x2, B, S, w_out[0], norm2_g[0], router_w[0], router_b[0])
    out = _moe_layer(h, m_t, slab, tcnt, x1, w1g, w1l, moe_b1[0], moe_w2[0], moe_b2[0], final_g)
    return out.reshape(B, S, D)
```

```python
import functools

import jax
import jax.numpy as jnp
import numpy as np
from jax import lax
from jax.experimental import pallas as pl
from jax.experimental.pallas import tpu as pltpu

F32, BF16, I32 = jnp.float32, jnp.bfloat16, jnp.int32

LANES = 128
HEAD_DIM = 64
ATTN_HEADS = 8
ATTN_W = ATTN_HEADS * HEAD_DIM
HALF_SPAN = 64
ATTN_QBLK = 128
ATTN_KWIN = 256
ATTN_STEP_ROWS = 512
DILATIONS = (1, 4, 16)
T5_BUCKETS = 32
T5_MAX_DISTANCE = 1024
NEG_INF = -1e30
LOG2_E = 1.4426950408889634
LN_2 = 0.6931471805599453
GLA_KEY_W = 256
GLA_VAL_W = 512
GLA_DV = 128
GLA_RANK = 16
GLA_TAU = 16.0
GLA_CHUNK = 64
N_EXPERTS = 32
TOP_K = 4
SWIGLU_LIMIT = 7.0
SWIGLU_ALPHA = 1.702
NORM_EPS = 1e-5
MOE_BLOCK = 1024
MOE_PART = 256
W1_SUBTILE = 512
ROUTE_TILE = 512
RUN_ALIGN = 16
RUN_SHIFT = 4
GROUP_ROWS = ROUTE_TILE * TOP_K + N_EXPERTS * RUN_ALIGN

ATTN_COLS = 3 * ATTN_W
GLA_COLS = 2 * GLA_KEY_W + 2 * GLA_VAL_W
Z_COLS = LANES
PROJ_COLS = ATTN_COLS + GLA_COLS + Z_COLS

NT_DIMS = (((1,), (1,)), ((), ()))


def _sigmoid(x):
    return 1.0 / (1.0 + jnp.exp(-x))


def _split3(a):
    h1 = a.astype(BF16)
    r1 = a - h1.astype(F32)
    h2 = r1.astype(BF16)
    h3 = (r1 - h2.astype(F32)).astype(BF16)
    return h1, h2, h3


def _dot_f32(a, b):
    ah, al, _ = _split3(a)
    bh, bl, _ = _split3(b)
    d = lambda u, v: jnp.dot(u, v, preferred_element_type=F32)
    return d(ah, bh) + d(al, bh) + d(ah, bl)


def _split_w1_tile(w_ref, g_ref, l_ref, t_ref):
    sub = t_ref.shape[1]
    for s in range(w_ref.shape[2] // sub):
        wt = w_ref[0, :, s * sub:(s + 1) * sub].T
        out_rows = slice(s * sub // 2, (s + 1) * sub // 2)
        for c in range(wt.shape[1] // LANES):
            cols = slice(c * LANES, (c + 1) * LANES)
            t_ref[c] = wt[:, cols]
            g_ref[0, out_rows, cols] = t_ref[c, pl.ds(0, sub // 2, stride=2), :].astype(BF16)
            l_ref[0, out_rows, cols] = t_ref[c, pl.ds(1, sub // 2, stride=2), :].astype(BF16)


def _in_proj_body(x_ref, g_ref, w_ref, wg_ref, bg_ref, w1_ref, a1_ref, a4_ref, a16_ref, gla_ref, b_ref,
                  w1g_ref, w1l_ref, t_ref, *, tm):
    _split_w1_tile(w1_ref, w1g_ref, w1l_ref, t_ref)

    x = x_ref[...]
    n = x * lax.rsqrt(jnp.mean(x * x, axis=-1, keepdims=True) + NORM_EPS) * g_ref[...]
    p = jnp.dot(n.astype(BF16), w_ref[...], preferred_element_type=F32)
    gla_ref[...] = p[:, ATTN_COLS:ATTN_COLS + GLA_COLS].astype(BF16)

    qkv = p[:, :ATTN_COLS].astype(BF16)
    a1_ref[0, 0] = qkv
    dst = lax.broadcasted_iota(I32, (tm, tm), 0)
    src = lax.broadcasted_iota(I32, (tm, tm), 1)
    for dil, a_ref in ((DILATIONS[1], a4_ref), (DILATIONS[2], a16_ref)):
        per = tm // dil
        perm = (src == (dst % per) * dil + dst // per).astype(F32).astype(BF16)
        moved = jnp.dot(perm, qkv, preferred_element_type=F32).astype(BF16)
        for r in range(dil):
            a_ref[0, r] = moved[r * per:(r + 1) * per]

    z = p[:, ATTN_COLS + GLA_COLS:]
    zz = _dot_f32(z, wg_ref[...]) + bg_ref[...]
    la = (jnp.minimum(zz, 0.0) - jnp.log1p(jnp.exp(-jnp.abs(zz)))) * (1.0 / GLA_TAU)
    C = GLA_CHUNK
    r_i = lax.broadcasted_iota(I32, (C, C), 0)
    c_i = lax.broadcasted_iota(I32, (C, C), 1)
    lower = (r_i >= c_i).astype(BF16)
    upper = (r_i <= c_i).astype(BF16)
    W = GLA_KEY_W
    for j in range(tm // C):
        rows = slice(j * C, (j + 1) * C)
        for tri, cols in ((lower, slice(0, W)), (upper, slice(W, 2 * W))):
            terms = jnp.concatenate(_split3(la[rows, cols]), axis=1)
            cs = jnp.dot(tri, terms, preferred_element_type=F32)
            b_ref[rows, cols] = cs[:, :W] + cs[:, W:2 * W] + cs[:, 2 * W:]


def _in_proj(x2, B, S, norm_g, w_in, wg_f, bg_f, wg_b, bg_b, w1):
    T, D = x2.shape
    tm = 256
    nt = S // tm
    E, _, F2 = w1.shape
    w1_cols = E * F2 // (B * nt)
    per_e = F2 // w1_cols
    assert w1_cols * B * nt == E * F2 and per_e * w1_cols == F2 and w1_cols % W1_SUBTILE == 0
    w1_in = pl.BlockSpec((1, D, w1_cols), lambda b, i: ((b * nt + i) // per_e, 0, (b * nt + i) % per_e))
    w1_out = pl.BlockSpec((1, w1_cols // 2, D), lambda b, i: ((b * nt + i) // per_e, (b * nt + i) % per_e, 0))
    w = jnp.pad(w_in.astype(BF16), ((0, 0), (0, PROJ_COLS - w_in.shape[1])))
    wg = jnp.zeros((Z_COLS, 2 * GLA_KEY_W), F32)
    wg = wg.at[:GLA_RANK, :GLA_KEY_W].set(wg_f).at[GLA_RANK:2 * GLA_RANK, GLA_KEY_W:].set(wg_b)
    bg = jnp.concatenate([bg_f, bg_b]).reshape(1, 2 * GLA_KEY_W)
    row = lambda c: pl.BlockSpec((tm, c), lambda b, i: (b * nt + i, 0))
    full = lambda a: pl.BlockSpec(a.shape, lambda b, i: (0,) * a.ndim)
    res = lambda d: pl.BlockSpec((1, d, tm // d, ATTN_COLS), lambda b, i: (b, 0, i, 0))
    res_shape = lambda d: jax.ShapeDtypeStruct((B, d, S // d, ATTN_COLS), BF16)
    g2 = norm_g.reshape(1, D)
    return pl.pallas_call(
        functools.partial(_in_proj_body, tm=tm),
        grid=(B, nt),
        in_specs=[row(D), full(g2), full(w), full(wg), full(bg), w1_in],
        out_specs=[res(d) for d in DILATIONS] + [row(GLA_COLS), row(2 * GLA_KEY_W), w1_out, w1_out],
        out_shape=[res_shape(d) for d in DILATIONS]
                  + [jax.ShapeDtypeStruct((T, GLA_COLS), BF16), jax.ShapeDtypeStruct((T, 2 * GLA_KEY_W), F32)]
                  + [jax.ShapeDtypeStruct((E, F2 // 2, D), BF16)] * 2,
        scratch_shapes=[pltpu.VMEM((D // LANES, W1_SUBTILE, LANES), F32)],
        compiler_params=pltpu.CompilerParams(dimension_semantics=("arbitrary", "arbitrary"),
                                             vmem_limit_bytes=56 * 2**20),
        name="in_proj",
    )(x2, g2, w, wg, bg, w1)


def _t5_bucket(rel):
    nb = T5_BUCKETS // 2
    max_exact = nb // 2
    n = np.abs(rel)
    large = max_exact + (np.log(np.maximum(n, 1).astype(np.float32) / max_exact)
                         / np.log(T5_MAX_DISTANCE / max_exact) * (nb - max_exact)).astype(np.int32)
    large = np.minimum(large, nb - 1)
    return (np.where(rel > 0, nb, 0) + np.where(n < max_exact, n, large)).astype(np.int32)


def _bucket_tables(dil):
    s = np.arange(ATTN_QBLK)[:, None]
    t = np.arange(ATTN_KWIN)[None, :]
    tabs = []
    for c in (0, HALF_SPAN, 2 * HALF_SPAN):
        off = t - c - s
        tabs.append(np.where(np.abs(off) <= HALF_SPAN, _t5_bucket(off * dil), -1))
    return np.stack(tabs).astype(np.int32)


def _attn_body(rb_ref, idx_ref, q_ref, k_ref, v_ref, o_ref, lse_ref, tab_ref, *, L, TQ):
    first = (pl.program_id(0) == 0) & (pl.program_id(1) == 0) & (pl.program_id(2) == 0)

    @pl.when(first)
    def _():
        for v in range(3):
            idx = idx_ref[v]
            for h in range(ATTN_HEADS):
                fill = lambda j, t, h=h, idx=idx: jnp.where(idx == j, rb_ref[j, h] * LOG2_E, t)
                tab_ref[v, h] = lax.fori_loop(0, T5_BUCKETS, fill, jnp.where(idx < 0, NEG_INF, 0.0).astype(F32))

    i = pl.program_id(2)
    lo = lax.broadcasted_iota(I32, (ATTN_QBLK, LANES), 1) < HEAD_DIM
    scale = HEAD_DIM ** -0.5 * LOG2_E
    for rb, sb in [(rb, sb) for rb in range(q_ref.shape[0]) for sb in range(TQ // ATTN_QBLK)]:
        l0 = i * TQ + sb * ATTN_QBLK
        ws = pl.multiple_of(jnp.clip(l0 - HALF_SPAN, 0, L - ATTN_KWIN), HALF_SPAN)
        var = jnp.where(l0 == 0, 0, jnp.where(l0 == L - ATTN_QBLK, 2, 1))
        rows = slice(sb * ATTN_QBLK, (sb + 1) * ATTN_QBLK)
        for hp in range(ATTN_HEADS // 2):
            cols = slice(hp * LANES, (hp + 1) * LANES)
            q2 = q_ref[rb, rows, cols]
            k2 = k_ref[rb, pl.ds(ws, ATTN_KWIN), cols]
            v2 = v_ref[rb, pl.ds(ws, ATTN_KWIN), cols]
            zero = jnp.zeros_like(q2)
            qs = jnp.concatenate([jnp.where(lo, q2, zero), jnp.where(lo, zero, q2)], axis=0)
            s = lax.dot_general(qs, k2, NT_DIMS, preferred_element_type=F32)
            bias = jnp.concatenate([tab_ref[var, 2 * hp], tab_ref[var, 2 * hp + 1]], axis=0)
            logits = s * scale + bias
            m = jnp.max(logits, axis=-1, keepdims=True)
            p = jnp.exp2(logits - m)
            den = jnp.sum(p, axis=-1, keepdims=True)
            pv = jnp.dot(p.astype(BF16), v2, preferred_element_type=F32) / den
            lse = jnp.broadcast_to(m * LN_2 + jnp.log(den), (2 * ATTN_QBLK, LANES))
            o_ref[rb, rows, cols] = jnp.where(lo, pv[:ATTN_QBLK], pv[ATTN_QBLK:]).astype(BF16)
            lse_ref[rb, rows, cols] = jnp.where(lo, lse[:ATTN_QBLK], lse[ATTN_QBLK:])


def _attn_branch(attn, rel_bias, dil):
    B, _, L, _ = attn.shape
    TQ = min(ATTN_STEP_ROWS, L)
    RB = ATTN_STEP_ROWS // TQ
    idx = jnp.asarray(_bucket_tables(dil))
    body = functools.partial(_attn_body, L=L, TQ=TQ)
    kv = lambda j: pl.BlockSpec((None, RB, L, ATTN_W), lambda b, r, i: (b, r, 0, j))
    q_spec = pl.BlockSpec((None, RB, TQ, ATTN_W), lambda b, r, i: (b, r, i, 0))
    return pl.pallas_call(
        body,
        grid=(B, dil // RB, L // TQ),
        in_specs=[pl.BlockSpec(memory_space=pltpu.SMEM),
                  pl.BlockSpec(idx.shape, lambda b, r, i: (0, 0, 0)),
                  q_spec, kv(1), kv(2)],
        out_specs=[q_spec, q_spec],
        out_shape=[jax.ShapeDtypeStruct((B, dil, L, ATTN_W), BF16),
                   jax.ShapeDtypeStruct((B, dil, L, ATTN_W), F32)],
        scratch_shapes=[pltpu.VMEM((3, ATTN_HEADS, ATTN_QBLK, ATTN_KWIN), F32)],
        compiler_params=pltpu.CompilerParams(dimension_semantics=("arbitrary",) * 3,
                                             vmem_limit_bytes=48 * 2**20),
        name=f"attn_d{dil}",
    )(rel_bias, idx, attn, attn, attn)


def _gla_body(q_ref, k_ref, v_ref, gate_ref, bf_ref, bb_ref, ng_ref, out_ref, of_ref, ob_ref, st_ref, *, S):
    C = GLA_CHUNK
    nc = S // C
    r = lax.broadcasted_iota(I32, (C, LANES), 0)
    s2 = lax.broadcasted_iota(I32, (C, LANES), 1) % HEAD_DIM
    lo = lax.broadcasted_iota(I32, (C, LANES), 1) < HEAD_DIM
    blockdiag = ((lax.broadcasted_iota(I32, (2 * GLA_DV, LANES), 0) < GLA_DV)
                 == (lax.broadcasted_iota(I32, (2 * GLA_DV, LANES), 1) < HEAD_DIM))
    st_ref[...] = jnp.zeros_like(st_ref)

    def one_chunk(ci, direction):
        mask = (r >= s2) if direction == 0 else (r <= s2)
        b_ref, o_ref = (bf_ref, of_ref) if direction == 0 else (bb_ref, ob_ref)
        rows = pl.ds(pl.multiple_of(ci * C, C), C)
        q = q_ref[0, rows, :].astype(F32) * (HEAD_DIM ** -0.5)
        k = k_ref[0, rows, :].astype(F32)
        v = v_ref[0, rows, :]
        b = b_ref[0, rows, :]
        g_tot = b[C - 1:C, :] if direction == 0 else b[0:1, :]
        qd = (q * jnp.exp(b)).astype(BF16)
        kd = (k * jnp.exp(-b)).astype(BF16)
        kst = (k * jnp.exp(g_tot - b)).astype(BF16)
        zk = jnp.zeros_like(kd)
        ks = jnp.concatenate([jnp.where(lo, kd, zk), jnp.where(lo, zk, kd)], axis=0)
        a = lax.dot_general(qd, ks, NT_DIMS, preferred_element_type=F32)
        a = jnp.where(mask, a, 0.0).astype(BF16)
        zv = jnp.zeros((C, GLA_DV), BF16)
        vblk = jnp.concatenate([jnp.concatenate([v[:, :GLA_DV], zv], axis=1),
                                jnp.concatenate([zv, v[:, GLA_DV:]], axis=1)], axis=0)
        st = st_ref[direction]
        o_ref[rows, :] = (jnp.dot(a, vblk, preferred_element_type=F32)
                          + lax.dot_general(qd, st.astype(BF16), NT_DIMS, preferred_element_type=F32))
        d_st = jnp.dot(v.astype(F32).T.astype(BF16), kst, preferred_element_type=F32)
        st_ref[direction] = st * jnp.exp(g_tot) + jnp.where(blockdiag, d_st, 0.0)

    def step(i, carry):
        one_chunk(i, 0)
        one_chunk(nc - 1 - i, 1)
        return carry

    lax.fori_loop(0, nc, step, 0, unroll=8)

    tr = 256
    def epilogue(i, carry):
        rows = pl.ds(pl.multiple_of(i * tr, tr), tr)
        for hh in range(2):
            cols = slice(hh * GLA_DV, (hh + 1) * GLA_DV)
            o = of_ref[rows, cols] + ob_ref[rows, cols]
            o = o * lax.rsqrt(jnp.mean(o * o, axis=-1, keepdims=True) + NORM_EPS) * ng_ref[0, :, cols]
            gate = gate_ref[0, rows, cols].astype(F32)
            out_ref[0, rows, cols] = (o * (gate * _sigmoid(gate))).astype(BF16)
        return carry

    lax.fori_loop(0, S // tr, epilogue, 0)


def _gla(gla, bcum, norm_g, B, S):
    g3 = gla.reshape(B, S, GLA_COLS)
    b3 = bcum.reshape(B, S, 2 * GLA_KEY_W)
    ng = norm_g.reshape(2, 1, 2 * GLA_DV)
    body = functools.partial(_gla_body, S=S)
    spec = lambda w, off: pl.BlockSpec((1, S, w), lambda b, p: (b, 0, off + p))
    out = pl.pallas_call(
        body,
        grid=(B, 2),
        in_specs=[spec(LANES, 0),
                  spec(LANES, GLA_KEY_W // LANES),
                  spec(2 * GLA_DV, (2 * GLA_KEY_W) // (2 * GLA_DV)),
                  spec(2 * GLA_DV, (2 * GLA_KEY_W + GLA_VAL_W) // (2 * GLA_DV)),
                  spec(LANES, 0),
                  spec(LANES, GLA_KEY_W // LANES),
                  pl.BlockSpec((1, 1, 2 * GLA_DV), lambda b, p: (p, 0, 0))],
        out_specs=pl.BlockSpec((1, S, 2 * GLA_DV), lambda b, p: (b, 0, p)),
        out_shape=jax.ShapeDtypeStruct((B, S, GLA_VAL_W), BF16),
        scratch_shapes=[pltpu.VMEM((S, 2 * GLA_DV), F32), pltpu.VMEM((S, 2 * GLA_DV), F32),
                        pltpu.VMEM((2, 2 * GLA_DV, LANES), F32)],
        compiler_params=pltpu.CompilerParams(dimension_semantics=("arbitrary", "arbitrary"),
                                             vmem_limit_bytes=56 * 2**20),
        name="gla",
    )(g3, g3, g3, g3, b3, b3, ng)
    return out.reshape(B * S, GLA_VAL_W)


def _natural_rows(blk_ref, u_ref, dil, tm):
    if dil == 1:
        return blk_ref[0, 0].astype(F32)
    for r in range(dil):
        for c in range(ATTN_W // LANES):
            u_ref[c, pl.ds(r, tm // dil, stride=dil), :] = blk_ref[0, r, :, c * LANES:(c + 1) * LANES].astype(F32)
    return jnp.concatenate([u_ref[c] for c in range(ATTN_W // LANES)], axis=1)


def _mix_body(o1, o2, o3, l1, l2, l3, g_ref, x_ref, wo_ref, n2_ref, rwh_ref, rwl_ref, rb_ref,
              x1_ref, h_ref, m_ref, slab_ref, tcnt_ref, u_ref, *, tm):
    ls = [_natural_rows(l, u_ref, d, tm) for l, d in zip((l1, l2, l3), DILATIONS)]
    m = jnp.maximum(jnp.maximum(ls[0], ls[1]), ls[2])
    es = [jnp.exp(l - m) for l in ls]
    a = es[0] * _natural_rows(o1, u_ref, DILATIONS[0], tm)
    a = a + es[1] * _natural_rows(o2, u_ref, DILATIONS[1], tm)
    a = a + es[2] * _natural_rows(o3, u_ref, DILATIONS[2], tm)
    a = a / (es[0] + es[1] + es[2])
    mix = (jnp.dot(a.astype(BF16), wo_ref[:ATTN_W, :], preferred_element_type=F32)
           + jnp.dot(g_ref[...], wo_ref[ATTN_W:, :], preferred_element_type=F32))
    x1 = x_ref[...] + mix
    x1_ref[...] = x1
    h = x1 * lax.rsqrt(jnp.mean(x1 * x1, axis=-1, keepdims=True) + NORM_EPS) * n2_ref[...]
    h_ref[...] = h

    hh, hl, _ = _split3(h)
    dg = lambda u, v: lax.dot_general(u, v, NT_DIMS, preferred_element_type=F32)
    logits = dg(rwh_ref[...], hh) + dg(rwl_ref[...], hh) + dg(rwh_ref[...], hl) + rb_ref[...]
    ie = lax.broadcasted_iota(I32, (N_EXPERTS, tm), 0)
    cur = logits
    vals, idxs = [], []
    for _ in range(TOP_K):
        mx = jnp.max(cur, axis=0, keepdims=True)
        ix = jnp.min(jnp.where(cur == mx, ie, N_EXPERTS), axis=0, keepdims=True)
        vals.append(mx)
        idxs.append(ix)
        cur = jnp.where(ie == ix, -jnp.inf, cur)
    ex = [jnp.exp(v - vals[0]) for v in vals]
    den = ex[0] + ex[1] + ex[2] + ex[3]
    gates = [e / den for e in ex]

    onehots = [ie == ix for ix in idxs]
    chosen = (onehots[0] | onehots[1] | onehots[2] | onehots[3]).astype(F32)
    tr = lax.broadcasted_iota(I32, (tm, tm), 0)
    tc = lax.broadcasted_iota(I32, (tm, tm), 1)
    before = jnp.dot(chosen.astype(BF16), (tr < tc).astype(BF16), preferred_element_type=F32)
    count = jnp.broadcast_to(jnp.sum(chosen, axis=1, keepdims=True), (N_EXPERTS, LANES))
    tcnt_ref[0] = count
    run = jnp.floor((count + (RUN_ALIGN - 1.0)) * (1.0 / RUN_ALIGN)) * RUN_ALIGN
    er = lax.broadcasted_iota(I32, (N_EXPERTS, N_EXPERTS), 0)
    ec = lax.broadcasted_iota(I32, (N_EXPERTS, N_EXPERTS), 1)
    run_start = jnp.dot((ec < er).astype(BF16), run.astype(BF16), preferred_element_type=F32)
    where_ = before + run_start[:, 0:1]
    locs = [jnp.sum(jnp.where(oh, where_, 0.0), axis=0, keepdims=True) for oh in onehots]

    def pack(rows, n):
        sub = lax.broadcasted_iota(I32, (n, tm), 0)
        out = jnp.zeros((n, tm), rows[0].dtype)
        for k, row in enumerate(rows):
            out = jnp.where(sub == k, row, out)
        return out

    m_ref[...] = pack([l.astype(I32) for l in locs], 8)
    slab = pack(gates + locs, 8)
    slab_ref[...] = jnp.concatenate([slab, jnp.zeros((LANES - 8, tm), F32)], axis=0).T


def _mix(outs, lses, g, x2, B, S, w_out, norm2_g, router_w, router_b):
    T, D = x2.shape
    tm = ROUTE_TILE
    nt = S // tm
    wo = w_out.astype(BF16)
    rwt = router_w.T
    rwh = rwt.astype(BF16)
    rwl = (rwt - rwh.astype(F32)).astype(BF16)
    rb = router_b.reshape(N_EXPERTS, 1)
    n2 = norm2_g.reshape(1, D)
    row = lambda c: pl.BlockSpec((tm, c), lambda b, i: (b * nt + i, 0))
    full = lambda a: pl.BlockSpec(a.shape, lambda b, i: (0,) * a.ndim)
    col = pl.BlockSpec((8, tm), lambda b, i: (0, b * nt + i))
    res = [pl.BlockSpec((1, d, tm // d, ATTN_W), lambda b, i: (b, 0, i, 0)) for d in DILATIONS]
    return pl.pallas_call(
        functools.partial(_mix_body, tm=tm),
        grid=(B, nt),
        in_specs=res + res + [row(GLA_VAL_W), row(D), full(wo), full(n2), full(rwh), full(rwl), full(rb)],
        out_specs=[row(D), row(D), col, row(LANES),
                   pl.BlockSpec((1, N_EXPERTS, LANES), lambda b, i: (b * nt + i, 0, 0))],
        out_shape=[jax.ShapeDtypeStruct((T, D), F32),
                   jax.ShapeDtypeStruct((T, D), F32),
                   jax.ShapeDtypeStruct((8, T), I32),
                   jax.ShapeDtypeStruct((T, LANES), F32),
                   jax.ShapeDtypeStruct((T // tm, N_EXPERTS, LANES), F32)],
        scratch_shapes=[pltpu.VMEM((ATTN_W // LANES, tm, LANES), F32)],
        compiler_params=pltpu.CompilerParams(dimension_semantics=("arbitrary", "arbitrary"),
                                             vmem_limit_bytes=48 * 2**20),
        name="mix_router",
    )(*outs, *lses, g, x2, wo, n2, rwh, rwl, rb)


def _run_groups(c):
    return (c + (RUN_ALIGN - 1)) >> RUN_SHIFT


def _plan_body(tc_ref, rs_ref, p0_ref, pn_ref, be_ref, hv_ref, nu_ref, *, nt, nblk):
    shift = MOE_BLOCK.bit_length() - 1
    start = jnp.int32(0)
    for ex in range(N_EXPERTS):
        rows_of = lambda i, ex=ex: _run_groups(tc_ref[i, ex]) << RUN_SHIFT
        tot = lax.fori_loop(0, nt, lambda i, tot: tot + rows_of(i), jnp.int32(0))
        nb = (tot + (MOE_BLOCK - 1)) >> shift
        pad = (nb << shift) - tot
        p0_ref[ex] = start
        pn_ref[ex] = pad >> RUN_SHIFT

        def run(i, at, ex=ex):
            rs_ref[i, ex] = at
            return at + rows_of(i)

        lax.fori_loop(0, nt, run, start + pad)
        b0 = start >> shift

        def fill(j, carry, b0=b0, ex=ex):
            be_ref[b0 + j] = jnp.int32(ex)
            hv_ref[b0 + j] = jnp.int32(MOE_BLOCK // MOE_PART)
            return carry

        lax.fori_loop(0, nb, fill, 0)

        @pl.when(nb > 0)
        def _(b0=b0, pad=pad):
            hv_ref[b0] = (MOE_BLOCK - pad + (MOE_PART - 1)) >> (MOE_PART.bit_length() - 1)

        start = start + (nb << shift)
    used = start >> shift

    def tail(j, carry):
        be_ref[j] = jnp.int32(N_EXPERTS - 1)
        hv_ref[j] = jnp.int32(0)
        return carry

    lax.fori_loop(used, nblk, tail, 0)
    nu_ref[0] = used


def _plan(tcnt, nblk):
    nt = tcnt.shape[0]
    smem = pl.BlockSpec(memory_space=pltpu.SMEM)
    return pl.pallas_call(
        functools.partial(_plan_body, nt=nt, nblk=nblk),
        in_specs=[smem],
        out_specs=[smem] * 6,
        out_shape=[jax.ShapeDtypeStruct(tcnt.shape, I32),
                   jax.ShapeDtypeStruct((N_EXPERTS,), I32),
                   jax.ShapeDtypeStruct((N_EXPERTS,), I32),
                   jax.ShapeDtypeStruct((nblk,), I32),
                   jax.ShapeDtypeStruct((nblk,), I32),
                   jax.ShapeDtypeStruct((1,), I32)],
        name="plan",
    )(tcnt)


def _for_each_run(tc_ref, rs_ref, i, fn):
    big = 4 * RUN_ALIGN
    start = jnp.int32(0)
    for ex in range(N_EXPERTS):
        ng = _run_groups(tc_ref[i, ex])
        g0 = rs_ref[i, ex]

        def piece(q, carry, start=start, g0=g0):
            fn(pl.multiple_of(start + q * big, RUN_ALIGN), pl.multiple_of(g0 + q * big, RUN_ALIGN), big)
            return carry

        def single(q, carry, start=start, g0=g0):
            fn(pl.multiple_of(start + q * RUN_ALIGN, RUN_ALIGN), pl.multiple_of(g0 + q * RUN_ALIGN, RUN_ALIGN),
               RUN_ALIGN)
            return carry

        lax.fori_loop(0, ng >> 2, piece, 0)
        lax.fori_loop((ng >> 2) << 2, ng, single, 0)
        start = start + (ng << RUN_SHIFT)


def _group_count(tc_ref, i):
    n = jnp.int32(0)
    for ex in range(N_EXPERTS):
        n = n + _run_groups(tc_ref[i, ex])
    return n


def _wait_groups(n, src_ref, dst_ref, sem):
    def waits(count, rows):
        def body(q, carry):
            pltpu.make_async_copy(src_ref.at[pl.ds(0, rows)], dst_ref.at[pl.ds(0, rows)], sem).wait()
            return carry
        lax.fori_loop(0, count, body, 0)

    waits(n >> 6, 64 * RUN_ALIGN)
    waits((n >> 3) & 7, 8 * RUN_ALIGN)
    waits(n & 7, RUN_ALIGN)


def _dispatch_body(tc_ref, rs_ref, p0_ref, pn_ref, nu_ref, m_ref, h_ref, xin_ref, g_ref, z_ref, sem, zsem, *, tm, R):
    i = pl.program_id(0)
    slot = i % 2
    rows = m_ref[...].astype(jnp.int16)
    jj = lax.broadcasted_iota(I32, (R, tm), 0).astype(jnp.int16)
    hit = (jj == rows[0:1]) | (jj == rows[1:2]) | (jj == rows[2:3]) | (jj == rows[3:4])
    onehot = jnp.where(hit, jnp.ones((), BF16), jnp.zeros((), BF16))
    g_ref[slot] = jnp.dot(onehot, h_ref[...].astype(BF16), preferred_element_type=F32).astype(BF16)

    group = lambda ref, r: ref.at[pl.ds(r, RUN_ALIGN)]

    @pl.when(i > 0)
    def _():
        _wait_groups(_group_count(tc_ref, i - 1), g_ref.at[1 - slot], xin_ref, sem.at[1 - slot])

    _for_each_run(tc_ref, rs_ref, i, lambda lr, gr, n: pltpu.make_async_copy(
        g_ref.at[slot, pl.ds(lr, n)], xin_ref.at[pl.ds(gr, n)], sem.at[slot]).start())

    @pl.when(i == pl.num_programs(0) - 1)
    def _():
        _wait_groups(_group_count(tc_ref, i), g_ref.at[slot], xin_ref, sem.at[slot])

    @pl.when(i == 0)
    def _():
        z_ref[...] = jnp.zeros_like(z_ref)
        total = jnp.int32(0)
        for ex in range(N_EXPERTS):
            def body(q, carry, ex=ex):
                pltpu.make_async_copy(group(z_ref, 0), group(xin_ref, pl.multiple_of(p0_ref[ex] + q * RUN_ALIGN, RUN_ALIGN)), zsem).start()
                return carry
            lax.fori_loop(0, pn_ref[ex], body, 0)
            total = total + pn_ref[ex]
        _wait_groups(total, z_ref, xin_ref, zsem)

        block = lambda j: xin_ref.at[pl.ds(pl.multiple_of(j * MOE_BLOCK, MOE_BLOCK), MOE_BLOCK)]
        n_blocks = xin_ref.shape[0] // MOE_BLOCK

        def fill(j, carry):
            pltpu.make_async_copy(z_ref, block(j), zsem).start()
            return carry

        def fill_wait(j, carry):
            pltpu.make_async_copy(z_ref, block(j), zsem).wait()
            return carry

        lax.fori_loop(nu_ref[0], n_blocks, fill, 0)
        lax.fori_loop(nu_ref[0], n_blocks, fill_wait, 0)


def _dispatch(tcnt, rowstart, pad0, padn, n_used, m_t, h, n_rows):
    T, D = h.shape
    tm = ROUTE_TILE
    col = pl.BlockSpec((8, tm), lambda i, *_: (0, i))
    return pl.pallas_call(
        functools.partial(_dispatch_body, tm=tm, R=GROUP_ROWS),
        grid_spec=pltpu.PrefetchScalarGridSpec(
            num_scalar_prefetch=5,
            grid=(T // tm,),
            in_specs=[col, pl.BlockSpec((tm, D), lambda i, *_: (i, 0))],
            out_specs=pl.BlockSpec(memory_space=pl.ANY),
            scratch_shapes=[pltpu.VMEM((2, GROUP_ROWS, D), BF16), pltpu.VMEM((MOE_BLOCK, D), BF16),
                            pltpu.SemaphoreType.DMA((2,)), pltpu.SemaphoreType.DMA(())]),
        out_shape=jax.ShapeDtypeStruct((n_rows, D), BF16),
        compiler_params=pltpu.CompilerParams(dimension_semantics=("arbitrary",), vmem_limit_bytes=56 * 2**20),
        name="dispatch",
    )(tcnt, rowstart, pad0, padn, n_used, m_t, h)


def _moe_body(be_ref, nu_ref, hv_ref, x_ref, w1g_ref, w1l_ref, b1g_ref, b1l_ref, w2_ref, b2_ref, y_ref):
    del be_ref
    j = pl.program_id(0)
    live = j < nu_ref[0]

    def expert_mlp(rows):
        x = x_ref[rows, :]
        hg = lax.dot_general(x, w1g_ref[0], NT_DIMS, preferred_element_type=F32) + b1g_ref[0]
        hl = lax.dot_general(x, w1l_ref[0], NT_DIMS, preferred_element_type=F32) + b1l_ref[0]
        xg = jnp.minimum(hg, SWIGLU_LIMIT)
        xl = jnp.clip(hl, -SWIGLU_LIMIT, SWIGLU_LIMIT)
        act = xg * _sigmoid(SWIGLU_ALPHA * xg) * (xl + 1.0)
        y = jnp.dot(act.astype(BF16), w2_ref[0].astype(BF16), preferred_element_type=F32) + b2_ref[0]
        y_ref[rows, :] = y.astype(BF16)

    for parts in range(1, MOE_BLOCK // MOE_PART + 1):
        @pl.when(live & (hv_ref[j] == parts))
        def _(parts=parts):
            skip = MOE_BLOCK - parts * MOE_PART
            expert_mlp(slice(skip, MOE_BLOCK))
            if skip:
                y_ref[:skip, :] = jnp.zeros((skip, y_ref.shape[1]), y_ref.dtype)

    @pl.when(jnp.logical_not(live))
    def _():
        y_ref[...] = jnp.zeros_like(y_ref)


def _moe(block_e, n_used, half_block, xin, w1g, w1l, b1, w2, b2):
    n_rows = xin.shape[0]
    D = w2.shape[2]
    F = w2.shape[1]
    nblk = n_rows // MOE_BLOCK
    b1g = b1[:, 0::2].reshape(N_EXPERTS, 1, F)
    b1l = b1[:, 1::2].reshape(N_EXPERTS, 1, F)
    b2r = b2.reshape(N_EXPERTS, 1, D)
    xspec = pl.BlockSpec((MOE_BLOCK, D), lambda j, be, nu, hv: (jnp.minimum(j, nu[0] - 1), 0))
    yspec = pl.BlockSpec((MOE_BLOCK, D), lambda j, be, nu, hv: (j, 0))
    wspec = lambda a: pl.BlockSpec((1,) + a.shape[1:], lambda j, be, nu, hv: (be[j], 0, 0))
    return pl.pallas_call(
        _moe_body,
        grid_spec=pltpu.PrefetchScalarGridSpec(
            num_scalar_prefetch=3,
            grid=(nblk,),
            in_specs=[xspec, wspec(w1g), wspec(w1l), wspec(b1g), wspec(b1l), wspec(w2), wspec(b2r)],
            out_specs=yspec),
        out_shape=jax.ShapeDtypeStruct(xin.shape, BF16),
        compiler_params=pltpu.CompilerParams(dimension_semantics=("arbitrary",),
                                             vmem_limit_bytes=48 * 2**20),
        name="moe",
    )(block_e, n_used, half_block, xin, w1g, w1l, b1g, b1l, w2, b2r)


def _combine_body(tc_ref, rs_ref, slab_ref, y_ref, x1_ref, fg_ref, out_ref, g_ref, sem, *, tm, R):
    i = pl.program_id(0)
    slot = i % 2

    def fetch(tile, slot):
        _for_each_run(tc_ref, rs_ref, tile, lambda lr, gr, n: pltpu.make_async_copy(
            y_ref.at[pl.ds(gr, n)], g_ref.at[slot, pl.ds(lr, n)], sem.at[slot]).start())

    @pl.when(i == 0)
    def _():
        g_ref[...] = jnp.zeros_like(g_ref)
        fetch(i, slot)

    @pl.when(i + 1 < pl.num_programs(0))
    def _():
        fetch(i + 1, 1 - slot)

    slab = slab_ref[...]
    gates = slab[:, 0:TOP_K].astype(BF16)
    rows = slab[:, TOP_K:2 * TOP_K].astype(I32).astype(jnp.int16)
    jj = lax.broadcasted_iota(I32, (tm, R), 1).astype(jnp.int16)
    sel = jnp.zeros((tm, R), BF16)
    for k in range(TOP_K):
        sel = jnp.where(jj == rows[:, k:k + 1], gates[:, k:k + 1], sel)

    _wait_groups(_group_count(tc_ref, i), y_ref, g_ref.at[slot], sem.at[slot])
    acc = x1_ref[...] + jnp.dot(sel, g_ref[slot], preferred_element_type=F32)
    out_ref[...] = acc * lax.rsqrt(jnp.mean(acc * acc, axis=-1, keepdims=True) + NORM_EPS) * fg_ref[...]


def _combine(tcnt, rowstart, slab, y, x1, final_g):
    T, D = x1.shape
    tm = ROUTE_TILE
    fg = final_g.reshape(1, D)
    return pl.pallas_call(
        functools.partial(_combine_body, tm=tm, R=GROUP_ROWS),
        grid_spec=pltpu.PrefetchScalarGridSpec(
            num_scalar_prefetch=2,
            grid=(T // tm,),
            in_specs=[pl.BlockSpec((tm, LANES), lambda i, *_: (i, 0)),
                      pl.BlockSpec(memory_space=pl.ANY),
                      pl.BlockSpec((tm, D), lambda i, *_: (i, 0)),
                      pl.BlockSpec((1, D), lambda i, *_: (0, 0))],
            out_specs=pl.BlockSpec((tm, D), lambda i, *_: (i, 0)),
            scratch_shapes=[pltpu.VMEM((2, GROUP_ROWS, D), BF16), pltpu.SemaphoreType.DMA((2,))]),
        out_shape=jax.ShapeDtypeStruct((T, D), F32),
        compiler_params=pltpu.CompilerParams(dimension_semantics=("arbitrary",), vmem_limit_bytes=56 * 2**20),
        name="combine",
    )(tcnt, rowstart, slab, y, x1, fg)


def _mixers(x2, B, S, norm1_g, w_in, rel_bias, wg_f, bg_f, wg_b, bg_b, gla_norm_g, w1):
    a1, a4, a16, gla, bcum, w1g, w1l = _in_proj(x2, B, S, norm1_g, w_in, wg_f, bg_f, wg_b, bg_b, w1)
    branches = [_attn_branch(a, rel_bias, d) for a, d in zip((a1, a4, a16), DILATIONS)]
    g = _gla(gla, bcum, gla_norm_g, B, S)
    return [o for o, _ in branches], [l for _, l in branches], g, w1g, w1l


def _moe_layer(h, m_t, slab, tcnt_f, x1, w1g, w1l, b1, w2, b2, final_g):
    T = x1.shape[0]
    nt = T // ROUTE_TILE
    nblk = (T * TOP_K + nt * N_EXPERTS * (RUN_ALIGN - 1)) // MOE_BLOCK + N_EXPERTS
    tcnt = tcnt_f[:, :, 0].astype(I32)
    rowstart, pad0, padn, block_e, half_block, n_used = _plan(tcnt, nblk)
    xin = _dispatch(tcnt, rowstart, pad0, padn, n_used, m_t, h, nblk * MOE_BLOCK)
    y = _moe(block_e, n_used, half_block, xin, w1g, w1l, b1, w2, b2)
    return _combine(tcnt, rowstart, slab, y, x1, final_g)


def kernel(x, norm1_g, w_in, rel_bias, gla_wg_fwd, gla_bg_fwd, gla_wg_bwd, gla_bg_bwd, gla_norm_g, w_out, norm2_g, router_w, router_b, moe_w1, moe_b1, moe_w2, moe_b2, final_g):
    B, S, D = x.shape
    assert w_in.shape[0] == 1, "one layer"
    x2 = x.reshape(B * S, D)
    outs, lses, g, w1g, w1l = _mixers(x2, B, S, norm1_g[0], w_in[0], rel_bias, gla_wg_fwd[0], gla_bg_fwd[0],
                                      gla_wg_bwd[0], gla_bg_bwd[0], gla_norm_g[0], moe_w1[0])
    x1, h, m_t, slab, tcnt = _mix(outs, lses, g, x2, B, S, w_out[0], norm2_g[0], router_w[0], router_b[0])
    out = _moe_layer(h, m_t, slab, tcnt, x1, w1g, w1l, moe_b1[0], moe_w2[0], moe_b2[0], final_g)
    return out.reshape(B, S, D)
```

```python
import functools

import jax
import jax.numpy as jnp
import numpy as np
from jax import lax
from jax.experimental import pallas as pl
from jax.experimental.pallas import tpu as pltpu

F32, BF16, I32 = jnp.float32, jnp.bfloat16, jnp.int32

LANES = 128
HEAD_DIM = 64
ATTN_HEADS = 8
ATTN_W = ATTN_HEADS * HEAD_DIM
HALF_SPAN = 64
ATTN_QBLK = 128
ATTN_KWIN = 256
ATTN_STEP_ROWS = 512
DILATIONS = (1, 4, 16)
T5_BUCKETS = 32
T5_MAX_DISTANCE = 1024
NEG_INF = -1e30
LOG2_E = 1.4426950408889634
LN_2 = 0.6931471805599453
GLA_KEY_W = 256
GLA_VAL_W = 512
GLA_DV = 128
GLA_RANK = 16
GLA_TAU = 16.0
GLA_CHUNK = 64
N_EXPERTS = 32
TOP_K = 4
SWIGLU_LIMIT = 7.0
SWIGLU_ALPHA = 1.702
NORM_EPS = 1e-5
MOE_BLOCK = 1024
MOE_PART = 256
W1_SUBTILE = 512
ROUTE_TILE = 512
RUN_ALIGN = 16
RUN_SHIFT = 4
GROUP_ROWS = ROUTE_TILE * TOP_K + N_EXPERTS * RUN_ALIGN

ATTN_COLS = 3 * ATTN_W
GLA_COLS = 2 * GLA_KEY_W + 2 * GLA_VAL_W
Z_COLS = LANES
PROJ_COLS = ATTN_COLS + GLA_COLS + Z_COLS

NT_DIMS = (((1,), (1,)), ((), ()))


def _sigmoid(x):
    return 1.0 / (1.0 + jnp.exp(-x))


def _split3(a):
    h1 = a.astype(BF16)
    r1 = a - h1.astype(F32)
    h2 = r1.astype(BF16)
    h3 = (r1 - h2.astype(F32)).astype(BF16)
    return h1, h2, h3


def _dot_f32(a, b):
    ah, al, _ = _split3(a)
    bh, bl, _ = _split3(b)
    d = lambda u, v: jnp.dot(u, v, preferred_element_type=F32)
    return d(ah, bh) + d(al, bh) + d(ah, bl)


def _split_w1_tile(w_ref, g_ref, l_ref, t_ref):
    sub = t_ref.shape[1]
    for s in range(w_ref.shape[2] // sub):
        wt = w_ref[0, :, s * sub:(s + 1) * sub].T
        out_rows = slice(s * sub // 2, (s + 1) * sub // 2)
        for c in range(wt.shape[1] // LANES):
            cols = slice(c * LANES, (c + 1) * LANES)
            t_ref[c] = wt[:, cols]
            g_ref[0, out_rows, cols] = t_ref[c, pl.ds(0, sub // 2, stride=2), :].astype(BF16)
            l_ref[0, out_rows, cols] = t_ref[c, pl.ds(1, sub // 2, stride=2), :].astype(BF16)


def _in_proj_body(x_ref, g_ref, w_ref, wg_ref, bg_ref, w1_ref, a1_ref, a4_ref, a16_ref, gla_ref, b_ref,
                  w1g_ref, w1l_ref, t_ref, *, tm):
    _split_w1_tile(w1_ref, w1g_ref, w1l_ref, t_ref)

    x = x_ref[...]
    n = x * lax.rsqrt(jnp.mean(x * x, axis=-1, keepdims=True) + NORM_EPS) * g_ref[...]
    p = jnp.dot(n.astype(BF16), w_ref[...], preferred_element_type=F32)
    gla_ref[...] = p[:, ATTN_COLS:ATTN_COLS + GLA_COLS].astype(BF16)

    qkv = p[:, :ATTN_COLS].astype(BF16)
    a1_ref[0, 0] = qkv
    dst = lax.broadcasted_iota(I32, (tm, tm), 0)
    src = lax.broadcasted_iota(I32, (tm, tm), 1)
    for dil, a_ref in ((DILATIONS[1], a4_ref), (DILATIONS[2], a16_ref)):
        per = tm // dil
        perm = (src == (dst % per) * dil + dst // per).astype(F32).astype(BF16)
        moved = jnp.dot(perm, qkv, preferred_element_type=F32).astype(BF16)
        for r in range(dil):
            a_ref[0, r] = moved[r * per:(r + 1) * per]

    z = p[:, ATTN_COLS + GLA_COLS:]
    zz = _dot_f32(z, wg_ref[...]) + bg_ref[...]
    la = (jnp.minimum(zz, 0.0) - jnp.log1p(jnp.exp(-jnp.abs(zz)))) * (1.0 / GLA_TAU)
    C = GLA_CHUNK
    r_i = lax.broadcasted_iota(I32, (C, C), 0)
    c_i = lax.broadcasted_iota(I32, (C, C), 1)
    lower = (r_i >= c_i).astype(BF16)
    upper = (r_i <= c_i).astype(BF16)
    W = GLA_KEY_W
    for j in range(tm // C):
        rows = slice(j * C, (j + 1) * C)
        for tri, cols in ((lower, slice(0, W)), (upper, slice(W, 2 * W))):
            terms = jnp.concatenate(_split3(la[rows, cols]), axis=1)
            cs = jnp.dot(tri, terms, preferred_element_type=F32)
            b_ref[rows, cols] = cs[:, :W] + cs[:, W:2 * W] + cs[:, 2 * W:]


def _in_proj(x2, B, S, norm_g, w_in, wg_f, bg_f, wg_b, bg_b, w1):
    T, D = x2.shape
    tm = 256
    nt = S // tm
    E, _, F2 = w1.shape
    w1_cols = E * F2 // (B * nt)
    per_e = F2 // w1_cols
    assert w1_cols * B * nt == E * F2 and per_e * w1_cols == F2 and w1_cols % W1_SUBTILE == 0
    w1_in = pl.BlockSpec((1, D, w1_cols), lambda b, i: ((b * nt + i) // per_e, 0, (b * nt + i) % per_e))
    w1_out = pl.BlockSpec((1, w1_cols // 2, D), lambda b, i: ((b * nt + i) // per_e, (b * nt + i) % per_e, 0))
    w = jnp.pad(w_in.astype(BF16), ((0, 0), (0, PROJ_COLS - w_in.shape[1])))
    wg = jnp.zeros((Z_COLS, 2 * GLA_KEY_W), F32)
    wg = wg.at[:GLA_RANK, :GLA_KEY_W].set(wg_f).at[GLA_RANK:2 * GLA_RANK, GLA_KEY_W:].set(wg_b)
    bg = jnp.concatenate([bg_f, bg_b]).reshape(1, 2 * GLA_KEY_W)
    row = lambda c: pl.BlockSpec((tm, c), lambda b, i: (b * nt + i, 0))
    full = lambda a: pl.BlockSpec(a.shape, lambda b, i: (0,) * a.ndim)
    res = lambda d: pl.BlockSpec((1, d, tm // d, ATTN_COLS), lambda b, i: (b, 0, i, 0))
    res_shape = lambda d: jax.ShapeDtypeStruct((B, d, S // d, ATTN_COLS), BF16)
    g2 = norm_g.reshape(1, D)
    return pl.pallas_call(
        functools.partial(_in_proj_body, tm=tm),
        grid=(B, nt),
        in_specs=[row(D), full(g2), full(w), full(wg), full(bg), w1_in],
        out_specs=[res(d) for d in DILATIONS] + [row(GLA_COLS), row(2 * GLA_KEY_W), w1_out, w1_out],
        out_shape=[res_shape(d) for d in DILATIONS]
                  + [jax.ShapeDtypeStruct((T, GLA_COLS), BF16), jax.ShapeDtypeStruct((T, 2 * GLA_KEY_W), F32)]
                  + [jax.ShapeDtypeStruct((E, F2 // 2, D), BF16)] * 2,
        scratch_shapes=[pltpu.VMEM((D // LANES, W1_SUBTILE, LANES), F32)],
        compiler_params=pltpu.CompilerParams(dimension_semantics=("arbitrary", "arbitrary"),
                                             vmem_limit_bytes=56 * 2**20),
        name="in_proj",
    )(x2, g2, w, wg, bg, w1)


def _t5_bucket(rel):
    nb = T5_BUCKETS // 2
    max_exact = nb // 2
    n = np.abs(rel)
    large = max_exact + (np.log(np.maximum(n, 1).astype(np.float32) / max_exact)
                         / np.log(T5_MAX_DISTANCE / max_exact) * (nb - max_exact)).astype(np.int32)
    large = np.minimum(large, nb - 1)
    return (np.where(rel > 0, nb, 0) + np.where(n < max_exact, n, large)).astype(np.int32)


def _bucket_tables(dil):
    s = np.arange(ATTN_QBLK)[:, None]
    t = np.arange(ATTN_KWIN)[None, :]
    tabs = []
    for c in (0, HALF_SPAN, 2 * HALF_SPAN):
        off = t - c - s
        tabs.append(np.where(np.abs(off) <= HALF_SPAN, _t5_bucket(off * dil), -1))
    return np.stack(tabs).astype(np.int32)


def _attn_body(rb_ref, idx_ref, q_ref, k_ref, v_ref, o_ref, lse_ref, tab_ref, *, L, TQ):
    first = (pl.program_id(0) == 0) & (pl.program_id(1) == 0) & (pl.program_id(2) == 0)

    @pl.when(first)
    def _():
        for v in range(3):
            idx = idx_ref[v]
            for h in range(ATTN_HEADS):
                fill = lambda j, t, h=h, idx=idx: jnp.where(idx == j, rb_ref[j, h] * LOG2_E, t)
                tab_ref[v, h] = lax.fori_loop(0, T5_BUCKETS, fill, jnp.where(idx < 0, NEG_INF, 0.0).astype(F32))

    i = pl.program_id(2)
    lo = lax.broadcasted_iota(I32, (ATTN_QBLK, LANES), 1) < HEAD_DIM
    scale = HEAD_DIM ** -0.5 * LOG2_E
    lane = lax.broadcasted_iota(I32, (ATTN_QBLK, LANES), 1)
    for rb, sb in [(rb, sb) for rb in range(q_ref.shape[0]) for sb in range(TQ // ATTN_QBLK)]:
        l0 = i * TQ + sb * ATTN_QBLK
        lse_tile = jnp.zeros((ATTN_QBLK, LANES), F32)
        ws = pl.multiple_of(jnp.clip(l0 - HALF_SPAN, 0, L - ATTN_KWIN), HALF_SPAN)
        var = jnp.where(l0 == 0, 0, jnp.where(l0 == L - ATTN_QBLK, 2, 1))
        rows = slice(sb * ATTN_QBLK, (sb + 1) * ATTN_QBLK)
        for hp in range(ATTN_HEADS // 2):
            cols = slice(hp * LANES, (hp + 1) * LANES)
            q2 = q_ref[rb, rows, cols]
            k2 = k_ref[rb, pl.ds(ws, ATTN_KWIN), cols]
            v2 = v_ref[rb, pl.ds(ws, ATTN_KWIN), cols]
            zero = jnp.zeros_like(q2)
            qs = jnp.concatenate([jnp.where(lo, q2, zero), jnp.where(lo, zero, q2)], axis=0)
            s = lax.dot_general(qs, k2, NT_DIMS, preferred_element_type=F32)
            bias = jnp.concatenate([tab_ref[var, 2 * hp], tab_ref[var, 2 * hp + 1]], axis=0)
            logits = s * scale + bias
            m = jnp.max(logits, axis=-1, keepdims=True)
            p = jnp.exp2(logits - m)
            den = jnp.sum(p, axis=-1, keepdims=True)
            pv = jnp.dot(p.astype(BF16), v2, preferred_element_type=F32) / den
            lse = m * LN_2 + jnp.log(den)
            o_ref[rb, rows, cols] = jnp.where(lo, pv[:ATTN_QBLK], pv[ATTN_QBLK:]).astype(BF16)
            lse_tile = jnp.where(lane == 2 * hp, lse[:ATTN_QBLK], lse_tile)
            lse_tile = jnp.where(lane == 2 * hp + 1, lse[ATTN_QBLK:], lse_tile)
        lse_ref[rb, rows, :] = lse_tile


def _attn_branch(attn, rel_bias, dil):
    B, _, L, _ = attn.shape
    TQ = min(ATTN_STEP_ROWS, L)
    RB = ATTN_STEP_ROWS // TQ
    idx = jnp.asarray(_bucket_tables(dil))
    body = functools.partial(_attn_body, L=L, TQ=TQ)
    kv = lambda j: pl.BlockSpec((None, RB, L, ATTN_W), lambda b, r, i: (b, r, 0, j))
    q_spec = pl.BlockSpec((None, RB, TQ, ATTN_W), lambda b, r, i: (b, r, i, 0))
    return pl.pallas_call(
        body,
        grid=(B, dil // RB, L // TQ),
        in_specs=[pl.BlockSpec(memory_space=pltpu.SMEM),
                  pl.BlockSpec(idx.shape, lambda b, r, i: (0, 0, 0)),
                  q_spec, kv(1), kv(2)],
        out_specs=[q_spec, pl.BlockSpec((None, RB, TQ, LANES), lambda b, r, i: (b, r, i, 0))],
        out_shape=[jax.ShapeDtypeStruct((B, dil, L, ATTN_W), BF16),
                   jax.ShapeDtypeStruct((B, dil, L, LANES), F32)],
        scratch_shapes=[pltpu.VMEM((3, ATTN_HEADS, ATTN_QBLK, ATTN_KWIN), F32)],
        compiler_params=pltpu.CompilerParams(dimension_semantics=("arbitrary",) * 3,
                                             vmem_limit_bytes=48 * 2**20),
        name=f"attn_d{dil}",
    )(rel_bias, idx, attn, attn, attn)


def _gla_body(q_ref, k_ref, v_ref, gate_ref, bf_ref, bb_ref, ng_ref, out_ref, of_ref, ob_ref, st_ref, *, S):
    C = GLA_CHUNK
    nc = S // C
    r = lax.broadcasted_iota(I32, (C, LANES), 0)
    s2 = lax.broadcasted_iota(I32, (C, LANES), 1) % HEAD_DIM
    lo = lax.broadcasted_iota(I32, (C, LANES), 1) < HEAD_DIM
    blockdiag = ((lax.broadcasted_iota(I32, (2 * GLA_DV, LANES), 0) < GLA_DV)
                 == (lax.broadcasted_iota(I32, (2 * GLA_DV, LANES), 1) < HEAD_DIM))
    st_ref[...] = jnp.zeros_like(st_ref)

    def one_chunk(ci, direction):
        mask = (r >= s2) if direction == 0 else (r <= s2)
        b_ref, o_ref = (bf_ref, of_ref) if direction == 0 else (bb_ref, ob_ref)
        rows = pl.ds(pl.multiple_of(ci * C, C), C)
        q = q_ref[0, rows, :].astype(F32) * (HEAD_DIM ** -0.5)
        k = k_ref[0, rows, :].astype(F32)
        v = v_ref[0, rows, :]
        b = b_ref[0, rows, :]
        g_tot = b[C - 1:C, :] if direction == 0 else b[0:1, :]
        qd = (q * jnp.exp(b)).astype(BF16)
        kd = (k * jnp.exp(-b)).astype(BF16)
        kst = (k * jnp.exp(g_tot - b)).astype(BF16)
        zk = jnp.zeros_like(kd)
        ks = jnp.concatenate([jnp.where(lo, kd, zk), jnp.where(lo, zk, kd)], axis=0)
        a = lax.dot_general(qd, ks, NT_DIMS, preferred_element_type=F32)
        a = jnp.where(mask, a, 0.0).astype(BF16)
        zv = jnp.zeros((C, GLA_DV), BF16)
        vblk = jnp.concatenate([jnp.concatenate([v[:, :GLA_DV], zv], axis=1),
                                jnp.concatenate([zv, v[:, GLA_DV:]], axis=1)], axis=0)
        st = st_ref[direction]
        o_ref[rows, :] = (jnp.dot(a, vblk, preferred_element_type=F32)
                          + lax.dot_general(qd, st.astype(BF16), NT_DIMS, preferred_element_type=F32))
        d_st = jnp.dot(v.astype(F32).T.astype(BF16), kst, preferred_element_type=F32)
        st_ref[direction] = st * jnp.exp(g_tot) + jnp.where(blockdiag, d_st, 0.0)

    def step(i, carry):
        one_chunk(i, 0)
        one_chunk(nc - 1 - i, 1)
        return carry

    lax.fori_loop(0, nc, step, 0, unroll=8)

    tr = 256
    def epilogue(i, carry):
        rows = pl.ds(pl.multiple_of(i * tr, tr), tr)
        for hh in range(2):
            cols = slice(hh * GLA_DV, (hh + 1) * GLA_DV)
            o = of_ref[rows, cols] + ob_ref[rows, cols]
            o = o * lax.rsqrt(jnp.mean(o * o, axis=-1, keepdims=True) + NORM_EPS) * ng_ref[0, :, cols]
            gate = gate_ref[0, rows, cols].astype(F32)
            out_ref[0, rows, cols] = (o * (gate * _sigmoid(gate))).astype(BF16)
        return carry

    lax.fori_loop(0, S // tr, epilogue, 0)


def _gla(gla, bcum, norm_g, B, S):
    g3 = gla.reshape(B, S, GLA_COLS)
    b3 = bcum.reshape(B, S, 2 * GLA_KEY_W)
    ng = norm_g.reshape(2, 1, 2 * GLA_DV)
    body = functools.partial(_gla_body, S=S)
    spec = lambda w, off: pl.BlockSpec((1, S, w), lambda b, p: (b, 0, off + p))
    out = pl.pallas_call(
        body,
        grid=(B, 2),
        in_specs=[spec(LANES, 0),
                  spec(LANES, GLA_KEY_W // LANES),
                  spec(2 * GLA_DV, (2 * GLA_KEY_W) // (2 * GLA_DV)),
                  spec(2 * GLA_DV, (2 * GLA_KEY_W + GLA_VAL_W) // (2 * GLA_DV)),
                  spec(LANES, 0),
                  spec(LANES, GLA_KEY_W // LANES),
                  pl.BlockSpec((1, 1, 2 * GLA_DV), lambda b, p: (p, 0, 0))],
        out_specs=pl.BlockSpec((1, S, 2 * GLA_DV), lambda b, p: (b, 0, p)),
        out_shape=jax.ShapeDtypeStruct((B, S, GLA_VAL_W), BF16),
        scratch_shapes=[pltpu.VMEM((S, 2 * GLA_DV), F32), pltpu.VMEM((S, 2 * GLA_DV), F32),
                        pltpu.VMEM((2, 2 * GLA_DV, LANES), F32)],
        compiler_params=pltpu.CompilerParams(dimension_semantics=("arbitrary", "arbitrary"),
                                             vmem_limit_bytes=56 * 2**20),
        name="gla",
    )(g3, g3, g3, g3, b3, b3, ng)
    return out.reshape(B * S, GLA_VAL_W)


def _natural_rows(blk_ref, u_ref, dil, tm):
    if dil == 1:
        return blk_ref[0, 0].astype(F32)
    chunks = blk_ref.shape[-1] // LANES
    for r in range(dil):
        for c in range(chunks):
            u_ref[c, pl.ds(r, tm // dil, stride=dil), :] = blk_ref[0, r, :, c * LANES:(c + 1) * LANES].astype(F32)
    return jnp.concatenate([u_ref[c] for c in range(chunks)], axis=1)


def _per_head_lanes(w):
    head = lax.broadcasted_iota(I32, (LANES, ATTN_W), 0)
    col = lax.broadcasted_iota(I32, (LANES, ATTN_W), 1)
    spread = (col // HEAD_DIM == head).astype(F32).astype(BF16)
    hi, lo, _ = _split3(w)
    return (jnp.dot(hi, spread, preferred_element_type=F32) + jnp.dot(lo, spread, preferred_element_type=F32))


def _mix_body(o1, o2, o3, l1, l2, l3, g_ref, x_ref, wo_ref, n2_ref, rwh_ref, rwl_ref, rb_ref,
              x1_ref, h_ref, m_ref, slab_ref, tcnt_ref, u_ref, *, tm):
    ls = [_natural_rows(l, u_ref, d, tm) for l, d in zip((l1, l2, l3), DILATIONS)]
    m = jnp.maximum(jnp.maximum(ls[0], ls[1]), ls[2])
    es = [jnp.exp(l - m) for l in ls]
    den = es[0] + es[1] + es[2]
    a = _per_head_lanes(es[0] / den) * _natural_rows(o1, u_ref, DILATIONS[0], tm)
    a = a + _per_head_lanes(es[1] / den) * _natural_rows(o2, u_ref, DILATIONS[1], tm)
    a = a + _per_head_lanes(es[2] / den) * _natural_rows(o3, u_ref, DILATIONS[2], tm)
    mix = (jnp.dot(a.astype(BF16), wo_ref[:ATTN_W, :], preferred_element_type=F32)
           + jnp.dot(g_ref[...], wo_ref[ATTN_W:, :], preferred_element_type=F32))
    x1 = x_ref[...] + mix
    x1_ref[...] = x1
    h = x1 * lax.rsqrt(jnp.mean(x1 * x1, axis=-1, keepdims=True) + NORM_EPS) * n2_ref[...]
    h_ref[...] = h

    hh, hl, _ = _split3(h)
    dg = lambda u, v: lax.dot_general(u, v, NT_DIMS, preferred_element_type=F32)
    logits = dg(rwh_ref[...], hh) + dg(rwl_ref[...], hh) + dg(rwh_ref[...], hl) + rb_ref[...]
    ie = lax.broadcasted_iota(I32, (N_EXPERTS, tm), 0)
    cur = logits
    vals, idxs = [], []
    for _ in range(TOP_K):
        mx = jnp.max(cur, axis=0, keepdims=True)
        ix = jnp.min(jnp.where(cur == mx, ie, N_EXPERTS), axis=0, keepdims=True)
        vals.append(mx)
        idxs.append(ix)
        cur = jnp.where(ie == ix, -jnp.inf, cur)
    ex = [jnp.exp(v - vals[0]) for v in vals]
    den = ex[0] + ex[1] + ex[2] + ex[3]
    gates = [e / den for e in ex]

    onehots = [ie == ix for ix in idxs]
    chosen = (onehots[0] | onehots[1] | onehots[2] | onehots[3]).astype(F32)
    tr = lax.broadcasted_iota(I32, (tm, tm), 0)
    tc = lax.broadcasted_iota(I32, (tm, tm), 1)
    before = jnp.dot(chosen.astype(BF16), (tr < tc).astype(BF16), preferred_element_type=F32)
    count = jnp.broadcast_to(jnp.sum(chosen, axis=1, keepdims=True), (N_EXPERTS, LANES))
    tcnt_ref[0] = count
    run = jnp.floor((count + (RUN_ALIGN - 1.0)) * (1.0 / RUN_ALIGN)) * RUN_ALIGN
    er = lax.broadcasted_iota(I32, (N_EXPERTS, N_EXPERTS), 0)
    ec = lax.broadcasted_iota(I32, (N_EXPERTS, N_EXPERTS), 1)
    run_start = jnp.dot((ec < er).astype(BF16), run.astype(BF16), preferred_element_type=F32)
    where_ = before + run_start[:, 0:1]
    locs = [jnp.sum(jnp.where(oh, where_, 0.0), axis=0, keepdims=True) for oh in onehots]

    def pack(rows, n):
        sub = lax.broadcasted_iota(I32, (n, tm), 0)
        out = jnp.zeros((n, tm), rows[0].dtype)
        for k, row in enumerate(rows):
            out = jnp.where(sub == k, row, out)
        return out

    m_ref[...] = pack([l.astype(I32) for l in locs], 8)
    slab = pack(gates + locs, 8)
    slab_ref[...] = jnp.concatenate([slab, jnp.zeros((LANES - 8, tm), F32)], axis=0).T


def _mix(outs, lses, g, x2, B, S, w_out, norm2_g, router_w, router_b):
    T, D = x2.shape
    tm = ROUTE_TILE
    nt = S // tm
    wo = w_out.astype(BF16)
    rwt = router_w.T
    rwh = rwt.astype(BF16)
    rwl = (rwt - rwh.astype(F32)).astype(BF16)
    rb = router_b.reshape(N_EXPERTS, 1)
    n2 = norm2_g.reshape(1, D)
    row = lambda c: pl.BlockSpec((tm, c), lambda b, i: (b * nt + i, 0))
    full = lambda a: pl.BlockSpec(a.shape, lambda b, i: (0,) * a.ndim)
    col = pl.BlockSpec((8, tm), lambda b, i: (0, b * nt + i))
    res = [pl.BlockSpec((1, d, tm // d, ATTN_W), lambda b, i: (b, 0, i, 0)) for d in DILATIONS]
    res_lse = [pl.BlockSpec((1, d, tm // d, LANES), lambda b, i: (b, 0, i, 0)) for d in DILATIONS]
    return pl.pallas_call(
        functools.partial(_mix_body, tm=tm),
        grid=(B, nt),
        in_specs=res + res_lse + [row(GLA_VAL_W), row(D), full(wo), full(n2), full(rwh), full(rwl), full(rb)],
        out_specs=[row(D), row(D), col, row(LANES),
                   pl.BlockSpec((1, N_EXPERTS, LANES), lambda b, i: (b * nt + i, 0, 0))],
        out_shape=[jax.ShapeDtypeStruct((T, D), F32),
                   jax.ShapeDtypeStruct((T, D), F32),
                   jax.ShapeDtypeStruct((8, T), I32),
                   jax.ShapeDtypeStruct((T, LANES), F32),
                   jax.ShapeDtypeStruct((T // tm, N_EXPERTS, LANES), F32)],
        scratch_shapes=[pltpu.VMEM((ATTN_W // LANES, tm, LANES), F32)],
        compiler_params=pltpu.CompilerParams(dimension_semantics=("arbitrary", "arbitrary"),
                                             vmem_limit_bytes=48 * 2**20),
        name="mix_router",
    )(*outs, *lses, g, x2, wo, n2, rwh, rwl, rb)


def _run_groups(c):
    return (c + (RUN_ALIGN - 1)) >> RUN_SHIFT


def _plan_body(tc_ref, rs_ref, p0_ref, pn_ref, be_ref, hv_ref, nu_ref, *, nt, nblk):
    shift = MOE_BLOCK.bit_length() - 1
    start = jnp.int32(0)
    for ex in range(N_EXPERTS):
        rows_of = lambda i, ex=ex: _run_groups(tc_ref[i, ex]) << RUN_SHIFT
        tot = lax.fori_loop(0, nt, lambda i, tot: tot + rows_of(i), jnp.int32(0))
        nb = (tot + (MOE_BLOCK - 1)) >> shift
        pad = (nb << shift) - tot
        p0_ref[ex] = start
        pn_ref[ex] = pad >> RUN_SHIFT

        def run(i, at, ex=ex):
            rs_ref[i, ex] = at
            return at + rows_of(i)

        lax.fori_loop(0, nt, run, start + pad)
        b0 = start >> shift

        def fill(j, carry, b0=b0, ex=ex):
            be_ref[b0 + j] = jnp.int32(ex)
            hv_ref[b0 + j] = jnp.int32(MOE_BLOCK // MOE_PART)
            return carry

        lax.fori_loop(0, nb, fill, 0)

        @pl.when(nb > 0)
        def _(b0=b0, pad=pad):
            hv_ref[b0] = (MOE_BLOCK - pad + (MOE_PART - 1)) >> (MOE_PART.bit_length() - 1)

        start = start + (nb << shift)
    used = start >> shift

    def tail(j, carry):
        be_ref[j] = jnp.int32(N_EXPERTS - 1)
        hv_ref[j] = jnp.int32(0)
        return carry

    lax.fori_loop(used, nblk, tail, 0)
    nu_ref[0] = used


def _plan(tcnt, nblk):
    nt = tcnt.shape[0]
    smem = pl.BlockSpec(memory_space=pltpu.SMEM)
    return pl.pallas_call(
        functools.partial(_plan_body, nt=nt, nblk=nblk),
        in_specs=[smem],
        out_specs=[smem] * 6,
        out_shape=[jax.ShapeDtypeStruct(tcnt.shape, I32),
                   jax.ShapeDtypeStruct((N_EXPERTS,), I32),
                   jax.ShapeDtypeStruct((N_EXPERTS,), I32),
                   jax.ShapeDtypeStruct((nblk,), I32),
                   jax.ShapeDtypeStruct((nblk,), I32),
                   jax.ShapeDtypeStruct((1,), I32)],
        name="plan",
    )(tcnt)


def _for_each_run(tc_ref, rs_ref, i, fn):
    big = 4 * RUN_ALIGN
    start = jnp.int32(0)
    for ex in range(N_EXPERTS):
        ng = _run_groups(tc_ref[i, ex])
        g0 = rs_ref[i, ex]

        def piece(q, carry, start=start, g0=g0):
            fn(pl.multiple_of(start + q * big, RUN_ALIGN), pl.multiple_of(g0 + q * big, RUN_ALIGN), big)
            return carry

        def single(q, carry, start=start, g0=g0):
            fn(pl.multiple_of(start + q * RUN_ALIGN, RUN_ALIGN), pl.multiple_of(g0 + q * RUN_ALIGN, RUN_ALIGN),
               RUN_ALIGN)
            return carry

        lax.fori_loop(0, ng >> 2, piece, 0)
        lax.fori_loop((ng >> 2) << 2, ng, single, 0)
        start = start + (ng << RUN_SHIFT)


def _group_count(tc_ref, i):
    n = jnp.int32(0)
    for ex in range(N_EXPERTS):
        n = n + _run_groups(tc_ref[i, ex])
    return n


def _wait_groups(n, src_ref, dst_ref, sem):
    def waits(count, rows):
        def body(q, carry):
            pltpu.make_async_copy(src_ref.at[pl.ds(0, rows)], dst_ref.at[pl.ds(0, rows)], sem).wait()
            return carry
        lax.fori_loop(0, count, body, 0)

    waits(n >> 6, 64 * RUN_ALIGN)
    waits((n >> 3) & 7, 8 * RUN_ALIGN)
    waits(n & 7, RUN_ALIGN)


def _dispatch_body(tc_ref, rs_ref, p0_ref, pn_ref, nu_ref, m_ref, h_ref, xin_ref, g_ref, z_ref, sem, zsem, *, tm, R):
    i = pl.program_id(0)
    slot = i % 2
    rows = m_ref[...].astype(jnp.int16)
    jj = lax.broadcasted_iota(I32, (R, tm), 0).astype(jnp.int16)
    hit = (jj == rows[0:1]) | (jj == rows[1:2]) | (jj == rows[2:3]) | (jj == rows[3:4])
    onehot = jnp.where(hit, jnp.ones((), BF16), jnp.zeros((), BF16))
    g_ref[slot] = jnp.dot(onehot, h_ref[...].astype(BF16), preferred_element_type=F32).astype(BF16)

    group = lambda ref, r: ref.at[pl.ds(r, RUN_ALIGN)]

    @pl.when(i > 0)
    def _():
        _wait_groups(_group_count(tc_ref, i - 1), g_ref.at[1 - slot], xin_ref, sem.at[1 - slot])

    _for_each_run(tc_ref, rs_ref, i, lambda lr, gr, n: pltpu.make_async_copy(
        g_ref.at[slot, pl.ds(lr, n)], xin_ref.at[pl.ds(gr, n)], sem.at[slot]).start())

    @pl.when(i == pl.num_programs(0) - 1)
    def _():
        _wait_groups(_group_count(tc_ref, i), g_ref.at[slot], xin_ref, sem.at[slot])

    @pl.when(i == 0)
    def _():
        z_ref[...] = jnp.zeros_like(z_ref)
        total = jnp.int32(0)
        for ex in range(N_EXPERTS):
            def body(q, carry, ex=ex):
                pltpu.make_async_copy(group(z_ref, 0), group(xin_ref, pl.multiple_of(p0_ref[ex] + q * RUN_ALIGN, RUN_ALIGN)), zsem).start()
                return carry
            lax.fori_loop(0, pn_ref[ex], body, 0)
            total = total + pn_ref[ex]
        _wait_groups(total, z_ref, xin_ref, zsem)

        block = lambda j: xin_ref.at[pl.ds(pl.multiple_of(j * MOE_BLOCK, MOE_BLOCK), MOE_BLOCK)]
        n_blocks = xin_ref.shape[0] // MOE_BLOCK

        def fill(j, carry):
            pltpu.make_async_copy(z_ref, block(j), zsem).start()
            return carry

        def fill_wait(j, carry):
            pltpu.make_async_copy(z_ref, block(j), zsem).wait()
            return carry

        lax.fori_loop(nu_ref[0], n_blocks, fill, 0)
        lax.fori_loop(nu_ref[0], n_blocks, fill_wait, 0)


def _dispatch(tcnt, rowstart, pad0, padn, n_used, m_t, h, n_rows):
    T, D = h.shape
    tm = ROUTE_TILE
    col = pl.BlockSpec((8, tm), lambda i, *_: (0, i))
    return pl.pallas_call(
        functools.partial(_dispatch_body, tm=tm, R=GROUP_ROWS),
        grid_spec=pltpu.PrefetchScalarGridSpec(
            num_scalar_prefetch=5,
            grid=(T // tm,),
            in_specs=[col, pl.BlockSpec((tm, D), lambda i, *_: (i, 0))],
            out_specs=pl.BlockSpec(memory_space=pl.ANY),
            scratch_shapes=[pltpu.VMEM((2, GROUP_ROWS, D), BF16), pltpu.VMEM((MOE_BLOCK, D), BF16),
                            pltpu.SemaphoreType.DMA((2,)), pltpu.SemaphoreType.DMA(())]),
        out_shape=jax.ShapeDtypeStruct((n_rows, D), BF16),
        compiler_params=pltpu.CompilerParams(dimension_semantics=("arbitrary",), vmem_limit_bytes=56 * 2**20),
        name="dispatch",
    )(tcnt, rowstart, pad0, padn, n_used, m_t, h)


def _moe_body(be_ref, nu_ref, hv_ref, x_ref, w1g_ref, w1l_ref, b1g_ref, b1l_ref, w2_ref, b2_ref, y_ref):
    del be_ref
    j = pl.program_id(0)
    live = j < nu_ref[0]

    def expert_mlp(rows):
        x = x_ref[rows, :]
        hg = lax.dot_general(x, w1g_ref[0], NT_DIMS, preferred_element_type=F32) + b1g_ref[0]
        hl = lax.dot_general(x, w1l_ref[0], NT_DIMS, preferred_element_type=F32) + b1l_ref[0]
        xg = jnp.minimum(hg, SWIGLU_LIMIT)
        xl = jnp.clip(hl, -SWIGLU_LIMIT, SWIGLU_LIMIT)
        act = xg * _sigmoid(SWIGLU_ALPHA * xg) * (xl + 1.0)
        y = jnp.dot(act.astype(BF16), w2_ref[0].astype(BF16), preferred_element_type=F32) + b2_ref[0]
        y_ref[rows, :] = y.astype(BF16)

    for parts in range(1, MOE_BLOCK // MOE_PART + 1):
        @pl.when(live & (hv_ref[j] == parts))
        def _(parts=parts):
            skip = MOE_BLOCK - parts * MOE_PART
            expert_mlp(slice(skip, MOE_BLOCK))
            if skip:
                y_ref[:skip, :] = jnp.zeros((skip, y_ref.shape[1]), y_ref.dtype)

    @pl.when(jnp.logical_not(live))
    def _():
        y_ref[...] = jnp.zeros_like(y_ref)


def _moe(block_e, n_used, half_block, xin, w1g, w1l, b1, w2, b2):
    n_rows = xin.shape[0]
    D = w2.shape[2]
    F = w2.shape[1]
    nblk = n_rows // MOE_BLOCK
    b1g = b1[:, 0::2].reshape(N_EXPERTS, 1, F)
    b1l = b1[:, 1::2].reshape(N_EXPERTS, 1, F)
    b2r = b2.reshape(N_EXPERTS, 1, D)
    xspec = pl.BlockSpec((MOE_BLOCK, D), lambda j, be, nu, hv: (jnp.minimum(j, nu[0] - 1), 0))
    yspec = pl.BlockSpec((MOE_BLOCK, D), lambda j, be, nu, hv: (j, 0))
    wspec = lambda a: pl.BlockSpec((1,) + a.shape[1:], lambda j, be, nu, hv: (be[j], 0, 0))
    return pl.pallas_call(
        _moe_body,
        grid_spec=pltpu.PrefetchScalarGridSpec(
            num_scalar_prefetch=3,
            grid=(nblk,),
            in_specs=[xspec, wspec(w1g), wspec(w1l), wspec(b1g), wspec(b1l), wspec(w2), wspec(b2r)],
            out_specs=yspec),
        out_shape=jax.ShapeDtypeStruct(xin.shape, BF16),
        compiler_params=pltpu.CompilerParams(dimension_semantics=("arbitrary",),
                                             vmem_limit_bytes=48 * 2**20),
        name="moe",
    )(block_e, n_used, half_block, xin, w1g, w1l, b1g, b1l, w2, b2r)


def _combine_body(tc_ref, rs_ref, slab_ref, y_ref, x1_ref, fg_ref, out_ref, g_ref, sem, *, tm, R):
    i = pl.program_id(0)
    slot = i % 2

    def fetch(tile, slot):
        _for_each_run(tc_ref, rs_ref, tile, lambda lr, gr, n: pltpu.make_async_copy(
            y_ref.at[pl.ds(gr, n)], g_ref.at[slot, pl.ds(lr, n)], sem.at[slot]).start())

    @pl.when(i == 0)
    def _():
        g_ref[...] = jnp.zeros_like(g_ref)
        fetch(i, slot)

    @pl.when(i + 1 < pl.num_programs(0))
    def _():
        fetch(i + 1, 1 - slot)

    slab = slab_ref[...]
    gates = slab[:, 0:TOP_K].astype(BF16)
    rows = slab[:, TOP_K:2 * TOP_K].astype(I32).astype(jnp.int16)
    jj = lax.broadcasted_iota(I32, (tm, R), 1).astype(jnp.int16)
    sel = jnp.zeros((tm, R), BF16)
    for k in range(TOP_K):
        sel = jnp.where(jj == rows[:, k:k + 1], gates[:, k:k + 1], sel)

    _wait_groups(_group_count(tc_ref, i), y_ref, g_ref.at[slot], sem.at[slot])
    acc = x1_ref[...] + jnp.dot(sel, g_ref[slot], preferred_element_type=F32)
    out_ref[...] = acc * lax.rsqrt(jnp.mean(acc * acc, axis=-1, keepdims=True) + NORM_EPS) * fg_ref[...]


def _combine(tcnt, rowstart, slab, y, x1, final_g):
    T, D = x1.shape
    tm = ROUTE_TILE
    fg = final_g.reshape(1, D)
    return pl.pallas_call(
        functools.partial(_combine_body, tm=tm, R=GROUP_ROWS),
        grid_spec=pltpu.PrefetchScalarGridSpec(
            num_scalar_prefetch=2,
            grid=(T // tm,),
            in_specs=[pl.BlockSpec((tm, LANES), lambda i, *_: (i, 0)),
                      pl.BlockSpec(memory_space=pl.ANY),
                      pl.BlockSpec((tm, D), lambda i, *_: (i, 0)),
                      pl.BlockSpec((1, D), lambda i, *_: (0, 0))],
            out_specs=pl.BlockSpec((tm, D), lambda i, *_: (i, 0)),
            scratch_shapes=[pltpu.VMEM((2, GROUP_ROWS, D), BF16), pltpu.SemaphoreType.DMA((2,))]),
        out_shape=jax.ShapeDtypeStruct((T, D), F32),
        compiler_params=pltpu.CompilerParams(dimension_semantics=("arbitrary",), vmem_limit_bytes=56 * 2**20),
        name="combine",
    )(tcnt, rowstart, slab, y, x1, fg)


def _mixers(x2, B, S, norm1_g, w_in, rel_bias, wg_f, bg_f, wg_b, bg_b, gla_norm_g, w1):
    a1, a4, a16, gla, bcum, w1g, w1l = _in_proj(x2, B, S, norm1_g, w_in, wg_f, bg_f, wg_b, bg_b, w1)
    branches = [_attn_branch(a, rel_bias, d) for a, d in zip((a1, a4, a16), DILATIONS)]
    g = _gla(gla, bcum, gla_norm_g, B, S)
    return [o for o, _ in branches], [l for _, l in branches], g, w1g, w1l


def _moe_layer(h, m_t, slab, tcnt_f, x1, w1g, w1l, b1, w2, b2, final_g):
    T = x1.shape[0]
    nt = T // ROUTE_TILE
    nblk = (T * TOP_K + nt * N_EXPERTS * (RUN_ALIGN - 1)) // MOE_BLOCK + N_EXPERTS
    tcnt = tcnt_f[:, :, 0].astype(I32)
    rowstart, pad0, padn, block_e, half_block, n_used = _plan(tcnt, nblk)
    xin = _dispatch(tcnt, rowstart, pad0, padn, n_used, m_t, h, nblk * MOE_BLOCK)
    y = _moe(block_e, n_used, half_block, xin, w1g, w1l, b1, w2, b2)
    return _combine(tcnt, rowstart, slab, y, x1, final_g)


def kernel(x, norm1_g, w_in, rel_bias, gla_wg_fwd, gla_bg_fwd, gla_wg_bwd, gla_bg_bwd, gla_norm_g, w_out, norm2_g, router_w, router_b, moe_w1, moe_b1, moe_w2, moe_b2, final_g):
    B, S, D = x.shape
    assert w_in.shape[0] == 1, "one layer"
    x2 = x.reshape(B * S, D)
    outs, lses, g, w1g, w1l = _mixers(x2, B, S, norm1_g[0], w_in[0], rel_bias, gla_wg_fwd[0], gla_bg_fwd[0],
                                      gla_wg_bwd[0], gla_bg_bwd[0], gla_norm_g[0], moe_w1[0])
    x1, h, m_t, slab, tcnt = _mix(outs, lses, g, x2, B, S, w_out[0], norm2_g[0], router_w[0], router_b[0])
    out = _moe_layer(h, m_t, slab, tcnt, x1, w1g, w1l, moe_b1[0], moe_w2[0], moe_b2[0], final_g)
    return out.reshape(B, S, D)
```

```python
import functools

import jax
import jax.numpy as jnp
import numpy as np
from jax import lax
from jax.experimental import pallas as pl
from jax.experimental.pallas import tpu as pltpu

F32, BF16, I32 = jnp.float32, jnp.bfloat16, jnp.int32

LANES = 128
HEAD_DIM = 64
ATTN_HEADS = 8
ATTN_W = ATTN_HEADS * HEAD_DIM
HALF_SPAN = 64
ATTN_QBLK = 128
ATTN_KWIN = 256
ATTN_STEP_ROWS = 512
DILATIONS = (1, 4, 16)
T5_BUCKETS = 32
T5_MAX_DISTANCE = 1024
NEG_INF = -1e30
LOG2_E = 1.4426950408889634
LN_2 = 0.6931471805599453
GLA_KEY_W = 256
GLA_VAL_W = 512
GLA_DV = 128
GLA_RANK = 16
GLA_TAU = 16.0
GLA_CHUNK = 64
N_EXPERTS = 32
TOP_K = 4
SWIGLU_LIMIT = 7.0
SWIGLU_ALPHA = 1.702
NORM_EPS = 1e-5
MOE_BLOCK = 1024
MOE_PART = 256
W1_SUBTILE = 512
ROUTE_TILE = 512
RUN_ALIGN = 16
RUN_SHIFT = 4
GROUP_ROWS = ROUTE_TILE * TOP_K + N_EXPERTS * RUN_ALIGN

ATTN_COLS = 3 * ATTN_W
GLA_COLS = 2 * GLA_KEY_W + 2 * GLA_VAL_W
Z_COLS = LANES
PROJ_COLS = ATTN_COLS + GLA_COLS + Z_COLS

NT_DIMS = (((1,), (1,)), ((), ()))


def _sigmoid(x):
    return 1.0 / (1.0 + jnp.exp(-x))


def _split3(a):
    h1 = a.astype(BF16)
    r1 = a - h1.astype(F32)
    h2 = r1.astype(BF16)
    h3 = (r1 - h2.astype(F32)).astype(BF16)
    return h1, h2, h3


def _dot_f32(a, b):
    ah, al, _ = _split3(a)
    bh, bl, _ = _split3(b)
    d = lambda u, v: jnp.dot(u, v, preferred_element_type=F32)
    return d(ah, bh) + d(al, bh) + d(ah, bl)


def _split_w1_tile(w_ref, g_ref, l_ref, t_ref):
    sub = t_ref.shape[1]
    for s in range(w_ref.shape[2] // sub):
        wt = w_ref[0, :, s * sub:(s + 1) * sub].T
        out_rows = slice(s * sub // 2, (s + 1) * sub // 2)
        for c in range(wt.shape[1] // LANES):
            cols = slice(c * LANES, (c + 1) * LANES)
            t_ref[c] = wt[:, cols]
            g_ref[0, out_rows, cols] = t_ref[c, pl.ds(0, sub // 2, stride=2), :].astype(BF16)
            l_ref[0, out_rows, cols] = t_ref[c, pl.ds(1, sub // 2, stride=2), :].astype(BF16)


def _in_proj_body(x_ref, g_ref, w_ref, wg_ref, bg_ref, w1_ref, a1_ref, a4_ref, a16_ref, gla_ref, b_ref,
                  w1g_ref, w1l_ref, t_ref, wb_ref, *, tm):
    @pl.when((pl.program_id(0) == 0) & (pl.program_id(1) == 0))
    def _():
        wb_ref[...] = jnp.zeros_like(wb_ref)
        wb_ref[:, :w_ref.shape[1]] = w_ref[...].astype(BF16)

    _split_w1_tile(w1_ref, w1g_ref, w1l_ref, t_ref)

    x = x_ref[...]
    n = x * lax.rsqrt(jnp.mean(x * x, axis=-1, keepdims=True) + NORM_EPS) * g_ref[...]
    p = jnp.dot(n.astype(BF16), wb_ref[...], preferred_element_type=F32)
    gla_ref[...] = p[:, ATTN_COLS:ATTN_COLS + GLA_COLS].astype(BF16)

    qkv = p[:, :ATTN_COLS].astype(BF16)
    a1_ref[0, 0] = qkv
    dst = lax.broadcasted_iota(I32, (tm, tm), 0)
    src = lax.broadcasted_iota(I32, (tm, tm), 1)
    for dil, a_ref in ((DILATIONS[1], a4_ref), (DILATIONS[2], a16_ref)):
        per = tm // dil
        perm = (src == (dst % per) * dil + dst // per).astype(F32).astype(BF16)
        moved = jnp.dot(perm, qkv, preferred_element_type=F32).astype(BF16)
        for r in range(dil):
            a_ref[0, r] = moved[r * per:(r + 1) * per]

    z = p[:, ATTN_COLS + GLA_COLS:]
    zz = _dot_f32(z, wg_ref[...]) + bg_ref[...]
    la = (jnp.minimum(zz, 0.0) - jnp.log1p(jnp.exp(-jnp.abs(zz)))) * (1.0 / GLA_TAU)
    C = GLA_CHUNK
    r_i = lax.broadcasted_iota(I32, (C, C), 0)
    c_i = lax.broadcasted_iota(I32, (C, C), 1)
    lower = (r_i >= c_i).astype(BF16)
    upper = (r_i <= c_i).astype(BF16)
    W = GLA_KEY_W
    for j in range(tm // C):
        rows = slice(j * C, (j + 1) * C)
        for tri, cols in ((lower, slice(0, W)), (upper, slice(W, 2 * W))):
            terms = jnp.concatenate(_split3(la[rows, cols]), axis=1)
            cs = jnp.dot(tri, terms, preferred_element_type=F32)
            b_ref[rows, cols] = cs[:, :W] + cs[:, W:2 * W] + cs[:, 2 * W:]


def _in_proj(x2, B, S, norm_g, w_in, wg_f, bg_f, wg_b, bg_b, w1):
    T, D = x2.shape
    tm = 256
    nt = S // tm
    E, _, F2 = w1.shape
    w1_cols = E * F2 // (B * nt)
    per_e = F2 // w1_cols
    assert w1_cols * B * nt == E * F2 and per_e * w1_cols == F2 and w1_cols % W1_SUBTILE == 0
    w1_in = pl.BlockSpec((1, D, w1_cols), lambda b, i: ((b * nt + i) // per_e, 0, (b * nt + i) % per_e))
    w1_out = pl.BlockSpec((1, w1_cols // 2, D), lambda b, i: ((b * nt + i) // per_e, (b * nt + i) % per_e, 0))
    w_spec = pl.BlockSpec(w_in.shape, lambda b, i: (0, 0), pipeline_mode=pl.Buffered(1))
    wg = jnp.zeros((Z_COLS, 2 * GLA_KEY_W), F32)
    wg = wg.at[:GLA_RANK, :GLA_KEY_W].set(wg_f).at[GLA_RANK:2 * GLA_RANK, GLA_KEY_W:].set(wg_b)
    bg = jnp.concatenate([bg_f, bg_b]).reshape(1, 2 * GLA_KEY_W)
    row = lambda c: pl.BlockSpec((tm, c), lambda b, i: (b * nt + i, 0))
    full = lambda a: pl.BlockSpec(a.shape, lambda b, i: (0,) * a.ndim)
    res = lambda d: pl.BlockSpec((1, d, tm // d, ATTN_COLS), lambda b, i: (b, 0, i, 0))
    res_shape = lambda d: jax.ShapeDtypeStruct((B, d, S // d, ATTN_COLS), BF16)
    g2 = norm_g.reshape(1, D)
    return pl.pallas_call(
        functools.partial(_in_proj_body, tm=tm),
        grid=(B, nt),
        in_specs=[row(D), full(g2), w_spec, full(wg), full(bg), w1_in],
        out_specs=[res(d) for d in DILATIONS] + [row(GLA_COLS), row(2 * GLA_KEY_W), w1_out, w1_out],
        out_shape=[res_shape(d) for d in DILATIONS]
                  + [jax.ShapeDtypeStruct((T, GLA_COLS), BF16), jax.ShapeDtypeStruct((T, 2 * GLA_KEY_W), F32)]
                  + [jax.ShapeDtypeStruct((E, F2 // 2, D), BF16)] * 2,
        scratch_shapes=[pltpu.VMEM((D // LANES, W1_SUBTILE, LANES), F32), pltpu.VMEM((D, PROJ_COLS), BF16)],
        compiler_params=pltpu.CompilerParams(dimension_semantics=("arbitrary", "arbitrary"),
                                             vmem_limit_bytes=56 * 2**20),
        name="in_proj",
    )(x2, g2, w_in, wg, bg, w1)


def _t5_bucket(rel):
    nb = T5_BUCKETS // 2
    max_exact = nb // 2
    n = np.abs(rel)
    large = max_exact + (np.log(np.maximum(n, 1).astype(np.float32) / max_exact)
                         / np.log(T5_MAX_DISTANCE / max_exact) * (nb - max_exact)).astype(np.int32)
    large = np.minimum(large, nb - 1)
    return (np.where(rel > 0, nb, 0) + np.where(n < max_exact, n, large)).astype(np.int32)


def _bucket_tables(dil):
    s = np.arange(ATTN_QBLK)[:, None]
    t = np.arange(ATTN_KWIN)[None, :]
    tabs = []
    for c in (0, HALF_SPAN, 2 * HALF_SPAN):
        off = t - c - s
        tabs.append(np.where(np.abs(off) <= HALF_SPAN, _t5_bucket(off * dil), -1))
    return np.stack(tabs).astype(np.int32)


def _attn_body(rb_ref, idx_ref, q_ref, k_ref, v_ref, o_ref, lse_ref, tab_ref, *, L, TQ):
    first = (pl.program_id(0) == 0) & (pl.program_id(1) == 0) & (pl.program_id(2) == 0)

    @pl.when(first)
    def _():
        for v in range(3):
            idx = idx_ref[v]
            for h in range(ATTN_HEADS):
                fill = lambda j, t, h=h, idx=idx: jnp.where(idx == j, rb_ref[j, h] * LOG2_E, t)
                tab_ref[v, h] = lax.fori_loop(0, T5_BUCKETS, fill, jnp.where(idx < 0, NEG_INF, 0.0).astype(F32))

    i = pl.program_id(2)
    lo = lax.broadcasted_iota(I32, (ATTN_QBLK, LANES), 1) < HEAD_DIM
    scale = HEAD_DIM ** -0.5 * LOG2_E
    lane = lax.broadcasted_iota(I32, (ATTN_QBLK, LANES), 1)
    for rb, sb in [(rb, sb) for rb in range(q_ref.shape[0]) for sb in range(TQ // ATTN_QBLK)]:
        l0 = i * TQ + sb * ATTN_QBLK
        lse_tile = jnp.zeros((ATTN_QBLK, LANES), F32)
        ws = pl.multiple_of(jnp.clip(l0 - HALF_SPAN, 0, L - ATTN_KWIN), HALF_SPAN)
        var = jnp.where(l0 == 0, 0, jnp.where(l0 == L - ATTN_QBLK, 2, 1))
        rows = slice(sb * ATTN_QBLK, (sb + 1) * ATTN_QBLK)
        for hp in range(ATTN_HEADS // 2):
            cols = slice(hp * LANES, (hp + 1) * LANES)
            q2 = q_ref[rb, rows, cols]
            k2 = k_ref[rb, pl.ds(ws, ATTN_KWIN), cols]
            v2 = v_ref[rb, pl.ds(ws, ATTN_KWIN), cols]
            zero = jnp.zeros_like(q2)
            qs = jnp.concatenate([jnp.where(lo, q2, zero), jnp.where(lo, zero, q2)], axis=0)
            s = lax.dot_general(qs, k2, NT_DIMS, preferred_element_type=F32)
            bias = jnp.concatenate([tab_ref[var, 2 * hp], tab_ref[var, 2 * hp + 1]], axis=0)
            logits = s * scale + bias
            m = jnp.max(logits, axis=-1, keepdims=True)
            p = jnp.exp2(logits - m)
            den = jnp.sum(p, axis=-1, keepdims=True)
            pv = jnp.dot(p.astype(BF16), v2, preferred_element_type=F32) / den
            lse = m * LN_2 + jnp.log(den)
            o_ref[rb, rows, cols] = jnp.where(lo, pv[:ATTN_QBLK], pv[ATTN_QBLK:]).astype(BF16)
            lse_tile = jnp.where(lane == 2 * hp, lse[:ATTN_QBLK], lse_tile)
            lse_tile = jnp.where(lane == 2 * hp + 1, lse[ATTN_QBLK:], lse_tile)
        lse_ref[rb, rows, :] = lse_tile


def _attn_branch(attn, rel_bias, dil):
    B, _, L, _ = attn.shape
    TQ = min(ATTN_STEP_ROWS, L)
    RB = ATTN_STEP_ROWS // TQ
    idx = jnp.asarray(_bucket_tables(dil))
    body = functools.partial(_attn_body, L=L, TQ=TQ)
    kv = lambda j: pl.BlockSpec((None, RB, L, ATTN_W), lambda b, r, i: (b, r, 0, j))
    q_spec = pl.BlockSpec((None, RB, TQ, ATTN_W), lambda b, r, i: (b, r, i, 0))
    return pl.pallas_call(
        body,
        grid=(B, dil // RB, L // TQ),
        in_specs=[pl.BlockSpec(memory_space=pltpu.SMEM),
                  pl.BlockSpec(idx.shape, lambda b, r, i: (0, 0, 0)),
                  q_spec, kv(1), kv(2)],
        out_specs=[q_spec, pl.BlockSpec((None, RB, TQ, LANES), lambda b, r, i: (b, r, i, 0))],
        out_shape=[jax.ShapeDtypeStruct((B, dil, L, ATTN_W), BF16),
                   jax.ShapeDtypeStruct((B, dil, L, LANES), F32)],
        scratch_shapes=[pltpu.VMEM((3, ATTN_HEADS, ATTN_QBLK, ATTN_KWIN), F32)],
        compiler_params=pltpu.CompilerParams(dimension_semantics=("arbitrary",) * 3,
                                             vmem_limit_bytes=48 * 2**20),
        name=f"attn_d{dil}",
    )(rel_bias, idx, attn, attn, attn)


def _gla_body(q_ref, k_ref, v_ref, gate_ref, bf_ref, bb_ref, ng_ref, out_ref, of_ref, ob_ref, st_ref, *, S):
    C = GLA_CHUNK
    nc = S // C
    r = lax.broadcasted_iota(I32, (C, LANES), 0)
    s2 = lax.broadcasted_iota(I32, (C, LANES), 1) % HEAD_DIM
    lo = lax.broadcasted_iota(I32, (C, LANES), 1) < HEAD_DIM
    blockdiag = ((lax.broadcasted_iota(I32, (2 * GLA_DV, LANES), 0) < GLA_DV)
                 == (lax.broadcasted_iota(I32, (2 * GLA_DV, LANES), 1) < HEAD_DIM))
    st_ref[...] = jnp.zeros_like(st_ref)

    def one_chunk(ci, direction):
        mask = (r >= s2) if direction == 0 else (r <= s2)
        b_ref, o_ref = (bf_ref, of_ref) if direction == 0 else (bb_ref, ob_ref)
        rows = pl.ds(pl.multiple_of(ci * C, C), C)
        q = q_ref[0, rows, :].astype(F32) * (HEAD_DIM ** -0.5)
        k = k_ref[0, rows, :].astype(F32)
        v = v_ref[0, rows, :]
        b = b_ref[0, rows, :]
        g_tot = b[C - 1:C, :] if direction == 0 else b[0:1, :]
        qd = (q * jnp.exp(b)).astype(BF16)
        kd = (k * jnp.exp(-b)).astype(BF16)
        kst = (k * jnp.exp(g_tot - b)).astype(BF16)
        zk = jnp.zeros_like(kd)
        ks = jnp.concatenate([jnp.where(lo, kd, zk), jnp.where(lo, zk, kd)], axis=0)
        a = lax.dot_general(qd, ks, NT_DIMS, preferred_element_type=F32)
        a = jnp.where(mask, a, 0.0).astype(BF16)
        zv = jnp.zeros((C, GLA_DV), BF16)
        vblk = jnp.concatenate([jnp.concatenate([v[:, :GLA_DV], zv], axis=1),
                                jnp.concatenate([zv, v[:, GLA_DV:]], axis=1)], axis=0)
        st = st_ref[direction]
        o_ref[rows, :] = (jnp.dot(a, vblk, preferred_element_type=F32)
                          + lax.dot_general(qd, st.astype(BF16), NT_DIMS, preferred_element_type=F32))
        d_st = jnp.dot(v.astype(F32).T.astype(BF16), kst, preferred_element_type=F32)
        st_ref[direction] = st * jnp.exp(g_tot) + jnp.where(blockdiag, d_st, 0.0)

    def step(i, carry):
        one_chunk(i, 0)
        one_chunk(nc - 1 - i, 1)
        return carry

    lax.fori_loop(0, nc, step, 0, unroll=8)

    tr = 256
    def epilogue(i, carry):
        rows = pl.ds(pl.multiple_of(i * tr, tr), tr)
        for hh in range(2):
            cols = slice(hh * GLA_DV, (hh + 1) * GLA_DV)
            o = of_ref[rows, cols] + ob_ref[rows, cols]
            o = o * lax.rsqrt(jnp.mean(o * o, axis=-1, keepdims=True) + NORM_EPS) * ng_ref[0, :, cols]
            gate = gate_ref[0, rows, cols].astype(F32)
            out_ref[0, rows, cols] = (o * (gate * _sigmoid(gate))).astype(BF16)
        return carry

    lax.fori_loop(0, S // tr, epilogue, 0)


def _gla(gla, bcum, norm_g, B, S):
    g3 = gla.reshape(B, S, GLA_COLS)
    b3 = bcum.reshape(B, S, 2 * GLA_KEY_W)
    ng = norm_g.reshape(2, 1, 2 * GLA_DV)
    body = functools.partial(_gla_body, S=S)
    spec = lambda w, off: pl.BlockSpec((1, S, w), lambda b, p: (b, 0, off + p))
    out = pl.pallas_call(
        body,
        grid=(B, 2),
        in_specs=[spec(LANES, 0),
                  spec(LANES, GLA_KEY_W // LANES),
                  spec(2 * GLA_DV, (2 * GLA_KEY_W) // (2 * GLA_DV)),
                  spec(2 * GLA_DV, (2 * GLA_KEY_W + GLA_VAL_W) // (2 * GLA_DV)),
                  spec(LANES, 0),
                  spec(LANES, GLA_KEY_W // LANES),
                  pl.BlockSpec((1, 1, 2 * GLA_DV), lambda b, p: (p, 0, 0))],
        out_specs=pl.BlockSpec((1, S, 2 * GLA_DV), lambda b, p: (b, 0, p)),
        out_shape=jax.ShapeDtypeStruct((B, S, GLA_VAL_W), BF16),
        scratch_shapes=[pltpu.VMEM((S, 2 * GLA_DV), F32), pltpu.VMEM((S, 2 * GLA_DV), F32),
                        pltpu.VMEM((2, 2 * GLA_DV, LANES), F32)],
        compiler_params=pltpu.CompilerParams(dimension_semantics=("arbitrary", "arbitrary"),
                                             vmem_limit_bytes=56 * 2**20),
        name="gla",
    )(g3, g3, g3, g3, b3, b3, ng)
    return out.reshape(B * S, GLA_VAL_W)


def _natural_rows(blk_ref, u_ref, dil, tm):
    if dil == 1:
        return blk_ref[0, 0].astype(F32)
    chunks = blk_ref.shape[-1] // LANES
    for r in range(dil):
        for c in range(chunks):
            u_ref[c, pl.ds(r, tm // dil, stride=dil), :] = blk_ref[0, r, :, c * LANES:(c + 1) * LANES].astype(F32)
    return jnp.concatenate([u_ref[c] for c in range(chunks)], axis=1)


def _per_head_lanes(w):
    head = lax.broadcasted_iota(I32, (LANES, ATTN_W), 0)
    col = lax.broadcasted_iota(I32, (LANES, ATTN_W), 1)
    spread = (col // HEAD_DIM == head).astype(F32).astype(BF16)
    hi, lo, _ = _split3(w)
    return (jnp.dot(hi, spread, preferred_element_type=F32) + jnp.dot(lo, spread, preferred_element_type=F32))


def _mix_body(o1, o2, o3, l1, l2, l3, g_ref, x_ref, wo_ref, n2_ref, rwh_ref, rwl_ref, rb_ref,
              x1_ref, h_ref, m_ref, slab_ref, tcnt_ref, u_ref, wob_ref, *, tm):
    @pl.when((pl.program_id(0) == 0) & (pl.program_id(1) == 0))
    def _():
        wob_ref[...] = wo_ref[...].astype(BF16)

    ls = [_natural_rows(l, u_ref, d, tm) for l, d in zip((l1, l2, l3), DILATIONS)]
    m = jnp.maximum(jnp.maximum(ls[0], ls[1]), ls[2])
    es = [jnp.exp(l - m) for l in ls]
    den = es[0] + es[1] + es[2]
    a = _per_head_lanes(es[0] / den) * _natural_rows(o1, u_ref, DILATIONS[0], tm)
    a = a + _per_head_lanes(es[1] / den) * _natural_rows(o2, u_ref, DILATIONS[1], tm)
    a = a + _per_head_lanes(es[2] / den) * _natural_rows(o3, u_ref, DILATIONS[2], tm)
    mix = (jnp.dot(a.astype(BF16), wob_ref[:ATTN_W, :], preferred_element_type=F32)
           + jnp.dot(g_ref[...], wob_ref[ATTN_W:, :], preferred_element_type=F32))
    x1 = x_ref[...] + mix
    x1_ref[...] = x1
    h = x1 * lax.rsqrt(jnp.mean(x1 * x1, axis=-1, keepdims=True) + NORM_EPS) * n2_ref[...]
    h_ref[...] = h

    hh, hl, _ = _split3(h)
    dg = lambda u, v: lax.dot_general(u, v, NT_DIMS, preferred_element_type=F32)
    logits = dg(rwh_ref[...], hh) + dg(rwl_ref[...], hh) + dg(rwh_ref[...], hl) + rb_ref[...]
    ie = lax.broadcasted_iota(I32, (N_EXPERTS, tm), 0)
    cur = logits
    vals, idxs = [], []
    for _ in range(TOP_K):
        mx = jnp.max(cur, axis=0, keepdims=True)
        ix = jnp.min(jnp.where(cur == mx, ie, N_EXPERTS), axis=0, keepdims=True)
        vals.append(mx)
        idxs.append(ix)
        cur = jnp.where(ie == ix, -jnp.inf, cur)
    ex = [jnp.exp(v - vals[0]) for v in vals]
    den = ex[0] + ex[1] + ex[2] + ex[3]
    gates = [e / den for e in ex]

    onehots = [ie == ix for ix in idxs]
    chosen = (onehots[0] | onehots[1] | onehots[2] | onehots[3]).astype(F32)
    tr = lax.broadcasted_iota(I32, (tm, tm), 0)
    tc = lax.broadcasted_iota(I32, (tm, tm), 1)
    before = jnp.dot(chosen.astype(BF16), (tr < tc).astype(BF16), preferred_element_type=F32)
    count = jnp.broadcast_to(jnp.sum(chosen, axis=1, keepdims=True), (N_EXPERTS, LANES))
    tcnt_ref[0] = count
    run = jnp.floor((count + (RUN_ALIGN - 1.0)) * (1.0 / RUN_ALIGN)) * RUN_ALIGN
    er = lax.broadcasted_iota(I32, (N_EXPERTS, N_EXPERTS), 0)
    ec = lax.broadcasted_iota(I32, (N_EXPERTS, N_EXPERTS), 1)
    run_start = jnp.dot((ec < er).astype(BF16), run.astype(BF16), preferred_element_type=F32)
    where_ = before + run_start[:, 0:1]
    locs = [jnp.sum(jnp.where(oh, where_, 0.0), axis=0, keepdims=True) for oh in onehots]

    def pack(rows, n):
        sub = lax.broadcasted_iota(I32, (n, tm), 0)
        out = jnp.zeros((n, tm), rows[0].dtype)
        for k, row in enumerate(rows):
            out = jnp.where(sub == k, row, out)
        return out

    m_ref[...] = pack([l.astype(I32) for l in locs], 8)
    slab = pack(gates + locs, 8)
    slab_ref[...] = jnp.concatenate([slab, jnp.zeros((LANES - 8, tm), F32)], axis=0).T


def _mix(outs, lses, g, x2, B, S, w_out, norm2_g, router_w, router_b):
    T, D = x2.shape
    tm = ROUTE_TILE
    nt = S // tm
    wo_spec = pl.BlockSpec(w_out.shape, lambda b, i: (0, 0), pipeline_mode=pl.Buffered(1))
    rwt = router_w.T
    rwh = rwt.astype(BF16)
    rwl = (rwt - rwh.astype(F32)).astype(BF16)
    rb = router_b.reshape(N_EXPERTS, 1)
    n2 = norm2_g.reshape(1, D)
    row = lambda c: pl.BlockSpec((tm, c), lambda b, i: (b * nt + i, 0))
    full = lambda a: pl.BlockSpec(a.shape, lambda b, i: (0,) * a.ndim)
    col = pl.BlockSpec((8, tm), lambda b, i: (0, b * nt + i))
    res = [pl.BlockSpec((1, d, tm // d, ATTN_W), lambda b, i: (b, 0, i, 0)) for d in DILATIONS]
    res_lse = [pl.BlockSpec((1, d, tm // d, LANES), lambda b, i: (b, 0, i, 0)) for d in DILATIONS]
    return pl.pallas_call(
        functools.partial(_mix_body, tm=tm),
        grid=(B, nt),
        in_specs=res + res_lse + [row(GLA_VAL_W), row(D), wo_spec, full(n2), full(rwh), full(rwl), full(rb)],
        out_specs=[row(D), row(D), col, row(LANES),
                   pl.BlockSpec((1, N_EXPERTS, LANES), lambda b, i: (b * nt + i, 0, 0))],
        out_shape=[jax.ShapeDtypeStruct((T, D), F32),
                   jax.ShapeDtypeStruct((T, D), F32),
                   jax.ShapeDtypeStruct((8, T), I32),
                   jax.ShapeDtypeStruct((T, LANES), F32),
                   jax.ShapeDtypeStruct((T // tm, N_EXPERTS, LANES), F32)],
        scratch_shapes=[pltpu.VMEM((ATTN_W // LANES, tm, LANES), F32), pltpu.VMEM(w_out.shape, BF16)],
        compiler_params=pltpu.CompilerParams(dimension_semantics=("arbitrary", "arbitrary"),
                                             vmem_limit_bytes=48 * 2**20),
        name="mix_router",
    )(*outs, *lses, g, x2, w_out, n2, rwh, rwl, rb)


def _run_groups(c):
    return (c + (RUN_ALIGN - 1)) >> RUN_SHIFT


def _plan_body(tc_ref, rs_ref, p0_ref, pn_ref, be_ref, hv_ref, nu_ref, *, nt, nblk):
    shift = MOE_BLOCK.bit_length() - 1
    start = jnp.int32(0)
    for ex in range(N_EXPERTS):
        rows_of = lambda i, ex=ex: _run_groups(tc_ref[i, ex]) << RUN_SHIFT
        tot = lax.fori_loop(0, nt, lambda i, tot: tot + rows_of(i), jnp.int32(0))
        nb = (tot + (MOE_BLOCK - 1)) >> shift
        pad = (nb << shift) - tot
        p0_ref[ex] = start
        pn_ref[ex] = pad >> RUN_SHIFT

        def run(i, at, ex=ex):
            rs_ref[i, ex] = at
            return at + rows_of(i)

        lax.fori_loop(0, nt, run, start + pad)
        b0 = start >> shift

        def fill(j, carry, b0=b0, ex=ex):
            be_ref[b0 + j] = jnp.int32(ex)
            hv_ref[b0 + j] = jnp.int32(MOE_BLOCK // MOE_PART)
            return carry

        lax.fori_loop(0, nb, fill, 0)

        @pl.when(nb > 0)
        def _(b0=b0, pad=pad):
            hv_ref[b0] = (MOE_BLOCK - pad + (MOE_PART - 1)) >> (MOE_PART.bit_length() - 1)

        start = start + (nb << shift)
    used = start >> shift

    def tail(j, carry):
        be_ref[j] = jnp.int32(N_EXPERTS - 1)
        hv_ref[j] = jnp.int32(0)
        return carry

    lax.fori_loop(used, nblk, tail, 0)
    nu_ref[0] = used


def _plan(tcnt, nblk):
    nt = tcnt.shape[0]
    smem = pl.BlockSpec(memory_space=pltpu.SMEM)
    return pl.pallas_call(
        functools.partial(_plan_body, nt=nt, nblk=nblk),
        in_specs=[smem],
        out_specs=[smem] * 6,
        out_shape=[jax.ShapeDtypeStruct(tcnt.shape, I32),
                   jax.ShapeDtypeStruct((N_EXPERTS,), I32),
                   jax.ShapeDtypeStruct((N_EXPERTS,), I32),
                   jax.ShapeDtypeStruct((nblk,), I32),
                   jax.ShapeDtypeStruct((nblk,), I32),
                   jax.ShapeDtypeStruct((1,), I32)],
        name="plan",
    )(tcnt)


def _for_each_run(tc_ref, rs_ref, i, fn):
    big = 4 * RUN_ALIGN
    start = jnp.int32(0)
    for ex in range(N_EXPERTS):
        ng = _run_groups(tc_ref[i, ex])
        g0 = rs_ref[i, ex]

        def piece(q, carry, start=start, g0=g0):
            fn(pl.multiple_of(start + q * big, RUN_ALIGN), pl.multiple_of(g0 + q * big, RUN_ALIGN), big)
            return carry

        def single(q, carry, start=start, g0=g0):
            fn(pl.multiple_of(start + q * RUN_ALIGN, RUN_ALIGN), pl.multiple_of(g0 + q * RUN_ALIGN, RUN_ALIGN),
               RUN_ALIGN)
            return carry

        lax.fori_loop(0, ng >> 2, piece, 0)
        lax.fori_loop((ng >> 2) << 2, ng, single, 0)
        start = start + (ng << RUN_SHIFT)


def _group_count(tc_ref, i):
    n = jnp.int32(0)
    for ex in range(N_EXPERTS):
        n = n + _run_groups(tc_ref[i, ex])
    return n


def _wait_groups(n, src_ref, dst_ref, sem):
    def waits(count, rows):
        def body(q, carry):
            pltpu.make_async_copy(src_ref.at[pl.ds(0, rows)], dst_ref.at[pl.ds(0, rows)], sem).wait()
            return carry
        lax.fori_loop(0, count, body, 0)

    waits(n >> 6, 64 * RUN_ALIGN)
    waits((n >> 3) & 7, 8 * RUN_ALIGN)
    waits(n & 7, RUN_ALIGN)


def _dispatch_body(tc_ref, rs_ref, p0_ref, pn_ref, nu_ref, m_ref, h_ref, xin_ref, g_ref, z_ref, sem, zsem, *, tm, R):
    i = pl.program_id(0)
    slot = i % 2
    rows = m_ref[...].astype(jnp.int16)
    jj = lax.broadcasted_iota(I32, (R, tm), 0).astype(jnp.int16)
    hit = (jj == rows[0:1]) | (jj == rows[1:2]) | (jj == rows[2:3]) | (jj == rows[3:4])
    onehot = jnp.where(hit, jnp.ones((), BF16), jnp.zeros((), BF16))
    g_ref[slot] = jnp.dot(onehot, h_ref[...].astype(BF16), preferred_element_type=F32).astype(BF16)

    group = lambda ref, r: ref.at[pl.ds(r, RUN_ALIGN)]

    @pl.when(i > 0)
    def _():
        _wait_groups(_group_count(tc_ref, i - 1), g_ref.at[1 - slot], xin_ref, sem.at[1 - slot])

    _for_each_run(tc_ref, rs_ref, i, lambda lr, gr, n: pltpu.make_async_copy(
        g_ref.at[slot, pl.ds(lr, n)], xin_ref.at[pl.ds(gr, n)], sem.at[slot]).start())

    @pl.when(i == pl.num_programs(0) - 1)
    def _():
        _wait_groups(_group_count(tc_ref, i), g_ref.at[slot], xin_ref, sem.at[slot])

    @pl.when(i == 0)
    def _():
        z_ref[...] = jnp.zeros_like(z_ref)
        total = jnp.int32(0)
        for ex in range(N_EXPERTS):
            def body(q, carry, ex=ex):
                pltpu.make_async_copy(group(z_ref, 0), group(xin_ref, pl.multiple_of(p0_ref[ex] + q * RUN_ALIGN, RUN_ALIGN)), zsem).start()
                return carry
            lax.fori_loop(0, pn_ref[ex], body, 0)
            total = total + pn_ref[ex]
        _wait_groups(total, z_ref, xin_ref, zsem)

        block = lambda j: xin_ref.at[pl.ds(pl.multiple_of(j * MOE_BLOCK, MOE_BLOCK), MOE_BLOCK)]
        n_blocks = xin_ref.shape[0] // MOE_BLOCK

        def fill(j, carry):
            pltpu.make_async_copy(z_ref, block(j), zsem).start()
            return carry

        def fill_wait(j, carry):
            pltpu.make_async_copy(z_ref, block(j), zsem).wait()
            return carry

        lax.fori_loop(nu_ref[0], n_blocks, fill, 0)
        lax.fori_loop(nu_ref[0], n_blocks, fill_wait, 0)


def _dispatch(tcnt, rowstart, pad0, padn, n_used, m_t, h, n_rows):
    T, D = h.shape
    tm = ROUTE_TILE
    col = pl.BlockSpec((8, tm), lambda i, *_: (0, i))
    return pl.pallas_call(
        functools.partial(_dispatch_body, tm=tm, R=GROUP_ROWS),
        grid_spec=pltpu.PrefetchScalarGridSpec(
            num_scalar_prefetch=5,
            grid=(T // tm,),
            in_specs=[col, pl.BlockSpec((tm, D), lambda i, *_: (i, 0))],
            out_specs=pl.BlockSpec(memory_space=pl.ANY),
            scratch_shapes=[pltpu.VMEM((2, GROUP_ROWS, D), BF16), pltpu.VMEM((MOE_BLOCK, D), BF16),
                            pltpu.SemaphoreType.DMA((2,)), pltpu.SemaphoreType.DMA(())]),
        out_shape=jax.ShapeDtypeStruct((n_rows, D), BF16),
        compiler_params=pltpu.CompilerParams(dimension_semantics=("arbitrary",), vmem_limit_bytes=56 * 2**20),
        name="dispatch",
    )(tcnt, rowstart, pad0, padn, n_used, m_t, h)


def _moe_body(be_ref, nu_ref, hv_ref, x_ref, w1g_ref, w1l_ref, b1g_ref, b1l_ref, w2_ref, b2_ref, y_ref):
    del be_ref
    j = pl.program_id(0)
    live = j < nu_ref[0]

    def expert_mlp(rows):
        x = x_ref[rows, :]
        hg = lax.dot_general(x, w1g_ref[0], NT_DIMS, preferred_element_type=F32) + b1g_ref[0]
        hl = lax.dot_general(x, w1l_ref[0], NT_DIMS, preferred_element_type=F32) + b1l_ref[0]
        xg = jnp.minimum(hg, SWIGLU_LIMIT)
        xl = jnp.clip(hl, -SWIGLU_LIMIT, SWIGLU_LIMIT)
        act = xg * _sigmoid(SWIGLU_ALPHA * xg) * (xl + 1.0)
        y = jnp.dot(act.astype(BF16), w2_ref[0].astype(BF16), preferred_element_type=F32) + b2_ref[0]
        y_ref[rows, :] = y.astype(BF16)

    for parts in range(1, MOE_BLOCK // MOE_PART + 1):
        @pl.when(live & (hv_ref[j] == parts))
        def _(parts=parts):
            skip = MOE_BLOCK - parts * MOE_PART
            expert_mlp(slice(skip, MOE_BLOCK))
            if skip:
                y_ref[:skip, :] = jnp.zeros((skip, y_ref.shape[1]), y_ref.dtype)

    @pl.when(jnp.logical_not(live))
    def _():
        y_ref[...] = jnp.zeros_like(y_ref)


def _moe(block_e, n_used, half_block, xin, w1g, w1l, b1, w2, b2):
    n_rows = xin.shape[0]
    D = w2.shape[2]
    F = w2.shape[1]
    nblk = n_rows // MOE_BLOCK
    b1g = b1[:, 0::2].reshape(N_EXPERTS, 1, F)
    b1l = b1[:, 1::2].reshape(N_EXPERTS, 1, F)
    b2r = b2.reshape(N_EXPERTS, 1, D)
    xspec = pl.BlockSpec((MOE_BLOCK, D), lambda j, be, nu, hv: (jnp.minimum(j, nu[0] - 1), 0))
    yspec = pl.BlockSpec((MOE_BLOCK, D), lambda j, be, nu, hv: (j, 0))
    wspec = lambda a: pl.BlockSpec((1,) + a.shape[1:], lambda j, be, nu, hv: (be[j], 0, 0))
    return pl.pallas_call(
        _moe_body,
        grid_spec=pltpu.PrefetchScalarGridSpec(
            num_scalar_prefetch=3,
            grid=(nblk,),
            in_specs=[xspec, wspec(w1g), wspec(w1l), wspec(b1g), wspec(b1l), wspec(w2), wspec(b2r)],
            out_specs=yspec),
        out_shape=jax.ShapeDtypeStruct(xin.shape, BF16),
        compiler_params=pltpu.CompilerParams(dimension_semantics=("arbitrary",),
                                             vmem_limit_bytes=48 * 2**20),
        name="moe",
    )(block_e, n_used, half_block, xin, w1g, w1l, b1g, b1l, w2, b2r)


def _combine_body(tc_ref, rs_ref, slab_ref, y_ref, x1_ref, fg_ref, out_ref, g_ref, sem, *, tm, R):
    i = pl.program_id(0)
    slot = i % 2

    def fetch(tile, slot):
        _for_each_run(tc_ref, rs_ref, tile, lambda lr, gr, n: pltpu.make_async_copy(
            y_ref.at[pl.ds(gr, n)], g_ref.at[slot, pl.ds(lr, n)], sem.at[slot]).start())

    @pl.when(i == 0)
    def _():
        g_ref[...] = jnp.zeros_like(g_ref)
        fetch(i, slot)

    @pl.when(i + 1 < pl.num_programs(0))
    def _():
        fetch(i + 1, 1 - slot)

    slab = slab_ref[...]
    gates = slab[:, 0:TOP_K].astype(BF16)
    rows = slab[:, TOP_K:2 * TOP_K].astype(I32).astype(jnp.int16)
    jj = lax.broadcasted_iota(I32, (tm, R), 1).astype(jnp.int16)
    sel = jnp.zeros((tm, R), BF16)
    for k in range(TOP_K):
        sel = jnp.where(jj == rows[:, k:k + 1], gates[:, k:k + 1], sel)

    _wait_groups(_group_count(tc_ref, i), y_ref, g_ref.at[slot], sem.at[slot])
    acc = x1_ref[...] + jnp.dot(sel, g_ref[slot], preferred_element_type=F32)
    out_ref[...] = acc * lax.rsqrt(jnp.mean(acc * acc, axis=-1, keepdims=True) + NORM_EPS) * fg_ref[...]


def _combine(tcnt, rowstart, slab, y, x1, final_g):
    T, D = x1.shape
    tm = ROUTE_TILE
    fg = final_g.reshape(1, D)
    return pl.pallas_call(
        functools.partial(_combine_body, tm=tm, R=GROUP_ROWS),
        grid_spec=pltpu.PrefetchScalarGridSpec(
            num_scalar_prefetch=2,
            grid=(T // tm,),
            in_specs=[pl.BlockSpec((tm, LANES), lambda i, *_: (i, 0)),
                      pl.BlockSpec(memory_space=pl.ANY),
                      pl.BlockSpec((tm, D), lambda i, *_: (i, 0)),
                      pl.BlockSpec((1, D), lambda i, *_: (0, 0))],
            out_specs=pl.BlockSpec((tm, D), lambda i, *_: (i, 0)),
            scratch_shapes=[pltpu.VMEM((2, GROUP_ROWS, D), BF16), pltpu.SemaphoreType.DMA((2,))]),
        out_shape=jax.ShapeDtypeStruct((T, D), F32),
        compiler_params=pltpu.CompilerParams(dimension_semantics=("arbitrary",), vmem_limit_bytes=56 * 2**20),
        name="combine",
    )(tcnt, rowstart, slab, y, x1, fg)


def _mixers(x2, B, S, norm1_g, w_in, rel_bias, wg_f, bg_f, wg_b, bg_b, gla_norm_g, w1):
    a1, a4, a16, gla, bcum, w1g, w1l = _in_proj(x2, B, S, norm1_g, w_in, wg_f, bg_f, wg_b, bg_b, w1)
    branches = [_attn_branch(a, rel_bias, d) for a, d in zip((a1, a4, a16), DILATIONS)]
    g = _gla(gla, bcum, gla_norm_g, B, S)
    return [o for o, _ in branches], [l for _, l in branches], g, w1g, w1l


def _moe_layer(h, m_t, slab, tcnt_f, x1, w1g, w1l, b1, w2, b2, final_g):
    T = x1.shape[0]
    nt = T // ROUTE_TILE
    nblk = (T * TOP_K + nt * N_EXPERTS * (RUN_ALIGN - 1)) // MOE_BLOCK + N_EXPERTS
    tcnt = tcnt_f[:, :, 0].astype(I32)
    rowstart, pad0, padn, block_e, half_block, n_used = _plan(tcnt, nblk)
    xin = _dispatch(tcnt, rowstart, pad0, padn, n_used, m_t, h, nblk * MOE_BLOCK)
    y = _moe(block_e, n_used, half_block, xin, w1g, w1l, b1, w2, b2)
    return _combine(tcnt, rowstart, slab, y, x1, final_g)


def kernel(x, norm1_g, w_in, rel_bias, gla_wg_fwd, gla_bg_fwd, gla_wg_bwd, gla_bg_bwd, gla_norm_g, w_out, norm2_g, router_w, router_b, moe_w1, moe_b1, moe_w2, moe_b2, final_g):
    B, S, D = x.shape
    assert w_in.shape[0] == 1, "one layer"
    x2 = x.reshape(B * S, D)
    outs, lses, g, w1g, w1l = _mixers(x2, B, S, norm1_g[0], w_in[0], rel_bias, gla_wg_fwd[0], gla_bg_fwd[0],
                                      gla_wg_bwd[0], gla_bg_bwd[0], gla_norm_g[0], moe_w1[0])
    x1, h, m_t, slab, tcnt = _mix(outs, lses, g, x2, B, S, w_out[0], norm2_g[0], router_w[0], router_b[0])
    out = _moe_layer(h, m_t, slab, tcnt, x1, w1g, w1l, moe_b1[0], moe_w2[0], moe_b2[0], final_g)
    return out.reshape(B, S, D)
```

```python
import functools

import jax
import jax.numpy as jnp
import numpy as np
from jax import lax
from jax.experimental import pallas as pl
from jax.experimental.pallas import tpu as pltpu

F32, BF16, I32 = jnp.float32, jnp.bfloat16, jnp.int32

LANES = 128
HEAD_DIM = 64
ATTN_HEADS = 8
ATTN_W = ATTN_HEADS * HEAD_DIM
HALF_SPAN = 64
ATTN_QBLK = 128
ATTN_KWIN = 256
ATTN_STEP_ROWS = 512
DILATIONS = (1, 4, 16)
T5_BUCKETS = 32
T5_MAX_DISTANCE = 1024
NEG_INF = -1e30
LOG2_E = 1.4426950408889634
LN_2 = 0.6931471805599453
GLA_KEY_W = 256
GLA_VAL_W = 512
GLA_DV = 128
GLA_RANK = 16
GLA_TAU = 16.0
GLA_CHUNK = 64
N_EXPERTS = 32
TOP_K = 4
SWIGLU_LIMIT = 7.0
SWIGLU_ALPHA = 1.702
NORM_EPS = 1e-5
MOE_BLOCK = 1024
MOE_PART = 256
W1_SUBTILE = 512
ROUTE_TILE = 512
RUN_ALIGN = 16
RUN_SHIFT = 4
GROUP_ROWS = ROUTE_TILE * TOP_K + N_EXPERTS * RUN_ALIGN

ATTN_COLS = 3 * ATTN_W
GLA_COLS = 2 * GLA_KEY_W + 2 * GLA_VAL_W
Z_COLS = LANES
PROJ_COLS = ATTN_COLS + GLA_COLS + Z_COLS

NT_DIMS = (((1,), (1,)), ((), ()))


def _sigmoid(x):
    return 1.0 / (1.0 + jnp.exp(-x))


def _split3(a):
    h1 = a.astype(BF16)
    r1 = a - h1.astype(F32)
    h2 = r1.astype(BF16)
    h3 = (r1 - h2.astype(F32)).astype(BF16)
    return h1, h2, h3


def _dot_f32(a, b):
    ah, al, _ = _split3(a)
    bh, bl, _ = _split3(b)
    d = lambda u, v: jnp.dot(u, v, preferred_element_type=F32)
    return d(ah, bh) + d(al, bh) + d(ah, bl)


def _split_w1_tile(w_ref, g_ref, l_ref, t_ref):
    sub = t_ref.shape[1]
    for s in range(w_ref.shape[2] // sub):
        wt = w_ref[0, :, s * sub:(s + 1) * sub].T
        out_rows = slice(s * sub // 2, (s + 1) * sub // 2)
        for c in range(wt.shape[1] // LANES):
            cols = slice(c * LANES, (c + 1) * LANES)
            t_ref[c] = wt[:, cols]
            g_ref[0, out_rows, cols] = t_ref[c, pl.ds(0, sub // 2, stride=2), :].astype(BF16)
            l_ref[0, out_rows, cols] = t_ref[c, pl.ds(1, sub // 2, stride=2), :].astype(BF16)


def _in_proj_body(x_ref, g_ref, w_ref, wg_ref, bg_ref, w1_ref, a1_ref, a4_ref, a16_ref, gla_ref, b_ref,
                  w1g_ref, w1l_ref, t_ref, wb_ref, *, tm):
    @pl.when((pl.program_id(0) == 0) & (pl.program_id(1) == 0))
    def _():
        wb_ref[...] = jnp.zeros_like(wb_ref)
        wb_ref[:, :w_ref.shape[1]] = w_ref[...].astype(BF16)

    _split_w1_tile(w1_ref, w1g_ref, w1l_ref, t_ref)

    x = x_ref[...]
    n = x * lax.rsqrt(jnp.mean(x * x, axis=-1, keepdims=True) + NORM_EPS) * g_ref[...]
    p = jnp.dot(n.astype(BF16), wb_ref[...], preferred_element_type=F32)
    gla_ref[...] = p[:, ATTN_COLS:ATTN_COLS + GLA_COLS].astype(BF16)

    qkv = p[:, :ATTN_COLS].astype(BF16)
    a1_ref[0, 0] = qkv
    dst = lax.broadcasted_iota(I32, (tm, tm), 0)
    src = lax.broadcasted_iota(I32, (tm, tm), 1)
    for dil, a_ref in ((DILATIONS[1], a4_ref), (DILATIONS[2], a16_ref)):
        per = tm // dil
        perm = (src == (dst % per) * dil + dst // per).astype(F32).astype(BF16)
        moved = jnp.dot(perm, qkv, preferred_element_type=F32).astype(BF16)
        for r in range(dil):
            a_ref[0, r] = moved[r * per:(r + 1) * per]

    z = p[:, ATTN_COLS + GLA_COLS:]
    zz = _dot_f32(z, wg_ref[...]) + bg_ref[...]
    la = (jnp.minimum(zz, 0.0) - jnp.log1p(jnp.exp(-jnp.abs(zz)))) * (1.0 / GLA_TAU)
    C = GLA_CHUNK
    r_i = lax.broadcasted_iota(I32, (C, C), 0)
    c_i = lax.broadcasted_iota(I32, (C, C), 1)
    lower = (r_i >= c_i).astype(BF16)
    upper = (r_i <= c_i).astype(BF16)
    W = GLA_KEY_W
    for j in range(tm // C):
        rows = slice(j * C, (j + 1) * C)
        for tri, cols in ((lower, slice(0, W)), (upper, slice(W, 2 * W))):
            terms = jnp.concatenate(_split3(la[rows, cols]), axis=1)
            cs = jnp.dot(tri, terms, preferred_element_type=F32)
            b_ref[rows, cols] = cs[:, :W] + cs[:, W:2 * W] + cs[:, 2 * W:]


def _in_proj(x2, B, S, norm_g, w_in, wg_f, bg_f, wg_b, bg_b, w1):
    T, D = x2.shape
    tm = 256
    nt = S // tm
    E, _, F2 = w1.shape
    w1_cols = E * F2 // (B * nt)
    per_e = F2 // w1_cols
    assert w1_cols * B * nt == E * F2 and per_e * w1_cols == F2 and w1_cols % W1_SUBTILE == 0
    w1_in = pl.BlockSpec((1, D, w1_cols), lambda b, i: ((b * nt + i) // per_e, 0, (b * nt + i) % per_e))
    w1_out = pl.BlockSpec((1, w1_cols // 2, D), lambda b, i: ((b * nt + i) // per_e, (b * nt + i) % per_e, 0))
    w_spec = pl.BlockSpec((None,) + w_in.shape[1:], lambda b, i: (0, 0, 0), pipeline_mode=pl.Buffered(1))
    wg = jnp.zeros((Z_COLS, 2 * GLA_KEY_W), F32)
    wg = wg.at[:GLA_RANK, :GLA_KEY_W].set(wg_f).at[GLA_RANK:2 * GLA_RANK, GLA_KEY_W:].set(wg_b)
    bg = jnp.concatenate([bg_f, bg_b]).reshape(1, 2 * GLA_KEY_W)
    row = lambda c: pl.BlockSpec((tm, c), lambda b, i: (b * nt + i, 0))
    full = lambda a: pl.BlockSpec(a.shape, lambda b, i: (0,) * a.ndim)
    res = lambda d: pl.BlockSpec((1, d, tm // d, ATTN_COLS), lambda b, i: (b, 0, i, 0))
    res_shape = lambda d: jax.ShapeDtypeStruct((B, d, S // d, ATTN_COLS), BF16)
    g2 = norm_g.reshape(1, D)
    return pl.pallas_call(
        functools.partial(_in_proj_body, tm=tm),
        grid=(B, nt),
        in_specs=[row(D), full(g2), w_spec, full(wg), full(bg), w1_in],
        out_specs=[res(d) for d in DILATIONS] + [row(GLA_COLS), row(2 * GLA_KEY_W), w1_out, w1_out],
        out_shape=[res_shape(d) for d in DILATIONS]
                  + [jax.ShapeDtypeStruct((T, GLA_COLS), BF16), jax.ShapeDtypeStruct((T, 2 * GLA_KEY_W), F32)]
                  + [jax.ShapeDtypeStruct((E, F2 // 2, D), BF16)] * 2,
        scratch_shapes=[pltpu.VMEM((D // LANES, W1_SUBTILE, LANES), F32), pltpu.VMEM((D, PROJ_COLS), BF16)],
        compiler_params=pltpu.CompilerParams(dimension_semantics=("arbitrary", "arbitrary"),
                                             vmem_limit_bytes=56 * 2**20),
        name="in_proj",
    )(x2, g2, w_in, wg, bg, w1)


def _t5_bucket(rel):
    nb = T5_BUCKETS // 2
    max_exact = nb // 2
    n = np.abs(rel)
    large = max_exact + (np.log(np.maximum(n, 1).astype(np.float32) / max_exact)
                         / np.log(T5_MAX_DISTANCE / max_exact) * (nb - max_exact)).astype(np.int32)
    large = np.minimum(large, nb - 1)
    return (np.where(rel > 0, nb, 0) + np.where(n < max_exact, n, large)).astype(np.int32)


def _bucket_tables(dil):
    s = np.arange(ATTN_QBLK)[:, None]
    t = np.arange(ATTN_KWIN)[None, :]
    tabs = []
    for c in (0, HALF_SPAN, 2 * HALF_SPAN):
        off = t - c - s
        tabs.append(np.where(np.abs(off) <= HALF_SPAN, _t5_bucket(off * dil), -1))
    return np.stack(tabs).astype(np.int32)


def _attn_body(rb_ref, idx_ref, q_ref, k_ref, v_ref, o_ref, lse_ref, tab_ref, *, L, TQ):
    first = (pl.program_id(0) == 0) & (pl.program_id(1) == 0) & (pl.program_id(2) == 0)

    @pl.when(first)
    def _():
        for v in range(3):
            idx = idx_ref[v]
            for h in range(ATTN_HEADS):
                fill = lambda j, t, h=h, idx=idx: jnp.where(idx == j, rb_ref[j, h] * LOG2_E, t)
                tab_ref[v, h] = lax.fori_loop(0, T5_BUCKETS, fill, jnp.where(idx < 0, NEG_INF, 0.0).astype(F32))

    i = pl.program_id(2)
    lo = lax.broadcasted_iota(I32, (ATTN_QBLK, LANES), 1) < HEAD_DIM
    scale = HEAD_DIM ** -0.5 * LOG2_E
    lane = lax.broadcasted_iota(I32, (ATTN_QBLK, LANES), 1)
    for rb, sb in [(rb, sb) for rb in range(q_ref.shape[0]) for sb in range(TQ // ATTN_QBLK)]:
        l0 = i * TQ + sb * ATTN_QBLK
        lse_tile = jnp.zeros((ATTN_QBLK, LANES), F32)
        ws = pl.multiple_of(jnp.clip(l0 - HALF_SPAN, 0, L - ATTN_KWIN), HALF_SPAN)
        var = jnp.where(l0 == 0, 0, jnp.where(l0 == L - ATTN_QBLK, 2, 1))
        rows = slice(sb * ATTN_QBLK, (sb + 1) * ATTN_QBLK)
        for hp in range(ATTN_HEADS // 2):
            cols = slice(hp * LANES, (hp + 1) * LANES)
            q2 = q_ref[rb, rows, cols]
            k2 = k_ref[rb, pl.ds(ws, ATTN_KWIN), cols]
            v2 = v_ref[rb, pl.ds(ws, ATTN_KWIN), cols]
            zero = jnp.zeros_like(q2)
            qs = jnp.concatenate([jnp.where(lo, q2, zero), jnp.where(lo, zero, q2)], axis=0)
            s = lax.dot_general(qs, k2, NT_DIMS, preferred_element_type=F32)
            bias = jnp.concatenate([tab_ref[var, 2 * hp], tab_ref[var, 2 * hp + 1]], axis=0)
            logits = s * scale + bias
            m = jnp.max(logits, axis=-1, keepdims=True)
            p = jnp.exp2(logits - m)
            den = jnp.sum(p, axis=-1, keepdims=True)
            pv = jnp.dot(p.astype(BF16), v2, preferred_element_type=F32) / den
            lse = m * LN_2 + jnp.log(den)
            o_ref[rb, rows, cols] = jnp.where(lo, pv[:ATTN_QBLK], pv[ATTN_QBLK:]).astype(BF16)
            lse_tile = jnp.where(lane == 2 * hp, lse[:ATTN_QBLK], lse_tile)
            lse_tile = jnp.where(lane == 2 * hp + 1, lse[ATTN_QBLK:], lse_tile)
        lse_ref[rb, rows, :] = lse_tile


def _attn_branch(attn, rel_bias, dil):
    B, _, L, _ = attn.shape
    TQ = min(ATTN_STEP_ROWS, L)
    RB = ATTN_STEP_ROWS // TQ
    idx = jnp.asarray(_bucket_tables(dil))
    body = functools.partial(_attn_body, L=L, TQ=TQ)
    kv = lambda j: pl.BlockSpec((None, RB, L, ATTN_W), lambda b, r, i: (b, r, 0, j))
    q_spec = pl.BlockSpec((None, RB, TQ, ATTN_W), lambda b, r, i: (b, r, i, 0))
    return pl.pallas_call(
        body,
        grid=(B, dil // RB, L // TQ),
        in_specs=[pl.BlockSpec(memory_space=pltpu.SMEM),
                  pl.BlockSpec(idx.shape, lambda b, r, i: (0, 0, 0)),
                  q_spec, kv(1), kv(2)],
        out_specs=[q_spec, pl.BlockSpec((None, RB, TQ, LANES), lambda b, r, i: (b, r, i, 0))],
        out_shape=[jax.ShapeDtypeStruct((B, dil, L, ATTN_W), BF16),
                   jax.ShapeDtypeStruct((B, dil, L, LANES), F32)],
        scratch_shapes=[pltpu.VMEM((3, ATTN_HEADS, ATTN_QBLK, ATTN_KWIN), F32)],
        compiler_params=pltpu.CompilerParams(dimension_semantics=("arbitrary",) * 3,
                                             vmem_limit_bytes=48 * 2**20),
        name=f"attn_d{dil}",
    )(rel_bias, idx, attn, attn, attn)


def _gla_body(q_ref, k_ref, v_ref, gate_ref, bf_ref, bb_ref, ng_ref, out_ref, of_ref, ob_ref, st_ref, *, S):
    C = GLA_CHUNK
    nc = S // C
    r = lax.broadcasted_iota(I32, (C, LANES), 0)
    s2 = lax.broadcasted_iota(I32, (C, LANES), 1) % HEAD_DIM
    lo = lax.broadcasted_iota(I32, (C, LANES), 1) < HEAD_DIM
    blockdiag = ((lax.broadcasted_iota(I32, (2 * GLA_DV, LANES), 0) < GLA_DV)
                 == (lax.broadcasted_iota(I32, (2 * GLA_DV, LANES), 1) < HEAD_DIM))
    st_ref[...] = jnp.zeros_like(st_ref)

    def one_chunk(ci, direction):
        mask = (r >= s2) if direction == 0 else (r <= s2)
        b_ref, o_ref = (bf_ref, of_ref) if direction == 0 else (bb_ref, ob_ref)
        rows = pl.ds(pl.multiple_of(ci * C, C), C)
        q = q_ref[0, rows, :].astype(F32) * (HEAD_DIM ** -0.5)
        k = k_ref[0, rows, :].astype(F32)
        v = v_ref[0, rows, :]
        b = b_ref[0, rows, :]
        g_tot = b[C - 1:C, :] if direction == 0 else b[0:1, :]
        qd = (q * jnp.exp(b)).astype(BF16)
        kd = (k * jnp.exp(-b)).astype(BF16)
        kst = (k * jnp.exp(g_tot - b)).astype(BF16)
        zk = jnp.zeros_like(kd)
        ks = jnp.concatenate([jnp.where(lo, kd, zk), jnp.where(lo, zk, kd)], axis=0)
        a = lax.dot_general(qd, ks, NT_DIMS, preferred_element_type=F32)
        a = jnp.where(mask, a, 0.0).astype(BF16)
        zv = jnp.zeros((C, GLA_DV), BF16)
        vblk = jnp.concatenate([jnp.concatenate([v[:, :GLA_DV], zv], axis=1),
                                jnp.concatenate([zv, v[:, GLA_DV:]], axis=1)], axis=0)
        st = st_ref[direction]
        o_ref[rows, :] = (jnp.dot(a, vblk, preferred_element_type=F32)
                          + lax.dot_general(qd, st.astype(BF16), NT_DIMS, preferred_element_type=F32))
        d_st = jnp.dot(v.astype(F32).T.astype(BF16), kst, preferred_element_type=F32)
        st_ref[direction] = st * jnp.exp(g_tot) + jnp.where(blockdiag, d_st, 0.0)

    def step(i, carry):
        one_chunk(i, 0)
        one_chunk(nc - 1 - i, 1)
        return carry

    lax.fori_loop(0, nc, step, 0, unroll=8)

    tr = 256
    def epilogue(i, carry):
        rows = pl.ds(pl.multiple_of(i * tr, tr), tr)
        for hh in range(2):
            cols = slice(hh * GLA_DV, (hh + 1) * GLA_DV)
            o = of_ref[rows, cols] + ob_ref[rows, cols]
            o = o * lax.rsqrt(jnp.mean(o * o, axis=-1, keepdims=True) + NORM_EPS) * ng_ref[0, :, cols]
            gate = gate_ref[0, rows, cols].astype(F32)
            out_ref[0, rows, cols] = (o * (gate * _sigmoid(gate))).astype(BF16)
        return carry

    lax.fori_loop(0, S // tr, epilogue, 0)


def _gla(gla, bcum, norm_g, B, S):
    g3 = gla.reshape(B, S, GLA_COLS)
    b3 = bcum.reshape(B, S, 2 * GLA_KEY_W)
    ng = norm_g.reshape(2, 1, 2 * GLA_DV)
    body = functools.partial(_gla_body, S=S)
    spec = lambda w, off: pl.BlockSpec((1, S, w), lambda b, p: (b, 0, off + p))
    out = pl.pallas_call(
        body,
        grid=(B, 2),
        in_specs=[spec(LANES, 0),
                  spec(LANES, GLA_KEY_W // LANES),
                  spec(2 * GLA_DV, (2 * GLA_KEY_W) // (2 * GLA_DV)),
                  spec(2 * GLA_DV, (2 * GLA_KEY_W + GLA_VAL_W) // (2 * GLA_DV)),
                  spec(LANES, 0),
                  spec(LANES, GLA_KEY_W // LANES),
                  pl.BlockSpec((1, 1, 2 * GLA_DV), lambda b, p: (p, 0, 0))],
        out_specs=pl.BlockSpec((1, S, 2 * GLA_DV), lambda b, p: (b, 0, p)),
        out_shape=jax.ShapeDtypeStruct((B, S, GLA_VAL_W), BF16),
        scratch_shapes=[pltpu.VMEM((S, 2 * GLA_DV), F32), pltpu.VMEM((S, 2 * GLA_DV), F32),
                        pltpu.VMEM((2, 2 * GLA_DV, LANES), F32)],
        compiler_params=pltpu.CompilerParams(dimension_semantics=("arbitrary", "arbitrary"),
                                             vmem_limit_bytes=56 * 2**20),
        name="gla",
    )(g3, g3, g3, g3, b3, b3, ng)
    return out.reshape(B * S, GLA_VAL_W)


def _natural_rows(blk_ref, u_ref, dil, tm):
    if dil == 1:
        return blk_ref[0, 0].astype(F32)
    chunks = blk_ref.shape[-1] // LANES
    for r in range(dil):
        for c in range(chunks):
            u_ref[c, pl.ds(r, tm // dil, stride=dil), :] = blk_ref[0, r, :, c * LANES:(c + 1) * LANES].astype(F32)
    return jnp.concatenate([u_ref[c] for c in range(chunks)], axis=1)


def _per_head_lanes(w):
    head = lax.broadcasted_iota(I32, (LANES, ATTN_W), 0)
    col = lax.broadcasted_iota(I32, (LANES, ATTN_W), 1)
    spread = (col // HEAD_DIM == head).astype(F32).astype(BF16)
    hi, lo, _ = _split3(w)
    return (jnp.dot(hi, spread, preferred_element_type=F32) + jnp.dot(lo, spread, preferred_element_type=F32))


def _mix_body(o1, o2, o3, l1, l2, l3, g_ref, x_ref, wo_ref, n2_ref, rwh_ref, rwl_ref, rb_ref,
              x1_ref, h_ref, m_ref, slab_ref, tcnt_ref, u_ref, wob_ref, *, tm):
    @pl.when((pl.program_id(0) == 0) & (pl.program_id(1) == 0))
    def _():
        wob_ref[...] = wo_ref[...].astype(BF16)

    ls = [_natural_rows(l, u_ref, d, tm) for l, d in zip((l1, l2, l3), DILATIONS)]
    m = jnp.maximum(jnp.maximum(ls[0], ls[1]), ls[2])
    es = [jnp.exp(l - m) for l in ls]
    den = es[0] + es[1] + es[2]
    a = _per_head_lanes(es[0] / den) * _natural_rows(o1, u_ref, DILATIONS[0], tm)
    a = a + _per_head_lanes(es[1] / den) * _natural_rows(o2, u_ref, DILATIONS[1], tm)
    a = a + _per_head_lanes(es[2] / den) * _natural_rows(o3, u_ref, DILATIONS[2], tm)
    mix = (jnp.dot(a.astype(BF16), wob_ref[:ATTN_W, :], preferred_element_type=F32)
           + jnp.dot(g_ref[...], wob_ref[ATTN_W:, :], preferred_element_type=F32))
    x1 = x_ref[...] + mix
    x1_ref[...] = x1
    h = x1 * lax.rsqrt(jnp.mean(x1 * x1, axis=-1, keepdims=True) + NORM_EPS) * n2_ref[...]
    h_ref[...] = h

    hh, hl, _ = _split3(h)
    dg = lambda u, v: lax.dot_general(u, v, NT_DIMS, preferred_element_type=F32)
    logits = dg(rwh_ref[...], hh) + dg(rwl_ref[...], hh) + dg(rwh_ref[...], hl) + rb_ref[...]
    ie = lax.broadcasted_iota(I32, (N_EXPERTS, tm), 0)
    cur = logits
    vals, idxs = [], []
    for _ in range(TOP_K):
        mx = jnp.max(cur, axis=0, keepdims=True)
        ix = jnp.min(jnp.where(cur == mx, ie, N_EXPERTS), axis=0, keepdims=True)
        vals.append(mx)
        idxs.append(ix)
        cur = jnp.where(ie == ix, -jnp.inf, cur)
    ex = [jnp.exp(v - vals[0]) for v in vals]
    den = ex[0] + ex[1] + ex[2] + ex[3]
    gates = [e / den for e in ex]

    onehots = [ie == ix for ix in idxs]
    chosen = (onehots[0] | onehots[1] | onehots[2] | onehots[3]).astype(F32)
    tr = lax.broadcasted_iota(I32, (tm, tm), 0)
    tc = lax.broadcasted_iota(I32, (tm, tm), 1)
    before = jnp.dot(chosen.astype(BF16), (tr < tc).astype(BF16), preferred_element_type=F32)
    count = jnp.broadcast_to(jnp.sum(chosen, axis=1, keepdims=True), (N_EXPERTS, LANES))
    tcnt_ref[0] = count
    run = jnp.floor((count + (RUN_ALIGN - 1.0)) * (1.0 / RUN_ALIGN)) * RUN_ALIGN
    er = lax.broadcasted_iota(I32, (N_EXPERTS, N_EXPERTS), 0)
    ec = lax.broadcasted_iota(I32, (N_EXPERTS, N_EXPERTS), 1)
    run_start = jnp.dot((ec < er).astype(BF16), run.astype(BF16), preferred_element_type=F32)
    where_ = before + run_start[:, 0:1]
    locs = [jnp.sum(jnp.where(oh, where_, 0.0), axis=0, keepdims=True) for oh in onehots]

    def pack(rows, n):
        sub = lax.broadcasted_iota(I32, (n, tm), 0)
        out = jnp.zeros((n, tm), rows[0].dtype)
        for k, row in enumerate(rows):
            out = jnp.where(sub == k, row, out)
        return out

    m_ref[...] = pack([l.astype(I32) for l in locs], 8)
    slab = pack(gates + locs, 8)
    slab_ref[...] = jnp.concatenate([slab, jnp.zeros((LANES - 8, tm), F32)], axis=0).T


def _mix(outs, lses, g, x2, B, S, w_out, norm2_g, router_w, router_b):
    T, D = x2.shape
    tm = ROUTE_TILE
    nt = S // tm
    wo_spec = pl.BlockSpec(w_out.shape, lambda b, i: (0, 0), pipeline_mode=pl.Buffered(1))
    rwt = router_w.T
    rwh = rwt.astype(BF16)
    rwl = (rwt - rwh.astype(F32)).astype(BF16)
    rb = router_b.reshape(N_EXPERTS, 1)
    n2 = norm2_g.reshape(1, D)
    row = lambda c: pl.BlockSpec((tm, c), lambda b, i: (b * nt + i, 0))
    full = lambda a: pl.BlockSpec(a.shape, lambda b, i: (0,) * a.ndim)
    col = pl.BlockSpec((8, tm), lambda b, i: (0, b * nt + i))
    res = [pl.BlockSpec((1, d, tm // d, ATTN_W), lambda b, i: (b, 0, i, 0)) for d in DILATIONS]
    res_lse = [pl.BlockSpec((1, d, tm // d, LANES), lambda b, i: (b, 0, i, 0)) for d in DILATIONS]
    return pl.pallas_call(
        functools.partial(_mix_body, tm=tm),
        grid=(B, nt),
        in_specs=res + res_lse + [row(GLA_VAL_W), row(D), wo_spec, full(n2), full(rwh), full(rwl), full(rb)],
        out_specs=[row(D), row(D), col, row(LANES),
                   pl.BlockSpec((1, N_EXPERTS, LANES), lambda b, i: (b * nt + i, 0, 0))],
        out_shape=[jax.ShapeDtypeStruct((T, D), F32),
                   jax.ShapeDtypeStruct((T, D), F32),
                   jax.ShapeDtypeStruct((8, T), I32),
                   jax.ShapeDtypeStruct((T, LANES), F32),
                   jax.ShapeDtypeStruct((T // tm, N_EXPERTS, LANES), F32)],
        scratch_shapes=[pltpu.VMEM((ATTN_W // LANES, tm, LANES), F32), pltpu.VMEM(w_out.shape, BF16)],
        compiler_params=pltpu.CompilerParams(dimension_semantics=("arbitrary", "arbitrary"),
                                             vmem_limit_bytes=48 * 2**20),
        name="mix_router",
    )(*outs, *lses, g, x2, w_out, n2, rwh, rwl, rb)


def _run_groups(c):
    return (c + (RUN_ALIGN - 1)) >> RUN_SHIFT


def _plan_body(tc_ref, rs_ref, p0_ref, pn_ref, be_ref, hv_ref, nu_ref, *, nt, nblk):
    shift = MOE_BLOCK.bit_length() - 1
    start = jnp.int32(0)
    for ex in range(N_EXPERTS):
        rows_of = lambda i, ex=ex: _run_groups(tc_ref[i, ex]) << RUN_SHIFT
        tot = lax.fori_loop(0, nt, lambda i, tot: tot + rows_of(i), jnp.int32(0), unroll=True)
        nb = (tot + (MOE_BLOCK - 1)) >> shift
        pad = (nb << shift) - tot
        p0_ref[ex] = start
        pn_ref[ex] = pad >> RUN_SHIFT

        def run(i, at, ex=ex):
            rs_ref[i, ex] = at
            return at + rows_of(i)

        lax.fori_loop(0, nt, run, start + pad, unroll=True)
        b0 = start >> shift

        def fill(j, carry, b0=b0, ex=ex):
            be_ref[b0 + j] = jnp.int32(ex)
            hv_ref[b0 + j] = jnp.int32(MOE_BLOCK // MOE_PART)
            return carry

        lax.fori_loop(0, nb, fill, 0)

        @pl.when(nb > 0)
        def _(b0=b0, pad=pad):
            hv_ref[b0] = (MOE_BLOCK - pad + (MOE_PART - 1)) >> (MOE_PART.bit_length() - 1)

        start = start + (nb << shift)
    used = start >> shift

    def tail(j, carry):
        be_ref[j] = jnp.int32(N_EXPERTS - 1)
        hv_ref[j] = jnp.int32(0)
        return carry

    lax.fori_loop(used, nblk, tail, 0)
    nu_ref[0] = used


def _plan(tcnt, nblk):
    nt = tcnt.shape[0]
    smem = pl.BlockSpec(memory_space=pltpu.SMEM)
    return pl.pallas_call(
        functools.partial(_plan_body, nt=nt, nblk=nblk),
        in_specs=[smem],
        out_specs=[smem] * 6,
        out_shape=[jax.ShapeDtypeStruct(tcnt.shape, I32),
                   jax.ShapeDtypeStruct((N_EXPERTS,), I32),
                   jax.ShapeDtypeStruct((N_EXPERTS,), I32),
                   jax.ShapeDtypeStruct((nblk,), I32),
                   jax.ShapeDtypeStruct((nblk,), I32),
                   jax.ShapeDtypeStruct((1,), I32)],
        name="plan",
    )(tcnt)


def _for_each_run(tc_ref, rs_ref, i, fn):
    big = 4 * RUN_ALIGN
    start = jnp.int32(0)
    for ex in range(N_EXPERTS):
        ng = _run_groups(tc_ref[i, ex])
        g0 = rs_ref[i, ex]

        def piece(q, carry, start=start, g0=g0):
            fn(pl.multiple_of(start + q * big, RUN_ALIGN), pl.multiple_of(g0 + q * big, RUN_ALIGN), big)
            return carry

        def single(q, carry, start=start, g0=g0):
            fn(pl.multiple_of(start + q * RUN_ALIGN, RUN_ALIGN), pl.multiple_of(g0 + q * RUN_ALIGN, RUN_ALIGN),
               RUN_ALIGN)
            return carry

        lax.fori_loop(0, ng >> 2, piece, 0)
        lax.fori_loop((ng >> 2) << 2, ng, single, 0)
        start = start + (ng << RUN_SHIFT)


def _group_count(tc_ref, i):
    n = jnp.int32(0)
    for ex in range(N_EXPERTS):
        n = n + _run_groups(tc_ref[i, ex])
    return n


def _wait_groups(n, src_ref, dst_ref, sem):
    def waits(count, rows):
        def body(q, carry):
            pltpu.make_async_copy(src_ref.at[pl.ds(0, rows)], dst_ref.at[pl.ds(0, rows)], sem).wait()
            return carry
        lax.fori_loop(0, count, body, 0)

    waits(n >> 6, 64 * RUN_ALIGN)
    waits((n >> 3) & 7, 8 * RUN_ALIGN)
    waits(n & 7, RUN_ALIGN)


def _dispatch_body(tc_ref, rs_ref, p0_ref, pn_ref, nu_ref, m_ref, h_ref, xin_ref, g_ref, z_ref, sem, zsem, *, tm, R):
    i = pl.program_id(0)
    slot = i % 2
    rows = m_ref[...].astype(jnp.int16)
    jj = lax.broadcasted_iota(I32, (R, tm), 0).astype(jnp.int16)
    hit = (jj == rows[0:1]) | (jj == rows[1:2]) | (jj == rows[2:3]) | (jj == rows[3:4])
    onehot = jnp.where(hit, jnp.ones((), BF16), jnp.zeros((), BF16))
    g_ref[slot] = jnp.dot(onehot, h_ref[...].astype(BF16), preferred_element_type=F32).astype(BF16)

    group = lambda ref, r: ref.at[pl.ds(r, RUN_ALIGN)]

    @pl.when(i > 0)
    def _():
        _wait_groups(_group_count(tc_ref, i - 1), g_ref.at[1 - slot], xin_ref, sem.at[1 - slot])

    _for_each_run(tc_ref, rs_ref, i, lambda lr, gr, n: pltpu.make_async_copy(
        g_ref.at[slot, pl.ds(lr, n)], xin_ref.at[pl.ds(gr, n)], sem.at[slot]).start())

    @pl.when(i == pl.num_programs(0) - 1)
    def _():
        _wait_groups(_group_count(tc_ref, i), g_ref.at[slot], xin_ref, sem.at[slot])

    @pl.when(i == 0)
    def _():
        z_ref[...] = jnp.zeros_like(z_ref)
        total = jnp.int32(0)
        for ex in range(N_EXPERTS):
            def body(q, carry, ex=ex):
                pltpu.make_async_copy(group(z_ref, 0), group(xin_ref, pl.multiple_of(p0_ref[ex] + q * RUN_ALIGN, RUN_ALIGN)), zsem).start()
                return carry
            lax.fori_loop(0, pn_ref[ex], body, 0)
            total = total + pn_ref[ex]
        _wait_groups(total, z_ref, xin_ref, zsem)

        block = lambda j: xin_ref.at[pl.ds(pl.multiple_of(j * MOE_BLOCK, MOE_BLOCK), MOE_BLOCK)]
        n_blocks = xin_ref.shape[0] // MOE_BLOCK

        def fill(j, carry):
            pltpu.make_async_copy(z_ref, block(j), zsem).start()
            return carry

        def fill_wait(j, carry):
            pltpu.make_async_copy(z_ref, block(j), zsem).wait()
            return carry

        lax.fori_loop(nu_ref[0], n_blocks, fill, 0)
        lax.fori_loop(nu_ref[0], n_blocks, fill_wait, 0)


def _dispatch(tcnt, rowstart, pad0, padn, n_used, m_t, h, n_rows):
    T, D = h.shape
    tm = ROUTE_TILE
    col = pl.BlockSpec((8, tm), lambda i, *_: (0, i))
    return pl.pallas_call(
        functools.partial(_dispatch_body, tm=tm, R=GROUP_ROWS),
        grid_spec=pltpu.PrefetchScalarGridSpec(
            num_scalar_prefetch=5,
            grid=(T // tm,),
            in_specs=[col, pl.BlockSpec((tm, D), lambda i, *_: (i, 0))],
            out_specs=pl.BlockSpec(memory_space=pl.ANY),
            scratch_shapes=[pltpu.VMEM((2, GROUP_ROWS, D), BF16), pltpu.VMEM((MOE_BLOCK, D), BF16),
                            pltpu.SemaphoreType.DMA((2,)), pltpu.SemaphoreType.DMA(())]),
        out_shape=jax.ShapeDtypeStruct((n_rows, D), BF16),
        compiler_params=pltpu.CompilerParams(dimension_semantics=("arbitrary",), vmem_limit_bytes=56 * 2**20),
        name="dispatch",
    )(tcnt, rowstart, pad0, padn, n_used, m_t, h)


def _moe_body(be_ref, nu_ref, hv_ref, x_ref, w1g_ref, w1l_ref, b1g_ref, b1l_ref, w2_ref, b2_ref, y_ref):
    del be_ref
    j = pl.program_id(0)
    live = j < nu_ref[0]

    def expert_mlp(rows):
        x = x_ref[rows, :]
        hg = lax.dot_general(x, w1g_ref[0], NT_DIMS, preferred_element_type=F32) + b1g_ref[0]
        hl = lax.dot_general(x, w1l_ref[0], NT_DIMS, preferred_element_type=F32) + b1l_ref[0]
        xg = jnp.minimum(hg, SWIGLU_LIMIT)
        xl = jnp.clip(hl, -SWIGLU_LIMIT, SWIGLU_LIMIT)
        act = xg * _sigmoid(SWIGLU_ALPHA * xg) * (xl + 1.0)
        y = jnp.dot(act.astype(BF16), w2_ref[0].astype(BF16), preferred_element_type=F32) + b2_ref[0]
        y_ref[rows, :] = y.astype(BF16)

    for parts in range(1, MOE_BLOCK // MOE_PART + 1):
        @pl.when(live & (hv_ref[j] == parts))
        def _(parts=parts):
            skip = MOE_BLOCK - parts * MOE_PART
            expert_mlp(slice(skip, MOE_BLOCK))
            if skip:
                y_ref[:skip, :] = jnp.zeros((skip, y_ref.shape[1]), y_ref.dtype)

    @pl.when(jnp.logical_not(live))
    def _():
        y_ref[...] = jnp.zeros_like(y_ref)


def _moe(block_e, n_used, half_block, xin, w1g, w1l, b1, w2, b2):
    n_rows = xin.shape[0]
    D = w2.shape[2]
    F = w2.shape[1]
    nblk = n_rows // MOE_BLOCK
    b1s = jnp.transpose(b1.reshape(N_EXPERTS, F, 2), (2, 0, 1)).reshape(2, N_EXPERTS, 1, F)
    b1spec = lambda k: pl.BlockSpec((None, 1, 1, F), lambda j, be, nu, hv: (k, be[j], 0, 0))
    b2r = b2.reshape(N_EXPERTS, 1, D)
    xspec = pl.BlockSpec((MOE_BLOCK, D), lambda j, be, nu, hv: (jnp.minimum(j, nu[0] - 1), 0))
    yspec = pl.BlockSpec((MOE_BLOCK, D), lambda j, be, nu, hv: (j, 0))
    wspec = lambda a: pl.BlockSpec((1,) + a.shape[1:], lambda j, be, nu, hv: (be[j], 0, 0))
    return pl.pallas_call(
        _moe_body,
        grid_spec=pltpu.PrefetchScalarGridSpec(
            num_scalar_prefetch=3,
            grid=(nblk,),
            in_specs=[xspec, wspec(w1g), wspec(w1l), b1spec(0), b1spec(1), wspec(w2), wspec(b2r)],
            out_specs=yspec),
        out_shape=jax.ShapeDtypeStruct(xin.shape, BF16),
        compiler_params=pltpu.CompilerParams(dimension_semantics=("arbitrary",),
                                             vmem_limit_bytes=48 * 2**20),
        name="moe",
    )(block_e, n_used, half_block, xin, w1g, w1l, b1s, b1s, w2, b2r)


def _combine_body(tc_ref, rs_ref, slab_ref, y_ref, x1_ref, fg_ref, out_ref, g_ref, sem, *, tm, R):
    i = pl.program_id(0)
    slot = i % 2

    def fetch(tile, slot):
        _for_each_run(tc_ref, rs_ref, tile, lambda lr, gr, n: pltpu.make_async_copy(
            y_ref.at[pl.ds(gr, n)], g_ref.at[slot, pl.ds(lr, n)], sem.at[slot]).start())

    @pl.when(i == 0)
    def _():
        g_ref[...] = jnp.zeros_like(g_ref)
        fetch(i, slot)

    @pl.when(i + 1 < pl.num_programs(0))
    def _():
        fetch(i + 1, 1 - slot)

    slab = slab_ref[...]
    gates = slab[:, 0:TOP_K].astype(BF16)
    rows = slab[:, TOP_K:2 * TOP_K].astype(I32).astype(jnp.int16)
    jj = lax.broadcasted_iota(I32, (tm, R), 1).astype(jnp.int16)
    sel = jnp.zeros((tm, R), BF16)
    for k in range(TOP_K):
        sel = jnp.where(jj == rows[:, k:k + 1], gates[:, k:k + 1], sel)

    _wait_groups(_group_count(tc_ref, i), y_ref, g_ref.at[slot], sem.at[slot])
    acc = x1_ref[...] + jnp.dot(sel, g_ref[slot], preferred_element_type=F32)
    out_ref[...] = acc * lax.rsqrt(jnp.mean(acc * acc, axis=-1, keepdims=True) + NORM_EPS) * fg_ref[...]


def _combine(tcnt, rowstart, slab, y, x1, final_g):
    T, D = x1.shape
    tm = ROUTE_TILE
    fg = final_g.reshape(1, D)
    return pl.pallas_call(
        functools.partial(_combine_body, tm=tm, R=GROUP_ROWS),
        grid_spec=pltpu.PrefetchScalarGridSpec(
            num_scalar_prefetch=2,
            grid=(T // tm,),
            in_specs=[pl.BlockSpec((tm, LANES), lambda i, *_: (i, 0)),
                      pl.BlockSpec(memory_space=pl.ANY),
                      pl.BlockSpec((tm, D), lambda i, *_: (i, 0)),
                      pl.BlockSpec((1, D), lambda i, *_: (0, 0))],
            out_specs=pl.BlockSpec((tm, D), lambda i, *_: (i, 0)),
            scratch_shapes=[pltpu.VMEM((2, GROUP_ROWS, D), BF16), pltpu.SemaphoreType.DMA((2,))]),
        out_shape=jax.ShapeDtypeStruct((T, D), F32),
        compiler_params=pltpu.CompilerParams(dimension_semantics=("arbitrary",), vmem_limit_bytes=56 * 2**20),
        name="combine",
    )(tcnt, rowstart, slab, y, x1, fg)


def _mixers(x2, B, S, norm1_g, w_in, rel_bias, wg_f, bg_f, wg_b, bg_b, gla_norm_g, w1):
    a1, a4, a16, gla, bcum, w1g, w1l = _in_proj(x2, B, S, norm1_g, w_in, wg_f, bg_f, wg_b, bg_b, w1)
    branches = [_attn_branch(a, rel_bias, d) for a, d in zip((a1, a4, a16), DILATIONS)]
    g = _gla(gla, bcum, gla_norm_g, B, S)
    return [o for o, _ in branches], [l for _, l in branches], g, w1g, w1l


def _moe_layer(h, m_t, slab, tcnt_f, x1, w1g, w1l, b1, w2, b2, final_g):
    T = x1.shape[0]
    nt = T // ROUTE_TILE
    nblk = (T * TOP_K + nt * N_EXPERTS * (RUN_ALIGN - 1)) // MOE_BLOCK + N_EXPERTS
    tcnt = tcnt_f[:, :, 0].astype(I32)
    rowstart, pad0, padn, block_e, half_block, n_used = _plan(tcnt, nblk)
    xin = _dispatch(tcnt, rowstart, pad0, padn, n_used, m_t, h, nblk * MOE_BLOCK)
    y = _moe(block_e, n_used, half_block, xin, w1g, w1l, b1, w2, b2)
    return _combine(tcnt, rowstart, slab, y, x1, final_g)


def kernel(x, norm1_g, w_in, rel_bias, gla_wg_fwd, gla_bg_fwd, gla_wg_bwd, gla_bg_bwd, gla_norm_g, w_out, norm2_g, router_w, router_b, moe_w1, moe_b1, moe_w2, moe_b2, final_g):
    B, S, D = x.shape
    assert w_in.shape[0] == 1, "one layer"
    x2 = x.reshape(B * S, D)
    outs, lses, g, w1g, w1l = _mixers(x2, B, S, norm1_g[0], w_in, rel_bias, gla_wg_fwd[0], gla_bg_fwd[0],
                                      gla_wg_bwd[0], gla_bg_bwd[0], gla_norm_g[0], moe_w1[0])
    x1, h, m_t, slab, tcnt = _mix(outs, lses, g, x2, B, S, w_out[0], norm2_g[0], router_w[0], router_b[0])
    out = _moe_layer(h, m_t, slab, tcnt, x1, w1g, w1l, moe_b1[0], moe_w2[0], moe_b2[0], final_g)
    return out.reshape(B, S, D)
```

```python
import functools

import jax
import jax.numpy as jnp
import numpy as np
from jax import lax
from jax.experimental import pallas as pl
from jax.experimental.pallas import tpu as pltpu

F32, BF16, I32 = jnp.float32, jnp.bfloat16, jnp.int32

LANES = 128
HEAD_DIM = 64
ATTN_HEADS = 8
ATTN_W = ATTN_HEADS * HEAD_DIM
HALF_SPAN = 64
ATTN_QBLK = 128
ATTN_KWIN = 256
ATTN_STEP_ROWS = 512
DILATIONS = (1, 4, 16)
T5_BUCKETS = 32
T5_MAX_DISTANCE = 1024
NEG_INF = -1e30
LOG2_E = 1.4426950408889634
LN_2 = 0.6931471805599453
GLA_KEY_W = 256
GLA_VAL_W = 512
GLA_DV = 128
GLA_RANK = 16
GLA_TAU = 16.0
GLA_CHUNK = 64
N_EXPERTS = 32
TOP_K = 4
SWIGLU_LIMIT = 7.0
SWIGLU_ALPHA = 1.702
NORM_EPS = 1e-5
MOE_BLOCK = 1024
MOE_PART = 256
W1_SUBTILE = 512
ROUTE_TILE = 512
RUN_ALIGN = 16
RUN_SHIFT = RUN_ALIGN.bit_length() - 1
GROUP_ROWS = ROUTE_TILE * TOP_K + N_EXPERTS * RUN_ALIGN

ATTN_COLS = 3 * ATTN_W
GLA_COLS = 2 * GLA_KEY_W + 2 * GLA_VAL_W
Z_COLS = LANES
PROJ_COLS = ATTN_COLS + GLA_COLS + Z_COLS

NT_DIMS = (((1,), (1,)), ((), ()))


def _sigmoid(x):
    return 1.0 / (1.0 + jnp.exp(-x))


def _split3(a):
    h1 = a.astype(BF16)
    r1 = a - h1.astype(F32)
    h2 = r1.astype(BF16)
    h3 = (r1 - h2.astype(F32)).astype(BF16)
    return h1, h2, h3


def _dot_f32(a, b):
    ah, al, _ = _split3(a)
    bh, bl, _ = _split3(b)
    d = lambda u, v: jnp.dot(u, v, preferred_element_type=F32)
    return d(ah, bh) + d(al, bh) + d(ah, bl)


def _split_w1_tile(w_ref, g_ref, l_ref, t_ref):
    sub = t_ref.shape[1]
    for s in range(w_ref.shape[2] // sub):
        wt = w_ref[0, :, s * sub:(s + 1) * sub].T
        out_rows = slice(s * sub // 2, (s + 1) * sub // 2)
        for c in range(wt.shape[1] // LANES):
            cols = slice(c * LANES, (c + 1) * LANES)
            t_ref[c] = wt[:, cols]
            g_ref[0, out_rows, cols] = t_ref[c, pl.ds(0, sub // 2, stride=2), :].astype(BF16)
            l_ref[0, out_rows, cols] = t_ref[c, pl.ds(1, sub // 2, stride=2), :].astype(BF16)


def _in_proj_body(x_ref, g_ref, w_ref, wg_ref, bg_ref, w1_ref, a1_ref, a4_ref, a16_ref, gla_ref, b_ref,
                  w1g_ref, w1l_ref, t_ref, wb_ref, *, tm):
    @pl.when((pl.program_id(0) == 0) & (pl.program_id(1) == 0))
    def _():
        wb_ref[...] = jnp.zeros_like(wb_ref)
        wb_ref[:, :w_ref.shape[1]] = w_ref[...].astype(BF16)

    _split_w1_tile(w1_ref, w1g_ref, w1l_ref, t_ref)

    x = x_ref[...]
    n = x * lax.rsqrt(jnp.mean(x * x, axis=-1, keepdims=True) + NORM_EPS) * g_ref[...]
    p = jnp.dot(n.astype(BF16), wb_ref[...], preferred_element_type=F32)
    gla_ref[...] = p[:, ATTN_COLS:ATTN_COLS + GLA_COLS].astype(BF16)

    qkv = p[:, :ATTN_COLS].astype(BF16)
    a1_ref[0, 0] = qkv
    dst = lax.broadcasted_iota(I32, (tm, tm), 0)
    src = lax.broadcasted_iota(I32, (tm, tm), 1)
    for dil, a_ref in ((DILATIONS[1], a4_ref), (DILATIONS[2], a16_ref)):
        per = tm // dil
        perm = (src == (dst % per) * dil + dst // per).astype(F32).astype(BF16)
        moved = jnp.dot(perm, qkv, preferred_element_type=F32).astype(BF16)
        for r in range(dil):
            a_ref[0, r] = moved[r * per:(r + 1) * per]

    z = p[:, ATTN_COLS + GLA_COLS:]
    zz = _dot_f32(z, wg_ref[...]) + bg_ref[...]
    la = (jnp.minimum(zz, 0.0) - jnp.log1p(jnp.exp(-jnp.abs(zz)))) * (1.0 / GLA_TAU)
    C = GLA_CHUNK
    r_i = lax.broadcasted_iota(I32, (C, C), 0)
    c_i = lax.broadcasted_iota(I32, (C, C), 1)
    lower = (r_i >= c_i).astype(BF16)
    upper = (r_i <= c_i).astype(BF16)
    W = GLA_KEY_W
    for j in range(tm // C):
        rows = slice(j * C, (j + 1) * C)
        for tri, cols in ((lower, slice(0, W)), (upper, slice(W, 2 * W))):
            terms = jnp.concatenate(_split3(la[rows, cols]), axis=1)
            cs = jnp.dot(tri, terms, preferred_element_type=F32)
            b_ref[rows, cols] = cs[:, :W] + cs[:, W:2 * W] + cs[:, 2 * W:]


def _in_proj(x2, B, S, norm_g, w_in, wg_f, bg_f, wg_b, bg_b, w1):
    T, D = x2.shape
    tm = 256
    nt = S // tm
    E, _, F2 = w1.shape
    w1_cols = E * F2 // (B * nt)
    per_e = F2 // w1_cols
    assert w1_cols * B * nt == E * F2 and per_e * w1_cols == F2 and w1_cols % W1_SUBTILE == 0
    w1_in = pl.BlockSpec((1, D, w1_cols), lambda b, i: ((b * nt + i) // per_e, 0, (b * nt + i) % per_e))
    w1_out = pl.BlockSpec((1, w1_cols // 2, D), lambda b, i: ((b * nt + i) // per_e, (b * nt + i) % per_e, 0))
    w_spec = pl.BlockSpec((None,) + w_in.shape[1:], lambda b, i: (0, 0, 0), pipeline_mode=pl.Buffered(1))
    wg = jnp.zeros((Z_COLS, 2 * GLA_KEY_W), F32)
    wg = wg.at[:GLA_RANK, :GLA_KEY_W].set(wg_f).at[GLA_RANK:2 * GLA_RANK, GLA_KEY_W:].set(wg_b)
    bg = jnp.concatenate([bg_f, bg_b]).reshape(1, 2 * GLA_KEY_W)
    row = lambda c: pl.BlockSpec((tm, c), lambda b, i: (b * nt + i, 0))
    full = lambda a: pl.BlockSpec(a.shape, lambda b, i: (0,) * a.ndim)
    res = lambda d: pl.BlockSpec((1, d, tm // d, ATTN_COLS), lambda b, i: (b, 0, i, 0))
    res_shape = lambda d: jax.ShapeDtypeStruct((B, d, S // d, ATTN_COLS), BF16)
    g2 = norm_g.reshape(1, D)
    return pl.pallas_call(
        functools.partial(_in_proj_body, tm=tm),
        grid=(B, nt),
        in_specs=[row(D), full(g2), w_spec, full(wg), full(bg), w1_in],
        out_specs=[res(d) for d in DILATIONS] + [row(GLA_COLS), row(2 * GLA_KEY_W), w1_out, w1_out],
        out_shape=[res_shape(d) for d in DILATIONS]
                  + [jax.ShapeDtypeStruct((T, GLA_COLS), BF16), jax.ShapeDtypeStruct((T, 2 * GLA_KEY_W), F32)]
                  + [jax.ShapeDtypeStruct((E, F2 // 2, D), BF16)] * 2,
        scratch_shapes=[pltpu.VMEM((D // LANES, W1_SUBTILE, LANES), F32), pltpu.VMEM((D, PROJ_COLS), BF16)],
        compiler_params=pltpu.CompilerParams(dimension_semantics=("arbitrary", "arbitrary"),
                                             vmem_limit_bytes=56 * 2**20),
        name="in_proj",
    )(x2, g2, w_in, wg, bg, w1)


def _t5_bucket(rel):
    nb = T5_BUCKETS // 2
    max_exact = nb // 2
    n = np.abs(rel)
    large = max_exact + (np.log(np.maximum(n, 1).astype(np.float32) / max_exact)
                         / np.log(T5_MAX_DISTANCE / max_exact) * (nb - max_exact)).astype(np.int32)
    large = np.minimum(large, nb - 1)
    return (np.where(rel > 0, nb, 0) + np.where(n < max_exact, n, large)).astype(np.int32)


def _bucket_tables(dil):
    s = np.arange(ATTN_QBLK)[:, None]
    t = np.arange(ATTN_KWIN)[None, :]
    tabs = []
    for c in (0, HALF_SPAN, 2 * HALF_SPAN):
        off = t - c - s
        tabs.append(np.where(np.abs(off) <= HALF_SPAN, _t5_bucket(off * dil), -1))
    return np.stack(tabs).astype(np.int32)


def _attn_body(rb_ref, idx_ref, q_ref, k_ref, v_ref, o_ref, lse_ref, tab_ref, *, L, TQ):
    first = (pl.program_id(0) == 0) & (pl.program_id(1) == 0) & (pl.program_id(2) == 0)

    @pl.when(first)
    def _():
        for v in range(3):
            idx = idx_ref[v]
            for h in range(ATTN_HEADS):
                fill = lambda j, t, h=h, idx=idx: jnp.where(idx == j, rb_ref[j, h] * LOG2_E, t)
                tab_ref[v, h] = lax.fori_loop(0, T5_BUCKETS, fill, jnp.where(idx < 0, NEG_INF, 0.0).astype(F32))

    i = pl.program_id(2)
    lo = lax.broadcasted_iota(I32, (ATTN_QBLK, LANES), 1) < HEAD_DIM
    scale = HEAD_DIM ** -0.5 * LOG2_E
    lane = lax.broadcasted_iota(I32, (ATTN_QBLK, LANES), 1)
    for rb, sb in [(rb, sb) for rb in range(q_ref.shape[0]) for sb in range(TQ // ATTN_QBLK)]:
        l0 = i * TQ + sb * ATTN_QBLK
        lse_tile = jnp.zeros((ATTN_QBLK, LANES), F32)
        ws = pl.multiple_of(jnp.clip(l0 - HALF_SPAN, 0, L - ATTN_KWIN), HALF_SPAN)
        var = jnp.where(l0 == 0, 0, jnp.where(l0 == L - ATTN_QBLK, 2, 1))
        rows = slice(sb * ATTN_QBLK, (sb + 1) * ATTN_QBLK)
        for hp in range(ATTN_HEADS // 2):
            cols = slice(hp * LANES, (hp + 1) * LANES)
            q2 = q_ref[rb, rows, cols]
            k2 = k_ref[rb, pl.ds(ws, ATTN_KWIN), cols]
            v2 = v_ref[rb, pl.ds(ws, ATTN_KWIN), cols]
            zero = jnp.zeros_like(q2)
            qs = jnp.concatenate([jnp.where(lo, q2, zero), jnp.where(lo, zero, q2)], axis=0)
            s = lax.dot_general(qs, k2, NT_DIMS, preferred_element_type=F32)
            bias = jnp.concatenate([tab_ref[var, 2 * hp], tab_ref[var, 2 * hp + 1]], axis=0)
            logits = s * scale + bias
            m = jnp.max(logits, axis=-1, keepdims=True)
            p = jnp.exp2(logits - m)
            den = jnp.sum(p, axis=-1, keepdims=True)
            pv = jnp.dot(p.astype(BF16), v2, preferred_element_type=F32) / den
            lse = m * LN_2 + jnp.log(den)
            o_ref[rb, rows, cols] = jnp.where(lo, pv[:ATTN_QBLK], pv[ATTN_QBLK:]).astype(BF16)
            lse_tile = jnp.where(lane == 2 * hp, lse[:ATTN_QBLK], lse_tile)
            lse_tile = jnp.where(lane == 2 * hp + 1, lse[ATTN_QBLK:], lse_tile)
        lse_ref[rb, rows, :] = lse_tile


def _attn_branch(attn, rel_bias, dil):
    B, _, L, _ = attn.shape
    TQ = min(ATTN_STEP_ROWS, L)
    RB = ATTN_STEP_ROWS // TQ
    idx = jnp.asarray(_bucket_tables(dil))
    body = functools.partial(_attn_body, L=L, TQ=TQ)
    kv = lambda j: pl.BlockSpec((None, RB, L, ATTN_W), lambda b, r, i: (b, r, 0, j))
    q_spec = pl.BlockSpec((None, RB, TQ, ATTN_W), lambda b, r, i: (b, r, i, 0))
    return pl.pallas_call(
        body,
        grid=(B, dil // RB, L // TQ),
        in_specs=[pl.BlockSpec(memory_space=pltpu.SMEM),
                  pl.BlockSpec(idx.shape, lambda b, r, i: (0, 0, 0)),
                  q_spec, kv(1), kv(2)],
        out_specs=[q_spec, pl.BlockSpec((None, RB, TQ, LANES), lambda b, r, i: (b, r, i, 0))],
        out_shape=[jax.ShapeDtypeStruct((B, dil, L, ATTN_W), BF16),
                   jax.ShapeDtypeStruct((B, dil, L, LANES), F32)],
        scratch_shapes=[pltpu.VMEM((3, ATTN_HEADS, ATTN_QBLK, ATTN_KWIN), F32)],
        compiler_params=pltpu.CompilerParams(dimension_semantics=("arbitrary",) * 3,
                                             vmem_limit_bytes=48 * 2**20),
        name=f"attn_d{dil}",
    )(rel_bias, idx, attn, attn, attn)


def _gla_body(q_ref, k_ref, v_ref, gate_ref, bf_ref, bb_ref, ng_ref, out_ref, of_ref, ob_ref, st_ref, *, S):
    C = GLA_CHUNK
    nc = S // C
    r = lax.broadcasted_iota(I32, (C, LANES), 0)
    s2 = lax.broadcasted_iota(I32, (C, LANES), 1) % HEAD_DIM
    lo = lax.broadcasted_iota(I32, (C, LANES), 1) < HEAD_DIM
    blockdiag = ((lax.broadcasted_iota(I32, (2 * GLA_DV, LANES), 0) < GLA_DV)
                 == (lax.broadcasted_iota(I32, (2 * GLA_DV, LANES), 1) < HEAD_DIM))
    st_ref[...] = jnp.zeros_like(st_ref)

    def one_chunk(ci, direction):
        mask = (r >= s2) if direction == 0 else (r <= s2)
        b_ref, o_ref = (bf_ref, of_ref) if direction == 0 else (bb_ref, ob_ref)
        rows = pl.ds(pl.multiple_of(ci * C, C), C)
        q = q_ref[0, rows, :].astype(F32) * (HEAD_DIM ** -0.5)
        k = k_ref[0, rows, :].astype(F32)
        v = v_ref[0, rows, :]
        b = b_ref[0, rows, :]
        g_tot = b[C - 1:C, :] if direction == 0 else b[0:1, :]
        qd = (q * jnp.exp(b)).astype(BF16)
        kd = (k * jnp.exp(-b)).astype(BF16)
        kst = (k * jnp.exp(g_tot - b)).astype(BF16)
        zk = jnp.zeros_like(kd)
        ks = jnp.concatenate([jnp.where(lo, kd, zk), jnp.where(lo, zk, kd)], axis=0)
        a = lax.dot_general(qd, ks, NT_DIMS, preferred_element_type=F32)
        a = jnp.where(mask, a, 0.0).astype(BF16)
        zv = jnp.zeros((C, GLA_DV), BF16)
        vblk = jnp.concatenate([jnp.concatenate([v[:, :GLA_DV], zv], axis=1),
                                jnp.concatenate([zv, v[:, GLA_DV:]], axis=1)], axis=0)
        st = st_ref[direction]
        o_ref[rows, :] = (jnp.dot(a, vblk, preferred_element_type=F32)
                          + lax.dot_general(qd, st.astype(BF16), NT_DIMS, preferred_element_type=F32))
        d_st = jnp.dot(v.astype(F32).T.astype(BF16), kst, preferred_element_type=F32)
        st_ref[direction] = st * jnp.exp(g_tot) + jnp.where(blockdiag, d_st, 0.0)

    def step(i, carry):
        one_chunk(i, 0)
        one_chunk(nc - 1 - i, 1)
        return carry

    lax.fori_loop(0, nc, step, 0, unroll=8)

    tr = 256
    def epilogue(i, carry):
        rows = pl.ds(pl.multiple_of(i * tr, tr), tr)
        for hh in range(2):
            cols = slice(hh * GLA_DV, (hh + 1) * GLA_DV)
            o = of_ref[rows, cols] + ob_ref[rows, cols]
            o = o * lax.rsqrt(jnp.mean(o * o, axis=-1, keepdims=True) + NORM_EPS) * ng_ref[0, :, cols]
            gate = gate_ref[0, rows, cols].astype(F32)
            out_ref[0, rows, cols] = (o * (gate * _sigmoid(gate))).astype(BF16)
        return carry

    lax.fori_loop(0, S // tr, epilogue, 0)


def _gla(gla, bcum, norm_g, B, S):
    g3 = gla.reshape(B, S, GLA_COLS)
    b3 = bcum.reshape(B, S, 2 * GLA_KEY_W)
    ng = norm_g.reshape(2, 1, 2 * GLA_DV)
    body = functools.partial(_gla_body, S=S)
    spec = lambda w, off: pl.BlockSpec((1, S, w), lambda b, p: (b, 0, off + p))
    out = pl.pallas_call(
        body,
        grid=(B, 2),
        in_specs=[spec(LANES, 0),
                  spec(LANES, GLA_KEY_W // LANES),
                  spec(2 * GLA_DV, (2 * GLA_KEY_W) // (2 * GLA_DV)),
                  spec(2 * GLA_DV, (2 * GLA_KEY_W + GLA_VAL_W) // (2 * GLA_DV)),
                  spec(LANES, 0),
                  spec(LANES, GLA_KEY_W // LANES),
                  pl.BlockSpec((1, 1, 2 * GLA_DV), lambda b, p: (p, 0, 0))],
        out_specs=pl.BlockSpec((1, S, 2 * GLA_DV), lambda b, p: (b, 0, p)),
        out_shape=jax.ShapeDtypeStruct((B, S, GLA_VAL_W), BF16),
        scratch_shapes=[pltpu.VMEM((S, 2 * GLA_DV), F32), pltpu.VMEM((S, 2 * GLA_DV), F32),
                        pltpu.VMEM((2, 2 * GLA_DV, LANES), F32)],
        compiler_params=pltpu.CompilerParams(dimension_semantics=("arbitrary", "arbitrary"),
                                             vmem_limit_bytes=56 * 2**20),
        name="gla",
    )(g3, g3, g3, g3, b3, b3, ng)
    return out.reshape(B * S, GLA_VAL_W)


def _natural_rows(blk_ref, u_ref, dil, tm):
    if dil == 1:
        return blk_ref[0, 0].astype(F32)
    chunks = blk_ref.shape[-1] // LANES
    for r in range(dil):
        for c in range(chunks):
            u_ref[c, pl.ds(r, tm // dil, stride=dil), :] = blk_ref[0, r, :, c * LANES:(c + 1) * LANES].astype(F32)
    return jnp.concatenate([u_ref[c] for c in range(chunks)], axis=1)


def _per_head_lanes(w):
    head = lax.broadcasted_iota(I32, (LANES, ATTN_W), 0)
    col = lax.broadcasted_iota(I32, (LANES, ATTN_W), 1)
    spread = (col // HEAD_DIM == head).astype(F32).astype(BF16)
    hi, lo, _ = _split3(w)
    return (jnp.dot(hi, spread, preferred_element_type=F32) + jnp.dot(lo, spread, preferred_element_type=F32))


def _mix_body(o1, o2, o3, l1, l2, l3, g_ref, x_ref, wo_ref, n2_ref, rwh_ref, rwl_ref, rb_ref,
              x1_ref, h_ref, m_ref, slab_ref, tcnt_ref, u_ref, wob_ref, *, tm):
    @pl.when((pl.program_id(0) == 0) & (pl.program_id(1) == 0))
    def _():
        wob_ref[...] = wo_ref[...].astype(BF16)

    ls = [_natural_rows(l, u_ref, d, tm) for l, d in zip((l1, l2, l3), DILATIONS)]
    m = jnp.maximum(jnp.maximum(ls[0], ls[1]), ls[2])
    es = [jnp.exp(l - m) for l in ls]
    den = es[0] + es[1] + es[2]
    a = _per_head_lanes(es[0] / den) * _natural_rows(o1, u_ref, DILATIONS[0], tm)
    a = a + _per_head_lanes(es[1] / den) * _natural_rows(o2, u_ref, DILATIONS[1], tm)
    a = a + _per_head_lanes(es[2] / den) * _natural_rows(o3, u_ref, DILATIONS[2], tm)
    mix = (jnp.dot(a.astype(BF16), wob_ref[:ATTN_W, :], preferred_element_type=F32)
           + jnp.dot(g_ref[...], wob_ref[ATTN_W:, :], preferred_element_type=F32))
    x1 = x_ref[...] + mix
    x1_ref[...] = x1
    h = x1 * lax.rsqrt(jnp.mean(x1 * x1, axis=-1, keepdims=True) + NORM_EPS) * n2_ref[...]
    h_ref[...] = h.astype(BF16)

    hh, hl, _ = _split3(h)
    dg = lambda u, v: lax.dot_general(u, v, NT_DIMS, preferred_element_type=F32)
    logits = dg(rwh_ref[...], hh) + dg(rwl_ref[...], hh) + dg(rwh_ref[...], hl) + rb_ref[...]
    ie = lax.broadcasted_iota(I32, (N_EXPERTS, tm), 0)
    cur = logits
    vals, idxs = [], []
    for _ in range(TOP_K):
        mx = jnp.max(cur, axis=0, keepdims=True)
        ix = jnp.min(jnp.where(cur == mx, ie, N_EXPERTS), axis=0, keepdims=True)
        vals.append(mx)
        idxs.append(ix)
        cur = jnp.where(ie == ix, -jnp.inf, cur)
    ex = [jnp.exp(v - vals[0]) for v in vals]
    den = ex[0] + ex[1] + ex[2] + ex[3]
    gates = [e / den for e in ex]

    onehots = [ie == ix for ix in idxs]
    chosen = (onehots[0] | onehots[1] | onehots[2] | onehots[3]).astype(F32)
    tr = lax.broadcasted_iota(I32, (tm, tm), 0)
    tc = lax.broadcasted_iota(I32, (tm, tm), 1)
    before = jnp.dot(chosen.astype(BF16), (tr < tc).astype(BF16), preferred_element_type=F32)
    count = jnp.broadcast_to(jnp.sum(chosen, axis=1, keepdims=True), (N_EXPERTS, LANES))
    tcnt_ref[0] = count
    run = jnp.floor((count + (RUN_ALIGN - 1.0)) * (1.0 / RUN_ALIGN)) * RUN_ALIGN
    er = lax.broadcasted_iota(I32, (N_EXPERTS, N_EXPERTS), 0)
    ec = lax.broadcasted_iota(I32, (N_EXPERTS, N_EXPERTS), 1)
    run_start = jnp.dot((ec < er).astype(BF16), run.astype(BF16), preferred_element_type=F32)
    where_ = before + run_start[:, 0:1]
    locs = [jnp.sum(jnp.where(oh, where_, 0.0), axis=0, keepdims=True) for oh in onehots]

    def pack(rows, n):
        sub = lax.broadcasted_iota(I32, (n, tm), 0)
        out = jnp.zeros((n, tm), rows[0].dtype)
        for k, row in enumerate(rows):
            out = jnp.where(sub == k, row, out)
        return out

    m_ref[...] = pack([l.astype(I32) for l in locs], 8)
    slab = pack(gates + locs, 8)
    slab_ref[...] = jnp.concatenate([slab, jnp.zeros((LANES - 8, tm), F32)], axis=0).T


def _mix(outs, lses, g, x2, B, S, w_out, norm2_g, router_w, router_b):
    T, D = x2.shape
    tm = ROUTE_TILE
    nt = S // tm
    wo_spec = pl.BlockSpec(w_out.shape, lambda b, i: (0, 0), pipeline_mode=pl.Buffered(1))
    rwt = router_w.T
    rwh = rwt.astype(BF16)
    rwl = (rwt - rwh.astype(F32)).astype(BF16)
    rb = router_b.reshape(N_EXPERTS, 1)
    n2 = norm2_g.reshape(1, D)
    row = lambda c: pl.BlockSpec((tm, c), lambda b, i: (b * nt + i, 0))
    full = lambda a: pl.BlockSpec(a.shape, lambda b, i: (0,) * a.ndim)
    col = pl.BlockSpec((8, tm), lambda b, i: (0, b * nt + i))
    res = [pl.BlockSpec((1, d, tm // d, ATTN_W), lambda b, i: (b, 0, i, 0)) for d in DILATIONS]
    res_lse = [pl.BlockSpec((1, d, tm // d, LANES), lambda b, i: (b, 0, i, 0)) for d in DILATIONS]
    return pl.pallas_call(
        functools.partial(_mix_body, tm=tm),
        grid=(B, nt),
        in_specs=res + res_lse + [row(GLA_VAL_W), row(D), wo_spec, full(n2), full(rwh), full(rwl), full(rb)],
        out_specs=[row(D), row(D), col, row(LANES),
                   pl.BlockSpec((1, N_EXPERTS, LANES), lambda b, i: (b * nt + i, 0, 0))],
        out_shape=[jax.ShapeDtypeStruct((T, D), F32),
                   jax.ShapeDtypeStruct((T, D), BF16),
                   jax.ShapeDtypeStruct((8, T), I32),
                   jax.ShapeDtypeStruct((T, LANES), F32),
                   jax.ShapeDtypeStruct((T // tm, N_EXPERTS, LANES), F32)],
        scratch_shapes=[pltpu.VMEM((ATTN_W // LANES, tm, LANES), F32), pltpu.VMEM(w_out.shape, BF16)],
        compiler_params=pltpu.CompilerParams(dimension_semantics=("arbitrary", "arbitrary"),
                                             vmem_limit_bytes=48 * 2**20),
        name="mix_router",
    )(*outs, *lses, g, x2, w_out, n2, rwh, rwl, rb)


def _run_groups(c):
    return (c + (RUN_ALIGN - 1)) >> RUN_SHIFT


def _plan_body(tc_ref, rs_ref, p0_ref, pn_ref, be_ref, hv_ref, nu_ref, *, nt, nblk):
    shift = MOE_BLOCK.bit_length() - 1
    start = jnp.int32(0)
    for ex in range(N_EXPERTS):
        rows_of = lambda i, ex=ex: _run_groups(tc_ref[i, ex]) << RUN_SHIFT
        tot = lax.fori_loop(0, nt, lambda i, tot: tot + rows_of(i), jnp.int32(0), unroll=True)
        nb = (tot + (MOE_BLOCK - 1)) >> shift
        pad = (nb << shift) - tot
        p0_ref[ex] = start
        pn_ref[ex] = pad >> RUN_SHIFT

        def run(i, at, ex=ex):
            rs_ref[i, ex] = at
            return at + rows_of(i)

        lax.fori_loop(0, nt, run, start + pad, unroll=True)
        b0 = start >> shift

        def fill(j, carry, b0=b0, ex=ex):
            be_ref[b0 + j] = jnp.int32(ex)
            hv_ref[b0 + j] = jnp.int32(MOE_BLOCK // MOE_PART)
            return carry

        lax.fori_loop(0, nb, fill, 0)

        @pl.when(nb > 0)
        def _(b0=b0, pad=pad):
            hv_ref[b0] = (MOE_BLOCK - pad + (MOE_PART - 1)) >> (MOE_PART.bit_length() - 1)

        start = start + (nb << shift)
    used = start >> shift

    def tail(j, carry):
        be_ref[j] = jnp.int32(N_EXPERTS - 1)
        hv_ref[j] = jnp.int32(0)
        return carry

    lax.fori_loop(used, nblk, tail, 0)
    nu_ref[0] = used


def _plan(tcnt, nblk):
    nt = tcnt.shape[0]
    smem = pl.BlockSpec(memory_space=pltpu.SMEM)
    return pl.pallas_call(
        functools.partial(_plan_body, nt=nt, nblk=nblk),
        in_specs=[smem],
        out_specs=[smem] * 6,
        out_shape=[jax.ShapeDtypeStruct(tcnt.shape, I32),
                   jax.ShapeDtypeStruct((N_EXPERTS,), I32),
                   jax.ShapeDtypeStruct((N_EXPERTS,), I32),
                   jax.ShapeDtypeStruct((nblk,), I32),
                   jax.ShapeDtypeStruct((nblk,), I32),
                   jax.ShapeDtypeStruct((1,), I32)],
        name="plan",
    )(tcnt)


def _for_each_run(tc_ref, rs_ref, i, fn):
    big = 4 * RUN_ALIGN
    start = jnp.int32(0)
    for ex in range(N_EXPERTS):
        ng = _run_groups(tc_ref[i, ex])
        g0 = rs_ref[i, ex]

        def piece(q, carry, start=start, g0=g0):
            fn(pl.multiple_of(start + q * big, RUN_ALIGN), pl.multiple_of(g0 + q * big, RUN_ALIGN), big)
            return carry

        def single(q, carry, start=start, g0=g0):
            fn(pl.multiple_of(start + q * RUN_ALIGN, RUN_ALIGN), pl.multiple_of(g0 + q * RUN_ALIGN, RUN_ALIGN),
               RUN_ALIGN)
            return carry

        lax.fori_loop(0, ng >> 2, piece, 0)
        lax.fori_loop((ng >> 2) << 2, ng, single, 0)
        start = start + (ng << RUN_SHIFT)


def _group_count(tc_ref, i):
    n = jnp.int32(0)
    for ex in range(N_EXPERTS):
        n = n + _run_groups(tc_ref[i, ex])
    return n


def _wait_groups(n, src_ref, dst_ref, sem):
    def waits(count, rows):
        def body(q, carry):
            pltpu.make_async_copy(src_ref.at[pl.ds(0, rows)], dst_ref.at[pl.ds(0, rows)], sem).wait()
            return carry
        lax.fori_loop(0, count, body, 0)

    waits(n >> 6, 64 * RUN_ALIGN)
    waits((n >> 3) & 7, 8 * RUN_ALIGN)
    waits(n & 7, RUN_ALIGN)


def _dispatch_body(tc_ref, rs_ref, p0_ref, pn_ref, nu_ref, m_ref, h_ref, xin_ref, g_ref, z_ref, sem, zsem, *, tm, R):
    i = pl.program_id(0)
    slot = i % 2
    rows = m_ref[...].astype(jnp.int16)
    jj = lax.broadcasted_iota(I32, (R, tm), 0).astype(jnp.int16)
    hit = (jj == rows[0:1]) | (jj == rows[1:2]) | (jj == rows[2:3]) | (jj == rows[3:4])
    onehot = jnp.where(hit, jnp.ones((), BF16), jnp.zeros((), BF16))
    g_ref[slot] = jnp.dot(onehot, h_ref[...], preferred_element_type=F32).astype(BF16)

    group = lambda ref, r: ref.at[pl.ds(r, RUN_ALIGN)]

    @pl.when(i > 0)
    def _():
        _wait_groups(_group_count(tc_ref, i - 1), g_ref.at[1 - slot], xin_ref, sem.at[1 - slot])

    _for_each_run(tc_ref, rs_ref, i, lambda lr, gr, n: pltpu.make_async_copy(
        g_ref.at[slot, pl.ds(lr, n)], xin_ref.at[pl.ds(gr, n)], sem.at[slot]).start())

    @pl.when(i == pl.num_programs(0) - 1)
    def _():
        _wait_groups(_group_count(tc_ref, i), g_ref.at[slot], xin_ref, sem.at[slot])

    @pl.when(i == 0)
    def _():
        z_ref[...] = jnp.zeros_like(z_ref)
        total = jnp.int32(0)
        for ex in range(N_EXPERTS):
            def body(q, carry, ex=ex):
                pltpu.make_async_copy(group(z_ref, 0), group(xin_ref, pl.multiple_of(p0_ref[ex] + q * RUN_ALIGN, RUN_ALIGN)), zsem).start()
                return carry
            lax.fori_loop(0, pn_ref[ex], body, 0)
            total = total + pn_ref[ex]
        _wait_groups(total, z_ref, xin_ref, zsem)

        block = lambda j: xin_ref.at[pl.ds(pl.multiple_of(j * MOE_BLOCK, MOE_BLOCK), MOE_BLOCK)]
        n_blocks = xin_ref.shape[0] // MOE_BLOCK

        def fill(j, carry):
            pltpu.make_async_copy(z_ref, block(j), zsem).start()
            return carry

        def fill_wait(j, carry):
            pltpu.make_async_copy(z_ref, block(j), zsem).wait()
            return carry

        lax.fori_loop(nu_ref[0], n_blocks, fill, 0)
        lax.fori_loop(nu_ref[0], n_blocks, fill_wait, 0)


def _dispatch(tcnt, rowstart, pad0, padn, n_used, m_t, h, n_rows):
    T, D = h.shape
    tm = ROUTE_TILE
    col = pl.BlockSpec((8, tm), lambda i, *_: (0, i))
    return pl.pallas_call(
        functools.partial(_dispatch_body, tm=tm, R=GROUP_ROWS),
        grid_spec=pltpu.PrefetchScalarGridSpec(
            num_scalar_prefetch=5,
            grid=(T // tm,),
            in_specs=[col, pl.BlockSpec((tm, D), lambda i, *_: (i, 0))],
            out_specs=pl.BlockSpec(memory_space=pl.ANY),
            scratch_shapes=[pltpu.VMEM((2, GROUP_ROWS, D), BF16), pltpu.VMEM((MOE_BLOCK, D), BF16),
                            pltpu.SemaphoreType.DMA((2,)), pltpu.SemaphoreType.DMA(())]),
        out_shape=jax.ShapeDtypeStruct((n_rows, D), BF16),
        compiler_params=pltpu.CompilerParams(dimension_semantics=("arbitrary",), vmem_limit_bytes=56 * 2**20),
        name="dispatch",
    )(tcnt, rowstart, pad0, padn, n_used, m_t, h)


def _moe_body(be_ref, nu_ref, hv_ref, x_ref, w1g_ref, w1l_ref, b1g_ref, b1l_ref, w2_ref, b2_ref, y_ref):
    del be_ref
    j = pl.program_id(0)
    live = j < nu_ref[0]

    def expert_mlp(rows):
        x = x_ref[rows, :]
        hg = lax.dot_general(x, w1g_ref[0], NT_DIMS, preferred_element_type=F32) + b1g_ref[0]
        hl = lax.dot_general(x, w1l_ref[0], NT_DIMS, preferred_element_type=F32) + b1l_ref[0]
        xg = jnp.minimum(hg, SWIGLU_LIMIT)
        xl = jnp.clip(hl, -SWIGLU_LIMIT, SWIGLU_LIMIT)
        act = xg * _sigmoid(SWIGLU_ALPHA * xg) * (xl + 1.0)
        y = jnp.dot(act.astype(BF16), w2_ref[0].astype(BF16), preferred_element_type=F32) + b2_ref[0]
        y_ref[rows, :] = y.astype(BF16)

    for parts in range(1, MOE_BLOCK // MOE_PART + 1):
        @pl.when(live & (hv_ref[j] == parts))
        def _(parts=parts):
            skip = MOE_BLOCK - parts * MOE_PART
            expert_mlp(slice(skip, MOE_BLOCK))
            if skip:
                y_ref[:skip, :] = jnp.zeros((skip, y_ref.shape[1]), y_ref.dtype)

    @pl.when(jnp.logical_not(live))
    def _():
        y_ref[...] = jnp.zeros_like(y_ref)


def _moe(block_e, n_used, half_block, xin, w1g, w1l, b1, w2, b2):
    n_rows = xin.shape[0]
    D = w2.shape[2]
    F = w2.shape[1]
    nblk = n_rows // MOE_BLOCK
    b1s = jnp.transpose(b1.reshape(N_EXPERTS, F, 2), (2, 0, 1)).reshape(2, N_EXPERTS, 1, F)
    b1spec = lambda k: pl.BlockSpec((None, 1, 1, F), lambda j, be, nu, hv: (k, be[j], 0, 0))
    b2r = b2.reshape(N_EXPERTS, 1, D)
    xspec = pl.BlockSpec((MOE_BLOCK, D), lambda j, be, nu, hv: (jnp.minimum(j, nu[0] - 1), 0))
    yspec = pl.BlockSpec((MOE_BLOCK, D), lambda j, be, nu, hv: (j, 0))
    wspec = lambda a: pl.BlockSpec((1,) + a.shape[1:], lambda j, be, nu, hv: (be[j], 0, 0))
    return pl.pallas_call(
        _moe_body,
        grid_spec=pltpu.PrefetchScalarGridSpec(
            num_scalar_prefetch=3,
            grid=(nblk,),
            in_specs=[xspec, wspec(w1g), wspec(w1l), b1spec(0), b1spec(1), wspec(w2), wspec(b2r)],
            out_specs=yspec),
        out_shape=jax.ShapeDtypeStruct(xin.shape, BF16),
        compiler_params=pltpu.CompilerParams(dimension_semantics=("arbitrary",),
                                             vmem_limit_bytes=48 * 2**20),
        name="moe",
    )(block_e, n_used, half_block, xin, w1g, w1l, b1s, b1s, w2, b2r)


def _combine_body(tc_ref, rs_ref, slab_ref, y_ref, x1_ref, fg_ref, out_ref, g_ref, sem, *, tm, R):
    i = pl.program_id(0)
    slot = i % 2

    def fetch(tile, slot):
        _for_each_run(tc_ref, rs_ref, tile, lambda lr, gr, n: pltpu.make_async_copy(
            y_ref.at[pl.ds(gr, n)], g_ref.at[slot, pl.ds(lr, n)], sem.at[slot]).start())

    @pl.when(i == 0)
    def _():
        g_ref[...] = jnp.zeros_like(g_ref)
        fetch(i, slot)

    @pl.when(i + 1 < pl.num_programs(0))
    def _():
        fetch(i + 1, 1 - slot)

    slab = slab_ref[...]
    gates = slab[:, 0:TOP_K].astype(BF16)
    rows = slab[:, TOP_K:2 * TOP_K].astype(I32).astype(jnp.int16)
    jj = lax.broadcasted_iota(I32, (tm, R), 1).astype(jnp.int16)
    sel = jnp.zeros((tm, R), BF16)
    for k in range(TOP_K):
        sel = jnp.where(jj == rows[:, k:k + 1], gates[:, k:k + 1], sel)

    _wait_groups(_group_count(tc_ref, i), y_ref, g_ref.at[slot], sem.at[slot])
    acc = x1_ref[...] + jnp.dot(sel, g_ref[slot], preferred_element_type=F32)
    out_ref[...] = acc * lax.rsqrt(jnp.mean(acc * acc, axis=-1, keepdims=True) + NORM_EPS) * fg_ref[...]


def _combine(tcnt, rowstart, slab, y, x1, final_g):
    T, D = x1.shape
    tm = ROUTE_TILE
    fg = final_g.reshape(1, D)
    return pl.pallas_call(
        functools.partial(_combine_body, tm=tm, R=GROUP_ROWS),
        grid_spec=pltpu.PrefetchScalarGridSpec(
            num_scalar_prefetch=2,
            grid=(T // tm,),
            in_specs=[pl.BlockSpec((tm, LANES), lambda i, *_: (i, 0)),
                      pl.BlockSpec(memory_space=pl.ANY),
                      pl.BlockSpec((tm, D), lambda i, *_: (i, 0)),
                      pl.BlockSpec((1, D), lambda i, *_: (0, 0))],
            out_specs=pl.BlockSpec((tm, D), lambda i, *_: (i, 0)),
            scratch_shapes=[pltpu.VMEM((2, GROUP_ROWS, D), BF16), pltpu.SemaphoreType.DMA((2,))]),
        out_shape=jax.ShapeDtypeStruct((T, D), F32),
        compiler_params=pltpu.CompilerParams(dimension_semantics=("arbitrary",), vmem_limit_bytes=56 * 2**20),
        name="combine",
    )(tcnt, rowstart, slab, y, x1, fg)


def _mixers(x2, B, S, norm1_g, w_in, rel_bias, wg_f, bg_f, wg_b, bg_b, gla_norm_g, w1):
    a1, a4, a16, gla, bcum, w1g, w1l = _in_proj(x2, B, S, norm1_g, w_in, wg_f, bg_f, wg_b, bg_b, w1)
    branches = [_attn_branch(a, rel_bias, d) for a, d in zip((a1, a4, a16), DILATIONS)]
    g = _gla(gla, bcum, gla_norm_g, B, S)
    return [o for o, _ in branches], [l for _, l in branches], g, w1g, w1l


def _moe_layer(h, m_t, slab, tcnt_f, x1, w1g, w1l, b1, w2, b2, final_g):
    T = x1.shape[0]
    nt = T // ROUTE_TILE
    nblk = (T * TOP_K + nt * N_EXPERTS * (RUN_ALIGN - 1)) // MOE_BLOCK + N_EXPERTS
    tcnt = tcnt_f[:, :, 0].astype(I32)
    rowstart, pad0, padn, block_e, half_block, n_used = _plan(tcnt, nblk)
    xin = _dispatch(tcnt, rowstart, pad0, padn, n_used, m_t, h, nblk * MOE_BLOCK)
    y = _moe(block_e, n_used, half_block, xin, w1g, w1l, b1, w2, b2)
    return _combine(tcnt, rowstart, slab, y, x1, final_g)


def kernel(x, norm1_g, w_in, rel_bias, gla_wg_fwd, gla_bg_fwd, gla_wg_bwd, gla_bg_bwd, gla_norm_g, w_out, norm2_g, router_w, router_b, moe_w1, moe_b1, moe_w2, moe_b2, final_g):
    B, S, D = x.shape
    assert w_in.shape[0] == 1, "one layer"
    x2 = x.reshape(B * S, D)
    outs, lses, g, w1g, w1l = _mixers(x2, B, S, norm1_g[0], w_in, rel_bias, gla_wg_fwd[0], gla_bg_fwd[0],
                                      gla_wg_bwd[0], gla_bg_bwd[0], gla_norm_g[0], moe_w1[0])
    x1, h, m_t, slab, tcnt = _mix(outs, lses, g, x2, B, S, w_out[0], norm2_g[0], router_w[0], router_b[0])
    out = _moe_layer(h, m_t, slab, tcnt, x1, w1g, w1l, moe_b1[0], moe_w2[0], moe_b2[0], final_g)
    return out.reshape(B, S, D)
```

```python
import functools

import jax
import jax.numpy as jnp
import numpy as np
from jax import lax
from jax.experimental import pallas as pl
from jax.experimental.pallas import tpu as pltpu

F32, BF16, I32 = jnp.float32, jnp.bfloat16, jnp.int32

LANES = 128
HEAD_DIM = 64
ATTN_HEADS = 8
ATTN_W = ATTN_HEADS * HEAD_DIM
HALF_SPAN = 64
ATTN_QBLK = 128
ATTN_KWIN = 256
ATTN_STEP_ROWS = 1024
DILATIONS = (1, 4, 16)
T5_BUCKETS = 32
T5_MAX_DISTANCE = 1024
NEG_INF = -1e30
LOG2_E = 1.4426950408889634
LN_2 = 0.6931471805599453
GLA_KEY_W = 256
GLA_VAL_W = 512
GLA_DV = 128
GLA_RANK = 16
GLA_TAU = 16.0
GLA_CHUNK = 64
N_EXPERTS = 32
TOP_K = 4
SWIGLU_LIMIT = 7.0
SWIGLU_ALPHA = 1.702
NORM_EPS = 1e-5
MOE_BLOCK = 1024
MOE_PART = 256
W1_SUBTILE = 512
ROUTE_TILE = 512
RUN_ALIGN = 16
RUN_SHIFT = RUN_ALIGN.bit_length() - 1
GROUP_ROWS = ROUTE_TILE * TOP_K + N_EXPERTS * RUN_ALIGN

ATTN_COLS = 3 * ATTN_W
GLA_COLS = 2 * GLA_KEY_W + 2 * GLA_VAL_W
Z_COLS = LANES
PROJ_COLS = ATTN_COLS + GLA_COLS + Z_COLS

NT_DIMS = (((1,), (1,)), ((), ()))


def _sigmoid(x):
    return 1.0 / (1.0 + jnp.exp(-x))


def _split3(a):
    h1 = a.astype(BF16)
    r1 = a - h1.astype(F32)
    h2 = r1.astype(BF16)
    h3 = (r1 - h2.astype(F32)).astype(BF16)
    return h1, h2, h3


def _dot_f32(a, b):
    ah, al, _ = _split3(a)
    bh, bl, _ = _split3(b)
    d = lambda u, v: jnp.dot(u, v, preferred_element_type=F32)
    return d(ah, bh) + d(al, bh) + d(ah, bl)


def _split_w1_tile(w_ref, g_ref, l_ref, t_ref):
    sub = t_ref.shape[1]
    for s in range(w_ref.shape[2] // sub):
        wt = w_ref[0, :, s * sub:(s + 1) * sub].T
        out_rows = slice(s * sub // 2, (s + 1) * sub // 2)
        for c in range(wt.shape[1] // LANES):
            cols = slice(c * LANES, (c + 1) * LANES)
            t_ref[c] = wt[:, cols]
            g_ref[0, out_rows, cols] = t_ref[c, pl.ds(0, sub // 2, stride=2), :].astype(BF16)
            l_ref[0, out_rows, cols] = t_ref[c, pl.ds(1, sub // 2, stride=2), :].astype(BF16)


def _in_proj_body(x_ref, g_ref, w_ref, wg_ref, bg_ref, w1_ref, a1_ref, a4_ref, a16_ref, gla_ref, b_ref,
                  w1g_ref, w1l_ref, t_ref, wb_ref, *, tm):
    @pl.when((pl.program_id(0) == 0) & (pl.program_id(1) == 0))
    def _():
        wb_ref[...] = jnp.zeros_like(wb_ref)
        wb_ref[:, :w_ref.shape[1]] = w_ref[...].astype(BF16)

    _split_w1_tile(w1_ref, w1g_ref, w1l_ref, t_ref)

    x = x_ref[...]
    n = x * lax.rsqrt(jnp.mean(x * x, axis=-1, keepdims=True) + NORM_EPS) * g_ref[...]
    p = jnp.dot(n.astype(BF16), wb_ref[...], preferred_element_type=F32)
    gla_ref[...] = p[:, ATTN_COLS:ATTN_COLS + GLA_COLS].astype(BF16)

    qkv = p[:, :ATTN_COLS].astype(BF16)
    a1_ref[0, 0] = qkv
    dst = lax.broadcasted_iota(I32, (tm, tm), 0)
    src = lax.broadcasted_iota(I32, (tm, tm), 1)
    for dil, a_ref in ((DILATIONS[1], a4_ref), (DILATIONS[2], a16_ref)):
        per = tm // dil
        perm = (src == (dst % per) * dil + dst // per).astype(F32).astype(BF16)
        moved = jnp.dot(perm, qkv, preferred_element_type=F32).astype(BF16)
        for r in range(dil):
            a_ref[0, r] = moved[r * per:(r + 1) * per]

    z = p[:, ATTN_COLS + GLA_COLS:]
    zz = _dot_f32(z, wg_ref[...]) + bg_ref[...]
    la = (jnp.minimum(zz, 0.0) - jnp.log1p(jnp.exp(-jnp.abs(zz)))) * (1.0 / GLA_TAU)
    C = GLA_CHUNK
    r_i = lax.broadcasted_iota(I32, (C, C), 0)
    c_i = lax.broadcasted_iota(I32, (C, C), 1)
    lower = (r_i >= c_i).astype(BF16)
    upper = (r_i <= c_i).astype(BF16)
    W = GLA_KEY_W
    for j in range(tm // C):
        rows = slice(j * C, (j + 1) * C)
        for tri, cols in ((lower, slice(0, W)), (upper, slice(W, 2 * W))):
            terms = jnp.concatenate(_split3(la[rows, cols]), axis=1)
            cs = jnp.dot(tri, terms, preferred_element_type=F32)
            b_ref[rows, cols] = cs[:, :W] + cs[:, W:2 * W] + cs[:, 2 * W:]


def _in_proj(x2, B, S, norm_g, w_in, wg_f, bg_f, wg_b, bg_b, w1):
    T, D = x2.shape
    tm = 256
    nt = S // tm
    E, _, F2 = w1.shape
    w1_cols = E * F2 // (B * nt)
    per_e = F2 // w1_cols
    assert w1_cols * B * nt == E * F2 and per_e * w1_cols == F2 and w1_cols % W1_SUBTILE == 0
    w1_in = pl.BlockSpec((1, D, w1_cols), lambda b, i: ((b * nt + i) // per_e, 0, (b * nt + i) % per_e))
    w1_out = pl.BlockSpec((1, w1_cols // 2, D), lambda b, i: ((b * nt + i) // per_e, (b * nt + i) % per_e, 0))
    w_spec = pl.BlockSpec((None,) + w_in.shape[1:], lambda b, i: (0, 0, 0), pipeline_mode=pl.Buffered(1))
    wg = jnp.zeros((Z_COLS, 2 * GLA_KEY_W), F32)
    wg = wg.at[:GLA_RANK, :GLA_KEY_W].set(wg_f).at[GLA_RANK:2 * GLA_RANK, GLA_KEY_W:].set(wg_b)
    bg = jnp.concatenate([bg_f, bg_b]).reshape(1, 2 * GLA_KEY_W)
    row = lambda c: pl.BlockSpec((tm, c), lambda b, i: (b * nt + i, 0))
    full = lambda a: pl.BlockSpec(a.shape, lambda b, i: (0,) * a.ndim)
    res = lambda d: pl.BlockSpec((1, d, tm // d, ATTN_COLS), lambda b, i: (b, 0, i, 0))
    res_shape = lambda d: jax.ShapeDtypeStruct((B, d, S // d, ATTN_COLS), BF16)
    g2 = norm_g.reshape(1, D)
    return pl.pallas_call(
        functools.partial(_in_proj_body, tm=tm),
        grid=(B, nt),
        in_specs=[row(D), full(g2), w_spec, full(wg), full(bg), w1_in],
        out_specs=[res(d) for d in DILATIONS] + [row(GLA_COLS), row(2 * GLA_KEY_W), w1_out, w1_out],
        out_shape=[res_shape(d) for d in DILATIONS]
                  + [jax.ShapeDtypeStruct((T, GLA_COLS), BF16), jax.ShapeDtypeStruct((T, 2 * GLA_KEY_W), F32)]
                  + [jax.ShapeDtypeStruct((E, F2 // 2, D), BF16)] * 2,
        scratch_shapes=[pltpu.VMEM((D // LANES, W1_SUBTILE, LANES), F32), pltpu.VMEM((D, PROJ_COLS), BF16)],
        compiler_params=pltpu.CompilerParams(dimension_semantics=("arbitrary", "arbitrary"),
                                             vmem_limit_bytes=56 * 2**20),
        name="in_proj",
    )(x2, g2, w_in, wg, bg, w1)


def _t5_bucket(rel):
    nb = T5_BUCKETS // 2
    max_exact = nb // 2
    n = np.abs(rel)
    large = max_exact + (np.log(np.maximum(n, 1).astype(np.float32) / max_exact)
                         / np.log(T5_MAX_DISTANCE / max_exact) * (nb - max_exact)).astype(np.int32)
    large = np.minimum(large, nb - 1)
    return (np.where(rel > 0, nb, 0) + np.where(n < max_exact, n, large)).astype(np.int32)


def _bucket_tables(dil):
    s = np.arange(ATTN_QBLK)[:, None]
    t = np.arange(ATTN_KWIN)[None, :]
    tabs = []
    for c in (0, HALF_SPAN, 2 * HALF_SPAN):
        off = t - c - s
        tabs.append(np.where(np.abs(off) <= HALF_SPAN, _t5_bucket(off * dil), -1))
    return np.stack(tabs).astype(np.int32)


def _attn_body(rb_ref, idx_ref, q_ref, k_ref, v_ref, o_ref, lse_ref, tab_ref, *, L, TQ):
    first = (pl.program_id(0) == 0) & (pl.program_id(1) == 0) & (pl.program_id(2) == 0)

    @pl.when(first)
    def _():
        for v in range(3):
            idx = idx_ref[v]
            for h in range(ATTN_HEADS):
                fill = lambda j, t, h=h, idx=idx: jnp.where(idx == j, rb_ref[j, h] * LOG2_E, t)
                tab_ref[v, h] = lax.fori_loop(0, T5_BUCKETS, fill, jnp.where(idx < 0, NEG_INF, 0.0).astype(F32))

    i = pl.program_id(2)
    lo = lax.broadcasted_iota(I32, (ATTN_QBLK, LANES), 1) < HEAD_DIM
    scale = HEAD_DIM ** -0.5 * LOG2_E
    lane = lax.broadcasted_iota(I32, (ATTN_QBLK, LANES), 1)
    for rb, sb in [(rb, sb) for rb in range(q_ref.shape[0]) for sb in range(TQ // ATTN_QBLK)]:
        l0 = i * TQ + sb * ATTN_QBLK
        lse_tile = jnp.zeros((ATTN_QBLK, LANES), F32)
        ws = pl.multiple_of(jnp.clip(l0 - HALF_SPAN, 0, L - ATTN_KWIN), HALF_SPAN)
        var = jnp.where(l0 == 0, 0, jnp.where(l0 == L - ATTN_QBLK, 2, 1))
        rows = slice(sb * ATTN_QBLK, (sb + 1) * ATTN_QBLK)
        for hp in range(ATTN_HEADS // 2):
            cols = slice(hp * LANES, (hp + 1) * LANES)
            q2 = q_ref[rb, rows, cols]
            k2 = k_ref[rb, pl.ds(ws, ATTN_KWIN), cols]
            v2 = v_ref[rb, pl.ds(ws, ATTN_KWIN), cols]
            zero = jnp.zeros_like(q2)
            qs = jnp.concatenate([jnp.where(lo, q2, zero), jnp.where(lo, zero, q2)], axis=0)
            s = lax.dot_general(qs, k2, NT_DIMS, preferred_element_type=F32)
            bias = jnp.concatenate([tab_ref[var, 2 * hp], tab_ref[var, 2 * hp + 1]], axis=0)
            logits = s * scale + bias
            m = jnp.max(logits, axis=-1, keepdims=True)
            p = jnp.exp2(logits - m)
            den = jnp.sum(p, axis=-1, keepdims=True)
            pv = jnp.dot(p.astype(BF16), v2, preferred_element_type=F32) / den
            lse = m * LN_2 + jnp.log(den)
            o_ref[rb, rows, cols] = jnp.where(lo, pv[:ATTN_QBLK], pv[ATTN_QBLK:]).astype(BF16)
            lse_tile = jnp.where(lane == 2 * hp, lse[:ATTN_QBLK], lse_tile)
            lse_tile = jnp.where(lane == 2 * hp + 1, lse[ATTN_QBLK:], lse_tile)
        lse_ref[rb, rows, :] = lse_tile


def _attn_branch(attn, rel_bias, dil):
    B, _, L, _ = attn.shape
    TQ = min(ATTN_STEP_ROWS, L)
    RB = ATTN_STEP_ROWS // TQ
    idx = jnp.asarray(_bucket_tables(dil))
    body = functools.partial(_attn_body, L=L, TQ=TQ)
    kv = lambda j: pl.BlockSpec((None, RB, L, ATTN_W), lambda b, r, i: (b, r, 0, j))
    q_spec = pl.BlockSpec((None, RB, TQ, ATTN_W), lambda b, r, i: (b, r, i, 0))
    return pl.pallas_call(
        body,
        grid=(B, dil // RB, L // TQ),
        in_specs=[pl.BlockSpec(memory_space=pltpu.SMEM),
                  pl.BlockSpec(idx.shape, lambda b, r, i: (0, 0, 0)),
                  q_spec, kv(1), kv(2)],
        out_specs=[q_spec, pl.BlockSpec((None, RB, TQ, LANES), lambda b, r, i: (b, r, i, 0))],
        out_shape=[jax.ShapeDtypeStruct((B, dil, L, ATTN_W), BF16),
                   jax.ShapeDtypeStruct((B, dil, L, LANES), F32)],
        scratch_shapes=[pltpu.VMEM((3, ATTN_HEADS, ATTN_QBLK, ATTN_KWIN), F32)],
        compiler_params=pltpu.CompilerParams(dimension_semantics=("arbitrary",) * 3,
                                             vmem_limit_bytes=48 * 2**20),
        name=f"attn_d{dil}",
    )(rel_bias, idx, attn, attn, attn)


def _gla_body(q_ref, k_ref, v_ref, gate_ref, bf_ref, bb_ref, ng_ref, out_ref, of_ref, ob_ref, st_ref, *, S):
    C = GLA_CHUNK
    nc = S // C
    r = lax.broadcasted_iota(I32, (C, LANES), 0)
    s2 = lax.broadcasted_iota(I32, (C, LANES), 1) % HEAD_DIM
    lo = lax.broadcasted_iota(I32, (C, LANES), 1) < HEAD_DIM
    blockdiag = ((lax.broadcasted_iota(I32, (2 * GLA_DV, LANES), 0) < GLA_DV)
                 == (lax.broadcasted_iota(I32, (2 * GLA_DV, LANES), 1) < HEAD_DIM))
    st_ref[...] = jnp.zeros_like(st_ref)

    def one_chunk(ci, direction):
        mask = (r >= s2) if direction == 0 else (r <= s2)
        b_ref, o_ref = (bf_ref, of_ref) if direction == 0 else (bb_ref, ob_ref)
        rows = pl.ds(pl.multiple_of(ci * C, C), C)
        q = q_ref[0, rows, :].astype(F32) * (HEAD_DIM ** -0.5)
        k = k_ref[0, rows, :].astype(F32)
        v = v_ref[0, rows, :]
        b = b_ref[0, rows, :]
        g_tot = b[C - 1:C, :] if direction == 0 else b[0:1, :]
        qd = (q * jnp.exp(b)).astype(BF16)
        kd = (k * jnp.exp(-b)).astype(BF16)
        kst = (k * jnp.exp(g_tot - b)).astype(BF16)
        zk = jnp.zeros_like(kd)
        ks = jnp.concatenate([jnp.where(lo, kd, zk), jnp.where(lo, zk, kd)], axis=0)
        a = lax.dot_general(qd, ks, NT_DIMS, preferred_element_type=F32)
        a = jnp.where(mask, a, 0.0).astype(BF16)
        zv = jnp.zeros((C, GLA_DV), BF16)
        vblk = jnp.concatenate([jnp.concatenate([v[:, :GLA_DV], zv], axis=1),
                                jnp.concatenate([zv, v[:, GLA_DV:]], axis=1)], axis=0)
        st = st_ref[direction]
        o_ref[rows, :] = (jnp.dot(a, vblk, preferred_element_type=F32)
                          + lax.dot_general(qd, st.astype(BF16), NT_DIMS, preferred_element_type=F32))
        d_st = jnp.dot(v.astype(F32).T.astype(BF16), kst, preferred_element_type=F32)
        st_ref[direction] = st * jnp.exp(g_tot) + jnp.where(blockdiag, d_st, 0.0)

    def step(i, carry):
        one_chunk(i, 0)
        one_chunk(nc - 1 - i, 1)
        return carry

    lax.fori_loop(0, nc, step, 0, unroll=8)

    tr = 256
    def epilogue(i, carry):
        rows = pl.ds(pl.multiple_of(i * tr, tr), tr)
        for hh in range(2):
            cols = slice(hh * GLA_DV, (hh + 1) * GLA_DV)
            o = of_ref[rows, cols] + ob_ref[rows, cols]
            o = o * lax.rsqrt(jnp.mean(o * o, axis=-1, keepdims=True) + NORM_EPS) * ng_ref[0, :, cols]
            gate = gate_ref[0, rows, cols].astype(F32)
            out_ref[0, rows, cols] = (o * (gate * _sigmoid(gate))).astype(BF16)
        return carry

    lax.fori_loop(0, S // tr, epilogue, 0)


def _gla(gla, bcum, norm_g, B, S):
    g3 = gla.reshape(B, S, GLA_COLS)
    b3 = bcum.reshape(B, S, 2 * GLA_KEY_W)
    ng = norm_g.reshape(2, 1, 2 * GLA_DV)
    body = functools.partial(_gla_body, S=S)
    spec = lambda w, off: pl.BlockSpec((1, S, w), lambda b, p: (b, 0, off + p))
    out = pl.pallas_call(
        body,
        grid=(B, 2),
        in_specs=[spec(LANES, 0),
                  spec(LANES, GLA_KEY_W // LANES),
                  spec(2 * GLA_DV, (2 * GLA_KEY_W) // (2 * GLA_DV)),
                  spec(2 * GLA_DV, (2 * GLA_KEY_W + GLA_VAL_W) // (2 * GLA_DV)),
                  spec(LANES, 0),
                  spec(LANES, GLA_KEY_W // LANES),
                  pl.BlockSpec((1, 1, 2 * GLA_DV), lambda b, p: (p, 0, 0))],
        out_specs=pl.BlockSpec((1, S, 2 * GLA_DV), lambda b, p: (b, 0, p)),
        out_shape=jax.ShapeDtypeStruct((B, S, GLA_VAL_W), BF16),
        scratch_shapes=[pltpu.VMEM((S, 2 * GLA_DV), F32), pltpu.VMEM((S, 2 * GLA_DV), F32),
                        pltpu.VMEM((2, 2 * GLA_DV, LANES), F32)],
        compiler_params=pltpu.CompilerParams(dimension_semantics=("arbitrary", "arbitrary"),
                                             vmem_limit_bytes=56 * 2**20),
        name="gla",
    )(g3, g3, g3, g3, b3, b3, ng)
    return out.reshape(B * S, GLA_VAL_W)


def _natural_rows(blk_ref, u_ref, dil, tm):
    if dil == 1:
        return blk_ref[0, 0].astype(F32)
    chunks = blk_ref.shape[-1] // LANES
    for r in range(dil):
        for c in range(chunks):
            u_ref[c, pl.ds(r, tm // dil, stride=dil), :] = blk_ref[0, r, :, c * LANES:(c + 1) * LANES].astype(F32)
    return jnp.concatenate([u_ref[c] for c in range(chunks)], axis=1)


def _per_head_lanes(w):
    head = lax.broadcasted_iota(I32, (LANES, ATTN_W), 0)
    col = lax.broadcasted_iota(I32, (LANES, ATTN_W), 1)
    spread = (col // HEAD_DIM == head).astype(F32).astype(BF16)
    hi, lo, _ = _split3(w)
    return (jnp.dot(hi, spread, preferred_element_type=F32) + jnp.dot(lo, spread, preferred_element_type=F32))


def _mix_body(o1, o2, o3, l1, l2, l3, g_ref, x_ref, wo_ref, n2_ref, rwh_ref, rwl_ref, rb_ref,
              x1_ref, h_ref, m_ref, slab_ref, tcnt_ref, u_ref, wob_ref, *, tm):
    @pl.when((pl.program_id(0) == 0) & (pl.program_id(1) == 0))
    def _():
        wob_ref[...] = wo_ref[...].astype(BF16)

    ls = [_natural_rows(l, u_ref, d, tm) for l, d in zip((l1, l2, l3), DILATIONS)]
    m = jnp.maximum(jnp.maximum(ls[0], ls[1]), ls[2])
    es = [jnp.exp(l - m) for l in ls]
    den = es[0] + es[1] + es[2]
    a = _per_head_lanes(es[0] / den) * _natural_rows(o1, u_ref, DILATIONS[0], tm)
    a = a + _per_head_lanes(es[1] / den) * _natural_rows(o2, u_ref, DILATIONS[1], tm)
    a = a + _per_head_lanes(es[2] / den) * _natural_rows(o3, u_ref, DILATIONS[2], tm)
    mix = (jnp.dot(a.astype(BF16), wob_ref[:ATTN_W, :], preferred_element_type=F32)
           + jnp.dot(g_ref[...], wob_ref[ATTN_W:, :], preferred_element_type=F32))
    x1 = x_ref[...] + mix
    x1_ref[...] = x1
    h = x1 * lax.rsqrt(jnp.mean(x1 * x1, axis=-1, keepdims=True) + NORM_EPS) * n2_ref[...]
    h_ref[...] = h.astype(BF16)

    hh, hl, _ = _split3(h)
    dg = lambda u, v: lax.dot_general(u, v, NT_DIMS, preferred_element_type=F32)
    logits = dg(rwh_ref[...], hh) + dg(rwl_ref[...], hh) + dg(rwh_ref[...], hl) + rb_ref[...]
    ie = lax.broadcasted_iota(I32, (N_EXPERTS, tm), 0)
    cur = logits
    vals, idxs = [], []
    for _ in range(TOP_K):
        mx = jnp.max(cur, axis=0, keepdims=True)
        ix = jnp.min(jnp.where(cur == mx, ie, N_EXPERTS), axis=0, keepdims=True)
        vals.append(mx)
        idxs.append(ix)
        cur = jnp.where(ie == ix, -jnp.inf, cur)
    ex = [jnp.exp(v - vals[0]) for v in vals]
    den = ex[0] + ex[1] + ex[2] + ex[3]
    gates = [e / den for e in ex]

    onehots = [ie == ix for ix in idxs]
    chosen = (onehots[0] | onehots[1] | onehots[2] | onehots[3]).astype(F32)
    tr = lax.broadcasted_iota(I32, (tm, tm), 0)
    tc = lax.broadcasted_iota(I32, (tm, tm), 1)
    before = jnp.dot(chosen.astype(BF16), (tr < tc).astype(BF16), preferred_element_type=F32)
    count = jnp.broadcast_to(jnp.sum(chosen, axis=1, keepdims=True), (N_EXPERTS, LANES))
    tcnt_ref[0] = count
    run = jnp.floor((count + (RUN_ALIGN - 1.0)) * (1.0 / RUN_ALIGN)) * RUN_ALIGN
    er = lax.broadcasted_iota(I32, (N_EXPERTS, N_EXPERTS), 0)
    ec = lax.broadcasted_iota(I32, (N_EXPERTS, N_EXPERTS), 1)
    run_start = jnp.dot((ec < er).astype(BF16), run.astype(BF16), preferred_element_type=F32)
    where_ = before + run_start[:, 0:1]
    locs = [jnp.sum(jnp.where(oh, where_, 0.0), axis=0, keepdims=True) for oh in onehots]

    def pack(rows, n):
        sub = lax.broadcasted_iota(I32, (n, tm), 0)
        out = jnp.zeros((n, tm), rows[0].dtype)
        for k, row in enumerate(rows):
            out = jnp.where(sub == k, row, out)
        return out

    m_ref[...] = pack([l.astype(I32) for l in locs], 8)
    slab = pack(gates + locs, 8)
    slab_ref[...] = jnp.concatenate([slab, jnp.zeros((LANES - 8, tm), F32)], axis=0).T


def _mix(outs, lses, g, x2, B, S, w_out, norm2_g, router_w, router_b):
    T, D = x2.shape
    tm = ROUTE_TILE
    nt = S // tm
    wo_spec = pl.BlockSpec(w_out.shape, lambda b, i: (0, 0), pipeline_mode=pl.Buffered(1))
    rwt = router_w.T
    rwh = rwt.astype(BF16)
    rwl = (rwt - rwh.astype(F32)).astype(BF16)
    rb = router_b.reshape(N_EXPERTS, 1)
    n2 = norm2_g.reshape(1, D)
    row = lambda c: pl.BlockSpec((tm, c), lambda b, i: (b * nt + i, 0))
    full = lambda a: pl.BlockSpec(a.shape, lambda b, i: (0,) * a.ndim)
    col = pl.BlockSpec((8, tm), lambda b, i: (0, b * nt + i))
    res = [pl.BlockSpec((1, d, tm // d, ATTN_W), lambda b, i: (b, 0, i, 0)) for d in DILATIONS]
    res_lse = [pl.BlockSpec((1, d, tm // d, LANES), lambda b, i: (b, 0, i, 0)) for d in DILATIONS]
    return pl.pallas_call(
        functools.partial(_mix_body, tm=tm),
        grid=(B, nt),
        in_specs=res + res_lse + [row(GLA_VAL_W), row(D), wo_spec, full(n2), full(rwh), full(rwl), full(rb)],
        out_specs=[row(D), row(D), col, row(LANES),
                   pl.BlockSpec((1, N_EXPERTS, LANES), lambda b, i: (b * nt + i, 0, 0))],
        out_shape=[jax.ShapeDtypeStruct((T, D), F32),
                   jax.ShapeDtypeStruct((T, D), BF16),
                   jax.ShapeDtypeStruct((8, T), I32),
                   jax.ShapeDtypeStruct((T, LANES), F32),
                   jax.ShapeDtypeStruct((T // tm, N_EXPERTS, LANES), F32)],
        scratch_shapes=[pltpu.VMEM((ATTN_W // LANES, tm, LANES), F32), pltpu.VMEM(w_out.shape, BF16)],
        compiler_params=pltpu.CompilerParams(dimension_semantics=("arbitrary", "arbitrary"),
                                             vmem_limit_bytes=48 * 2**20),
        name="mix_router",
    )(*outs, *lses, g, x2, w_out, n2, rwh, rwl, rb)


def _run_groups(c):
    return (c + (RUN_ALIGN - 1)) >> RUN_SHIFT


def _plan_body(tc_ref, rs_ref, p0_ref, pn_ref, be_ref, hv_ref, nu_ref, *, nt, nblk):
    shift = MOE_BLOCK.bit_length() - 1
    start = jnp.int32(0)
    for ex in range(N_EXPERTS):
        rows_of = lambda i, ex=ex: _run_groups(tc_ref[i, ex]) << RUN_SHIFT
        tot = lax.fori_loop(0, nt, lambda i, tot: tot + rows_of(i), jnp.int32(0), unroll=True)
        nb = (tot + (MOE_BLOCK - 1)) >> shift
        pad = (nb << shift) - tot
        p0_ref[ex] = start
        pn_ref[ex] = pad >> RUN_SHIFT

        def run(i, at, ex=ex):
            rs_ref[i, ex] = at
            return at + rows_of(i)

        lax.fori_loop(0, nt, run, start + pad, unroll=True)
        b0 = start >> shift

        def fill(j, carry, b0=b0, ex=ex):
            be_ref[b0 + j] = jnp.int32(ex)
            hv_ref[b0 + j] = jnp.int32(MOE_BLOCK // MOE_PART)
            return carry

        lax.fori_loop(0, nb, fill, 0)

        @pl.when(nb > 0)
        def _(b0=b0, pad=pad):
            hv_ref[b0] = (MOE_BLOCK - pad + (MOE_PART - 1)) >> (MOE_PART.bit_length() - 1)

        start = start + (nb << shift)
    used = start >> shift

    def tail(j, carry):
        be_ref[j] = jnp.int32(N_EXPERTS - 1)
        hv_ref[j] = jnp.int32(0)
        return carry

    lax.fori_loop(used, nblk, tail, 0)
    nu_ref[0] = used


def _plan(tcnt, nblk):
    nt = tcnt.shape[0]
    smem = pl.BlockSpec(memory_space=pltpu.SMEM)
    return pl.pallas_call(
        functools.partial(_plan_body, nt=nt, nblk=nblk),
        in_specs=[smem],
        out_specs=[smem] * 6,
        out_shape=[jax.ShapeDtypeStruct(tcnt.shape, I32),
                   jax.ShapeDtypeStruct((N_EXPERTS,), I32),
                   jax.ShapeDtypeStruct((N_EXPERTS,), I32),
                   jax.ShapeDtypeStruct((nblk,), I32),
                   jax.ShapeDtypeStruct((nblk,), I32),
                   jax.ShapeDtypeStruct((1,), I32)],
        name="plan",
    )(tcnt)


def _for_each_run(tc_ref, rs_ref, i, fn):
    big = 4 * RUN_ALIGN
    start = jnp.int32(0)
    for ex in range(N_EXPERTS):
        ng = _run_groups(tc_ref[i, ex])
        g0 = rs_ref[i, ex]

        def piece(q, carry, start=start, g0=g0):
            fn(pl.multiple_of(start + q * big, RUN_ALIGN), pl.multiple_of(g0 + q * big, RUN_ALIGN), big)
            return carry

        def single(q, carry, start=start, g0=g0):
            fn(pl.multiple_of(start + q * RUN_ALIGN, RUN_ALIGN), pl.multiple_of(g0 + q * RUN_ALIGN, RUN_ALIGN),
               RUN_ALIGN)
            return carry

        lax.fori_loop(0, ng >> 2, piece, 0)
        lax.fori_loop((ng >> 2) << 2, ng, single, 0)
        start = start + (ng << RUN_SHIFT)


def _group_count(tc_ref, i):
    n = jnp.int32(0)
    for ex in range(N_EXPERTS):
        n = n + _run_groups(tc_ref[i, ex])
    return n


def _wait_groups(n, src_ref, dst_ref, sem):
    def waits(count, rows):
        def body(q, carry):
            pltpu.make_async_copy(src_ref.at[pl.ds(0, rows)], dst_ref.at[pl.ds(0, rows)], sem).wait()
            return carry
        lax.fori_loop(0, count, body, 0)

    waits(n >> 6, 64 * RUN_ALIGN)
    waits((n >> 3) & 7, 8 * RUN_ALIGN)
    waits(n & 7, RUN_ALIGN)


def _dispatch_body(tc_ref, rs_ref, p0_ref, pn_ref, nu_ref, m_ref, h_ref, xin_ref, g_ref, z_ref, sem, zsem, *, tm, R):
    i = pl.program_id(0)
    slot = i % 2
    rows = m_ref[...].astype(jnp.int16)
    jj = lax.broadcasted_iota(I32, (R, tm), 0).astype(jnp.int16)
    hit = (jj == rows[0:1]) | (jj == rows[1:2]) | (jj == rows[2:3]) | (jj == rows[3:4])
    onehot = jnp.where(hit, jnp.ones((), BF16), jnp.zeros((), BF16))
    g_ref[slot] = jnp.dot(onehot, h_ref[...], preferred_element_type=F32).astype(BF16)

    group = lambda ref, r: ref.at[pl.ds(r, RUN_ALIGN)]

    @pl.when(i > 0)
    def _():
        _wait_groups(_group_count(tc_ref, i - 1), g_ref.at[1 - slot], xin_ref, sem.at[1 - slot])

    _for_each_run(tc_ref, rs_ref, i, lambda lr, gr, n: pltpu.make_async_copy(
        g_ref.at[slot, pl.ds(lr, n)], xin_ref.at[pl.ds(gr, n)], sem.at[slot]).start())

    @pl.when(i == pl.num_programs(0) - 1)
    def _():
        _wait_groups(_group_count(tc_ref, i), g_ref.at[slot], xin_ref, sem.at[slot])

    @pl.when(i == 0)
    def _():
        z_ref[...] = jnp.zeros_like(z_ref)
        total = jnp.int32(0)
        for ex in range(N_EXPERTS):
            def body(q, carry, ex=ex):
                pltpu.make_async_copy(group(z_ref, 0), group(xin_ref, pl.multiple_of(p0_ref[ex] + q * RUN_ALIGN, RUN_ALIGN)), zsem).start()
                return carry
            lax.fori_loop(0, pn_ref[ex], body, 0)
            total = total + pn_ref[ex]
        _wait_groups(total, z_ref, xin_ref, zsem)

        block = lambda j: xin_ref.at[pl.ds(pl.multiple_of(j * MOE_BLOCK, MOE_BLOCK), MOE_BLOCK)]
        n_blocks = xin_ref.shape[0] // MOE_BLOCK

        def fill(j, carry):
            pltpu.make_async_copy(z_ref, block(j), zsem).start()
            return carry

        def fill_wait(j, carry):
            pltpu.make_async_copy(z_ref, block(j), zsem).wait()
            return carry

        lax.fori_loop(nu_ref[0], n_blocks, fill, 0)
        lax.fori_loop(nu_ref[0], n_blocks, fill_wait, 0)


def _dispatch(tcnt, rowstart, pad0, padn, n_used, m_t, h, n_rows):
    T, D = h.shape
    tm = ROUTE_TILE
    col = pl.BlockSpec((8, tm), lambda i, *_: (0, i))
    return pl.pallas_call(
        functools.partial(_dispatch_body, tm=tm, R=GROUP_ROWS),
        grid_spec=pltpu.PrefetchScalarGridSpec(
            num_scalar_prefetch=5,
            grid=(T // tm,),
            in_specs=[col, pl.BlockSpec((tm, D), lambda i, *_: (i, 0))],
            out_specs=pl.BlockSpec(memory_space=pl.ANY),
            scratch_shapes=[pltpu.VMEM((2, GROUP_ROWS, D), BF16), pltpu.VMEM((MOE_BLOCK, D), BF16),
                            pltpu.SemaphoreType.DMA((2,)), pltpu.SemaphoreType.DMA(())]),
        out_shape=jax.ShapeDtypeStruct((n_rows, D), BF16),
        compiler_params=pltpu.CompilerParams(dimension_semantics=("arbitrary",), vmem_limit_bytes=56 * 2**20),
        name="dispatch",
    )(tcnt, rowstart, pad0, padn, n_used, m_t, h)


def _moe_body(be_ref, nu_ref, hv_ref, x_ref, w1g_ref, w1l_ref, b1g_ref, b1l_ref, w2_ref, b2_ref, y_ref):
    del be_ref
    j = pl.program_id(0)
    live = j < nu_ref[0]

    def expert_mlp(rows):
        x = x_ref[rows, :]
        hg = lax.dot_general(x, w1g_ref[0], NT_DIMS, preferred_element_type=F32) + b1g_ref[0]
        hl = lax.dot_general(x, w1l_ref[0], NT_DIMS, preferred_element_type=F32) + b1l_ref[0]
        xg = jnp.minimum(hg, SWIGLU_LIMIT)
        xl = jnp.clip(hl, -SWIGLU_LIMIT, SWIGLU_LIMIT)
        act = xg * _sigmoid(SWIGLU_ALPHA * xg) * (xl + 1.0)
        y = jnp.dot(act.astype(BF16), w2_ref[0].astype(BF16), preferred_element_type=F32) + b2_ref[0]
        y_ref[rows, :] = y.astype(BF16)

    for parts in range(1, MOE_BLOCK // MOE_PART + 1):
        @pl.when(live & (hv_ref[j] == parts))
        def _(parts=parts):
            skip = MOE_BLOCK - parts * MOE_PART
            expert_mlp(slice(skip, MOE_BLOCK))
            if skip:
                y_ref[:skip, :] = jnp.zeros((skip, y_ref.shape[1]), y_ref.dtype)

    @pl.when(jnp.logical_not(live))
    def _():
        y_ref[...] = jnp.zeros_like(y_ref)


def _moe(block_e, n_used, half_block, xin, w1g, w1l, b1, w2, b2):
    n_rows = xin.shape[0]
    D = w2.shape[2]
    F = w2.shape[1]
    nblk = n_rows // MOE_BLOCK
    b1s = jnp.transpose(b1.reshape(N_EXPERTS, F, 2), (2, 0, 1)).reshape(2, N_EXPERTS, 1, F)
    b1spec = lambda k: pl.BlockSpec((None, 1, 1, F), lambda j, be, nu, hv: (k, be[j], 0, 0))
    b2r = b2.reshape(N_EXPERTS, 1, D)
    xspec = pl.BlockSpec((MOE_BLOCK, D), lambda j, be, nu, hv: (jnp.minimum(j, nu[0] - 1), 0))
    yspec = pl.BlockSpec((MOE_BLOCK, D), lambda j, be, nu, hv: (j, 0))
    wspec = lambda a: pl.BlockSpec((1,) + a.shape[1:], lambda j, be, nu, hv: (be[j], 0, 0))
    return pl.pallas_call(
        _moe_body,
        grid_spec=pltpu.PrefetchScalarGridSpec(
            num_scalar_prefetch=3,
            grid=(nblk,),
            in_specs=[xspec, wspec(w1g), wspec(w1l), b1spec(0), b1spec(1), wspec(w2), wspec(b2r)],
            out_specs=yspec),
        out_shape=jax.ShapeDtypeStruct(xin.shape, BF16),
        compiler_params=pltpu.CompilerParams(dimension_semantics=("arbitrary",),
                                             vmem_limit_bytes=48 * 2**20),
        name="moe",
    )(block_e, n_used, half_block, xin, w1g, w1l, b1s, b1s, w2, b2r)


def _combine_body(tc_ref, rs_ref, slab_ref, y_ref, x1_ref, fg_ref, out_ref, g_ref, sem, *, tm, R):
    i = pl.program_id(0)
    slot = i % 2

    def fetch(tile, slot):
        _for_each_run(tc_ref, rs_ref, tile, lambda lr, gr, n: pltpu.make_async_copy(
            y_ref.at[pl.ds(gr, n)], g_ref.at[slot, pl.ds(lr, n)], sem.at[slot]).start())

    @pl.when(i == 0)
    def _():
        g_ref[...] = jnp.zeros_like(g_ref)
        fetch(i, slot)

    @pl.when(i + 1 < pl.num_programs(0))
    def _():
        fetch(i + 1, 1 - slot)

    slab = slab_ref[...]
    gates = slab[:, 0:TOP_K].astype(BF16)
    rows = slab[:, TOP_K:2 * TOP_K].astype(I32).astype(jnp.int16)
    jj = lax.broadcasted_iota(I32, (tm, R), 1).astype(jnp.int16)
    sel = jnp.zeros((tm, R), BF16)
    for k in range(TOP_K):
        sel = jnp.where(jj == rows[:, k:k + 1], gates[:, k:k + 1], sel)

    _wait_groups(_group_count(tc_ref, i), y_ref, g_ref.at[slot], sem.at[slot])
    acc = x1_ref[...] + jnp.dot(sel, g_ref[slot], preferred_element_type=F32)
    out_ref[...] = acc * lax.rsqrt(jnp.mean(acc * acc, axis=-1, keepdims=True) + NORM_EPS) * fg_ref[...]


def _combine(tcnt, rowstart, slab, y, x1, final_g):
    T, D = x1.shape
    tm = ROUTE_TILE
    fg = final_g.reshape(1, D)
    return pl.pallas_call(
        functools.partial(_combine_body, tm=tm, R=GROUP_ROWS),
        grid_spec=pltpu.PrefetchScalarGridSpec(
            num_scalar_prefetch=2,
            grid=(T // tm,),
            in_specs=[pl.BlockSpec((tm, LANES), lambda i, *_: (i, 0)),
                      pl.BlockSpec(memory_space=pl.ANY),
                      pl.BlockSpec((tm, D), lambda i, *_: (i, 0)),
                      pl.BlockSpec((1, D), lambda i, *_: (0, 0))],
            out_specs=pl.BlockSpec((tm, D), lambda i, *_: (i, 0)),
            scratch_shapes=[pltpu.VMEM((2, GROUP_ROWS, D), BF16), pltpu.SemaphoreType.DMA((2,))]),
        out_shape=jax.ShapeDtypeStruct((T, D), F32),
        compiler_params=pltpu.CompilerParams(dimension_semantics=("arbitrary",), vmem_limit_bytes=56 * 2**20),
        name="combine",
    )(tcnt, rowstart, slab, y, x1, fg)


def _mixers(x2, B, S, norm1_g, w_in, rel_bias, wg_f, bg_f, wg_b, bg_b, gla_norm_g, w1):
    a1, a4, a16, gla, bcum, w1g, w1l = _in_proj(x2, B, S, norm1_g, w_in, wg_f, bg_f, wg_b, bg_b, w1)
    branches = [_attn_branch(a, rel_bias, d) for a, d in zip((a1, a4, a16), DILATIONS)]
    g = _gla(gla, bcum, gla_norm_g, B, S)
    return [o for o, _ in branches], [l for _, l in branches], g, w1g, w1l


def _moe_layer(h, m_t, slab, tcnt_f, x1, w1g, w1l, b1, w2, b2, final_g):
    T = x1.shape[0]
    nt = T // ROUTE_TILE
    nblk = (T * TOP_K + nt * N_EXPERTS * (RUN_ALIGN - 1)) // MOE_BLOCK + N_EXPERTS
    tcnt = tcnt_f[:, :, 0].astype(I32)
    rowstart, pad0, padn, block_e, half_block, n_used = _plan(tcnt, nblk)
    xin = _dispatch(tcnt, rowstart, pad0, padn, n_used, m_t, h, nblk * MOE_BLOCK)
    y = _moe(block_e, n_used, half_block, xin, w1g, w1l, b1, w2, b2)
    return _combine(tcnt, rowstart, slab, y, x1, final_g)


def kernel(x, norm1_g, w_in, rel_bias, gla_wg_fwd, gla_bg_fwd, gla_wg_bwd, gla_bg_bwd, gla_norm_g, w_out, norm2_g, router_w, router_b, moe_w1, moe_b1, moe_w2, moe_b2, final_g):
    B, S, D = x.shape
    assert w_in.shape[0] == 1, "one layer"
    x2 = x.reshape(B * S, D)
    outs, lses, g, w1g, w1l = _mixers(x2, B, S, norm1_g[0], w_in, rel_bias, gla_wg_fwd[0], gla_bg_fwd[0],
                                      gla_wg_bwd[0], gla_bg_bwd[0], gla_norm_g[0], moe_w1[0])
    x1, h, m_t, slab, tcnt = _mix(outs, lses, g, x2, B, S, w_out[0], norm2_g[0], router_w[0], router_b[0])
    out = _moe_layer(h, m_t, slab, tcnt, x1, w1g, w1l, moe_b1[0], moe_w2[0], moe_b2[0], final_g)
    return out.reshape(B, S, D)
```

```python
import functools

import jax
import jax.numpy as jnp
import numpy as np
from jax import lax
from jax.experimental import pallas as pl
from jax.experimental.pallas import tpu as pltpu

F32, BF16, I32 = jnp.float32, jnp.bfloat16, jnp.int32

LANES = 128
HEAD_DIM = 64
ATTN_HEADS = 8
ATTN_W = ATTN_HEADS * HEAD_DIM
HALF_SPAN = 64
ATTN_QBLK = 128
ATTN_KWIN = 256
ATTN_STEP_ROWS = 2048
DILATIONS = (1, 4, 16)
T5_BUCKETS = 32
T5_MAX_DISTANCE = 1024
NEG_INF = -1e30
LOG2_E = 1.4426950408889634
LN_2 = 0.6931471805599453
GLA_KEY_W = 256
GLA_VAL_W = 512
GLA_DV = 128
GLA_RANK = 16
GLA_TAU = 16.0
GLA_CHUNK = 64
N_EXPERTS = 32
TOP_K = 4
SWIGLU_LIMIT = 7.0
SWIGLU_ALPHA = 1.702
NORM_EPS = 1e-5
MOE_BLOCK = 1024
MOE_PART = 256
W1_SUBTILE = 512
ROUTE_TILE = 512
RUN_ALIGN = 16
RUN_SHIFT = RUN_ALIGN.bit_length() - 1
GROUP_ROWS = ROUTE_TILE * TOP_K + N_EXPERTS * RUN_ALIGN

ATTN_COLS = 3 * ATTN_W
GLA_COLS = 2 * GLA_KEY_W + 2 * GLA_VAL_W
Z_COLS = LANES
PROJ_COLS = ATTN_COLS + GLA_COLS + Z_COLS

NT_DIMS = (((1,), (1,)), ((), ()))


def _sigmoid(x):
    return 1.0 / (1.0 + jnp.exp(-x))


def _split3(a):
    h1 = a.astype(BF16)
    r1 = a - h1.astype(F32)
    h2 = r1.astype(BF16)
    h3 = (r1 - h2.astype(F32)).astype(BF16)
    return h1, h2, h3


def _dot_f32(a, b):
    ah, al, _ = _split3(a)
    bh, bl, _ = _split3(b)
    d = lambda u, v: jnp.dot(u, v, preferred_element_type=F32)
    return d(ah, bh) + d(al, bh) + d(ah, bl)


def _split_w1_tile(w_ref, g_ref, l_ref, t_ref):
    sub = t_ref.shape[1]
    for s in range(w_ref.shape[2] // sub):
        wt = w_ref[0, :, s * sub:(s + 1) * sub].T
        out_rows = slice(s * sub // 2, (s + 1) * sub // 2)
        for c in range(wt.shape[1] // LANES):
            cols = slice(c * LANES, (c + 1) * LANES)
            t_ref[c] = wt[:, cols]
            g_ref[0, out_rows, cols] = t_ref[c, pl.ds(0, sub // 2, stride=2), :].astype(BF16)
            l_ref[0, out_rows, cols] = t_ref[c, pl.ds(1, sub // 2, stride=2), :].astype(BF16)


def _in_proj_body(x_ref, g_ref, w_ref, wg_ref, bg_ref, w1_ref, a1_ref, a4_ref, a16_ref, gla_ref, b_ref,
                  w1g_ref, w1l_ref, t_ref, wb_ref, wf_ref, wsem, *, tm):
    @pl.when((pl.program_id(0) == 0) & (pl.program_id(1) == 0))
    def _():
        fetch = pltpu.make_async_copy(w_ref.at[0], wf_ref, wsem)
        fetch.start()
        fetch.wait()
        wb_ref[...] = jnp.zeros_like(wb_ref)
        wb_ref[:, :wf_ref.shape[1]] = wf_ref[...].astype(BF16)

    _split_w1_tile(w1_ref, w1g_ref, w1l_ref, t_ref)

    x = x_ref[...]
    n = x * lax.rsqrt(jnp.mean(x * x, axis=-1, keepdims=True) + NORM_EPS) * g_ref[...]
    p = jnp.dot(n.astype(BF16), wb_ref[...], preferred_element_type=F32)
    gla_ref[...] = p[:, ATTN_COLS:ATTN_COLS + GLA_COLS].astype(BF16)

    qkv = p[:, :ATTN_COLS].astype(BF16)
    a1_ref[0, 0] = qkv
    dst = lax.broadcasted_iota(I32, (tm, tm), 0)
    src = lax.broadcasted_iota(I32, (tm, tm), 1)
    for dil, a_ref in ((DILATIONS[1], a4_ref), (DILATIONS[2], a16_ref)):
        per = tm // dil
        perm = (src == (dst % per) * dil + dst // per).astype(F32).astype(BF16)
        moved = jnp.dot(perm, qkv, preferred_element_type=F32).astype(BF16)
        for r in range(dil):
            a_ref[0, r] = moved[r * per:(r + 1) * per]

    z = p[:, ATTN_COLS + GLA_COLS:]
    zz = _dot_f32(z, wg_ref[...]) + bg_ref[...]
    la = (jnp.minimum(zz, 0.0) - jnp.log1p(jnp.exp(-jnp.abs(zz)))) * (1.0 / GLA_TAU)
    C = GLA_CHUNK
    r_i = lax.broadcasted_iota(I32, (C, C), 0)
    c_i = lax.broadcasted_iota(I32, (C, C), 1)
    lower = (r_i >= c_i).astype(BF16)
    upper = (r_i <= c_i).astype(BF16)
    W = GLA_KEY_W
    for j in range(tm // C):
        rows = slice(j * C, (j + 1) * C)
        for tri, cols in ((lower, slice(0, W)), (upper, slice(W, 2 * W))):
            terms = jnp.concatenate(_split3(la[rows, cols]), axis=1)
            cs = jnp.dot(tri, terms, preferred_element_type=F32)
            b_ref[rows, cols] = cs[:, :W] + cs[:, W:2 * W] + cs[:, 2 * W:]


def _in_proj(x2, B, S, norm_g, w_in, wg_f, bg_f, wg_b, bg_b, w1):
    T, D = x2.shape
    tm = 256
    nt = S // tm
    E, _, F2 = w1.shape
    w1_cols = E * F2 // (B * nt)
    per_e = F2 // w1_cols
    assert w1_cols * B * nt == E * F2 and per_e * w1_cols == F2 and w1_cols % W1_SUBTILE == 0
    w1_in = pl.BlockSpec((1, D, w1_cols), lambda b, i: ((b * nt + i) // per_e, 0, (b * nt + i) % per_e))
    w1_out = pl.BlockSpec((1, w1_cols // 2, D), lambda b, i: ((b * nt + i) // per_e, (b * nt + i) % per_e, 0))
    w_spec = pl.BlockSpec(memory_space=pl.ANY)
    wg = jnp.zeros((Z_COLS, 2 * GLA_KEY_W), F32)
    wg = wg.at[:GLA_RANK, :GLA_KEY_W].set(wg_f).at[GLA_RANK:2 * GLA_RANK, GLA_KEY_W:].set(wg_b)
    bg = jnp.concatenate([bg_f, bg_b]).reshape(1, 2 * GLA_KEY_W)
    row = lambda c: pl.BlockSpec((tm, c), lambda b, i: (b * nt + i, 0))
    full = lambda a: pl.BlockSpec(a.shape, lambda b, i: (0,) * a.ndim)
    res = lambda d: pl.BlockSpec((1, d, tm // d, ATTN_COLS), lambda b, i: (b, 0, i, 0))
    res_shape = lambda d: jax.ShapeDtypeStruct((B, d, S // d, ATTN_COLS), BF16)
    g2 = norm_g.reshape(1, D)
    return pl.pallas_call(
        functools.partial(_in_proj_body, tm=tm),
        grid=(B, nt),
        in_specs=[row(D), full(g2), w_spec, full(wg), full(bg), w1_in],
        out_specs=[res(d) for d in DILATIONS] + [row(GLA_COLS), row(2 * GLA_KEY_W), w1_out, w1_out],
        out_shape=[res_shape(d) for d in DILATIONS]
                  + [jax.ShapeDtypeStruct((T, GLA_COLS), BF16), jax.ShapeDtypeStruct((T, 2 * GLA_KEY_W), F32)]
                  + [jax.ShapeDtypeStruct((E, F2 // 2, D), BF16)] * 2,
        scratch_shapes=[pltpu.VMEM((D // LANES, W1_SUBTILE, LANES), F32), pltpu.VMEM((D, PROJ_COLS), BF16),
                        pltpu.VMEM(w_in.shape[1:], F32), pltpu.SemaphoreType.DMA(())],
        compiler_params=pltpu.CompilerParams(dimension_semantics=("arbitrary", "arbitrary"),
                                             vmem_limit_bytes=56 * 2**20),
        name="in_proj",
    )(x2, g2, w_in, wg, bg, w1)


def _t5_bucket(rel):
    nb = T5_BUCKETS // 2
    max_exact = nb // 2
    n = np.abs(rel)
    large = max_exact + (np.log(np.maximum(n, 1).astype(np.float32) / max_exact)
                         / np.log(T5_MAX_DISTANCE / max_exact) * (nb - max_exact)).astype(np.int32)
    large = np.minimum(large, nb - 1)
    return (np.where(rel > 0, nb, 0) + np.where(n < max_exact, n, large)).astype(np.int32)


def _bucket_tables(dil):
    s = np.arange(ATTN_QBLK)[:, None]
    t = np.arange(ATTN_KWIN)[None, :]
    tabs = []
    for c in (0, HALF_SPAN, 2 * HALF_SPAN):
        off = t - c - s
        tabs.append(np.where(np.abs(off) <= HALF_SPAN, _t5_bucket(off * dil), -1))
    return np.stack(tabs).astype(np.int32)


def _attn_body(rb_ref, idx_ref, q_ref, k_ref, v_ref, o_ref, lse_ref, tab_ref, *, L, TQ):
    first = (pl.program_id(0) == 0) & (pl.program_id(1) == 0) & (pl.program_id(2) == 0)

    @pl.when(first)
    def _():
        for v in range(3):
            idx = idx_ref[v]
            for h in range(ATTN_HEADS):
                fill = lambda j, t, h=h, idx=idx: jnp.where(idx == j, rb_ref[j, h] * LOG2_E, t)
                tab_ref[v, h] = lax.fori_loop(0, T5_BUCKETS, fill, jnp.where(idx < 0, NEG_INF, 0.0).astype(F32))

    i = pl.program_id(2)
    lo = lax.broadcasted_iota(I32, (ATTN_QBLK, LANES), 1) < HEAD_DIM
    scale = HEAD_DIM ** -0.5 * LOG2_E
    lane = lax.broadcasted_iota(I32, (ATTN_QBLK, LANES), 1)
    for rb, sb in [(rb, sb) for rb in range(q_ref.shape[0]) for sb in range(TQ // ATTN_QBLK)]:
        l0 = i * TQ + sb * ATTN_QBLK
        lse_tile = jnp.zeros((ATTN_QBLK, LANES), F32)
        ws = pl.multiple_of(jnp.clip(l0 - HALF_SPAN, 0, L - ATTN_KWIN), HALF_SPAN)
        var = jnp.where(l0 == 0, 0, jnp.where(l0 == L - ATTN_QBLK, 2, 1))
        rows = slice(sb * ATTN_QBLK, (sb + 1) * ATTN_QBLK)
        for hp in range(ATTN_HEADS // 2):
            cols = slice(hp * LANES, (hp + 1) * LANES)
            q2 = q_ref[rb, rows, cols]
            k2 = k_ref[rb, pl.ds(ws, ATTN_KWIN), cols]
            v2 = v_ref[rb, pl.ds(ws, ATTN_KWIN), cols]
            zero = jnp.zeros_like(q2)
            qs = jnp.concatenate([jnp.where(lo, q2, zero), jnp.where(lo, zero, q2)], axis=0)
            s = lax.dot_general(qs, k2, NT_DIMS, preferred_element_type=F32)
            bias = jnp.concatenate([tab_ref[var, 2 * hp], tab_ref[var, 2 * hp + 1]], axis=0)
            logits = s * scale + bias
            m = jnp.max(logits, axis=-1, keepdims=True)
            p = jnp.exp2(logits - m)
            den = jnp.sum(p, axis=-1, keepdims=True)
            pv = jnp.dot(p.astype(BF16), v2, preferred_element_type=F32) / den
            lse = m * LN_2 + jnp.log(den)
            o_ref[rb, rows, cols] = jnp.where(lo, pv[:ATTN_QBLK], pv[ATTN_QBLK:]).astype(BF16)
            lse_tile = jnp.where(lane == 2 * hp, lse[:ATTN_QBLK], lse_tile)
            lse_tile = jnp.where(lane == 2 * hp + 1, lse[ATTN_QBLK:], lse_tile)
        lse_ref[rb, rows, :] = lse_tile


def _attn_branch(attn, rel_bias, dil):
    B, _, L, _ = attn.shape
    TQ = min(ATTN_STEP_ROWS, L)
    RB = ATTN_STEP_ROWS // TQ
    idx = jnp.asarray(_bucket_tables(dil))
    body = functools.partial(_attn_body, L=L, TQ=TQ)
    kv = lambda j: pl.BlockSpec((None, RB, L, ATTN_W), lambda b, r, i: (b, r, 0, j))
    q_spec = pl.BlockSpec((None, RB, TQ, ATTN_W), lambda b, r, i: (b, r, i, 0))
    return pl.pallas_call(
        body,
        grid=(B, dil // RB, L // TQ),
        in_specs=[pl.BlockSpec(memory_space=pltpu.SMEM),
                  pl.BlockSpec(idx.shape, lambda b, r, i: (0, 0, 0)),
                  q_spec, kv(1), kv(2)],
        out_specs=[q_spec, pl.BlockSpec((None, RB, TQ, LANES), lambda b, r, i: (b, r, i, 0))],
        out_shape=[jax.ShapeDtypeStruct((B, dil, L, ATTN_W), BF16),
                   jax.ShapeDtypeStruct((B, dil, L, LANES), F32)],
        scratch_shapes=[pltpu.VMEM((3, ATTN_HEADS, ATTN_QBLK, ATTN_KWIN), F32)],
        compiler_params=pltpu.CompilerParams(dimension_semantics=("arbitrary",) * 3,
                                             vmem_limit_bytes=48 * 2**20),
        name=f"attn_d{dil}",
    )(rel_bias, idx, attn, attn, attn)


def _gla_body(q_ref, k_ref, v_ref, gate_ref, bf_ref, bb_ref, ng_ref, out_ref, of_ref, ob_ref, st_ref, *, S):
    C = GLA_CHUNK
    nc = S // C
    r = lax.broadcasted_iota(I32, (C, LANES), 0)
    s2 = lax.broadcasted_iota(I32, (C, LANES), 1) % HEAD_DIM
    lo = lax.broadcasted_iota(I32, (C, LANES), 1) < HEAD_DIM
    blockdiag = ((lax.broadcasted_iota(I32, (2 * GLA_DV, LANES), 0) < GLA_DV)
                 == (lax.broadcasted_iota(I32, (2 * GLA_DV, LANES), 1) < HEAD_DIM))
    st_ref[...] = jnp.zeros_like(st_ref)

    def one_chunk(ci, direction):
        mask = (r >= s2) if direction == 0 else (r <= s2)
        b_ref, o_ref = (bf_ref, of_ref) if direction == 0 else (bb_ref, ob_ref)
        rows = pl.ds(pl.multiple_of(ci * C, C), C)
        q = q_ref[0, rows, :].astype(F32) * (HEAD_DIM ** -0.5)
        k = k_ref[0, rows, :].astype(F32)
        v = v_ref[0, rows, :]
        b = b_ref[0, rows, :]
        g_tot = b[C - 1:C, :] if direction == 0 else b[0:1, :]
        qd = (q * jnp.exp(b)).astype(BF16)
        kd = (k * jnp.exp(-b)).astype(BF16)
        kst = (k * jnp.exp(g_tot - b)).astype(BF16)
        zk = jnp.zeros_like(kd)
        ks = jnp.concatenate([jnp.where(lo, kd, zk), jnp.where(lo, zk, kd)], axis=0)
        a = lax.dot_general(qd, ks, NT_DIMS, preferred_element_type=F32)
        a = jnp.where(mask, a, 0.0).astype(BF16)
        zv = jnp.zeros((C, GLA_DV), BF16)
        vblk = jnp.concatenate([jnp.concatenate([v[:, :GLA_DV], zv], axis=1),
                                jnp.concatenate([zv, v[:, GLA_DV:]], axis=1)], axis=0)
        st = st_ref[direction]
        o_ref[rows, :] = (jnp.dot(a, vblk, preferred_element_type=F32)
                          + lax.dot_general(qd, st.astype(BF16), NT_DIMS, preferred_element_type=F32))
        d_st = jnp.dot(v.astype(F32).T.astype(BF16), kst, preferred_element_type=F32)
        st_ref[direction] = st * jnp.exp(g_tot) + jnp.where(blockdiag, d_st, 0.0)

    def step(i, carry):
        one_chunk(i, 0)
        one_chunk(nc - 1 - i, 1)
        return carry

    lax.fori_loop(0, nc, step, 0, unroll=16)

    tr = 256
    def epilogue(i, carry):
        rows = pl.ds(pl.multiple_of(i * tr, tr), tr)
        for hh in range(2):
            cols = slice(hh * GLA_DV, (hh + 1) * GLA_DV)
            o = of_ref[rows, cols] + ob_ref[rows, cols]
            o = o * lax.rsqrt(jnp.mean(o * o, axis=-1, keepdims=True) + NORM_EPS) * ng_ref[0, :, cols]
            gate = gate_ref[0, rows, cols].astype(F32)
            out_ref[0, rows, cols] = (o * (gate * _sigmoid(gate))).astype(BF16)
        return carry

    lax.fori_loop(0, S // tr, epilogue, 0)


def _gla(gla, bcum, norm_g, B, S):
    g3 = gla.reshape(B, S, GLA_COLS)
    b3 = bcum.reshape(B, S, 2 * GLA_KEY_W)
    ng = norm_g.reshape(2, 1, 2 * GLA_DV)
    body = functools.partial(_gla_body, S=S)
    spec = lambda w, off: pl.BlockSpec((1, S, w), lambda b, p: (b, 0, off + p))
    out = pl.pallas_call(
        body,
        grid=(B, 2),
        in_specs=[spec(LANES, 0),
                  spec(LANES, GLA_KEY_W // LANES),
                  spec(2 * GLA_DV, (2 * GLA_KEY_W) // (2 * GLA_DV)),
                  spec(2 * GLA_DV, (2 * GLA_KEY_W + GLA_VAL_W) // (2 * GLA_DV)),
                  spec(LANES, 0),
                  spec(LANES, GLA_KEY_W // LANES),
                  pl.BlockSpec((1, 1, 2 * GLA_DV), lambda b, p: (p, 0, 0))],
        out_specs=pl.BlockSpec((1, S, 2 * GLA_DV), lambda b, p: (b, 0, p)),
        out_shape=jax.ShapeDtypeStruct((B, S, GLA_VAL_W), BF16),
        scratch_shapes=[pltpu.VMEM((S, 2 * GLA_DV), F32), pltpu.VMEM((S, 2 * GLA_DV), F32),
                        pltpu.VMEM((2, 2 * GLA_DV, LANES), F32)],
        compiler_params=pltpu.CompilerParams(dimension_semantics=("arbitrary", "arbitrary"),
                                             vmem_limit_bytes=56 * 2**20),
        name="gla",
    )(g3, g3, g3, g3, b3, b3, ng)
    return out.reshape(B * S, GLA_VAL_W)


def _natural_rows(blk_ref, u_ref, dil, tm):
    if dil == 1:
        return blk_ref[0, 0].astype(F32)
    chunks = blk_ref.shape[-1] // LANES
    for r in range(dil):
        for c in range(chunks):
            u_ref[c, pl.ds(r, tm // dil, stride=dil), :] = blk_ref[0, r, :, c * LANES:(c + 1) * LANES].astype(F32)
    return jnp.concatenate([u_ref[c] for c in range(chunks)], axis=1)


def _per_head_lanes(w):
    head = lax.broadcasted_iota(I32, (LANES, ATTN_W), 0)
    col = lax.broadcasted_iota(I32, (LANES, ATTN_W), 1)
    spread = (col // HEAD_DIM == head).astype(F32).astype(BF16)
    hi, lo, _ = _split3(w)
    return (jnp.dot(hi, spread, preferred_element_type=F32) + jnp.dot(lo, spread, preferred_element_type=F32))


def _mix_body(o1, o2, o3, l1, l2, l3, g_ref, x_ref, wo_ref, n2_ref, rwh_ref, rwl_ref, rb_ref,
              x1_ref, h_ref, m_ref, slab_ref, tcnt_ref, u_ref, wob_ref, *, tm):
    @pl.when((pl.program_id(0) == 0) & (pl.program_id(1) == 0))
    def _():
        wob_ref[...] = wo_ref[...].astype(BF16)

    ls = [_natural_rows(l, u_ref, d, tm) for l, d in zip((l1, l2, l3), DILATIONS)]
    m = jnp.maximum(jnp.maximum(ls[0], ls[1]), ls[2])
    es = [jnp.exp(l - m) for l in ls]
    den = es[0] + es[1] + es[2]
    a = _per_head_lanes(es[0] / den) * _natural_rows(o1, u_ref, DILATIONS[0], tm)
    a = a + _per_head_lanes(es[1] / den) * _natural_rows(o2, u_ref, DILATIONS[1], tm)
    a = a + _per_head_lanes(es[2] / den) * _natural_rows(o3, u_ref, DILATIONS[2], tm)
    mix = (jnp.dot(a.astype(BF16), wob_ref[:ATTN_W, :], preferred_element_type=F32)
           + jnp.dot(g_ref[...], wob_ref[ATTN_W:, :], preferred_element_type=F32))
    x1 = x_ref[...] + mix
    x1_ref[...] = x1
    h = x1 * lax.rsqrt(jnp.mean(x1 * x1, axis=-1, keepdims=True) + NORM_EPS) * n2_ref[...]
    h_ref[...] = h.astype(BF16)

    hh, hl, _ = _split3(h)
    dg = lambda u, v: lax.dot_general(u, v, NT_DIMS, preferred_element_type=F32)
    logits = dg(rwh_ref[...], hh) + dg(rwl_ref[...], hh) + dg(rwh_ref[...], hl) + rb_ref[...]
    ie = lax.broadcasted_iota(I32, (N_EXPERTS, tm), 0)
    cur = logits
    vals, idxs = [], []
    for _ in range(TOP_K):
        mx = jnp.max(cur, axis=0, keepdims=True)
        ix = jnp.min(jnp.where(cur == mx, ie, N_EXPERTS), axis=0, keepdims=True)
        vals.append(mx)
        idxs.append(ix)
        cur = jnp.where(ie == ix, -jnp.inf, cur)
    ex = [jnp.exp(v - vals[0]) for v in vals]
    den = ex[0] + ex[1] + ex[2] + ex[3]
    gates = [e / den for e in ex]

    onehots = [ie == ix for ix in idxs]
    chosen = (onehots[0] | onehots[1] | onehots[2] | onehots[3]).astype(F32)
    tr = lax.broadcasted_iota(I32, (tm, tm), 0)
    tc = lax.broadcasted_iota(I32, (tm, tm), 1)
    before = jnp.dot(chosen.astype(BF16), (tr < tc).astype(BF16), preferred_element_type=F32)
    count = jnp.broadcast_to(jnp.sum(chosen, axis=1, keepdims=True), (N_EXPERTS, LANES))
    tcnt_ref[0] = count
    run = jnp.floor((count + (RUN_ALIGN - 1.0)) * (1.0 / RUN_ALIGN)) * RUN_ALIGN
    er = lax.broadcasted_iota(I32, (N_EXPERTS, N_EXPERTS), 0)
    ec = lax.broadcasted_iota(I32, (N_EXPERTS, N_EXPERTS), 1)
    run_start = jnp.dot((ec < er).astype(BF16), run.astype(BF16), preferred_element_type=F32)
    where_ = before + run_start[:, 0:1]
    locs = [jnp.sum(jnp.where(oh, where_, 0.0), axis=0, keepdims=True) for oh in onehots]

    def pack(rows, n):
        sub = lax.broadcasted_iota(I32, (n, tm), 0)
        out = jnp.zeros((n, tm), rows[0].dtype)
        for k, row in enumerate(rows):
            out = jnp.where(sub == k, row, out)
        return out

    m_ref[...] = pack([l.astype(I32) for l in locs], 8)
    slab = pack(gates + locs, 8)
    slab_ref[...] = jnp.concatenate([slab, jnp.zeros((LANES - 8, tm), F32)], axis=0).T


def _mix(outs, lses, g, x2, B, S, w_out, norm2_g, router_w, router_b):
    T, D = x2.shape
    tm = ROUTE_TILE
    nt = S // tm
    wo_spec = pl.BlockSpec(w_out.shape, lambda b, i: (0, 0), pipeline_mode=pl.Buffered(1))
    rwt = router_w.T
    rwh = rwt.astype(BF16)
    rwl = (rwt - rwh.astype(F32)).astype(BF16)
    rb = router_b.reshape(N_EXPERTS, 1)
    n2 = norm2_g.reshape(1, D)
    row = lambda c: pl.BlockSpec((tm, c), lambda b, i: (b * nt + i, 0))
    full = lambda a: pl.BlockSpec(a.shape, lambda b, i: (0,) * a.ndim)
    col = pl.BlockSpec((8, tm), lambda b, i: (0, b * nt + i))
    res = [pl.BlockSpec((1, d, tm // d, ATTN_W), lambda b, i: (b, 0, i, 0)) for d in DILATIONS]
    res_lse = [pl.BlockSpec((1, d, tm // d, LANES), lambda b, i: (b, 0, i, 0)) for d in DILATIONS]
    return pl.pallas_call(
        functools.partial(_mix_body, tm=tm),
        grid=(B, nt),
        in_specs=res + res_lse + [row(GLA_VAL_W), row(D), wo_spec, full(n2), full(rwh), full(rwl), full(rb)],
        out_specs=[row(D), row(D), col, row(LANES),
                   pl.BlockSpec((1, N_EXPERTS, LANES), lambda b, i: (b * nt + i, 0, 0))],
        out_shape=[jax.ShapeDtypeStruct((T, D), F32),
                   jax.ShapeDtypeStruct((T, D), BF16),
                   jax.ShapeDtypeStruct((8, T), I32),
                   jax.ShapeDtypeStruct((T, LANES), F32),
                   jax.ShapeDtypeStruct((T // tm, N_EXPERTS, LANES), F32)],
        scratch_shapes=[pltpu.VMEM((ATTN_W // LANES, tm, LANES), F32), pltpu.VMEM(w_out.shape, BF16)],
        compiler_params=pltpu.CompilerParams(dimension_semantics=("arbitrary", "arbitrary"),
                                             vmem_limit_bytes=48 * 2**20),
        name="mix_router",
    )(*outs, *lses, g, x2, w_out, n2, rwh, rwl, rb)


def _run_groups(c):
    return (c + (RUN_ALIGN - 1)) >> RUN_SHIFT


def _plan_body(tc_ref, rs_ref, p0_ref, pn_ref, be_ref, hv_ref, nu_ref, *, nt, nblk):
    shift = MOE_BLOCK.bit_length() - 1
    start = jnp.int32(0)
    for ex in range(N_EXPERTS):
        rows_of = lambda i, ex=ex: _run_groups(tc_ref[i, ex]) << RUN_SHIFT
        tot = lax.fori_loop(0, nt, lambda i, tot: tot + rows_of(i), jnp.int32(0), unroll=True)
        nb = (tot + (MOE_BLOCK - 1)) >> shift
        pad = (nb << shift) - tot
        p0_ref[ex] = start
        pn_ref[ex] = pad >> RUN_SHIFT

        def run(i, at, ex=ex):
            rs_ref[i, ex] = at
            return at + rows_of(i)

        lax.fori_loop(0, nt, run, start + pad, unroll=True)
        b0 = start >> shift

        def fill(j, carry, b0=b0, ex=ex):
            be_ref[b0 + j] = jnp.int32(ex)
            hv_ref[b0 + j] = jnp.int32(MOE_BLOCK // MOE_PART)
            return carry

        lax.fori_loop(0, nb, fill, 0)

        @pl.when(nb > 0)
        def _(b0=b0, pad=pad):
            hv_ref[b0] = (MOE_BLOCK - pad + (MOE_PART - 1)) >> (MOE_PART.bit_length() - 1)

        start = start + (nb << shift)
    used = start >> shift

    def tail(j, carry):
        be_ref[j] = jnp.int32(N_EXPERTS - 1)
        hv_ref[j] = jnp.int32(0)
        return carry

    lax.fori_loop(used, nblk, tail, 0)
    nu_ref[0] = used


def _plan(tcnt, nblk):
    nt = tcnt.shape[0]
    smem = pl.BlockSpec(memory_space=pltpu.SMEM)
    return pl.pallas_call(
        functools.partial(_plan_body, nt=nt, nblk=nblk),
        in_specs=[smem],
        out_specs=[smem] * 6,
        out_shape=[jax.ShapeDtypeStruct(tcnt.shape, I32),
                   jax.ShapeDtypeStruct((N_EXPERTS,), I32),
                   jax.ShapeDtypeStruct((N_EXPERTS,), I32),
                   jax.ShapeDtypeStruct((nblk,), I32),
                   jax.ShapeDtypeStruct((nblk,), I32),
                   jax.ShapeDtypeStruct((1,), I32)],
        name="plan",
    )(tcnt)


def _for_each_run(tc_ref, rs_ref, i, fn):
    big = 4 * RUN_ALIGN
    start = jnp.int32(0)
    for ex in range(N_EXPERTS):
        ng = _run_groups(tc_ref[i, ex])
        g0 = rs_ref[i, ex]

        def piece(q, carry, start=start, g0=g0):
            fn(pl.multiple_of(start + q * big, RUN_ALIGN), pl.multiple_of(g0 + q * big, RUN_ALIGN), big)
            return carry

        def single(q, carry, start=start, g0=g0):
            fn(pl.multiple_of(start + q * RUN_ALIGN, RUN_ALIGN), pl.multiple_of(g0 + q * RUN_ALIGN, RUN_ALIGN),
               RUN_ALIGN)
            return carry

        lax.fori_loop(0, ng >> 2, piece, 0)
        lax.fori_loop((ng >> 2) << 2, ng, single, 0)
        start = start + (ng << RUN_SHIFT)


def _group_count(tc_ref, i):
    n = jnp.int32(0)
    for ex in range(N_EXPERTS):
        n = n + _run_groups(tc_ref[i, ex])
    return n


def _wait_groups(n, src_ref, dst_ref, sem):
    def waits(count, rows):
        def body(q, carry):
            pltpu.make_async_copy(src_ref.at[pl.ds(0, rows)], dst_ref.at[pl.ds(0, rows)], sem).wait()
            return carry
        lax.fori_loop(0, count, body, 0)

    waits(n >> 6, 64 * RUN_ALIGN)
    waits((n >> 3) & 7, 8 * RUN_ALIGN)
    waits(n & 7, RUN_ALIGN)


def _dispatch_body(tc_ref, rs_ref, p0_ref, pn_ref, nu_ref, m_ref, h_ref, xin_ref, g_ref, z_ref, sem, zsem, *, tm, R):
    i = pl.program_id(0)
    slot = i % 2
    rows = m_ref[...].astype(jnp.int16)
    jj = lax.broadcasted_iota(I32, (R, tm), 0).astype(jnp.int16)
    hit = (jj == rows[0:1]) | (jj == rows[1:2]) | (jj == rows[2:3]) | (jj == rows[3:4])
    onehot = jnp.where(hit, jnp.ones((), BF16), jnp.zeros((), BF16))
    g_ref[slot] = jnp.dot(onehot, h_ref[...], preferred_element_type=F32).astype(BF16)

    group = lambda ref, r: ref.at[pl.ds(r, RUN_ALIGN)]

    @pl.when(i > 0)
    def _():
        _wait_groups(_group_count(tc_ref, i - 1), g_ref.at[1 - slot], xin_ref, sem.at[1 - slot])

    _for_each_run(tc_ref, rs_ref, i, lambda lr, gr, n: pltpu.make_async_copy(
        g_ref.at[slot, pl.ds(lr, n)], xin_ref.at[pl.ds(gr, n)], sem.at[slot]).start())

    @pl.when(i == pl.num_programs(0) - 1)
    def _():
        _wait_groups(_group_count(tc_ref, i), g_ref.at[slot], xin_ref, sem.at[slot])

    @pl.when(i == 0)
    def _():
        z_ref[...] = jnp.zeros_like(z_ref)
        total = jnp.int32(0)
        for ex in range(N_EXPERTS):
            def body(q, carry, ex=ex):
                pltpu.make_async_copy(group(z_ref, 0), group(xin_ref, pl.multiple_of(p0_ref[ex] + q * RUN_ALIGN, RUN_ALIGN)), zsem).start()
                return carry
            lax.fori_loop(0, pn_ref[ex], body, 0)
            total = total + pn_ref[ex]
        _wait_groups(total, z_ref, xin_ref, zsem)

        block = lambda j: xin_ref.at[pl.ds(pl.multiple_of(j * MOE_BLOCK, MOE_BLOCK), MOE_BLOCK)]
        n_blocks = xin_ref.shape[0] // MOE_BLOCK

        def fill(j, carry):
            pltpu.make_async_copy(z_ref, block(j), zsem).start()
            return carry

        def fill_wait(j, carry):
            pltpu.make_async_copy(z_ref, block(j), zsem).wait()
            return carry

        lax.fori_loop(nu_ref[0], n_blocks, fill, 0)
        lax.fori_loop(nu_ref[0], n_blocks, fill_wait, 0)


def _dispatch(tcnt, rowstart, pad0, padn, n_used, m_t, h, n_rows):
    T, D = h.shape
    tm = ROUTE_TILE
    col = pl.BlockSpec((8, tm), lambda i, *_: (0, i))
    return pl.pallas_call(
        functools.partial(_dispatch_body, tm=tm, R=GROUP_ROWS),
        grid_spec=pltpu.PrefetchScalarGridSpec(
            num_scalar_prefetch=5,
            grid=(T // tm,),
            in_specs=[col, pl.BlockSpec((tm, D), lambda i, *_: (i, 0))],
            out_specs=pl.BlockSpec(memory_space=pl.ANY),
            scratch_shapes=[pltpu.VMEM((2, GROUP_ROWS, D), BF16), pltpu.VMEM((MOE_BLOCK, D), BF16),
                            pltpu.SemaphoreType.DMA((2,)), pltpu.SemaphoreType.DMA(())]),
        out_shape=jax.ShapeDtypeStruct((n_rows, D), BF16),
        compiler_params=pltpu.CompilerParams(dimension_semantics=("arbitrary",), vmem_limit_bytes=56 * 2**20),
        name="dispatch",
    )(tcnt, rowstart, pad0, padn, n_used, m_t, h)


def _moe_body(be_ref, nu_ref, hv_ref, x_ref, w1g_ref, w1l_ref, b1g_ref, b1l_ref, w2_ref, b2_ref, y_ref):
    del be_ref
    j = pl.program_id(0)
    live = j < nu_ref[0]

    def expert_mlp(rows):
        x = x_ref[rows, :]
        hg = lax.dot_general(x, w1g_ref[0], NT_DIMS, preferred_element_type=F32) + b1g_ref[0]
        hl = lax.dot_general(x, w1l_ref[0], NT_DIMS, preferred_element_type=F32) + b1l_ref[0]
        xg = jnp.minimum(hg, SWIGLU_LIMIT)
        xl = jnp.clip(hl, -SWIGLU_LIMIT, SWIGLU_LIMIT)
        act = xg * _sigmoid(SWIGLU_ALPHA * xg) * (xl + 1.0)
        y = jnp.dot(act.astype(BF16), w2_ref[0].astype(BF16), preferred_element_type=F32) + b2_ref[0]
        y_ref[rows, :] = y.astype(BF16)

    for parts in range(1, MOE_BLOCK // MOE_PART + 1):
        @pl.when(live & (hv_ref[j] == parts))
        def _(parts=parts):
            skip = MOE_BLOCK - parts * MOE_PART
            expert_mlp(slice(skip, MOE_BLOCK))
            if skip:
                y_ref[:skip, :] = jnp.zeros((skip, y_ref.shape[1]), y_ref.dtype)

    @pl.when(jnp.logical_not(live))
    def _():
        y_ref[...] = jnp.zeros_like(y_ref)


def _moe(block_e, n_used, half_block, xin, w1g, w1l, b1, w2, b2):
    n_rows = xin.shape[0]
    D = w2.shape[2]
    F = w2.shape[1]
    nblk = n_rows // MOE_BLOCK
    b1s = jnp.transpose(b1.reshape(N_EXPERTS, F, 2), (2, 0, 1)).reshape(2, N_EXPERTS, 1, F)
    b1spec = lambda k: pl.BlockSpec((None, 1, 1, F), lambda j, be, nu, hv: (k, be[j], 0, 0))
    b2r = b2.reshape(N_EXPERTS, 1, D)
    xspec = pl.BlockSpec((MOE_BLOCK, D), lambda j, be, nu, hv: (jnp.minimum(j, nu[0] - 1), 0))
    yspec = pl.BlockSpec((MOE_BLOCK, D), lambda j, be, nu, hv: (j, 0))
    wspec = lambda a: pl.BlockSpec((1,) + a.shape[1:], lambda j, be, nu, hv: (be[j], 0, 0))
    return pl.pallas_call(
        _moe_body,
        grid_spec=pltpu.PrefetchScalarGridSpec(
            num_scalar_prefetch=3,
            grid=(nblk,),
            in_specs=[xspec, wspec(w1g), wspec(w1l), b1spec(0), b1spec(1), wspec(w2), wspec(b2r)],
            out_specs=yspec),
        out_shape=jax.ShapeDtypeStruct(xin.shape, BF16),
        compiler_params=pltpu.CompilerParams(dimension_semantics=("arbitrary",),
                                             vmem_limit_bytes=48 * 2**20),
        name="moe",
    )(block_e, n_used, half_block, xin, w1g, w1l, b1s, b1s, w2, b2r)


def _combine_body(tc_ref, rs_ref, slab_ref, y_ref, x1_ref, fg_ref, out_ref, g_ref, sem, *, tm, R):
    i = pl.program_id(0)
    slot = i % 2

    def fetch(tile, slot):
        _for_each_run(tc_ref, rs_ref, tile, lambda lr, gr, n: pltpu.make_async_copy(
            y_ref.at[pl.ds(gr, n)], g_ref.at[slot, pl.ds(lr, n)], sem.at[slot]).start())

    @pl.when(i == 0)
    def _():
        g_ref[...] = jnp.zeros_like(g_ref)
        fetch(i, slot)

    @pl.when(i + 1 < pl.num_programs(0))
    def _():
        fetch(i + 1, 1 - slot)

    slab = slab_ref[...]
    gates = slab[:, 0:TOP_K].astype(BF16)
    rows = slab[:, TOP_K:2 * TOP_K].astype(I32).astype(jnp.int16)
    jj = lax.broadcasted_iota(I32, (tm, R), 1).astype(jnp.int16)
    sel = jnp.zeros((tm, R), BF16)
    for k in range(TOP_K):
        sel = jnp.where(jj == rows[:, k:k + 1], gates[:, k:k + 1], sel)

    _wait_groups(_group_count(tc_ref, i), y_ref, g_ref.at[slot], sem.at[slot])
    acc = x1_ref[...] + jnp.dot(sel, g_ref[slot], preferred_element_type=F32)
    out_ref[...] = acc * lax.rsqrt(jnp.mean(acc * acc, axis=-1, keepdims=True) + NORM_EPS) * fg_ref[...]


def _combine(tcnt, rowstart, slab, y, x1, final_g):
    T, D = x1.shape
    tm = ROUTE_TILE
    fg = final_g.reshape(1, D)
    return pl.pallas_call(
        functools.partial(_combine_body, tm=tm, R=GROUP_ROWS),
        grid_spec=pltpu.PrefetchScalarGridSpec(
            num_scalar_prefetch=2,
            grid=(T // tm,),
            in_specs=[pl.BlockSpec((tm, LANES), lambda i, *_: (i, 0)),
                      pl.BlockSpec(memory_space=pl.ANY),
                      pl.BlockSpec((tm, D), lambda i, *_: (i, 0)),
                      pl.BlockSpec((1, D), lambda i, *_: (0, 0))],
            out_specs=pl.BlockSpec((tm, D), lambda i, *_: (i, 0)),
            scratch_shapes=[pltpu.VMEM((2, GROUP_ROWS, D), BF16), pltpu.SemaphoreType.DMA((2,))]),
        out_shape=jax.ShapeDtypeStruct((T, D), F32),
        compiler_params=pltpu.CompilerParams(dimension_semantics=("arbitrary",), vmem_limit_bytes=56 * 2**20),
        name="combine",
    )(tcnt, rowstart, slab, y, x1, fg)


def _mixers(x2, B, S, norm1_g, w_in, rel_bias, wg_f, bg_f, wg_b, bg_b, gla_norm_g, w1):
    a1, a4, a16, gla, bcum, w1g, w1l = _in_proj(x2, B, S, norm1_g, w_in, wg_f, bg_f, wg_b, bg_b, w1)
    branches = [_attn_branch(a, rel_bias, d) for a, d in zip((a1, a4, a16), DILATIONS)]
    g = _gla(gla, bcum, gla_norm_g, B, S)
    return [o for o, _ in branches], [l for _, l in branches], g, w1g, w1l


def _moe_layer(h, m_t, slab, tcnt_f, x1, w1g, w1l, b1, w2, b2, final_g):
    T = x1.shape[0]
    nt = T // ROUTE_TILE
    nblk = (T * TOP_K + nt * N_EXPERTS * (RUN_ALIGN - 1)) // MOE_BLOCK + N_EXPERTS
    tcnt = tcnt_f[:, :, 0].astype(I32)
    rowstart, pad0, padn, block_e, half_block, n_used = _plan(tcnt, nblk)
    xin = _dispatch(tcnt, rowstart, pad0, padn, n_used, m_t, h, nblk * MOE_BLOCK)
    y = _moe(block_e, n_used, half_block, xin, w1g, w1l, b1, w2, b2)
    return _combine(tcnt, rowstart, slab, y, x1, final_g)


def kernel(x, norm1_g, w_in, rel_bias, gla_wg_fwd, gla_bg_fwd, gla_wg_bwd, gla_bg_bwd, gla_norm_g, w_out, norm2_g, router_w, router_b, moe_w1, moe_b1, moe_w2, moe_b2, final_g):
    B, S, D = x.shape
    assert w_in.shape[0] == 1, "one layer"
    x2 = x.reshape(B * S, D)
    outs, lses, g, w1g, w1l = _mixers(x2, B, S, norm1_g[0], w_in, rel_bias, gla_wg_fwd[0], gla_bg_fwd[0],
                                      gla_wg_bwd[0], gla_bg_bwd[0], gla_norm_g[0], moe_w1[0])
    x1, h, m_t, slab, tcnt = _mix(outs, lses, g, x2, B, S, w_out[0], norm2_g[0], router_w[0], router_b[0])
    out = _moe_layer(h, m_t, slab, tcnt, x1, w1g, w1l, moe_b1[0], moe_w2[0], moe_b2[0], final_g)
    return out.reshape(B, S, D)
```

```python
import functools

import jax
import jax.numpy as jnp
import numpy as np
from jax import lax
from jax.experimental import pallas as pl
from jax.experimental.pallas import tpu as pltpu

F32, BF16, I32 = jnp.float32, jnp.bfloat16, jnp.int32

LANES = 128
HEAD_DIM = 64
ATTN_HEADS = 8
ATTN_W = ATTN_HEADS * HEAD_DIM
HALF_SPAN = 64
ATTN_QBLK = 128
ATTN_KWIN = 256
ATTN_STEP_ROWS = 2048
DILATIONS = (1, 4, 16)
T5_BUCKETS = 32
T5_MAX_DISTANCE = 1024
NEG_INF = -1e30
LOG2_E = 1.4426950408889634
LN_2 = 0.6931471805599453
GLA_KEY_W = 256
GLA_VAL_W = 512
GLA_DV = 128
GLA_RANK = 16
GLA_TAU = 16.0
GLA_CHUNK = 64
N_EXPERTS = 32
TOP_K = 4
SWIGLU_LIMIT = 7.0
SWIGLU_ALPHA = 1.702
NORM_EPS = 1e-5
MOE_BLOCK = 1024
MOE_PART = 256
W1_SUBTILE = 512
ROUTE_TILE = 512
RUN_ALIGN = 16
RUN_SHIFT = RUN_ALIGN.bit_length() - 1
GROUP_ROWS = ROUTE_TILE * TOP_K + N_EXPERTS * RUN_ALIGN

ATTN_COLS = 3 * ATTN_W
GLA_COLS = 2 * GLA_KEY_W + 2 * GLA_VAL_W
Z_COLS = LANES
PROJ_COLS = ATTN_COLS + GLA_COLS + Z_COLS

NT_DIMS = (((1,), (1,)), ((), ()))


def _sigmoid(x):
    return 1.0 / (1.0 + jnp.exp(-x))


def _split3(a):
    h1 = a.astype(BF16)
    r1 = a - h1.astype(F32)
    h2 = r1.astype(BF16)
    h3 = (r1 - h2.astype(F32)).astype(BF16)
    return h1, h2, h3


def _dot_f32(a, b):
    ah, al, _ = _split3(a)
    bh, bl, _ = _split3(b)
    d = lambda u, v: jnp.dot(u, v, preferred_element_type=F32)
    return d(ah, bh) + d(al, bh) + d(ah, bl)


def _split_w1_tile(w_ref, g_ref, l_ref, t_ref):
    sub = t_ref.shape[1]
    for s in range(w_ref.shape[2] // sub):
        wt = w_ref[0, :, s * sub:(s + 1) * sub].T
        out_rows = slice(s * sub // 2, (s + 1) * sub // 2)
        for c in range(wt.shape[1] // LANES):
            cols = slice(c * LANES, (c + 1) * LANES)
            t_ref[c] = wt[:, cols]
            g_ref[0, out_rows, cols] = t_ref[c, pl.ds(0, sub // 2, stride=2), :].astype(BF16)
            l_ref[0, out_rows, cols] = t_ref[c, pl.ds(1, sub // 2, stride=2), :].astype(BF16)


def _in_proj_body(x_ref, g_ref, w_ref, wg_ref, bg_ref, w1_ref, a1_ref, a4_ref, a16_ref, gla_ref, b_ref,
                  w1g_ref, w1l_ref, t_ref, wb_ref, wf_ref, wsem, *, tm):
    @pl.when((pl.program_id(0) == 0) & (pl.program_id(1) == 0))
    def _():
        fetch = pltpu.make_async_copy(w_ref.at[0], wf_ref, wsem)
        fetch.start()
        fetch.wait()
        wb_ref[...] = jnp.zeros_like(wb_ref)
        wb_ref[:, :wf_ref.shape[1]] = wf_ref[...].astype(BF16)

    _split_w1_tile(w1_ref, w1g_ref, w1l_ref, t_ref)

    x = x_ref[...]
    n = x * lax.rsqrt(jnp.mean(x * x, axis=-1, keepdims=True) + NORM_EPS) * g_ref[...]
    p = jnp.dot(n.astype(BF16), wb_ref[...], preferred_element_type=F32)
    gla_ref[...] = p[:, ATTN_COLS:ATTN_COLS + GLA_COLS].astype(BF16)

    qkv = p[:, :ATTN_COLS].astype(BF16)
    a1_ref[0, 0] = qkv
    dst = lax.broadcasted_iota(I32, (tm, tm), 0)
    src = lax.broadcasted_iota(I32, (tm, tm), 1)
    for dil, a_ref in ((DILATIONS[1], a4_ref), (DILATIONS[2], a16_ref)):
        per = tm // dil
        perm = (src == (dst % per) * dil + dst // per).astype(F32).astype(BF16)
        moved = jnp.dot(perm, qkv, preferred_element_type=F32).astype(BF16)
        for r in range(dil):
            a_ref[0, r] = moved[r * per:(r + 1) * per]

    z = p[:, ATTN_COLS + GLA_COLS:]
    zz = _dot_f32(z, wg_ref[...]) + bg_ref[...]
    la = (jnp.minimum(zz, 0.0) - jnp.log1p(jnp.exp(-jnp.abs(zz)))) * (1.0 / GLA_TAU)
    C = GLA_CHUNK
    r_i = lax.broadcasted_iota(I32, (C, C), 0)
    c_i = lax.broadcasted_iota(I32, (C, C), 1)
    lower = (r_i >= c_i).astype(BF16)
    upper = (r_i <= c_i).astype(BF16)
    W = GLA_KEY_W
    for j in range(tm // C):
        rows = slice(j * C, (j + 1) * C)
        for tri, cols in ((lower, slice(0, W)), (upper, slice(W, 2 * W))):
            terms = jnp.concatenate(_split3(la[rows, cols]), axis=1)
            cs = jnp.dot(tri, terms, preferred_element_type=F32)
            b_ref[rows, cols] = cs[:, :W] + cs[:, W:2 * W] + cs[:, 2 * W:]


def _in_proj(x2, B, S, norm_g, w_in, wg_f, bg_f, wg_b, bg_b, w1):
    T, D = x2.shape
    tm = 256
    nt = S // tm
    E, _, F2 = w1.shape
    w1_cols = E * F2 // (B * nt)
    per_e = F2 // w1_cols
    assert w1_cols * B * nt == E * F2 and per_e * w1_cols == F2 and w1_cols % W1_SUBTILE == 0
    w1_in = pl.BlockSpec((1, D, w1_cols), lambda b, i: ((b * nt + i) // per_e, 0, (b * nt + i) % per_e))
    w1_out = pl.BlockSpec((1, w1_cols // 2, D), lambda b, i: ((b * nt + i) // per_e, (b * nt + i) % per_e, 0))
    w_spec = pl.BlockSpec(memory_space=pl.ANY)
    wg = jnp.zeros((Z_COLS, 2 * GLA_KEY_W), F32)
    wg = wg.at[:GLA_RANK, :GLA_KEY_W].set(wg_f).at[GLA_RANK:2 * GLA_RANK, GLA_KEY_W:].set(wg_b)
    bg = jnp.concatenate([bg_f, bg_b]).reshape(1, 2 * GLA_KEY_W)
    row = lambda c: pl.BlockSpec((tm, c), lambda b, i: (b * nt + i, 0))
    full = lambda a: pl.BlockSpec(a.shape, lambda b, i: (0,) * a.ndim)
    res = lambda d: pl.BlockSpec((1, d, tm // d, ATTN_COLS), lambda b, i: (b, 0, i, 0))
    res_shape = lambda d: jax.ShapeDtypeStruct((B, d, S // d, ATTN_COLS), BF16)
    g2 = norm_g.reshape(1, D)
    return pl.pallas_call(
        functools.partial(_in_proj_body, tm=tm),
        grid=(B, nt),
        in_specs=[row(D), full(g2), w_spec, full(wg), full(bg), w1_in],
        out_specs=[res(d) for d in DILATIONS] + [row(GLA_COLS), row(2 * GLA_KEY_W), w1_out, w1_out],
        out_shape=[res_shape(d) for d in DILATIONS]
                  + [jax.ShapeDtypeStruct((T, GLA_COLS), BF16), jax.ShapeDtypeStruct((T, 2 * GLA_KEY_W), F32)]
                  + [jax.ShapeDtypeStruct((E, F2 // 2, D), BF16)] * 2,
        scratch_shapes=[pltpu.VMEM((D // LANES, W1_SUBTILE, LANES), F32), pltpu.VMEM((D, PROJ_COLS), BF16),
                        pltpu.VMEM(w_in.shape[1:], F32), pltpu.SemaphoreType.DMA(())],
        compiler_params=pltpu.CompilerParams(dimension_semantics=("arbitrary", "arbitrary"),
                                             vmem_limit_bytes=56 * 2**20),
        name="in_proj",
    )(x2, g2, w_in, wg, bg, w1)


def _t5_bucket(rel):
    nb = T5_BUCKETS // 2
    max_exact = nb // 2
    n = np.abs(rel)
    large = max_exact + (np.log(np.maximum(n, 1).astype(np.float32) / max_exact)
                         / np.log(T5_MAX_DISTANCE / max_exact) * (nb - max_exact)).astype(np.int32)
    large = np.minimum(large, nb - 1)
    return (np.where(rel > 0, nb, 0) + np.where(n < max_exact, n, large)).astype(np.int32)


def _bucket_tables(dil):
    s = np.arange(ATTN_QBLK)[:, None]
    t = np.arange(ATTN_KWIN)[None, :]
    tabs = []
    for c in (0, HALF_SPAN, 2 * HALF_SPAN):
        off = t - c - s
        tabs.append(np.where(np.abs(off) <= HALF_SPAN, _t5_bucket(off * dil), -1))
    return np.stack(tabs).astype(np.int32)


def _attn_body(rb_ref, idx_ref, q_ref, k_ref, v_ref, o_ref, lse_ref, tab_ref, *, L, TQ):
    first = (pl.program_id(0) == 0) & (pl.program_id(1) == 0) & (pl.program_id(2) == 0)

    @pl.when(first)
    def _():
        for v in range(3):
            idx = idx_ref[v]
            for h in range(ATTN_HEADS):
                fill = lambda j, t, h=h, idx=idx: jnp.where(idx == j, rb_ref[j, h] * LOG2_E, t)
                tab_ref[v, h] = lax.fori_loop(0, T5_BUCKETS, fill, jnp.where(idx < 0, NEG_INF, 0.0).astype(F32))

    i = pl.program_id(2)
    lo = lax.broadcasted_iota(I32, (ATTN_QBLK, LANES), 1) < HEAD_DIM
    scale = HEAD_DIM ** -0.5 * LOG2_E
    lane = lax.broadcasted_iota(I32, (ATTN_QBLK, LANES), 1)
    for rb, sb in [(rb, sb) for rb in range(q_ref.shape[0]) for sb in range(TQ // ATTN_QBLK)]:
        l0 = i * TQ + sb * ATTN_QBLK
        lse_tile = jnp.zeros((ATTN_QBLK, LANES), F32)
        ws = pl.multiple_of(jnp.clip(l0 - HALF_SPAN, 0, L - ATTN_KWIN), HALF_SPAN)
        var = jnp.where(l0 == 0, 0, jnp.where(l0 == L - ATTN_QBLK, 2, 1))
        rows = slice(sb * ATTN_QBLK, (sb + 1) * ATTN_QBLK)
        for hp in range(ATTN_HEADS // 2):
            cols = slice(hp * LANES, (hp + 1) * LANES)
            q2 = q_ref[rb, rows, cols]
            k2 = k_ref[rb, pl.ds(ws, ATTN_KWIN), cols]
            v2 = v_ref[rb, pl.ds(ws, ATTN_KWIN), cols]
            zero = jnp.zeros_like(q2)
            qs = jnp.concatenate([jnp.where(lo, q2, zero), jnp.where(lo, zero, q2)], axis=0)
            s = lax.dot_general(qs, k2, NT_DIMS, preferred_element_type=F32)
            bias = jnp.concatenate([tab_ref[var, 2 * hp], tab_ref[var, 2 * hp + 1]], axis=0)
            logits = s * scale + bias
            m = jnp.max(logits, axis=-1, keepdims=True)
            p = jnp.exp2(logits - m)
            den = jnp.sum(p, axis=-1, keepdims=True)
            pv = jnp.dot(p.astype(BF16), v2, preferred_element_type=F32) / den
            lse = m * LN_2 + jnp.log(den)
            o_ref[rb, rows, cols] = jnp.where(lo, pv[:ATTN_QBLK], pv[ATTN_QBLK:]).astype(BF16)
            lse_tile = jnp.where(lane == 2 * hp, lse[:ATTN_QBLK], lse_tile)
            lse_tile = jnp.where(lane == 2 * hp + 1, lse[ATTN_QBLK:], lse_tile)
        lse_ref[rb, rows, :] = lse_tile


def _attn_branch(attn, rel_bias, dil):
    B, _, L, _ = attn.shape
    TQ = min(ATTN_STEP_ROWS, L)
    RB = ATTN_STEP_ROWS // TQ
    idx = jnp.asarray(_bucket_tables(dil))
    body = functools.partial(_attn_body, L=L, TQ=TQ)
    kv = lambda j: pl.BlockSpec((None, RB, L, ATTN_W), lambda b, r, i: (b, r, 0, j))
    q_spec = pl.BlockSpec((None, RB, TQ, ATTN_W), lambda b, r, i: (b, r, i, 0))
    return pl.pallas_call(
        body,
        grid=(B, dil // RB, L // TQ),
        in_specs=[pl.BlockSpec(memory_space=pltpu.SMEM),
                  pl.BlockSpec(idx.shape, lambda b, r, i: (0, 0, 0)),
                  q_spec, kv(1), kv(2)],
        out_specs=[q_spec, pl.BlockSpec((None, RB, TQ, LANES), lambda b, r, i: (b, r, i, 0))],
        out_shape=[jax.ShapeDtypeStruct((B, dil, L, ATTN_W), BF16),
                   jax.ShapeDtypeStruct((B, dil, L, LANES), F32)],
        scratch_shapes=[pltpu.VMEM((3, ATTN_HEADS, ATTN_QBLK, ATTN_KWIN), F32)],
        compiler_params=pltpu.CompilerParams(dimension_semantics=("arbitrary",) * 3,
                                             vmem_limit_bytes=48 * 2**20),
        name=f"attn_d{dil}",
    )(rel_bias, idx, attn, attn, attn)


def _gla_body(q_ref, k_ref, v_ref, gate_ref, bf_ref, bb_ref, ng_ref, out_ref, of_ref, ob_ref, st_ref, *, S):
    C = GLA_CHUNK
    nc = S // C
    r = lax.broadcasted_iota(I32, (C, LANES), 0)
    s2 = lax.broadcasted_iota(I32, (C, LANES), 1) % HEAD_DIM
    lo = lax.broadcasted_iota(I32, (C, LANES), 1) < HEAD_DIM
    blockdiag = ((lax.broadcasted_iota(I32, (2 * GLA_DV, LANES), 0) < GLA_DV)
                 == (lax.broadcasted_iota(I32, (2 * GLA_DV, LANES), 1) < HEAD_DIM))
    st_ref[...] = jnp.zeros_like(st_ref)

    def one_chunk(ci, direction):
        mask = (r >= s2) if direction == 0 else (r <= s2)
        b_ref, o_ref = (bf_ref, of_ref) if direction == 0 else (bb_ref, ob_ref)
        rows = pl.ds(pl.multiple_of(ci * C, C), C)
        q = q_ref[0, rows, :].astype(F32) * (HEAD_DIM ** -0.5)
        k = k_ref[0, rows, :].astype(F32)
        v = v_ref[0, rows, :]
        b = b_ref[0, rows, :]
        g_tot = b[C - 1:C, :] if direction == 0 else b[0:1, :]
        qd = (q * jnp.exp(b)).astype(BF16)
        kd = (k * jnp.exp(-b)).astype(BF16)
        kst = (k * jnp.exp(g_tot - b)).astype(BF16)
        zk = jnp.zeros_like(kd)
        ks = jnp.concatenate([jnp.where(lo, kd, zk), jnp.where(lo, zk, kd)], axis=0)
        a = lax.dot_general(qd, ks, NT_DIMS, preferred_element_type=F32)
        a = jnp.where(mask, a, 0.0).astype(BF16)
        zv = jnp.zeros((C, GLA_DV), BF16)
        vblk = jnp.concatenate([jnp.concatenate([v[:, :GLA_DV], zv], axis=1),
                                jnp.concatenate([zv, v[:, GLA_DV:]], axis=1)], axis=0)
        st = st_ref[direction]
        o_ref[rows, :] = (jnp.dot(a, vblk, preferred_element_type=F32)
                          + lax.dot_general(qd, st.astype(BF16), NT_DIMS, preferred_element_type=F32))
        d_st = jnp.dot(v.astype(F32).T.astype(BF16), kst, preferred_element_type=F32)
        st_ref[direction] = st * jnp.exp(g_tot) + jnp.where(blockdiag, d_st, 0.0)

    def step(i, carry):
        one_chunk(i, 0)
        one_chunk(nc - 1 - i, 1)
        return carry

    lax.fori_loop(0, nc, step, 0, unroll=16)

    tr = 256
    def epilogue(i, carry):
        rows = pl.ds(pl.multiple_of(i * tr, tr), tr)
        for hh in range(2):
            cols = slice(hh * GLA_DV, (hh + 1) * GLA_DV)
            o = of_ref[rows, cols] + ob_ref[rows, cols]
            o = o * lax.rsqrt(jnp.mean(o * o, axis=-1, keepdims=True) + NORM_EPS) * ng_ref[0, :, cols]
            gate = gate_ref[0, rows, cols].astype(F32)
            out_ref[0, rows, cols] = (o * (gate * _sigmoid(gate))).astype(BF16)
        return carry

    lax.fori_loop(0, S // tr, epilogue, 0)


def _gla(gla, bcum, norm_g, B, S):
    g3 = gla.reshape(B, S, GLA_COLS)
    b3 = bcum.reshape(B, S, 2 * GLA_KEY_W)
    ng = norm_g.reshape(2, 1, 2 * GLA_DV)
    body = functools.partial(_gla_body, S=S)
    spec = lambda w, off: pl.BlockSpec((1, S, w), lambda b, p: (b, 0, off + p))
    out = pl.pallas_call(
        body,
        grid=(B, 2),
        in_specs=[spec(LANES, 0),
                  spec(LANES, GLA_KEY_W // LANES),
                  spec(2 * GLA_DV, (2 * GLA_KEY_W) // (2 * GLA_DV)),
                  spec(2 * GLA_DV, (2 * GLA_KEY_W + GLA_VAL_W) // (2 * GLA_DV)),
                  spec(LANES, 0),
                  spec(LANES, GLA_KEY_W // LANES),
                  pl.BlockSpec((1, 1, 2 * GLA_DV), lambda b, p: (p, 0, 0))],
        out_specs=pl.BlockSpec((1, S, 2 * GLA_DV), lambda b, p: (b, 0, p)),
        out_shape=jax.ShapeDtypeStruct((B, S, GLA_VAL_W), BF16),
        scratch_shapes=[pltpu.VMEM((S, 2 * GLA_DV), F32), pltpu.VMEM((S, 2 * GLA_DV), F32),
                        pltpu.VMEM((2, 2 * GLA_DV, LANES), F32)],
        compiler_params=pltpu.CompilerParams(dimension_semantics=("arbitrary", "arbitrary"),
                                             vmem_limit_bytes=56 * 2**20),
        name="gla",
    )(g3, g3, g3, g3, b3, b3, ng)
    return out.reshape(B * S, GLA_VAL_W)


def _natural_rows(blk_ref, u_ref, dil, tm):
    if dil == 1:
        return blk_ref[0, 0].astype(F32)
    chunks = blk_ref.shape[-1] // LANES
    for r in range(dil):
        for c in range(chunks):
            u_ref[c, pl.ds(r, tm // dil, stride=dil), :] = blk_ref[0, r, :, c * LANES:(c + 1) * LANES].astype(F32)
    return jnp.concatenate([u_ref[c] for c in range(chunks)], axis=1)


def _per_head_lanes(w):
    head = lax.broadcasted_iota(I32, (LANES, ATTN_W), 0)
    col = lax.broadcasted_iota(I32, (LANES, ATTN_W), 1)
    spread = (col // HEAD_DIM == head).astype(F32).astype(BF16)
    hi, lo, _ = _split3(w)
    return (jnp.dot(hi, spread, preferred_element_type=F32) + jnp.dot(lo, spread, preferred_element_type=F32))


def _mix_body(o1, o2, o3, l1, l2, l3, g_ref, x_ref, wo_ref, n2_ref, rwh_ref, rwl_ref, rb_ref,
              x1_ref, h_ref, m_ref, slab_ref, tcnt_ref, u_ref, wob_ref, *, tm):
    @pl.when((pl.program_id(0) == 0) & (pl.program_id(1) == 0))
    def _():
        wob_ref[...] = wo_ref[...].astype(BF16)

    ls = [_natural_rows(l, u_ref, d, tm) for l, d in zip((l1, l2, l3), DILATIONS)]
    m = jnp.maximum(jnp.maximum(ls[0], ls[1]), ls[2])
    es = [jnp.exp(l - m) for l in ls]
    den = es[0] + es[1] + es[2]
    a = _per_head_lanes(es[0] / den) * _natural_rows(o1, u_ref, DILATIONS[0], tm)
    a = a + _per_head_lanes(es[1] / den) * _natural_rows(o2, u_ref, DILATIONS[1], tm)
    a = a + _per_head_lanes(es[2] / den) * _natural_rows(o3, u_ref, DILATIONS[2], tm)
    mix = (jnp.dot(a.astype(BF16), wob_ref[:ATTN_W, :], preferred_element_type=F32)
           + jnp.dot(g_ref[...], wob_ref[ATTN_W:, :], preferred_element_type=F32))
    x1 = x_ref[...] + mix
    x1_ref[...] = x1
    h = x1 * lax.rsqrt(jnp.mean(x1 * x1, axis=-1, keepdims=True) + NORM_EPS) * n2_ref[...]
    h_ref[...] = h.astype(BF16)

    hh, hl, _ = _split3(h)
    dg = lambda u, v: lax.dot_general(u, v, NT_DIMS, preferred_element_type=F32)
    logits = dg(rwh_ref[...], hh) + dg(rwl_ref[...], hh) + dg(rwh_ref[...], hl) + rb_ref[...]
    ie = lax.broadcasted_iota(I32, (N_EXPERTS, tm), 0)
    cur = logits
    vals, idxs = [], []
    for _ in range(TOP_K):
        mx = jnp.max(cur, axis=0, keepdims=True)
        ix = jnp.min(jnp.where(cur == mx, ie, N_EXPERTS), axis=0, keepdims=True)
        vals.append(mx)
        idxs.append(ix)
        cur = jnp.where(ie == ix, -jnp.inf, cur)
    ex = [jnp.exp(v - vals[0]) for v in vals]
    den = ex[0] + ex[1] + ex[2] + ex[3]
    gates = [e / den for e in ex]

    onehots = [ie == ix for ix in idxs]
    chosen = (onehots[0] | onehots[1] | onehots[2] | onehots[3]).astype(F32)
    tr = lax.broadcasted_iota(I32, (tm, tm), 0)
    tc = lax.broadcasted_iota(I32, (tm, tm), 1)
    before = jnp.dot(chosen.astype(BF16), (tr < tc).astype(BF16), preferred_element_type=F32)
    count = jnp.broadcast_to(jnp.sum(chosen, axis=1, keepdims=True), (N_EXPERTS, LANES))
    tcnt_ref[0] = count
    run = jnp.floor((count + (RUN_ALIGN - 1.0)) * (1.0 / RUN_ALIGN)) * RUN_ALIGN
    er = lax.broadcasted_iota(I32, (N_EXPERTS, N_EXPERTS), 0)
    ec = lax.broadcasted_iota(I32, (N_EXPERTS, N_EXPERTS), 1)
    run_start = jnp.dot((ec < er).astype(BF16), run.astype(BF16), preferred_element_type=F32)
    where_ = before + run_start[:, 0:1]
    locs = [jnp.sum(jnp.where(oh, where_, 0.0), axis=0, keepdims=True) for oh in onehots]

    def pack(rows, n):
        sub = lax.broadcasted_iota(I32, (n, tm), 0)
        out = jnp.zeros((n, tm), rows[0].dtype)
        for k, row in enumerate(rows):
            out = jnp.where(sub == k, row, out)
        return out

    m_ref[...] = pack([l.astype(I32) for l in locs], 8)
    slab = pack(gates + locs, 8)
    slab_ref[...] = jnp.concatenate([slab, jnp.zeros((LANES - 8, tm), F32)], axis=0).T


def _mix(outs, lses, g, x2, B, S, w_out, norm2_g, router_w, router_b):
    T, D = x2.shape
    tm = ROUTE_TILE
    nt = S // tm
    wo_spec = pl.BlockSpec(w_out.shape, lambda b, i: (0, 0), pipeline_mode=pl.Buffered(1))
    rwt = router_w.T
    rwh = rwt.astype(BF16)
    rwl = (rwt - rwh.astype(F32)).astype(BF16)
    rb = router_b.reshape(N_EXPERTS, 1)
    n2 = norm2_g.reshape(1, D)
    row = lambda c: pl.BlockSpec((tm, c), lambda b, i: (b * nt + i, 0))
    full = lambda a: pl.BlockSpec(a.shape, lambda b, i: (0,) * a.ndim)
    col = pl.BlockSpec((8, tm), lambda b, i: (0, b * nt + i))
    res = [pl.BlockSpec((1, d, tm // d, ATTN_W), lambda b, i: (b, 0, i, 0)) for d in DILATIONS]
    res_lse = [pl.BlockSpec((1, d, tm // d, LANES), lambda b, i: (b, 0, i, 0)) for d in DILATIONS]
    return pl.pallas_call(
        functools.partial(_mix_body, tm=tm),
        grid=(B, nt),
        in_specs=res + res_lse + [row(GLA_VAL_W), row(D), wo_spec, full(n2), full(rwh), full(rwl), full(rb)],
        out_specs=[row(D), row(D), col, row(LANES),
                   pl.BlockSpec((1, N_EXPERTS, LANES), lambda b, i: (b * nt + i, 0, 0))],
        out_shape=[jax.ShapeDtypeStruct((T, D), F32),
                   jax.ShapeDtypeStruct((T, D), BF16),
                   jax.ShapeDtypeStruct((8, T), I32),
                   jax.ShapeDtypeStruct((T, LANES), F32),
                   jax.ShapeDtypeStruct((T // tm, N_EXPERTS, LANES), F32)],
        scratch_shapes=[pltpu.VMEM((ATTN_W // LANES, tm, LANES), F32), pltpu.VMEM(w_out.shape, BF16)],
        compiler_params=pltpu.CompilerParams(dimension_semantics=("arbitrary", "arbitrary"),
                                             vmem_limit_bytes=48 * 2**20),
        name="mix_router",
    )(*outs, *lses, g, x2, w_out, n2, rwh, rwl, rb)


def _run_groups(c):
    return (c + (RUN_ALIGN - 1)) >> RUN_SHIFT


def _plan_body(tc_ref, rs_ref, p0_ref, pn_ref, be_ref, hv_ref, nu_ref, *, nt, nblk):
    shift = MOE_BLOCK.bit_length() - 1
    start = jnp.int32(0)
    for ex in range(N_EXPERTS):
        rows_of = lambda i, ex=ex: _run_groups(tc_ref[i, ex]) << RUN_SHIFT
        tot = lax.fori_loop(0, nt, lambda i, tot: tot + rows_of(i), jnp.int32(0), unroll=True)
        nb = (tot + (MOE_BLOCK - 1)) >> shift
        pad = (nb << shift) - tot
        p0_ref[ex] = start
        pn_ref[ex] = pad >> RUN_SHIFT

        def run(i, at, ex=ex):
            rs_ref[i, ex] = at
            return at + rows_of(i)

        lax.fori_loop(0, nt, run, start + pad, unroll=True)
        b0 = start >> shift

        def fill(j, carry, b0=b0, ex=ex):
            be_ref[b0 + j] = jnp.int32(ex)
            hv_ref[b0 + j] = jnp.int32(MOE_BLOCK // MOE_PART)
            return carry

        lax.fori_loop(0, nb, fill, 0)

        @pl.when(nb > 0)
        def _(b0=b0, pad=pad):
            hv_ref[b0] = (MOE_BLOCK - pad + (MOE_PART - 1)) >> (MOE_PART.bit_length() - 1)

        start = start + (nb << shift)
    used = start >> shift

    def tail(j, carry):
        be_ref[j] = jnp.int32(N_EXPERTS - 1)
        hv_ref[j] = jnp.int32(0)
        return carry

    lax.fori_loop(used, nblk, tail, 0)
    nu_ref[0] = used


def _plan(tcnt, nblk):
    nt = tcnt.shape[0]
    smem = pl.BlockSpec(memory_space=pltpu.SMEM)
    return pl.pallas_call(
        functools.partial(_plan_body, nt=nt, nblk=nblk),
        in_specs=[smem],
        out_specs=[smem] * 6,
        out_shape=[jax.ShapeDtypeStruct(tcnt.shape, I32),
                   jax.ShapeDtypeStruct((N_EXPERTS,), I32),
                   jax.ShapeDtypeStruct((N_EXPERTS,), I32),
                   jax.ShapeDtypeStruct((nblk,), I32),
                   jax.ShapeDtypeStruct((nblk,), I32),
                   jax.ShapeDtypeStruct((1,), I32)],
        name="plan",
    )(tcnt)


def _for_each_run(tc_ref, rs_ref, i, fn):
    big = 4 * RUN_ALIGN
    start = jnp.int32(0)
    for ex in range(N_EXPERTS):
        ng = _run_groups(tc_ref[i, ex])
        g0 = rs_ref[i, ex]

        def piece(q, carry, start=start, g0=g0):
            fn(pl.multiple_of(start + q * big, RUN_ALIGN), pl.multiple_of(g0 + q * big, RUN_ALIGN), big)
            return carry

        def single(q, carry, start=start, g0=g0):
            fn(pl.multiple_of(start + q * RUN_ALIGN, RUN_ALIGN), pl.multiple_of(g0 + q * RUN_ALIGN, RUN_ALIGN),
               RUN_ALIGN)
            return carry

        lax.fori_loop(0, ng >> 2, piece, 0)
        lax.fori_loop((ng >> 2) << 2, ng, single, 0)
        start = start + (ng << RUN_SHIFT)


def _group_count(tc_ref, i):
    n = jnp.int32(0)
    for ex in range(N_EXPERTS):
        n = n + _run_groups(tc_ref[i, ex])
    return n


def _wait_groups(n, src_ref, dst_ref, sem):
    def waits(count, rows):
        def body(q, carry):
            pltpu.make_async_copy(src_ref.at[pl.ds(0, rows)], dst_ref.at[pl.ds(0, rows)], sem).wait()
            return carry
        lax.fori_loop(0, count, body, 0)

    waits(n >> 6, 64 * RUN_ALIGN)
    waits((n >> 3) & 7, 8 * RUN_ALIGN)
    waits(n & 7, RUN_ALIGN)


def _dispatch_body(tc_ref, rs_ref, p0_ref, pn_ref, nu_ref, m_ref, h_ref, xin_ref, g_ref, z_ref, sem, zsem, *, tm, R):
    i = pl.program_id(0)
    slot = i % 2
    rows = m_ref[...].astype(jnp.int16)
    jj = lax.broadcasted_iota(I32, (R, tm), 0).astype(jnp.int16)
    hit = (jj == rows[0:1]) | (jj == rows[1:2]) | (jj == rows[2:3]) | (jj == rows[3:4])
    onehot = jnp.where(hit, jnp.ones((), BF16), jnp.zeros((), BF16))
    g_ref[slot] = jnp.dot(onehot, h_ref[...], preferred_element_type=F32).astype(BF16)

    group = lambda ref, r: ref.at[pl.ds(r, RUN_ALIGN)]

    @pl.when(i > 0)
    def _():
        _wait_groups(_group_count(tc_ref, i - 1), g_ref.at[1 - slot], xin_ref, sem.at[1 - slot])

    _for_each_run(tc_ref, rs_ref, i, lambda lr, gr, n: pltpu.make_async_copy(
        g_ref.at[slot, pl.ds(lr, n)], xin_ref.at[pl.ds(gr, n)], sem.at[slot]).start())

    @pl.when(i == pl.num_programs(0) - 1)
    def _():
        _wait_groups(_group_count(tc_ref, i), g_ref.at[slot], xin_ref, sem.at[slot])

    @pl.when(i == 0)
    def _():
        z_ref[...] = jnp.zeros_like(z_ref)
        total = jnp.int32(0)
        for ex in range(N_EXPERTS):
            def body(q, carry, ex=ex):
                pltpu.make_async_copy(group(z_ref, 0), group(xin_ref, pl.multiple_of(p0_ref[ex] + q * RUN_ALIGN, RUN_ALIGN)), zsem).start()
                return carry
            lax.fori_loop(0, pn_ref[ex], body, 0)
            total = total + pn_ref[ex]
        _wait_groups(total, z_ref, xin_ref, zsem)

        block = lambda j: xin_ref.at[pl.ds(pl.multiple_of(j * MOE_BLOCK, MOE_BLOCK), MOE_BLOCK)]
        n_blocks = xin_ref.shape[0] // MOE_BLOCK

        def fill(j, carry):
            pltpu.make_async_copy(z_ref, block(j), zsem).start()
            return carry

        def fill_wait(j, carry):
            pltpu.make_async_copy(z_ref, block(j), zsem).wait()
            return carry

        lax.fori_loop(nu_ref[0], n_blocks, fill, 0)
        lax.fori_loop(nu_ref[0], n_blocks, fill_wait, 0)


def _dispatch(tcnt, rowstart, pad0, padn, n_used, m_t, h, n_rows):
    T, D = h.shape
    tm = ROUTE_TILE
    col = pl.BlockSpec((8, tm), lambda i, *_: (0, i))
    return pl.pallas_call(
        functools.partial(_dispatch_body, tm=tm, R=GROUP_ROWS),
        grid_spec=pltpu.PrefetchScalarGridSpec(
            num_scalar_prefetch=5,
            grid=(T // tm,),
            in_specs=[col, pl.BlockSpec((tm, D), lambda i, *_: (i, 0))],
            out_specs=pl.BlockSpec(memory_space=pl.ANY),
            scratch_shapes=[pltpu.VMEM((2, GROUP_ROWS, D), BF16), pltpu.VMEM((MOE_BLOCK, D), BF16),
                            pltpu.SemaphoreType.DMA((2,)), pltpu.SemaphoreType.DMA(())]),
        out_shape=jax.ShapeDtypeStruct((n_rows, D), BF16),
        compiler_params=pltpu.CompilerParams(dimension_semantics=("arbitrary",), vmem_limit_bytes=56 * 2**20),
        name="dispatch",
    )(tcnt, rowstart, pad0, padn, n_used, m_t, h)


def _moe_body(be_ref, nu_ref, hv_ref, x_ref, w1g_ref, w1l_ref, b1g_ref, b1l_ref, w2_ref, b2_ref, y_ref):
    del be_ref
    j = pl.program_id(0)
    live = j < nu_ref[0]

    def expert_mlp(rows):
        x = x_ref[rows, :]
        hg = lax.dot_general(x, w1g_ref[0], NT_DIMS, preferred_element_type=F32) + b1g_ref[0]
        hl = lax.dot_general(x, w1l_ref[0], NT_DIMS, preferred_element_type=F32) + b1l_ref[0]
        xg = jnp.minimum(hg, SWIGLU_LIMIT)
        xl = jnp.clip(hl, -SWIGLU_LIMIT, SWIGLU_LIMIT)
        act = xg * _sigmoid(SWIGLU_ALPHA * xg) * (xl + 1.0)
        y = jnp.dot(act.astype(BF16), w2_ref[0].astype(BF16), preferred_element_type=F32) + b2_ref[0]
        y_ref[rows, :] = y.astype(BF16)

    for parts in range(1, MOE_BLOCK // MOE_PART + 1):
        @pl.when(live & (hv_ref[j] == parts))
        def _(parts=parts):
            skip = MOE_BLOCK - parts * MOE_PART
            expert_mlp(slice(skip, MOE_BLOCK))
            if skip:
                y_ref[:skip, :] = jnp.zeros((skip, y_ref.shape[1]), y_ref.dtype)


def _moe(block_e, n_used, half_block, xin, w1g, w1l, b1, w2, b2):
    n_rows = xin.shape[0]
    D = w2.shape[2]
    F = w2.shape[1]
    nblk = n_rows // MOE_BLOCK
    b1s = jnp.transpose(b1.reshape(N_EXPERTS, F, 2), (2, 0, 1)).reshape(2, N_EXPERTS, 1, F)
    b1spec = lambda k: pl.BlockSpec((None, 1, 1, F), lambda j, be, nu, hv: (k, be[j], 0, 0))
    b2r = b2.reshape(N_EXPERTS, 1, D)
    xspec = pl.BlockSpec((MOE_BLOCK, D), lambda j, be, nu, hv: (jnp.minimum(j, nu[0] - 1), 0))
    wspec = lambda a: pl.BlockSpec((1,) + a.shape[1:], lambda j, be, nu, hv: (be[j], 0, 0))
    return pl.pallas_call(
        _moe_body,
        grid_spec=pltpu.PrefetchScalarGridSpec(
            num_scalar_prefetch=3,
            grid=(nblk,),
            in_specs=[xspec, wspec(w1g), wspec(w1l), b1spec(0), b1spec(1), wspec(w2), wspec(b2r)],
            out_specs=xspec),
        out_shape=jax.ShapeDtypeStruct(xin.shape, BF16),
        input_output_aliases={3: 0},
        compiler_params=pltpu.CompilerParams(dimension_semantics=("arbitrary",),
                                             vmem_limit_bytes=48 * 2**20),
        name="moe",
    )(block_e, n_used, half_block, xin, w1g, w1l, b1s, b1s, w2, b2r)


def _combine_body(tc_ref, rs_ref, slab_ref, y_ref, x1_ref, fg_ref, out_ref, g_ref, sem, *, tm, R):
    i = pl.program_id(0)
    slot = i % 2

    def fetch(tile, slot):
        _for_each_run(tc_ref, rs_ref, tile, lambda lr, gr, n: pltpu.make_async_copy(
            y_ref.at[pl.ds(gr, n)], g_ref.at[slot, pl.ds(lr, n)], sem.at[slot]).start())

    @pl.when(i == 0)
    def _():
        g_ref[...] = jnp.zeros_like(g_ref)
        fetch(i, slot)

    @pl.when(i + 1 < pl.num_programs(0))
    def _():
        fetch(i + 1, 1 - slot)

    slab = slab_ref[...]
    gates = slab[:, 0:TOP_K].astype(BF16)
    rows = slab[:, TOP_K:2 * TOP_K].astype(I32).astype(jnp.int16)
    jj = lax.broadcasted_iota(I32, (tm, R), 1).astype(jnp.int16)
    sel = jnp.zeros((tm, R), BF16)
    for k in range(TOP_K):
        sel = jnp.where(jj == rows[:, k:k + 1], gates[:, k:k + 1], sel)

    _wait_groups(_group_count(tc_ref, i), y_ref, g_ref.at[slot], sem.at[slot])
    acc = x1_ref[...] + jnp.dot(sel, g_ref[slot], preferred_element_type=F32)
    out_ref[...] = acc * lax.rsqrt(jnp.mean(acc * acc, axis=-1, keepdims=True) + NORM_EPS) * fg_ref[...]


def _combine(tcnt, rowstart, slab, y, x1, final_g):
    T, D = x1.shape
    tm = ROUTE_TILE
    fg = final_g.reshape(1, D)
    return pl.pallas_call(
        functools.partial(_combine_body, tm=tm, R=GROUP_ROWS),
        grid_spec=pltpu.PrefetchScalarGridSpec(
            num_scalar_prefetch=2,
            grid=(T // tm,),
            in_specs=[pl.BlockSpec((tm, LANES), lambda i, *_: (i, 0)),
                      pl.BlockSpec(memory_space=pl.ANY),
                      pl.BlockSpec((tm, D), lambda i, *_: (i, 0)),
                      pl.BlockSpec((1, D), lambda i, *_: (0, 0))],
            out_specs=pl.BlockSpec((tm, D), lambda i, *_: (i, 0)),
            scratch_shapes=[pltpu.VMEM((2, GROUP_ROWS, D), BF16), pltpu.SemaphoreType.DMA((2,))]),
        out_shape=jax.ShapeDtypeStruct((T, D), F32),
        compiler_params=pltpu.CompilerParams(dimension_semantics=("arbitrary",), vmem_limit_bytes=56 * 2**20),
        name="combine",
    )(tcnt, rowstart, slab, y, x1, fg)


def _mixers(x2, B, S, norm1_g, w_in, rel_bias, wg_f, bg_f, wg_b, bg_b, gla_norm_g, w1):
    a1, a4, a16, gla, bcum, w1g, w1l = _in_proj(x2, B, S, norm1_g, w_in, wg_f, bg_f, wg_b, bg_b, w1)
    branches = [_attn_branch(a, rel_bias, d) for a, d in zip((a1, a4, a16), DILATIONS)]
    g = _gla(gla, bcum, gla_norm_g, B, S)
    return [o for o, _ in branches], [l for _, l in branches], g, w1g, w1l


def _moe_layer(h, m_t, slab, tcnt_f, x1, w1g, w1l, b1, w2, b2, final_g):
    T = x1.shape[0]
    nt = T // ROUTE_TILE
    nblk = (T * TOP_K + nt * N_EXPERTS * (RUN_ALIGN - 1)) // MOE_BLOCK + N_EXPERTS
    tcnt = tcnt_f[:, :, 0].astype(I32)
    rowstart, pad0, padn, block_e, half_block, n_used = _plan(tcnt, nblk)
    xin = _dispatch(tcnt, rowstart, pad0, padn, n_used, m_t, h, nblk * MOE_BLOCK)
    y = _moe(block_e, n_used, half_block, xin, w1g, w1l, b1, w2, b2)
    return _combine(tcnt, rowstart, slab, y, x1, final_g)


def kernel(x, norm1_g, w_in, rel_bias, gla_wg_fwd, gla_bg_fwd, gla_wg_bwd, gla_bg_bwd, gla_norm_g, w_out, norm2_g, router_w, router_b, moe_w1, moe_b1, moe_w2, moe_b2, final_g):
    B, S, D = x.shape
    assert w_in.shape[0] == 1, "one layer"
    x2 = x.reshape(B * S, D)
    outs, lses, g, w1g, w1l = _mixers(x2, B, S, norm1_g[0], w_in, rel_bias, gla_wg_fwd[0], gla_bg_fwd[0],
                                      gla_wg_bwd[0], gla_bg_bwd[0], gla_norm_g[0], moe_w1[0])
    x1, h, m_t, slab, tcnt = _mix(outs, lses, g, x2, B, S, w_out[0], norm2_g[0], router_w[0], router_b[0])
    out = _moe_layer(h, m_t, slab, tcnt, x1, w1g, w1l, moe_b1[0], moe_w2[0], moe_b2[0], final_g)
    return out.reshape(B, S, D)
```

```python
import functools

import jax
import jax.numpy as jnp
import numpy as np
from jax import lax
from jax.experimental import pallas as pl
from jax.experimental.pallas import tpu as pltpu

F32, BF16, I32 = jnp.float32, jnp.bfloat16, jnp.int32

LANES = 128
HEAD_DIM = 64
ATTN_HEADS = 8
ATTN_W = ATTN_HEADS * HEAD_DIM
HALF_SPAN = 64
ATTN_QBLK = 128
ATTN_KWIN = 256
ATTN_STEP_ROWS = 2048
DILATIONS = (1, 4, 16)
T5_BUCKETS = 32
T5_MAX_DISTANCE = 1024
NEG_INF = -1e30
LOG2_E = 1.4426950408889634
LN_2 = 0.6931471805599453
GLA_KEY_W = 256
GLA_VAL_W = 512
GLA_DV = 128
GLA_RANK = 16
GLA_TAU = 16.0
GLA_CHUNK = 64
N_EXPERTS = 32
TOP_K = 4
SWIGLU_LIMIT = 7.0
SWIGLU_ALPHA = 1.702
NORM_EPS = 1e-5
MOE_BLOCK = 1024
MOE_PART = 256
W1_SUBTILE = 512
ROUTE_TILE = 512
RUN_ALIGN = 16
RUN_SHIFT = RUN_ALIGN.bit_length() - 1
GROUP_ROWS = ROUTE_TILE * TOP_K + N_EXPERTS * RUN_ALIGN

ATTN_COLS = 3 * ATTN_W
GLA_COLS = 2 * GLA_KEY_W + 2 * GLA_VAL_W
Z_COLS = LANES
PROJ_COLS = ATTN_COLS + GLA_COLS + Z_COLS

NT_DIMS = (((1,), (1,)), ((), ()))


def _sigmoid(x):
    return 1.0 / (1.0 + jnp.exp(-x))


def _split3(a):
    h1 = a.astype(BF16)
    r1 = a - h1.astype(F32)
    h2 = r1.astype(BF16)
    h3 = (r1 - h2.astype(F32)).astype(BF16)
    return h1, h2, h3


def _dot_f32(a, b):
    ah, al, _ = _split3(a)
    bh, bl, _ = _split3(b)
    d = lambda u, v: jnp.dot(u, v, preferred_element_type=F32)
    return d(ah, bh) + d(al, bh) + d(ah, bl)


def _split_w1_tile(w_ref, g_ref, l_ref, t_ref):
    sub = t_ref.shape[1]
    for s in range(w_ref.shape[2] // sub):
        wt = w_ref[0, :, s * sub:(s + 1) * sub].T
        out_rows = slice(s * sub // 2, (s + 1) * sub // 2)
        for c in range(wt.shape[1] // LANES):
            cols = slice(c * LANES, (c + 1) * LANES)
            t_ref[c] = wt[:, cols]
            g_ref[0, out_rows, cols] = t_ref[c, pl.ds(0, sub // 2, stride=2), :].astype(BF16)
            l_ref[0, out_rows, cols] = t_ref[c, pl.ds(1, sub // 2, stride=2), :].astype(BF16)


def _in_proj_body(x_ref, g_ref, w_ref, wg_ref, bg_ref, w1_ref, a1_ref, a4_ref, a16_ref, gla_ref, b_ref,
                  w1g_ref, w1l_ref, t_ref, wb_ref, wf_ref, wsem, *, tm):
    @pl.when((pl.program_id(0) == 0) & (pl.program_id(1) == 0))
    def _():
        fetch = pltpu.make_async_copy(w_ref.at[0], wf_ref, wsem)
        fetch.start()
        fetch.wait()
        wb_ref[...] = jnp.zeros_like(wb_ref)
        wb_ref[:, :wf_ref.shape[1]] = wf_ref[...].astype(BF16)

    _split_w1_tile(w1_ref, w1g_ref, w1l_ref, t_ref)

    x = x_ref[...]
    n = x * lax.rsqrt(jnp.mean(x * x, axis=-1, keepdims=True) + NORM_EPS) * g_ref[...]
    p = jnp.dot(n.astype(BF16), wb_ref[...], preferred_element_type=F32)
    gla_ref[...] = p[:, ATTN_COLS:ATTN_COLS + GLA_COLS].astype(BF16)

    qkv = p[:, :ATTN_COLS].astype(BF16)
    a1_ref[0, 0] = qkv
    dst = lax.broadcasted_iota(I32, (tm, tm), 0)
    src = lax.broadcasted_iota(I32, (tm, tm), 1)
    for dil, a_ref in ((DILATIONS[1], a4_ref), (DILATIONS[2], a16_ref)):
        per = tm // dil
        perm = (src == (dst % per) * dil + dst // per).astype(F32).astype(BF16)
        moved = jnp.dot(perm, qkv, preferred_element_type=F32).astype(BF16)
        for r in range(dil):
            a_ref[0, r] = moved[r * per:(r + 1) * per]

    z = p[:, ATTN_COLS + GLA_COLS:]
    zz = _dot_f32(z, wg_ref[...]) + bg_ref[...]
    la = (jnp.minimum(zz, 0.0) - jnp.log1p(jnp.exp(-jnp.abs(zz)))) * (1.0 / GLA_TAU)
    C = GLA_CHUNK
    r_i = lax.broadcasted_iota(I32, (C, C), 0)
    c_i = lax.broadcasted_iota(I32, (C, C), 1)
    lower = (r_i >= c_i).astype(BF16)
    upper = (r_i <= c_i).astype(BF16)
    W = GLA_KEY_W
    for j in range(tm // C):
        rows = slice(j * C, (j + 1) * C)
        for tri, cols in ((lower, slice(0, W)), (upper, slice(W, 2 * W))):
            terms = jnp.concatenate(_split3(la[rows, cols]), axis=1)
            cs = jnp.dot(tri, terms, preferred_element_type=F32)
            b_ref[rows, cols] = cs[:, :W] + cs[:, W:2 * W] + cs[:, 2 * W:]


def _in_proj(x2, B, S, norm_g, w_in, wg_f, bg_f, wg_b, bg_b, w1):
    T, D = x2.shape
    tm = 256
    nt = S // tm
    E, _, F2 = w1.shape
    w1_cols = E * F2 // (B * nt)
    per_e = F2 // w1_cols
    assert w1_cols * B * nt == E * F2 and per_e * w1_cols == F2 and w1_cols % W1_SUBTILE == 0
    w1_in = pl.BlockSpec((1, D, w1_cols), lambda b, i: ((b * nt + i) // per_e, 0, (b * nt + i) % per_e))
    w1_out = pl.BlockSpec((1, w1_cols // 2, D), lambda b, i: ((b * nt + i) // per_e, (b * nt + i) % per_e, 0))
    w_spec = pl.BlockSpec(memory_space=pl.ANY)
    wg = jnp.zeros((Z_COLS, 2 * GLA_KEY_W), F32)
    wg = wg.at[:GLA_RANK, :GLA_KEY_W].set(wg_f).at[GLA_RANK:2 * GLA_RANK, GLA_KEY_W:].set(wg_b)
    bg = jnp.concatenate([bg_f, bg_b]).reshape(1, 2 * GLA_KEY_W)
    row = lambda c: pl.BlockSpec((tm, c), lambda b, i: (b * nt + i, 0))
    full = lambda a: pl.BlockSpec(a.shape, lambda b, i: (0,) * a.ndim)
    res = lambda d: pl.BlockSpec((1, d, tm // d, ATTN_COLS), lambda b, i: (b, 0, i, 0))
    res_shape = lambda d: jax.ShapeDtypeStruct((B, d, S // d, ATTN_COLS), BF16)
    g2 = norm_g.reshape(1, D)
    return pl.pallas_call(
        functools.partial(_in_proj_body, tm=tm),
        grid=(B, nt),
        in_specs=[row(D), full(g2), w_spec, full(wg), full(bg), w1_in],
        out_specs=[res(d) for d in DILATIONS] + [row(GLA_COLS), row(2 * GLA_KEY_W), w1_out, w1_out],
        out_shape=[res_shape(d) for d in DILATIONS]
                  + [jax.ShapeDtypeStruct((T, GLA_COLS), BF16), jax.ShapeDtypeStruct((T, 2 * GLA_KEY_W), F32)]
                  + [jax.ShapeDtypeStruct((E, F2 // 2, D), BF16)] * 2,
        scratch_shapes=[pltpu.VMEM((D // LANES, W1_SUBTILE, LANES), F32), pltpu.VMEM((D, PROJ_COLS), BF16),
                        pltpu.VMEM(w_in.shape[1:], F32), pltpu.SemaphoreType.DMA(())],
        compiler_params=pltpu.CompilerParams(dimension_semantics=("arbitrary", "arbitrary"),
                                             vmem_limit_bytes=56 * 2**20),
        name="in_proj",
    )(x2, g2, w_in, wg, bg, w1)


def _t5_bucket(rel):
    nb = T5_BUCKETS // 2
    max_exact = nb // 2
    n = np.abs(rel)
    large = max_exact + (np.log(np.maximum(n, 1).astype(np.float32) / max_exact)
                         / np.log(T5_MAX_DISTANCE / max_exact) * (nb - max_exact)).astype(np.int32)
    large = np.minimum(large, nb - 1)
    return (np.where(rel > 0, nb, 0) + np.where(n < max_exact, n, large)).astype(np.int32)


def _bucket_tables(dil):
    s = np.arange(ATTN_QBLK)[:, None]
    t = np.arange(ATTN_KWIN)[None, :]
    tabs = []
    for c in (0, HALF_SPAN, 2 * HALF_SPAN):
        off = t - c - s
        tabs.append(np.where(np.abs(off) <= HALF_SPAN, _t5_bucket(off * dil), -1))
    return np.stack(tabs).astype(np.int32)


def _attn_body(rb_ref, idx_ref, q_ref, k_ref, v_ref, o_ref, lse_ref, tab_ref, *, L, TQ):
    first = (pl.program_id(0) == 0) & (pl.program_id(1) == 0) & (pl.program_id(2) == 0)

    @pl.when(first)
    def _():
        for v in range(3):
            idx = idx_ref[v]
            for h in range(ATTN_HEADS):
                fill = lambda j, t, h=h, idx=idx: jnp.where(idx == j, rb_ref[j, h] * LOG2_E, t)
                tab_ref[v, h] = lax.fori_loop(0, T5_BUCKETS, fill, jnp.where(idx < 0, NEG_INF, 0.0).astype(F32))

    i = pl.program_id(2)
    lo = lax.broadcasted_iota(I32, (ATTN_QBLK, LANES), 1) < HEAD_DIM
    scale = HEAD_DIM ** -0.5 * LOG2_E
    lane = lax.broadcasted_iota(I32, (ATTN_QBLK, LANES), 1)
    for rb, sb in [(rb, sb) for rb in range(q_ref.shape[0]) for sb in range(TQ // ATTN_QBLK)]:
        l0 = i * TQ + sb * ATTN_QBLK
        lse_tile = jnp.zeros((ATTN_QBLK, LANES), F32)
        ws = pl.multiple_of(jnp.clip(l0 - HALF_SPAN, 0, L - ATTN_KWIN), HALF_SPAN)
        var = jnp.where(l0 == 0, 0, jnp.where(l0 == L - ATTN_QBLK, 2, 1))
        rows = slice(sb * ATTN_QBLK, (sb + 1) * ATTN_QBLK)
        for hp in range(ATTN_HEADS // 2):
            cols = slice(hp * LANES, (hp + 1) * LANES)
            q2 = q_ref[rb, rows, cols]
            k2 = k_ref[rb, pl.ds(ws, ATTN_KWIN), cols]
            v2 = v_ref[rb, pl.ds(ws, ATTN_KWIN), cols]
            zero = jnp.zeros_like(q2)
            qs = jnp.concatenate([jnp.where(lo, q2, zero), jnp.where(lo, zero, q2)], axis=0)
            s = lax.dot_general(qs, k2, NT_DIMS, preferred_element_type=F32)
            bias = jnp.concatenate([tab_ref[var, 2 * hp], tab_ref[var, 2 * hp + 1]], axis=0)
            logits = s * scale + bias
            m = jnp.max(logits, axis=-1, keepdims=True)
            p = jnp.exp2(logits - m)
            den = jnp.sum(p, axis=-1, keepdims=True)
            pv = jnp.dot(p.astype(BF16), v2, preferred_element_type=F32) / den
            lse = m * LN_2 + jnp.log(den)
            o_ref[rb, rows, cols] = jnp.where(lo, pv[:ATTN_QBLK], pv[ATTN_QBLK:]).astype(BF16)
            lse_tile = jnp.where(lane == 2 * hp, lse[:ATTN_QBLK], lse_tile)
            lse_tile = jnp.where(lane == 2 * hp + 1, lse[ATTN_QBLK:], lse_tile)
        lse_ref[rb, rows, :] = lse_tile


def _attn_branch(attn, rel_bias, dil):
    B, _, L, _ = attn.shape
    TQ = min(ATTN_STEP_ROWS, L)
    RB = ATTN_STEP_ROWS // TQ
    idx = jnp.asarray(_bucket_tables(dil))
    body = functools.partial(_attn_body, L=L, TQ=TQ)
    kv = lambda j: pl.BlockSpec((None, RB, L, ATTN_W), lambda b, r, i: (b, r, 0, j))
    q_spec = pl.BlockSpec((None, RB, TQ, ATTN_W), lambda b, r, i: (b, r, i, 0))
    return pl.pallas_call(
        body,
        grid=(B, dil // RB, L // TQ),
        in_specs=[pl.BlockSpec(memory_space=pltpu.SMEM),
                  pl.BlockSpec(idx.shape, lambda b, r, i: (0, 0, 0)),
                  q_spec, kv(1), kv(2)],
        out_specs=[q_spec, pl.BlockSpec((None, RB, TQ, LANES), lambda b, r, i: (b, r, i, 0))],
        out_shape=[jax.ShapeDtypeStruct((B, dil, L, ATTN_W), BF16),
                   jax.ShapeDtypeStruct((B, dil, L, LANES), F32)],
        scratch_shapes=[pltpu.VMEM((3, ATTN_HEADS, ATTN_QBLK, ATTN_KWIN), F32)],
        compiler_params=pltpu.CompilerParams(dimension_semantics=("arbitrary",) * 3,
                                             vmem_limit_bytes=48 * 2**20),
        name=f"attn_d{dil}",
    )(rel_bias, idx, attn, attn, attn)


def _gla_body(q_ref, k_ref, v_ref, gate_ref, bf_ref, bb_ref, ng_ref, out_ref, of_ref, ob_ref, st_ref, *, S):
    C = GLA_CHUNK
    nc = S // C
    r = lax.broadcasted_iota(I32, (C, LANES), 0)
    s2 = lax.broadcasted_iota(I32, (C, LANES), 1) % HEAD_DIM
    lo = lax.broadcasted_iota(I32, (C, LANES), 1) < HEAD_DIM
    blockdiag = ((lax.broadcasted_iota(I32, (2 * GLA_DV, LANES), 0) < GLA_DV)
                 == (lax.broadcasted_iota(I32, (2 * GLA_DV, LANES), 1) < HEAD_DIM))
    st_ref[...] = jnp.zeros_like(st_ref)

    def one_chunk(ci, direction):
        mask = (r >= s2) if direction == 0 else (r <= s2)
        b_ref, o_ref = (bf_ref, of_ref) if direction == 0 else (bb_ref, ob_ref)
        rows = pl.ds(pl.multiple_of(ci * C, C), C)
        q = q_ref[0, rows, :].astype(F32) * (HEAD_DIM ** -0.5)
        k = k_ref[0, rows, :].astype(F32)
        v = v_ref[0, rows, :]
        b = b_ref[0, rows, :]
        g_tot = b[C - 1:C, :] if direction == 0 else b[0:1, :]
        qd = (q * jnp.exp(b)).astype(BF16)
        kd = (k * jnp.exp(-b)).astype(BF16)
        kst = (k * jnp.exp(g_tot - b)).astype(BF16)
        zk = jnp.zeros_like(kd)
        ks = jnp.concatenate([jnp.where(lo, kd, zk), jnp.where(lo, zk, kd)], axis=0)
        a = lax.dot_general(qd, ks, NT_DIMS, preferred_element_type=F32)
        a = jnp.where(mask, a, 0.0).astype(BF16)
        zv = jnp.zeros((C, GLA_DV), BF16)
        vblk = jnp.concatenate([jnp.concatenate([v[:, :GLA_DV], zv], axis=1),
                                jnp.concatenate([zv, v[:, GLA_DV:]], axis=1)], axis=0)
        st = st_ref[direction]
        o_ref[rows, :] = (jnp.dot(a, vblk, preferred_element_type=F32)
                          + lax.dot_general(qd, st.astype(BF16), NT_DIMS, preferred_element_type=F32))
        d_st = jnp.dot(v.astype(F32).T.astype(BF16), kst, preferred_element_type=F32)
        st_ref[direction] = st * jnp.exp(g_tot) + jnp.where(blockdiag, d_st, 0.0)

    def step(i, carry):
        one_chunk(i, 0)
        one_chunk(nc - 1 - i, 1)
        return carry

    lax.fori_loop(0, nc, step, 0, unroll=16)

    tr = 256
    def epilogue(i, carry):
        rows = pl.ds(pl.multiple_of(i * tr, tr), tr)
        for hh in range(2):
            cols = slice(hh * GLA_DV, (hh + 1) * GLA_DV)
            o = of_ref[rows, cols] + ob_ref[rows, cols]
            o = o * lax.rsqrt(jnp.mean(o * o, axis=-1, keepdims=True) + NORM_EPS) * ng_ref[0, :, cols]
            gate = gate_ref[0, rows, cols].astype(F32)
            out_ref[0, rows, cols] = (o * (gate * _sigmoid(gate))).astype(BF16)
        return carry

    lax.fori_loop(0, S // tr, epilogue, 0)


def _gla(gla, bcum, norm_g, B, S):
    g3 = gla.reshape(B, S, GLA_COLS)
    b3 = bcum.reshape(B, S, 2 * GLA_KEY_W)
    ng = norm_g.reshape(2, 1, 2 * GLA_DV)
    body = functools.partial(_gla_body, S=S)
    spec = lambda w, off: pl.BlockSpec((1, S, w), lambda b, p: (b, 0, off + p))
    out = pl.pallas_call(
        body,
        grid=(B, 2),
        in_specs=[spec(LANES, 0),
                  spec(LANES, GLA_KEY_W // LANES),
                  spec(2 * GLA_DV, (2 * GLA_KEY_W) // (2 * GLA_DV)),
                  spec(2 * GLA_DV, (2 * GLA_KEY_W + GLA_VAL_W) // (2 * GLA_DV)),
                  spec(LANES, 0),
                  spec(LANES, GLA_KEY_W // LANES),
                  pl.BlockSpec((1, 1, 2 * GLA_DV), lambda b, p: (p, 0, 0))],
        out_specs=pl.BlockSpec((1, S, 2 * GLA_DV), lambda b, p: (b, 0, p)),
        out_shape=jax.ShapeDtypeStruct((B, S, GLA_VAL_W), BF16),
        scratch_shapes=[pltpu.VMEM((S, 2 * GLA_DV), F32), pltpu.VMEM((S, 2 * GLA_DV), F32),
                        pltpu.VMEM((2, 2 * GLA_DV, LANES), F32)],
        compiler_params=pltpu.CompilerParams(dimension_semantics=("arbitrary", "arbitrary"),
                                             vmem_limit_bytes=56 * 2**20),
        name="gla",
    )(g3, g3, g3, g3, b3, b3, ng)
    return out.reshape(B * S, GLA_VAL_W)


def _natural_rows(blk_ref, u_ref, dil, tm):
    if dil == 1:
        return blk_ref[0, 0].astype(F32)
    chunks = blk_ref.shape[-1] // LANES
    for r in range(dil):
        for c in range(chunks):
            u_ref[c, pl.ds(r, tm // dil, stride=dil), :] = blk_ref[0, r, :, c * LANES:(c + 1) * LANES].astype(F32)
    return jnp.concatenate([u_ref[c] for c in range(chunks)], axis=1)


def _per_head_lanes(w):
    head = lax.broadcasted_iota(I32, (LANES, ATTN_W), 0)
    col = lax.broadcasted_iota(I32, (LANES, ATTN_W), 1)
    spread = (col // HEAD_DIM == head).astype(F32).astype(BF16)
    hi, lo, _ = _split3(w)
    return (jnp.dot(hi, spread, preferred_element_type=F32) + jnp.dot(lo, spread, preferred_element_type=F32))


def _mix_body(o1, o2, o3, l1, l2, l3, g_ref, x_ref, wo_ref, n2_ref, rwh_ref, rwl_ref, rb_ref,
              x1_ref, h_ref, m_ref, slab_ref, tcnt_ref, u_ref, wob_ref, *, tm):
    @pl.when((pl.program_id(0) == 0) & (pl.program_id(1) == 0))
    def _():
        wob_ref[...] = wo_ref[...].astype(BF16)

    ls = [_natural_rows(l, u_ref, d, tm) for l, d in zip((l1, l2, l3), DILATIONS)]
    m = jnp.maximum(jnp.maximum(ls[0], ls[1]), ls[2])
    es = [jnp.exp(l - m) for l in ls]
    den = es[0] + es[1] + es[2]
    a = _per_head_lanes(es[0] / den) * _natural_rows(o1, u_ref, DILATIONS[0], tm)
    a = a + _per_head_lanes(es[1] / den) * _natural_rows(o2, u_ref, DILATIONS[1], tm)
    a = a + _per_head_lanes(es[2] / den) * _natural_rows(o3, u_ref, DILATIONS[2], tm)
    mix = (jnp.dot(a.astype(BF16), wob_ref[:ATTN_W, :], preferred_element_type=F32)
           + jnp.dot(g_ref[...], wob_ref[ATTN_W:, :], preferred_element_type=F32))
    x1 = x_ref[...] + mix
    x1_ref[...] = x1
    h = x1 * lax.rsqrt(jnp.mean(x1 * x1, axis=-1, keepdims=True) + NORM_EPS) * n2_ref[...]
    h_ref[...] = h.astype(BF16)

    hh, hl, _ = _split3(h)
    dg = lambda u, v: lax.dot_general(u, v, NT_DIMS, preferred_element_type=F32)
    logits = dg(rwh_ref[...], hh) + dg(rwl_ref[...], hh) + dg(rwh_ref[...], hl) + rb_ref[...]
    ie = lax.broadcasted_iota(I32, (N_EXPERTS, tm), 0)
    cur = logits
    vals, idxs = [], []
    for _ in range(TOP_K):
        mx = jnp.max(cur, axis=0, keepdims=True)
        ix = jnp.min(jnp.where(cur == mx, ie, N_EXPERTS), axis=0, keepdims=True)
        vals.append(mx)
        idxs.append(ix)
        cur = jnp.where(ie == ix, -jnp.inf, cur)
    ex = [jnp.exp(v - vals[0]) for v in vals]
    den = ex[0] + ex[1] + ex[2] + ex[3]
    gates = [e / den for e in ex]

    onehots = [ie == ix for ix in idxs]
    chosen = (onehots[0] | onehots[1] | onehots[2] | onehots[3]).astype(F32)
    tr = lax.broadcasted_iota(I32, (tm, tm), 0)
    tc = lax.broadcasted_iota(I32, (tm, tm), 1)
    before = jnp.dot(chosen.astype(BF16), (tr < tc).astype(BF16), preferred_element_type=F32)
    count = jnp.broadcast_to(jnp.sum(chosen, axis=1, keepdims=True), (N_EXPERTS, LANES))
    tcnt_ref[0] = count
    run = jnp.floor((count + (RUN_ALIGN - 1.0)) * (1.0 / RUN_ALIGN)) * RUN_ALIGN
    er = lax.broadcasted_iota(I32, (N_EXPERTS, N_EXPERTS), 0)
    ec = lax.broadcasted_iota(I32, (N_EXPERTS, N_EXPERTS), 1)
    run_start = jnp.dot((ec < er).astype(BF16), run.astype(BF16), preferred_element_type=F32)
    where_ = before + run_start[:, 0:1]
    locs = [jnp.sum(jnp.where(oh, where_, 0.0), axis=0, keepdims=True) for oh in onehots]

    def pack(rows, n):
        sub = lax.broadcasted_iota(I32, (n, tm), 0)
        out = jnp.zeros((n, tm), rows[0].dtype)
        for k, row in enumerate(rows):
            out = jnp.where(sub == k, row, out)
        return out

    m_ref[...] = pack([l.astype(I32) for l in locs], 8)
    slab = pack(gates + locs, 8)
    slab_ref[...] = jnp.concatenate([slab, jnp.zeros((LANES - 8, tm), F32)], axis=0).T


def _mix(outs, lses, g, x2, B, S, w_out, norm2_g, router_w, router_b):
    T, D = x2.shape
    tm = ROUTE_TILE
    nt = S // tm
    wo_spec = pl.BlockSpec(w_out.shape, lambda b, i: (0, 0), pipeline_mode=pl.Buffered(1))
    rwt = router_w.T
    rwh = rwt.astype(BF16)
    rwl = (rwt - rwh.astype(F32)).astype(BF16)
    rb = router_b.reshape(N_EXPERTS, 1)
    n2 = norm2_g.reshape(1, D)
    row = lambda c: pl.BlockSpec((tm, c), lambda b, i: (b * nt + i, 0))
    full = lambda a: pl.BlockSpec(a.shape, lambda b, i: (0,) * a.ndim)
    col = pl.BlockSpec((8, tm), lambda b, i: (0, b * nt + i))
    res = [pl.BlockSpec((1, d, tm // d, ATTN_W), lambda b, i: (b, 0, i, 0)) for d in DILATIONS]
    res_lse = [pl.BlockSpec((1, d, tm // d, LANES), lambda b, i: (b, 0, i, 0)) for d in DILATIONS]
    return pl.pallas_call(
        functools.partial(_mix_body, tm=tm),
        grid=(B, nt),
        in_specs=res + res_lse + [row(GLA_VAL_W), row(D), wo_spec, full(n2), full(rwh), full(rwl), full(rb)],
        out_specs=[row(D), row(D), col, row(LANES),
                   pl.BlockSpec((1, N_EXPERTS, LANES), lambda b, i: (b * nt + i, 0, 0))],
        out_shape=[jax.ShapeDtypeStruct((T, D), F32),
                   jax.ShapeDtypeStruct((T, D), BF16),
                   jax.ShapeDtypeStruct((8, T), I32),
                   jax.ShapeDtypeStruct((T, LANES), F32),
                   jax.ShapeDtypeStruct((T // tm, N_EXPERTS, LANES), F32)],
        scratch_shapes=[pltpu.VMEM((ATTN_W // LANES, tm, LANES), F32), pltpu.VMEM(w_out.shape, BF16)],
        compiler_params=pltpu.CompilerParams(dimension_semantics=("arbitrary", "arbitrary"),
                                             vmem_limit_bytes=48 * 2**20),
        name="mix_router",
    )(*outs, *lses, g, x2, w_out, n2, rwh, rwl, rb)


def _run_groups(c):
    return (c + (RUN_ALIGN - 1)) >> RUN_SHIFT


def _plan_body(tc_ref, rs_ref, p0_ref, pn_ref, be_ref, hv_ref, nu_ref, *, nt, nblk):
    shift = MOE_BLOCK.bit_length() - 1
    start = jnp.int32(0)
    for ex in range(N_EXPERTS):
        rows_of = lambda i, ex=ex: _run_groups(tc_ref[i, ex]) << RUN_SHIFT
        tot = lax.fori_loop(0, nt, lambda i, tot: tot + rows_of(i), jnp.int32(0), unroll=True)
        nb = (tot + (MOE_BLOCK - 1)) >> shift
        pad = (nb << shift) - tot
        p0_ref[ex] = start
        pn_ref[ex] = pad >> RUN_SHIFT

        def run(i, at, ex=ex):
            rs_ref[i, ex] = at
            return at + rows_of(i)

        lax.fori_loop(0, nt, run, start + pad, unroll=True)
        b0 = start >> shift

        def fill(j, carry, b0=b0, ex=ex):
            be_ref[b0 + j] = jnp.int32(ex)
            hv_ref[b0 + j] = jnp.int32(MOE_BLOCK // MOE_PART)
            return carry

        lax.fori_loop(0, nb, fill, 0)

        @pl.when(nb > 0)
        def _(b0=b0, pad=pad):
            hv_ref[b0] = (MOE_BLOCK - pad + (MOE_PART - 1)) >> (MOE_PART.bit_length() - 1)

        start = start + (nb << shift)
    used = start >> shift

    def tail(j, carry):
        be_ref[j] = jnp.int32(N_EXPERTS - 1)
        hv_ref[j] = jnp.int32(0)
        return carry

    lax.fori_loop(used, nblk, tail, 0)
    nu_ref[0] = used


def _plan(tcnt, nblk):
    nt = tcnt.shape[0]
    smem = pl.BlockSpec(memory_space=pltpu.SMEM)
    return pl.pallas_call(
        functools.partial(_plan_body, nt=nt, nblk=nblk),
        in_specs=[smem],
        out_specs=[smem] * 6,
        out_shape=[jax.ShapeDtypeStruct(tcnt.shape, I32),
                   jax.ShapeDtypeStruct((N_EXPERTS,), I32),
                   jax.ShapeDtypeStruct((N_EXPERTS,), I32),
                   jax.ShapeDtypeStruct((nblk,), I32),
                   jax.ShapeDtypeStruct((nblk,), I32),
                   jax.ShapeDtypeStruct((1,), I32)],
        name="plan",
    )(tcnt)


def _for_each_run(tc_ref, rs_ref, i, fn):
    big = 4 * RUN_ALIGN
    start = jnp.int32(0)
    for ex in range(N_EXPERTS):
        ng = _run_groups(tc_ref[i, ex])
        g0 = rs_ref[i, ex]

        def piece(q, carry, start=start, g0=g0):
            fn(pl.multiple_of(start + q * big, RUN_ALIGN), pl.multiple_of(g0 + q * big, RUN_ALIGN), big)
            return carry

        def single(q, carry, start=start, g0=g0):
            fn(pl.multiple_of(start + q * RUN_ALIGN, RUN_ALIGN), pl.multiple_of(g0 + q * RUN_ALIGN, RUN_ALIGN),
               RUN_ALIGN)
            return carry

        lax.fori_loop(0, ng >> 2, piece, 0)
        lax.fori_loop((ng >> 2) << 2, ng, single, 0)
        start = start + (ng << RUN_SHIFT)


def _group_count(tc_ref, i):
    n = jnp.int32(0)
    for ex in range(N_EXPERTS):
        n = n + _run_groups(tc_ref[i, ex])
    return n


def _wait_groups(n, src_ref, dst_ref, sem):
    def waits(count, rows):
        def body(q, carry):
            pltpu.make_async_copy(src_ref.at[pl.ds(0, rows)], dst_ref.at[pl.ds(0, rows)], sem).wait()
            return carry
        lax.fori_loop(0, count, body, 0)

    waits(n >> 6, 64 * RUN_ALIGN)
    waits((n >> 3) & 7, 8 * RUN_ALIGN)
    waits(n & 7, RUN_ALIGN)


def _dispatch_body(tc_ref, rs_ref, p0_ref, pn_ref, nu_ref, m_ref, h_ref, xin_ref, g_ref, z_ref, sem, zsem, *, tm, R):
    i = pl.program_id(0)
    slot = i % 2
    rows = m_ref[...].astype(jnp.int16)
    jj = lax.broadcasted_iota(I32, (R, tm), 0).astype(jnp.int16)
    hit = (jj == rows[0:1]) | (jj == rows[1:2]) | (jj == rows[2:3]) | (jj == rows[3:4])
    onehot = jnp.where(hit, jnp.ones((), BF16), jnp.zeros((), BF16))
    g_ref[slot] = jnp.dot(onehot, h_ref[...], preferred_element_type=F32).astype(BF16)

    group = lambda ref, r: ref.at[pl.ds(r, RUN_ALIGN)]

    @pl.when(i > 0)
    def _():
        _wait_groups(_group_count(tc_ref, i - 1), g_ref.at[1 - slot], xin_ref, sem.at[1 - slot])

    _for_each_run(tc_ref, rs_ref, i, lambda lr, gr, n: pltpu.make_async_copy(
        g_ref.at[slot, pl.ds(lr, n)], xin_ref.at[pl.ds(gr, n)], sem.at[slot]).start())

    block = lambda j: xin_ref.at[pl.ds(pl.multiple_of(j * MOE_BLOCK, MOE_BLOCK), MOE_BLOCK)]
    n_blocks = xin_ref.shape[0] // MOE_BLOCK

    @pl.when(i == 0)
    def _():
        z_ref[...] = jnp.zeros_like(z_ref)
        for ex in range(N_EXPERTS):
            def body(q, carry, ex=ex):
                pltpu.make_async_copy(group(z_ref, 0), group(xin_ref, pl.multiple_of(p0_ref[ex] + q * RUN_ALIGN, RUN_ALIGN)), zsem).start()
                return carry
            lax.fori_loop(0, pn_ref[ex], body, 0)

        def fill(j, carry):
            pltpu.make_async_copy(z_ref, block(j), zsem).start()
            return carry

        lax.fori_loop(nu_ref[0], n_blocks, fill, 0)

    @pl.when(i == pl.num_programs(0) - 1)
    def _():
        _wait_groups(_group_count(tc_ref, i), g_ref.at[slot], xin_ref, sem.at[slot])
        total = jnp.int32(0)
        for ex in range(N_EXPERTS):
            total = total + pn_ref[ex]
        _wait_groups(total, z_ref, xin_ref, zsem)

        def fill_wait(j, carry):
            pltpu.make_async_copy(z_ref, block(j), zsem).wait()
            return carry

        lax.fori_loop(nu_ref[0], n_blocks, fill_wait, 0)


def _dispatch(tcnt, rowstart, pad0, padn, n_used, m_t, h, n_rows):
    T, D = h.shape
    tm = ROUTE_TILE
    col = pl.BlockSpec((8, tm), lambda i, *_: (0, i))
    return pl.pallas_call(
        functools.partial(_dispatch_body, tm=tm, R=GROUP_ROWS),
        grid_spec=pltpu.PrefetchScalarGridSpec(
            num_scalar_prefetch=5,
            grid=(T // tm,),
            in_specs=[col, pl.BlockSpec((tm, D), lambda i, *_: (i, 0))],
            out_specs=pl.BlockSpec(memory_space=pl.ANY),
            scratch_shapes=[pltpu.VMEM((2, GROUP_ROWS, D), BF16), pltpu.VMEM((MOE_BLOCK, D), BF16),
                            pltpu.SemaphoreType.DMA((2,)), pltpu.SemaphoreType.DMA(())]),
        out_shape=jax.ShapeDtypeStruct((n_rows, D), BF16),
        compiler_params=pltpu.CompilerParams(dimension_semantics=("arbitrary",), vmem_limit_bytes=56 * 2**20),
        name="dispatch",
    )(tcnt, rowstart, pad0, padn, n_used, m_t, h)


def _moe_body(be_ref, nu_ref, hv_ref, x_ref, w1g_ref, w1l_ref, b1g_ref, b1l_ref, w2_ref, b2_ref, y_ref):
    del be_ref
    j = pl.program_id(0)
    live = j < nu_ref[0]

    def expert_mlp(rows):
        x = x_ref[rows, :]
        hg = lax.dot_general(x, w1g_ref[0], NT_DIMS, preferred_element_type=F32) + b1g_ref[0]
        hl = lax.dot_general(x, w1l_ref[0], NT_DIMS, preferred_element_type=F32) + b1l_ref[0]
        xg = jnp.minimum(hg, SWIGLU_LIMIT)
        xl = jnp.clip(hl, -SWIGLU_LIMIT, SWIGLU_LIMIT)
        act = xg * _sigmoid(SWIGLU_ALPHA * xg) * (xl + 1.0)
        y = jnp.dot(act.astype(BF16), w2_ref[0].astype(BF16), preferred_element_type=F32) + b2_ref[0]
        y_ref[rows, :] = y.astype(BF16)

    for parts in range(1, MOE_BLOCK // MOE_PART + 1):
        @pl.when(live & (hv_ref[j] == parts))
        def _(parts=parts):
            skip = MOE_BLOCK - parts * MOE_PART
            expert_mlp(slice(skip, MOE_BLOCK))
            if skip:
                y_ref[:skip, :] = jnp.zeros((skip, y_ref.shape[1]), y_ref.dtype)


def _moe(block_e, n_used, half_block, xin, w1g, w1l, b1, w2, b2):
    n_rows = xin.shape[0]
    D = w2.shape[2]
    F = w2.shape[1]
    nblk = n_rows // MOE_BLOCK
    b1s = jnp.transpose(b1.reshape(N_EXPERTS, F, 2), (2, 0, 1)).reshape(2, N_EXPERTS, 1, F)
    b1spec = lambda k: pl.BlockSpec((None, 1, 1, F), lambda j, be, nu, hv: (k, be[j], 0, 0))
    b2r = b2.reshape(N_EXPERTS, 1, D)
    xspec = pl.BlockSpec((MOE_BLOCK, D), lambda j, be, nu, hv: (jnp.minimum(j, nu[0] - 1), 0))
    wspec = lambda a: pl.BlockSpec((1,) + a.shape[1:], lambda j, be, nu, hv: (be[j], 0, 0))
    return pl.pallas_call(
        _moe_body,
        grid_spec=pltpu.PrefetchScalarGridSpec(
            num_scalar_prefetch=3,
            grid=(nblk,),
            in_specs=[xspec, wspec(w1g), wspec(w1l), b1spec(0), b1spec(1), wspec(w2), wspec(b2r)],
            out_specs=xspec),
        out_shape=jax.ShapeDtypeStruct(xin.shape, BF16),
        input_output_aliases={3: 0},
        compiler_params=pltpu.CompilerParams(dimension_semantics=("arbitrary",),
                                             vmem_limit_bytes=48 * 2**20),
        name="moe",
    )(block_e, n_used, half_block, xin, w1g, w1l, b1s, b1s, w2, b2r)


def _combine_body(tc_ref, rs_ref, slab_ref, y_ref, x1_ref, fg_ref, out_ref, g_ref, sem, *, tm, R):
    i = pl.program_id(0)
    slot = i % 2

    def fetch(tile, slot):
        _for_each_run(tc_ref, rs_ref, tile, lambda lr, gr, n: pltpu.make_async_copy(
            y_ref.at[pl.ds(gr, n)], g_ref.at[slot, pl.ds(lr, n)], sem.at[slot]).start())

    @pl.when(i == 0)
    def _():
        g_ref[...] = jnp.zeros_like(g_ref)
        fetch(i, slot)

    @pl.when(i + 1 < pl.num_programs(0))
    def _():
        fetch(i + 1, 1 - slot)

    slab = slab_ref[...]
    gates = slab[:, 0:TOP_K].astype(BF16)
    rows = slab[:, TOP_K:2 * TOP_K].astype(I32).astype(jnp.int16)
    jj = lax.broadcasted_iota(I32, (tm, R), 1).astype(jnp.int16)
    sel = jnp.zeros((tm, R), BF16)
    for k in range(TOP_K):
        sel = jnp.where(jj == rows[:, k:k + 1], gates[:, k:k + 1], sel)

    _wait_groups(_group_count(tc_ref, i), y_ref, g_ref.at[slot], sem.at[slot])
    acc = x1_ref[...] + jnp.dot(sel, g_ref[slot], preferred_element_type=F32)
    out_ref[...] = acc * lax.rsqrt(jnp.mean(acc * acc, axis=-1, keepdims=True) + NORM_EPS) * fg_ref[...]


def _combine(tcnt, rowstart, slab, y, x1, final_g):
    T, D = x1.shape
    tm = ROUTE_TILE
    fg = final_g.reshape(1, D)
    return pl.pallas_call(
        functools.partial(_combine_body, tm=tm, R=GROUP_ROWS),
        grid_spec=pltpu.PrefetchScalarGridSpec(
            num_scalar_prefetch=2,
            grid=(T // tm,),
            in_specs=[pl.BlockSpec((tm, LANES), lambda i, *_: (i, 0)),
                      pl.BlockSpec(memory_space=pl.ANY),
                      pl.BlockSpec((tm, D), lambda i, *_: (i, 0)),
                      pl.BlockSpec((1, D), lambda i, *_: (0, 0))],
            out_specs=pl.BlockSpec((tm, D), lambda i, *_: (i, 0)),
            scratch_shapes=[pltpu.VMEM((2, GROUP_ROWS, D), BF16), pltpu.SemaphoreType.DMA((2,))]),
        out_shape=jax.ShapeDtypeStruct((T, D), F32),
        compiler_params=pltpu.CompilerParams(dimension_semantics=("arbitrary",), vmem_limit_bytes=56 * 2**20),
        name="combine",
    )(tcnt, rowstart, slab, y, x1, fg)


def _mixers(x2, B, S, norm1_g, w_in, rel_bias, wg_f, bg_f, wg_b, bg_b, gla_norm_g, w1):
    a1, a4, a16, gla, bcum, w1g, w1l = _in_proj(x2, B, S, norm1_g, w_in, wg_f, bg_f, wg_b, bg_b, w1)
    branches = [_attn_branch(a, rel_bias, d) for a, d in zip((a1, a4, a16), DILATIONS)]
    g = _gla(gla, bcum, gla_norm_g, B, S)
    return [o for o, _ in branches], [l for _, l in branches], g, w1g, w1l


def _moe_layer(h, m_t, slab, tcnt_f, x1, w1g, w1l, b1, w2, b2, final_g):
    T = x1.shape[0]
    nt = T // ROUTE_TILE
    nblk = (T * TOP_K + nt * N_EXPERTS * (RUN_ALIGN - 1)) // MOE_BLOCK + N_EXPERTS
    tcnt = tcnt_f[:, :, 0].astype(I32)
    rowstart, pad0, padn, block_e, half_block, n_used = _plan(tcnt, nblk)
    xin = _dispatch(tcnt, rowstart, pad0, padn, n_used, m_t, h, nblk * MOE_BLOCK)
    y = _moe(block_e, n_used, half_block, xin, w1g, w1l, b1, w2, b2)
    return _combine(tcnt, rowstart, slab, y, x1, final_g)


def kernel(x, norm1_g, w_in, rel_bias, gla_wg_fwd, gla_bg_fwd, gla_wg_bwd, gla_bg_bwd, gla_norm_g, w_out, norm2_g, router_w, router_b, moe_w1, moe_b1, moe_w2, moe_b2, final_g):
    B, S, D = x.shape
    assert w_in.shape[0] == 1, "one layer"
    x2 = x.reshape(B * S, D)
    outs, lses, g, w1g, w1l = _mixers(x2, B, S, norm1_g[0], w_in, rel_bias, gla_wg_fwd[0], gla_bg_fwd[0],
                                      gla_wg_bwd[0], gla_bg_bwd[0], gla_norm_g[0], moe_w1[0])
    x1, h, m_t, slab, tcnt = _mix(outs, lses, g, x2, B, S, w_out[0], norm2_g[0], router_w[0], router_b[0])
    out = _moe_layer(h, m_t, slab, tcnt, x1, w1g, w1l, moe_b1[0], moe_w2[0], moe_b2[0], final_g)
    return out.reshape(B, S, D)
```

```python
import functools

import jax
import jax.numpy as jnp
import numpy as np
from jax import lax
from jax.experimental import pallas as pl
from jax.experimental.pallas import tpu as pltpu

F32, BF16, I32 = jnp.float32, jnp.bfloat16, jnp.int32

LANES = 128
HEAD_DIM = 64
ATTN_HEADS = 8
ATTN_W = ATTN_HEADS * HEAD_DIM
HALF_SPAN = 64
ATTN_QBLK = 128
ATTN_KWIN = 256
ATTN_STEP_ROWS = 2048
DILATIONS = (1, 4, 16)
T5_BUCKETS = 32
T5_MAX_DISTANCE = 1024
NEG_INF = -1e30
LOG2_E = 1.4426950408889634
LN_2 = 0.6931471805599453
GLA_KEY_W = 256
GLA_VAL_W = 512
GLA_DV = 128
GLA_RANK = 16
GLA_TAU = 16.0
GLA_CHUNK = 64
N_EXPERTS = 32
TOP_K = 4
SWIGLU_LIMIT = 7.0
SWIGLU_ALPHA = 1.702
NORM_EPS = 1e-5
MOE_BLOCK = 1024
MOE_PART = 256
W1_SUBTILE = 512
ROUTE_TILE = 512
RUN_ALIGN = 16
RUN_SHIFT = RUN_ALIGN.bit_length() - 1
GROUP_ROWS = ROUTE_TILE * TOP_K + N_EXPERTS * RUN_ALIGN

ATTN_COLS = 3 * ATTN_W
GLA_COLS = 2 * GLA_KEY_W + 2 * GLA_VAL_W
Z_COLS = LANES
PROJ_COLS = ATTN_COLS + GLA_COLS + Z_COLS

NT_DIMS = (((1,), (1,)), ((), ()))


def _sigmoid(x):
    return 1.0 / (1.0 + jnp.exp(-x))


def _split3(a):
    h1 = a.astype(BF16)
    r1 = a - h1.astype(F32)
    h2 = r1.astype(BF16)
    h3 = (r1 - h2.astype(F32)).astype(BF16)
    return h1, h2, h3


def _dot_f32(a, b):
    ah, al, _ = _split3(a)
    bh, bl, _ = _split3(b)
    d = lambda u, v: jnp.dot(u, v, preferred_element_type=F32)
    return d(ah, bh) + d(al, bh) + d(ah, bl)


def _split_w1_tile(w_ref, g_ref, l_ref, t_ref):
    sub = t_ref.shape[1]
    for s in range(w_ref.shape[2] // sub):
        wt = w_ref[0, :, s * sub:(s + 1) * sub].T
        out_rows = slice(s * sub // 2, (s + 1) * sub // 2)
        for c in range(wt.shape[1] // LANES):
            cols = slice(c * LANES, (c + 1) * LANES)
            t_ref[c] = wt[:, cols]
            g_ref[0, out_rows, cols] = t_ref[c, pl.ds(0, sub // 2, stride=2), :].astype(BF16)
            l_ref[0, out_rows, cols] = t_ref[c, pl.ds(1, sub // 2, stride=2), :].astype(BF16)


def _in_proj_body(x_ref, g_ref, w_ref, wg_ref, bg_ref, w1_ref, a1_ref, a4_ref, a16_ref, gla_ref, b_ref,
                  w1g_ref, w1l_ref, t_ref, wb_ref, wf_ref, wsem, *, tm):
    @pl.when((pl.program_id(0) == 0) & (pl.program_id(1) == 0))
    def _():
        fetch = pltpu.make_async_copy(w_ref.at[0], wf_ref, wsem)
        fetch.start()
        fetch.wait()
        wb_ref[...] = jnp.zeros_like(wb_ref)
        wb_ref[:wf_ref.shape[0], :] = wf_ref[...].astype(BF16)

    _split_w1_tile(w1_ref, w1g_ref, w1l_ref, t_ref)

    x = x_ref[...]
    n = x * lax.rsqrt(jnp.mean(x * x, axis=-1, keepdims=True) + NORM_EPS) * g_ref[...]
    p = lax.dot_general(n.astype(BF16), wb_ref[...], NT_DIMS, preferred_element_type=F32)
    gla_ref[...] = p[:, ATTN_COLS:ATTN_COLS + GLA_COLS].astype(BF16)

    qkv = p[:, :ATTN_COLS].astype(BF16)
    a1_ref[0, 0] = qkv
    dst = lax.broadcasted_iota(I32, (tm, tm), 0)
    src = lax.broadcasted_iota(I32, (tm, tm), 1)
    for dil, a_ref in ((DILATIONS[1], a4_ref), (DILATIONS[2], a16_ref)):
        per = tm // dil
        perm = (src == (dst % per) * dil + dst // per).astype(F32).astype(BF16)
        moved = jnp.dot(perm, qkv, preferred_element_type=F32).astype(BF16)
        for r in range(dil):
            a_ref[0, r] = moved[r * per:(r + 1) * per]

    z = p[:, ATTN_COLS + GLA_COLS:]
    zz = _dot_f32(z, wg_ref[...]) + bg_ref[...]
    la = (jnp.minimum(zz, 0.0) - jnp.log1p(jnp.exp(-jnp.abs(zz)))) * (1.0 / GLA_TAU)
    C = GLA_CHUNK
    r_i = lax.broadcasted_iota(I32, (C, C), 0)
    c_i = lax.broadcasted_iota(I32, (C, C), 1)
    lower = (r_i >= c_i).astype(BF16)
    upper = (r_i <= c_i).astype(BF16)
    W = GLA_KEY_W
    for j in range(tm // C):
        rows = slice(j * C, (j + 1) * C)
        for tri, cols in ((lower, slice(0, W)), (upper, slice(W, 2 * W))):
            terms = jnp.concatenate(_split3(la[rows, cols]), axis=1)
            cs = jnp.dot(tri, terms, preferred_element_type=F32)
            b_ref[rows, cols] = cs[:, :W] + cs[:, W:2 * W] + cs[:, 2 * W:]


def _in_proj(x2, B, S, norm_g, w_in_t, wg_f, bg_f, wg_b, bg_b, w1):
    T, D = x2.shape
    tm = 256
    nt = S // tm
    E, _, F2 = w1.shape
    w1_cols = E * F2 // (B * nt)
    per_e = F2 // w1_cols
    assert w1_cols * B * nt == E * F2 and per_e * w1_cols == F2 and w1_cols % W1_SUBTILE == 0
    w1_in = pl.BlockSpec((1, D, w1_cols), lambda b, i: ((b * nt + i) // per_e, 0, (b * nt + i) % per_e))
    w1_out = pl.BlockSpec((1, w1_cols // 2, D), lambda b, i: ((b * nt + i) // per_e, (b * nt + i) % per_e, 0))
    w_spec = pl.BlockSpec(memory_space=pl.ANY)
    wg = jnp.zeros((Z_COLS, 2 * GLA_KEY_W), F32)
    wg = wg.at[:GLA_RANK, :GLA_KEY_W].set(wg_f).at[GLA_RANK:2 * GLA_RANK, GLA_KEY_W:].set(wg_b)
    bg = jnp.concatenate([bg_f, bg_b]).reshape(1, 2 * GLA_KEY_W)
    row = lambda c: pl.BlockSpec((tm, c), lambda b, i: (b * nt + i, 0))
    full = lambda a: pl.BlockSpec(a.shape, lambda b, i: (0,) * a.ndim)
    res = lambda d: pl.BlockSpec((1, d, tm // d, ATTN_COLS), lambda b, i: (b, 0, i, 0))
    res_shape = lambda d: jax.ShapeDtypeStruct((B, d, S // d, ATTN_COLS), BF16)
    g2 = norm_g.reshape(1, D)
    return pl.pallas_call(
        functools.partial(_in_proj_body, tm=tm),
        grid=(B, nt),
        in_specs=[row(D), full(g2), w_spec, full(wg), full(bg), w1_in],
        out_specs=[res(d) for d in DILATIONS] + [row(GLA_COLS), row(2 * GLA_KEY_W), w1_out, w1_out],
        out_shape=[res_shape(d) for d in DILATIONS]
                  + [jax.ShapeDtypeStruct((T, GLA_COLS), BF16), jax.ShapeDtypeStruct((T, 2 * GLA_KEY_W), F32)]
                  + [jax.ShapeDtypeStruct((E, F2 // 2, D), BF16)] * 2,
        scratch_shapes=[pltpu.VMEM((D // LANES, W1_SUBTILE, LANES), F32), pltpu.VMEM((PROJ_COLS, D), BF16),
                        pltpu.VMEM(w_in_t.shape[1:], F32), pltpu.SemaphoreType.DMA(())],
        compiler_params=pltpu.CompilerParams(dimension_semantics=("arbitrary", "arbitrary"),
                                             vmem_limit_bytes=56 * 2**20),
        name="in_proj",
    )(x2, g2, w_in_t, wg, bg, w1)


def _t5_bucket(rel):
    nb = T5_BUCKETS // 2
    max_exact = nb // 2
    n = np.abs(rel)
    large = max_exact + (np.log(np.maximum(n, 1).astype(np.float32) / max_exact)
                         / np.log(T5_MAX_DISTANCE / max_exact) * (nb - max_exact)).astype(np.int32)
    large = np.minimum(large, nb - 1)
    return (np.where(rel > 0, nb, 0) + np.where(n < max_exact, n, large)).astype(np.int32)


def _bucket_tables(dil):
    s = np.arange(ATTN_QBLK)[:, None]
    t = np.arange(ATTN_KWIN)[None, :]
    tabs = []
    for c in (0, HALF_SPAN, 2 * HALF_SPAN):
        off = t - c - s
        tabs.append(np.where(np.abs(off) <= HALF_SPAN, _t5_bucket(off * dil), -1))
    return np.stack(tabs).astype(np.int32)


def _attn_body(rb_ref, idx_ref, q_ref, k_ref, v_ref, o_ref, lse_ref, tab_ref, *, L, TQ):
    first = (pl.program_id(0) == 0) & (pl.program_id(1) == 0) & (pl.program_id(2) == 0)

    @pl.when(first)
    def _():
        for v in range(3):
            idx = idx_ref[v]
            for h in range(ATTN_HEADS):
                fill = lambda j, t, h=h, idx=idx: jnp.where(idx == j, rb_ref[j, h] * LOG2_E, t)
                tab_ref[v, h] = lax.fori_loop(0, T5_BUCKETS, fill, jnp.where(idx < 0, NEG_INF, 0.0).astype(F32))

    i = pl.program_id(2)
    lo = lax.broadcasted_iota(I32, (ATTN_QBLK, LANES), 1) < HEAD_DIM
    scale = HEAD_DIM ** -0.5 * LOG2_E
    lane = lax.broadcasted_iota(I32, (ATTN_QBLK, LANES), 1)
    for rb, sb in [(rb, sb) for rb in range(q_ref.shape[0]) for sb in range(TQ // ATTN_QBLK)]:
        l0 = i * TQ + sb * ATTN_QBLK
        lse_tile = jnp.zeros((ATTN_QBLK, LANES), F32)
        ws = pl.multiple_of(jnp.clip(l0 - HALF_SPAN, 0, L - ATTN_KWIN), HALF_SPAN)
        var = jnp.where(l0 == 0, 0, jnp.where(l0 == L - ATTN_QBLK, 2, 1))
        rows = slice(sb * ATTN_QBLK, (sb + 1) * ATTN_QBLK)
        for hp in range(ATTN_HEADS // 2):
            cols = slice(hp * LANES, (hp + 1) * LANES)
            q2 = q_ref[rb, rows, cols]
            k2 = k_ref[rb, pl.ds(ws, ATTN_KWIN), cols]
            v2 = v_ref[rb, pl.ds(ws, ATTN_KWIN), cols]
            zero = jnp.zeros_like(q2)
            qs = jnp.concatenate([jnp.where(lo, q2, zero), jnp.where(lo, zero, q2)], axis=0)
            s = lax.dot_general(qs, k2, NT_DIMS, preferred_element_type=F32)
            bias = jnp.concatenate([tab_ref[var, 2 * hp], tab_ref[var, 2 * hp + 1]], axis=0)
            logits = s * scale + bias
            m = jnp.max(logits, axis=-1, keepdims=True)
            p = jnp.exp2(logits - m)
            den = jnp.sum(p, axis=-1, keepdims=True)
            pv = jnp.dot(p.astype(BF16), v2, preferred_element_type=F32) / den
            lse = m * LN_2 + jnp.log(den)
            o_ref[rb, rows, cols] = jnp.where(lo, pv[:ATTN_QBLK], pv[ATTN_QBLK:]).astype(BF16)
            lse_tile = jnp.where(lane == 2 * hp, lse[:ATTN_QBLK], lse_tile)
            lse_tile = jnp.where(lane == 2 * hp + 1, lse[ATTN_QBLK:], lse_tile)
        lse_ref[rb, rows, :] = lse_tile


def _attn_branch(attn, rel_bias, dil):
    B, _, L, _ = attn.shape
    TQ = min(ATTN_STEP_ROWS, L)
    RB = ATTN_STEP_ROWS // TQ
    idx = jnp.asarray(_bucket_tables(dil))
    body = functools.partial(_attn_body, L=L, TQ=TQ)
    kv = lambda j: pl.BlockSpec((None, RB, L, ATTN_W), lambda b, r, i: (b, r, 0, j))
    q_spec = pl.BlockSpec((None, RB, TQ, ATTN_W), lambda b, r, i: (b, r, i, 0))
    return pl.pallas_call(
        body,
        grid=(B, dil // RB, L // TQ),
        in_specs=[pl.BlockSpec(memory_space=pltpu.SMEM),
                  pl.BlockSpec(idx.shape, lambda b, r, i: (0, 0, 0)),
                  q_spec, kv(1), kv(2)],
        out_specs=[q_spec, pl.BlockSpec((None, RB, TQ, LANES), lambda b, r, i: (b, r, i, 0))],
        out_shape=[jax.ShapeDtypeStruct((B, dil, L, ATTN_W), BF16),
                   jax.ShapeDtypeStruct((B, dil, L, LANES), F32)],
        scratch_shapes=[pltpu.VMEM((3, ATTN_HEADS, ATTN_QBLK, ATTN_KWIN), F32)],
        compiler_params=pltpu.CompilerParams(dimension_semantics=("arbitrary",) * 3,
                                             vmem_limit_bytes=48 * 2**20),
        name=f"attn_d{dil}",
    )(rel_bias, idx, attn, attn, attn)


def _gla_body(q_ref, k_ref, v_ref, gate_ref, bf_ref, bb_ref, ng_ref, out_ref, of_ref, ob_ref, st_ref, *, S):
    C = GLA_CHUNK
    nc = S // C
    r = lax.broadcasted_iota(I32, (C, LANES), 0)
    s2 = lax.broadcasted_iota(I32, (C, LANES), 1) % HEAD_DIM
    lo = lax.broadcasted_iota(I32, (C, LANES), 1) < HEAD_DIM
    blockdiag = ((lax.broadcasted_iota(I32, (2 * GLA_DV, LANES), 0) < GLA_DV)
                 == (lax.broadcasted_iota(I32, (2 * GLA_DV, LANES), 1) < HEAD_DIM))
    st_ref[...] = jnp.zeros_like(st_ref)

    def finish(rows, o):
        for hh in range(2):
            cols = slice(hh * GLA_DV, (hh + 1) * GLA_DV)
            oh = o[:, cols]
            oh = oh * lax.rsqrt(jnp.mean(oh * oh, axis=-1, keepdims=True) + NORM_EPS) * ng_ref[0, :, cols]
            gate = gate_ref[0, rows, cols].astype(F32)
            out_ref[0, rows, cols] = (oh * (gate * _sigmoid(gate))).astype(BF16)

    def one_chunk(ci, direction, last):
        mask = (r >= s2) if direction == 0 else (r <= s2)
        b_ref, mine_ref, other_ref = (bf_ref, of_ref, ob_ref) if direction == 0 else (bb_ref, ob_ref, of_ref)
        rows = pl.ds(pl.multiple_of(ci * C, C), C)
        q = q_ref[0, rows, :].astype(F32) * (HEAD_DIM ** -0.5)
        k = k_ref[0, rows, :].astype(F32)
        v = v_ref[0, rows, :]
        b = b_ref[0, rows, :]
        g_tot = b[C - 1:C, :] if direction == 0 else b[0:1, :]
        qd = (q * jnp.exp(b)).astype(BF16)
        kd = (k * jnp.exp(-b)).astype(BF16)
        kst = (k * jnp.exp(g_tot - b)).astype(BF16)
        zk = jnp.zeros_like(kd)
        ks = jnp.concatenate([jnp.where(lo, kd, zk), jnp.where(lo, zk, kd)], axis=0)
        a = lax.dot_general(qd, ks, NT_DIMS, preferred_element_type=F32)
        a = jnp.where(mask, a, 0.0).astype(BF16)
        zv = jnp.zeros((C, GLA_DV), BF16)
        vblk = jnp.concatenate([jnp.concatenate([v[:, :GLA_DV], zv], axis=1),
                                jnp.concatenate([zv, v[:, GLA_DV:]], axis=1)], axis=0)
        st = st_ref[direction]
        o = (jnp.dot(a, vblk, preferred_element_type=F32)
             + lax.dot_general(qd, st.astype(BF16), NT_DIMS, preferred_element_type=F32))
        if last:
            finish(rows, o + other_ref[rows, :])
        else:
            mine_ref[rows, :] = o
        d_st = jnp.dot(v.astype(F32).T.astype(BF16), kst, preferred_element_type=F32)
        st_ref[direction] = st * jnp.exp(g_tot) + jnp.where(blockdiag, d_st, 0.0)

    def step(last):
        def body(i, carry):
            one_chunk(i, 0, last)
            one_chunk(nc - 1 - i, 1, last)
            return carry
        return body

    lax.fori_loop(0, nc // 2, step(False), 0, unroll=16)
    lax.fori_loop(nc // 2, nc, step(True), 0, unroll=16)


def _gla(gla, bcum, norm_g, B, S):
    g3 = gla.reshape(B, S, GLA_COLS)
    b3 = bcum.reshape(B, S, 2 * GLA_KEY_W)
    ng = norm_g.reshape(2, 1, 2 * GLA_DV)
    body = functools.partial(_gla_body, S=S)
    spec = lambda w, off: pl.BlockSpec((1, S, w), lambda b, p: (b, 0, off + p))
    out = pl.pallas_call(
        body,
        grid=(B, 2),
        in_specs=[spec(LANES, 0),
                  spec(LANES, GLA_KEY_W // LANES),
                  spec(2 * GLA_DV, (2 * GLA_KEY_W) // (2 * GLA_DV)),
                  spec(2 * GLA_DV, (2 * GLA_KEY_W + GLA_VAL_W) // (2 * GLA_DV)),
                  spec(LANES, 0),
                  spec(LANES, GLA_KEY_W // LANES),
                  pl.BlockSpec((1, 1, 2 * GLA_DV), lambda b, p: (p, 0, 0))],
        out_specs=pl.BlockSpec((1, S, 2 * GLA_DV), lambda b, p: (b, 0, p)),
        out_shape=jax.ShapeDtypeStruct((B, S, GLA_VAL_W), BF16),
        scratch_shapes=[pltpu.VMEM((S, 2 * GLA_DV), F32), pltpu.VMEM((S, 2 * GLA_DV), F32),
                        pltpu.VMEM((2, 2 * GLA_DV, LANES), F32)],
        compiler_params=pltpu.CompilerParams(dimension_semantics=("arbitrary", "arbitrary"),
                                             vmem_limit_bytes=56 * 2**20),
        name="gla",
    )(g3, g3, g3, g3, b3, b3, ng)
    return out.reshape(B * S, GLA_VAL_W)


def _natural_rows(blk_ref, u_ref, dil, tm):
    if dil == 1:
        return blk_ref[0, 0].astype(F32)
    chunks = blk_ref.shape[-1] // LANES
    for r in range(dil):
        for c in range(chunks):
            u_ref[c, pl.ds(r, tm // dil, stride=dil), :] = blk_ref[0, r, :, c * LANES:(c + 1) * LANES].astype(F32)
    return jnp.concatenate([u_ref[c] for c in range(chunks)], axis=1)


def _per_head_lanes(w):
    head = lax.broadcasted_iota(I32, (LANES, ATTN_W), 0)
    col = lax.broadcasted_iota(I32, (LANES, ATTN_W), 1)
    spread = (col // HEAD_DIM == head).astype(F32).astype(BF16)
    hi, lo, _ = _split3(w)
    return (jnp.dot(hi, spread, preferred_element_type=F32) + jnp.dot(lo, spread, preferred_element_type=F32))


def _mix_body(o1, o2, o3, l1, l2, l3, g_ref, x_ref, wo_ref, n2_ref, rwh_ref, rwl_ref, rb_ref,
              x1_ref, h_ref, m_ref, slab_ref, tcnt_ref, u_ref, wob_ref, *, tm):
    @pl.when((pl.program_id(0) == 0) & (pl.program_id(1) == 0))
    def _():
        wob_ref[...] = wo_ref[...].astype(BF16)

    ls = [_natural_rows(l, u_ref, d, tm) for l, d in zip((l1, l2, l3), DILATIONS)]
    m = jnp.maximum(jnp.maximum(ls[0], ls[1]), ls[2])
    es = [jnp.exp(l - m) for l in ls]
    den = es[0] + es[1] + es[2]
    a = _per_head_lanes(es[0] / den) * _natural_rows(o1, u_ref, DILATIONS[0], tm)
    a = a + _per_head_lanes(es[1] / den) * _natural_rows(o2, u_ref, DILATIONS[1], tm)
    a = a + _per_head_lanes(es[2] / den) * _natural_rows(o3, u_ref, DILATIONS[2], tm)
    mix = (jnp.dot(a.astype(BF16), wob_ref[:ATTN_W, :], preferred_element_type=F32)
           + jnp.dot(g_ref[...], wob_ref[ATTN_W:, :], preferred_element_type=F32))
    x1 = x_ref[...] + mix
    x1_ref[...] = x1
    h = x1 * lax.rsqrt(jnp.mean(x1 * x1, axis=-1, keepdims=True) + NORM_EPS) * n2_ref[...]
    h_ref[...] = h.astype(BF16)

    hh, hl, _ = _split3(h)
    dg = lambda u, v: lax.dot_general(u, v, NT_DIMS, preferred_element_type=F32)
    logits = dg(rwh_ref[...], hh) + dg(rwl_ref[...], hh) + dg(rwh_ref[...], hl) + rb_ref[...]
    ie = lax.broadcasted_iota(I32, (N_EXPERTS, tm), 0)
    cur = logits
    vals, idxs = [], []
    for _ in range(TOP_K):
        mx = jnp.max(cur, axis=0, keepdims=True)
        ix = jnp.min(jnp.where(cur == mx, ie, N_EXPERTS), axis=0, keepdims=True)
        vals.append(mx)
        idxs.append(ix)
        cur = jnp.where(ie == ix, -jnp.inf, cur)
    ex = [jnp.exp(v - vals[0]) for v in vals]
    den = ex[0] + ex[1] + ex[2] + ex[3]
    gates = [e / den for e in ex]

    onehots = [ie == ix for ix in idxs]
    chosen = (onehots[0] | onehots[1] | onehots[2] | onehots[3]).astype(F32)
    tr = lax.broadcasted_iota(I32, (tm, tm), 0)
    tc = lax.broadcasted_iota(I32, (tm, tm), 1)
    before = jnp.dot(chosen.astype(BF16), (tr < tc).astype(BF16), preferred_element_type=F32)
    count = jnp.broadcast_to(jnp.sum(chosen, axis=1, keepdims=True), (N_EXPERTS, LANES))
    tcnt_ref[0] = count
    run = jnp.floor((count + (RUN_ALIGN - 1.0)) * (1.0 / RUN_ALIGN)) * RUN_ALIGN
    er = lax.broadcasted_iota(I32, (N_EXPERTS, N_EXPERTS), 0)
    ec = lax.broadcasted_iota(I32, (N_EXPERTS, N_EXPERTS), 1)
    run_start = jnp.dot((ec < er).astype(BF16), run.astype(BF16), preferred_element_type=F32)
    where_ = before + run_start[:, 0:1]
    locs = [jnp.sum(jnp.where(oh, where_, 0.0), axis=0, keepdims=True) for oh in onehots]

    def pack(rows, n):
        sub = lax.broadcasted_iota(I32, (n, tm), 0)
        out = jnp.zeros((n, tm), rows[0].dtype)
        for k, row in enumerate(rows):
            out = jnp.where(sub == k, row, out)
        return out

    m_ref[...] = pack([l.astype(I32) for l in locs], 8)
    slab = pack(gates + locs, 8)
    slab_ref[...] = jnp.concatenate([slab, jnp.zeros((LANES - 8, tm), F32)], axis=0).T


def _mix(outs, lses, g, x2, B, S, w_out, norm2_g, router_w, router_b):
    T, D = x2.shape
    tm = ROUTE_TILE
    nt = S // tm
    wo_spec = pl.BlockSpec(w_out.shape, lambda b, i: (0, 0), pipeline_mode=pl.Buffered(1))
    rwt = router_w.T
    rwh = rwt.astype(BF16)
    rwl = (rwt - rwh.astype(F32)).astype(BF16)
    rb = router_b.reshape(N_EXPERTS, 1)
    n2 = norm2_g.reshape(1, D)
    row = lambda c: pl.BlockSpec((tm, c), lambda b, i: (b * nt + i, 0))
    full = lambda a: pl.BlockSpec(a.shape, lambda b, i: (0,) * a.ndim)
    col = pl.BlockSpec((8, tm), lambda b, i: (0, b * nt + i))
    res = [pl.BlockSpec((1, d, tm // d, ATTN_W), lambda b, i: (b, 0, i, 0)) for d in DILATIONS]
    res_lse = [pl.BlockSpec((1, d, tm // d, LANES), lambda b, i: (b, 0, i, 0)) for d in DILATIONS]
    return pl.pallas_call(
        functools.partial(_mix_body, tm=tm),
        grid=(B, nt),
        in_specs=res + res_lse + [row(GLA_VAL_W), row(D), wo_spec, full(n2), full(rwh), full(rwl), full(rb)],
        out_specs=[row(D), row(D), col, row(LANES),
                   pl.BlockSpec((1, N_EXPERTS, LANES), lambda b, i: (b * nt + i, 0, 0))],
        out_shape=[jax.ShapeDtypeStruct((T, D), F32),
                   jax.ShapeDtypeStruct((T, D), BF16),
                   jax.ShapeDtypeStruct((8, T), I32),
                   jax.ShapeDtypeStruct((T, LANES), F32),
                   jax.ShapeDtypeStruct((T // tm, N_EXPERTS, LANES), F32)],
        scratch_shapes=[pltpu.VMEM((ATTN_W // LANES, tm, LANES), F32), pltpu.VMEM(w_out.shape, BF16)],
        compiler_params=pltpu.CompilerParams(dimension_semantics=("arbitrary", "arbitrary"),
                                             vmem_limit_bytes=48 * 2**20),
        name="mix_router",
    )(*outs, *lses, g, x2, w_out, n2, rwh, rwl, rb)


def _run_groups(c):
    return (c + (RUN_ALIGN - 1)) >> RUN_SHIFT


def _plan_body(tc_ref, rs_ref, p0_ref, pn_ref, be_ref, hv_ref, nu_ref, *, nt, nblk):
    shift = MOE_BLOCK.bit_length() - 1
    start = jnp.int32(0)
    for ex in range(N_EXPERTS):
        rows_of = lambda i, ex=ex: _run_groups(tc_ref[i, ex]) << RUN_SHIFT
        tot = lax.fori_loop(0, nt, lambda i, tot: tot + rows_of(i), jnp.int32(0), unroll=True)
        nb = (tot + (MOE_BLOCK - 1)) >> shift
        pad = (nb << shift) - tot
        p0_ref[ex] = start
        pn_ref[ex] = pad >> RUN_SHIFT

        def run(i, at, ex=ex):
            rs_ref[i, ex] = at
            return at + rows_of(i)

        lax.fori_loop(0, nt, run, start + pad, unroll=True)
        b0 = start >> shift

        def fill(j, carry, b0=b0, ex=ex):
            be_ref[b0 + j] = jnp.int32(ex)
            hv_ref[b0 + j] = jnp.int32(MOE_BLOCK // MOE_PART)
            return carry

        lax.fori_loop(0, nb, fill, 0)

        @pl.when(nb > 0)
        def _(b0=b0, pad=pad):
            hv_ref[b0] = (MOE_BLOCK - pad + (MOE_PART - 1)) >> (MOE_PART.bit_length() - 1)

        start = start + (nb << shift)
    used = start >> shift

    def tail(j, carry):
        be_ref[j] = jnp.int32(N_EXPERTS - 1)
        hv_ref[j] = jnp.int32(0)
        return carry

    lax.fori_loop(used, nblk, tail, 0)
    nu_ref[0] = used


def _plan(tcnt, nblk):
    nt = tcnt.shape[0]
    smem = pl.BlockSpec(memory_space=pltpu.SMEM)
    return pl.pallas_call(
        functools.partial(_plan_body, nt=nt, nblk=nblk),
        in_specs=[smem],
        out_specs=[smem] * 6,
        out_shape=[jax.ShapeDtypeStruct(tcnt.shape, I32),
                   jax.ShapeDtypeStruct((N_EXPERTS,), I32),
                   jax.ShapeDtypeStruct((N_EXPERTS,), I32),
                   jax.ShapeDtypeStruct((nblk,), I32),
                   jax.ShapeDtypeStruct((nblk,), I32),
                   jax.ShapeDtypeStruct((1,), I32)],
        name="plan",
    )(tcnt)


def _for_each_run(tc_ref, rs_ref, i, fn):
    big = 4 * RUN_ALIGN
    start = jnp.int32(0)
    for ex in range(N_EXPERTS):
        ng = _run_groups(tc_ref[i, ex])
        g0 = rs_ref[i, ex]

        def piece(q, carry, start=start, g0=g0):
            fn(pl.multiple_of(start + q * big, RUN_ALIGN), pl.multiple_of(g0 + q * big, RUN_ALIGN), big)
            return carry

        def single(q, carry, start=start, g0=g0):
            fn(pl.multiple_of(start + q * RUN_ALIGN, RUN_ALIGN), pl.multiple_of(g0 + q * RUN_ALIGN, RUN_ALIGN),
               RUN_ALIGN)
            return carry

        lax.fori_loop(0, ng >> 2, piece, 0)
        lax.fori_loop((ng >> 2) << 2, ng, single, 0)
        start = start + (ng << RUN_SHIFT)


def _group_count(tc_ref, i):
    n = jnp.int32(0)
    for ex in range(N_EXPERTS):
        n = n + _run_groups(tc_ref[i, ex])
    return n


def _wait_groups(n, src_ref, dst_ref, sem):
    def waits(count, rows):
        def body(q, carry):
            pltpu.make_async_copy(src_ref.at[pl.ds(0, rows)], dst_ref.at[pl.ds(0, rows)], sem).wait()
            return carry
        lax.fori_loop(0, count, body, 0)

    waits(n >> 6, 64 * RUN_ALIGN)
    waits((n >> 3) & 7, 8 * RUN_ALIGN)
    waits(n & 7, RUN_ALIGN)


def _dispatch_body(tc_ref, rs_ref, p0_ref, pn_ref, nu_ref, m_ref, h_ref, xin_ref, g_ref, z_ref, sem, zsem, *, tm, R):
    i = pl.program_id(0)
    slot = i % 2
    rows = m_ref[...].astype(jnp.int16)
    jj = lax.broadcasted_iota(I32, (R, tm), 0).astype(jnp.int16)
    hit = (jj == rows[0:1]) | (jj == rows[1:2]) | (jj == rows[2:3]) | (jj == rows[3:4])
    onehot = jnp.where(hit, jnp.ones((), BF16), jnp.zeros((), BF16))
    g_ref[slot] = jnp.dot(onehot, h_ref[...], preferred_element_type=F32).astype(BF16)

    group = lambda ref, r: ref.at[pl.ds(r, RUN_ALIGN)]

    @pl.when(i > 0)
    def _():
        _wait_groups(_group_count(tc_ref, i - 1), g_ref.at[1 - slot], xin_ref, sem.at[1 - slot])

    _for_each_run(tc_ref, rs_ref, i, lambda lr, gr, n: pltpu.make_async_copy(
        g_ref.at[slot, pl.ds(lr, n)], xin_ref.at[pl.ds(gr, n)], sem.at[slot]).start())

    block = lambda j: xin_ref.at[pl.ds(pl.multiple_of(j * MOE_BLOCK, MOE_BLOCK), MOE_BLOCK)]
    n_blocks = xin_ref.shape[0] // MOE_BLOCK

    @pl.when(i == 0)
    def _():
        z_ref[...] = jnp.zeros_like(z_ref)
        for ex in range(N_EXPERTS):
            def body(q, carry, ex=ex):
                pltpu.make_async_copy(group(z_ref, 0), group(xin_ref, pl.multiple_of(p0_ref[ex] + q * RUN_ALIGN, RUN_ALIGN)), zsem).start()
                return carry
            lax.fori_loop(0, pn_ref[ex], body, 0)

        def fill(j, carry):
            pltpu.make_async_copy(z_ref, block(j), zsem).start()
            return carry

        lax.fori_loop(nu_ref[0], n_blocks, fill, 0)

    @pl.when(i == pl.num_programs(0) - 1)
    def _():
        _wait_groups(_group_count(tc_ref, i), g_ref.at[slot], xin_ref, sem.at[slot])
        total = jnp.int32(0)
        for ex in range(N_EXPERTS):
            total = total + pn_ref[ex]
        _wait_groups(total, z_ref, xin_ref, zsem)

        def fill_wait(j, carry):
            pltpu.make_async_copy(z_ref, block(j), zsem).wait()
            return carry

        lax.fori_loop(nu_ref[0], n_blocks, fill_wait, 0)


def _dispatch(tcnt, rowstart, pad0, padn, n_used, m_t, h, n_rows):
    T, D = h.shape
    tm = ROUTE_TILE
    col = pl.BlockSpec((8, tm), lambda i, *_: (0, i))
    return pl.pallas_call(
        functools.partial(_dispatch_body, tm=tm, R=GROUP_ROWS),
        grid_spec=pltpu.PrefetchScalarGridSpec(
            num_scalar_prefetch=5,
            grid=(T // tm,),
            in_specs=[col, pl.BlockSpec((tm, D), lambda i, *_: (i, 0))],
            out_specs=pl.BlockSpec(memory_space=pl.ANY),
            scratch_shapes=[pltpu.VMEM((2, GROUP_ROWS, D), BF16), pltpu.VMEM((MOE_BLOCK, D), BF16),
                            pltpu.SemaphoreType.DMA((2,)), pltpu.SemaphoreType.DMA(())]),
        out_shape=jax.ShapeDtypeStruct((n_rows, D), BF16),
        compiler_params=pltpu.CompilerParams(dimension_semantics=("arbitrary",), vmem_limit_bytes=56 * 2**20),
        name="dispatch",
    )(tcnt, rowstart, pad0, padn, n_used, m_t, h)


def _moe_body(be_ref, nu_ref, hv_ref, x_ref, w1g_ref, w1l_ref, b1g_ref, b1l_ref, w2_ref, b2_ref, y_ref):
    del be_ref
    j = pl.program_id(0)
    live = j < nu_ref[0]

    def expert_mlp(rows):
        x = x_ref[rows, :]
        hg = lax.dot_general(x, w1g_ref[0], NT_DIMS, preferred_element_type=F32) + b1g_ref[0]
        hl = lax.dot_general(x, w1l_ref[0], NT_DIMS, preferred_element_type=F32) + b1l_ref[0]
        xg = jnp.minimum(hg, SWIGLU_LIMIT)
        xl = jnp.clip(hl, -SWIGLU_LIMIT, SWIGLU_LIMIT)
        act = xg * _sigmoid(SWIGLU_ALPHA * xg) * (xl + 1.0)
        y = jnp.dot(act.astype(BF16), w2_ref[0].astype(BF16), preferred_element_type=F32) + b2_ref[0]
        y_ref[rows, :] = y.astype(BF16)

    for parts in range(1, MOE_BLOCK // MOE_PART + 1):
        @pl.when(live & (hv_ref[j] == parts))
        def _(parts=parts):
            skip = MOE_BLOCK - parts * MOE_PART
            expert_mlp(slice(skip, MOE_BLOCK))
            if skip:
                y_ref[:skip, :] = jnp.zeros((skip, y_ref.shape[1]), y_ref.dtype)


def _moe(block_e, n_used, half_block, xin, w1g, w1l, b1, w2, b2):
    n_rows = xin.shape[0]
    D = w2.shape[2]
    F = w2.shape[1]
    nblk = n_rows // MOE_BLOCK
    b1s = jnp.transpose(b1.reshape(N_EXPERTS, F, 2), (2, 0, 1)).reshape(2, N_EXPERTS, 1, F)
    b1spec = lambda k: pl.BlockSpec((None, 1, 1, F), lambda j, be, nu, hv: (k, be[j], 0, 0))
    b2r = b2.reshape(N_EXPERTS, 1, D)
    xspec = pl.BlockSpec((MOE_BLOCK, D), lambda j, be, nu, hv: (jnp.minimum(j, nu[0] - 1), 0))
    wspec = lambda a: pl.BlockSpec((1,) + a.shape[1:], lambda j, be, nu, hv: (be[j], 0, 0))
    return pl.pallas_call(
        _moe_body,
        grid_spec=pltpu.PrefetchScalarGridSpec(
            num_scalar_prefetch=3,
            grid=(nblk,),
            in_specs=[xspec, wspec(w1g), wspec(w1l), b1spec(0), b1spec(1), wspec(w2), wspec(b2r)],
            out_specs=xspec),
        out_shape=jax.ShapeDtypeStruct(xin.shape, BF16),
        input_output_aliases={3: 0},
        compiler_params=pltpu.CompilerParams(dimension_semantics=("arbitrary",),
                                             vmem_limit_bytes=48 * 2**20),
        name="moe",
    )(block_e, n_used, half_block, xin, w1g, w1l, b1s, b1s, w2, b2r)


def _combine_body(tc_ref, rs_ref, slab_ref, y_ref, x1_ref, fg_ref, out_ref, g_ref, sem, *, tm, R):
    i = pl.program_id(0)
    slot = i % 2

    def fetch(tile, slot):
        _for_each_run(tc_ref, rs_ref, tile, lambda lr, gr, n: pltpu.make_async_copy(
            y_ref.at[pl.ds(gr, n)], g_ref.at[slot, pl.ds(lr, n)], sem.at[slot]).start())

    @pl.when(i == 0)
    def _():
        g_ref[...] = jnp.zeros_like(g_ref)
        fetch(i, slot)

    @pl.when(i + 1 < pl.num_programs(0))
    def _():
        fetch(i + 1, 1 - slot)

    slab = slab_ref[...]
    gates = slab[:, 0:TOP_K].astype(BF16)
    rows = slab[:, TOP_K:2 * TOP_K].astype(I32).astype(jnp.int16)
    jj = lax.broadcasted_iota(I32, (tm, R), 1).astype(jnp.int16)
    sel = jnp.zeros((tm, R), BF16)
    for k in range(TOP_K):
        sel = jnp.where(jj == rows[:, k:k + 1], gates[:, k:k + 1], sel)

    _wait_groups(_group_count(tc_ref, i), y_ref, g_ref.at[slot], sem.at[slot])
    acc = x1_ref[...] + jnp.dot(sel, g_ref[slot], preferred_element_type=F32)
    out_ref[...] = acc * lax.rsqrt(jnp.mean(acc * acc, axis=-1, keepdims=True) + NORM_EPS) * fg_ref[...]


def _combine(tcnt, rowstart, slab, y, x1, final_g):
    T, D = x1.shape
    tm = ROUTE_TILE
    fg = final_g.reshape(1, D)
    return pl.pallas_call(
        functools.partial(_combine_body, tm=tm, R=GROUP_ROWS),
        grid_spec=pltpu.PrefetchScalarGridSpec(
            num_scalar_prefetch=2,
            grid=(T // tm,),
            in_specs=[pl.BlockSpec((tm, LANES), lambda i, *_: (i, 0)),
                      pl.BlockSpec(memory_space=pl.ANY),
                      pl.BlockSpec((tm, D), lambda i, *_: (i, 0)),
                      pl.BlockSpec((1, D), lambda i, *_: (0, 0))],
            out_specs=pl.BlockSpec((tm, D), lambda i, *_: (i, 0)),
            scratch_shapes=[pltpu.VMEM((2, GROUP_ROWS, D), BF16), pltpu.SemaphoreType.DMA((2,))]),
        out_shape=jax.ShapeDtypeStruct((T, D), F32),
        compiler_params=pltpu.CompilerParams(dimension_semantics=("arbitrary",), vmem_limit_bytes=56 * 2**20),
        name="combine",
    )(tcnt, rowstart, slab, y, x1, fg)


def _mixers(x2, B, S, norm1_g, w_in, rel_bias, wg_f, bg_f, wg_b, bg_b, gla_norm_g, w1):
    a1, a4, a16, gla, bcum, w1g, w1l = _in_proj(x2, B, S, norm1_g, w_in, wg_f, bg_f, wg_b, bg_b, w1)
    branches = [_attn_branch(a, rel_bias, d) for a, d in zip((a1, a4, a16), DILATIONS)]
    g = _gla(gla, bcum, gla_norm_g, B, S)
    return [o for o, _ in branches], [l for _, l in branches], g, w1g, w1l


def _moe_layer(h, m_t, slab, tcnt_f, x1, w1g, w1l, b1, w2, b2, final_g):
    T = x1.shape[0]
    nt = T // ROUTE_TILE
    nblk = (T * TOP_K + nt * N_EXPERTS * (RUN_ALIGN - 1)) // MOE_BLOCK + N_EXPERTS
    tcnt = tcnt_f[:, :, 0].astype(I32)
    rowstart, pad0, padn, block_e, half_block, n_used = _plan(tcnt, nblk)
    xin = _dispatch(tcnt, rowstart, pad0, padn, n_used, m_t, h, nblk * MOE_BLOCK)
    y = _moe(block_e, n_used, half_block, xin, w1g, w1l, b1, w2, b2)
    return _combine(tcnt, rowstart, slab, y, x1, final_g)


def kernel(x, norm1_g, w_in, rel_bias, gla_wg_fwd, gla_bg_fwd, gla_wg_bwd, gla_bg_bwd, gla_norm_g, w_out, norm2_g, router_w, router_b, moe_w1, moe_b1, moe_w2, moe_b2, final_g):
    B, S, D = x.shape
    assert w_in.shape[0] == 1, "one layer"
    x2 = x.reshape(B * S, D)
    outs, lses, g, w1g, w1l = _mixers(x2, B, S, norm1_g[0], jnp.swapaxes(w_in, 1, 2), rel_bias, gla_wg_fwd[0], gla_bg_fwd[0],
                                      gla_wg_bwd[0], gla_bg_bwd[0], gla_norm_g[0], moe_w1[0])
    x1, h, m_t, slab, tcnt = _mix(outs, lses, g, x2, B, S, w_out[0], norm2_g[0], router_w[0], router_b[0])
    out = _moe_layer(h, m_t, slab, tcnt, x1, w1g, w1l, moe_b1[0], moe_w2[0], moe_b2[0], final_g)
    return out.reshape(B, S, D)
```

```python
import functools

import jax
import jax.numpy as jnp
import numpy as np
from jax import lax
from jax.experimental import pallas as pl
from jax.experimental.pallas import tpu as pltpu

F32, BF16, I32 = jnp.float32, jnp.bfloat16, jnp.int32

LANES = 128
HEAD_DIM = 64
ATTN_HEADS = 8
ATTN_W = ATTN_HEADS * HEAD_DIM
HALF_SPAN = 64
ATTN_QBLK = 128
ATTN_KWIN = 256
ATTN_STEP_ROWS = 2048
DILATIONS = (1, 4, 16)
T5_BUCKETS = 32
T5_MAX_DISTANCE = 1024
NEG_INF = -1e30
LOG2_E = 1.4426950408889634
LN_2 = 0.6931471805599453
GLA_KEY_W = 256
GLA_VAL_W = 512
GLA_DV = 128
GLA_RANK = 16
GLA_TAU = 16.0
GLA_CHUNK = 64
N_EXPERTS = 32
TOP_K = 4
SWIGLU_LIMIT = 7.0
SWIGLU_ALPHA = 1.702
NORM_EPS = 1e-5
MOE_BLOCK = 1024
MOE_PART = 256
W1_SUBTILE = 512
ROUTE_TILE = 512
RUN_ALIGN = 16
RUN_SHIFT = RUN_ALIGN.bit_length() - 1
GROUP_ROWS = ROUTE_TILE * TOP_K + N_EXPERTS * RUN_ALIGN

ATTN_COLS = 3 * ATTN_W
GLA_COLS = 2 * GLA_KEY_W + 2 * GLA_VAL_W
Z_COLS = LANES
PROJ_COLS = ATTN_COLS + GLA_COLS + Z_COLS

NT_DIMS = (((1,), (1,)), ((), ()))


def _sigmoid(x):
    return 1.0 / (1.0 + jnp.exp(-x))


def _split3(a):
    h1 = a.astype(BF16)
    r1 = a - h1.astype(F32)
    h2 = r1.astype(BF16)
    h3 = (r1 - h2.astype(F32)).astype(BF16)
    return h1, h2, h3


def _dot_f32(a, b):
    ah, al, _ = _split3(a)
    bh, bl, _ = _split3(b)
    d = lambda u, v: jnp.dot(u, v, preferred_element_type=F32)
    return d(ah, bh) + d(al, bh) + d(ah, bl)


def _split_w1_tile(w_ref, g_ref, l_ref, t_ref):
    sub = t_ref.shape[1]
    for s in range(w_ref.shape[2] // sub):
        wt = w_ref[0, :, s * sub:(s + 1) * sub].T
        out_rows = slice(s * sub // 2, (s + 1) * sub // 2)
        for c in range(wt.shape[1] // LANES):
            cols = slice(c * LANES, (c + 1) * LANES)
            t_ref[c] = wt[:, cols]
            g_ref[0, out_rows, cols] = t_ref[c, pl.ds(0, sub // 2, stride=2), :].astype(BF16)
            l_ref[0, out_rows, cols] = t_ref[c, pl.ds(1, sub // 2, stride=2), :].astype(BF16)


def _in_proj_body(x_ref, g_ref, w_ref, wg_ref, bg_ref, w1_ref, a1_ref, a4_ref, a16_ref, gla_ref, b_ref,
                  w1g_ref, w1l_ref, t_ref, wb_ref, wf_ref, wsem, *, tm):
    @pl.when((pl.program_id(0) == 0) & (pl.program_id(1) == 0))
    def _():
        fetch = pltpu.make_async_copy(w_ref.at[0], wf_ref, wsem)
        fetch.start()
        fetch.wait()
        wb_ref[...] = jnp.zeros_like(wb_ref)
        wb_ref[:wf_ref.shape[0], :] = wf_ref[...].astype(BF16)

    _split_w1_tile(w1_ref, w1g_ref, w1l_ref, t_ref)

    x = x_ref[...]
    n = x * lax.rsqrt(jnp.mean(x * x, axis=-1, keepdims=True) + NORM_EPS) * g_ref[...]
    p = lax.dot_general(n.astype(BF16), wb_ref[...], NT_DIMS, preferred_element_type=F32)
    gla_ref[...] = p[:, ATTN_COLS:ATTN_COLS + GLA_COLS].astype(BF16)

    qkv = p[:, :ATTN_COLS].astype(BF16)
    a1_ref[0, 0] = qkv
    dst = lax.broadcasted_iota(I32, (tm, tm), 0)
    src = lax.broadcasted_iota(I32, (tm, tm), 1)
    for dil, a_ref in ((DILATIONS[1], a4_ref), (DILATIONS[2], a16_ref)):
        per = tm // dil
        perm = (src == (dst % per) * dil + dst // per).astype(F32).astype(BF16)
        moved = jnp.dot(perm, qkv, preferred_element_type=F32).astype(BF16)
        for r in range(dil):
            a_ref[0, r] = moved[r * per:(r + 1) * per]

    z = p[:, ATTN_COLS + GLA_COLS:]
    zz = _dot_f32(z, wg_ref[...]) + bg_ref[...]
    la = (jnp.minimum(zz, 0.0) - jnp.log1p(jnp.exp(-jnp.abs(zz)))) * (1.0 / GLA_TAU)
    C = GLA_CHUNK
    r_i = lax.broadcasted_iota(I32, (C, C), 0)
    c_i = lax.broadcasted_iota(I32, (C, C), 1)
    lower = (r_i >= c_i).astype(BF16)
    upper = (r_i <= c_i).astype(BF16)
    W = GLA_KEY_W
    for j in range(tm // C):
        rows = slice(j * C, (j + 1) * C)
        for tri, cols in ((lower, slice(0, W)), (upper, slice(W, 2 * W))):
            terms = jnp.concatenate(_split3(la[rows, cols]), axis=1)
            cs = jnp.dot(tri, terms, preferred_element_type=F32)
            b_ref[rows, cols] = cs[:, :W] + cs[:, W:2 * W] + cs[:, 2 * W:]


def _in_proj(x2, B, S, norm_g, w_in_t, wg_f, bg_f, wg_b, bg_b, w1):
    T, D = x2.shape
    tm = 256
    nt = S // tm
    E, _, F2 = w1.shape
    w1_cols = E * F2 // (B * nt)
    per_e = F2 // w1_cols
    assert w1_cols * B * nt == E * F2 and per_e * w1_cols == F2 and w1_cols % W1_SUBTILE == 0
    w1_in = pl.BlockSpec((1, D, w1_cols), lambda b, i: ((b * nt + i) // per_e, 0, (b * nt + i) % per_e))
    w1_out = pl.BlockSpec((1, w1_cols // 2, D), lambda b, i: ((b * nt + i) // per_e, (b * nt + i) % per_e, 0))
    w_spec = pl.BlockSpec(memory_space=pl.ANY)
    wg = jnp.zeros((Z_COLS, 2 * GLA_KEY_W), F32)
    wg = wg.at[:GLA_RANK, :GLA_KEY_W].set(wg_f).at[GLA_RANK:2 * GLA_RANK, GLA_KEY_W:].set(wg_b)
    bg = jnp.concatenate([bg_f, bg_b]).reshape(1, 2 * GLA_KEY_W)
    row = lambda c: pl.BlockSpec((tm, c), lambda b, i: (b * nt + i, 0))
    full = lambda a: pl.BlockSpec(a.shape, lambda b, i: (0,) * a.ndim)
    res = lambda d: pl.BlockSpec((1, d, tm // d, ATTN_COLS), lambda b, i: (b, 0, i, 0))
    res_shape = lambda d: jax.ShapeDtypeStruct((B, d, S // d, ATTN_COLS), BF16)
    g2 = norm_g.reshape(1, D)
    return pl.pallas_call(
        functools.partial(_in_proj_body, tm=tm),
        grid=(B, nt),
        in_specs=[row(D), full(g2), w_spec, full(wg), full(bg), w1_in],
        out_specs=[res(d) for d in DILATIONS] + [row(GLA_COLS), row(2 * GLA_KEY_W), w1_out, w1_out],
        out_shape=[res_shape(d) for d in DILATIONS]
                  + [jax.ShapeDtypeStruct((T, GLA_COLS), BF16), jax.ShapeDtypeStruct((T, 2 * GLA_KEY_W), F32)]
                  + [jax.ShapeDtypeStruct((E, F2 // 2, D), BF16)] * 2,
        scratch_shapes=[pltpu.VMEM((D // LANES, W1_SUBTILE, LANES), F32), pltpu.VMEM((PROJ_COLS, D), BF16),
                        pltpu.VMEM(w_in_t.shape[1:], F32), pltpu.SemaphoreType.DMA(())],
        compiler_params=pltpu.CompilerParams(dimension_semantics=("arbitrary", "arbitrary"),
                                             vmem_limit_bytes=56 * 2**20),
        name="in_proj",
    )(x2, g2, w_in_t, wg, bg, w1)


def _t5_bucket(rel):
    nb = T5_BUCKETS // 2
    max_exact = nb // 2
    n = np.abs(rel)
    large = max_exact + (np.log(np.maximum(n, 1).astype(np.float32) / max_exact)
                         / np.log(T5_MAX_DISTANCE / max_exact) * (nb - max_exact)).astype(np.int32)
    large = np.minimum(large, nb - 1)
    return (np.where(rel > 0, nb, 0) + np.where(n < max_exact, n, large)).astype(np.int32)


def _bucket_tables(dil):
    s = np.arange(ATTN_QBLK)[:, None]
    t = np.arange(ATTN_KWIN)[None, :]
    tabs = []
    for c in (0, HALF_SPAN, 2 * HALF_SPAN):
        off = t - c - s
        tabs.append(np.where(np.abs(off) <= HALF_SPAN, _t5_bucket(off * dil), -1))
    return np.stack(tabs).astype(np.int32)


def _attn_body(rb_ref, idx_ref, q_ref, k_ref, v_ref, o_ref, lse_ref, tab_ref, *, L, TQ):
    first = (pl.program_id(0) == 0) & (pl.program_id(1) == 0) & (pl.program_id(2) == 0)

    @pl.when(first)
    def _():
        for v in range(3):
            idx = idx_ref[v]
            for h in range(ATTN_HEADS):
                fill = lambda j, t, h=h, idx=idx: jnp.where(idx == j, rb_ref[j, h] * LOG2_E, t)
                tab_ref[v, h] = lax.fori_loop(0, T5_BUCKETS, fill, jnp.where(idx < 0, NEG_INF, 0.0).astype(F32))

    i = pl.program_id(2)
    lo = lax.broadcasted_iota(I32, (ATTN_QBLK, LANES), 1) < HEAD_DIM
    scale = HEAD_DIM ** -0.5 * LOG2_E
    lane = lax.broadcasted_iota(I32, (ATTN_QBLK, LANES), 1)
    for rb, sb in [(rb, sb) for rb in range(q_ref.shape[0]) for sb in range(TQ // ATTN_QBLK)]:
        l0 = i * TQ + sb * ATTN_QBLK
        lse_tile = jnp.zeros((ATTN_QBLK, LANES), F32)
        ws = pl.multiple_of(jnp.clip(l0 - HALF_SPAN, 0, L - ATTN_KWIN), HALF_SPAN)
        var = jnp.where(l0 == 0, 0, jnp.where(l0 == L - ATTN_QBLK, 2, 1))
        rows = slice(sb * ATTN_QBLK, (sb + 1) * ATTN_QBLK)
        for hp in range(ATTN_HEADS // 2):
            cols = slice(hp * LANES, (hp + 1) * LANES)
            q2 = q_ref[rb, rows, cols]
            k2 = k_ref[rb, pl.ds(ws, ATTN_KWIN), cols]
            v2 = v_ref[rb, pl.ds(ws, ATTN_KWIN), cols]
            zero = jnp.zeros_like(q2)
            qs = jnp.concatenate([jnp.where(lo, q2, zero), jnp.where(lo, zero, q2)], axis=0)
            s = lax.dot_general(qs, k2, NT_DIMS, preferred_element_type=F32)
            bias = jnp.concatenate([tab_ref[var, 2 * hp], tab_ref[var, 2 * hp + 1]], axis=0)
            logits = s * scale + bias
            m = jnp.max(logits, axis=-1, keepdims=True)
            p = jnp.exp2(logits - m)
            den = jnp.sum(p, axis=-1, keepdims=True)
            pv = jnp.dot(p.astype(BF16), v2, preferred_element_type=F32) / den
            lse = m * LN_2 + jnp.log(den)
            o_ref[rb, rows, cols] = jnp.where(lo, pv[:ATTN_QBLK], pv[ATTN_QBLK:]).astype(BF16)
            lse_tile = jnp.where(lane == 2 * hp, lse[:ATTN_QBLK], lse_tile)
            lse_tile = jnp.where(lane == 2 * hp + 1, lse[ATTN_QBLK:], lse_tile)
        lse_ref[rb, rows, :] = lse_tile


def _attn_branch(attn, rel_bias, dil):
    B, _, L, _ = attn.shape
    TQ = min(ATTN_STEP_ROWS, L)
    RB = ATTN_STEP_ROWS // TQ
    idx = jnp.asarray(_bucket_tables(dil))
    body = functools.partial(_attn_body, L=L, TQ=TQ)
    kv = lambda j: pl.BlockSpec((None, RB, L, ATTN_W), lambda b, r, i: (b, r, 0, j))
    q_spec = pl.BlockSpec((None, RB, TQ, ATTN_W), lambda b, r, i: (b, r, i, 0))
    return pl.pallas_call(
        body,
        grid=(B, dil // RB, L // TQ),
        in_specs=[pl.BlockSpec(memory_space=pltpu.SMEM),
                  pl.BlockSpec(idx.shape, lambda b, r, i: (0, 0, 0)),
                  q_spec, kv(1), kv(2)],
        out_specs=[q_spec, pl.BlockSpec((None, RB, TQ, LANES), lambda b, r, i: (b, r, i, 0))],
        out_shape=[jax.ShapeDtypeStruct((B, dil, L, ATTN_W), BF16),
                   jax.ShapeDtypeStruct((B, dil, L, LANES), F32)],
        scratch_shapes=[pltpu.VMEM((3, ATTN_HEADS, ATTN_QBLK, ATTN_KWIN), F32)],
        compiler_params=pltpu.CompilerParams(dimension_semantics=("arbitrary",) * 3,
                                             vmem_limit_bytes=48 * 2**20),
        name=f"attn_d{dil}",
    )(rel_bias, idx, attn, attn, attn)


def _gla_body(q_ref, k_ref, v_ref, gate_ref, bf_ref, bb_ref, ng_ref, out_ref, of_ref, ob_ref, st_ref, *, S):
    C = GLA_CHUNK
    nc = S // C
    r = lax.broadcasted_iota(I32, (C, LANES), 0)
    s2 = lax.broadcasted_iota(I32, (C, LANES), 1) % HEAD_DIM
    lo = lax.broadcasted_iota(I32, (C, LANES), 1) < HEAD_DIM
    blockdiag = ((lax.broadcasted_iota(I32, (2 * GLA_DV, LANES), 0) < GLA_DV)
                 == (lax.broadcasted_iota(I32, (2 * GLA_DV, LANES), 1) < HEAD_DIM))
    st_ref[...] = jnp.zeros_like(st_ref)

    def finish(rows, o):
        for hh in range(2):
            cols = slice(hh * GLA_DV, (hh + 1) * GLA_DV)
            oh = o[:, cols]
            oh = oh * lax.rsqrt(jnp.mean(oh * oh, axis=-1, keepdims=True) + NORM_EPS) * ng_ref[0, :, cols]
            gate = gate_ref[0, rows, cols].astype(F32)
            out_ref[0, rows, cols] = (oh * (gate * _sigmoid(gate))).astype(BF16)

    def one_chunk(ci, direction, last):
        mask = (r >= s2) if direction == 0 else (r <= s2)
        b_ref, mine_ref, other_ref = (bf_ref, of_ref, ob_ref) if direction == 0 else (bb_ref, ob_ref, of_ref)
        rows = pl.ds(pl.multiple_of(ci * C, C), C)
        q = q_ref[0, rows, :].astype(F32) * (HEAD_DIM ** -0.5)
        k = k_ref[0, rows, :].astype(F32)
        v = v_ref[0, rows, :]
        b = b_ref[0, rows, :]
        g_tot = b[C - 1:C, :] if direction == 0 else b[0:1, :]
        qd = (q * jnp.exp(b)).astype(BF16)
        kd = (k * jnp.exp(-b)).astype(BF16)
        kst = (k * jnp.exp(g_tot - b)).astype(BF16)
        zk = jnp.zeros_like(kd)
        ks = jnp.concatenate([jnp.where(lo, kd, zk), jnp.where(lo, zk, kd)], axis=0)
        a = lax.dot_general(qd, ks, NT_DIMS, preferred_element_type=F32)
        a = jnp.where(mask, a, 0.0).astype(BF16)
        zv = jnp.zeros((C, GLA_DV), BF16)
        vblk = jnp.concatenate([jnp.concatenate([v[:, :GLA_DV], zv], axis=1),
                                jnp.concatenate([zv, v[:, GLA_DV:]], axis=1)], axis=0)
        st = st_ref[direction]
        o = (jnp.dot(a, vblk, preferred_element_type=F32)
             + lax.dot_general(qd, st.astype(BF16), NT_DIMS, preferred_element_type=F32))
        if last:
            finish(rows, o + other_ref[rows, :])
        else:
            mine_ref[rows, :] = o
        d_st = jnp.dot(v.astype(F32).T.astype(BF16), kst, preferred_element_type=F32)
        st_ref[direction] = st * jnp.exp(g_tot) + jnp.where(blockdiag, d_st, 0.0)

    def step(last):
        def body(i, carry):
            one_chunk(i, 0, last)
            one_chunk(nc - 1 - i, 1, last)
            return carry
        return body

    lax.fori_loop(0, nc // 2, step(False), 0, unroll=32)
    lax.fori_loop(nc // 2, nc, step(True), 0, unroll=32)


def _gla(gla, bcum, norm_g, B, S):
    g3 = gla.reshape(B, S, GLA_COLS)
    b3 = bcum.reshape(B, S, 2 * GLA_KEY_W)
    ng = norm_g.reshape(2, 1, 2 * GLA_DV)
    body = functools.partial(_gla_body, S=S)
    spec = lambda w, off: pl.BlockSpec((1, S, w), lambda b, p: (b, 0, off + p))
    out = pl.pallas_call(
        body,
        grid=(B, 2),
        in_specs=[spec(LANES, 0),
                  spec(LANES, GLA_KEY_W // LANES),
                  spec(2 * GLA_DV, (2 * GLA_KEY_W) // (2 * GLA_DV)),
                  spec(2 * GLA_DV, (2 * GLA_KEY_W + GLA_VAL_W) // (2 * GLA_DV)),
                  spec(LANES, 0),
                  spec(LANES, GLA_KEY_W // LANES),
                  pl.BlockSpec((1, 1, 2 * GLA_DV), lambda b, p: (p, 0, 0))],
        out_specs=pl.BlockSpec((1, S, 2 * GLA_DV), lambda b, p: (b, 0, p)),
        out_shape=jax.ShapeDtypeStruct((B, S, GLA_VAL_W), BF16),
        scratch_shapes=[pltpu.VMEM((S, 2 * GLA_DV), F32), pltpu.VMEM((S, 2 * GLA_DV), F32),
                        pltpu.VMEM((2, 2 * GLA_DV, LANES), F32)],
        compiler_params=pltpu.CompilerParams(dimension_semantics=("arbitrary", "arbitrary"),
                                             vmem_limit_bytes=56 * 2**20),
        name="gla",
    )(g3, g3, g3, g3, b3, b3, ng)
    return out.reshape(B * S, GLA_VAL_W)


def _natural_rows(blk_ref, u_ref, dil, tm):
    if dil == 1:
        return blk_ref[0, 0].astype(F32)
    chunks = blk_ref.shape[-1] // LANES
    for r in range(dil):
        for c in range(chunks):
            u_ref[c, pl.ds(r, tm // dil, stride=dil), :] = blk_ref[0, r, :, c * LANES:(c + 1) * LANES].astype(F32)
    return jnp.concatenate([u_ref[c] for c in range(chunks)], axis=1)


def _per_head_lanes(w):
    head = lax.broadcasted_iota(I32, (LANES, ATTN_W), 0)
    col = lax.broadcasted_iota(I32, (LANES, ATTN_W), 1)
    spread = (col // HEAD_DIM == head).astype(F32).astype(BF16)
    hi, lo, _ = _split3(w)
    return (jnp.dot(hi, spread, preferred_element_type=F32) + jnp.dot(lo, spread, preferred_element_type=F32))


def _mix_body(o1, o2, o3, l1, l2, l3, g_ref, x_ref, wo_ref, n2_ref, rwh_ref, rwl_ref, rb_ref,
              x1_ref, h_ref, m_ref, slab_ref, tcnt_ref, u_ref, wob_ref, *, tm):
    @pl.when((pl.program_id(0) == 0) & (pl.program_id(1) == 0))
    def _():
        wob_ref[...] = wo_ref[...].astype(BF16)

    ls = [_natural_rows(l, u_ref, d, tm) for l, d in zip((l1, l2, l3), DILATIONS)]
    m = jnp.maximum(jnp.maximum(ls[0], ls[1]), ls[2])
    es = [jnp.exp(l - m) for l in ls]
    den = es[0] + es[1] + es[2]
    a = _per_head_lanes(es[0] / den) * _natural_rows(o1, u_ref, DILATIONS[0], tm)
    a = a + _per_head_lanes(es[1] / den) * _natural_rows(o2, u_ref, DILATIONS[1], tm)
    a = a + _per_head_lanes(es[2] / den) * _natural_rows(o3, u_ref, DILATIONS[2], tm)
    mix = (jnp.dot(a.astype(BF16), wob_ref[:ATTN_W, :], preferred_element_type=F32)
           + jnp.dot(g_ref[...], wob_ref[ATTN_W:, :], preferred_element_type=F32))
    x1 = x_ref[...] + mix
    x1_ref[...] = x1
    h = x1 * lax.rsqrt(jnp.mean(x1 * x1, axis=-1, keepdims=True) + NORM_EPS) * n2_ref[...]
    h_ref[...] = h.astype(BF16)

    hh, hl, _ = _split3(h)
    dg = lambda u, v: lax.dot_general(u, v, NT_DIMS, preferred_element_type=F32)
    logits = dg(rwh_ref[...], hh) + dg(rwl_ref[...], hh) + dg(rwh_ref[...], hl) + rb_ref[...]
    ie = lax.broadcasted_iota(I32, (N_EXPERTS, tm), 0)
    cur = logits
    vals, idxs = [], []
    for _ in range(TOP_K):
        mx = jnp.max(cur, axis=0, keepdims=True)
        ix = jnp.min(jnp.where(cur == mx, ie, N_EXPERTS), axis=0, keepdims=True)
        vals.append(mx)
        idxs.append(ix)
        cur = jnp.where(ie == ix, -jnp.inf, cur)
    ex = [jnp.exp(v - vals[0]) for v in vals]
    den = ex[0] + ex[1] + ex[2] + ex[3]
    gates = [e / den for e in ex]

    onehots = [ie == ix for ix in idxs]
    chosen = (onehots[0] | onehots[1] | onehots[2] | onehots[3]).astype(F32)
    tr = lax.broadcasted_iota(I32, (tm, tm), 0)
    tc = lax.broadcasted_iota(I32, (tm, tm), 1)
    before = jnp.dot(chosen.astype(BF16), (tr < tc).astype(BF16), preferred_element_type=F32)
    count = jnp.broadcast_to(jnp.sum(chosen, axis=1, keepdims=True), (N_EXPERTS, LANES))
    tcnt_ref[0] = count
    run = jnp.floor((count + (RUN_ALIGN - 1.0)) * (1.0 / RUN_ALIGN)) * RUN_ALIGN
    er = lax.broadcasted_iota(I32, (N_EXPERTS, N_EXPERTS), 0)
    ec = lax.broadcasted_iota(I32, (N_EXPERTS, N_EXPERTS), 1)
    run_start = jnp.dot((ec < er).astype(BF16), run.astype(BF16), preferred_element_type=F32)
    where_ = before + run_start[:, 0:1]
    locs = [jnp.sum(jnp.where(oh, where_, 0.0), axis=0, keepdims=True) for oh in onehots]

    def pack(rows, n):
        sub = lax.broadcasted_iota(I32, (n, tm), 0)
        out = jnp.zeros((n, tm), rows[0].dtype)
        for k, row in enumerate(rows):
            out = jnp.where(sub == k, row, out)
        return out

    m_ref[...] = pack([l.astype(I32) for l in locs], 8)
    slab = pack(gates + locs, 8)
    slab_ref[...] = jnp.concatenate([slab, jnp.zeros((LANES - 8, tm), F32)], axis=0).T


def _mix(outs, lses, g, x2, B, S, w_out, norm2_g, router_w, router_b):
    T, D = x2.shape
    tm = ROUTE_TILE
    nt = S // tm
    wo_spec = pl.BlockSpec(w_out.shape, lambda b, i: (0, 0), pipeline_mode=pl.Buffered(1))
    rwt = router_w.T
    rwh = rwt.astype(BF16)
    rwl = (rwt - rwh.astype(F32)).astype(BF16)
    rb = router_b.reshape(N_EXPERTS, 1)
    n2 = norm2_g.reshape(1, D)
    row = lambda c: pl.BlockSpec((tm, c), lambda b, i: (b * nt + i, 0))
    full = lambda a: pl.BlockSpec(a.shape, lambda b, i: (0,) * a.ndim)
    col = pl.BlockSpec((8, tm), lambda b, i: (0, b * nt + i))
    res = [pl.BlockSpec((1, d, tm // d, ATTN_W), lambda b, i: (b, 0, i, 0)) for d in DILATIONS]
    res_lse = [pl.BlockSpec((1, d, tm // d, LANES), lambda b, i: (b, 0, i, 0)) for d in DILATIONS]
    return pl.pallas_call(
        functools.partial(_mix_body, tm=tm),
        grid=(B, nt),
        in_specs=res + res_lse + [row(GLA_VAL_W), row(D), wo_spec, full(n2), full(rwh), full(rwl), full(rb)],
        out_specs=[row(D), row(D), col, row(LANES),
                   pl.BlockSpec((1, N_EXPERTS, LANES), lambda b, i: (b * nt + i, 0, 0))],
        out_shape=[jax.ShapeDtypeStruct((T, D), F32),
                   jax.ShapeDtypeStruct((T, D), BF16),
                   jax.ShapeDtypeStruct((8, T), I32),
                   jax.ShapeDtypeStruct((T, LANES), F32),
                   jax.ShapeDtypeStruct((T // tm, N_EXPERTS, LANES), F32)],
        scratch_shapes=[pltpu.VMEM((ATTN_W // LANES, tm, LANES), F32), pltpu.VMEM(w_out.shape, BF16)],
        compiler_params=pltpu.CompilerParams(dimension_semantics=("arbitrary", "arbitrary"),
                                             vmem_limit_bytes=48 * 2**20),
        name="mix_router",
    )(*outs, *lses, g, x2, w_out, n2, rwh, rwl, rb)


def _run_groups(c):
    return (c + (RUN_ALIGN - 1)) >> RUN_SHIFT


def _plan_body(tc_ref, rs_ref, p0_ref, pn_ref, be_ref, hv_ref, nu_ref, *, nt, nblk):
    shift = MOE_BLOCK.bit_length() - 1
    start = jnp.int32(0)
    for ex in range(N_EXPERTS):
        rows_of = lambda i, ex=ex: _run_groups(tc_ref[i, ex]) << RUN_SHIFT
        tot = lax.fori_loop(0, nt, lambda i, tot: tot + rows_of(i), jnp.int32(0), unroll=True)
        nb = (tot + (MOE_BLOCK - 1)) >> shift
        pad = (nb << shift) - tot
        p0_ref[ex] = start
        pn_ref[ex] = pad >> RUN_SHIFT

        def run(i, at, ex=ex):
            rs_ref[i, ex] = at
            return at + rows_of(i)

        lax.fori_loop(0, nt, run, start + pad, unroll=True)
        b0 = start >> shift

        def fill(j, carry, b0=b0, ex=ex):
            be_ref[b0 + j] = jnp.int32(ex)
            hv_ref[b0 + j] = jnp.int32(MOE_BLOCK // MOE_PART)
            return carry

        lax.fori_loop(0, nb, fill, 0)

        @pl.when(nb > 0)
        def _(b0=b0, pad=pad):
            hv_ref[b0] = (MOE_BLOCK - pad + (MOE_PART - 1)) >> (MOE_PART.bit_length() - 1)

        start = start + (nb << shift)
    used = start >> shift

    def tail(j, carry):
        be_ref[j] = jnp.int32(N_EXPERTS - 1)
        hv_ref[j] = jnp.int32(0)
        return carry

    lax.fori_loop(used, nblk, tail, 0)
    nu_ref[0] = used


def _plan(tcnt, nblk):
    nt = tcnt.shape[0]
    smem = pl.BlockSpec(memory_space=pltpu.SMEM)
    return pl.pallas_call(
        functools.partial(_plan_body, nt=nt, nblk=nblk),
        in_specs=[smem],
        out_specs=[smem] * 6,
        out_shape=[jax.ShapeDtypeStruct(tcnt.shape, I32),
                   jax.ShapeDtypeStruct((N_EXPERTS,), I32),
                   jax.ShapeDtypeStruct((N_EXPERTS,), I32),
                   jax.ShapeDtypeStruct((nblk,), I32),
                   jax.ShapeDtypeStruct((nblk,), I32),
                   jax.ShapeDtypeStruct((1,), I32)],
        name="plan",
    )(tcnt)


def _for_each_run(tc_ref, rs_ref, i, fn):
    big = 4 * RUN_ALIGN
    start = jnp.int32(0)
    for ex in range(N_EXPERTS):
        ng = _run_groups(tc_ref[i, ex])
        g0 = rs_ref[i, ex]

        def piece(q, carry, start=start, g0=g0):
            fn(pl.multiple_of(start + q * big, RUN_ALIGN), pl.multiple_of(g0 + q * big, RUN_ALIGN), big)
            return carry

        def single(q, carry, start=start, g0=g0):
            fn(pl.multiple_of(start + q * RUN_ALIGN, RUN_ALIGN), pl.multiple_of(g0 + q * RUN_ALIGN, RUN_ALIGN),
               RUN_ALIGN)
            return carry

        lax.fori_loop(0, ng >> 2, piece, 0)
        lax.fori_loop((ng >> 2) << 2, ng, single, 0)
        start = start + (ng << RUN_SHIFT)


def _group_count(tc_ref, i):
    n = jnp.int32(0)
    for ex in range(N_EXPERTS):
        n = n + _run_groups(tc_ref[i, ex])
    return n


def _wait_groups(n, src_ref, dst_ref, sem):
    def waits(count, rows):
        def body(q, carry):
            pltpu.make_async_copy(src_ref.at[pl.ds(0, rows)], dst_ref.at[pl.ds(0, rows)], sem).wait()
            return carry
        lax.fori_loop(0, count, body, 0)

    waits(n >> 6, 64 * RUN_ALIGN)
    waits((n >> 3) & 7, 8 * RUN_ALIGN)
    waits(n & 7, RUN_ALIGN)


def _dispatch_body(tc_ref, rs_ref, p0_ref, pn_ref, nu_ref, m_ref, h_ref, xin_ref, g_ref, z_ref, sem, zsem, *, tm, R):
    i = pl.program_id(0)
    slot = i % 2
    rows = m_ref[...].astype(jnp.int16)
    jj = lax.broadcasted_iota(I32, (R, tm), 0).astype(jnp.int16)
    hit = (jj == rows[0:1]) | (jj == rows[1:2]) | (jj == rows[2:3]) | (jj == rows[3:4])
    onehot = jnp.where(hit, jnp.ones((), BF16), jnp.zeros((), BF16))
    g_ref[slot] = jnp.dot(onehot, h_ref[...], preferred_element_type=F32).astype(BF16)

    group = lambda ref, r: ref.at[pl.ds(r, RUN_ALIGN)]

    @pl.when(i > 0)
    def _():
        _wait_groups(_group_count(tc_ref, i - 1), g_ref.at[1 - slot], xin_ref, sem.at[1 - slot])

    _for_each_run(tc_ref, rs_ref, i, lambda lr, gr, n: pltpu.make_async_copy(
        g_ref.at[slot, pl.ds(lr, n)], xin_ref.at[pl.ds(gr, n)], sem.at[slot]).start())

    block = lambda j: xin_ref.at[pl.ds(pl.multiple_of(j * MOE_BLOCK, MOE_BLOCK), MOE_BLOCK)]
    n_blocks = xin_ref.shape[0] // MOE_BLOCK

    @pl.when(i == 0)
    def _():
        z_ref[...] = jnp.zeros_like(z_ref)
        for ex in range(N_EXPERTS):
            def body(q, carry, ex=ex):
                pltpu.make_async_copy(group(z_ref, 0), group(xin_ref, pl.multiple_of(p0_ref[ex] + q * RUN_ALIGN, RUN_ALIGN)), zsem).start()
                return carry
            lax.fori_loop(0, pn_ref[ex], body, 0)

        def fill(j, carry):
            pltpu.make_async_copy(z_ref, block(j), zsem).start()
            return carry

        lax.fori_loop(nu_ref[0], n_blocks, fill, 0)

    @pl.when(i == pl.num_programs(0) - 1)
    def _():
        _wait_groups(_group_count(tc_ref, i), g_ref.at[slot], xin_ref, sem.at[slot])
        total = jnp.int32(0)
        for ex in range(N_EXPERTS):
            total = total + pn_ref[ex]
        _wait_groups(total, z_ref, xin_ref, zsem)

        def fill_wait(j, carry):
            pltpu.make_async_copy(z_ref, block(j), zsem).wait()
            return carry

        lax.fori_loop(nu_ref[0], n_blocks, fill_wait, 0)


def _dispatch(tcnt, rowstart, pad0, padn, n_used, m_t, h, n_rows):
    T, D = h.shape
    tm = ROUTE_TILE
    col = pl.BlockSpec((8, tm), lambda i, *_: (0, i))
    return pl.pallas_call(
        functools.partial(_dispatch_body, tm=tm, R=GROUP_ROWS),
        grid_spec=pltpu.PrefetchScalarGridSpec(
            num_scalar_prefetch=5,
            grid=(T // tm,),
            in_specs=[col, pl.BlockSpec((tm, D), lambda i, *_: (i, 0))],
            out_specs=pl.BlockSpec(memory_space=pl.ANY),
            scratch_shapes=[pltpu.VMEM((2, GROUP_ROWS, D), BF16), pltpu.VMEM((MOE_BLOCK, D), BF16),
                            pltpu.SemaphoreType.DMA((2,)), pltpu.SemaphoreType.DMA(())]),
        out_shape=jax.ShapeDtypeStruct((n_rows, D), BF16),
        compiler_params=pltpu.CompilerParams(dimension_semantics=("arbitrary",), vmem_limit_bytes=56 * 2**20),
        name="dispatch",
    )(tcnt, rowstart, pad0, padn, n_used, m_t, h)


def _moe_body(be_ref, nu_ref, hv_ref, x_ref, w1g_ref, w1l_ref, b1g_ref, b1l_ref, w2_ref, b2_ref, y_ref):
    del be_ref
    j = pl.program_id(0)
    live = j < nu_ref[0]

    def expert_mlp(rows):
        x = x_ref[rows, :]
        hg = lax.dot_general(x, w1g_ref[0], NT_DIMS, preferred_element_type=F32) + b1g_ref[0]
        hl = lax.dot_general(x, w1l_ref[0], NT_DIMS, preferred_element_type=F32) + b1l_ref[0]
        xg = jnp.minimum(hg, SWIGLU_LIMIT)
        xl = jnp.clip(hl, -SWIGLU_LIMIT, SWIGLU_LIMIT)
        act = xg * _sigmoid(SWIGLU_ALPHA * xg) * (xl + 1.0)
        y = jnp.dot(act.astype(BF16), w2_ref[0].astype(BF16), preferred_element_type=F32) + b2_ref[0]
        y_ref[rows, :] = y.astype(BF16)

    for parts in range(1, MOE_BLOCK // MOE_PART + 1):
        @pl.when(live & (hv_ref[j] == parts))
        def _(parts=parts):
            skip = MOE_BLOCK - parts * MOE_PART
            expert_mlp(slice(skip, MOE_BLOCK))
            if skip:
                y_ref[:skip, :] = jnp.zeros((skip, y_ref.shape[1]), y_ref.dtype)


def _moe(block_e, n_used, half_block, xin, w1g, w1l, b1, w2, b2):
    n_rows = xin.shape[0]
    D = w2.shape[2]
    F = w2.shape[1]
    nblk = n_rows // MOE_BLOCK
    b1s = jnp.transpose(b1.reshape(N_EXPERTS, F, 2), (2, 0, 1)).reshape(2, N_EXPERTS, 1, F)
    b1spec = lambda k: pl.BlockSpec((None, 1, 1, F), lambda j, be, nu, hv: (k, be[j], 0, 0))
    b2r = b2.reshape(N_EXPERTS, 1, D)
    xspec = pl.BlockSpec((MOE_BLOCK, D), lambda j, be, nu, hv: (jnp.minimum(j, nu[0] - 1), 0))
    wspec = lambda a: pl.BlockSpec((1,) + a.shape[1:], lambda j, be, nu, hv: (be[j], 0, 0))
    return pl.pallas_call(
        _moe_body,
        grid_spec=pltpu.PrefetchScalarGridSpec(
            num_scalar_prefetch=3,
            grid=(nblk,),
            in_specs=[xspec, wspec(w1g), wspec(w1l), b1spec(0), b1spec(1), wspec(w2), wspec(b2r)],
            out_specs=xspec),
        out_shape=jax.ShapeDtypeStruct(xin.shape, BF16),
        input_output_aliases={3: 0},
        compiler_params=pltpu.CompilerParams(dimension_semantics=("arbitrary",),
                                             vmem_limit_bytes=48 * 2**20),
        name="moe",
    )(block_e, n_used, half_block, xin, w1g, w1l, b1s, b1s, w2, b2r)


def _combine_body(tc_ref, rs_ref, slab_ref, y_ref, x1_ref, fg_ref, out_ref, g_ref, sem, *, tm, R):
    i = pl.program_id(0)
    slot = i % 2

    def fetch(tile, slot):
        _for_each_run(tc_ref, rs_ref, tile, lambda lr, gr, n: pltpu.make_async_copy(
            y_ref.at[pl.ds(gr, n)], g_ref.at[slot, pl.ds(lr, n)], sem.at[slot]).start())

    @pl.when(i == 0)
    def _():
        g_ref[...] = jnp.zeros_like(g_ref)
        fetch(i, slot)

    @pl.when(i + 1 < pl.num_programs(0))
    def _():
        fetch(i + 1, 1 - slot)

    slab = slab_ref[...]
    gates = slab[:, 0:TOP_K].astype(BF16)
    rows = slab[:, TOP_K:2 * TOP_K].astype(I32).astype(jnp.int16)
    jj = lax.broadcasted_iota(I32, (tm, R), 1).astype(jnp.int16)
    sel = jnp.zeros((tm, R), BF16)
    for k in range(TOP_K):
        sel = jnp.where(jj == rows[:, k:k + 1], gates[:, k:k + 1], sel)

    _wait_groups(_group_count(tc_ref, i), y_ref, g_ref.at[slot], sem.at[slot])
    acc = x1_ref[...] + jnp.dot(sel, g_ref[slot], preferred_element_type=F32)
    out_ref[...] = acc * lax.rsqrt(jnp.mean(acc * acc, axis=-1, keepdims=True) + NORM_EPS) * fg_ref[...]


def _combine(tcnt, rowstart, slab, y, x1, final_g):
    T, D = x1.shape
    tm = ROUTE_TILE
    fg = final_g.reshape(1, D)
    return pl.pallas_call(
        functools.partial(_combine_body, tm=tm, R=GROUP_ROWS),
        grid_spec=pltpu.PrefetchScalarGridSpec(
            num_scalar_prefetch=2,
            grid=(T // tm,),
            in_specs=[pl.BlockSpec((tm, LANES), lambda i, *_: (i, 0)),
                      pl.BlockSpec(memory_space=pl.ANY),
                      pl.BlockSpec((tm, D), lambda i, *_: (i, 0)),
                      pl.BlockSpec((1, D), lambda i, *_: (0, 0))],
            out_specs=pl.BlockSpec((tm, D), lambda i, *_: (i, 0)),
            scratch_shapes=[pltpu.VMEM((2, GROUP_ROWS, D), BF16), pltpu.SemaphoreType.DMA((2,))]),
        out_shape=jax.ShapeDtypeStruct((T, D), F32),
        compiler_params=pltpu.CompilerParams(dimension_semantics=("arbitrary",), vmem_limit_bytes=56 * 2**20),
        name="combine",
    )(tcnt, rowstart, slab, y, x1, fg)


def _mixers(x2, B, S, norm1_g, w_in, rel_bias, wg_f, bg_f, wg_b, bg_b, gla_norm_g, w1):
    a1, a4, a16, gla, bcum, w1g, w1l = _in_proj(x2, B, S, norm1_g, w_in, wg_f, bg_f, wg_b, bg_b, w1)
    branches = [_attn_branch(a, rel_bias, d) for a, d in zip((a1, a4, a16), DILATIONS)]
    g = _gla(gla, bcum, gla_norm_g, B, S)
    return [o for o, _ in branches], [l for _, l in branches], g, w1g, w1l


def _moe_layer(h, m_t, slab, tcnt_f, x1, w1g, w1l, b1, w2, b2, final_g):
    T = x1.shape[0]
    nt = T // ROUTE_TILE
    nblk = (T * TOP_K + nt * N_EXPERTS * (RUN_ALIGN - 1)) // MOE_BLOCK + N_EXPERTS
    tcnt = tcnt_f[:, :, 0].astype(I32)
    rowstart, pad0, padn, block_e, half_block, n_used = _plan(tcnt, nblk)
    xin = _dispatch(tcnt, rowstart, pad0, padn, n_used, m_t, h, nblk * MOE_BLOCK)
    y = _moe(block_e, n_used, half_block, xin, w1g, w1l, b1, w2, b2)
    return _combine(tcnt, rowstart, slab, y, x1, final_g)


def kernel(x, norm1_g, w_in, rel_bias, gla_wg_fwd, gla_bg_fwd, gla_wg_bwd, gla_bg_bwd, gla_norm_g, w_out, norm2_g, router_w, router_b, moe_w1, moe_b1, moe_w2, moe_b2, final_g):
    B, S, D = x.shape
    assert w_in.shape[0] == 1, "one layer"
    x2 = x.reshape(B * S, D)
    outs, lses, g, w1g, w1l = _mixers(x2, B, S, norm1_g[0], jnp.swapaxes(w_in, 1, 2), rel_bias, gla_wg_fwd[0], gla_bg_fwd[0],
                                      gla_wg_bwd[0], gla_bg_bwd[0], gla_norm_g[0], moe_w1[0])
    x1, h, m_t, slab, tcnt = _mix(outs, lses, g, x2, B, S, w_out[0], norm2_g[0], router_w[0], router_b[0])
    out = _moe_layer(h, m_t, slab, tcnt, x1, w1g, w1l, moe_b1[0], moe_w2[0], moe_b2[0], final_g)
    return out.reshape(B, S, D)
```
